```python
import jax, jax.numpy as jnp
from jax import lax
import numpy as np

D_MODEL = 1024
BATCH = 8
SEQ = 4096
DEPTH = 4

N_META = 16
GRID_W = 64
GLA_HEADS = 4
GLA_DK = 64
GLA_DV = 128
GLA_KEY = GLA_HEADS * GLA_DK
GLA_VAL = GLA_HEADS * GLA_DV
GLA_RANK = 16
GLA_TAU = 16.0
CHUNK = 64
ATT_HEADS = 8
ATT_KV_HEADS = 2
HEAD_DIM = 64
ATT_Q = ATT_HEADS * HEAD_DIM
ATT_KV = ATT_KV_HEADS * HEAD_DIM
Q_BLOCK = 128
ROPE_THETA = 10000.0
D_FF = 2816
EPS = 1e-6

IN_SIZES = (GLA_KEY, GLA_KEY, GLA_VAL, GLA_VAL, GLA_RANK, GLA_RANK,
            ATT_Q, ATT_KV, ATT_KV, D_MODEL, D_MODEL)
D_IN = sum(IN_SIZES)
SPLIT_POINTS = tuple(int(s) for s in np.cumsum(IN_SIZES)[:-1])

kernel_name = "hybrid_gla_gqa_macaron_encoder"


def rmsnorm(x, g):
    xf = x.astype(jnp.float32)
    y = xf * lax.rsqrt(jnp.mean(xf * xf, axis=-1, keepdims=True) + EPS)
    return (y * g.astype(jnp.float32)).astype(x.dtype)


def swiglu(x, w_gate, w_up, w_down):
    return (jax.nn.silu(x @ w_gate) * (x @ w_up)) @ w_down


def gla_causal_chunked(q, k, v, log_a):
    f32 = jnp.float32
    B, H, T, dk = q.shape
    dv = v.shape[-1]
    n = T // CHUNK
    qc = q.astype(f32).reshape(B, H, n, CHUNK, dk)
    kc = k.astype(f32).reshape(B, H, n, CHUNK, dk)
    vc = v.astype(f32).reshape(B, H, n, CHUNK, dv)
    bcum = jnp.cumsum(log_a.astype(f32).reshape(B, H, n, CHUNK, dk), axis=3)
    btot = bcum[:, :, :, -1:, :]
    q_dec = qc * jnp.exp(bcum)
    k_inv = kc * jnp.exp(-bcum)
    k_end = kc * jnp.exp(btot - bcum)
    mask = jnp.tril(jnp.ones((CHUNK, CHUNK), dtype=bool))
    att = jnp.where(mask, jnp.einsum('bhnid,bhnjd->bhnij', q_dec, k_inv), 0.0)
    o_intra = jnp.einsum('bhnij,bhnjv->bhniv', att, vc)
    kv_chunk = jnp.einsum('bhnjd,bhnjv->bhndv', k_end, vc)
    decay = jnp.exp(btot[:, :, :, 0, :])

    def step(S, inp):
        d, kv = inp
        return d[..., None] * S + kv, S

    S0 = jnp.zeros((B, H, dk, dv), f32)
    _, S_prev = lax.scan(step, S0, (jnp.moveaxis(decay, 2, 0), jnp.moveaxis(kv_chunk, 2, 0)))
    S_prev = jnp.moveaxis(S_prev, 0, 2)
    o_inter = jnp.einsum('bhnid,bhndv->bhniv', q_dec, S_prev)
    return (o_intra + o_inter).reshape(B, H, T, dv)


def gla_branch(q, k, v, r, lr_f, lr_b, w2, b2, gn_gain):
    B, L, _ = q.shape
    pad = CHUNK - N_META

    def heads(t, d):
        return t.reshape(B, L, GLA_HEADS, d).transpose(0, 2, 1, 3)

    def log_gate(lr, w, b):
        return jax.nn.log_sigmoid((lr @ w + b).astype(jnp.float32)) / GLA_TAU

    def padseq(t):
        return jnp.pad(t, ((0, 0), (0, 0), (pad, 0), (0, 0)))

    def flip(t):
        return jnp.flip(t, axis=2)

    qh = padseq(heads(q * GLA_DK ** -0.5, GLA_DK))
    kh = padseq(heads(k, GLA_DK))
    vh = padseq(heads(v, GLA_DV))
    gf = padseq(heads(log_gate(lr_f, w2[0], b2[0]), GLA_DK))
    gb = padseq(heads(log_gate(lr_b, w2[1], b2[1]), GLA_DK))
    o_f = gla_causal_chunked(qh, kh, vh, gf)
    o_b = flip(gla_causal_chunked(flip(qh), flip(kh), flip(vh), flip(gb)))
    o = (o_f + o_b)[:, :, pad:, :].transpose(0, 2, 1, 3)
    o = o * lax.rsqrt(jnp.mean(o * o, axis=-1, keepdims=True) + EPS)
    o = o.reshape(B, L, GLA_VAL) * gn_gain.astype(jnp.float32)
    return o.astype(r.dtype) * jax.nn.silu(r)


def axial_rope_tables(n_tok):
    f32 = jnp.float32
    rows = n_tok // GRID_W
    row = jnp.repeat(jnp.arange(rows), GRID_W).astype(f32)
    col = jnp.tile(jnp.arange(GRID_W), rows).astype(f32)
    axis_dim = HEAD_DIM // 2
    inv = ROPE_THETA ** (-jnp.arange(0, axis_dim, 2, dtype=f32) / axis_dim)
    ang = jnp.concatenate([row[:, None] * inv, col[:, None] * inv], axis=-1)
    ang = jnp.concatenate([jnp.zeros((N_META, axis_dim), f32), ang], axis=0)
    return jnp.cos(ang), jnp.sin(ang)


def apply_axial_rope(x, cos, sin):
    L = x.shape[-2]
    q4 = HEAD_DIM // 4
    xs = x.astype(jnp.float32).reshape(*x.shape[:-1], 2, 2, q4)
    x1, x2 = xs[..., 0, :], xs[..., 1, :]
    c = cos.reshape(L, 2, q4)
    s = sin.reshape(L, 2, q4)
    return jnp.stack([x1 * c - x2 * s, x2 * c + x1 * s], axis=-2).reshape(x.shape)


def attention_branch(q, k, v, gq, gk, cos, sin):
    B, L, _ = q.shape
    G = ATT_HEADS // ATT_KV_HEADS
    qh = q.reshape(B, L, ATT_KV_HEADS, G, HEAD_DIM).transpose(0, 2, 3, 1, 4)
    kh = k.reshape(B, L, ATT_KV_HEADS, HEAD_DIM).transpose(0, 2, 1, 3)
    vh = v.reshape(B, L, ATT_KV_HEADS, HEAD_DIM).transpose(0, 2, 1, 3)
    qh = apply_axial_rope(rmsnorm(qh, gq), cos, sin) * HEAD_DIM ** -0.5
    kh = apply_axial_rope(rmsnorm(kh, gk), cos, sin)
    pad = Q_BLOCK - N_META
    qp = jnp.pad(qh, ((0, 0), (0, 0), (0, 0), (pad, 0), (0, 0)))
    nblk = qp.shape[3] // Q_BLOCK
    qb = jnp.moveaxis(qp.reshape(B, ATT_KV_HEADS, G, nblk, Q_BLOCK, HEAD_DIM), 3, 0)

    def block(qblk):
        s = jnp.einsum('bkgqd,bksd->bkgqs', qblk, kh)
        p = jax.nn.softmax(s, axis=-1)
        return jnp.einsum('bkgqs,bksd->bkgqd', p.astype(vh.dtype), vh)

    o = lax.map(block, qb)
    o = jnp.moveaxis(o, 0, 3).reshape(B, ATT_KV_HEADS, G, nblk * Q_BLOCK, HEAD_DIM)[:, :, :, pad:]
    return o.transpose(0, 3, 1, 2, 4).reshape(B, L, ATT_Q).astype(q.dtype)


def mixer(z, w_in, gla_w2, gla_b2, gla_gn, q_norm, k_norm, w_pa, w_pb, b_merge, w_out, cos, sin):
    h = z @ w_in
    (q_a, k_a, v_a, r_a, lr_f, lr_b, q_b, k_b, v_b, g_a, g_b) = jnp.split(h, SPLIT_POINTS, axis=-1)
    a = gla_branch(q_a, k_a, v_a, r_a, lr_f, lr_b, gla_w2, gla_b2, gla_gn)
    b = attention_branch(q_b, k_b, v_b, q_norm, k_norm, cos, sin)
    y = (jax.nn.sigmoid(g_a + b_merge[0]) * (a @ w_pa)
         + jax.nn.sigmoid(g_b + b_merge[1]) * (b @ w_pb))
    return y @ w_out


def _fwd_setup_inputs(seed: int = 0) -> dict:
    key = jax.random.key(seed)
    ks = jax.random.split(key, 18)

    def nrm(k, shape, scale):
        return jax.random.normal(k, shape, jnp.float32) * scale

    return {
        "x": nrm(ks[0], (BATCH, SEQ, D_MODEL), 1.0),
        "meta_tokens": nrm(ks[1], (N_META, D_MODEL), 1.0),
        "norm_gains": 1.0 + nrm(ks[2], (DEPTH, 3, D_MODEL), 0.02),
        "ffn_w_gate": nrm(ks[3], (DEPTH, 2, D_MODEL, D_FF), D_MODEL ** -0.5),
        "ffn_w_up": nrm(ks[4], (DEPTH, 2, D_MODEL, D_FF), D_MODEL ** -0.5),
        "ffn_w_down": nrm(ks[5], (DEPTH, 2, D_FF, D_MODEL), D_FF ** -0.5),
        "w_in": nrm(ks[6], (DEPTH, D_MODEL, D_IN), D_MODEL ** -0.5),
        "gla_w2": nrm(ks[7], (DEPTH, 2, GLA_RANK, GLA_KEY), GLA_RANK ** -0.5),
        "gla_b2": nrm(ks[8], (DEPTH, 2, GLA_KEY), 0.1),
        "gla_gn": 1.0 + nrm(ks[9], (DEPTH, GLA_VAL), 0.02),
        "q_norm": 1.0 + nrm(ks[10], (DEPTH, HEAD_DIM), 0.02),
        "k_norm": 1.0 + nrm(ks[11], (DEPTH, HEAD_DIM), 0.02),
        "w_pa": nrm(ks[12], (DEPTH, GLA_VAL, D_MODEL), GLA_VAL ** -0.5),
        "w_pb": nrm(ks[13], (DEPTH, ATT_Q, D_MODEL), ATT_Q ** -0.5),
        "b_merge": nrm(ks[14], (DEPTH, 2, D_MODEL), 0.02),
        "w_out": nrm(ks[15], (DEPTH, D_MODEL, D_MODEL), D_MODEL ** -0.5),
        "final_norm": 1.0 + nrm(ks[16], (D_MODEL,), 0.02),
    }


def _fwd_reference(x, meta_tokens, norm_gains, ffn_w_gate, ffn_w_up, ffn_w_down, w_in, gla_w2, gla_b2,
              gla_gn, q_norm, k_norm, w_pa, w_pb, b_merge, w_out, final_norm):
    B, N, D = x.shape
    meta = jnp.broadcast_to(meta_tokens[None].astype(x.dtype), (B, N_META, D))
    h = jnp.concatenate([meta, x], axis=1)
    cos, sin = axial_rope_tables(N)
    for l in range(DEPTH):
        h = h + 0.5 * swiglu(rmsnorm(h, norm_gains[l, 0]),
                             ffn_w_gate[l, 0], ffn_w_up[l, 0], ffn_w_down[l, 0])
        h = h + mixer(rmsnorm(h, norm_gains[l, 1]), w_in[l], gla_w2[l], gla_b2[l], gla_gn[l],
                      q_norm[l], k_norm[l], w_pa[l], w_pb[l], b_merge[l], w_out[l], cos, sin)
        h = h + 0.5 * swiglu(rmsnorm(h, norm_gains[l, 2]),
                             ffn_w_gate[l, 1], ffn_w_up[l, 1], ffn_w_down[l, 1])
    return rmsnorm(h, final_norm)[:, N_META:]


import jax as _jax
import jax.numpy as _jnp

TWIN_FORMAT = 'train_step'
FWD_PARAMS = ['x', 'meta_tokens', 'norm_gains', 'ffn_w_gate', 'ffn_w_up', 'ffn_w_down', 'w_in', 'gla_w2', 'gla_b2', 'gla_gn', 'q_norm', 'k_norm', 'w_pa', 'w_pb', 'b_merge', 'w_out', 'final_norm']
TWIN_WEIGHTS = ['meta_tokens', 'norm_gains', 'ffn_w_gate', 'ffn_w_up', 'ffn_w_down', 'w_in', 'gla_w2', 'gla_b2', 'gla_gn', 'q_norm', 'k_norm', 'w_pa', 'w_pb', 'b_merge', 'w_out', 'final_norm']
TWIN_DIFF_INPUT = 'x'
TWIN_INPUTS = ['x', 'meta_tokens', 'norm_gains', 'ffn_w_gate', 'ffn_w_up', 'ffn_w_down', 'w_in', 'gla_w2', 'gla_b2', 'gla_gn', 'q_norm', 'k_norm', 'w_pa', 'w_pb', 'b_merge', 'w_out', 'final_norm', 'loss_target', 'm_meta_tokens', 'm_norm_gains', 'm_ffn_w_gate', 'm_ffn_w_up', 'm_ffn_w_down', 'm_w_in', 'm_gla_w2', 'm_gla_b2', 'm_gla_gn', 'm_q_norm', 'm_k_norm', 'm_w_pa', 'm_w_pb', 'm_b_merge', 'm_w_out', 'm_final_norm', 'v_meta_tokens', 'v_norm_gains', 'v_ffn_w_gate', 'v_ffn_w_up', 'v_ffn_w_down', 'v_w_in', 'v_gla_w2', 'v_gla_b2', 'v_gla_gn', 'v_q_norm', 'v_k_norm', 'v_w_pa', 'v_w_pb', 'v_b_merge', 'v_w_out', 'v_final_norm']
TWIN_OUTPUTS = ['loss', 'grad_x', 'grad_meta_tokens', 'grad_norm_gains', 'grad_ffn_w_gate', 'grad_ffn_w_up', 'grad_ffn_w_down', 'grad_w_in', 'grad_gla_w2', 'grad_gla_b2', 'grad_gla_gn', 'grad_q_norm', 'grad_k_norm', 'grad_w_pa', 'grad_w_pb', 'grad_b_merge', 'grad_w_out', 'grad_final_norm', 'delta_meta_tokens', 'delta_norm_gains', 'delta_ffn_w_gate', 'delta_ffn_w_up', 'delta_ffn_w_down', 'delta_w_in', 'delta_gla_w2', 'delta_gla_b2', 'delta_gla_gn', 'delta_q_norm', 'delta_k_norm', 'delta_w_pa', 'delta_w_pb', 'delta_b_merge', 'delta_w_out', 'delta_final_norm', 'new_m_meta_tokens', 'new_m_norm_gains', 'new_m_ffn_w_gate', 'new_m_ffn_w_up', 'new_m_ffn_w_down', 'new_m_w_in', 'new_m_gla_w2', 'new_m_gla_b2', 'new_m_gla_gn', 'new_m_q_norm', 'new_m_k_norm', 'new_m_w_pa', 'new_m_w_pb', 'new_m_b_merge', 'new_m_w_out', 'new_m_final_norm', 'new_v_meta_tokens', 'new_v_norm_gains', 'new_v_ffn_w_gate', 'new_v_ffn_w_up', 'new_v_ffn_w_down', 'new_v_w_in', 'new_v_gla_w2', 'new_v_gla_b2', 'new_v_gla_gn', 'new_v_q_norm', 'new_v_k_norm', 'new_v_w_pa', 'new_v_w_pb', 'new_v_b_merge', 'new_v_w_out', 'new_v_final_norm']
TWIN_LEAF_KINDS = {'loss': 'loss', 'grad_x': 'grad_x', 'grad_meta_tokens': 'grad_w', 'grad_norm_gains': 'grad_w', 'grad_ffn_w_gate': 'grad_w', 'grad_ffn_w_up': 'grad_w', 'grad_ffn_w_down': 'grad_w', 'grad_w_in': 'grad_w', 'grad_gla_w2': 'grad_w', 'grad_gla_b2': 'grad_w', 'grad_gla_gn': 'grad_w', 'grad_q_norm': 'grad_w', 'grad_k_norm': 'grad_w', 'grad_w_pa': 'grad_w', 'grad_w_pb': 'grad_w', 'grad_b_merge': 'grad_w', 'grad_w_out': 'grad_w', 'grad_final_norm': 'grad_w', 'delta_meta_tokens': 'delta_w', 'delta_norm_gains': 'delta_w', 'delta_ffn_w_gate': 'delta_w', 'delta_ffn_w_up': 'delta_w', 'delta_ffn_w_down': 'delta_w', 'delta_w_in': 'delta_w', 'delta_gla_w2': 'delta_w', 'delta_gla_b2': 'delta_w', 'delta_gla_gn': 'delta_w', 'delta_q_norm': 'delta_w', 'delta_k_norm': 'delta_w', 'delta_w_pa': 'delta_w', 'delta_w_pb': 'delta_w', 'delta_b_merge': 'delta_w', 'delta_w_out': 'delta_w', 'delta_final_norm': 'delta_w', 'new_m_meta_tokens': 'new_m', 'new_m_norm_gains': 'new_m', 'new_m_ffn_w_gate': 'new_m', 'new_m_ffn_w_up': 'new_m', 'new_m_ffn_w_down': 'new_m', 'new_m_w_in': 'new_m', 'new_m_gla_w2': 'new_m', 'new_m_gla_b2': 'new_m', 'new_m_gla_gn': 'new_m', 'new_m_q_norm': 'new_m', 'new_m_k_norm': 'new_m', 'new_m_w_pa': 'new_m', 'new_m_w_pb': 'new_m', 'new_m_b_merge': 'new_m', 'new_m_w_out': 'new_m', 'new_m_final_norm': 'new_m', 'new_v_meta_tokens': 'new_v', 'new_v_norm_gains': 'new_v', 'new_v_ffn_w_gate': 'new_v', 'new_v_ffn_w_up': 'new_v', 'new_v_ffn_w_down': 'new_v', 'new_v_w_in': 'new_v', 'new_v_gla_w2': 'new_v', 'new_v_gla_b2': 'new_v', 'new_v_gla_gn': 'new_v', 'new_v_q_norm': 'new_v', 'new_v_k_norm': 'new_v', 'new_v_w_pa': 'new_v', 'new_v_w_pb': 'new_v', 'new_v_b_merge': 'new_v', 'new_v_w_out': 'new_v', 'new_v_final_norm': 'new_v'}


def _forward(args):
    return _fwd_reference(*[args[k] for k in FWD_PARAMS])


def _output_shape():
    out = _jax.eval_shape(lambda: _forward(_fwd_setup_inputs(0)))
    return out.shape, out.dtype

N_MICROBATCH = 1
ADAM_LR = 0.001
ADAM_B1 = 0.9
ADAM_B2 = 0.999
ADAM_EPS = 1e-08
ADAM_WD = 0.01
ADAM_STEP = 10
PER_EXAMPLE_BATCH_AXIS = {'x': 0, 'loss_target': 0}
SHARED_INPUTS = []
_WEIGHT_DTYPES = {'meta_tokens': _jnp.float32, 'norm_gains': _jnp.float32, 'ffn_w_gate': _jnp.float32, 'ffn_w_up': _jnp.float32, 'ffn_w_down': _jnp.float32, 'w_in': _jnp.float32, 'gla_w2': _jnp.float32, 'gla_b2': _jnp.float32, 'gla_gn': _jnp.float32, 'q_norm': _jnp.float32, 'k_norm': _jnp.float32, 'w_pa': _jnp.float32, 'w_pb': _jnp.float32, 'b_merge': _jnp.float32, 'w_out': _jnp.float32, 'final_norm': _jnp.float32}
MOMENT_SCALE = {'meta_tokens': 6.831007e-03, 'norm_gains': 1.030936e-01, 'ffn_w_gate': 3.730665e-02, 'ffn_w_up': 3.616704e-02, 'ffn_w_down': 5.990832e-02, 'w_in': 6.244106e-02, 'gla_w2': 9.591057e-03, 'gla_b2': 4.221725e-02, 'gla_gn': 8.543819e-02, 'q_norm': 4.115351e-02, 'k_norm': 4.191661e-02, 'w_pa': 6.156038e-02, 'w_pb': 1.103309e-02, 'b_merge': 1.736564e-02, 'w_out': 6.257089e-02, 'final_norm': 3.200831e+01}


def _to_microbatches(a, axis):
    t = _jnp.moveaxis(a, axis, 0)
    t = t.reshape((N_MICROBATCH, t.shape[0] // N_MICROBATCH) + t.shape[1:])
    return _jnp.moveaxis(t, 1, axis + 1)


def setup_inputs(seed: int = 0) -> dict:
    inp = _fwd_setup_inputs(seed)
    key = _jax.random.fold_in(_jax.random.key(seed), 7919)
    shape, _ = _output_shape()
    out = dict(inp)
    out["loss_target"] = _jax.random.normal(_jax.random.fold_in(key, 0), shape, _jnp.float32)
    for i, name in enumerate(TWIN_WEIGHTS):
        w = inp[name].astype(_jnp.float32)
        if MOMENT_SCALE is None:
            s = _jnp.sqrt(_jnp.mean(_jnp.square(w)) + 1e-30)
        else:
            s = MOMENT_SCALE[name]
        km, kv = _jax.random.split(_jax.random.fold_in(key, i + 1))
        out[name] = w
        out["m_" + name] = s * _jax.random.normal(km, w.shape, _jnp.float32)
        out["v_" + name] = (s * s) * _jax.random.uniform(kv, w.shape, _jnp.float32, 0.5, 1.5)
    if N_MICROBATCH > 1:
        for name, axis in PER_EXAMPLE_BATCH_AXIS.items():
            out[name] = _to_microbatches(out[name], axis)
    return {'x': out['x'], 'meta_tokens': out['meta_tokens'], 'norm_gains': out['norm_gains'], 'ffn_w_gate': out['ffn_w_gate'], 'ffn_w_up': out['ffn_w_up'], 'ffn_w_down': out['ffn_w_down'], 'w_in': out['w_in'], 'gla_w2': out['gla_w2'], 'gla_b2': out['gla_b2'], 'gla_gn': out['gla_gn'], 'q_norm': out['q_norm'], 'k_norm': out['k_norm'], 'w_pa': out['w_pa'], 'w_pb': out['w_pb'], 'b_merge': out['b_merge'], 'w_out': out['w_out'], 'final_norm': out['final_norm'], 'loss_target': out['loss_target'], 'm_meta_tokens': out['m_meta_tokens'], 'm_norm_gains': out['m_norm_gains'], 'm_ffn_w_gate': out['m_ffn_w_gate'], 'm_ffn_w_up': out['m_ffn_w_up'], 'm_ffn_w_down': out['m_ffn_w_down'], 'm_w_in': out['m_w_in'], 'm_gla_w2': out['m_gla_w2'], 'm_gla_b2': out['m_gla_b2'], 'm_gla_gn': out['m_gla_gn'], 'm_q_norm': out['m_q_norm'], 'm_k_norm': out['m_k_norm'], 'm_w_pa': out['m_w_pa'], 'm_w_pb': out['m_w_pb'], 'm_b_merge': out['m_b_merge'], 'm_w_out': out['m_w_out'], 'm_final_norm': out['m_final_norm'], 'v_meta_tokens': out['v_meta_tokens'], 'v_norm_gains': out['v_norm_gains'], 'v_ffn_w_gate': out['v_ffn_w_gate'], 'v_ffn_w_up': out['v_ffn_w_up'], 'v_ffn_w_down': out['v_ffn_w_down'], 'v_w_in': out['v_w_in'], 'v_gla_w2': out['v_gla_w2'], 'v_gla_b2': out['v_gla_b2'], 'v_gla_gn': out['v_gla_gn'], 'v_q_norm': out['v_q_norm'], 'v_k_norm': out['v_k_norm'], 'v_w_pa': out['v_w_pa'], 'v_w_pb': out['v_w_pb'], 'v_b_merge': out['v_b_merge'], 'v_w_out': out['v_w_out'], 'v_final_norm': out['v_final_norm']}


def _loss(weights, diff, rest, loss_target):
    with _jax.named_scope("forward"):
        args = {**rest, TWIN_DIFF_INPUT: diff, **{k: w.astype(_WEIGHT_DTYPES[k]) for k, w in weights.items()}}
        y = _forward(args)
    with _jax.named_scope("loss_head"):
        err = _jnp.square(y.astype(_jnp.float32) - loss_target)
        return 0.5 * _jnp.sum(_jnp.mean(err, axis=-1)) if err.ndim else 0.5 * err


def _adamw(w, g, m, v):
    m = ADAM_B1 * m + (1.0 - ADAM_B1) * g
    v = ADAM_B2 * v + (1.0 - ADAM_B2) * _jnp.square(g)
    m_hat = m / (1.0 - ADAM_B1 ** ADAM_STEP)
    v_hat = v / (1.0 - ADAM_B2 ** ADAM_STEP)
    delta = -ADAM_LR * (m_hat / (_jnp.sqrt(v_hat) + ADAM_EPS) + ADAM_WD * w)
    return delta, m, v


def reference(x, meta_tokens, norm_gains, ffn_w_gate, ffn_w_up, ffn_w_down, w_in, gla_w2, gla_b2, gla_gn, q_norm, k_norm, w_pa, w_pb, b_merge, w_out, final_norm, loss_target, m_meta_tokens, m_norm_gains, m_ffn_w_gate, m_ffn_w_up, m_ffn_w_down, m_w_in, m_gla_w2, m_gla_b2, m_gla_gn, m_q_norm, m_k_norm, m_w_pa, m_w_pb, m_b_merge, m_w_out, m_final_norm, v_meta_tokens, v_norm_gains, v_ffn_w_gate, v_ffn_w_up, v_ffn_w_down, v_w_in, v_gla_w2, v_gla_b2, v_gla_gn, v_q_norm, v_k_norm, v_w_pa, v_w_pb, v_b_merge, v_w_out, v_final_norm):
    given = dict(x=x, meta_tokens=meta_tokens, norm_gains=norm_gains, ffn_w_gate=ffn_w_gate, ffn_w_up=ffn_w_up, ffn_w_down=ffn_w_down, w_in=w_in, gla_w2=gla_w2, gla_b2=gla_b2, gla_gn=gla_gn, q_norm=q_norm, k_norm=k_norm, w_pa=w_pa, w_pb=w_pb, b_merge=b_merge, w_out=w_out, final_norm=final_norm, loss_target=loss_target, m_meta_tokens=m_meta_tokens, m_norm_gains=m_norm_gains, m_ffn_w_gate=m_ffn_w_gate, m_ffn_w_up=m_ffn_w_up, m_ffn_w_down=m_ffn_w_down, m_w_in=m_w_in, m_gla_w2=m_gla_w2, m_gla_b2=m_gla_b2, m_gla_gn=m_gla_gn, m_q_norm=m_q_norm, m_k_norm=m_k_norm, m_w_pa=m_w_pa, m_w_pb=m_w_pb, m_b_merge=m_b_merge, m_w_out=m_w_out, m_final_norm=m_final_norm, v_meta_tokens=v_meta_tokens, v_norm_gains=v_norm_gains, v_ffn_w_gate=v_ffn_w_gate, v_ffn_w_up=v_ffn_w_up, v_ffn_w_down=v_ffn_w_down, v_w_in=v_w_in, v_gla_w2=v_gla_w2, v_gla_b2=v_gla_b2, v_gla_gn=v_gla_gn, v_q_norm=v_q_norm, v_k_norm=v_k_norm, v_w_pa=v_w_pa, v_w_pb=v_w_pb, v_b_merge=v_b_merge, v_w_out=v_w_out, v_final_norm=v_final_norm)
    weights = {n: given[n] for n in TWIN_WEIGHTS}
    shared = {n: given[n] for n in SHARED_INPUTS}
    per_example = {n: given[n] for n in ['x']}
    grad_fn = _jax.value_and_grad(_loss, argnums=(0, 1))

    def one_microbatch(ex, loss_target):
        ex = dict(ex)
        diff = ex.pop(TWIN_DIFF_INPUT)
        return grad_fn(weights, diff, {**shared, **ex}, loss_target)

    if N_MICROBATCH == 1:
        loss, (grad_w, grad_x) = one_microbatch(per_example, given["loss_target"])
    else:
        def body(carry, xs):
            loss_sum, grad_sum = carry
            l_k, (gw_k, gx_k) = one_microbatch(xs[0], xs[1])
            with _jax.named_scope("update"):
                return (loss_sum + l_k, _jax.tree.map(_jnp.add, grad_sum, gw_k)), gx_k

        init = (_jnp.zeros((), _jnp.float32), _jax.tree.map(_jnp.zeros_like, weights))
        (loss, grad_w), grad_x = _jax.lax.scan(body, init, (per_example, given["loss_target"]))
    with _jax.named_scope("update"):
        delta_w, new_m, new_v = {}, {}, {}
        for n in TWIN_WEIGHTS:
            delta_w[n], new_m[n], new_v[n] = _adamw(weights[n], grad_w[n], given["m_" + n], given["v_" + n])
    return (loss, grad_x, *[grad_w[n] for n in TWIN_WEIGHTS], *[delta_w[n] for n in TWIN_WEIGHTS],
            *[new_m[n] for n in TWIN_WEIGHTS], *[new_v[n] for n in TWIN_WEIGHTS])
```

```python
import functools

import numpy as np
import jax
import jax.numpy as jnp
from jax import lax
from jax.experimental import pallas as pl
from jax.experimental.pallas import tpu as pltpu

F32, BF16 = jnp.float32, jnp.bfloat16
SDS = jax.ShapeDtypeStruct
HIGHEST = lax.Precision.HIGHEST
MESH = pl.DeviceIdType.MESH

D = 1024
DEPTH = 4
N_META = 16
GRID_W = 64
GLA_H, GLA_DK, GLA_DV, GLA_RANK, GLA_TAU, CHUNK = 4, 64, 128, 16, 16.0, 64
ATT_H, ATT_KV, HD = 8, 2, 64
D_FF = 2816
EPS = 1e-6
ROPE_THETA = 10000.0
ADAM_LR, ADAM_B1, ADAM_B2, ADAM_EPS, ADAM_WD, ADAM_STEP = 0.001, 0.9, 0.999, 1e-08, 0.01, 10

NSH = 4
FF_S = D_FF // NSH
FF_P = 768
LANE = 128
PAD = LANE - N_META
D_IN = 4384
C_GA, C_GB, C_QA, C_KA, C_VA, C_RA, C_QB, C_KB, C_VB, C_LR, HP = 0, 1024, 2048, 2304, 2560, 3072, 3584, 4096, 4224, 4352, 4480
VMEM_BIG = 56 * 2 ** 20


def _rt(n, cap, mult=LANE):
    best = None
    t = mult
    while t <= min(n, cap):
        if n % t == 0:
            best = t
        t += mult
    assert best is not None, (n, cap, mult)
    return best


def _cp(big=False):
    return pltpu.CompilerParams(vmem_limit_bytes=VMEM_BIG) if big else None


def _row_ids(i, tm):
    return i * tm + lax.broadcasted_iota(jnp.int32, (tm, 1), 0)


def _mm(name, a, b, *, grid, a_spec, b_spec, o_spec, o_shape, o_dtype, dims, acc_shape, nk=1, scale=None,
        res=None, res_spec=None, a_sl=None, b_sl=None, pad_w=None, into=None):
    has_res, has_into = res is not None, into is not None

    def body(*refs):
        a_ref, b_ref = refs[0], refs[1]
        p = 2
        res_ref = None
        if has_res:
            res_ref = refs[p]
            p += 1
        if has_into:
            p += 1
        o_ref = refs[p]
        acc_ref = refs[p + 1] if nk > 1 else None
        av = (a_ref[a_sl] if a_sl is not None else a_ref[...]).astype(BF16)
        bv = (b_ref[b_sl] if b_sl is not None else b_ref[...]).astype(BF16)
        prod = lax.dot_general(av, bv, dims, preferred_element_type=F32)

        def finish(v):
            if scale is not None:
                v = v * scale
            if has_res:
                v = v + res_ref[...]
            v = v.astype(o_dtype)
            if pad_w is None:
                o_ref[...] = v
            else:
                w = v.shape[-1]
                o_ref[:, :w] = v
                o_ref[:, w:] = jnp.zeros((v.shape[0], pad_w - w), o_dtype)

        if nk == 1:
            finish(prod)
        else:
            k = pl.program_id(len(grid) - 1)

            @pl.when(k == 0)
            def _():
                acc_ref[...] = prod

            @pl.when(k > 0)
            def _():
                acc_ref[...] += prod

            @pl.when(k == nk - 1)
            def _():
                finish(acc_ref[...])

    in_specs = [a_spec, b_spec]
    args = [a, b]
    if has_res:
        in_specs.append(res_spec)
        args.append(res)
    aliases = {}
    if has_into:
        aliases = {len(args): 0}
        in_specs.append(pl.BlockSpec(memory_space=pl.ANY))
        args.append(into)
        o_shape = into.shape
    return pl.pallas_call(
        body, name=name, grid=grid, in_specs=in_specs, out_specs=o_spec, out_shape=SDS(tuple(o_shape), o_dtype),
        scratch_shapes=[pltpu.VMEM(acc_shape, F32)] if nk > 1 else [], input_output_aliases=aliases,
        compiler_params=_cp(True))(*args)


NN = (((1,), (0,)), ((), ()))
NT = (((1,), (1,)), ((), ()))
TN = (((0,), (0,)), ((), ()))


def _rms_fwd(name, h, g):
    Lp = h.shape[0]
    tm = _rt(Lp, 384)

    def body(h_ref, g_ref, n_ref, nt_ref):
        x = h_ref[...]
        r = lax.rsqrt(jnp.mean(x * x, axis=-1, keepdims=True) + EPS)
        y = x * r * g_ref[...]
        n_ref[...] = y.astype(BF16)
        nt_ref[...] = y.T.astype(BF16)

    return pl.pallas_call(
        body, name=name, grid=(Lp // tm,),
        in_specs=[pl.BlockSpec((tm, D), lambda i: (i, 0)), pl.BlockSpec((1, D), lambda i: (0, 0))],
        out_specs=[pl.BlockSpec((tm, D), lambda i: (i, 0)), pl.BlockSpec((D, tm), lambda i: (0, i))],
        out_shape=[SDS((Lp, D), BF16), SDS((D, Lp), BF16)])(h, g)


def _rms_bwd(name, dn, h, g, dh):
    Lp = h.shape[0]
    tm = _rt(Lp, 384)

    def body(dn_ref, h_ref, g_ref, dh_ref, o_ref, dg_ref):
        x = h_ref[...]
        r = lax.rsqrt(jnp.mean(x * x, axis=-1, keepdims=True) + EPS)
        xh = x * r
        dn = dn_ref[...]
        u = dn * g_ref[...]
        o_ref[...] = dh_ref[...] + r * (u - xh * jnp.mean(u * xh, axis=-1, keepdims=True))

        @pl.when(pl.program_id(0) == 0)
        def _():
            dg_ref[...] = jnp.zeros_like(dg_ref)

        dg_ref[...] += jnp.sum(dn * xh, axis=0, keepdims=True)

    row = pl.BlockSpec((tm, D), lambda i: (i, 0))
    vec = pl.BlockSpec((1, D), lambda i: (0, 0))
    return pl.pallas_call(
        body, name=name, grid=(Lp // tm,), in_specs=[row, row, vec, row], out_specs=[row, vec],
        out_shape=[SDS((Lp, D), F32), SDS((1, D), F32)])(dn, h, g, dh)


def _loss_head(name, h, g, tgt):
    Lp = h.shape[0]
    tm = LANE

    def body(h_ref, g_ref, t_ref, dh_ref, dg_ref, loss_ref):
        i = pl.program_id(0)
        x = h_ref[...]
        r = lax.rsqrt(jnp.mean(x * x, axis=-1, keepdims=True) + EPS)
        xh = x * r
        gg = g_ref[...]
        err = jnp.where(i >= 1, xh * gg - t_ref[...], 0.0)
        dy = err * (1.0 / D)
        u = dy * gg
        dh_ref[...] = r * (u - xh * jnp.mean(u * xh, axis=-1, keepdims=True))

        @pl.when(i == 0)
        def _():
            dg_ref[...] = jnp.zeros_like(dg_ref)
            loss_ref[...] = jnp.zeros_like(loss_ref)

        dg_ref[...] += jnp.sum(dy * xh, axis=0, keepdims=True)
        loss_ref[...] += (0.5 / D) * jnp.sum(err * err)

    row = pl.BlockSpec((tm, D), lambda i: (i, 0))
    vec = pl.BlockSpec((1, D), lambda i: (0, 0))
    return pl.pallas_call(
        body, name=name, grid=(Lp // tm,),
        in_specs=[row, vec, pl.BlockSpec((tm, D), lambda i: (jnp.maximum(i - 1, 0), 0))],
        out_specs=[row, vec, pl.BlockSpec((8, LANE), lambda i: (0, 0))],
        out_shape=[SDS((Lp, D), F32), SDS((1, D), F32), SDS((8, LANE), F32)])(h, g, tgt)


def _swiglu_fwd(name, a, b):
    Lp = a.shape[1]
    tm = _rt(Lp, 384)

    def body(a_ref, b_ref, s_ref, st_ref):
        av = a_ref[...]
        s = av * jax.nn.sigmoid(av) * b_ref[...]
        s_ref[...] = s.astype(BF16)
        st_ref[...] = s.T.astype(BF16)

    slab = pl.BlockSpec((None, tm, FF_P), lambda j, i: (j, i, 0))
    return pl.pallas_call(
        body, name=name, grid=(NSH, Lp // tm), in_specs=[slab, slab],
        out_specs=[slab, pl.BlockSpec((None, FF_P, tm), lambda j, i: (j, 0, i))],
        out_shape=[SDS((NSH, Lp, FF_P), BF16), SDS((NSH, FF_P, Lp), BF16)])(a, b)


def _swiglu_bwd(name, a, b, ds):
    Lp = a.shape[1]
    tm = _rt(Lp, 384)

    def body(a_ref, b_ref, ds_ref, da_ref, db_ref):
        av = a_ref[...]
        sg = jax.nn.sigmoid(av)
        dsv = ds_ref[...]
        da_ref[...] = (dsv * b_ref[...] * (sg * (1.0 + av * (1.0 - sg)))).astype(BF16)
        db_ref[...] = (dsv * (av * sg)).astype(BF16)

    slab = pl.BlockSpec((None, tm, FF_P), lambda j, i: (j, i, 0))
    return pl.pallas_call(
        body, name=name, grid=(NSH, Lp // tm), in_specs=[slab, slab, slab], out_specs=[slab, slab],
        out_shape=[SDS((NSH, Lp, FF_P), BF16), SDS((NSH, Lp, FF_P), BF16)])(a, b, ds)


def _merge_fwd(name, H, pa, pb, bm):
    Lp = H.shape[0]
    tm = _rt(Lp, 384)

    def body(g_ref, pa_ref, pb_ref, bm_ref, y_ref, yt_ref):
        gv = g_ref[...]
        y = (jax.nn.sigmoid(gv[:, :D] + bm_ref[0:1, :]) * pa_ref[...]
             + jax.nn.sigmoid(gv[:, D:] + bm_ref[1:2, :]) * pb_ref[...])
        y_ref[...] = y.astype(BF16)
        yt_ref[...] = y.T.astype(BF16)

    row = pl.BlockSpec((tm, D), lambda i: (i, 0))
    return pl.pallas_call(
        body, name=name, grid=(Lp // tm,),
        in_specs=[pl.BlockSpec((tm, 2 * D), lambda i: (i, 0)), row, row, pl.BlockSpec((2, D), lambda i: (0, 0))],
        out_specs=[row, pl.BlockSpec((D, tm), lambda i: (0, i))],
        out_shape=[SDS((Lp, D), BF16), SDS((D, Lp), BF16)])(H, pa, pb, bm)


def _merge_bwd(name, H, pa, pb, bm, dy):
    Lp = H.shape[0]
    tm = _rt(Lp, 384)

    def body(g_ref, pa_ref, pb_ref, bm_ref, dy_ref, dpa_ref, dpb_ref, dh_ref, dbm_ref):
        gv = g_ref[...]
        dyv = dy_ref[...]
        sa = jax.nn.sigmoid(gv[:, :D] + bm_ref[0:1, :])
        sb = jax.nn.sigmoid(gv[:, D:] + bm_ref[1:2, :])
        dpa_ref[...] = (dyv * sa).astype(BF16)
        dpb_ref[...] = (dyv * sb).astype(BF16)
        dga = dyv * pa_ref[...] * (sa * (1.0 - sa))
        dgb = dyv * pb_ref[...] * (sb * (1.0 - sb))
        dh_ref[:, :D] = dga.astype(BF16)
        dh_ref[:, D:] = dgb.astype(BF16)

        @pl.when(pl.program_id(0) == 0)
        def _():
            dbm_ref[...] = jnp.zeros_like(dbm_ref)

        dbm_ref[0:1, :] += jnp.sum(dga, axis=0, keepdims=True)
        dbm_ref[1:2, :] += jnp.sum(dgb, axis=0, keepdims=True)

    row = pl.BlockSpec((tm, D), lambda i: (i, 0))
    two = pl.BlockSpec((2, D), lambda i: (0, 0))
    gate = pl.BlockSpec((tm, 2 * D), lambda i: (i, 0))
    return pl.pallas_call(
        body, name=name, grid=(Lp // tm,), in_specs=[gate, row, row, two, row], out_specs=[row, row, gate, two],
        out_shape=[SDS((Lp, D), BF16), SDS((Lp, D), BF16), SDS((Lp, HP), BF16), SDS((2, D), F32)])(H, pa, pb, bm, dy)


def _gla_prep(name, H, w2p, b2p):
    Lp = H.shape[0]
    tm = _rt(Lp, 384)

    def body(qk_ref, v_ref, lr_ref, w_ref, b_ref, q_o, k_o, v_o, gf_o, gb_o):
        valid = _row_ids(pl.program_id(0), tm) >= PAD
        qk = qk_ref[...]
        vv = v_ref[...]
        pre = jnp.dot(lr_ref[...].astype(BF16), w_ref[...], preferred_element_type=F32) + b_ref[...]
        g = jnp.where(valid, jax.nn.log_sigmoid(pre) * (1.0 / GLA_TAU), 0.0)
        for hh in range(GLA_H):
            q_o[hh] = qk[:, 64 * hh:64 * hh + 64] * (GLA_DK ** -0.5)
            k_o[hh] = qk[:, 256 + 64 * hh:256 + 64 * hh + 64]
            v_o[hh] = vv[:, 128 * hh:128 * hh + 128].astype(BF16)
            gf_o[hh] = g[:, 64 * hh:64 * hh + 64]
            gb_o[hh] = g[:, 256 + 64 * hh:256 + 64 * hh + 64]

    h64 = pl.BlockSpec((GLA_H, tm, 64), lambda i: (0, i, 0))
    h128 = pl.BlockSpec((GLA_H, tm, 128), lambda i: (0, i, 0))
    return pl.pallas_call(
        body, name=name, grid=(Lp // tm,),
        in_specs=[pl.BlockSpec((tm, 512), lambda i: (i, C_QA // 512)), pl.BlockSpec((tm, 512), lambda i: (i, C_VA // 512)),
                  pl.BlockSpec((tm, LANE), lambda i: (i, C_LR // LANE)), pl.BlockSpec((LANE, 512), lambda i: (0, 0)),
                  pl.BlockSpec((1, 512), lambda i: (0, 0))],
        out_specs=[h64, h64, h128, h64, h64],
        out_shape=[SDS((GLA_H, Lp, 64), F32), SDS((GLA_H, Lp, 64), F32), SDS((GLA_H, Lp, 128), BF16),
                   SDS((GLA_H, Lp, 64), F32), SDS((GLA_H, Lp, 64), F32)])(H, H, H, w2p, b2p)


def _bdot(a, b, ca, cb, precision=None):
    return lax.dot_general(a, b, ((ca, cb), ((0,), (0,))), precision=precision, preferred_element_type=F32)


def _gla_chunk_terms(q_ref, k_ref, g_ref, v_ref, G, rev):
    B = GLA_H * G
    qv = q_ref[...].reshape(B, CHUNK, GLA_DK)
    kv = k_ref[...].reshape(B, CHUNK, GLA_DK)
    gv = g_ref[...].reshape(B, CHUNK, GLA_DK)
    vv = v_ref[...].reshape(B, CHUNK, GLA_DV)
    ii = lax.broadcasted_iota(jnp.int32, (CHUNK, CHUNK), 0)
    jj = lax.broadcasted_iota(jnp.int32, (CHUNK, CHUNK), 1)
    tri = (jj >= ii) if rev else (jj <= ii)
    tb = jnp.broadcast_to(tri.astype(F32)[None], (B, CHUNK, CHUNK))
    bc = _bdot(tb, gv, (2,), (1,), HIGHEST)
    bt = bc[:, 0:1, :] if rev else bc[:, CHUNK - 1:CHUNK, :]
    eq, eki, eke = jnp.exp(bc), jnp.exp(-bc), jnp.exp(bt - bc)
    qd, ki, ke = qv * eq, kv * eki, kv * eke
    att = jnp.where(tri[None], _bdot(qd.astype(BF16), ki.astype(BF16), (2,), (2,)), 0.0)
    dm = jnp.exp(_bdot(gv, jnp.ones((B, CHUNK, GLA_DV), F32), (1,), (1,), HIGHEST))
    return dict(B=B, vv=vv, tri=tri, tb=tb, bt=bt, eq=eq, eki=eki, eke=eke, qd=qd, ki=ki, ke=ke, att=att, dm=dm)


def _gla_fwd(name, q, k, g, v, rev, G):
    Lp = q.shape[1]
    tg = G * CHUNK
    ng = Lp // tg

    def body(q_ref, k_ref, g_ref, v_ref, o_ref, ss_ref, s_scr):
        @pl.when(pl.program_id(0) == 0)
        def _():
            s_scr[...] = jnp.zeros_like(s_scr)

        t = _gla_chunk_terms(q_ref, k_ref, g_ref, v_ref, G, rev)
        B, vv = t["B"], t["vv"]
        qd = t["qd"].astype(BF16)
        oi = _bdot(t["att"].astype(BF16), vv, (2,), (1,))
        kvc = _bdot(t["ke"].astype(BF16), vv, (1,), (1,)).reshape(GLA_H, G, GLA_DK, GLA_DV)
        dm = t["dm"].reshape(GLA_H, G, GLA_DK, GLA_DV)
        s = s_scr[...]
        sp = [None] * G
        for c in (range(G - 1, -1, -1) if rev else range(G)):
            sp[c] = s
            ss_ref[c] = s
            s = dm[:, c] * s + kvc[:, c]
        s_scr[...] = s
        spb = jnp.stack(sp, axis=1).reshape(B, GLA_DK, GLA_DV).astype(BF16)
        o_ref[...] = (oi + _bdot(qd, spb, (2,), (1,))).reshape(GLA_H, tg, GLA_DV)

    blk = (lambda i: (0, ng - 1 - i, 0)) if rev else (lambda i: (0, i, 0))
    sblk = (lambda i: (ng - 1 - i, 0, 0, 0)) if rev else (lambda i: (i, 0, 0, 0))
    h64 = pl.BlockSpec((GLA_H, tg, 64), blk)
    h128 = pl.BlockSpec((GLA_H, tg, 128), blk)
    return pl.pallas_call(
        body, name=name, grid=(ng,), in_specs=[h64, h64, h64, h128],
        out_specs=[h128, pl.BlockSpec((G, GLA_H, GLA_DK, GLA_DV), sblk)],
        out_shape=[SDS((GLA_H, Lp, GLA_DV), F32), SDS((Lp // CHUNK, GLA_H, GLA_DK, GLA_DV), F32)],
        scratch_shapes=[pltpu.VMEM((GLA_H, GLA_DK, GLA_DV), F32)], compiler_params=_cp(True))(q, k, g, v)


def _gla_bwd(name, q, k, g, v, ss, do, rev, G):
    Lp = q.shape[1]
    tg = G * CHUNK
    ng = Lp // tg

    def body(q_ref, k_ref, g_ref, v_ref, ss_ref, do_ref, dq_ref, dk_ref, dg_ref, dv_ref, ds_scr):
        @pl.when(pl.program_id(0) == 0)
        def _():
            ds_scr[...] = jnp.zeros_like(ds_scr)

        t = _gla_chunk_terms(q_ref, k_ref, g_ref, v_ref, G, rev)
        B, vv, tri = t["B"], t["vv"], t["tri"]
        qd, ki, ke = t["qd"], t["ki"], t["ke"]
        qdb, kib, keb = qd.astype(BF16), ki.astype(BF16), ke.astype(BF16)
        sp = jnp.stack([ss_ref[c] for c in range(G)], axis=1).reshape(B, GLA_DK, GLA_DV)
        dob = do_ref[...].reshape(B, CHUNK, GLA_DV).astype(BF16)
        da = jnp.where(tri[None], _bdot(dob, vv, (2,), (2,)), 0.0).astype(BF16)
        dqd = _bdot(da, kib, (2,), (1,)) + _bdot(dob, sp.astype(BF16), (2,), (2,))
        dki = _bdot(da, qdb, (1,), (1,))
        dv = _bdot(t["att"].astype(BF16), dob, (1,), (1,))
        cc = _bdot(qdb, dob, (1,), (1,)).reshape(GLA_H, G, GLA_DK, GLA_DV)
        dm = t["dm"].reshape(GLA_H, G, GLA_DK, GLA_DV)
        dsc = ds_scr[...]
        dsn = [None] * G
        for c in (range(G) if rev else range(G - 1, -1, -1)):
            dsn[c] = dsc
            dsc = dm[:, c] * dsc + cc[:, c]
        ds_scr[...] = dsc
        dsn = jnp.stack(dsn, axis=1).reshape(B, GLA_DK, GLA_DV)
        dsnb = dsn.astype(BF16)
        dv = dv + _bdot(keb, dsnb, (2,), (1,))
        dke = _bdot(vv, dsnb, (2,), (2,))
        ddrow = _bdot(jnp.ones((B, CHUNK, GLA_DV), F32), dsn * sp, (2,), (2,), HIGHEST)
        dbt = ddrow * jnp.exp(t["bt"]) + jnp.sum(dke * ke, axis=1, keepdims=True)
        db = dqd * qd - dki * ki - dke * ke
        dq_ref[...] = (dqd * t["eq"]).reshape(GLA_H, tg, GLA_DK)
        dk_ref[...] = (dki * t["eki"] + dke * t["eke"]).reshape(GLA_H, tg, GLA_DK)
        dg_ref[...] = (_bdot(t["tb"], db, (1,), (1,), HIGHEST) + dbt).reshape(GLA_H, tg, GLA_DK)
        dv_ref[...] = dv.reshape(GLA_H, tg, GLA_DV)

    blk = (lambda i: (0, i, 0)) if rev else (lambda i: (0, ng - 1 - i, 0))
    sblk = (lambda i: (i, 0, 0, 0)) if rev else (lambda i: (ng - 1 - i, 0, 0, 0))
    h64 = pl.BlockSpec((GLA_H, tg, 64), blk)
    h128 = pl.BlockSpec((GLA_H, tg, 128), blk)
    return pl.pallas_call(
        body, name=name, grid=(ng,),
        in_specs=[h64, h64, h64, h128, pl.BlockSpec((G, GLA_H, GLA_DK, GLA_DV), sblk), h128],
        out_specs=[h64, h64, h64, h128],
        out_shape=[SDS((GLA_H, Lp, 64), F32), SDS((GLA_H, Lp, 64), F32), SDS((GLA_H, Lp, 64), F32),
                   SDS((GLA_H, Lp, GLA_DV), F32)],
        scratch_shapes=[pltpu.VMEM((GLA_H, GLA_DK, GLA_DV), F32)], compiler_params=_cp(True))(q, k, g, v, ss, do)


def _gla_post(name, of, ob, H, gn):
    Lp = H.shape[0]
    tm = _rt(Lp, 384)

    def body(of_ref, ob_ref, r_ref, gn_ref, a_ref, at_ref):
        parts = []
        for hh in range(GLA_H):
            o = of_ref[hh] + ob_ref[hh]
            parts.append(o * lax.rsqrt(jnp.mean(o * o, axis=-1, keepdims=True) + EPS))
        rv = r_ref[...]
        a = (jnp.concatenate(parts, axis=1) * gn_ref[...]) * (rv * jax.nn.sigmoid(rv))
        a_ref[...] = a.astype(BF16)
        at_ref[...] = a.T.astype(BF16)

    h128 = pl.BlockSpec((GLA_H, tm, 128), lambda i: (0, i, 0))
    return pl.pallas_call(
        body, name=name, grid=(Lp // tm,),
        in_specs=[h128, h128, pl.BlockSpec((tm, 512), lambda i: (i, C_RA // 512)), pl.BlockSpec((1, 512), lambda i: (0, 0))],
        out_specs=[pl.BlockSpec((tm, 512), lambda i: (i, 0)), pl.BlockSpec((512, tm), lambda i: (0, i))],
        out_shape=[SDS((Lp, 512), BF16), SDS((512, Lp), BF16)])(of, ob, H, gn)


def _gla_post_bwd(name, of, ob, H, gn, da, dH):
    Lp = H.shape[0]
    tm = _rt(Lp, 384)

    def body(of_ref, ob_ref, r_ref, gn_ref, da_ref, dh_in, do_ref, dh_ref, dgn_ref):
        rv = r_ref[...]
        sg = jax.nn.sigmoid(rv)
        dav = da_ref[...]
        gnv = gn_ref[...]
        ons, rs = [], []
        for hh in range(GLA_H):
            o = of_ref[hh] + ob_ref[hh]
            r = lax.rsqrt(jnp.mean(o * o, axis=-1, keepdims=True) + EPS)
            rs.append(r)
            ons.append(o * r)
        on = jnp.concatenate(ons, axis=1)
        dw = dav * (rv * sg)
        dh_ref[...] = (dav * (on * gnv) * (sg * (1.0 + rv * (1.0 - sg)))).astype(BF16)

        @pl.when(pl.program_id(0) == 0)
        def _():
            dgn_ref[...] = jnp.zeros_like(dgn_ref)

        dgn_ref[...] += jnp.sum(dw * on, axis=0, keepdims=True)
        don = dw * gnv
        for hh in range(GLA_H):
            dd = don[:, 128 * hh:128 * hh + 128]
            do_ref[hh] = rs[hh] * (dd - ons[hh] * jnp.mean(dd * ons[hh], axis=-1, keepdims=True))

    h128 = pl.BlockSpec((GLA_H, tm, 128), lambda i: (0, i, 0))
    rblk = pl.BlockSpec((tm, 512), lambda i: (i, C_RA // 512))
    vec = pl.BlockSpec((1, 512), lambda i: (0, 0))
    return pl.pallas_call(
        body, name=name, grid=(Lp // tm,),
        in_specs=[h128, h128, rblk, vec, pl.BlockSpec((tm, 512), lambda i: (i, 0)), pl.BlockSpec(memory_space=pl.ANY)],
        out_specs=[h128, rblk, vec],
        out_shape=[SDS((GLA_H, Lp, 128), F32), SDS(dH.shape, BF16), SDS((1, 512), F32)],
        input_output_aliases={5: 1})(of, ob, H, gn, da, dH)


def _gla_qkv_bwd(name, dqf, dqb, dkf, dkb, dvf, dvb, dH):
    Lp = dqf.shape[1]
    tm = _rt(Lp, 384)

    def body(dqf_ref, dqb_ref, dkf_ref, dkb_ref, dvf_ref, dvb_ref, dh_in, dh_ref):
        valid = _row_ids(pl.program_id(0), tm) >= PAD
        for hh in range(GLA_H):
            dq = (dqf_ref[hh] + dqb_ref[hh]) * (GLA_DK ** -0.5)
            dh_ref[:, 64 * hh:64 * hh + 64] = jnp.where(valid, dq, 0.0).astype(BF16)
            dh_ref[:, 256 + 64 * hh:256 + 64 * hh + 64] = jnp.where(valid, dkf_ref[hh] + dkb_ref[hh], 0.0).astype(BF16)
            dh_ref[:, 512 + 128 * hh:512 + 128 * hh + 128] = jnp.where(valid, dvf_ref[hh] + dvb_ref[hh], 0.0).astype(BF16)

    h64 = pl.BlockSpec((GLA_H, tm, 64), lambda i: (0, i, 0))
    h128 = pl.BlockSpec((GLA_H, tm, 128), lambda i: (0, i, 0))
    return pl.pallas_call(
        body, name=name, grid=(Lp // tm,),
        in_specs=[h64, h64, h64, h64, h128, h128, pl.BlockSpec(memory_space=pl.ANY)],
        out_specs=pl.BlockSpec((tm, 1024), lambda i: (i, C_QA // 1024)), out_shape=SDS(dH.shape, BF16),
        input_output_aliases={6: 0})(dqf, dqb, dkf, dkb, dvf, dvb, dH)


def _gla_gate_bwd(name, H, w2p, b2p, dgf, dgb, dH):
    Lp = H.shape[0]
    tm = _rt(Lp, 384)

    def body(lr_ref, w_ref, b_ref, dgf_ref, dgb_ref, dh_in, dh_ref, dw_ref, db_ref, dg_scr):
        valid = _row_ids(pl.program_id(0), tm) >= PAD
        for hh in range(GLA_H):
            dg_scr[:, 64 * hh:64 * hh + 64] = dgf_ref[hh]
            dg_scr[:, 256 + 64 * hh:256 + 64 * hh + 64] = dgb_ref[hh]
        lrb = lr_ref[...].astype(BF16)
        wv = w_ref[...]
        pre = jnp.dot(lrb, wv, preferred_element_type=F32) + b_ref[...]
        dpre = jnp.where(valid, dg_scr[...] * (1.0 / GLA_TAU) * jax.nn.sigmoid(-pre), 0.0)
        dpb = dpre.astype(BF16)
        dh_ref[...] = lax.dot_general(dpb, wv, NT, preferred_element_type=F32).astype(BF16)

        @pl.when(pl.program_id(0) == 0)
        def _():
            dw_ref[...] = jnp.zeros_like(dw_ref)
            db_ref[...] = jnp.zeros_like(db_ref)

        dw_ref[...] += lax.dot_general(lrb, dpb, TN, preferred_element_type=F32)
        db_ref[...] += jnp.sum(dpre, axis=0, keepdims=True)

    h64 = pl.BlockSpec((GLA_H, tm, 64), lambda i: (0, i, 0))
    lrblk = pl.BlockSpec((tm, LANE), lambda i: (i, C_LR // LANE))
    wblk = pl.BlockSpec((LANE, 512), lambda i: (0, 0))
    vec = pl.BlockSpec((1, 512), lambda i: (0, 0))
    return pl.pallas_call(
        body, name=name, grid=(Lp // tm,),
        in_specs=[lrblk, wblk, vec, h64, h64, pl.BlockSpec(memory_space=pl.ANY)],
        out_specs=[lrblk, wblk, vec],
        out_shape=[SDS(dH.shape, BF16), SDS((LANE, 512), F32), SDS((1, 512), F32)],
        scratch_shapes=[pltpu.VMEM((tm, 512), F32)], input_output_aliases={5: 0})(H, w2p, b2p, dgf, dgb, dH)


def _swap16(x):
    n = x.shape[1]
    lane = lax.broadcasted_iota(jnp.int32, x.shape, 1)
    return jnp.where(lane % 32 < 16, pltpu.roll(x, n - 16, 1), pltpu.roll(x, 16, 1))


def _headnorm_rope(x, gain, cos, sin, bd):
    r = lax.rsqrt(jnp.dot(x * x, bd, precision=HIGHEST, preferred_element_type=F32) + EPS)
    xh = x * r
    xn = xh * gain
    return xn * cos + _swap16(xn) * sin, xh, r


def _headnorm_rope_bwd(dxr, xh, r, gain, cos, sin, bd):
    dxn = cos * dxr + _swap16(sin * dxr)
    u = dxn * gain
    dx = r * (u - xh * jnp.dot(u * xh, bd, precision=HIGHEST, preferred_element_type=F32))
    return dx, jnp.sum(dxn * xh, axis=0, keepdims=True)


def _att_prep(name, H, gq, gk, cos, sin, bd):
    Lp = H.shape[0]
    tm = _rt(Lp, 384)

    def body(q_ref, kv_ref, gq_ref, gk_ref, c_ref, s_ref, bd_ref, qp_ref, k_ref, v_ref):
        c1, s1 = c_ref[...], s_ref[...]
        c4, s4 = jnp.concatenate([c1] * 4, axis=1), jnp.concatenate([s1] * 4, axis=1)
        xr, _, _ = _headnorm_rope(q_ref[...], gq_ref[...], c4, s4, bd_ref[...])
        xr = xr * (HD ** -0.5)
        lane = lax.broadcasted_iota(jnp.int32, (tm, LANE), 1)
        for hh in range(ATT_H):
            grp = xr[:, LANE * (hh // 2):LANE * (hh // 2) + LANE]
            e, gi = hh % 2, hh // 4
            if e != gi:
                grp = pltpu.roll(grp, 64, 1)
            keep = (lane < 64) if gi == 0 else (lane >= 64)
            qp_ref[hh] = jnp.where(keep, grp, 0.0).astype(BF16)
        kv = kv_ref[...]
        kr, _, _ = _headnorm_rope(kv[:, :LANE], gk_ref[...], c1, s1, bd_ref[0:LANE, 0:LANE])
        k_ref[...] = kr.astype(BF16)
        v_ref[...] = kv[:, LANE:].astype(BF16)

    row128 = pl.BlockSpec((tm, LANE), lambda i: (i, 0))
    return pl.pallas_call(
        body, name=name, grid=(Lp // tm,),
        in_specs=[pl.BlockSpec((tm, 512), lambda i: (i, C_QB // 512)), pl.BlockSpec((tm, 256), lambda i: (i, C_KB // 256)),
                  pl.BlockSpec((1, 512), lambda i: (0, 0)), pl.BlockSpec((1, LANE), lambda i: (0, 0)), row128, row128,
                  pl.BlockSpec((512, 512), lambda i: (0, 0))],
        out_specs=[pl.BlockSpec((ATT_H, tm, LANE), lambda i: (0, i, 0)), row128, row128],
        out_shape=[SDS((ATT_H, Lp, LANE), BF16), SDS((Lp, LANE), BF16), SDS((Lp, LANE), BF16)])(H, H, gq, gk, cos, sin, bd)


def _att_q_bwd(name, H, gq, cos, sin, bd, dqp, dH):
    Lp = H.shape[0]
    tm = _rt(Lp, 384)

    def body(q_ref, gq_ref, c_ref, s_ref, bd_ref, dqp_ref, dh_in, dh_ref, dg_ref):
        c1, s1 = c_ref[...], s_ref[...]
        c4, s4 = jnp.concatenate([c1] * 4, axis=1), jnp.concatenate([s1] * 4, axis=1)
        gqv = gq_ref[...]
        _, xh, r = _headnorm_rope(q_ref[...], gqv, c4, s4, bd_ref[...])
        lane = lax.broadcasted_iota(jnp.int32, (tm, LANE), 1)
        groups = []
        for j in range(ATT_H // 2):
            pieces = []
            for e in range(2):
                hh = 2 * j + e
                piece = dqp_ref[hh]
                if e != hh // 4:
                    piece = pltpu.roll(piece, 64, 1)
                pieces.append(piece)
            groups.append(jnp.where(lane < 64, pieces[0], pieces[1]))
        dxr = jnp.concatenate(groups, axis=1) * (HD ** -0.5)
        dx, dg = _headnorm_rope_bwd(dxr, xh, r, gqv, c4, s4, bd_ref[...])
        dh_ref[...] = dx.astype(BF16)

        @pl.when(pl.program_id(0) == 0)
        def _():
            dg_ref[...] = jnp.zeros_like(dg_ref)

        dg_ref[...] += dg

    row128 = pl.BlockSpec((tm, LANE), lambda i: (i, 0))
    qblk = pl.BlockSpec((tm, 512), lambda i: (i, C_QB // 512))
    vec = pl.BlockSpec((1, 512), lambda i: (0, 0))
    return pl.pallas_call(
        body, name=name, grid=(Lp // tm,),
        in_specs=[qblk, vec, row128, row128, pl.BlockSpec((512, 512), lambda i: (0, 0)),
                  pl.BlockSpec((ATT_H, tm, LANE), lambda i: (0, i, 0)), pl.BlockSpec(memory_space=pl.ANY)],
        out_specs=[qblk, vec], out_shape=[SDS(dH.shape, BF16), SDS((1, 512), F32)],
        input_output_aliases={6: 0})(H, gq, cos, sin, bd, dqp, dH)


def _att_kv_bwd(name, H, gk, cos, sin, bd, dkr, dvb, dH):
    Lp = H.shape[0]
    tm = _rt(Lp, 384)

    def body(kv_ref, gk_ref, c_ref, s_ref, bd_ref, dk_ref, dv_ref, dh_in, dh_ref, dg_ref):
        c1, s1 = c_ref[...], s_ref[...]
        gkv = gk_ref[...]
        bdv = bd_ref[0:LANE, 0:LANE]
        _, xh, r = _headnorm_rope(kv_ref[:, :LANE], gkv, c1, s1, bdv)
        dx, dg = _headnorm_rope_bwd(dk_ref[...], xh, r, gkv, c1, s1, bdv)
        dh_ref[:, :LANE] = dx.astype(BF16)
        dh_ref[:, LANE:] = dv_ref[...].astype(BF16)

        @pl.when(pl.program_id(0) == 0)
        def _():
            dg_ref[...] = jnp.zeros_like(dg_ref)

        dg_ref[...] += dg

    row128 = pl.BlockSpec((tm, LANE), lambda i: (i, 0))
    kvblk = pl.BlockSpec((tm, 256), lambda i: (i, C_KB // 256))
    vec = pl.BlockSpec((1, LANE), lambda i: (0, 0))
    return pl.pallas_call(
        body, name=name, grid=(Lp // tm,),
        in_specs=[kvblk, vec, row128, row128, pl.BlockSpec((512, 512), lambda i: (0, 0)), row128, row128,
                  pl.BlockSpec(memory_space=pl.ANY)],
        out_specs=[kvblk, vec], out_shape=[SDS(dH.shape, BF16), SDS((1, LANE), F32)],
        input_output_aliases={7: 0})(H, gk, cos, sin, bd, dkr, dvb, dH)


def _att_fwd(name, qp, kr, vb, bias):
    Lp = kr.shape[0]
    tq = _rt(Lp, 384)

    def body(q_ref, k_ref, v_ref, bias_ref, b_ref, bt_ref, lse_ref):
        j, i = pl.program_id(0), pl.program_id(1)
        valid = _row_ids(i, tq) >= PAD
        kk, vv, bb = k_ref[...], v_ref[...], bias_ref[...]
        outs = []
        for e in range(2):
            s = lax.dot_general(q_ref[e], kk, NT, preferred_element_type=F32) + bb
            m = jnp.max(s, axis=-1, keepdims=True)
            p = jnp.exp(s - m)
            l = jnp.sum(p, axis=-1, keepdims=True)
            p = p * (1.0 / l)
            o = jnp.dot(p.astype(BF16), vv, preferred_element_type=F32)
            outs.append(jnp.where(valid, o, 0.0))
            lse_ref[e] = m + jnp.log(l)
        lane = lax.broadcasted_iota(jnp.int32, (tq, LANE), 1)
        low = j < 2
        o0 = jnp.where(low, outs[0], pltpu.roll(outs[0], 64, 1))
        o1 = jnp.where(low, pltpu.roll(outs[1], 64, 1), outs[1])
        blk = jnp.where(lane < 64, o0, o1)
        b_ref[...] = blk.astype(BF16)
        bt_ref[...] = blk.T.astype(BF16)

    full = pl.BlockSpec((Lp, LANE), lambda j, i: (0, 0))
    return pl.pallas_call(
        body, name=name, grid=(ATT_H // 2, Lp // tq),
        in_specs=[pl.BlockSpec((2, tq, LANE), lambda j, i: (j, i, 0)), full, full, pl.BlockSpec((1, Lp), lambda j, i: (0, 0))],
        out_specs=[pl.BlockSpec((tq, LANE), lambda j, i: (i, j)), pl.BlockSpec((LANE, tq), lambda j, i: (j, i)),
                   pl.BlockSpec((2, tq, 1), lambda j, i: (j, i, 0))],
        out_shape=[SDS((Lp, 512), BF16), SDS((512, Lp), BF16), SDS((ATT_H, Lp, 1), F32)],
        compiler_params=_cp(True))(qp, kr, vb, bias)


def _att_bwd(name, qp, kr, vb, bias, lse, db):
    Lp = kr.shape[0]
    tq = _rt(Lp, 384)
    nq = Lp // tq

    def body(q_ref, k_ref, v_ref, bias_ref, lse_ref, db_ref, dq_ref, dk_ref, dv_ref, dkt_scr, dvt_scr):
        j, i = pl.program_id(0), pl.program_id(1)

        @pl.when((j == 0) & (i == 0))
        def _():
            dkt_scr[...] = jnp.zeros_like(dkt_scr)
            dvt_scr[...] = jnp.zeros_like(dvt_scr)

        kk, vv, bb = k_ref[...], v_ref[...], bias_ref[...]
        dbv = db_ref[...]
        rolled = pltpu.roll(dbv, 64, 1)
        lane = lax.broadcasted_iota(jnp.int32, (tq, LANE), 1)
        low = j < 2
        first = jnp.where(low, 0, 64)
        keep = (lane >= first) & (lane < first + 64)
        for e in range(2):
            qe = q_ref[e]
            s = lax.dot_general(qe, kk, NT, preferred_element_type=F32) + bb
            p = jnp.exp(s - lse_ref[e])
            src = jnp.where(low, dbv, rolled) if e == 0 else jnp.where(low, rolled, dbv)
            dop = jnp.where(keep, src, 0.0).astype(BF16)
            dp = lax.dot_general(dop, vv, NT, preferred_element_type=F32)
            ds = (p * (dp - jnp.sum(p * dp, axis=-1, keepdims=True))).astype(BF16)
            dq_ref[e] = jnp.where(keep, jnp.dot(ds, kk, preferred_element_type=F32), 0.0)
            dkt_scr[...] += lax.dot_general(qe, ds, TN, preferred_element_type=F32)
            dvt_scr[...] += lax.dot_general(dop, p.astype(BF16), TN, preferred_element_type=F32)

        @pl.when((j == ATT_H // 2 - 1) & (i == nq - 1))
        def _():
            dk_ref[...] = dkt_scr[...].T
            dv_ref[...] = dvt_scr[...].T

    full = pl.BlockSpec((Lp, LANE), lambda j, i: (0, 0))
    pair = pl.BlockSpec((2, tq, LANE), lambda j, i: (j, i, 0))
    return pl.pallas_call(
        body, name=name, grid=(ATT_H // 2, nq),
        in_specs=[pair, full, full, pl.BlockSpec((1, Lp), lambda j, i: (0, 0)), pl.BlockSpec((2, tq, 1), lambda j, i: (j, i, 0)),
                  pl.BlockSpec((tq, LANE), lambda j, i: (i, j))],
        out_specs=[pair, full, full],
        out_shape=[SDS((ATT_H, Lp, LANE), F32), SDS((Lp, LANE), F32), SDS((Lp, LANE), F32)],
        scratch_shapes=[pltpu.VMEM((LANE, Lp), F32), pltpu.VMEM((LANE, Lp), F32)], compiler_params=_cp(True))(
            qp, kr, vb, bias, lse, db)


def _rope_tables(n_tok):
    rows = n_tok // GRID_W
    row = jnp.repeat(jnp.arange(rows), GRID_W).astype(F32)
    col = jnp.tile(jnp.arange(GRID_W), rows).astype(F32)
    axis_dim = HD // 2
    inv = ROPE_THETA ** (-jnp.arange(0, axis_dim, 2, dtype=F32) / axis_dim)
    ang = jnp.concatenate([row[:, None] * inv, col[:, None] * inv], axis=-1)
    ang = jnp.concatenate([jnp.zeros((LANE, axis_dim), F32), ang], axis=0)
    c, s = jnp.cos(ang), jnp.sin(ang)
    c64 = jnp.concatenate([c[:, :16], c[:, :16], c[:, 16:], c[:, 16:]], axis=1)
    s64 = jnp.concatenate([-s[:, :16], s[:, :16], -s[:, 16:], s[:, 16:]], axis=1)
    return jnp.concatenate([c64, c64], axis=1), jnp.concatenate([s64, s64], axis=1)


def _ffn_fwd(tag, h, W, l, j):
    Lp = h.shape[0]
    tm = _rt(Lp, 1408)
    g = W["norm_gains"][l, 2 * j].reshape(1, D)
    n, nt = _rms_fwd(f"rmsf_{tag}", h, g)
    wspec = pl.BlockSpec((None, None, None, D, FF_S), lambda s, i: (s, l, j, 0, 0))
    up = functools.partial(
        _mm, grid=(NSH, Lp // tm), a_spec=pl.BlockSpec((tm, D), lambda s, i: (i, 0)), b_spec=wspec,
        o_spec=pl.BlockSpec((None, tm, FF_P), lambda s, i: (s, i, 0)), o_shape=(NSH, Lp, FF_P), o_dtype=F32, dims=NN,
        acc_shape=None, pad_w=FF_P)
    a = up(f"ffa_{tag}", n, W["gate"])
    b = up(f"ffb_{tag}", n, W["up"])
    s, st = _swiglu_fwd(f"swf_{tag}", a, b)
    row = pl.BlockSpec((tm, D), lambda i, s: (i, 0))
    h2 = _mm(f"ffd_{tag}", s, W["down"], grid=(Lp // tm, NSH),
             a_spec=pl.BlockSpec((None, tm, FF_P), lambda i, s: (s, i, 0)), a_sl=(slice(None), slice(0, FF_S)),
             b_spec=pl.BlockSpec((None, None, None, FF_S, D), lambda i, s: (s, l, j, 0, 0)),
             o_spec=row, o_shape=(Lp, D), o_dtype=F32, dims=NN, acc_shape=(tm, D), nk=NSH, scale=0.5, res=h, res_spec=row)
    return h2, dict(h=h, g=g, nt=nt, a=a, b=b, st=st)


def _ffn_bwd(tag, dh, sv, W, G, l, j):
    Lp = dh.shape[0]
    tm = _rt(Lp, 1408)
    tn = 512
    ds = _mm(f"bds_{tag}", dh, W["down"], grid=(NSH, Lp // tm), a_spec=pl.BlockSpec((tm, D), lambda s, i: (i, 0)),
             b_spec=pl.BlockSpec((None, None, None, FF_S, D), lambda s, i: (s, l, j, 0, 0)),
             o_spec=pl.BlockSpec((None, tm, FF_P), lambda s, i: (s, i, 0)), o_shape=(NSH, Lp, FF_P), o_dtype=F32, dims=NT,
             acc_shape=None, scale=0.5, pad_w=FF_P)
    G["down"] = _mm(f"bwd_{tag}", sv["st"], dh, grid=(NSH, D // tn),
                    a_spec=pl.BlockSpec((None, FF_P, Lp), lambda s, n: (s, 0, 0)), a_sl=(slice(0, FF_S), slice(None)),
                    b_spec=pl.BlockSpec((Lp, tn), lambda s, n: (0, n)),
                    o_spec=pl.BlockSpec((None, None, None, FF_S, tn), lambda s, n: (s, l, j, 0, n)), o_shape=None, o_dtype=F32,
                    dims=NN, acc_shape=None, scale=0.5, into=G["down"])
    da, db = _swiglu_bwd(f"swb_{tag}", sv["a"], sv["b"], ds)
    for key, dact in (("gate", da), ("up", db)):
        G[key] = _mm(f"bw{key[0]}_{tag}", sv["nt"], dact, grid=(NSH, D // tn),
                     a_spec=pl.BlockSpec((tn, Lp), lambda s, m: (m, 0)),
                     b_spec=pl.BlockSpec((None, Lp, FF_P), lambda s, m: (s, 0, 0)), b_sl=(slice(None), slice(0, FF_S)),
                     o_spec=pl.BlockSpec((None, None, None, tn, FF_S), lambda s, m: (s, l, j, m, 0)), o_shape=None,
                     o_dtype=F32, dims=NN, acc_shape=None, into=G[key])
    row = pl.BlockSpec((tm, D), lambda i, s: (i, 0))
    dn = None
    for key, dact in (("gate", da), ("up", db)):
        dn = _mm(f"bn{key[0]}_{tag}", dact, W[key], grid=(Lp // tm, NSH),
                 a_spec=pl.BlockSpec((None, tm, FF_P), lambda i, s: (s, i, 0)), a_sl=(slice(None), slice(0, FF_S)),
                 b_spec=pl.BlockSpec((None, None, None, D, FF_S), lambda i, s: (s, l, j, 0, 0)),
                 o_spec=row, o_shape=(Lp, D), o_dtype=F32, dims=NT, acc_shape=(tm, D), nk=NSH,
                 res=dn, res_spec=row if dn is not None else None)
    return _rms_bwd(f"rmsb_{tag}", dn, sv["h"], sv["g"], dh)


def _mixer_fwd(tag, h, W, C, l, G_chunks):
    Lp = h.shape[0]
    tm = _rt(Lp, 1408)
    g = W["norm_gains"][l, 1].reshape(1, D)
    z, zt = _rms_fwd(f"rmsf_{tag}", h, g)
    tn = 896
    H = _mm(f"win_{tag}", z, W["winp"], grid=(HP // tn, Lp // tm), a_spec=pl.BlockSpec((tm, D), lambda n, i: (i, 0)),
            b_spec=pl.BlockSpec((None, D, tn), lambda n, i: (l, 0, n)), o_spec=pl.BlockSpec((tm, tn), lambda n, i: (i, n)),
            o_shape=(Lp, HP), o_dtype=F32, dims=NN, acc_shape=None)
    w2p, b2p = C["w2p"][l], C["b2p"][l]
    qh, kh, vh, gf, gb = _gla_prep(f"glap_{tag}", H, w2p, b2p)
    of, sf = _gla_fwd(f"glaf_{tag}", qh, kh, gf, vh, False, G_chunks)
    ob, sb = _gla_fwd(f"glar_{tag}", qh, kh, gb, vh, True, G_chunks)
    gn = W["gn"][l].reshape(1, 512)
    a, at = _gla_post(f"glao_{tag}", of, ob, H, gn)
    gq, gk = C["gq"][l], C["gk"][l]
    qp, kr, vb = _att_prep(f"attp_{tag}", H, gq, gk, C["cos"], C["sin"], C["bd"])
    b, bt, lse = _att_fwd(f"attf_{tag}", qp, kr, vb, C["bias"])
    proj = functools.partial(
        _mm, grid=(NSH, Lp // tm), a_spec=pl.BlockSpec((tm, 512), lambda s, i: (i, 0)),
        b_spec=pl.BlockSpec((None, None, 512, 256), lambda s, i: (s, l, 0, 0)), o_spec=pl.BlockSpec((tm, 256), lambda s, i: (i, s)),
        o_shape=(Lp, D), o_dtype=F32, dims=NN, acc_shape=None)
    pa = proj(f"pa_{tag}", a, W["wpa"])
    pb = proj(f"pb_{tag}", b, W["wpb"])
    bm = W["bm"][l]
    y, yt = _merge_fwd(f"mrg_{tag}", H, pa, pb, bm)
    row = pl.BlockSpec((tm, D), lambda i, s: (i, 0))
    h2 = _mm(f"wout_{tag}", y, W["wout"], grid=(Lp // tm, NSH), a_spec=pl.BlockSpec((tm, 256), lambda i, s: (i, s)),
             b_spec=pl.BlockSpec((None, None, 256, D), lambda i, s: (s, l, 0, 0)), o_spec=row, o_shape=(Lp, D), o_dtype=F32,
             dims=NN, acc_shape=(tm, D), nk=NSH, res=h, res_spec=row)
    sv = dict(h=h, g=g, zt=zt, H=H, w2p=w2p, b2p=b2p, qh=qh, kh=kh, vh=vh, gf=gf, gb=gb, of=of, ob=ob, sf=sf, sb=sb, gn=gn,
              at=at, gq=gq, gk=gk, qp=qp, kr=kr, vb=vb, bt=bt, lse=lse, pa=pa, pb=pb, bm=bm, yt=yt)
    return h2, sv


def _mixer_bwd(tag, dh, sv, W, C, G, S, l, G_chunks):
    Lp = dh.shape[0]
    tm = _rt(Lp, 1408)
    H = sv["H"]
    dy = _mm(f"bdy_{tag}", dh, W["wout"], grid=(NSH, Lp // tm), a_spec=pl.BlockSpec((tm, D), lambda s, i: (i, 0)),
             b_spec=pl.BlockSpec((None, None, 256, D), lambda s, i: (s, l, 0, 0)), o_spec=pl.BlockSpec((tm, 256), lambda s, i: (i, s)),
             o_shape=(Lp, D), o_dtype=F32, dims=NT, acc_shape=None)
    tn = 512
    G["wout"] = _mm(f"bwo_{tag}", sv["yt"], dh, grid=(NSH, D // tn), a_spec=pl.BlockSpec((256, Lp), lambda s, n: (s, 0)),
                    b_spec=pl.BlockSpec((Lp, tn), lambda s, n: (0, n)),
                    o_spec=pl.BlockSpec((None, None, 256, tn), lambda s, n: (s, l, 0, n)), o_shape=None, o_dtype=F32, dims=NN,
                    acc_shape=None, into=G["wout"])
    dpa, dpb, dH, S["bm"][l] = _merge_bwd(f"bmrg_{tag}", H, sv["pa"], sv["pb"], sv["bm"], dy)
    dbranch = {}
    for key, dp, xt in (("wpa", dpa, sv["at"]), ("wpb", dpb, sv["bt"])):
        dbranch[key] = _mm(f"bx{key[2]}_{tag}", dp, W[key], grid=(Lp // tm, NSH),
                           a_spec=pl.BlockSpec((tm, 256), lambda i, s: (i, s)),
                           b_spec=pl.BlockSpec((None, None, 512, 256), lambda i, s: (s, l, 0, 0)),
                           o_spec=pl.BlockSpec((tm, 512), lambda i, s: (i, 0)), o_shape=(Lp, 512), o_dtype=F32, dims=NT,
                           acc_shape=(tm, 512), nk=NSH)
        G[key] = _mm(f"bw{key[2]}_{tag}", xt, dp, grid=(NSH,), a_spec=pl.BlockSpec((512, Lp), lambda s: (0, 0)),
                     b_spec=pl.BlockSpec((Lp, 256), lambda s: (0, s)),
                     o_spec=pl.BlockSpec((None, None, 512, 256), lambda s: (s, l, 0, 0)), o_shape=None, o_dtype=F32, dims=NN,
                     acc_shape=None, into=G[key])
    dqp, dkr, dvb = _att_bwd(f"attb_{tag}", sv["qp"], sv["kr"], sv["vb"], C["bias"], sv["lse"], dbranch["wpb"])
    dH, S["gq"][l] = _att_q_bwd(f"attq_{tag}", H, sv["gq"], C["cos"], C["sin"], C["bd"], dqp, dH)
    dH, S["gk"][l] = _att_kv_bwd(f"attk_{tag}", H, sv["gk"], C["cos"], C["sin"], C["bd"], dkr, dvb, dH)
    do, dH, S["gn"][l] = _gla_post_bwd(f"bglo_{tag}", sv["of"], sv["ob"], H, sv["gn"], dbranch["wpa"], dH)
    dqf, dkf, dgf, dvf = _gla_bwd(f"bglf_{tag}", sv["qh"], sv["kh"], sv["gf"], sv["vh"], sv["sf"], do, False, G_chunks)
    dqb, dkb, dgb, dvr = _gla_bwd(f"bglr_{tag}", sv["qh"], sv["kh"], sv["gb"], sv["vh"], sv["sb"], do, True, G_chunks)
    dH = _gla_qkv_bwd(f"bglq_{tag}", dqf, dqb, dkf, dkb, dvf, dvr, dH)
    dH, S["w2p"][l], S["b2p"][l] = _gla_gate_bwd(f"bglg_{tag}", H, sv["w2p"], sv["b2p"], dgf, dgb, dH)
    tmm = 256
    G["winp"] = _mm(f"bwi_{tag}", sv["zt"], dH, grid=(D // tmm, HP // 896), a_spec=pl.BlockSpec((tmm, Lp), lambda m, n: (m, 0)),
                    b_spec=pl.BlockSpec((Lp, 896), lambda m, n: (0, n)), o_spec=pl.BlockSpec((None, tmm, 896), lambda m, n: (l, m, n)),
                    o_shape=None, o_dtype=F32, dims=NN, acc_shape=None, into=G["winp"])
    dz = _mm(f"bdz_{tag}", dH, W["winp"], grid=(D // 512, Lp // tm), a_spec=pl.BlockSpec((tm, HP), lambda n, i: (i, 0)),
             b_spec=pl.BlockSpec((None, 512, HP), lambda n, i: (l, n, 0)), o_spec=pl.BlockSpec((tm, 512), lambda n, i: (i, n)),
             o_shape=(Lp, D), o_dtype=F32, dims=NT, acc_shape=None)
    return _rms_bwd(f"rmsb_{tag}", dz, sv["h"], sv["g"], dh)


def _local_step(x2, tgt2, W):
    n_tok = x2.shape[0]
    Lp = n_tok + LANE
    nc = Lp // CHUNK
    g_chunks = max(g for g in (1, 2, 3, 6) if nc % g == 0)
    cos, sin = _rope_tables(n_tok)
    bd = jnp.asarray(np.kron(np.eye(ATT_H, dtype=np.float32), np.full((HD, HD), 1.0 / HD, np.float32)))
    bias = jnp.where(jnp.arange(Lp) >= PAD, 0.0, -1e30).astype(F32).reshape(1, Lp)
    w2, b2 = W["w2"], W["b2"]
    w2p = jnp.zeros((DEPTH, LANE, 512), F32)
    w2p = w2p.at[:, 0:GLA_RANK, 0:256].set(w2[:, 0]).at[:, GLA_RANK:2 * GLA_RANK, 256:512].set(w2[:, 1]).astype(BF16)
    C = dict(cos=cos, sin=sin, bd=bd, bias=bias, w2p=w2p, b2p=b2.reshape(DEPTH, 1, 512),
             gq=jnp.tile(W["qn"], (1, ATT_H)).reshape(DEPTH, 1, 512), gk=jnp.tile(W["kn"], (1, ATT_KV)).reshape(DEPTH, 1, LANE))
    h = jnp.concatenate([jnp.zeros((PAD, D), F32), W["meta"], x2], axis=0)
    saved = []
    for l in range(DEPTH):
        h, s0 = _ffn_fwd(f"l{l}a", h, W, l, 0)
        h, sm = _mixer_fwd(f"l{l}m", h, W, C, l, g_chunks)
        h, s1 = _ffn_fwd(f"l{l}b", h, W, l, 1)
        saved.append((s0, sm, s1))
    dh, dfin, loss = _loss_head("loss_head", h, W["fin"].reshape(1, D), tgt2)
    G = dict(gate=jnp.zeros((NSH, DEPTH, 2, D, FF_S), F32), up=jnp.zeros((NSH, DEPTH, 2, D, FF_S), F32),
             down=jnp.zeros((NSH, DEPTH, 2, FF_S, D), F32), winp=jnp.zeros((DEPTH, D, HP), F32),
             wpa=jnp.zeros((NSH, DEPTH, 512, 256), F32), wpb=jnp.zeros((NSH, DEPTH, 512, 256), F32),
             wout=jnp.zeros((NSH, DEPTH, 256, D), F32))
    S = dict(bm=[None] * DEPTH, gq=[None] * DEPTH, gk=[None] * DEPTH, gn=[None] * DEPTH, w2p=[None] * DEPTH,
             b2p=[None] * DEPTH, ng=[[None] * 3 for _ in range(DEPTH)])
    for l in reversed(range(DEPTH)):
        s0, sm, s1 = saved[l]
        dh, S["ng"][l][2] = _ffn_bwd(f"l{l}b", dh, s1, W, G, l, 1)
        dh, S["ng"][l][1] = _mixer_bwd(f"l{l}m", dh, sm, W, C, G, S, l, g_chunks)
        dh, S["ng"][l][0] = _ffn_bwd(f"l{l}a", dh, s0, W, G, l, 0)
    small = dict(
        meta=dh[PAD:LANE],
        norm_gains=jnp.stack([jnp.concatenate(S["ng"][l], axis=0) for l in range(DEPTH)]),
        w2=jnp.stack([jnp.stack([S["w2p"][l][0:GLA_RANK, 0:256], S["w2p"][l][GLA_RANK:2 * GLA_RANK, 256:512]]) for l in range(DEPTH)]),
        b2=jnp.stack([S["b2p"][l].reshape(2, 256) for l in range(DEPTH)]),
        gn=jnp.concatenate(S["gn"], axis=0),
        qn=jnp.stack([S["gq"][l].reshape(ATT_H, HD).sum(0) for l in range(DEPTH)]),
        kn=jnp.stack([S["gk"][l].reshape(ATT_KV, HD).sum(0) for l in range(DEPTH)]),
        bm=jnp.stack(S["bm"]),
        fin=dfin.reshape(D),
    )
    return loss[0, 0], dh, G, small


def _win_to_padded(w):
    pad = jnp.zeros(w.shape[:-1] + (HP - D_IN,), w.dtype)
    return jnp.concatenate([w[..., 2336:4384], w[..., 0:1536], w[..., 1568:2336], w[..., 1536:1568], pad], axis=-1)


def _win_from_padded(w):
    return jnp.concatenate([w[..., 2048:3584], w[..., 4352:4384], w[..., 3584:4352], w[..., 0:2048]], axis=-1)


def _assemble(g):
    W = dict(gate=g["gate"], up=g["up"], down=g["down"], wpa=g["wpa"], wpb=g["wpb"], wout=g["wout"])
    win = jnp.transpose(g["win"], (1, 2, 0, 3)).reshape(DEPTH, D, D_IN)
    W["winp"] = _win_to_padded(win)
    sm = g["small"]
    parts = _unpack(sm, SHARDED_SMALL)
    W["meta"] = jnp.transpose(parts["meta"], (1, 0, 2)).reshape(N_META, D)
    W["norm_gains"] = jnp.transpose(parts["norm_gains"], (1, 2, 0, 3)).reshape(DEPTH, 3, D)
    W["w2"] = jnp.transpose(parts["w2"], (1, 2, 3, 0, 4)).reshape(DEPTH, 2, GLA_RANK, 256)
    W["b2"] = jnp.transpose(parts["b2"], (1, 2, 0, 3)).reshape(DEPTH, 2, 256)
    W["bm"] = jnp.transpose(parts["bm"], (1, 2, 0, 3)).reshape(DEPTH, 2, D)
    return W


SHARDED_SMALL = dict(meta=(N_META, 256), norm_gains=(DEPTH, 3, 256), w2=(DEPTH, 2, GLA_RANK, 64), b2=(DEPTH, 2, 64),
                     bm=(DEPTH, 2, 256))
FULL_SMALL = dict(meta=(N_META, D), norm_gains=(DEPTH, 3, D), w2=(DEPTH, 2, GLA_RANK, 256), b2=(DEPTH, 2, 256),
                  gn=(DEPTH, 512), qn=(DEPTH, HD), kn=(DEPTH, HD), bm=(DEPTH, 2, D), fin=(D,))


def _pack(parts, table, rows):
    flat = jnp.concatenate([parts[k].reshape(-1).astype(F32) for k in table])
    return jnp.pad(flat, (0, rows * LANE - flat.shape[0])).reshape(rows, LANE)


def _unpack(packed, table):
    lead = packed.shape[:-2]
    flat = packed.reshape(lead + (-1,))
    out, off = {}, 0
    for k, shp in table.items():
        n = int(np.prod(shp))
        out[k] = flat[..., off:off + n].reshape(lead + tuple(shp))
        off += n
    return out


def _rows_for(table, mult):
    n = sum(int(np.prod(s)) for s in table.values())
    return -(-n // (LANE * mult)) * mult


SMALL_ROWS = _rows_for(SHARDED_SMALL, 16)
GRAD_ROWS = _rows_for(dict(FULL_SMALL, loss=(1,)), 8)


def _place():
    x, y, c = lax.axis_index("x"), lax.axis_index("y"), lax.axis_index("c")
    return x, y, c


def _other_chips(x, y):
    return [(1 - x, y), (x, 1 - y), (1 - x, 1 - y)]


def _gather_weights(arrs):
    n = len(arrs)

    def body(*refs):
        ins, outs = refs[:n], refs[n:2 * n]
        send, recv, fsend, frecv, lsem = refs[2 * n:]
        x, y, c = _place()
        me = 2 * x + y
        sib = (x, y, 1 - c)
        chips = _other_chips(x, y)
        copies = []
        for a in range(n):
            half = ins[a].shape[0] // 2
            loc = pltpu.make_async_copy(ins[a], outs[a].at[me], lsem.at[a])
            loc.start()
            copies.append(("local", loc))
            mine = pl.ds(c * half, half)
            for k, (px, py) in enumerate(chips):
                cp = pltpu.make_async_remote_copy(
                    src_ref=ins[a].at[mine], dst_ref=outs[a].at[me, mine], send_sem=send.at[a, k], recv_sem=recv.at[a, k],
                    device_id=(px, py, c), device_id_type=MESH)
                cp.start()
                copies.append(("send", cp))
        for a in range(n):
            half = ins[a].shape[0] // 2
            mine = pl.ds(c * half, half)
            for k, (px, py) in enumerate(chips):
                src = 2 * px + py
                landed = outs[a].at[src, mine]
                pltpu.make_async_remote_copy(
                    src_ref=landed, dst_ref=landed, send_sem=send.at[a, k], recv_sem=recv.at[a, k],
                    device_id=(px, py, c), device_id_type=MESH).wait_recv()
                fw = pltpu.make_async_remote_copy(
                    src_ref=landed, dst_ref=landed, send_sem=fsend.at[a, k], recv_sem=frecv.at[a, k],
                    device_id=sib, device_id_type=MESH)
                fw.start()
                copies.append(("send", fw))
        for a in range(n):
            half = ins[a].shape[0] // 2
            theirs = pl.ds((1 - c) * half, half)
            for k, (px, py) in enumerate(chips):
                src = 2 * px + py
                landed = outs[a].at[src, theirs]
                pltpu.make_async_remote_copy(
                    src_ref=landed, dst_ref=landed, send_sem=fsend.at[a, k], recv_sem=frecv.at[a, k],
                    device_id=sib, device_id_type=MESH).wait_recv()
        for kind, cp in copies:
            if kind == "local":
                cp.wait()
            else:
                cp.wait_send()

    any_spec = pl.BlockSpec(memory_space=pl.ANY)
    return pl.pallas_call(
        body, name="gather_weights", in_specs=[any_spec] * n, out_specs=[any_spec] * n,
        out_shape=[SDS((NSH,) + a.shape, a.dtype) for a in arrs],
        scratch_shapes=[pltpu.SemaphoreType.DMA((n, 3)), pltpu.SemaphoreType.DMA((n, 3)), pltpu.SemaphoreType.DMA((n, 3)),
                        pltpu.SemaphoreType.DMA((n, 3)), pltpu.SemaphoreType.DMA((n,))])(*arrs)


def _pair_exchange(arrs):
    n = len(arrs)

    def body(*refs):
        ins, outs = refs[:n], refs[n:2 * n]
        send, recv = refs[2 * n:]
        x, y, c = _place()
        cps = []
        for a in range(n):
            cp = pltpu.make_async_remote_copy(
                src_ref=ins[a].at[:, pl.ds(2 * (1 - c), 2)], dst_ref=outs[a], send_sem=send.at[a], recv_sem=recv.at[a],
                device_id=(x, y, 1 - c), device_id_type=MESH)
            cp.start()
            cps.append(cp)
        for cp in cps:
            cp.wait()

    any_spec = pl.BlockSpec(memory_space=pl.ANY)
    return pl.pallas_call(
        body, name="pair_exchange", in_specs=[any_spec] * n, out_specs=[any_spec] * n,
        out_shape=[SDS((NSH, 2) + a.shape[2:], a.dtype) for a in arrs],
        scratch_shapes=[pltpu.SemaphoreType.DMA((n,)), pltpu.SemaphoreType.DMA((n,))])(*arrs)


def _pair_add(name, g, p, cidx):
    _, _, R, Cc = g.shape
    tr = _rt(R, 512, 8)

    def body(c_ref, g_ref, p_ref, o_ref, ob_ref):
        v = g_ref[...] + p_ref[...]
        o_ref[...] = v
        ob_ref[...] = v.astype(BF16)

    blk = pl.BlockSpec((None, None, tr, Cc), lambda s, l, r, c_ref: (s, l, r, 0))
    return pl.pallas_call(
        body, name=name,
        grid_spec=pltpu.PrefetchScalarGridSpec(
            num_scalar_prefetch=1, grid=(NSH, 2, R // tr),
            in_specs=[pl.BlockSpec((None, None, tr, Cc), lambda s, l, r, c_ref: (s, 2 * c_ref[0] + l, r, 0)), blk],
            out_specs=[blk, blk]),
        out_shape=[SDS((NSH, 2, R, Cc), F32), SDS((NSH, 2, R, Cc), BF16)])(cidx, g, p)


def _chip_exchange(arrs):
    n = len(arrs)

    def body(*refs):
        ins, outs = refs[:n], refs[n:2 * n]
        send, recv = refs[2 * n:]
        x, y, c = _place()
        cps = []
        for a in range(n):
            for k, (px, py) in enumerate(_other_chips(x, y)):
                cp = pltpu.make_async_remote_copy(
                    src_ref=ins[a].at[2 * px + py], dst_ref=outs[a].at[k], send_sem=send.at[a, k], recv_sem=recv.at[a, k],
                    device_id=(px, py, c), device_id_type=MESH)
                cp.start()
                cps.append(cp)
        for cp in cps:
            cp.wait()

    any_spec = pl.BlockSpec(memory_space=pl.ANY)
    return pl.pallas_call(
        body, name="chip_exchange", in_specs=[any_spec] * n, out_specs=[any_spec] * n,
        out_shape=[SDS((3,) + a.shape[1:], a.dtype) for a in arrs],
        scratch_shapes=[pltpu.SemaphoreType.DMA((n, 3)), pltpu.SemaphoreType.DMA((n, 3))])(*arrs)


def _chip_add(name, hsum, q, chip):
    _, _, R, Cc = hsum.shape
    tr = _rt(R, 512, 8)

    def body(p_ref, h_ref, q_ref, o_ref):
        o_ref[...] = ((h_ref[...] + q_ref[0].astype(F32)) + q_ref[1].astype(F32)) + q_ref[2].astype(F32)

    return pl.pallas_call(
        body, name=name,
        grid_spec=pltpu.PrefetchScalarGridSpec(
            num_scalar_prefetch=1, grid=(2, R // tr),
            in_specs=[pl.BlockSpec((None, None, tr, Cc), lambda l, r, p_ref: (p_ref[0], l, r, 0)),
                      pl.BlockSpec((3, None, tr, Cc), lambda l, r, p_ref: (0, l, r, 0))],
            out_specs=pl.BlockSpec((None, tr, Cc), lambda l, r, p_ref: (l, r, 0))),
        out_shape=SDS((2, R, Cc), F32))(chip, hsum, q)


def _pair_share(arrs):
    n = len(arrs)

    def body(*refs):
        ins, outs = refs[:n], refs[n:2 * n]
        send, recv, lsem = refs[2 * n:]
        x, y, c = _place()
        cps = []
        for a in range(n):
            mine = outs[a].at[pl.ds(2 * c, 2)]
            loc = pltpu.make_async_copy(ins[a], mine, lsem.at[a])
            loc.start()
            cp = pltpu.make_async_remote_copy(
                src_ref=ins[a], dst_ref=mine, send_sem=send.at[a], recv_sem=recv.at[a],
                device_id=(x, y, 1 - c), device_id_type=MESH)
            cp.start()
            cps.append((loc, cp))
        for a, (loc, cp) in enumerate(cps):
            loc.wait()
            cp.wait_send()
            theirs = outs[a].at[pl.ds(2 * (1 - c), 2)]
            pltpu.make_async_remote_copy(
                src_ref=ins[a], dst_ref=theirs, send_sem=send.at[a], recv_sem=recv.at[a],
                device_id=(x, y, 1 - c), device_id_type=MESH).wait_recv()

    any_spec = pl.BlockSpec(memory_space=pl.ANY)
    return pl.pallas_call(
        body, name="pair_share", in_specs=[any_spec] * n, out_specs=[any_spec] * n,
        out_shape=[SDS((DEPTH,) + a.shape[1:], a.dtype) for a in arrs],
        scratch_shapes=[pltpu.SemaphoreType.DMA((n,)), pltpu.SemaphoreType.DMA((n,)), pltpu.SemaphoreType.DMA((n,))])(*arrs)


def _allreduce_small(v):
    rows = v.shape[0]

    def body(v_ref, o_ref, buf, send, recv):
        x, y, c = _place()
        me = 4 * x + 2 * y + c
        buf[me] = v_ref[...]
        cps = []
        k = 0
        for dx in range(2):
            for dy in range(2):
                for dc in range(2):
                    if dx + dy + dc == 0:
                        continue
                    cp = pltpu.make_async_remote_copy(
                        src_ref=v_ref, dst_ref=buf.at[me], send_sem=send.at[k], recv_sem=recv.at[k],
                        device_id=(jnp.bitwise_xor(x, dx), jnp.bitwise_xor(y, dy), jnp.bitwise_xor(c, dc)), device_id_type=MESH)
                    cp.start()
                    cps.append((cp, dx, dy, dc))
                    k += 1
        for k, (cp, dx, dy, dc) in enumerate(cps):
            cp.wait_send()
            src = 4 * jnp.bitwise_xor(x, dx) + 2 * jnp.bitwise_xor(y, dy) + jnp.bitwise_xor(c, dc)
            pltpu.make_async_remote_copy(
                src_ref=v_ref, dst_ref=buf.at[src], send_sem=send.at[k], recv_sem=recv.at[k],
                device_id=(x, y, c), device_id_type=MESH).wait_recv()
        acc = buf[0]
        for d in range(1, 8):
            acc = acc + buf[d]
        o_ref[...] = acc

    vm = pl.BlockSpec(memory_space=pltpu.VMEM)
    return pl.pallas_call(
        body, name="allreduce_small", in_specs=[vm], out_specs=vm, out_shape=SDS((rows, LANE), F32),
        scratch_shapes=[pltpu.VMEM((8, rows, LANE), F32), pltpu.SemaphoreType.DMA((7,)), pltpu.SemaphoreType.DMA((7,))])(v)


def _adamw(name, w, g, m, v):
    A, R, Cc = w.shape
    tr = _rt(R, 512, 8)

    def body(w_ref, g_ref, m_ref, v_ref, d_ref, mo_ref, vo_ref):
        gv = g_ref[...]
        mn = ADAM_B1 * m_ref[...] + (1.0 - ADAM_B1) * gv
        vn = ADAM_B2 * v_ref[...] + (1.0 - ADAM_B2) * (gv * gv)
        m_hat = mn / (1.0 - ADAM_B1 ** ADAM_STEP)
        v_hat = vn / (1.0 - ADAM_B2 ** ADAM_STEP)
        d_ref[...] = -ADAM_LR * (m_hat / (jnp.sqrt(v_hat) + ADAM_EPS) + ADAM_WD * w_ref[...])
        mo_ref[...] = mn
        vo_ref[...] = vn

    blk = pl.BlockSpec((None, tr, Cc), lambda a, r: (a, r, 0))
    return pl.pallas_call(
        body, name=name, grid=(A, R // tr), in_specs=[blk] * 4, out_specs=[blk] * 3,
        out_shape=[SDS(w.shape, F32)] * 3)(w, g, m, v)


BIG = ("gate", "up", "down", "win", "wpa", "wpb", "wout")
SMALL = ("meta", "norm_gains", "w2", "b2", "gn", "qn", "kn", "bm", "fin")


def _view3(a):
    return a.reshape(a.shape[0], -1, a.shape[-1])


def kernel(x, meta_tokens, norm_gains, ffn_w_gate, ffn_w_up, ffn_w_down, w_in, gla_w2, gla_b2, gla_gn, q_norm, k_norm, w_pa, w_pb, b_merge, w_out, final_norm, loss_target, m_meta_tokens, m_norm_gains, m_ffn_w_gate, m_ffn_w_up, m_ffn_w_down, m_w_in, m_gla_w2, m_gla_b2, m_gla_gn, m_q_norm, m_k_norm, m_w_pa, m_w_pb, m_b_merge, m_w_out, m_final_norm, v_meta_tokens, v_norm_gains, v_ffn_w_gate, v_ffn_w_up, v_ffn_w_down, v_w_in, v_gla_w2, v_gla_b2, v_gla_gn, v_q_norm, v_k_norm, v_w_pa, v_w_pb, v_b_merge, v_w_out, v_final_norm):
    big_w = dict(gate=ffn_w_gate, up=ffn_w_up, down=ffn_w_down, win=w_in, wpa=w_pa, wpb=w_pb, wout=w_out)
    big_m = dict(gate=m_ffn_w_gate, up=m_ffn_w_up, down=m_ffn_w_down, win=m_w_in, wpa=m_w_pa, wpb=m_w_pb, wout=m_w_out)
    big_v = dict(gate=v_ffn_w_gate, up=v_ffn_w_up, down=v_ffn_w_down, win=v_w_in, wpa=v_w_pa, wpb=v_w_pb, wout=v_w_out)
    small_w = dict(meta=meta_tokens, norm_gains=norm_gains, w2=gla_w2, b2=gla_b2, gn=gla_gn, qn=q_norm, kn=k_norm,
                   bm=b_merge, fin=final_norm)
    small_m = dict(meta=m_meta_tokens, norm_gains=m_norm_gains, w2=m_gla_w2, b2=m_gla_b2, gn=m_gla_gn, qn=m_q_norm,
                   kn=m_k_norm, bm=m_b_merge, fin=m_final_norm)
    small_v = dict(meta=v_meta_tokens, norm_gains=v_norm_gains, w2=v_gla_w2, b2=v_gla_b2, gn=v_gla_gn, qn=v_q_norm,
                   kn=v_k_norm, bm=v_b_merge, fin=v_final_norm)
    xi, yi, ci = _place()
    chip = (2 * xi + yi).astype(jnp.int32)

    shard_pack = _pack({k: small_w[k] for k in SHARDED_SMALL}, SHARDED_SMALL, SMALL_ROWS)
    gathered = _gather_weights([big_w[k].astype(BF16) for k in BIG] + [shard_pack])
    g = dict(zip(BIG + ("small",), gathered))
    W = _assemble(g)
    W.update(gn=gla_gn, qn=q_norm, kn=k_norm, fin=final_norm)

    loss, dh0, G, gs = _local_step(x[0], loss_target[0], W)
    grad_x = dh0[LANE:][None]

    dwin = _win_from_padded(G.pop("winp")).reshape(DEPTH, D, NSH, D_IN // NSH)
    G["win"] = jnp.transpose(dwin, (2, 0, 1, 3))
    loc = [G[k].reshape((NSH, DEPTH, -1, G[k].shape[-1])) for k in BIG]
    got = _pair_exchange(loc)
    cvec = ci.astype(jnp.int32).reshape(1)
    sums = [_pair_add(f"pair_add_{k}", a, p, cvec) for k, a, p in zip(BIG, loc, got)]
    arrived = _chip_exchange([s[1] for s in sums])
    pvec = chip.reshape(1)
    mine = [_chip_add(f"chip_add_{k}", s[0], q, pvec) for k, s, q in zip(BIG, sums, arrived)]
    shared = _pair_share(mine)
    grads, deltas, new_m, new_v = {}, {}, {}, {}
    for k, r in zip(BIG, shared):
        shp = big_w[k].shape
        grads[k] = r.reshape(shp)
        d, mn, vn = _adamw(f"adamw_{k}", _view3(big_w[k]), r, _view3(big_m[k]), _view3(big_v[k]))
        deltas[k], new_m[k], new_v[k] = d.reshape(shp), mn.reshape(shp), vn.reshape(shp)

    gs["loss"] = loss.reshape(1)
    table = dict(FULL_SMALL, loss=(1,))
    tot = _unpack(_allreduce_small(_pack(gs, table, GRAD_ROWS)), table)
    loss_out = tot["loss"][0]
    sl = dict(meta=(1, 256), norm_gains=(2, 256), w2=(3, 64), b2=(2, 64), bm=(2, 256))
    for k in SMALL:
        gk = tot[k]
        if k in sl:
            ax, width = sl[k]
            gk = lax.dynamic_slice_in_dim(gk, chip * width, width, axis=ax)
        grads[k] = gk
    tbl = {k: small_w[k].shape for k in SMALL}
    rows = _rows_for(tbl, 8)
    packs = [_pack(src, tbl, rows)[None] for src in (small_w, grads, small_m, small_v)]
    d, mn, vn = _adamw("adamw_small", *packs)
    for dst, packed in ((deltas, d), (new_m, mn), (new_v, vn)):
        dst.update(_unpack(packed[0], tbl))

    order = ("meta", "norm_gains", "gate", "up", "down", "win", "w2", "b2", "gn", "qn", "kn", "wpa", "wpb", "bm", "wout", "fin")
    return (loss_out, grad_x, *[grads[k] for k in order], *[deltas[k] for k in order], *[new_m[k] for k in order],
            *[new_v[k] for k in order])
```

```python
import functools

import numpy as np
import jax
import jax.numpy as jnp
from jax import lax
from jax.experimental import pallas as pl
from jax.experimental.pallas import tpu as pltpu

F32, BF16 = jnp.float32, jnp.bfloat16
SDS = jax.ShapeDtypeStruct
HIGHEST = lax.Precision.HIGHEST
MESH = pl.DeviceIdType.MESH

D = 1024
DEPTH = 4
N_META = 16
GRID_W = 64
GLA_H, GLA_DK, GLA_DV, GLA_RANK, GLA_TAU, CHUNK = 4, 64, 128, 16, 16.0, 64
ATT_H, ATT_KV, HD = 8, 2, 64
D_FF = 2816
EPS = 1e-6
ROPE_THETA = 10000.0
ADAM_LR, ADAM_B1, ADAM_B2, ADAM_EPS, ADAM_WD, ADAM_STEP = 0.001, 0.9, 0.999, 1e-08, 0.01, 10

NSH = 4
FF_S = D_FF // NSH
FF_P = 768
LANE = 128
PAD = LANE - N_META
D_IN = 4384
C_GA, C_GB, C_QA, C_KA, C_VA, C_RA, C_QB, C_KB, C_VB, C_LR, HP = 0, 1024, 2048, 2304, 2560, 3072, 3584, 4096, 4224, 4352, 4480
VMEM_BIG = 56 * 2 ** 20


def _rt(n, cap, mult=LANE):
    best = None
    t = mult
    while t <= min(n, cap):
        if n % t == 0:
            best = t
        t += mult
    assert best is not None, (n, cap, mult)
    return best


def _cp(big=False):
    return pltpu.CompilerParams(vmem_limit_bytes=VMEM_BIG) if big else None


def _row_ids(i, tm):
    return i * tm + lax.broadcasted_iota(jnp.int32, (tm, 1), 0)


def _mm(name, a, b, *, grid, a_spec, b_spec, o_spec, o_shape, o_dtype, dims, acc_shape, nk=1, scale=None,
        res=None, res_spec=None, a_sl=None, b_sl=None, pad_w=None, into=None):
    has_res, has_into = res is not None, into is not None

    def body(*refs):
        a_ref, b_ref = refs[0], refs[1]
        p = 2
        res_ref = None
        if has_res:
            res_ref = refs[p]
            p += 1
        if has_into:
            p += 1
        o_ref = refs[p]
        acc_ref = refs[p + 1] if nk > 1 else None
        av = (a_ref[a_sl] if a_sl is not None else a_ref[...]).astype(BF16)
        bv = (b_ref[b_sl] if b_sl is not None else b_ref[...]).astype(BF16)
        prod = lax.dot_general(av, bv, dims, preferred_element_type=F32)

        def finish(v):
            if scale is not None:
                v = v * scale
            if has_res:
                v = v + res_ref[...]
            v = v.astype(o_dtype)
            if pad_w is None:
                o_ref[...] = v
            else:
                w = v.shape[-1]
                o_ref[:, :w] = v
                o_ref[:, w:] = jnp.zeros((v.shape[0], pad_w - w), o_dtype)

        if nk == 1:
            finish(prod)
        else:
            k = pl.program_id(len(grid) - 1)

            @pl.when(k == 0)
            def _():
                acc_ref[...] = prod

            @pl.when(k > 0)
            def _():
                acc_ref[...] += prod

            @pl.when(k == nk - 1)
            def _():
                finish(acc_ref[...])

    in_specs = [a_spec, b_spec]
    args = [a, b]
    if has_res:
        in_specs.append(res_spec)
        args.append(res)
    aliases = {}
    if has_into:
        aliases = {len(args): 0}
        in_specs.append(pl.BlockSpec(memory_space=pl.ANY))
        args.append(into)
        o_shape = into.shape
    return pl.pallas_call(
        body, name=name, grid=grid, in_specs=in_specs, out_specs=o_spec, out_shape=SDS(tuple(o_shape), o_dtype),
        scratch_shapes=[pltpu.VMEM(acc_shape, F32)] if nk > 1 else [], input_output_aliases=aliases,
        compiler_params=_cp(True))(*args)


NN = (((1,), (0,)), ((), ()))
NT = (((1,), (1,)), ((), ()))
TN = (((0,), (0,)), ((), ()))


def _rms_fwd(name, h, g):
    Lp = h.shape[0]
    tm = _rt(Lp, 384)

    def body(h_ref, g_ref, n_ref, nt_ref):
        x = h_ref[...]
        r = lax.rsqrt(jnp.mean(x * x, axis=-1, keepdims=True) + EPS)
        y = x * r * g_ref[...]
        n_ref[...] = y.astype(BF16)
        nt_ref[...] = y.T.astype(BF16)

    return pl.pallas_call(
        body, name=name, grid=(Lp // tm,),
        in_specs=[pl.BlockSpec((tm, D), lambda i: (i, 0)), pl.BlockSpec((1, D), lambda i: (0, 0))],
        out_specs=[pl.BlockSpec((tm, D), lambda i: (i, 0)), pl.BlockSpec((D, tm), lambda i: (0, i))],
        out_shape=[SDS((Lp, D), BF16), SDS((D, Lp), BF16)])(h, g)


def _rms_bwd(name, dn, h, g, dh):
    Lp = h.shape[0]
    tm = _rt(Lp, 384)

    def body(dn_ref, h_ref, g_ref, dh_ref, o_ref, dg_ref):
        x = h_ref[...]
        r = lax.rsqrt(jnp.mean(x * x, axis=-1, keepdims=True) + EPS)
        xh = x * r
        dn = dn_ref[...]
        u = dn * g_ref[...]
        o_ref[...] = dh_ref[...] + r * (u - xh * jnp.mean(u * xh, axis=-1, keepdims=True))

        @pl.when(pl.program_id(0) == 0)
        def _():
            dg_ref[...] = jnp.zeros_like(dg_ref)

        dg_ref[...] += jnp.sum(dn * xh, axis=0, keepdims=True)

    row = pl.BlockSpec((tm, D), lambda i: (i, 0))
    vec = pl.BlockSpec((1, D), lambda i: (0, 0))
    return pl.pallas_call(
        body, name=name, grid=(Lp // tm,), in_specs=[row, row, vec, row], out_specs=[row, vec],
        out_shape=[SDS((Lp, D), F32), SDS((1, D), F32)])(dn, h, g, dh)


def _loss_head(name, h, g, tgt):
    Lp = h.shape[0]
    tm = LANE

    def body(h_ref, g_ref, t_ref, dh_ref, dg_ref, loss_ref):
        i = pl.program_id(0)
        x = h_ref[...]
        r = lax.rsqrt(jnp.mean(x * x, axis=-1, keepdims=True) + EPS)
        xh = x * r
        gg = g_ref[...]
        err = jnp.where(i >= 1, xh * gg - t_ref[...], 0.0)
        dy = err * (1.0 / D)
        u = dy * gg
        dh_ref[...] = r * (u - xh * jnp.mean(u * xh, axis=-1, keepdims=True))

        @pl.when(i == 0)
        def _():
            dg_ref[...] = jnp.zeros_like(dg_ref)
            loss_ref[...] = jnp.zeros_like(loss_ref)

        dg_ref[...] += jnp.sum(dy * xh, axis=0, keepdims=True)
        loss_ref[...] += (0.5 / D) * jnp.sum(err * err)

    row = pl.BlockSpec((tm, D), lambda i: (i, 0))
    vec = pl.BlockSpec((1, D), lambda i: (0, 0))
    return pl.pallas_call(
        body, name=name, grid=(Lp // tm,),
        in_specs=[row, vec, pl.BlockSpec((tm, D), lambda i: (jnp.maximum(i - 1, 0), 0))],
        out_specs=[row, vec, pl.BlockSpec((8, LANE), lambda i: (0, 0))],
        out_shape=[SDS((Lp, D), F32), SDS((1, D), F32), SDS((8, LANE), F32)])(h, g, tgt)


def _swiglu_fwd(name, a, b):
    Lp = a.shape[1]
    tm = _rt(Lp, 384)

    def body(a_ref, b_ref, s_ref, st_ref):
        av = a_ref[...]
        s = av * jax.nn.sigmoid(av) * b_ref[...]
        s_ref[...] = s.astype(BF16)
        st_ref[...] = s.T.astype(BF16)

    slab = pl.BlockSpec((None, tm, FF_P), lambda j, i: (j, i, 0))
    return pl.pallas_call(
        body, name=name, grid=(NSH, Lp // tm), in_specs=[slab, slab],
        out_specs=[slab, pl.BlockSpec((None, FF_P, tm), lambda j, i: (j, 0, i))],
        out_shape=[SDS((NSH, Lp, FF_P), BF16), SDS((NSH, FF_P, Lp), BF16)])(a, b)


def _swiglu_bwd(name, a, b, ds):
    Lp = a.shape[1]
    tm = _rt(Lp, 384)

    def body(a_ref, b_ref, ds_ref, da_ref, db_ref):
        av = a_ref[...]
        sg = jax.nn.sigmoid(av)
        dsv = ds_ref[...]
        da_ref[...] = (dsv * b_ref[...] * (sg * (1.0 + av * (1.0 - sg)))).astype(BF16)
        db_ref[...] = (dsv * (av * sg)).astype(BF16)

    slab = pl.BlockSpec((None, tm, FF_P), lambda j, i: (j, i, 0))
    return pl.pallas_call(
        body, name=name, grid=(NSH, Lp // tm), in_specs=[slab, slab, slab], out_specs=[slab, slab],
        out_shape=[SDS((NSH, Lp, FF_P), BF16), SDS((NSH, Lp, FF_P), BF16)])(a, b, ds)


def _merge_fwd(name, H, pa, pb, bm):
    Lp = H.shape[0]
    tm = _rt(Lp, 384)

    def body(g_ref, pa_ref, pb_ref, bm_ref, y_ref, yt_ref):
        gv = g_ref[...]
        y = (jax.nn.sigmoid(gv[:, :D] + bm_ref[0:1, :]) * pa_ref[...]
             + jax.nn.sigmoid(gv[:, D:] + bm_ref[1:2, :]) * pb_ref[...])
        y_ref[...] = y.astype(BF16)
        yt_ref[...] = y.T.astype(BF16)

    row = pl.BlockSpec((tm, D), lambda i: (i, 0))
    return pl.pallas_call(
        body, name=name, grid=(Lp // tm,),
        in_specs=[pl.BlockSpec((tm, 2 * D), lambda i: (i, 0)), row, row, pl.BlockSpec((2, D), lambda i: (0, 0))],
        out_specs=[row, pl.BlockSpec((D, tm), lambda i: (0, i))],
        out_shape=[SDS((Lp, D), BF16), SDS((D, Lp), BF16)])(H, pa, pb, bm)


def _merge_bwd(name, H, pa, pb, bm, dy):
    Lp = H.shape[0]
    tm = _rt(Lp, 384)

    def body(g_ref, pa_ref, pb_ref, bm_ref, dy_ref, dpa_ref, dpb_ref, dh_ref, dbm_ref):
        gv = g_ref[...]
        dyv = dy_ref[...]
        sa = jax.nn.sigmoid(gv[:, :D] + bm_ref[0:1, :])
        sb = jax.nn.sigmoid(gv[:, D:] + bm_ref[1:2, :])
        dpa_ref[...] = (dyv * sa).astype(BF16)
        dpb_ref[...] = (dyv * sb).astype(BF16)
        dga = dyv * pa_ref[...] * (sa * (1.0 - sa))
        dgb = dyv * pb_ref[...] * (sb * (1.0 - sb))
        dh_ref[:, :D] = dga.astype(BF16)
        dh_ref[:, D:] = dgb.astype(BF16)

        @pl.when(pl.program_id(0) == 0)
        def _():
            dbm_ref[...] = jnp.zeros_like(dbm_ref)

        dbm_ref[0:1, :] += jnp.sum(dga, axis=0, keepdims=True)
        dbm_ref[1:2, :] += jnp.sum(dgb, axis=0, keepdims=True)

    row = pl.BlockSpec((tm, D), lambda i: (i, 0))
    two = pl.BlockSpec((2, D), lambda i: (0, 0))
    gate = pl.BlockSpec((tm, 2 * D), lambda i: (i, 0))
    return pl.pallas_call(
        body, name=name, grid=(Lp // tm,), in_specs=[gate, row, row, two, row], out_specs=[row, row, gate, two],
        out_shape=[SDS((Lp, D), BF16), SDS((Lp, D), BF16), SDS((Lp, HP), BF16), SDS((2, D), F32)])(H, pa, pb, bm, dy)


def _gla_prep(name, H, w2p, b2p):
    Lp = H.shape[0]
    tm = _rt(Lp, 384)

    def body(qk_ref, v_ref, lr_ref, w_ref, b_ref, q_o, k_o, v_o, gf_o, gb_o):
        valid = _row_ids(pl.program_id(0), tm) >= PAD
        qk = qk_ref[...]
        vv = v_ref[...]
        pre = jnp.dot(lr_ref[...].astype(BF16), w_ref[...], preferred_element_type=F32) + b_ref[...]
        g = jnp.where(valid, jax.nn.log_sigmoid(pre) * (1.0 / GLA_TAU), 0.0)
        for hh in range(GLA_H):
            q_o[hh] = qk[:, 64 * hh:64 * hh + 64] * (GLA_DK ** -0.5)
            k_o[hh] = qk[:, 256 + 64 * hh:256 + 64 * hh + 64]
            v_o[hh] = vv[:, 128 * hh:128 * hh + 128].astype(BF16)
            gf_o[hh] = g[:, 64 * hh:64 * hh + 64]
            gb_o[hh] = g[:, 256 + 64 * hh:256 + 64 * hh + 64]

    h64 = pl.BlockSpec((GLA_H, tm, 64), lambda i: (0, i, 0))
    h128 = pl.BlockSpec((GLA_H, tm, 128), lambda i: (0, i, 0))
    return pl.pallas_call(
        body, name=name, grid=(Lp // tm,),
        in_specs=[pl.BlockSpec((tm, 512), lambda i: (i, C_QA // 512)), pl.BlockSpec((tm, 512), lambda i: (i, C_VA // 512)),
                  pl.BlockSpec((tm, LANE), lambda i: (i, C_LR // LANE)), pl.BlockSpec((LANE, 512), lambda i: (0, 0)),
                  pl.BlockSpec((1, 512), lambda i: (0, 0))],
        out_specs=[h64, h64, h128, h64, h64],
        out_shape=[SDS((GLA_H, Lp, 64), F32), SDS((GLA_H, Lp, 64), F32), SDS((GLA_H, Lp, 128), BF16),
                   SDS((GLA_H, Lp, 64), F32), SDS((GLA_H, Lp, 64), F32)])(H, H, H, w2p, b2p)


def _bdot(a, b, ca, cb, precision=None):
    return lax.dot_general(a, b, ((ca, cb), ((0,), (0,))), precision=precision, preferred_element_type=F32)


def _gla_chunk_terms(q_ref, k_ref, g_ref, v_ref, G, rev):
    B = GLA_H * G
    qv = q_ref[...].reshape(B, CHUNK, GLA_DK)
    kv = k_ref[...].reshape(B, CHUNK, GLA_DK)
    gv = g_ref[...].reshape(B, CHUNK, GLA_DK)
    vv = v_ref[...].reshape(B, CHUNK, GLA_DV)
    ii = lax.broadcasted_iota(jnp.int32, (CHUNK, CHUNK), 0)
    jj = lax.broadcasted_iota(jnp.int32, (CHUNK, CHUNK), 1)
    tri = (jj >= ii) if rev else (jj <= ii)
    tb = jnp.broadcast_to(tri.astype(F32)[None], (B, CHUNK, CHUNK))
    bc = _bdot(tb, gv, (2,), (1,), HIGHEST)
    bt = bc[:, 0:1, :] if rev else bc[:, CHUNK - 1:CHUNK, :]
    eq, eki, eke = jnp.exp(bc), jnp.exp(-bc), jnp.exp(bt - bc)
    qd, ki, ke = qv * eq, kv * eki, kv * eke
    att = jnp.where(tri[None], _bdot(qd.astype(BF16), ki.astype(BF16), (2,), (2,)), 0.0)
    dm = jnp.exp(_bdot(gv, jnp.ones((B, CHUNK, GLA_DV), F32), (1,), (1,), HIGHEST))
    return dict(B=B, vv=vv, tri=tri, tb=tb, bt=bt, eq=eq, eki=eki, eke=eke, qd=qd, ki=ki, ke=ke, att=att, dm=dm)


def _gla_fwd(name, q, k, g, v, rev, G):
    Lp = q.shape[1]
    tg = G * CHUNK
    ng = Lp // tg

    def body(q_ref, k_ref, g_ref, v_ref, o_ref, ss_ref, s_scr):
        @pl.when(pl.program_id(0) == 0)
        def _():
            s_scr[...] = jnp.zeros_like(s_scr)

        t = _gla_chunk_terms(q_ref, k_ref, g_ref, v_ref, G, rev)
        B, vv = t["B"], t["vv"]
        qd = t["qd"].astype(BF16)
        oi = _bdot(t["att"].astype(BF16), vv, (2,), (1,))
        kvc = _bdot(t["ke"].astype(BF16), vv, (1,), (1,)).reshape(GLA_H, G, GLA_DK, GLA_DV)
        dm = t["dm"].reshape(GLA_H, G, GLA_DK, GLA_DV)
        s = s_scr[...]
        sp = [None] * G
        for c in (range(G - 1, -1, -1) if rev else range(G)):
            sp[c] = s
            ss_ref[c] = s
            s = dm[:, c] * s + kvc[:, c]
        s_scr[...] = s
        spb = jnp.stack(sp, axis=1).reshape(B, GLA_DK, GLA_DV).astype(BF16)
        o_ref[...] = (oi + _bdot(qd, spb, (2,), (1,))).reshape(GLA_H, tg, GLA_DV)

    blk = (lambda i: (0, ng - 1 - i, 0)) if rev else (lambda i: (0, i, 0))
    sblk = (lambda i: (ng - 1 - i, 0, 0, 0)) if rev else (lambda i: (i, 0, 0, 0))
    h64 = pl.BlockSpec((GLA_H, tg, 64), blk)
    h128 = pl.BlockSpec((GLA_H, tg, 128), blk)
    return pl.pallas_call(
        body, name=name, grid=(ng,), in_specs=[h64, h64, h64, h128],
        out_specs=[h128, pl.BlockSpec((G, GLA_H, GLA_DK, GLA_DV), sblk)],
        out_shape=[SDS((GLA_H, Lp, GLA_DV), F32), SDS((Lp // CHUNK, GLA_H, GLA_DK, GLA_DV), F32)],
        scratch_shapes=[pltpu.VMEM((GLA_H, GLA_DK, GLA_DV), F32)], compiler_params=_cp(True))(q, k, g, v)


def _gla_bwd(name, q, k, g, v, ss, do, rev, G):
    Lp = q.shape[1]
    tg = G * CHUNK
    ng = Lp // tg

    def body(q_ref, k_ref, g_ref, v_ref, ss_ref, do_ref, dq_ref, dk_ref, dg_ref, dv_ref, ds_scr):
        @pl.when(pl.program_id(0) == 0)
        def _():
            ds_scr[...] = jnp.zeros_like(ds_scr)

        t = _gla_chunk_terms(q_ref, k_ref, g_ref, v_ref, G, rev)
        B, vv, tri = t["B"], t["vv"], t["tri"]
        qd, ki, ke = t["qd"], t["ki"], t["ke"]
        qdb, kib, keb = qd.astype(BF16), ki.astype(BF16), ke.astype(BF16)
        sp = jnp.stack([ss_ref[c] for c in range(G)], axis=1).reshape(B, GLA_DK, GLA_DV)
        dob = do_ref[...].reshape(B, CHUNK, GLA_DV).astype(BF16)
        da = jnp.where(tri[None], _bdot(dob, vv, (2,), (2,)), 0.0).astype(BF16)
        dqd = _bdot(da, kib, (2,), (1,)) + _bdot(dob, sp.astype(BF16), (2,), (2,))
        dki = _bdot(da, qdb, (1,), (1,))
        dv = _bdot(t["att"].astype(BF16), dob, (1,), (1,))
        cc = _bdot(qdb, dob, (1,), (1,)).reshape(GLA_H, G, GLA_DK, GLA_DV)
        dm = t["dm"].reshape(GLA_H, G, GLA_DK, GLA_DV)
        dsc = ds_scr[...]
        dsn = [None] * G
        for c in (range(G) if rev else range(G - 1, -1, -1)):
            dsn[c] = dsc
            dsc = dm[:, c] * dsc + cc[:, c]
        ds_scr[...] = dsc
        dsn = jnp.stack(dsn, axis=1).reshape(B, GLA_DK, GLA_DV)
        dsnb = dsn.astype(BF16)
        dv = dv + _bdot(keb, dsnb, (2,), (1,))
        dke = _bdot(vv, dsnb, (2,), (2,))
        ddrow = _bdot(jnp.ones((B, CHUNK, GLA_DV), F32), dsn * sp, (2,), (2,), HIGHEST)
        dbt = ddrow * jnp.exp(t["bt"]) + jnp.sum(dke * ke, axis=1, keepdims=True)
        db = dqd * qd - dki * ki - dke * ke
        dq_ref[...] = (dqd * t["eq"]).reshape(GLA_H, tg, GLA_DK)
        dk_ref[...] = (dki * t["eki"] + dke * t["eke"]).reshape(GLA_H, tg, GLA_DK)
        dg_ref[...] = (_bdot(t["tb"], db, (1,), (1,), HIGHEST) + dbt).reshape(GLA_H, tg, GLA_DK)
        dv_ref[...] = dv.reshape(GLA_H, tg, GLA_DV)

    blk = (lambda i: (0, i, 0)) if rev else (lambda i: (0, ng - 1 - i, 0))
    sblk = (lambda i: (i, 0, 0, 0)) if rev else (lambda i: (ng - 1 - i, 0, 0, 0))
    h64 = pl.BlockSpec((GLA_H, tg, 64), blk)
    h128 = pl.BlockSpec((GLA_H, tg, 128), blk)
    return pl.pallas_call(
        body, name=name, grid=(ng,),
        in_specs=[h64, h64, h64, h128, pl.BlockSpec((G, GLA_H, GLA_DK, GLA_DV), sblk), h128],
        out_specs=[h64, h64, h64, h128],
        out_shape=[SDS((GLA_H, Lp, 64), F32), SDS((GLA_H, Lp, 64), F32), SDS((GLA_H, Lp, 64), F32),
                   SDS((GLA_H, Lp, GLA_DV), F32)],
        scratch_shapes=[pltpu.VMEM((GLA_H, GLA_DK, GLA_DV), F32)], compiler_params=_cp(True))(q, k, g, v, ss, do)


def _gla_post(name, of, ob, H, gn):
    Lp = H.shape[0]
    tm = _rt(Lp, 384)

    def body(of_ref, ob_ref, r_ref, gn_ref, a_ref, at_ref):
        parts = []
        for hh in range(GLA_H):
            o = of_ref[hh] + ob_ref[hh]
            parts.append(o * lax.rsqrt(jnp.mean(o * o, axis=-1, keepdims=True) + EPS))
        rv = r_ref[...]
        a = (jnp.concatenate(parts, axis=1) * gn_ref[...]) * (rv * jax.nn.sigmoid(rv))
        a_ref[...] = a.astype(BF16)
        at_ref[...] = a.T.astype(BF16)

    h128 = pl.BlockSpec((GLA_H, tm, 128), lambda i: (0, i, 0))
    return pl.pallas_call(
        body, name=name, grid=(Lp // tm,),
        in_specs=[h128, h128, pl.BlockSpec((tm, 512), lambda i: (i, C_RA // 512)), pl.BlockSpec((1, 512), lambda i: (0, 0))],
        out_specs=[pl.BlockSpec((tm, 512), lambda i: (i, 0)), pl.BlockSpec((512, tm), lambda i: (0, i))],
        out_shape=[SDS((Lp, 512), BF16), SDS((512, Lp), BF16)])(of, ob, H, gn)


def _gla_post_bwd(name, of, ob, H, gn, da, dH):
    Lp = H.shape[0]
    tm = _rt(Lp, 384)

    def body(of_ref, ob_ref, r_ref, gn_ref, da_ref, dh_in, do_ref, dh_ref, dgn_ref):
        rv = r_ref[...]
        sg = jax.nn.sigmoid(rv)
        dav = da_ref[...]
        gnv = gn_ref[...]
        ons, rs = [], []
        for hh in range(GLA_H):
            o = of_ref[hh] + ob_ref[hh]
            r = lax.rsqrt(jnp.mean(o * o, axis=-1, keepdims=True) + EPS)
            rs.append(r)
            ons.append(o * r)
        on = jnp.concatenate(ons, axis=1)
        dw = dav * (rv * sg)
        dh_ref[...] = (dav * (on * gnv) * (sg * (1.0 + rv * (1.0 - sg)))).astype(BF16)

        @pl.when(pl.program_id(0) == 0)
        def _():
            dgn_ref[...] = jnp.zeros_like(dgn_ref)

        dgn_ref[...] += jnp.sum(dw * on, axis=0, keepdims=True)
        don = dw * gnv
        for hh in range(GLA_H):
            dd = don[:, 128 * hh:128 * hh + 128]
            do_ref[hh] = rs[hh] * (dd - ons[hh] * jnp.mean(dd * ons[hh], axis=-1, keepdims=True))

    h128 = pl.BlockSpec((GLA_H, tm, 128), lambda i: (0, i, 0))
    rblk = pl.BlockSpec((tm, 512), lambda i: (i, C_RA // 512))
    vec = pl.BlockSpec((1, 512), lambda i: (0, 0))
    return pl.pallas_call(
        body, name=name, grid=(Lp // tm,),
        in_specs=[h128, h128, rblk, vec, pl.BlockSpec((tm, 512), lambda i: (i, 0)), pl.BlockSpec(memory_space=pl.ANY)],
        out_specs=[h128, rblk, vec],
        out_shape=[SDS((GLA_H, Lp, 128), F32), SDS(dH.shape, BF16), SDS((1, 512), F32)],
        input_output_aliases={5: 1})(of, ob, H, gn, da, dH)


def _gla_qkv_bwd(name, dqf, dqb, dkf, dkb, dvf, dvb, dH):
    Lp = dqf.shape[1]
    tm = _rt(Lp, 384)

    def body(dqf_ref, dqb_ref, dkf_ref, dkb_ref, dvf_ref, dvb_ref, dh_in, dh_ref):
        valid = _row_ids(pl.program_id(0), tm) >= PAD
        for hh in range(GLA_H):
            dq = (dqf_ref[hh] + dqb_ref[hh]) * (GLA_DK ** -0.5)
            dh_ref[:, 64 * hh:64 * hh + 64] = jnp.where(valid, dq, 0.0).astype(BF16)
            dh_ref[:, 256 + 64 * hh:256 + 64 * hh + 64] = jnp.where(valid, dkf_ref[hh] + dkb_ref[hh], 0.0).astype(BF16)
            dh_ref[:, 512 + 128 * hh:512 + 128 * hh + 128] = jnp.where(valid, dvf_ref[hh] + dvb_ref[hh], 0.0).astype(BF16)

    h64 = pl.BlockSpec((GLA_H, tm, 64), lambda i: (0, i, 0))
    h128 = pl.BlockSpec((GLA_H, tm, 128), lambda i: (0, i, 0))
    return pl.pallas_call(
        body, name=name, grid=(Lp // tm,),
        in_specs=[h64, h64, h64, h64, h128, h128, pl.BlockSpec(memory_space=pl.ANY)],
        out_specs=pl.BlockSpec((tm, 1024), lambda i: (i, C_QA // 1024)), out_shape=SDS(dH.shape, BF16),
        input_output_aliases={6: 0})(dqf, dqb, dkf, dkb, dvf, dvb, dH)


def _gla_gate_bwd(name, H, w2p, b2p, dgf, dgb, dH):
    Lp = H.shape[0]
    tm = _rt(Lp, 384)

    def body(lr_ref, w_ref, b_ref, dgf_ref, dgb_ref, dh_in, dh_ref, dw_ref, db_ref, dg_scr):
        valid = _row_ids(pl.program_id(0), tm) >= PAD
        for hh in range(GLA_H):
            dg_scr[:, 64 * hh:64 * hh + 64] = dgf_ref[hh]
            dg_scr[:, 256 + 64 * hh:256 + 64 * hh + 64] = dgb_ref[hh]
        lrb = lr_ref[...].astype(BF16)
        wv = w_ref[...]
        pre = jnp.dot(lrb, wv, preferred_element_type=F32) + b_ref[...]
        dpre = jnp.where(valid, dg_scr[...] * (1.0 / GLA_TAU) * jax.nn.sigmoid(-pre), 0.0)
        dpb = dpre.astype(BF16)
        dh_ref[...] = lax.dot_general(dpb, wv, NT, preferred_element_type=F32).astype(BF16)

        @pl.when(pl.program_id(0) == 0)
        def _():
            dw_ref[...] = jnp.zeros_like(dw_ref)
            db_ref[...] = jnp.zeros_like(db_ref)

        dw_ref[...] += lax.dot_general(lrb, dpb, TN, preferred_element_type=F32)
        db_ref[...] += jnp.sum(dpre, axis=0, keepdims=True)

    h64 = pl.BlockSpec((GLA_H, tm, 64), lambda i: (0, i, 0))
    lrblk = pl.BlockSpec((tm, LANE), lambda i: (i, C_LR // LANE))
    wblk = pl.BlockSpec((LANE, 512), lambda i: (0, 0))
    vec = pl.BlockSpec((1, 512), lambda i: (0, 0))
    return pl.pallas_call(
        body, name=name, grid=(Lp // tm,),
        in_specs=[lrblk, wblk, vec, h64, h64, pl.BlockSpec(memory_space=pl.ANY)],
        out_specs=[lrblk, wblk, vec],
        out_shape=[SDS(dH.shape, BF16), SDS((LANE, 512), F32), SDS((1, 512), F32)],
        scratch_shapes=[pltpu.VMEM((tm, 512), F32)], input_output_aliases={5: 0})(H, w2p, b2p, dgf, dgb, dH)


def _swap16(x):
    n = x.shape[1]
    lane = lax.broadcasted_iota(jnp.int32, x.shape, 1)
    return jnp.where(lane % 32 < 16, pltpu.roll(x, n - 16, 1), pltpu.roll(x, 16, 1))


def _headnorm_rope(x, gain, cos, sin, bd):
    r = lax.rsqrt(jnp.dot(x * x, bd, precision=HIGHEST, preferred_element_type=F32) + EPS)
    xh = x * r
    xn = xh * gain
    return xn * cos + _swap16(xn) * sin, xh, r


def _headnorm_rope_bwd(dxr, xh, r, gain, cos, sin, bd):
    dxn = cos * dxr + _swap16(sin * dxr)
    u = dxn * gain
    dx = r * (u - xh * jnp.dot(u * xh, bd, precision=HIGHEST, preferred_element_type=F32))
    return dx, jnp.sum(dxn * xh, axis=0, keepdims=True)


def _att_prep(name, H, gq, gk, cos, sin, bd):
    Lp = H.shape[0]
    tm = _rt(Lp, 384)

    def body(q_ref, kv_ref, gq_ref, gk_ref, c_ref, s_ref, bd_ref, qp_ref, k_ref, v_ref):
        c1, s1 = c_ref[...], s_ref[...]
        c4, s4 = jnp.concatenate([c1] * 4, axis=1), jnp.concatenate([s1] * 4, axis=1)
        xr, _, _ = _headnorm_rope(q_ref[...], gq_ref[...], c4, s4, bd_ref[...])
        xr = xr * (HD ** -0.5)
        lane = lax.broadcasted_iota(jnp.int32, (tm, LANE), 1)
        for hh in range(ATT_H):
            grp = xr[:, LANE * (hh // 2):LANE * (hh // 2) + LANE]
            e, gi = hh % 2, hh // 4
            if e != gi:
                grp = pltpu.roll(grp, 64, 1)
            keep = (lane < 64) if gi == 0 else (lane >= 64)
            qp_ref[hh] = jnp.where(keep, grp, 0.0).astype(BF16)
        kv = kv_ref[...]
        kr, _, _ = _headnorm_rope(kv[:, :LANE], gk_ref[...], c1, s1, bd_ref[0:LANE, 0:LANE])
        k_ref[...] = kr.astype(BF16)
        v_ref[...] = kv[:, LANE:].astype(BF16)

    row128 = pl.BlockSpec((tm, LANE), lambda i: (i, 0))
    return pl.pallas_call(
        body, name=name, grid=(Lp // tm,),
        in_specs=[pl.BlockSpec((tm, 512), lambda i: (i, C_QB // 512)), pl.BlockSpec((tm, 256), lambda i: (i, C_KB // 256)),
                  pl.BlockSpec((1, 512), lambda i: (0, 0)), pl.BlockSpec((1, LANE), lambda i: (0, 0)), row128, row128,
                  pl.BlockSpec((512, 512), lambda i: (0, 0))],
        out_specs=[pl.BlockSpec((ATT_H, tm, LANE), lambda i: (0, i, 0)), row128, row128],
        out_shape=[SDS((ATT_H, Lp, LANE), BF16), SDS((Lp, LANE), BF16), SDS((Lp, LANE), BF16)])(H, H, gq, gk, cos, sin, bd)


def _att_q_bwd(name, H, gq, cos, sin, bd, dqp, dH):
    Lp = H.shape[0]
    tm = _rt(Lp, 384)

    def body(q_ref, gq_ref, c_ref, s_ref, bd_ref, dqp_ref, dh_in, dh_ref, dg_ref):
        c1, s1 = c_ref[...], s_ref[...]
        c4, s4 = jnp.concatenate([c1] * 4, axis=1), jnp.concatenate([s1] * 4, axis=1)
        gqv = gq_ref[...]
        _, xh, r = _headnorm_rope(q_ref[...], gqv, c4, s4, bd_ref[...])
        lane = lax.broadcasted_iota(jnp.int32, (tm, LANE), 1)
        groups = []
        for j in range(ATT_H // 2):
            pieces = []
            for e in range(2):
                hh = 2 * j + e
                piece = dqp_ref[hh]
                if e != hh // 4:
                    piece = pltpu.roll(piece, 64, 1)
                pieces.append(piece)
            groups.append(jnp.where(lane < 64, pieces[0], pieces[1]))
        dxr = jnp.concatenate(groups, axis=1) * (HD ** -0.5)
        dx, dg = _headnorm_rope_bwd(dxr, xh, r, gqv, c4, s4, bd_ref[...])
        dh_ref[...] = dx.astype(BF16)

        @pl.when(pl.program_id(0) == 0)
        def _():
            dg_ref[...] = jnp.zeros_like(dg_ref)

        dg_ref[...] += dg

    row128 = pl.BlockSpec((tm, LANE), lambda i: (i, 0))
    qblk = pl.BlockSpec((tm, 512), lambda i: (i, C_QB // 512))
    vec = pl.BlockSpec((1, 512), lambda i: (0, 0))
    return pl.pallas_call(
        body, name=name, grid=(Lp // tm,),
        in_specs=[qblk, vec, row128, row128, pl.BlockSpec((512, 512), lambda i: (0, 0)),
                  pl.BlockSpec((ATT_H, tm, LANE), lambda i: (0, i, 0)), pl.BlockSpec(memory_space=pl.ANY)],
        out_specs=[qblk, vec], out_shape=[SDS(dH.shape, BF16), SDS((1, 512), F32)],
        input_output_aliases={6: 0})(H, gq, cos, sin, bd, dqp, dH)


def _att_kv_bwd(name, H, gk, cos, sin, bd, dkr, dvb, dH):
    Lp = H.shape[0]
    tm = _rt(Lp, 384)

    def body(kv_ref, gk_ref, c_ref, s_ref, bd_ref, dk_ref, dv_ref, dh_in, dh_ref, dg_ref):
        c1, s1 = c_ref[...], s_ref[...]
        gkv = gk_ref[...]
        bdv = bd_ref[0:LANE, 0:LANE]
        _, xh, r = _headnorm_rope(kv_ref[:, :LANE], gkv, c1, s1, bdv)
        dx, dg = _headnorm_rope_bwd(dk_ref[...], xh, r, gkv, c1, s1, bdv)
        dh_ref[:, :LANE] = dx.astype(BF16)
        dh_ref[:, LANE:] = dv_ref[...].astype(BF16)

        @pl.when(pl.program_id(0) == 0)
        def _():
            dg_ref[...] = jnp.zeros_like(dg_ref)

        dg_ref[...] += dg

    row128 = pl.BlockSpec((tm, LANE), lambda i: (i, 0))
    kvblk = pl.BlockSpec((tm, 256), lambda i: (i, C_KB // 256))
    vec = pl.BlockSpec((1, LANE), lambda i: (0, 0))
    return pl.pallas_call(
        body, name=name, grid=(Lp // tm,),
        in_specs=[kvblk, vec, row128, row128, pl.BlockSpec((512, 512), lambda i: (0, 0)), row128, row128,
                  pl.BlockSpec(memory_space=pl.ANY)],
        out_specs=[kvblk, vec], out_shape=[SDS(dH.shape, BF16), SDS((1, LANE), F32)],
        input_output_aliases={7: 0})(H, gk, cos, sin, bd, dkr, dvb, dH)


def _att_fwd(name, qp, kr, vb, bias):
    Lp = kr.shape[0]
    tq = _rt(Lp, 384)

    def body(q_ref, k_ref, v_ref, bias_ref, b_ref, bt_ref, lse_ref):
        j, i = pl.program_id(0), pl.program_id(1)
        valid = _row_ids(i, tq) >= PAD
        kk, vv, bb = k_ref[...], v_ref[...], bias_ref[...]
        outs = []
        for e in range(2):
            s = lax.dot_general(q_ref[e], kk, NT, preferred_element_type=F32) + bb
            m = jnp.max(s, axis=-1, keepdims=True)
            p = jnp.exp(s - m)
            l = jnp.sum(p, axis=-1, keepdims=True)
            p = p * (1.0 / l)
            o = jnp.dot(p.astype(BF16), vv, preferred_element_type=F32)
            outs.append(jnp.where(valid, o, 0.0))
            lse_ref[e] = m + jnp.log(l)
        lane = lax.broadcasted_iota(jnp.int32, (tq, LANE), 1)
        low = j < 2
        o0 = jnp.where(low, outs[0], pltpu.roll(outs[0], 64, 1))
        o1 = jnp.where(low, pltpu.roll(outs[1], 64, 1), outs[1])
        blk = jnp.where(lane < 64, o0, o1)
        b_ref[...] = blk.astype(BF16)
        bt_ref[...] = blk.T.astype(BF16)

    full = pl.BlockSpec((Lp, LANE), lambda j, i: (0, 0))
    return pl.pallas_call(
        body, name=name, grid=(ATT_H // 2, Lp // tq),
        in_specs=[pl.BlockSpec((2, tq, LANE), lambda j, i: (j, i, 0)), full, full, pl.BlockSpec((1, Lp), lambda j, i: (0, 0))],
        out_specs=[pl.BlockSpec((tq, LANE), lambda j, i: (i, j)), pl.BlockSpec((LANE, tq), lambda j, i: (j, i)),
                   pl.BlockSpec((2, tq, 1), lambda j, i: (j, i, 0))],
        out_shape=[SDS((Lp, 512), BF16), SDS((512, Lp), BF16), SDS((ATT_H, Lp, 1), F32)],
        compiler_params=_cp(True))(qp, kr, vb, bias)


def _att_bwd(name, qp, kr, vb, bias, lse, db):
    Lp = kr.shape[0]
    tq = _rt(Lp, 384)
    nq = Lp // tq

    def body(q_ref, k_ref, v_ref, bias_ref, lse_ref, db_ref, dq_ref, dk_ref, dv_ref, dkt_scr, dvt_scr):
        j, i = pl.program_id(0), pl.program_id(1)

        @pl.when((j == 0) & (i == 0))
        def _():
            dkt_scr[...] = jnp.zeros_like(dkt_scr)
            dvt_scr[...] = jnp.zeros_like(dvt_scr)

        kk, vv, bb = k_ref[...], v_ref[...], bias_ref[...]
        dbv = db_ref[...]
        rolled = pltpu.roll(dbv, 64, 1)
        lane = lax.broadcasted_iota(jnp.int32, (tq, LANE), 1)
        low = j < 2
        first = jnp.where(low, 0, 64)
        keep = (lane >= first) & (lane < first + 64)
        for e in range(2):
            qe = q_ref[e]
            s = lax.dot_general(qe, kk, NT, preferred_element_type=F32) + bb
            p = jnp.exp(s - lse_ref[e])
            src = jnp.where(low, dbv, rolled) if e == 0 else jnp.where(low, rolled, dbv)
            dop = jnp.where(keep, src, 0.0).astype(BF16)
            dp = lax.dot_general(dop, vv, NT, preferred_element_type=F32)
            ds = (p * (dp - jnp.sum(p * dp, axis=-1, keepdims=True))).astype(BF16)
            dq_ref[e] = jnp.where(keep, jnp.dot(ds, kk, preferred_element_type=F32), 0.0)
            dkt_scr[...] += lax.dot_general(qe, ds, TN, preferred_element_type=F32)
            dvt_scr[...] += lax.dot_general(dop, p.astype(BF16), TN, preferred_element_type=F32)

        @pl.when((j == ATT_H // 2 - 1) & (i == nq - 1))
        def _():
            dk_ref[...] = dkt_scr[...].T
            dv_ref[...] = dvt_scr[...].T

    full = pl.BlockSpec((Lp, LANE), lambda j, i: (0, 0))
    pair = pl.BlockSpec((2, tq, LANE), lambda j, i: (j, i, 0))
    return pl.pallas_call(
        body, name=name, grid=(ATT_H // 2, nq),
        in_specs=[pair, full, full, pl.BlockSpec((1, Lp), lambda j, i: (0, 0)), pl.BlockSpec((2, tq, 1), lambda j, i: (j, i, 0)),
                  pl.BlockSpec((tq, LANE), lambda j, i: (i, j))],
        out_specs=[pair, full, full],
        out_shape=[SDS((ATT_H, Lp, LANE), F32), SDS((Lp, LANE), F32), SDS((Lp, LANE), F32)],
        scratch_shapes=[pltpu.VMEM((LANE, Lp), F32), pltpu.VMEM((LANE, Lp), F32)], compiler_params=_cp(True))(
            qp, kr, vb, bias, lse, db)


def _rope_tables(n_tok):
    rows = n_tok // GRID_W
    row = jnp.repeat(jnp.arange(rows), GRID_W).astype(F32)
    col = jnp.tile(jnp.arange(GRID_W), rows).astype(F32)
    axis_dim = HD // 2
    inv = ROPE_THETA ** (-jnp.arange(0, axis_dim, 2, dtype=F32) / axis_dim)
    ang = jnp.concatenate([row[:, None] * inv, col[:, None] * inv], axis=-1)
    ang = jnp.concatenate([jnp.zeros((LANE, axis_dim), F32), ang], axis=0)
    c, s = jnp.cos(ang), jnp.sin(ang)
    c64 = jnp.concatenate([c[:, :16], c[:, :16], c[:, 16:], c[:, 16:]], axis=1)
    s64 = jnp.concatenate([-s[:, :16], s[:, :16], -s[:, 16:], s[:, 16:]], axis=1)
    return jnp.concatenate([c64, c64], axis=1), jnp.concatenate([s64, s64], axis=1)


def _ffn_fwd(tag, h, W, l, j):
    Lp = h.shape[0]
    tm = _rt(Lp, 1408)
    g = W["norm_gains"][l, 2 * j].reshape(1, D)
    n, nt = _rms_fwd(f"rmsf_{tag}", h, g)
    wspec = pl.BlockSpec((None, None, None, D, FF_S), lambda s, i: (s, l, j, 0, 0))
    up = functools.partial(
        _mm, grid=(NSH, Lp // tm), a_spec=pl.BlockSpec((tm, D), lambda s, i: (i, 0)), b_spec=wspec,
        o_spec=pl.BlockSpec((None, tm, FF_P), lambda s, i: (s, i, 0)), o_shape=(NSH, Lp, FF_P), o_dtype=F32, dims=NN,
        acc_shape=None, pad_w=FF_P)
    a = up(f"ffa_{tag}", n, W["gate"])
    b = up(f"ffb_{tag}", n, W["up"])
    s, st = _swiglu_fwd(f"swf_{tag}", a, b)
    row = pl.BlockSpec((tm, D), lambda i, s: (i, 0))
    h2 = _mm(f"ffd_{tag}", s, W["down"], grid=(Lp // tm, NSH),
             a_spec=pl.BlockSpec((None, tm, FF_P), lambda i, s: (s, i, 0)), a_sl=(slice(None), slice(0, FF_S)),
             b_spec=pl.BlockSpec((None, None, None, FF_S, D), lambda i, s: (s, l, j, 0, 0)),
             o_spec=row, o_shape=(Lp, D), o_dtype=F32, dims=NN, acc_shape=(tm, D), nk=NSH, scale=0.5, res=h, res_spec=row)
    return h2, dict(h=h, g=g, nt=nt, a=a, b=b, st=st)


def _ffn_bwd(tag, dh, sv, W, G, l, j):
    Lp = dh.shape[0]
    tm = _rt(Lp, 1408)
    tn = 512
    ds = _mm(f"bds_{tag}", dh, W["down"], grid=(NSH, Lp // tm), a_spec=pl.BlockSpec((tm, D), lambda s, i: (i, 0)),
             b_spec=pl.BlockSpec((None, None, None, FF_S, D), lambda s, i: (s, l, j, 0, 0)),
             o_spec=pl.BlockSpec((None, tm, FF_P), lambda s, i: (s, i, 0)), o_shape=(NSH, Lp, FF_P), o_dtype=F32, dims=NT,
             acc_shape=None, scale=0.5, pad_w=FF_P)
    G["down"] = _mm(f"bwd_{tag}", sv["st"], dh, grid=(NSH, D // tn),
                    a_spec=pl.BlockSpec((None, FF_P, Lp), lambda s, n: (s, 0, 0)), a_sl=(slice(0, FF_S), slice(None)),
                    b_spec=pl.BlockSpec((Lp, tn), lambda s, n: (0, n)),
                    o_spec=pl.BlockSpec((None, None, None, FF_S, tn), lambda s, n: (s, l, j, 0, n)), o_shape=None, o_dtype=F32,
                    dims=NN, acc_shape=None, scale=0.5, into=G["down"])
    da, db = _swiglu_bwd(f"swb_{tag}", sv["a"], sv["b"], ds)
    for key, dact in (("gate", da), ("up", db)):
        G[key] = _mm(f"bw{key[0]}_{tag}", sv["nt"], dact, grid=(NSH, D // tn),
                     a_spec=pl.BlockSpec((tn, Lp), lambda s, m: (m, 0)),
                     b_spec=pl.BlockSpec((None, Lp, FF_P), lambda s, m: (s, 0, 0)), b_sl=(slice(None), slice(0, FF_S)),
                     o_spec=pl.BlockSpec((None, None, None, tn, FF_S), lambda s, m: (s, l, j, m, 0)), o_shape=None,
                     o_dtype=F32, dims=NN, acc_shape=None, into=G[key])
    row = pl.BlockSpec((tm, D), lambda i, s: (i, 0))
    dn = None
    for key, dact in (("gate", da), ("up", db)):
        dn = _mm(f"bn{key[0]}_{tag}", dact, W[key], grid=(Lp // tm, NSH),
                 a_spec=pl.BlockSpec((None, tm, FF_P), lambda i, s: (s, i, 0)), a_sl=(slice(None), slice(0, FF_S)),
                 b_spec=pl.BlockSpec((None, None, None, D, FF_S), lambda i, s: (s, l, j, 0, 0)),
                 o_spec=row, o_shape=(Lp, D), o_dtype=F32, dims=NT, acc_shape=(tm, D), nk=NSH,
                 res=dn, res_spec=row if dn is not None else None)
    return _rms_bwd(f"rmsb_{tag}", dn, sv["h"], sv["g"], dh)


def _mixer_fwd(tag, h, W, C, l, G_chunks):
    Lp = h.shape[0]
    tm = _rt(Lp, 1408)
    g = W["norm_gains"][l, 1].reshape(1, D)
    z, zt = _rms_fwd(f"rmsf_{tag}", h, g)
    tn = 896
    H = _mm(f"win_{tag}", z, W["winp"], grid=(HP // tn, Lp // tm), a_spec=pl.BlockSpec((tm, D), lambda n, i: (i, 0)),
            b_spec=pl.BlockSpec((None, D, tn), lambda n, i: (l, 0, n)), o_spec=pl.BlockSpec((tm, tn), lambda n, i: (i, n)),
            o_shape=(Lp, HP), o_dtype=F32, dims=NN, acc_shape=None)
    w2p, b2p = C["w2p"][l], C["b2p"][l]
    qh, kh, vh, gf, gb = _gla_prep(f"glap_{tag}", H, w2p, b2p)
    of, sf = _gla_fwd(f"glaf_{tag}", qh, kh, gf, vh, False, G_chunks)
    ob, sb = _gla_fwd(f"glar_{tag}", qh, kh, gb, vh, True, G_chunks)
    gn = W["gn"][l].reshape(1, 512)
    a, at = _gla_post(f"glao_{tag}", of, ob, H, gn)
    gq, gk = C["gq"][l], C["gk"][l]
    qp, kr, vb = _att_prep(f"attp_{tag}", H, gq, gk, C["cos"], C["sin"], C["bd"])
    b, bt, lse = _att_fwd(f"attf_{tag}", qp, kr, vb, C["bias"])
    proj = functools.partial(
        _mm, grid=(NSH, Lp // tm), a_spec=pl.BlockSpec((tm, 512), lambda s, i: (i, 0)),
        b_spec=pl.BlockSpec((None, None, 512, 256), lambda s, i: (s, l, 0, 0)), o_spec=pl.BlockSpec((tm, 256), lambda s, i: (i, s)),
        o_shape=(Lp, D), o_dtype=F32, dims=NN, acc_shape=None)
    pa = proj(f"pa_{tag}", a, W["wpa"])
    pb = proj(f"pb_{tag}", b, W["wpb"])
    bm = W["bm"][l]
    y, yt = _merge_fwd(f"mrg_{tag}", H, pa, pb, bm)
    row = pl.BlockSpec((tm, D), lambda i, s: (i, 0))
    h2 = _mm(f"wout_{tag}", y, W["wout"], grid=(Lp // tm, NSH), a_spec=pl.BlockSpec((tm, 256), lambda i, s: (i, s)),
             b_spec=pl.BlockSpec((None, None, 256, D), lambda i, s: (s, l, 0, 0)), o_spec=row, o_shape=(Lp, D), o_dtype=F32,
             dims=NN, acc_shape=(tm, D), nk=NSH, res=h, res_spec=row)
    sv = dict(h=h, g=g, zt=zt, H=H, w2p=w2p, b2p=b2p, qh=qh, kh=kh, vh=vh, gf=gf, gb=gb, of=of, ob=ob, sf=sf, sb=sb, gn=gn,
              at=at, gq=gq, gk=gk, qp=qp, kr=kr, vb=vb, bt=bt, lse=lse, pa=pa, pb=pb, bm=bm, yt=yt)
    return h2, sv


def _mixer_bwd(tag, dh, sv, W, C, G, S, l, G_chunks):
    Lp = dh.shape[0]
    tm = _rt(Lp, 1408)
    H = sv["H"]
    dy = _mm(f"bdy_{tag}", dh, W["wout"], grid=(NSH, Lp // tm), a_spec=pl.BlockSpec((tm, D), lambda s, i: (i, 0)),
             b_spec=pl.BlockSpec((None, None, 256, D), lambda s, i: (s, l, 0, 0)), o_spec=pl.BlockSpec((tm, 256), lambda s, i: (i, s)),
             o_shape=(Lp, D), o_dtype=F32, dims=NT, acc_shape=None)
    tn = 512
    G["wout"] = _mm(f"bwo_{tag}", sv["yt"], dh, grid=(NSH, D // tn), a_spec=pl.BlockSpec((256, Lp), lambda s, n: (s, 0)),
                    b_spec=pl.BlockSpec((Lp, tn), lambda s, n: (0, n)),
                    o_spec=pl.BlockSpec((None, None, 256, tn), lambda s, n: (s, l, 0, n)), o_shape=None, o_dtype=F32, dims=NN,
                    acc_shape=None, into=G["wout"])
    dpa, dpb, dH, S["bm"][l] = _merge_bwd(f"bmrg_{tag}", H, sv["pa"], sv["pb"], sv["bm"], dy)
    dbranch = {}
    for key, dp, xt in (("wpa", dpa, sv["at"]), ("wpb", dpb, sv["bt"])):
        dbranch[key] = _mm(f"bx{key[2]}_{tag}", dp, W[key], grid=(Lp // tm, NSH),
                           a_spec=pl.BlockSpec((tm, 256), lambda i, s: (i, s)),
                           b_spec=pl.BlockSpec((None, None, 512, 256), lambda i, s: (s, l, 0, 0)),
                           o_spec=pl.BlockSpec((tm, 512), lambda i, s: (i, 0)), o_shape=(Lp, 512), o_dtype=F32, dims=NT,
                           acc_shape=(tm, 512), nk=NSH)
        G[key] = _mm(f"bw{key[2]}_{tag}", xt, dp, grid=(NSH,), a_spec=pl.BlockSpec((512, Lp), lambda s: (0, 0)),
                     b_spec=pl.BlockSpec((Lp, 256), lambda s: (0, s)),
                     o_spec=pl.BlockSpec((None, None, 512, 256), lambda s: (s, l, 0, 0)), o_shape=None, o_dtype=F32, dims=NN,
                     acc_shape=None, into=G[key])
    dqp, dkr, dvb = _att_bwd(f"attb_{tag}", sv["qp"], sv["kr"], sv["vb"], C["bias"], sv["lse"], dbranch["wpb"])
    dH, S["gq"][l] = _att_q_bwd(f"attq_{tag}", H, sv["gq"], C["cos"], C["sin"], C["bd"], dqp, dH)
    dH, S["gk"][l] = _att_kv_bwd(f"attk_{tag}", H, sv["gk"], C["cos"], C["sin"], C["bd"], dkr, dvb, dH)
    do, dH, S["gn"][l] = _gla_post_bwd(f"bglo_{tag}", sv["of"], sv["ob"], H, sv["gn"], dbranch["wpa"], dH)
    dqf, dkf, dgf, dvf = _gla_bwd(f"bglf_{tag}", sv["qh"], sv["kh"], sv["gf"], sv["vh"], sv["sf"], do, False, G_chunks)
    dqb, dkb, dgb, dvr = _gla_bwd(f"bglr_{tag}", sv["qh"], sv["kh"], sv["gb"], sv["vh"], sv["sb"], do, True, G_chunks)
    dH = _gla_qkv_bwd(f"bglq_{tag}", dqf, dqb, dkf, dkb, dvf, dvr, dH)
    dH, S["w2p"][l], S["b2p"][l] = _gla_gate_bwd(f"bglg_{tag}", H, sv["w2p"], sv["b2p"], dgf, dgb, dH)
    tmm = 256
    G["winp"] = _mm(f"bwi_{tag}", sv["zt"], dH, grid=(D // tmm, HP // 896), a_spec=pl.BlockSpec((tmm, Lp), lambda m, n: (m, 0)),
                    b_spec=pl.BlockSpec((Lp, 896), lambda m, n: (0, n)), o_spec=pl.BlockSpec((None, tmm, 896), lambda m, n: (l, m, n)),
                    o_shape=None, o_dtype=F32, dims=NN, acc_shape=None, into=G["winp"])
    dz = _mm(f"bdz_{tag}", dH, W["winp"], grid=(D // 512, Lp // tm), a_spec=pl.BlockSpec((tm, HP), lambda n, i: (i, 0)),
             b_spec=pl.BlockSpec((None, 512, HP), lambda n, i: (l, n, 0)), o_spec=pl.BlockSpec((tm, 512), lambda n, i: (i, n)),
             o_shape=(Lp, D), o_dtype=F32, dims=NT, acc_shape=None)
    return _rms_bwd(f"rmsb_{tag}", dz, sv["h"], sv["g"], dh)


def _local_step(x2, tgt2, W):
    n_tok = x2.shape[0]
    Lp = n_tok + LANE
    nc = Lp // CHUNK
    g_chunks = max(g for g in (1, 2, 3, 6) if nc % g == 0)
    cos, sin = _rope_tables(n_tok)
    bd = jnp.asarray(np.kron(np.eye(ATT_H, dtype=np.float32), np.full((HD, HD), 1.0 / HD, np.float32)))
    bias = jnp.where(jnp.arange(Lp) >= PAD, 0.0, -1e30).astype(F32).reshape(1, Lp)
    w2, b2 = W["w2"], W["b2"]
    w2p = jnp.zeros((DEPTH, LANE, 512), F32)
    w2p = w2p.at[:, 0:GLA_RANK, 0:256].set(w2[:, 0]).at[:, GLA_RANK:2 * GLA_RANK, 256:512].set(w2[:, 1]).astype(BF16)
    C = dict(cos=cos, sin=sin, bd=bd, bias=bias, w2p=w2p, b2p=b2.reshape(DEPTH, 1, 512),
             gq=jnp.tile(W["qn"], (1, ATT_H)).reshape(DEPTH, 1, 512), gk=jnp.tile(W["kn"], (1, ATT_KV)).reshape(DEPTH, 1, LANE))
    h = jnp.concatenate([jnp.zeros((PAD, D), F32), W["meta"], x2], axis=0)
    saved = []
    for l in range(DEPTH):
        h, s0 = _ffn_fwd(f"l{l}a", h, W, l, 0)
        h, sm = _mixer_fwd(f"l{l}m", h, W, C, l, g_chunks)
        h, s1 = _ffn_fwd(f"l{l}b", h, W, l, 1)
        saved.append((s0, sm, s1))
    dh, dfin, loss = _loss_head("loss_head", h, W["fin"].reshape(1, D), tgt2)
    G = dict(gate=lax.empty((NSH, DEPTH, 2, D, FF_S), F32), up=lax.empty((NSH, DEPTH, 2, D, FF_S), F32),
             down=lax.empty((NSH, DEPTH, 2, FF_S, D), F32), winp=lax.empty((DEPTH, D, HP), F32),
             wpa=lax.empty((NSH, DEPTH, 512, 256), F32), wpb=lax.empty((NSH, DEPTH, 512, 256), F32),
             wout=lax.empty((NSH, DEPTH, 256, D), F32))
    S = dict(bm=[None] * DEPTH, gq=[None] * DEPTH, gk=[None] * DEPTH, gn=[None] * DEPTH, w2p=[None] * DEPTH,
             b2p=[None] * DEPTH, ng=[[None] * 3 for _ in range(DEPTH)])
    for l in reversed(range(DEPTH)):
        s0, sm, s1 = saved[l]
        dh, S["ng"][l][2] = _ffn_bwd(f"l{l}b", dh, s1, W, G, l, 1)
        dh, S["ng"][l][1] = _mixer_bwd(f"l{l}m", dh, sm, W, C, G, S, l, g_chunks)
        dh, S["ng"][l][0] = _ffn_bwd(f"l{l}a", dh, s0, W, G, l, 0)
    small = dict(
        meta=dh[PAD:LANE],
        norm_gains=jnp.stack([jnp.concatenate(S["ng"][l], axis=0) for l in range(DEPTH)]),
        w2=jnp.stack([jnp.stack([S["w2p"][l][0:GLA_RANK, 0:256], S["w2p"][l][GLA_RANK:2 * GLA_RANK, 256:512]]) for l in range(DEPTH)]),
        b2=jnp.stack([S["b2p"][l].reshape(2, 256) for l in range(DEPTH)]),
        gn=jnp.concatenate(S["gn"], axis=0),
        qn=jnp.stack([S["gq"][l].reshape(ATT_H, HD).sum(0) for l in range(DEPTH)]),
        kn=jnp.stack([S["gk"][l].reshape(ATT_KV, HD).sum(0) for l in range(DEPTH)]),
        bm=jnp.stack(S["bm"]),
        fin=dfin.reshape(D),
    )
    return loss[0, 0], dh, G, small


def _win_to_padded(w):
    pad = jnp.zeros(w.shape[:-1] + (HP - D_IN,), w.dtype)
    return jnp.concatenate([w[..., 2336:4384], w[..., 0:1536], w[..., 1568:2336], w[..., 1536:1568], pad], axis=-1)


def _win_from_padded(w):
    return jnp.concatenate([w[..., 2048:3584], w[..., 4352:4384], w[..., 3584:4352], w[..., 0:2048]], axis=-1)


def _assemble(g):
    W = dict(gate=g["gate"], up=g["up"], down=g["down"], wpa=g["wpa"], wpb=g["wpb"], wout=g["wout"])
    win = jnp.transpose(g["win"], (1, 2, 0, 3)).reshape(DEPTH, D, D_IN)
    W["winp"] = _win_to_padded(win)
    sm = g["small"]
    parts = _unpack(sm, SHARDED_SMALL)
    W["meta"] = jnp.transpose(parts["meta"], (1, 0, 2)).reshape(N_META, D)
    W["norm_gains"] = jnp.transpose(parts["norm_gains"], (1, 2, 0, 3)).reshape(DEPTH, 3, D)
    W["w2"] = jnp.transpose(parts["w2"], (1, 2, 3, 0, 4)).reshape(DEPTH, 2, GLA_RANK, 256)
    W["b2"] = jnp.transpose(parts["b2"], (1, 2, 0, 3)).reshape(DEPTH, 2, 256)
    W["bm"] = jnp.transpose(parts["bm"], (1, 2, 0, 3)).reshape(DEPTH, 2, D)
    return W


SHARDED_SMALL = dict(meta=(N_META, 256), norm_gains=(DEPTH, 3, 256), w2=(DEPTH, 2, GLA_RANK, 64), b2=(DEPTH, 2, 64),
                     bm=(DEPTH, 2, 256))
FULL_SMALL = dict(meta=(N_META, D), norm_gains=(DEPTH, 3, D), w2=(DEPTH, 2, GLA_RANK, 256), b2=(DEPTH, 2, 256),
                  gn=(DEPTH, 512), qn=(DEPTH, HD), kn=(DEPTH, HD), bm=(DEPTH, 2, D), fin=(D,))


def _pack(parts, table, rows):
    flat = jnp.concatenate([parts[k].reshape(-1).astype(F32) for k in table])
    return jnp.pad(flat, (0, rows * LANE - flat.shape[0])).reshape(rows, LANE)


def _unpack(packed, table):
    lead = packed.shape[:-2]
    flat = packed.reshape(lead + (-1,))
    out, off = {}, 0
    for k, shp in table.items():
        n = int(np.prod(shp))
        out[k] = flat[..., off:off + n].reshape(lead + tuple(shp))
        off += n
    return out


def _rows_for(table, mult):
    n = sum(int(np.prod(s)) for s in table.values())
    return -(-n // (LANE * mult)) * mult


SMALL_ROWS = _rows_for(SHARDED_SMALL, 16)
GRAD_ROWS = _rows_for(dict(FULL_SMALL, loss=(1,)), 8)


def _place():
    x, y, c = lax.axis_index("x"), lax.axis_index("y"), lax.axis_index("c")
    return x, y, c


def _other_chips(x, y):
    return [(1 - x, y), (x, 1 - y), (1 - x, 1 - y)]


def _cast_place(name, w3, slot):
    A, R, Cc = w3.shape
    tr = _rt(R, 512, 16)

    def body(p_ref, w_ref, o_ref):
        o_ref[...] = w_ref[...].astype(BF16)

    return pl.pallas_call(
        body, name=name,
        grid_spec=pltpu.PrefetchScalarGridSpec(
            num_scalar_prefetch=1, grid=(A, R // tr),
            in_specs=[pl.BlockSpec((None, tr, Cc), lambda a, r, p_ref: (a, r, 0))],
            out_specs=pl.BlockSpec((None, None, tr, Cc), lambda a, r, p_ref: (p_ref[0], a, r, 0))),
        out_shape=SDS((NSH, A, R, Cc), BF16))(slot, w3)


def _gather_weights(arrs):
    n = len(arrs)

    def body(*refs):
        outs = refs[n:2 * n]
        send, recv, fsend, frecv = refs[2 * n:]
        x, y, c = _place()
        me = 2 * x + y
        sib = (x, y, 1 - c)
        chips = _other_chips(x, y)
        copies = []
        for a in range(n):
            half = outs[a].shape[1] // 2
            mine = pl.ds(c * half, half)
            for k, (px, py) in enumerate(chips):
                cp = pltpu.make_async_remote_copy(
                    src_ref=outs[a].at[me, mine], dst_ref=outs[a].at[me, mine], send_sem=send.at[a, k], recv_sem=recv.at[a, k],
                    device_id=(px, py, c), device_id_type=MESH)
                cp.start()
                copies.append(cp)
        for a in range(n):
            half = outs[a].shape[1] // 2
            mine = pl.ds(c * half, half)
            for k, (px, py) in enumerate(chips):
                src = 2 * px + py
                landed = outs[a].at[src, mine]
                pltpu.make_async_remote_copy(
                    src_ref=landed, dst_ref=landed, send_sem=send.at[a, k], recv_sem=recv.at[a, k],
                    device_id=(px, py, c), device_id_type=MESH).wait_recv()
                fw = pltpu.make_async_remote_copy(
                    src_ref=landed, dst_ref=landed, send_sem=fsend.at[a, k], recv_sem=frecv.at[a, k],
                    device_id=sib, device_id_type=MESH)
                fw.start()
                copies.append(fw)
        for a in range(n):
            half = outs[a].shape[1] // 2
            theirs = pl.ds((1 - c) * half, half)
            for k, (px, py) in enumerate(chips):
                src = 2 * px + py
                landed = outs[a].at[src, theirs]
                pltpu.make_async_remote_copy(
                    src_ref=landed, dst_ref=landed, send_sem=fsend.at[a, k], recv_sem=frecv.at[a, k],
                    device_id=sib, device_id_type=MESH).wait_recv()
        for cp in copies:
            cp.wait_send()

    any_spec = pl.BlockSpec(memory_space=pl.ANY)
    return pl.pallas_call(
        body, name="gather_weights", in_specs=[any_spec] * n, out_specs=[any_spec] * n,
        out_shape=[SDS(a.shape, a.dtype) for a in arrs], input_output_aliases={a: a for a in range(n)},
        scratch_shapes=[pltpu.SemaphoreType.DMA((n, 3)), pltpu.SemaphoreType.DMA((n, 3)), pltpu.SemaphoreType.DMA((n, 3)),
                        pltpu.SemaphoreType.DMA((n, 3))])(*arrs)


def _pair_exchange(arrs):
    n = len(arrs)

    def body(*refs):
        ins, outs = refs[:n], refs[n:2 * n]
        send, recv = refs[2 * n:]
        x, y, c = _place()
        cps = []
        for a in range(n):
            cp = pltpu.make_async_remote_copy(
                src_ref=ins[a].at[:, pl.ds(2 * (1 - c), 2)], dst_ref=outs[a], send_sem=send.at[a], recv_sem=recv.at[a],
                device_id=(x, y, 1 - c), device_id_type=MESH)
            cp.start()
            cps.append(cp)
        for cp in cps:
            cp.wait()

    any_spec = pl.BlockSpec(memory_space=pl.ANY)
    return pl.pallas_call(
        body, name="pair_exchange", in_specs=[any_spec] * n, out_specs=[any_spec] * n,
        out_shape=[SDS((NSH, 2) + a.shape[2:], a.dtype) for a in arrs],
        scratch_shapes=[pltpu.SemaphoreType.DMA((n,)), pltpu.SemaphoreType.DMA((n,))])(*arrs)


def _pair_add(name, g, p, cidx):
    _, _, R, Cc = g.shape
    tr = _rt(R, 512, 8)

    def body(c_ref, g_ref, p_ref, o_ref, ob_ref):
        v = g_ref[...] + p_ref[...]
        o_ref[...] = v
        ob_ref[...] = v.astype(BF16)

    blk = pl.BlockSpec((None, None, tr, Cc), lambda s, l, r, c_ref: (s, l, r, 0))
    return pl.pallas_call(
        body, name=name,
        grid_spec=pltpu.PrefetchScalarGridSpec(
            num_scalar_prefetch=1, grid=(NSH, 2, R // tr),
            in_specs=[pl.BlockSpec((None, None, tr, Cc), lambda s, l, r, c_ref: (s, 2 * c_ref[0] + l, r, 0)), blk],
            out_specs=[blk, blk]),
        out_shape=[SDS((NSH, 2, R, Cc), F32), SDS((NSH, 2, R, Cc), BF16)])(cidx, g, p)


def _chip_exchange(arrs):
    n = len(arrs)

    def body(*refs):
        ins, outs = refs[:n], refs[n:2 * n]
        send, recv = refs[2 * n:]
        x, y, c = _place()
        cps = []
        for a in range(n):
            for k, (px, py) in enumerate(_other_chips(x, y)):
                cp = pltpu.make_async_remote_copy(
                    src_ref=ins[a].at[2 * px + py], dst_ref=outs[a].at[k], send_sem=send.at[a, k], recv_sem=recv.at[a, k],
                    device_id=(px, py, c), device_id_type=MESH)
                cp.start()
                cps.append(cp)
        for cp in cps:
            cp.wait()

    any_spec = pl.BlockSpec(memory_space=pl.ANY)
    return pl.pallas_call(
        body, name="chip_exchange", in_specs=[any_spec] * n, out_specs=[any_spec] * n,
        out_shape=[SDS((3,) + a.shape[1:], a.dtype) for a in arrs],
        scratch_shapes=[pltpu.SemaphoreType.DMA((n, 3)), pltpu.SemaphoreType.DMA((n, 3))])(*arrs)


def _chip_add(name, hsum, q, chip, core):
    _, _, R, Cc = hsum.shape
    tr = _rt(R, 512, 8)

    def body(p_ref, c_ref, h_ref, q_ref, o_ref):
        o_ref[...] = ((h_ref[...] + q_ref[0].astype(F32)) + q_ref[1].astype(F32)) + q_ref[2].astype(F32)

    return pl.pallas_call(
        body, name=name,
        grid_spec=pltpu.PrefetchScalarGridSpec(
            num_scalar_prefetch=2, grid=(2, R // tr),
            in_specs=[pl.BlockSpec((None, None, tr, Cc), lambda l, r, p_ref, c_ref: (p_ref[0], l, r, 0)),
                      pl.BlockSpec((3, None, tr, Cc), lambda l, r, p_ref, c_ref: (0, l, r, 0))],
            out_specs=pl.BlockSpec((None, tr, Cc), lambda l, r, p_ref, c_ref: (2 * c_ref[0] + l, r, 0))),
        out_shape=SDS((DEPTH, R, Cc), F32))(chip, core, hsum, q)


def _pair_share(arrs):
    n = len(arrs)

    def body(*refs):
        outs = refs[n:2 * n]
        send, recv = refs[2 * n:]
        x, y, c = _place()
        cps = []
        for a in range(n):
            mine = outs[a].at[pl.ds(2 * c, 2)]
            cp = pltpu.make_async_remote_copy(
                src_ref=mine, dst_ref=mine, send_sem=send.at[a], recv_sem=recv.at[a],
                device_id=(x, y, 1 - c), device_id_type=MESH)
            cp.start()
            cps.append(cp)
        for a, cp in enumerate(cps):
            cp.wait_send()
            theirs = outs[a].at[pl.ds(2 * (1 - c), 2)]
            pltpu.make_async_remote_copy(
                src_ref=theirs, dst_ref=theirs, send_sem=send.at[a], recv_sem=recv.at[a],
                device_id=(x, y, 1 - c), device_id_type=MESH).wait_recv()

    any_spec = pl.BlockSpec(memory_space=pl.ANY)
    return pl.pallas_call(
        body, name="pair_share", in_specs=[any_spec] * n, out_specs=[any_spec] * n,
        out_shape=[SDS(a.shape, a.dtype) for a in arrs], input_output_aliases={a: a for a in range(n)},
        scratch_shapes=[pltpu.SemaphoreType.DMA((n,)), pltpu.SemaphoreType.DMA((n,))])(*arrs)


def _allreduce_small(v):
    rows = v.shape[0]

    def body(v_ref, o_ref, buf, send, recv):
        x, y, c = _place()
        me = 4 * x + 2 * y + c
        buf[me] = v_ref[...]
        cps = []
        k = 0
        for dx in range(2):
            for dy in range(2):
                for dc in range(2):
                    if dx + dy + dc == 0:
                        continue
                    cp = pltpu.make_async_remote_copy(
                        src_ref=v_ref, dst_ref=buf.at[me], send_sem=send.at[k], recv_sem=recv.at[k],
                        device_id=(jnp.bitwise_xor(x, dx), jnp.bitwise_xor(y, dy), jnp.bitwise_xor(c, dc)), device_id_type=MESH)
                    cp.start()
                    cps.append((cp, dx, dy, dc))
                    k += 1
        for k, (cp, dx, dy, dc) in enumerate(cps):
            cp.wait_send()
            src = 4 * jnp.bitwise_xor(x, dx) + 2 * jnp.bitwise_xor(y, dy) + jnp.bitwise_xor(c, dc)
            pltpu.make_async_remote_copy(
                src_ref=v_ref, dst_ref=buf.at[src], send_sem=send.at[k], recv_sem=recv.at[k],
                device_id=(x, y, c), device_id_type=MESH).wait_recv()
        acc = buf[0]
        for d in range(1, 8):
            acc = acc + buf[d]
        o_ref[...] = acc

    vm = pl.BlockSpec(memory_space=pltpu.VMEM)
    return pl.pallas_call(
        body, name="allreduce_small", in_specs=[vm], out_specs=vm, out_shape=SDS((rows, LANE), F32),
        scratch_shapes=[pltpu.VMEM((8, rows, LANE), F32), pltpu.SemaphoreType.DMA((7,)), pltpu.SemaphoreType.DMA((7,))])(v)


def _adamw(name, w, g, m, v):
    A, R, Cc = w.shape
    tr = _rt(R, 512, 8)

    def body(w_ref, g_ref, m_ref, v_ref, d_ref, mo_ref, vo_ref):
        gv = g_ref[...]
        mn = ADAM_B1 * m_ref[...] + (1.0 - ADAM_B1) * gv
        vn = ADAM_B2 * v_ref[...] + (1.0 - ADAM_B2) * (gv * gv)
        m_hat = mn / (1.0 - ADAM_B1 ** ADAM_STEP)
        v_hat = vn / (1.0 - ADAM_B2 ** ADAM_STEP)
        d_ref[...] = -ADAM_LR * (m_hat / (jnp.sqrt(v_hat) + ADAM_EPS) + ADAM_WD * w_ref[...])
        mo_ref[...] = mn
        vo_ref[...] = vn

    blk = pl.BlockSpec((None, tr, Cc), lambda a, r: (a, r, 0))
    return pl.pallas_call(
        body, name=name, grid=(A, R // tr), in_specs=[blk] * 4, out_specs=[blk] * 3,
        out_shape=[SDS(w.shape, F32)] * 3)(w, g, m, v)


BIG = ("gate", "up", "down", "win", "wpa", "wpb", "wout")
SMALL = ("meta", "norm_gains", "w2", "b2", "gn", "qn", "kn", "bm", "fin")


def _view3(a):
    return a.reshape(a.shape[0], -1, a.shape[-1])


def kernel(x, meta_tokens, norm_gains, ffn_w_gate, ffn_w_up, ffn_w_down, w_in, gla_w2, gla_b2, gla_gn, q_norm, k_norm, w_pa, w_pb, b_merge, w_out, final_norm, loss_target, m_meta_tokens, m_norm_gains, m_ffn_w_gate, m_ffn_w_up, m_ffn_w_down, m_w_in, m_gla_w2, m_gla_b2, m_gla_gn, m_q_norm, m_k_norm, m_w_pa, m_w_pb, m_b_merge, m_w_out, m_final_norm, v_meta_tokens, v_norm_gains, v_ffn_w_gate, v_ffn_w_up, v_ffn_w_down, v_w_in, v_gla_w2, v_gla_b2, v_gla_gn, v_q_norm, v_k_norm, v_w_pa, v_w_pb, v_b_merge, v_w_out, v_final_norm):
    big_w = dict(gate=ffn_w_gate, up=ffn_w_up, down=ffn_w_down, win=w_in, wpa=w_pa, wpb=w_pb, wout=w_out)
    big_m = dict(gate=m_ffn_w_gate, up=m_ffn_w_up, down=m_ffn_w_down, win=m_w_in, wpa=m_w_pa, wpb=m_w_pb, wout=m_w_out)
    big_v = dict(gate=v_ffn_w_gate, up=v_ffn_w_up, down=v_ffn_w_down, win=v_w_in, wpa=v_w_pa, wpb=v_w_pb, wout=v_w_out)
    small_w = dict(meta=meta_tokens, norm_gains=norm_gains, w2=gla_w2, b2=gla_b2, gn=gla_gn, qn=q_norm, kn=k_norm,
                   bm=b_merge, fin=final_norm)
    small_m = dict(meta=m_meta_tokens, norm_gains=m_norm_gains, w2=m_gla_w2, b2=m_gla_b2, gn=m_gla_gn, qn=m_q_norm,
                   kn=m_k_norm, bm=m_b_merge, fin=m_final_norm)
    small_v = dict(meta=v_meta_tokens, norm_gains=v_norm_gains, w2=v_gla_w2, b2=v_gla_b2, gn=v_gla_gn, qn=v_q_norm,
                   kn=v_k_norm, bm=v_b_merge, fin=v_final_norm)
    xi, yi, ci = _place()
    chip = (2 * xi + yi).astype(jnp.int32)

    pvec = chip.reshape(1)
    shard_pack = _pack({k: small_w[k] for k in SHARDED_SMALL}, SHARDED_SMALL, SMALL_ROWS)
    placed = [_cast_place(f"cast_{k}", _view3(big_w[k]), pvec) for k in BIG]
    placed.append(lax.dynamic_update_slice(jnp.zeros((NSH, SMALL_ROWS, LANE), F32), shard_pack[None], (chip, 0, 0)))
    gathered = _gather_weights(placed)
    g = {k: a.reshape((NSH,) + big_w[k].shape) for k, a in zip(BIG, gathered)}
    g["small"] = gathered[-1]
    W = _assemble(g)
    W.update(gn=gla_gn, qn=q_norm, kn=k_norm, fin=final_norm)

    loss, dh0, G, gs = _local_step(x[0], loss_target[0], W)
    grad_x = dh0[LANE:][None]

    dwin = _win_from_padded(G.pop("winp")).reshape(DEPTH, D, NSH, D_IN // NSH)
    G["win"] = jnp.transpose(dwin, (2, 0, 1, 3))
    loc = [G[k].reshape((NSH, DEPTH, -1, G[k].shape[-1])) for k in BIG]
    got = _pair_exchange(loc)
    cvec = ci.astype(jnp.int32).reshape(1)
    sums = [_pair_add(f"pair_add_{k}", a, p, cvec) for k, a, p in zip(BIG, loc, got)]
    arrived = _chip_exchange([s[1] for s in sums])
    mine = [_chip_add(f"chip_add_{k}", s[0], q, pvec, cvec) for k, s, q in zip(BIG, sums, arrived)]
    shared = _pair_share(mine)
    grads, deltas, new_m, new_v = {}, {}, {}, {}
    for k, r in zip(BIG, shared):
        shp = big_w[k].shape
        grads[k] = r.reshape(shp)
        d, mn, vn = _adamw(f"adamw_{k}", _view3(big_w[k]), r, _view3(big_m[k]), _view3(big_v[k]))
        deltas[k], new_m[k], new_v[k] = d.reshape(shp), mn.reshape(shp), vn.reshape(shp)

    gs["loss"] = loss.reshape(1)
    table = dict(FULL_SMALL, loss=(1,))
    tot = _unpack(_allreduce_small(_pack(gs, table, GRAD_ROWS)), table)
    loss_out = tot["loss"][0]
    sl = dict(meta=(1, 256), norm_gains=(2, 256), w2=(3, 64), b2=(2, 64), bm=(2, 256))
    for k in SMALL:
        gk = tot[k]
        if k in sl:
            ax, width = sl[k]
            gk = lax.dynamic_slice_in_dim(gk, chip * width, width, axis=ax)
        grads[k] = gk
    tbl = {k: small_w[k].shape for k in SMALL}
    rows = _rows_for(tbl, 8)
    packs = [_pack(src, tbl, rows)[None] for src in (small_w, grads, small_m, small_v)]
    d, mn, vn = _adamw("adamw_small", *packs)
    for dst, packed in ((deltas, d), (new_m, mn), (new_v, vn)):
        dst.update(_unpack(packed[0], tbl))

    order = ("meta", "norm_gains", "gate", "up", "down", "win", "w2", "b2", "gn", "qn", "kn", "wpa", "wpb", "bm", "wout", "fin")
    return (loss_out, grad_x, *[grads[k] for k in order], *[deltas[k] for k in order], *[new_m[k] for k in order],
            *[new_v[k] for k in order])
```

```python
import functools

import numpy as np
import jax
import jax.numpy as jnp
from jax import lax
from jax.experimental import pallas as pl
from jax.experimental.pallas import tpu as pltpu

F32, BF16 = jnp.float32, jnp.bfloat16
SDS = jax.ShapeDtypeStruct
HIGHEST = lax.Precision.HIGHEST
MESH = pl.DeviceIdType.MESH

D = 1024
DEPTH = 4
N_META = 16
GRID_W = 64
GLA_H, GLA_DK, GLA_DV, GLA_RANK, GLA_TAU, CHUNK = 4, 64, 128, 16, 16.0, 64
ATT_H, ATT_KV, HD = 8, 2, 64
D_FF = 2816
EPS = 1e-6
ROPE_THETA = 10000.0
ADAM_LR, ADAM_B1, ADAM_B2, ADAM_EPS, ADAM_WD, ADAM_STEP = 0.001, 0.9, 0.999, 1e-08, 0.01, 10

NSH = 4
FF_S = D_FF // NSH
FF_P = 768
LANE = 128
PAD = LANE - N_META
D_IN = 4384
C_GA, C_GB, C_QA, C_KA, C_VA, C_RA, C_QB, C_KB, C_VB, C_LR, HP = 0, 1024, 2048, 2304, 2560, 3072, 3584, 4096, 4224, 4352, 4480
VMEM_BIG = 56 * 2 ** 20


def _rt(n, cap, mult=LANE):
    best = None
    t = mult
    while t <= min(n, cap):
        if n % t == 0:
            best = t
        t += mult
    assert best is not None, (n, cap, mult)
    return best


def _cp(big=False):
    return pltpu.CompilerParams(vmem_limit_bytes=VMEM_BIG) if big else None


def _row_ids(i, tm):
    return i * tm + lax.broadcasted_iota(jnp.int32, (tm, 1), 0)


def _mm(name, a, b, *, grid, a_spec, b_spec, o_spec, o_shape, o_dtype, dims, acc_shape, nk=1, scale=None,
        res=None, res_spec=None, a_sl=None, b_sl=None, pad_w=None, into=None):
    has_res, has_into = res is not None, into is not None

    def body(*refs):
        a_ref, b_ref = refs[0], refs[1]
        p = 2
        res_ref = None
        if has_res:
            res_ref = refs[p]
            p += 1
        if has_into:
            p += 1
        o_ref = refs[p]
        acc_ref = refs[p + 1] if nk > 1 else None
        av = (a_ref[a_sl] if a_sl is not None else a_ref[...]).astype(BF16)
        bv = (b_ref[b_sl] if b_sl is not None else b_ref[...]).astype(BF16)
        prod = lax.dot_general(av, bv, dims, preferred_element_type=F32)

        def finish(v):
            if scale is not None:
                v = v * scale
            if has_res:
                v = v + res_ref[...]
            v = v.astype(o_dtype)
            if pad_w is None:
                o_ref[...] = v
            else:
                w = v.shape[-1]
                o_ref[:, :w] = v
                o_ref[:, w:] = jnp.zeros((v.shape[0], pad_w - w), o_dtype)

        if nk == 1:
            finish(prod)
        else:
            k = pl.program_id(len(grid) - 1)

            @pl.when(k == 0)
            def _():
                acc_ref[...] = prod

            @pl.when(k > 0)
            def _():
                acc_ref[...] += prod

            @pl.when(k == nk - 1)
            def _():
                finish(acc_ref[...])

    in_specs = [a_spec, b_spec]
    args = [a, b]
    if has_res:
        in_specs.append(res_spec)
        args.append(res)
    aliases = {}
    if has_into:
        aliases = {len(args): 0}
        in_specs.append(pl.BlockSpec(memory_space=pl.ANY))
        args.append(into)
        o_shape = into.shape
    return pl.pallas_call(
        body, name=name, grid=grid, in_specs=in_specs, out_specs=o_spec, out_shape=SDS(tuple(o_shape), o_dtype),
        scratch_shapes=[pltpu.VMEM(acc_shape, F32)] if nk > 1 else [], input_output_aliases=aliases,
        compiler_params=_cp(True))(*args)


NN = (((1,), (0,)), ((), ()))
NT = (((1,), (1,)), ((), ()))
TN = (((0,), (0,)), ((), ()))


def _rms_fwd(name, h, g):
    Lp = h.shape[0]
    tm = _rt(Lp, 384)

    def body(h_ref, g_ref, n_ref, nt_ref):
        x = h_ref[...]
        r = lax.rsqrt(jnp.mean(x * x, axis=-1, keepdims=True) + EPS)
        y = x * r * g_ref[...]
        n_ref[...] = y.astype(BF16)
        nt_ref[...] = y.T.astype(BF16)

    return pl.pallas_call(
        body, name=name, grid=(Lp // tm,),
        in_specs=[pl.BlockSpec((tm, D), lambda i: (i, 0)), pl.BlockSpec((1, D), lambda i: (0, 0))],
        out_specs=[pl.BlockSpec((tm, D), lambda i: (i, 0)), pl.BlockSpec((D, tm), lambda i: (0, i))],
        out_shape=[SDS((Lp, D), BF16), SDS((D, Lp), BF16)])(h, g)


def _rms_bwd(name, dn, h, g, dh):
    Lp = h.shape[0]
    tm = _rt(Lp, 384)

    def body(dn_ref, h_ref, g_ref, dh_ref, o_ref, dg_ref):
        x = h_ref[...]
        r = lax.rsqrt(jnp.mean(x * x, axis=-1, keepdims=True) + EPS)
        xh = x * r
        dn = dn_ref[...]
        u = dn * g_ref[...]
        o_ref[...] = dh_ref[...] + r * (u - xh * jnp.mean(u * xh, axis=-1, keepdims=True))

        @pl.when(pl.program_id(0) == 0)
        def _():
            dg_ref[...] = jnp.zeros_like(dg_ref)

        dg_ref[...] += jnp.sum(dn * xh, axis=0, keepdims=True)

    row = pl.BlockSpec((tm, D), lambda i: (i, 0))
    vec = pl.BlockSpec((1, D), lambda i: (0, 0))
    return pl.pallas_call(
        body, name=name, grid=(Lp // tm,), in_specs=[row, row, vec, row], out_specs=[row, vec],
        out_shape=[SDS((Lp, D), F32), SDS((1, D), F32)])(dn, h, g, dh)


def _loss_head(name, h, g, tgt):
    Lp = h.shape[0]
    tm = LANE

    def body(h_ref, g_ref, t_ref, dh_ref, dg_ref, loss_ref):
        i = pl.program_id(0)
        x = h_ref[...]
        r = lax.rsqrt(jnp.mean(x * x, axis=-1, keepdims=True) + EPS)
        xh = x * r
        gg = g_ref[...]
        err = jnp.where(i >= 1, xh * gg - t_ref[...], 0.0)
        dy = err * (1.0 / D)
        u = dy * gg
        dh_ref[...] = r * (u - xh * jnp.mean(u * xh, axis=-1, keepdims=True))

        @pl.when(i == 0)
        def _():
            dg_ref[...] = jnp.zeros_like(dg_ref)
            loss_ref[...] = jnp.zeros_like(loss_ref)

        dg_ref[...] += jnp.sum(dy * xh, axis=0, keepdims=True)
        loss_ref[...] += (0.5 / D) * jnp.sum(err * err)

    row = pl.BlockSpec((tm, D), lambda i: (i, 0))
    vec = pl.BlockSpec((1, D), lambda i: (0, 0))
    return pl.pallas_call(
        body, name=name, grid=(Lp // tm,),
        in_specs=[row, vec, pl.BlockSpec((tm, D), lambda i: (jnp.maximum(i - 1, 0), 0))],
        out_specs=[row, vec, pl.BlockSpec((8, LANE), lambda i: (0, 0))],
        out_shape=[SDS((Lp, D), F32), SDS((1, D), F32), SDS((8, LANE), F32)])(h, g, tgt)


def _swiglu_fwd(name, a, b):
    Lp = a.shape[1]
    tm = _rt(Lp, 384)

    def body(a_ref, b_ref, s_ref, st_ref):
        av = a_ref[...]
        s = av * jax.nn.sigmoid(av) * b_ref[...]
        s_ref[...] = s.astype(BF16)
        st_ref[...] = s.T.astype(BF16)

    slab = pl.BlockSpec((None, tm, FF_P), lambda j, i: (j, i, 0))
    return pl.pallas_call(
        body, name=name, grid=(NSH, Lp // tm), in_specs=[slab, slab],
        out_specs=[slab, pl.BlockSpec((None, FF_P, tm), lambda j, i: (j, 0, i))],
        out_shape=[SDS((NSH, Lp, FF_P), BF16), SDS((NSH, FF_P, Lp), BF16)])(a, b)


def _swiglu_bwd(name, a, b, ds):
    Lp = a.shape[1]
    tm = _rt(Lp, 384)

    def body(a_ref, b_ref, ds_ref, da_ref, db_ref):
        av = a_ref[...]
        sg = jax.nn.sigmoid(av)
        dsv = ds_ref[...]
        da_ref[...] = (dsv * b_ref[...] * (sg * (1.0 + av * (1.0 - sg)))).astype(BF16)
        db_ref[...] = (dsv * (av * sg)).astype(BF16)

    slab = pl.BlockSpec((None, tm, FF_P), lambda j, i: (j, i, 0))
    return pl.pallas_call(
        body, name=name, grid=(NSH, Lp // tm), in_specs=[slab, slab, slab], out_specs=[slab, slab],
        out_shape=[SDS((NSH, Lp, FF_P), BF16), SDS((NSH, Lp, FF_P), BF16)])(a, b, ds)


def _merge_fwd(name, H, pa, pb, bm):
    Lp = H.shape[0]
    tm = _rt(Lp, 384)

    def body(g_ref, pa_ref, pb_ref, bm_ref, y_ref, yt_ref):
        gv = g_ref[...]
        y = (jax.nn.sigmoid(gv[:, :D] + bm_ref[0:1, :]) * pa_ref[...]
             + jax.nn.sigmoid(gv[:, D:] + bm_ref[1:2, :]) * pb_ref[...])
        y_ref[...] = y.astype(BF16)
        yt_ref[...] = y.T.astype(BF16)

    row = pl.BlockSpec((tm, D), lambda i: (i, 0))
    return pl.pallas_call(
        body, name=name, grid=(Lp // tm,),
        in_specs=[pl.BlockSpec((tm, 2 * D), lambda i: (i, 0)), row, row, pl.BlockSpec((2, D), lambda i: (0, 0))],
        out_specs=[row, pl.BlockSpec((D, tm), lambda i: (0, i))],
        out_shape=[SDS((Lp, D), BF16), SDS((D, Lp), BF16)])(H, pa, pb, bm)


def _merge_bwd(name, H, pa, pb, bm, dy):
    Lp = H.shape[0]
    tm = _rt(Lp, 384)

    def body(g_ref, pa_ref, pb_ref, bm_ref, dy_ref, dpa_ref, dpb_ref, dh_ref, dbm_ref):
        gv = g_ref[...]
        dyv = dy_ref[...]
        sa = jax.nn.sigmoid(gv[:, :D] + bm_ref[0:1, :])
        sb = jax.nn.sigmoid(gv[:, D:] + bm_ref[1:2, :])
        dpa_ref[...] = (dyv * sa).astype(BF16)
        dpb_ref[...] = (dyv * sb).astype(BF16)
        dga = dyv * pa_ref[...] * (sa * (1.0 - sa))
        dgb = dyv * pb_ref[...] * (sb * (1.0 - sb))
        dh_ref[:, :D] = dga.astype(BF16)
        dh_ref[:, D:] = dgb.astype(BF16)

        @pl.when(pl.program_id(0) == 0)
        def _():
            dbm_ref[...] = jnp.zeros_like(dbm_ref)

        dbm_ref[0:1, :] += jnp.sum(dga, axis=0, keepdims=True)
        dbm_ref[1:2, :] += jnp.sum(dgb, axis=0, keepdims=True)

    row = pl.BlockSpec((tm, D), lambda i: (i, 0))
    two = pl.BlockSpec((2, D), lambda i: (0, 0))
    gate = pl.BlockSpec((tm, 2 * D), lambda i: (i, 0))
    return pl.pallas_call(
        body, name=name, grid=(Lp // tm,), in_specs=[gate, row, row, two, row], out_specs=[row, row, gate, two],
        out_shape=[SDS((Lp, D), BF16), SDS((Lp, D), BF16), SDS((Lp, HP), BF16), SDS((2, D), F32)])(H, pa, pb, bm, dy)


def _gla_prep(name, H, w2p, b2p):
    Lp = H.shape[0]
    tm = _rt(Lp, 384)

    def body(qk_ref, v_ref, lr_ref, w_ref, b_ref, q_o, k_o, v_o, gf_o, gb_o):
        valid = _row_ids(pl.program_id(0), tm) >= PAD
        qk = qk_ref[...]
        vv = v_ref[...]
        pre = jnp.dot(lr_ref[...].astype(BF16), w_ref[...], preferred_element_type=F32) + b_ref[...]
        g = jnp.where(valid, jax.nn.log_sigmoid(pre) * (1.0 / GLA_TAU), 0.0)
        for hh in range(GLA_H):
            q_o[hh] = qk[:, 64 * hh:64 * hh + 64] * (GLA_DK ** -0.5)
            k_o[hh] = qk[:, 256 + 64 * hh:256 + 64 * hh + 64]
            v_o[hh] = vv[:, 128 * hh:128 * hh + 128].astype(BF16)
            gf_o[hh] = g[:, 64 * hh:64 * hh + 64]
            gb_o[hh] = g[:, 256 + 64 * hh:256 + 64 * hh + 64]

    h64 = pl.BlockSpec((GLA_H, tm, 64), lambda i: (0, i, 0))
    h128 = pl.BlockSpec((GLA_H, tm, 128), lambda i: (0, i, 0))
    return pl.pallas_call(
        body, name=name, grid=(Lp // tm,),
        in_specs=[pl.BlockSpec((tm, 512), lambda i: (i, C_QA // 512)), pl.BlockSpec((tm, 512), lambda i: (i, C_VA // 512)),
                  pl.BlockSpec((tm, LANE), lambda i: (i, C_LR // LANE)), pl.BlockSpec((LANE, 512), lambda i: (0, 0)),
                  pl.BlockSpec((1, 512), lambda i: (0, 0))],
        out_specs=[h64, h64, h128, h64, h64],
        out_shape=[SDS((GLA_H, Lp, 64), F32), SDS((GLA_H, Lp, 64), F32), SDS((GLA_H, Lp, 128), BF16),
                   SDS((GLA_H, Lp, 64), F32), SDS((GLA_H, Lp, 64), F32)])(H, H, H, w2p, b2p)


def _bdot(a, b, ca, cb, precision=None):
    return lax.dot_general(a, b, ((ca, cb), ((0,), (0,))), precision=precision, preferred_element_type=F32)


def _gla_chunk_terms(q_ref, k_ref, g_ref, v_ref, G, rev):
    B = GLA_H * G
    qv = q_ref[...].reshape(B, CHUNK, GLA_DK)
    kv = k_ref[...].reshape(B, CHUNK, GLA_DK)
    gv = g_ref[...].reshape(B, CHUNK, GLA_DK)
    vv = v_ref[...].reshape(B, CHUNK, GLA_DV)
    ii = lax.broadcasted_iota(jnp.int32, (CHUNK, CHUNK), 0)
    jj = lax.broadcasted_iota(jnp.int32, (CHUNK, CHUNK), 1)
    tri = (jj >= ii) if rev else (jj <= ii)
    tb = jnp.broadcast_to(tri.astype(F32)[None], (B, CHUNK, CHUNK))
    bc = _bdot(tb, gv, (2,), (1,), HIGHEST)
    bt = bc[:, 0:1, :] if rev else bc[:, CHUNK - 1:CHUNK, :]
    eq, eki, eke = jnp.exp(bc), jnp.exp(-bc), jnp.exp(bt - bc)
    qd, ki, ke = qv * eq, kv * eki, kv * eke
    att = jnp.where(tri[None], _bdot(qd.astype(BF16), ki.astype(BF16), (2,), (2,)), 0.0)
    dm = jnp.exp(_bdot(gv, jnp.ones((B, CHUNK, GLA_DV), F32), (1,), (1,), HIGHEST))
    return dict(B=B, vv=vv, tri=tri, tb=tb, bt=bt, eq=eq, eki=eki, eke=eke, qd=qd, ki=ki, ke=ke, att=att, dm=dm)


def _gla_fwd(name, q, k, g, v, rev, G):
    Lp = q.shape[1]
    tg = G * CHUNK
    ng = Lp // tg

    def body(q_ref, k_ref, g_ref, v_ref, o_ref, ss_ref, s_scr):
        @pl.when(pl.program_id(0) == 0)
        def _():
            s_scr[...] = jnp.zeros_like(s_scr)

        t = _gla_chunk_terms(q_ref, k_ref, g_ref, v_ref, G, rev)
        B, vv = t["B"], t["vv"]
        qd = t["qd"].astype(BF16)
        oi = _bdot(t["att"].astype(BF16), vv, (2,), (1,))
        kvc = _bdot(t["ke"].astype(BF16), vv, (1,), (1,)).reshape(GLA_H, G, GLA_DK, GLA_DV)
        dm = t["dm"].reshape(GLA_H, G, GLA_DK, GLA_DV)
        s = s_scr[...]
        sp = [None] * G
        for c in (range(G - 1, -1, -1) if rev else range(G)):
            sp[c] = s
            ss_ref[c] = s
            s = dm[:, c] * s + kvc[:, c]
        s_scr[...] = s
        spb = jnp.stack(sp, axis=1).reshape(B, GLA_DK, GLA_DV).astype(BF16)
        o_ref[...] = (oi + _bdot(qd, spb, (2,), (1,))).reshape(GLA_H, tg, GLA_DV)

    blk = (lambda i: (0, ng - 1 - i, 0)) if rev else (lambda i: (0, i, 0))
    sblk = (lambda i: (ng - 1 - i, 0, 0, 0)) if rev else (lambda i: (i, 0, 0, 0))
    h64 = pl.BlockSpec((GLA_H, tg, 64), blk)
    h128 = pl.BlockSpec((GLA_H, tg, 128), blk)
    return pl.pallas_call(
        body, name=name, grid=(ng,), in_specs=[h64, h64, h64, h128],
        out_specs=[h128, pl.BlockSpec((G, GLA_H, GLA_DK, GLA_DV), sblk)],
        out_shape=[SDS((GLA_H, Lp, GLA_DV), F32), SDS((Lp // CHUNK, GLA_H, GLA_DK, GLA_DV), F32)],
        scratch_shapes=[pltpu.VMEM((GLA_H, GLA_DK, GLA_DV), F32)], compiler_params=_cp(True))(q, k, g, v)


def _gla_bwd(name, q, k, g, v, ss, do, rev, G):
    Lp = q.shape[1]
    tg = G * CHUNK
    ng = Lp // tg

    def body(q_ref, k_ref, g_ref, v_ref, ss_ref, do_ref, dq_ref, dk_ref, dg_ref, dv_ref, ds_scr):
        @pl.when(pl.program_id(0) == 0)
        def _():
            ds_scr[...] = jnp.zeros_like(ds_scr)

        t = _gla_chunk_terms(q_ref, k_ref, g_ref, v_ref, G, rev)
        B, vv, tri = t["B"], t["vv"], t["tri"]
        qd, ki, ke = t["qd"], t["ki"], t["ke"]
        qdb, kib, keb = qd.astype(BF16), ki.astype(BF16), ke.astype(BF16)
        sp = jnp.stack([ss_ref[c] for c in range(G)], axis=1).reshape(B, GLA_DK, GLA_DV)
        dob = do_ref[...].reshape(B, CHUNK, GLA_DV).astype(BF16)
        da = jnp.where(tri[None], _bdot(dob, vv, (2,), (2,)), 0.0).astype(BF16)
        dqd = _bdot(da, kib, (2,), (1,)) + _bdot(dob, sp.astype(BF16), (2,), (2,))
        dki = _bdot(da, qdb, (1,), (1,))
        dv = _bdot(t["att"].astype(BF16), dob, (1,), (1,))
        cc = _bdot(qdb, dob, (1,), (1,)).reshape(GLA_H, G, GLA_DK, GLA_DV)
        dm = t["dm"].reshape(GLA_H, G, GLA_DK, GLA_DV)
        dsc = ds_scr[...]
        dsn = [None] * G
        for c in (range(G) if rev else range(G - 1, -1, -1)):
            dsn[c] = dsc
            dsc = dm[:, c] * dsc + cc[:, c]
        ds_scr[...] = dsc
        dsn = jnp.stack(dsn, axis=1).reshape(B, GLA_DK, GLA_DV)
        dsnb = dsn.astype(BF16)
        dv = dv + _bdot(keb, dsnb, (2,), (1,))
        dke = _bdot(vv, dsnb, (2,), (2,))
        ddrow = _bdot(jnp.ones((B, CHUNK, GLA_DV), F32), dsn * sp, (2,), (2,), HIGHEST)
        dbt = ddrow * jnp.exp(t["bt"]) + jnp.sum(dke * ke, axis=1, keepdims=True)
        db = dqd * qd - dki * ki - dke * ke
        dq_ref[...] = (dqd * t["eq"]).reshape(GLA_H, tg, GLA_DK)
        dk_ref[...] = (dki * t["eki"] + dke * t["eke"]).reshape(GLA_H, tg, GLA_DK)
        dg_ref[...] = (_bdot(t["tb"], db, (1,), (1,), HIGHEST) + dbt).reshape(GLA_H, tg, GLA_DK)
        dv_ref[...] = dv.reshape(GLA_H, tg, GLA_DV)

    blk = (lambda i: (0, i, 0)) if rev else (lambda i: (0, ng - 1 - i, 0))
    sblk = (lambda i: (i, 0, 0, 0)) if rev else (lambda i: (ng - 1 - i, 0, 0, 0))
    h64 = pl.BlockSpec((GLA_H, tg, 64), blk)
    h128 = pl.BlockSpec((GLA_H, tg, 128), blk)
    return pl.pallas_call(
        body, name=name, grid=(ng,),
        in_specs=[h64, h64, h64, h128, pl.BlockSpec((G, GLA_H, GLA_DK, GLA_DV), sblk), h128],
        out_specs=[h64, h64, h64, h128],
        out_shape=[SDS((GLA_H, Lp, 64), F32), SDS((GLA_H, Lp, 64), F32), SDS((GLA_H, Lp, 64), F32),
                   SDS((GLA_H, Lp, GLA_DV), F32)],
        scratch_shapes=[pltpu.VMEM((GLA_H, GLA_DK, GLA_DV), F32)], compiler_params=_cp(True))(q, k, g, v, ss, do)


def _gla_post(name, of, ob, H, gn):
    Lp = H.shape[0]
    tm = _rt(Lp, 384)

    def body(of_ref, ob_ref, r_ref, gn_ref, a_ref, at_ref):
        parts = []
        for hh in range(GLA_H):
            o = of_ref[hh] + ob_ref[hh]
            parts.append(o * lax.rsqrt(jnp.mean(o * o, axis=-1, keepdims=True) + EPS))
        rv = r_ref[...]
        a = (jnp.concatenate(parts, axis=1) * gn_ref[...]) * (rv * jax.nn.sigmoid(rv))
        a_ref[...] = a.astype(BF16)
        at_ref[...] = a.T.astype(BF16)

    h128 = pl.BlockSpec((GLA_H, tm, 128), lambda i: (0, i, 0))
    return pl.pallas_call(
        body, name=name, grid=(Lp // tm,),
        in_specs=[h128, h128, pl.BlockSpec((tm, 512), lambda i: (i, C_RA // 512)), pl.BlockSpec((1, 512), lambda i: (0, 0))],
        out_specs=[pl.BlockSpec((tm, 512), lambda i: (i, 0)), pl.BlockSpec((512, tm), lambda i: (0, i))],
        out_shape=[SDS((Lp, 512), BF16), SDS((512, Lp), BF16)])(of, ob, H, gn)


def _gla_post_bwd(name, of, ob, H, gn, da, dH):
    Lp = H.shape[0]
    tm = _rt(Lp, 384)

    def body(of_ref, ob_ref, r_ref, gn_ref, da_ref, dh_in, do_ref, dh_ref, dgn_ref):
        rv = r_ref[...]
        sg = jax.nn.sigmoid(rv)
        dav = da_ref[...]
        gnv = gn_ref[...]
        ons, rs = [], []
        for hh in range(GLA_H):
            o = of_ref[hh] + ob_ref[hh]
            r = lax.rsqrt(jnp.mean(o * o, axis=-1, keepdims=True) + EPS)
            rs.append(r)
            ons.append(o * r)
        on = jnp.concatenate(ons, axis=1)
        dw = dav * (rv * sg)
        dh_ref[...] = (dav * (on * gnv) * (sg * (1.0 + rv * (1.0 - sg)))).astype(BF16)

        @pl.when(pl.program_id(0) == 0)
        def _():
            dgn_ref[...] = jnp.zeros_like(dgn_ref)

        dgn_ref[...] += jnp.sum(dw * on, axis=0, keepdims=True)
        don = dw * gnv
        for hh in range(GLA_H):
            dd = don[:, 128 * hh:128 * hh + 128]
            do_ref[hh] = rs[hh] * (dd - ons[hh] * jnp.mean(dd * ons[hh], axis=-1, keepdims=True))

    h128 = pl.BlockSpec((GLA_H, tm, 128), lambda i: (0, i, 0))
    rblk = pl.BlockSpec((tm, 512), lambda i: (i, C_RA // 512))
    vec = pl.BlockSpec((1, 512), lambda i: (0, 0))
    return pl.pallas_call(
        body, name=name, grid=(Lp // tm,),
        in_specs=[h128, h128, rblk, vec, pl.BlockSpec((tm, 512), lambda i: (i, 0)), pl.BlockSpec(memory_space=pl.ANY)],
        out_specs=[h128, rblk, vec],
        out_shape=[SDS((GLA_H, Lp, 128), F32), SDS(dH.shape, BF16), SDS((1, 512), F32)],
        input_output_aliases={5: 1})(of, ob, H, gn, da, dH)


def _gla_qkv_bwd(name, dqf, dqb, dkf, dkb, dvf, dvb, dH):
    Lp = dqf.shape[1]
    tm = _rt(Lp, 384)

    def body(dqf_ref, dqb_ref, dkf_ref, dkb_ref, dvf_ref, dvb_ref, dh_in, dh_ref):
        valid = _row_ids(pl.program_id(0), tm) >= PAD
        for hh in range(GLA_H):
            dq = (dqf_ref[hh] + dqb_ref[hh]) * (GLA_DK ** -0.5)
            dh_ref[:, 64 * hh:64 * hh + 64] = jnp.where(valid, dq, 0.0).astype(BF16)
            dh_ref[:, 256 + 64 * hh:256 + 64 * hh + 64] = jnp.where(valid, dkf_ref[hh] + dkb_ref[hh], 0.0).astype(BF16)
            dh_ref[:, 512 + 128 * hh:512 + 128 * hh + 128] = jnp.where(valid, dvf_ref[hh] + dvb_ref[hh], 0.0).astype(BF16)

    h64 = pl.BlockSpec((GLA_H, tm, 64), lambda i: (0, i, 0))
    h128 = pl.BlockSpec((GLA_H, tm, 128), lambda i: (0, i, 0))
    return pl.pallas_call(
        body, name=name, grid=(Lp // tm,),
        in_specs=[h64, h64, h64, h64, h128, h128, pl.BlockSpec(memory_space=pl.ANY)],
        out_specs=pl.BlockSpec((tm, 1024), lambda i: (i, C_QA // 1024)), out_shape=SDS(dH.shape, BF16),
        input_output_aliases={6: 0})(dqf, dqb, dkf, dkb, dvf, dvb, dH)


def _gla_gate_bwd(name, H, w2p, b2p, dgf, dgb, dH):
    Lp = H.shape[0]
    tm = _rt(Lp, 384)

    def body(lr_ref, w_ref, b_ref, dgf_ref, dgb_ref, dh_in, dh_ref, dw_ref, db_ref, dg_scr):
        valid = _row_ids(pl.program_id(0), tm) >= PAD
        for hh in range(GLA_H):
            dg_scr[:, 64 * hh:64 * hh + 64] = dgf_ref[hh]
            dg_scr[:, 256 + 64 * hh:256 + 64 * hh + 64] = dgb_ref[hh]
        lrb = lr_ref[...].astype(BF16)
        wv = w_ref[...]
        pre = jnp.dot(lrb, wv, preferred_element_type=F32) + b_ref[...]
        dpre = jnp.where(valid, dg_scr[...] * (1.0 / GLA_TAU) * jax.nn.sigmoid(-pre), 0.0)
        dpb = dpre.astype(BF16)
        dh_ref[...] = lax.dot_general(dpb, wv, NT, preferred_element_type=F32).astype(BF16)

        @pl.when(pl.program_id(0) == 0)
        def _():
            dw_ref[...] = jnp.zeros_like(dw_ref)
            db_ref[...] = jnp.zeros_like(db_ref)

        dw_ref[...] += lax.dot_general(lrb, dpb, TN, preferred_element_type=F32)
        db_ref[...] += jnp.sum(dpre, axis=0, keepdims=True)

    h64 = pl.BlockSpec((GLA_H, tm, 64), lambda i: (0, i, 0))
    lrblk = pl.BlockSpec((tm, LANE), lambda i: (i, C_LR // LANE))
    wblk = pl.BlockSpec((LANE, 512), lambda i: (0, 0))
    vec = pl.BlockSpec((1, 512), lambda i: (0, 0))
    return pl.pallas_call(
        body, name=name, grid=(Lp // tm,),
        in_specs=[lrblk, wblk, vec, h64, h64, pl.BlockSpec(memory_space=pl.ANY)],
        out_specs=[lrblk, wblk, vec],
        out_shape=[SDS(dH.shape, BF16), SDS((LANE, 512), F32), SDS((1, 512), F32)],
        scratch_shapes=[pltpu.VMEM((tm, 512), F32)], input_output_aliases={5: 0})(H, w2p, b2p, dgf, dgb, dH)


def _swap16(x):
    n = x.shape[1]
    lane = lax.broadcasted_iota(jnp.int32, x.shape, 1)
    return jnp.where(lane % 32 < 16, pltpu.roll(x, n - 16, 1), pltpu.roll(x, 16, 1))


def _headnorm_rope(x, gain, cos, sin, bd):
    r = lax.rsqrt(jnp.dot(x * x, bd, precision=HIGHEST, preferred_element_type=F32) + EPS)
    xh = x * r
    xn = xh * gain
    return xn * cos + _swap16(xn) * sin, xh, r


def _headnorm_rope_bwd(dxr, xh, r, gain, cos, sin, bd):
    dxn = cos * dxr + _swap16(sin * dxr)
    u = dxn * gain
    dx = r * (u - xh * jnp.dot(u * xh, bd, precision=HIGHEST, preferred_element_type=F32))
    return dx, jnp.sum(dxn * xh, axis=0, keepdims=True)


def _att_prep(name, H, gq, gk, cos, sin, bd):
    Lp = H.shape[0]
    tm = _rt(Lp, 384)

    def body(q_ref, kv_ref, gq_ref, gk_ref, c_ref, s_ref, bd_ref, qp_ref, k_ref, v_ref):
        c1, s1 = c_ref[...], s_ref[...]
        c4, s4 = jnp.concatenate([c1] * 4, axis=1), jnp.concatenate([s1] * 4, axis=1)
        xr, _, _ = _headnorm_rope(q_ref[...], gq_ref[...], c4, s4, bd_ref[...])
        xr = xr * (HD ** -0.5)
        lane = lax.broadcasted_iota(jnp.int32, (tm, LANE), 1)
        for hh in range(ATT_H):
            grp = xr[:, LANE * (hh // 2):LANE * (hh // 2) + LANE]
            e, gi = hh % 2, hh // 4
            if e != gi:
                grp = pltpu.roll(grp, 64, 1)
            keep = (lane < 64) if gi == 0 else (lane >= 64)
            qp_ref[hh] = jnp.where(keep, grp, 0.0).astype(BF16)
        kv = kv_ref[...]
        kr, _, _ = _headnorm_rope(kv[:, :LANE], gk_ref[...], c1, s1, bd_ref[0:LANE, 0:LANE])
        k_ref[...] = kr.astype(BF16)
        v_ref[...] = kv[:, LANE:].astype(BF16)

    row128 = pl.BlockSpec((tm, LANE), lambda i: (i, 0))
    return pl.pallas_call(
        body, name=name, grid=(Lp // tm,),
        in_specs=[pl.BlockSpec((tm, 512), lambda i: (i, C_QB // 512)), pl.BlockSpec((tm, 256), lambda i: (i, C_KB // 256)),
                  pl.BlockSpec((1, 512), lambda i: (0, 0)), pl.BlockSpec((1, LANE), lambda i: (0, 0)), row128, row128,
                  pl.BlockSpec((512, 512), lambda i: (0, 0))],
        out_specs=[pl.BlockSpec((ATT_H, tm, LANE), lambda i: (0, i, 0)), row128, row128],
        out_shape=[SDS((ATT_H, Lp, LANE), BF16), SDS((Lp, LANE), BF16), SDS((Lp, LANE), BF16)])(H, H, gq, gk, cos, sin, bd)


def _att_q_bwd(name, H, gq, cos, sin, bd, dqp, dH):
    Lp = H.shape[0]
    tm = _rt(Lp, 384)

    def body(q_ref, gq_ref, c_ref, s_ref, bd_ref, dqp_ref, dh_in, dh_ref, dg_ref):
        c1, s1 = c_ref[...], s_ref[...]
        c4, s4 = jnp.concatenate([c1] * 4, axis=1), jnp.concatenate([s1] * 4, axis=1)
        gqv = gq_ref[...]
        _, xh, r = _headnorm_rope(q_ref[...], gqv, c4, s4, bd_ref[...])
        lane = lax.broadcasted_iota(jnp.int32, (tm, LANE), 1)
        groups = []
        for j in range(ATT_H // 2):
            pieces = []
            for e in range(2):
                hh = 2 * j + e
                piece = dqp_ref[hh]
                if e != hh // 4:
                    piece = pltpu.roll(piece, 64, 1)
                pieces.append(piece)
            groups.append(jnp.where(lane < 64, pieces[0], pieces[1]))
        dxr = jnp.concatenate(groups, axis=1) * (HD ** -0.5)
        dx, dg = _headnorm_rope_bwd(dxr, xh, r, gqv, c4, s4, bd_ref[...])
        dh_ref[...] = dx.astype(BF16)

        @pl.when(pl.program_id(0) == 0)
        def _():
            dg_ref[...] = jnp.zeros_like(dg_ref)

        dg_ref[...] += dg

    row128 = pl.BlockSpec((tm, LANE), lambda i: (i, 0))
    qblk = pl.BlockSpec((tm, 512), lambda i: (i, C_QB // 512))
    vec = pl.BlockSpec((1, 512), lambda i: (0, 0))
    return pl.pallas_call(
        body, name=name, grid=(Lp // tm,),
        in_specs=[qblk, vec, row128, row128, pl.BlockSpec((512, 512), lambda i: (0, 0)),
                  pl.BlockSpec((ATT_H, tm, LANE), lambda i: (0, i, 0)), pl.BlockSpec(memory_space=pl.ANY)],
        out_specs=[qblk, vec], out_shape=[SDS(dH.shape, BF16), SDS((1, 512), F32)],
        input_output_aliases={6: 0})(H, gq, cos, sin, bd, dqp, dH)


def _att_kv_bwd(name, H, gk, cos, sin, bd, dkr, dvb, dH):
    Lp = H.shape[0]
    tm = _rt(Lp, 384)

    def body(kv_ref, gk_ref, c_ref, s_ref, bd_ref, dk_ref, dv_ref, dh_in, dh_ref, dg_ref):
        c1, s1 = c_ref[...], s_ref[...]
        gkv = gk_ref[...]
        bdv = bd_ref[0:LANE, 0:LANE]
        _, xh, r = _headnorm_rope(kv_ref[:, :LANE], gkv, c1, s1, bdv)
        dx, dg = _headnorm_rope_bwd(dk_ref[...], xh, r, gkv, c1, s1, bdv)
        dh_ref[:, :LANE] = dx.astype(BF16)
        dh_ref[:, LANE:] = dv_ref[...].astype(BF16)

        @pl.when(pl.program_id(0) == 0)
        def _():
            dg_ref[...] = jnp.zeros_like(dg_ref)

        dg_ref[...] += dg

    row128 = pl.BlockSpec((tm, LANE), lambda i: (i, 0))
    kvblk = pl.BlockSpec((tm, 256), lambda i: (i, C_KB // 256))
    vec = pl.BlockSpec((1, LANE), lambda i: (0, 0))
    return pl.pallas_call(
        body, name=name, grid=(Lp // tm,),
        in_specs=[kvblk, vec, row128, row128, pl.BlockSpec((512, 512), lambda i: (0, 0)), row128, row128,
                  pl.BlockSpec(memory_space=pl.ANY)],
        out_specs=[kvblk, vec], out_shape=[SDS(dH.shape, BF16), SDS((1, LANE), F32)],
        input_output_aliases={7: 0})(H, gk, cos, sin, bd, dkr, dvb, dH)


def _att_fwd(name, qp, kr, vb, bias):
    Lp = kr.shape[0]
    tq = _rt(Lp, 384)

    def body(q_ref, k_ref, v_ref, bias_ref, b_ref, bt_ref, lse_ref):
        j, i = pl.program_id(0), pl.program_id(1)
        valid = _row_ids(i, tq) >= PAD
        kk, vv, bb = k_ref[...], v_ref[...], bias_ref[...]
        outs = []
        for e in range(2):
            s = lax.dot_general(q_ref[e], kk, NT, preferred_element_type=F32) + bb
            m = jnp.max(s, axis=-1, keepdims=True)
            p = jnp.exp(s - m)
            l = jnp.sum(p, axis=-1, keepdims=True)
            p = p * (1.0 / l)
            o = jnp.dot(p.astype(BF16), vv, preferred_element_type=F32)
            outs.append(jnp.where(valid, o, 0.0))
            lse_ref[e] = m + jnp.log(l)
        lane = lax.broadcasted_iota(jnp.int32, (tq, LANE), 1)
        low = j < 2
        o0 = jnp.where(low, outs[0], pltpu.roll(outs[0], 64, 1))
        o1 = jnp.where(low, pltpu.roll(outs[1], 64, 1), outs[1])
        blk = jnp.where(lane < 64, o0, o1)
        b_ref[...] = blk.astype(BF16)
        bt_ref[...] = blk.T.astype(BF16)

    full = pl.BlockSpec((Lp, LANE), lambda j, i: (0, 0))
    return pl.pallas_call(
        body, name=name, grid=(ATT_H // 2, Lp // tq),
        in_specs=[pl.BlockSpec((2, tq, LANE), lambda j, i: (j, i, 0)), full, full, pl.BlockSpec((1, Lp), lambda j, i: (0, 0))],
        out_specs=[pl.BlockSpec((tq, LANE), lambda j, i: (i, j)), pl.BlockSpec((LANE, tq), lambda j, i: (j, i)),
                   pl.BlockSpec((2, tq, 1), lambda j, i: (j, i, 0))],
        out_shape=[SDS((Lp, 512), BF16), SDS((512, Lp), BF16), SDS((ATT_H, Lp, 1), F32)],
        compiler_params=_cp(True))(qp, kr, vb, bias)


def _att_bwd(name, qp, kr, vb, bias, lse, db):
    Lp = kr.shape[0]
    tq = _rt(Lp, 384)
    nq = Lp // tq

    def body(q_ref, k_ref, v_ref, bias_ref, lse_ref, db_ref, dq_ref, dk_ref, dv_ref, dkt_scr, dvt_scr):
        j, i = pl.program_id(0), pl.program_id(1)

        @pl.when((j == 0) & (i == 0))
        def _():
            dkt_scr[...] = jnp.zeros_like(dkt_scr)
            dvt_scr[...] = jnp.zeros_like(dvt_scr)

        kk, vv, bb = k_ref[...], v_ref[...], bias_ref[...]
        dbv = db_ref[...]
        rolled = pltpu.roll(dbv, 64, 1)
        lane = lax.broadcasted_iota(jnp.int32, (tq, LANE), 1)
        low = j < 2
        first = jnp.where(low, 0, 64)
        keep = (lane >= first) & (lane < first + 64)
        for e in range(2):
            qe = q_ref[e]
            s = lax.dot_general(qe, kk, NT, preferred_element_type=F32) + bb
            p = jnp.exp(s - lse_ref[e])
            src = jnp.where(low, dbv, rolled) if e == 0 else jnp.where(low, rolled, dbv)
            dop = jnp.where(keep, src, 0.0).astype(BF16)
            dp = lax.dot_general(dop, vv, NT, preferred_element_type=F32)
            ds = (p * (dp - jnp.sum(p * dp, axis=-1, keepdims=True))).astype(BF16)
            dq_ref[e] = jnp.where(keep, jnp.dot(ds, kk, preferred_element_type=F32), 0.0)
            dkt_scr[...] += lax.dot_general(qe, ds, TN, preferred_element_type=F32)
            dvt_scr[...] += lax.dot_general(dop, p.astype(BF16), TN, preferred_element_type=F32)

        @pl.when((j == ATT_H // 2 - 1) & (i == nq - 1))
        def _():
            dk_ref[...] = dkt_scr[...].T
            dv_ref[...] = dvt_scr[...].T

    full = pl.BlockSpec((Lp, LANE), lambda j, i: (0, 0))
    pair = pl.BlockSpec((2, tq, LANE), lambda j, i: (j, i, 0))
    return pl.pallas_call(
        body, name=name, grid=(ATT_H // 2, nq),
        in_specs=[pair, full, full, pl.BlockSpec((1, Lp), lambda j, i: (0, 0)), pl.BlockSpec((2, tq, 1), lambda j, i: (j, i, 0)),
                  pl.BlockSpec((tq, LANE), lambda j, i: (i, j))],
        out_specs=[pair, full, full],
        out_shape=[SDS((ATT_H, Lp, LANE), F32), SDS((Lp, LANE), F32), SDS((Lp, LANE), F32)],
        scratch_shapes=[pltpu.VMEM((LANE, Lp), F32), pltpu.VMEM((LANE, Lp), F32)], compiler_params=_cp(True))(
            qp, kr, vb, bias, lse, db)


def _rope_tables(n_tok):
    rows = n_tok // GRID_W
    row = jnp.repeat(jnp.arange(rows), GRID_W).astype(F32)
    col = jnp.tile(jnp.arange(GRID_W), rows).astype(F32)
    axis_dim = HD // 2
    inv = ROPE_THETA ** (-jnp.arange(0, axis_dim, 2, dtype=F32) / axis_dim)
    ang = jnp.concatenate([row[:, None] * inv, col[:, None] * inv], axis=-1)
    ang = jnp.concatenate([jnp.zeros((LANE, axis_dim), F32), ang], axis=0)
    c, s = jnp.cos(ang), jnp.sin(ang)
    c64 = jnp.concatenate([c[:, :16], c[:, :16], c[:, 16:], c[:, 16:]], axis=1)
    s64 = jnp.concatenate([-s[:, :16], s[:, :16], -s[:, 16:], s[:, 16:]], axis=1)
    return jnp.concatenate([c64, c64], axis=1), jnp.concatenate([s64, s64], axis=1)


def _ffn_fwd(tag, h, W, l, j):
    Lp = h.shape[0]
    tm = _rt(Lp, 384)
    ni = Lp // tm
    g = W["norm_gains"][l, 2 * j].reshape(1, D)
    last = NSH - 1

    def body(h_ref, g_ref, wg_ref, wu_ref, wd_ref, h2_ref, a_ref, b_ref, st_ref, nt_ref, n_scr, acc_scr, pad_scr):
        s, i = pl.program_id(0), pl.program_id(1)
        rows = pl.ds(pl.multiple_of(i * tm, tm), tm)

        @pl.when(s == 0)
        def _():
            x = h_ref[...]
            y = x * lax.rsqrt(jnp.mean(x * x, axis=-1, keepdims=True) + EPS) * g_ref[...]
            n_scr[rows, :] = y.astype(BF16)
            nt_ref[...] = y.T.astype(BF16)

        nv = n_scr[rows, :]
        a = jnp.dot(nv, wg_ref[...], preferred_element_type=F32)
        b = jnp.dot(nv, wu_ref[...], preferred_element_type=F32)
        sv = a * jax.nn.sigmoid(a) * b
        a_ref[:, :FF_S] = a.astype(BF16)
        a_ref[:, FF_S:] = jnp.zeros((tm, FF_P - FF_S), BF16)
        b_ref[:, :FF_S] = b.astype(BF16)
        b_ref[:, FF_S:] = jnp.zeros((tm, FF_P - FF_S), BF16)
        pad_scr[:, :FF_S] = sv
        pad_scr[:, FF_S:] = jnp.zeros((tm, FF_P - FF_S), F32)
        st_ref[...] = pad_scr[...].T.astype(BF16)
        part = jnp.dot(sv.astype(BF16), wd_ref[...], preferred_element_type=F32)

        @pl.when(s == 0)
        def _():
            acc_scr[rows, :] = part

        @pl.when(s > 0)
        def _():
            acc_scr[rows, :] += part

        @pl.when(s == last)
        def _():
            h2_ref[...] = h_ref[...] + 0.5 * acc_scr[rows, :]

    slab = pl.BlockSpec((None, tm, FF_P), lambda s, i: (s, i, 0))
    wup = pl.BlockSpec((None, None, None, D, FF_S), lambda s, i: (s, l, j, 0, 0))
    h2, a, b, st, nt = pl.pallas_call(
        body, name=f"ffn_{tag}", grid=(NSH, ni),
        in_specs=[pl.BlockSpec((tm, D), lambda s, i: (jnp.where((s == 0) | (s == last), i, ni - 1), 0)),
                  pl.BlockSpec((1, D), lambda s, i: (0, 0)), wup, wup,
                  pl.BlockSpec((None, None, None, FF_S, D), lambda s, i: (s, l, j, 0, 0))],
        out_specs=[pl.BlockSpec((tm, D), lambda s, i: (jnp.where(s == last, i, 0), 0)), slab, slab,
                   pl.BlockSpec((None, FF_P, tm), lambda s, i: (s, 0, i)),
                   pl.BlockSpec((D, tm), lambda s, i: (0, jnp.where(s == 0, i, ni - 1)))],
        out_shape=[SDS((Lp, D), F32), SDS((NSH, Lp, FF_P), BF16), SDS((NSH, Lp, FF_P), BF16), SDS((NSH, FF_P, Lp), BF16),
                   SDS((D, Lp), BF16)],
        scratch_shapes=[pltpu.VMEM((Lp, D), BF16), pltpu.VMEM((Lp, D), F32), pltpu.VMEM((tm, FF_P), F32)],
        compiler_params=_cp(True))(h, g, W["gate"], W["up"], W["down"])
    return h2, dict(h=h, g=g, nt=nt, a=a, b=b, st=st)


def _ffn_bwd(tag, dh, sv, W, G, l, j):
    Lp = dh.shape[0]
    tn = 512
    da, db, dh_prev, dg = _ffn_bwd_acts(tag, dh, sv, W, l, j)
    G["down"] = _mm(f"bwd_{tag}", sv["st"], dh, grid=(NSH, D // tn),
                    a_spec=pl.BlockSpec((None, FF_P, Lp), lambda s, n: (s, 0, 0)), a_sl=(slice(0, FF_S), slice(None)),
                    b_spec=pl.BlockSpec((Lp, tn), lambda s, n: (0, n)),
                    o_spec=pl.BlockSpec((None, None, None, FF_S, tn), lambda s, n: (s, l, j, 0, n)), o_shape=None, o_dtype=F32,
                    dims=NN, acc_shape=None, scale=0.5, into=G["down"])
    for key, dact in (("gate", da), ("up", db)):
        G[key] = _mm(f"bw{key[0]}_{tag}", sv["nt"], dact, grid=(NSH, D // tn),
                     a_spec=pl.BlockSpec((tn, Lp), lambda s, m: (m, 0)),
                     b_spec=pl.BlockSpec((None, Lp, FF_P), lambda s, m: (s, 0, 0)), b_sl=(slice(None), slice(0, FF_S)),
                     o_spec=pl.BlockSpec((None, None, None, tn, FF_S), lambda s, m: (s, l, j, m, 0)), o_shape=None,
                     o_dtype=F32, dims=NN, acc_shape=None, into=G[key])
    return dh_prev, dg


def _ffn_bwd_acts(tag, dh, sv, W, l, j):
    Lp = dh.shape[0]
    tm = _rt(Lp, 384)
    last = NSH - 1

    def body(dh_ref, h_ref, g_ref, wd_ref, wg_ref, wu_ref, a_ref, b_ref, da_ref, db_ref, dho_ref, dg_ref, acc_scr):
        s, i = pl.program_id(0), pl.program_id(1)
        rows = pl.ds(pl.multiple_of(i * tm, tm), tm)
        dhv = dh_ref[...]
        ds = 0.5 * lax.dot_general(dhv.astype(BF16), wd_ref[...], NT, preferred_element_type=F32)
        av = a_ref[:, :FF_S].astype(F32)
        bv = b_ref[:, :FF_S].astype(F32)
        sg = jax.nn.sigmoid(av)
        da = (ds * bv * (sg * (1.0 + av * (1.0 - sg)))).astype(BF16)
        db = (ds * (av * sg)).astype(BF16)
        da_ref[:, :FF_S] = da
        da_ref[:, FF_S:] = jnp.zeros((tm, FF_P - FF_S), BF16)
        db_ref[:, :FF_S] = db
        db_ref[:, FF_S:] = jnp.zeros((tm, FF_P - FF_S), BF16)
        part = (lax.dot_general(da, wg_ref[...], NT, preferred_element_type=F32)
                + lax.dot_general(db, wu_ref[...], NT, preferred_element_type=F32))

        @pl.when(s == 0)
        def _():
            acc_scr[rows, :] = part

        @pl.when(s > 0)
        def _():
            acc_scr[rows, :] += part

        @pl.when((s == 0) & (i == 0))
        def _():
            dg_ref[...] = jnp.zeros_like(dg_ref)

        @pl.when(s == last)
        def _():
            x = h_ref[...]
            r = lax.rsqrt(jnp.mean(x * x, axis=-1, keepdims=True) + EPS)
            xh = x * r
            dn = acc_scr[rows, :]
            u = dn * g_ref[...]
            dho_ref[...] = dhv + r * (u - xh * jnp.mean(u * xh, axis=-1, keepdims=True))
            dg_ref[...] += jnp.sum(dn * xh, axis=0, keepdims=True)

    slab = pl.BlockSpec((None, tm, FF_P), lambda s, i: (s, i, 0))
    wup = pl.BlockSpec((None, None, None, D, FF_S), lambda s, i: (s, l, j, 0, 0))
    at_last = pl.BlockSpec((tm, D), lambda s, i: (jnp.where(s == last, i, 0), 0))
    vec = pl.BlockSpec((1, D), lambda s, i: (0, 0))
    return pl.pallas_call(
        body, name=f"bffn_{tag}", grid=(NSH, Lp // tm),
        in_specs=[pl.BlockSpec((tm, D), lambda s, i: (i, 0)), at_last, vec,
                  pl.BlockSpec((None, None, None, FF_S, D), lambda s, i: (s, l, j, 0, 0)), wup, wup, slab, slab],
        out_specs=[slab, slab, at_last, vec],
        out_shape=[SDS((NSH, Lp, FF_P), BF16), SDS((NSH, Lp, FF_P), BF16), SDS((Lp, D), F32), SDS((1, D), F32)],
        scratch_shapes=[pltpu.VMEM((Lp, D), F32)], compiler_params=_cp(True))(
            dh, sv["h"], sv["g"], W["down"], W["gate"], W["up"], sv["a"], sv["b"])


def _mixer_fwd(tag, h, W, C, l, G_chunks):
    Lp = h.shape[0]
    tm = _rt(Lp, 1408)
    g = W["norm_gains"][l, 1].reshape(1, D)
    z, zt = _rms_fwd(f"rmsf_{tag}", h, g)
    tn = 896
    H = _mm(f"win_{tag}", z, W["winp"], grid=(HP // tn, Lp // tm), a_spec=pl.BlockSpec((tm, D), lambda n, i: (i, 0)),
            b_spec=pl.BlockSpec((None, D, tn), lambda n, i: (l, 0, n)), o_spec=pl.BlockSpec((tm, tn), lambda n, i: (i, n)),
            o_shape=(Lp, HP), o_dtype=F32, dims=NN, acc_shape=None)
    w2p, b2p = C["w2p"][l], C["b2p"][l]
    qh, kh, vh, gf, gb = _gla_prep(f"glap_{tag}", H, w2p, b2p)
    of, sf = _gla_fwd(f"glaf_{tag}", qh, kh, gf, vh, False, G_chunks)
    ob, sb = _gla_fwd(f"glar_{tag}", qh, kh, gb, vh, True, G_chunks)
    gn = W["gn"][l].reshape(1, 512)
    a, at = _gla_post(f"glao_{tag}", of, ob, H, gn)
    gq, gk = C["gq"][l], C["gk"][l]
    qp, kr, vb = _att_prep(f"attp_{tag}", H, gq, gk, C["cos"], C["sin"], C["bd"])
    b, bt, lse = _att_fwd(f"attf_{tag}", qp, kr, vb, C["bias"])
    proj = functools.partial(
        _mm, grid=(NSH, Lp // tm), a_spec=pl.BlockSpec((tm, 512), lambda s, i: (i, 0)),
        b_spec=pl.BlockSpec((None, None, 512, 256), lambda s, i: (s, l, 0, 0)), o_spec=pl.BlockSpec((tm, 256), lambda s, i: (i, s)),
        o_shape=(Lp, D), o_dtype=F32, dims=NN, acc_shape=None)
    pa = proj(f"pa_{tag}", a, W["wpa"])
    pb = proj(f"pb_{tag}", b, W["wpb"])
    bm = W["bm"][l]
    y, yt = _merge_fwd(f"mrg_{tag}", H, pa, pb, bm)
    row = pl.BlockSpec((tm, D), lambda i, s: (i, 0))
    h2 = _mm(f"wout_{tag}", y, W["wout"], grid=(Lp // tm, NSH), a_spec=pl.BlockSpec((tm, 256), lambda i, s: (i, s)),
             b_spec=pl.BlockSpec((None, None, 256, D), lambda i, s: (s, l, 0, 0)), o_spec=row, o_shape=(Lp, D), o_dtype=F32,
             dims=NN, acc_shape=(tm, D), nk=NSH, res=h, res_spec=row)
    sv = dict(h=h, g=g, zt=zt, H=H, w2p=w2p, b2p=b2p, qh=qh, kh=kh, vh=vh, gf=gf, gb=gb, of=of, ob=ob, sf=sf, sb=sb, gn=gn,
              at=at, gq=gq, gk=gk, qp=qp, kr=kr, vb=vb, bt=bt, lse=lse, pa=pa, pb=pb, bm=bm, yt=yt)
    return h2, sv


def _mixer_bwd(tag, dh, sv, W, C, G, S, l, G_chunks):
    Lp = dh.shape[0]
    tm = _rt(Lp, 1408)
    H = sv["H"]
    dy = _mm(f"bdy_{tag}", dh, W["wout"], grid=(NSH, Lp // tm), a_spec=pl.BlockSpec((tm, D), lambda s, i: (i, 0)),
             b_spec=pl.BlockSpec((None, None, 256, D), lambda s, i: (s, l, 0, 0)), o_spec=pl.BlockSpec((tm, 256), lambda s, i: (i, s)),
             o_shape=(Lp, D), o_dtype=F32, dims=NT, acc_shape=None)
    tn = 512
    G["wout"] = _mm(f"bwo_{tag}", sv["yt"], dh, grid=(NSH, D // tn), a_spec=pl.BlockSpec((256, Lp), lambda s, n: (s, 0)),
                    b_spec=pl.BlockSpec((Lp, tn), lambda s, n: (0, n)),
                    o_spec=pl.BlockSpec((None, None, 256, tn), lambda s, n: (s, l, 0, n)), o_shape=None, o_dtype=F32, dims=NN,
                    acc_shape=None, into=G["wout"])
    dpa, dpb, dH, S["bm"][l] = _merge_bwd(f"bmrg_{tag}", H, sv["pa"], sv["pb"], sv["bm"], dy)
    dbranch = {}
    for key, dp, xt in (("wpa", dpa, sv["at"]), ("wpb", dpb, sv["bt"])):
        dbranch[key] = _mm(f"bx{key[2]}_{tag}", dp, W[key], grid=(Lp // tm, NSH),
                           a_spec=pl.BlockSpec((tm, 256), lambda i, s: (i, s)),
                           b_spec=pl.BlockSpec((None, None, 512, 256), lambda i, s: (s, l, 0, 0)),
                           o_spec=pl.BlockSpec((tm, 512), lambda i, s: (i, 0)), o_shape=(Lp, 512), o_dtype=F32, dims=NT,
                           acc_shape=(tm, 512), nk=NSH)
        G[key] = _mm(f"bw{key[2]}_{tag}", xt, dp, grid=(NSH,), a_spec=pl.BlockSpec((512, Lp), lambda s: (0, 0)),
                     b_spec=pl.BlockSpec((Lp, 256), lambda s: (0, s)),
                     o_spec=pl.BlockSpec((None, None, 512, 256), lambda s: (s, l, 0, 0)), o_shape=None, o_dtype=F32, dims=NN,
                     acc_shape=None, into=G[key])
    dqp, dkr, dvb = _att_bwd(f"attb_{tag}", sv["qp"], sv["kr"], sv["vb"], C["bias"], sv["lse"], dbranch["wpb"])
    dH, S["gq"][l] = _att_q_bwd(f"attq_{tag}", H, sv["gq"], C["cos"], C["sin"], C["bd"], dqp, dH)
    dH, S["gk"][l] = _att_kv_bwd(f"attk_{tag}", H, sv["gk"], C["cos"], C["sin"], C["bd"], dkr, dvb, dH)
    do, dH, S["gn"][l] = _gla_post_bwd(f"bglo_{tag}", sv["of"], sv["ob"], H, sv["gn"], dbranch["wpa"], dH)
    dqf, dkf, dgf, dvf = _gla_bwd(f"bglf_{tag}", sv["qh"], sv["kh"], sv["gf"], sv["vh"], sv["sf"], do, False, G_chunks)
    dqb, dkb, dgb, dvr = _gla_bwd(f"bglr_{tag}", sv["qh"], sv["kh"], sv["gb"], sv["vh"], sv["sb"], do, True, G_chunks)
    dH = _gla_qkv_bwd(f"bglq_{tag}", dqf, dqb, dkf, dkb, dvf, dvr, dH)
    dH, S["w2p"][l], S["b2p"][l] = _gla_gate_bwd(f"bglg_{tag}", H, sv["w2p"], sv["b2p"], dgf, dgb, dH)
    tmm = 256
    G["winp"] = _mm(f"bwi_{tag}", sv["zt"], dH, grid=(D // tmm, HP // 896), a_spec=pl.BlockSpec((tmm, Lp), lambda m, n: (m, 0)),
                    b_spec=pl.BlockSpec((Lp, 896), lambda m, n: (0, n)), o_spec=pl.BlockSpec((None, tmm, 896), lambda m, n: (l, m, n)),
                    o_shape=None, o_dtype=F32, dims=NN, acc_shape=None, into=G["winp"])
    dz = _mm(f"bdz_{tag}", dH, W["winp"], grid=(D // 512, Lp // tm), a_spec=pl.BlockSpec((tm, HP), lambda n, i: (i, 0)),
             b_spec=pl.BlockSpec((None, 512, HP), lambda n, i: (l, n, 0)), o_spec=pl.BlockSpec((tm, 512), lambda n, i: (i, n)),
             o_shape=(Lp, D), o_dtype=F32, dims=NT, acc_shape=None)
    return _rms_bwd(f"rmsb_{tag}", dz, sv["h"], sv["g"], dh)


def _local_step(x2, tgt2, W):
    n_tok = x2.shape[0]
    Lp = n_tok + LANE
    nc = Lp // CHUNK
    g_chunks = max(g for g in (1, 2, 3, 6) if nc % g == 0)
    cos, sin = _rope_tables(n_tok)
    bd = jnp.asarray(np.kron(np.eye(ATT_H, dtype=np.float32), np.full((HD, HD), 1.0 / HD, np.float32)))
    bias = jnp.where(jnp.arange(Lp) >= PAD, 0.0, -1e30).astype(F32).reshape(1, Lp)
    w2, b2 = W["w2"], W["b2"]
    w2p = jnp.zeros((DEPTH, LANE, 512), F32)
    w2p = w2p.at[:, 0:GLA_RANK, 0:256].set(w2[:, 0]).at[:, GLA_RANK:2 * GLA_RANK, 256:512].set(w2[:, 1]).astype(BF16)
    C = dict(cos=cos, sin=sin, bd=bd, bias=bias, w2p=w2p, b2p=b2.reshape(DEPTH, 1, 512),
             gq=jnp.tile(W["qn"], (1, ATT_H)).reshape(DEPTH, 1, 512), gk=jnp.tile(W["kn"], (1, ATT_KV)).reshape(DEPTH, 1, LANE))
    h = jnp.concatenate([jnp.zeros((PAD, D), F32), W["meta"], x2], axis=0)
    saved = []
    for l in range(DEPTH):
        h, s0 = _ffn_fwd(f"l{l}a", h, W, l, 0)
        h, sm = _mixer_fwd(f"l{l}m", h, W, C, l, g_chunks)
        h, s1 = _ffn_fwd(f"l{l}b", h, W, l, 1)
        saved.append((s0, sm, s1))
    dh, dfin, loss = _loss_head("loss_head", h, W["fin"].reshape(1, D), tgt2)
    G = dict(gate=lax.empty((NSH, DEPTH, 2, D, FF_S), F32), up=lax.empty((NSH, DEPTH, 2, D, FF_S), F32),
             down=lax.empty((NSH, DEPTH, 2, FF_S, D), F32), winp=lax.empty((DEPTH, D, HP), F32),
             wpa=lax.empty((NSH, DEPTH, 512, 256), F32), wpb=lax.empty((NSH, DEPTH, 512, 256), F32),
             wout=lax.empty((NSH, DEPTH, 256, D), F32))
    S = dict(bm=[None] * DEPTH, gq=[None] * DEPTH, gk=[None] * DEPTH, gn=[None] * DEPTH, w2p=[None] * DEPTH,
             b2p=[None] * DEPTH, ng=[[None] * 3 for _ in range(DEPTH)])
    for l in reversed(range(DEPTH)):
        s0, sm, s1 = saved[l]
        dh, S["ng"][l][2] = _ffn_bwd(f"l{l}b", dh, s1, W, G, l, 1)
        dh, S["ng"][l][1] = _mixer_bwd(f"l{l}m", dh, sm, W, C, G, S, l, g_chunks)
        dh, S["ng"][l][0] = _ffn_bwd(f"l{l}a", dh, s0, W, G, l, 0)
    small = dict(
        meta=dh[PAD:LANE],
        norm_gains=jnp.stack([jnp.concatenate(S["ng"][l], axis=0) for l in range(DEPTH)]),
        w2=jnp.stack([jnp.stack([S["w2p"][l][0:GLA_RANK, 0:256], S["w2p"][l][GLA_RANK:2 * GLA_RANK, 256:512]]) for l in range(DEPTH)]),
        b2=jnp.stack([S["b2p"][l].reshape(2, 256) for l in range(DEPTH)]),
        gn=jnp.concatenate(S["gn"], axis=0),
        qn=jnp.stack([S["gq"][l].reshape(ATT_H, HD).sum(0) for l in range(DEPTH)]),
        kn=jnp.stack([S["gk"][l].reshape(ATT_KV, HD).sum(0) for l in range(DEPTH)]),
        bm=jnp.stack(S["bm"]),
        fin=dfin.reshape(D),
    )
    return loss[0, 0], dh, G, small


def _win_to_padded(w):
    pad = jnp.zeros(w.shape[:-1] + (HP - D_IN,), w.dtype)
    return jnp.concatenate([w[..., 2336:4384], w[..., 0:1536], w[..., 1568:2336], w[..., 1536:1568], pad], axis=-1)


def _win_from_padded(w):
    return jnp.concatenate([w[..., 2048:3584], w[..., 4352:4384], w[..., 3584:4352], w[..., 0:2048]], axis=-1)


def _assemble(g):
    W = dict(gate=g["gate"], up=g["up"], down=g["down"], wpa=g["wpa"], wpb=g["wpb"], wout=g["wout"])
    win = jnp.transpose(g["win"], (1, 2, 0, 3)).reshape(DEPTH, D, D_IN)
    W["winp"] = _win_to_padded(win)
    sm = g["small"]
    parts = _unpack(sm, SHARDED_SMALL)
    W["meta"] = jnp.transpose(parts["meta"], (1, 0, 2)).reshape(N_META, D)
    W["norm_gains"] = jnp.transpose(parts["norm_gains"], (1, 2, 0, 3)).reshape(DEPTH, 3, D)
    W["w2"] = jnp.transpose(parts["w2"], (1, 2, 3, 0, 4)).reshape(DEPTH, 2, GLA_RANK, 256)
    W["b2"] = jnp.transpose(parts["b2"], (1, 2, 0, 3)).reshape(DEPTH, 2, 256)
    W["bm"] = jnp.transpose(parts["bm"], (1, 2, 0, 3)).reshape(DEPTH, 2, D)
    return W


SHARDED_SMALL = dict(meta=(N_META, 256), norm_gains=(DEPTH, 3, 256), w2=(DEPTH, 2, GLA_RANK, 64), b2=(DEPTH, 2, 64),
                     bm=(DEPTH, 2, 256))
FULL_SMALL = dict(meta=(N_META, D), norm_gains=(DEPTH, 3, D), w2=(DEPTH, 2, GLA_RANK, 256), b2=(DEPTH, 2, 256),
                  gn=(DEPTH, 512), qn=(DEPTH, HD), kn=(DEPTH, HD), bm=(DEPTH, 2, D), fin=(D,))


def _pack(parts, table, rows):
    flat = jnp.concatenate([parts[k].reshape(-1).astype(F32) for k in table])
    return jnp.pad(flat, (0, rows * LANE - flat.shape[0])).reshape(rows, LANE)


def _unpack(packed, table):
    lead = packed.shape[:-2]
    flat = packed.reshape(lead + (-1,))
    out, off = {}, 0
    for k, shp in table.items():
        n = int(np.prod(shp))
        out[k] = flat[..., off:off + n].reshape(lead + tuple(shp))
        off += n
    return out


def _rows_for(table, mult):
    n = sum(int(np.prod(s)) for s in table.values())
    return -(-n // (LANE * mult)) * mult


SMALL_ROWS = _rows_for(SHARDED_SMALL, 16)
GRAD_ROWS = _rows_for(dict(FULL_SMALL, loss=(1,)), 8)


def _place():
    x, y, c = lax.axis_index("x"), lax.axis_index("y"), lax.axis_index("c")
    return x, y, c


def _other_chips(x, y):
    return [(1 - x, y), (x, 1 - y), (1 - x, 1 - y)]


def _cast_place(name, w3, slot):
    A, R, Cc = w3.shape
    tr = _rt(R, 512, 16)

    def body(p_ref, w_ref, o_ref):
        o_ref[...] = w_ref[...].astype(BF16)

    return pl.pallas_call(
        body, name=name,
        grid_spec=pltpu.PrefetchScalarGridSpec(
            num_scalar_prefetch=1, grid=(A, R // tr),
            in_specs=[pl.BlockSpec((None, tr, Cc), lambda a, r, p_ref: (a, r, 0))],
            out_specs=pl.BlockSpec((None, None, tr, Cc), lambda a, r, p_ref: (p_ref[0], a, r, 0))),
        out_shape=SDS((NSH, A, R, Cc), BF16))(slot, w3)


def _gather_weights(arrs):
    n = len(arrs)

    def body(*refs):
        outs = refs[n:2 * n]
        send, recv, fsend, frecv = refs[2 * n:]
        x, y, c = _place()
        me = 2 * x + y
        sib = (x, y, 1 - c)
        chips = _other_chips(x, y)
        copies = []
        for a in range(n):
            half = outs[a].shape[1] // 2
            mine = pl.ds(c * half, half)
            for k, (px, py) in enumerate(chips):
                cp = pltpu.make_async_remote_copy(
                    src_ref=outs[a].at[me, mine], dst_ref=outs[a].at[me, mine], send_sem=send.at[a, k], recv_sem=recv.at[a, k],
                    device_id=(px, py, c), device_id_type=MESH)
                cp.start()
                copies.append(cp)
        for a in range(n):
            half = outs[a].shape[1] // 2
            mine = pl.ds(c * half, half)
            for k, (px, py) in enumerate(chips):
                src = 2 * px + py
                landed = outs[a].at[src, mine]
                pltpu.make_async_remote_copy(
                    src_ref=landed, dst_ref=landed, send_sem=send.at[a, k], recv_sem=recv.at[a, k],
                    device_id=(px, py, c), device_id_type=MESH).wait_recv()
                fw = pltpu.make_async_remote_copy(
                    src_ref=landed, dst_ref=landed, send_sem=fsend.at[a, k], recv_sem=frecv.at[a, k],
                    device_id=sib, device_id_type=MESH)
                fw.start()
                copies.append(fw)
        for a in range(n):
            half = outs[a].shape[1] // 2
            theirs = pl.ds((1 - c) * half, half)
            for k, (px, py) in enumerate(chips):
                src = 2 * px + py
                landed = outs[a].at[src, theirs]
                pltpu.make_async_remote_copy(
                    src_ref=landed, dst_ref=landed, send_sem=fsend.at[a, k], recv_sem=frecv.at[a, k],
                    device_id=sib, device_id_type=MESH).wait_recv()
        for cp in copies:
            cp.wait_send()

    any_spec = pl.BlockSpec(memory_space=pl.ANY)
    return pl.pallas_call(
        body, name="gather_weights", in_specs=[any_spec] * n, out_specs=[any_spec] * n,
        out_shape=[SDS(a.shape, a.dtype) for a in arrs], input_output_aliases={a: a for a in range(n)},
        scratch_shapes=[pltpu.SemaphoreType.DMA((n, 3)), pltpu.SemaphoreType.DMA((n, 3)), pltpu.SemaphoreType.DMA((n, 3)),
                        pltpu.SemaphoreType.DMA((n, 3))])(*arrs)


def _pair_exchange(arrs):
    n = len(arrs)

    def body(*refs):
        ins, outs = refs[:n], refs[n:2 * n]
        send, recv = refs[2 * n:]
        x, y, c = _place()
        cps = []
        for a in range(n):
            cp = pltpu.make_async_remote_copy(
                src_ref=ins[a].at[:, pl.ds(2 * (1 - c), 2)], dst_ref=outs[a], send_sem=send.at[a], recv_sem=recv.at[a],
                device_id=(x, y, 1 - c), device_id_type=MESH)
            cp.start()
            cps.append(cp)
        for cp in cps:
            cp.wait()

    any_spec = pl.BlockSpec(memory_space=pl.ANY)
    return pl.pallas_call(
        body, name="pair_exchange", in_specs=[any_spec] * n, out_specs=[any_spec] * n,
        out_shape=[SDS((NSH, 2) + a.shape[2:], a.dtype) for a in arrs],
        scratch_shapes=[pltpu.SemaphoreType.DMA((n,)), pltpu.SemaphoreType.DMA((n,))])(*arrs)


def _pair_add(name, g, p, cidx):
    _, _, R, Cc = g.shape
    tr = _rt(R, 512, 8)

    def body(c_ref, g_ref, p_ref, o_ref, ob_ref):
        v = g_ref[...] + p_ref[...]
        o_ref[...] = v
        ob_ref[...] = v.astype(BF16)

    blk = pl.BlockSpec((None, None, tr, Cc), lambda s, l, r, c_ref: (s, l, r, 0))
    return pl.pallas_call(
        body, name=name,
        grid_spec=pltpu.PrefetchScalarGridSpec(
            num_scalar_prefetch=1, grid=(NSH, 2, R // tr),
            in_specs=[pl.BlockSpec((None, None, tr, Cc), lambda s, l, r, c_ref: (s, 2 * c_ref[0] + l, r, 0)), blk],
            out_specs=[blk, blk]),
        out_shape=[SDS((NSH, 2, R, Cc), F32), SDS((NSH, 2, R, Cc), BF16)])(cidx, g, p)


def _chip_exchange(arrs):
    n = len(arrs)

    def body(*refs):
        ins, outs = refs[:n], refs[n:2 * n]
        send, recv = refs[2 * n:]
        x, y, c = _place()
        cps = []
        for a in range(n):
            for k, (px, py) in enumerate(_other_chips(x, y)):
                cp = pltpu.make_async_remote_copy(
                    src_ref=ins[a].at[2 * px + py], dst_ref=outs[a].at[k], send_sem=send.at[a, k], recv_sem=recv.at[a, k],
                    device_id=(px, py, c), device_id_type=MESH)
                cp.start()
                cps.append(cp)
        for cp in cps:
            cp.wait()

    any_spec = pl.BlockSpec(memory_space=pl.ANY)
    return pl.pallas_call(
        body, name="chip_exchange", in_specs=[any_spec] * n, out_specs=[any_spec] * n,
        out_shape=[SDS((3,) + a.shape[1:], a.dtype) for a in arrs],
        scratch_shapes=[pltpu.SemaphoreType.DMA((n, 3)), pltpu.SemaphoreType.DMA((n, 3))])(*arrs)


def _chip_add(name, hsum, q, chip, core):
    _, _, R, Cc = hsum.shape
    tr = _rt(R, 512, 8)

    def body(p_ref, c_ref, h_ref, q_ref, o_ref):
        o_ref[...] = ((h_ref[...] + q_ref[0].astype(F32)) + q_ref[1].astype(F32)) + q_ref[2].astype(F32)

    return pl.pallas_call(
        body, name=name,
        grid_spec=pltpu.PrefetchScalarGridSpec(
            num_scalar_prefetch=2, grid=(2, R // tr),
            in_specs=[pl.BlockSpec((None, None, tr, Cc), lambda l, r, p_ref, c_ref: (p_ref[0], l, r, 0)),
                      pl.BlockSpec((3, None, tr, Cc), lambda l, r, p_ref, c_ref: (0, l, r, 0))],
            out_specs=pl.BlockSpec((None, tr, Cc), lambda l, r, p_ref, c_ref: (2 * c_ref[0] + l, r, 0))),
        out_shape=SDS((DEPTH, R, Cc), F32))(chip, core, hsum, q)


def _pair_share(arrs):
    n = len(arrs)

    def body(*refs):
        outs = refs[n:2 * n]
        send, recv = refs[2 * n:]
        x, y, c = _place()
        cps = []
        for a in range(n):
            mine = outs[a].at[pl.ds(2 * c, 2)]
            cp = pltpu.make_async_remote_copy(
                src_ref=mine, dst_ref=mine, send_sem=send.at[a], recv_sem=recv.at[a],
                device_id=(x, y, 1 - c), device_id_type=MESH)
            cp.start()
            cps.append(cp)
        for a, cp in enumerate(cps):
            cp.wait_send()
            theirs = outs[a].at[pl.ds(2 * (1 - c), 2)]
            pltpu.make_async_remote_copy(
                src_ref=theirs, dst_ref=theirs, send_sem=send.at[a], recv_sem=recv.at[a],
                device_id=(x, y, 1 - c), device_id_type=MESH).wait_recv()

    any_spec = pl.BlockSpec(memory_space=pl.ANY)
    return pl.pallas_call(
        body, name="pair_share", in_specs=[any_spec] * n, out_specs=[any_spec] * n,
        out_shape=[SDS(a.shape, a.dtype) for a in arrs], input_output_aliases={a: a for a in range(n)},
        scratch_shapes=[pltpu.SemaphoreType.DMA((n,)), pltpu.SemaphoreType.DMA((n,))])(*arrs)


def _allreduce_small(v):
    rows = v.shape[0]

    def body(v_ref, o_ref, buf, send, recv):
        x, y, c = _place()
        me = 4 * x + 2 * y + c
        buf[me] = v_ref[...]
        cps = []
        k = 0
        for dx in range(2):
            for dy in range(2):
                for dc in range(2):
                    if dx + dy + dc == 0:
                        continue
                    cp = pltpu.make_async_remote_copy(
                        src_ref=v_ref, dst_ref=buf.at[me], send_sem=send.at[k], recv_sem=recv.at[k],
                        device_id=(jnp.bitwise_xor(x, dx), jnp.bitwise_xor(y, dy), jnp.bitwise_xor(c, dc)), device_id_type=MESH)
                    cp.start()
                    cps.append((cp, dx, dy, dc))
                    k += 1
        for k, (cp, dx, dy, dc) in enumerate(cps):
            cp.wait_send()
            src = 4 * jnp.bitwise_xor(x, dx) + 2 * jnp.bitwise_xor(y, dy) + jnp.bitwise_xor(c, dc)
            pltpu.make_async_remote_copy(
                src_ref=v_ref, dst_ref=buf.at[src], send_sem=send.at[k], recv_sem=recv.at[k],
                device_id=(x, y, c), device_id_type=MESH).wait_recv()
        acc = buf[0]
        for d in range(1, 8):
            acc = acc + buf[d]
        o_ref[...] = acc

    vm = pl.BlockSpec(memory_space=pltpu.VMEM)
    return pl.pallas_call(
        body, name="allreduce_small", in_specs=[vm], out_specs=vm, out_shape=SDS((rows, LANE), F32),
        scratch_shapes=[pltpu.VMEM((8, rows, LANE), F32), pltpu.SemaphoreType.DMA((7,)), pltpu.SemaphoreType.DMA((7,))])(v)


def _adamw(name, w, g, m, v):
    A, R, Cc = w.shape
    tr = _rt(R, 512, 8)

    def body(w_ref, g_ref, m_ref, v_ref, d_ref, mo_ref, vo_ref):
        gv = g_ref[...]
        mn = ADAM_B1 * m_ref[...] + (1.0 - ADAM_B1) * gv
        vn = ADAM_B2 * v_ref[...] + (1.0 - ADAM_B2) * (gv * gv)
        m_hat = mn / (1.0 - ADAM_B1 ** ADAM_STEP)
        v_hat = vn / (1.0 - ADAM_B2 ** ADAM_STEP)
        d_ref[...] = -ADAM_LR * (m_hat / (jnp.sqrt(v_hat) + ADAM_EPS) + ADAM_WD * w_ref[...])
        mo_ref[...] = mn
        vo_ref[...] = vn

    blk = pl.BlockSpec((None, tr, Cc), lambda a, r: (a, r, 0))
    return pl.pallas_call(
        body, name=name, grid=(A, R // tr), in_specs=[blk] * 4, out_specs=[blk] * 3,
        out_shape=[SDS(w.shape, F32)] * 3)(w, g, m, v)


BIG = ("gate", "up", "down", "win", "wpa", "wpb", "wout")
SMALL = ("meta", "norm_gains", "w2", "b2", "gn", "qn", "kn", "bm", "fin")


def _view3(a):
    return a.reshape(a.shape[0], -1, a.shape[-1])


def kernel(x, meta_tokens, norm_gains, ffn_w_gate, ffn_w_up, ffn_w_down, w_in, gla_w2, gla_b2, gla_gn, q_norm, k_norm, w_pa, w_pb, b_merge, w_out, final_norm, loss_target, m_meta_tokens, m_norm_gains, m_ffn_w_gate, m_ffn_w_up, m_ffn_w_down, m_w_in, m_gla_w2, m_gla_b2, m_gla_gn, m_q_norm, m_k_norm, m_w_pa, m_w_pb, m_b_merge, m_w_out, m_final_norm, v_meta_tokens, v_norm_gains, v_ffn_w_gate, v_ffn_w_up, v_ffn_w_down, v_w_in, v_gla_w2, v_gla_b2, v_gla_gn, v_q_norm, v_k_norm, v_w_pa, v_w_pb, v_b_merge, v_w_out, v_final_norm):
    big_w = dict(gate=ffn_w_gate, up=ffn_w_up, down=ffn_w_down, win=w_in, wpa=w_pa, wpb=w_pb, wout=w_out)
    big_m = dict(gate=m_ffn_w_gate, up=m_ffn_w_up, down=m_ffn_w_down, win=m_w_in, wpa=m_w_pa, wpb=m_w_pb, wout=m_w_out)
    big_v = dict(gate=v_ffn_w_gate, up=v_ffn_w_up, down=v_ffn_w_down, win=v_w_in, wpa=v_w_pa, wpb=v_w_pb, wout=v_w_out)
    small_w = dict(meta=meta_tokens, norm_gains=norm_gains, w2=gla_w2, b2=gla_b2, gn=gla_gn, qn=q_norm, kn=k_norm,
                   bm=b_merge, fin=final_norm)
    small_m = dict(meta=m_meta_tokens, norm_gains=m_norm_gains, w2=m_gla_w2, b2=m_gla_b2, gn=m_gla_gn, qn=m_q_norm,
                   kn=m_k_norm, bm=m_b_merge, fin=m_final_norm)
    small_v = dict(meta=v_meta_tokens, norm_gains=v_norm_gains, w2=v_gla_w2, b2=v_gla_b2, gn=v_gla_gn, qn=v_q_norm,
                   kn=v_k_norm, bm=v_b_merge, fin=v_final_norm)
    xi, yi, ci = _place()
    chip = (2 * xi + yi).astype(jnp.int32)

    pvec = chip.reshape(1)
    shard_pack = _pack({k: small_w[k] for k in SHARDED_SMALL}, SHARDED_SMALL, SMALL_ROWS)
    placed = [_cast_place(f"cast_{k}", _view3(big_w[k]), pvec) for k in BIG]
    placed.append(lax.dynamic_update_slice(jnp.zeros((NSH, SMALL_ROWS, LANE), F32), shard_pack[None], (chip, 0, 0)))
    gathered = _gather_weights(placed)
    g = {k: a.reshape((NSH,) + big_w[k].shape) for k, a in zip(BIG, gathered)}
    g["small"] = gathered[-1]
    W = _assemble(g)
    W.update(gn=gla_gn, qn=q_norm, kn=k_norm, fin=final_norm)

    loss, dh0, G, gs = _local_step(x[0], loss_target[0], W)
    grad_x = dh0[LANE:][None]

    dwin = _win_from_padded(G.pop("winp")).reshape(DEPTH, D, NSH, D_IN // NSH)
    G["win"] = jnp.transpose(dwin, (2, 0, 1, 3))
    loc = [G[k].reshape((NSH, DEPTH, -1, G[k].shape[-1])) for k in BIG]
    got = _pair_exchange(loc)
    cvec = ci.astype(jnp.int32).reshape(1)
    sums = [_pair_add(f"pair_add_{k}", a, p, cvec) for k, a, p in zip(BIG, loc, got)]
    arrived = _chip_exchange([s[1] for s in sums])
    mine = [_chip_add(f"chip_add_{k}", s[0], q, pvec, cvec) for k, s, q in zip(BIG, sums, arrived)]
    shared = _pair_share(mine)
    grads, deltas, new_m, new_v = {}, {}, {}, {}
    for k, r in zip(BIG, shared):
        shp = big_w[k].shape
        grads[k] = r.reshape(shp)
        d, mn, vn = _adamw(f"adamw_{k}", _view3(big_w[k]), r, _view3(big_m[k]), _view3(big_v[k]))
        deltas[k], new_m[k], new_v[k] = d.reshape(shp), mn.reshape(shp), vn.reshape(shp)

    gs["loss"] = loss.reshape(1)
    table = dict(FULL_SMALL, loss=(1,))
    tot = _unpack(_allreduce_small(_pack(gs, table, GRAD_ROWS)), table)
    loss_out = tot["loss"][0]
    sl = dict(meta=(1, 256), norm_gains=(2, 256), w2=(3, 64), b2=(2, 64), bm=(2, 256))
    for k in SMALL:
        gk = tot[k]
        if k in sl:
            ax, width = sl[k]
            gk = lax.dynamic_slice_in_dim(gk, chip * width, width, axis=ax)
        grads[k] = gk
    tbl = {k: small_w[k].shape for k in SMALL}
    rows = _rows_for(tbl, 8)
    packs = [_pack(src, tbl, rows)[None] for src in (small_w, grads, small_m, small_v)]
    d, mn, vn = _adamw("adamw_small", *packs)
    for dst, packed in ((deltas, d), (new_m, mn), (new_v, vn)):
        dst.update(_unpack(packed[0], tbl))

    order = ("meta", "norm_gains", "gate", "up", "down", "win", "w2", "b2", "gn", "qn", "kn", "wpa", "wpb", "bm", "wout", "fin")
    return (loss_out, grad_x, *[grads[k] for k in order], *[deltas[k] for k in order], *[new_m[k] for k in order],
            *[new_v[k] for k in order])
```

```python
import functools

import numpy as np
import jax
import jax.numpy as jnp
from jax import lax
from jax.experimental import pallas as pl
from jax.experimental.pallas import tpu as pltpu

F32, BF16 = jnp.float32, jnp.bfloat16
SDS = jax.ShapeDtypeStruct
HIGHEST = lax.Precision.HIGHEST
MESH = pl.DeviceIdType.MESH

D = 1024
DEPTH = 4
N_META = 16
GRID_W = 64
GLA_H, GLA_DK, GLA_DV, GLA_RANK, GLA_TAU, CHUNK = 4, 64, 128, 16, 16.0, 64
ATT_H, ATT_KV, HD = 8, 2, 64
D_FF = 2816
EPS = 1e-6
ROPE_THETA = 10000.0
ADAM_LR, ADAM_B1, ADAM_B2, ADAM_EPS, ADAM_WD, ADAM_STEP = 0.001, 0.9, 0.999, 1e-08, 0.01, 10

NSH = 4
FF_S = D_FF // NSH
FF_P = 768
LANE = 128
PAD = LANE - N_META
D_IN = 4384
C_GA, C_GB, C_QA, C_KA, C_VA, C_RA, C_QB, C_KB, C_VB, C_LR, HP = 0, 1024, 2048, 2304, 2560, 3072, 3584, 4096, 4224, 4352, 4480
VMEM_BIG = 56 * 2 ** 20
RIDER_MID_BACK = 5


def _rt(n, cap, mult=LANE):
    best = None
    t = mult
    while t <= min(n, cap):
        if n % t == 0:
            best = t
        t += mult
    assert best is not None, (n, cap, mult)
    return best


def _cp(big=False):
    return pltpu.CompilerParams(vmem_limit_bytes=VMEM_BIG) if big else None


def _row_ids(i, tm):
    return i * tm + lax.broadcasted_iota(jnp.int32, (tm, 1), 0)


def _mm(name, a, b, *, grid, a_spec, b_spec, o_spec, o_shape, o_dtype, dims, acc_shape, nk=1, scale=None,
        res=None, res_spec=None, a_sl=None, b_sl=None, pad_w=None, into=None):
    has_res, has_into = res is not None, into is not None

    def body(*refs):
        a_ref, b_ref = refs[0], refs[1]
        p = 2
        res_ref = None
        if has_res:
            res_ref = refs[p]
            p += 1
        if has_into:
            p += 1
        o_ref = refs[p]
        acc_ref = refs[p + 1] if nk > 1 else None
        av = (a_ref[a_sl] if a_sl is not None else a_ref[...]).astype(BF16)
        bv = (b_ref[b_sl] if b_sl is not None else b_ref[...]).astype(BF16)
        prod = lax.dot_general(av, bv, dims, preferred_element_type=F32)

        def finish(v):
            if scale is not None:
                v = v * scale
            if has_res:
                v = v + res_ref[...]
            v = v.astype(o_dtype)
            if pad_w is None:
                o_ref[...] = v
            else:
                w = v.shape[-1]
                o_ref[:, :w] = v
                o_ref[:, w:] = jnp.zeros((v.shape[0], pad_w - w), o_dtype)

        if nk == 1:
            finish(prod)
        else:
            k = pl.program_id(len(grid) - 1)

            @pl.when(k == 0)
            def _():
                acc_ref[...] = prod

            @pl.when(k > 0)
            def _():
                acc_ref[...] += prod

            @pl.when(k == nk - 1)
            def _():
                finish(acc_ref[...])

    in_specs = [a_spec, b_spec]
    args = [a, b]
    if has_res:
        in_specs.append(res_spec)
        args.append(res)
    aliases = {}
    if has_into:
        aliases = {len(args): 0}
        in_specs.append(pl.BlockSpec(memory_space=pl.ANY))
        args.append(into)
        o_shape = into.shape
    return pl.pallas_call(
        body, name=name, grid=grid, in_specs=in_specs, out_specs=o_spec, out_shape=SDS(tuple(o_shape), o_dtype),
        scratch_shapes=[pltpu.VMEM(acc_shape, F32)] if nk > 1 else [], input_output_aliases=aliases,
        compiler_params=_cp(True))(*args)


NN = (((1,), (0,)), ((), ()))
NT = (((1,), (1,)), ((), ()))
TN = (((0,), (0,)), ((), ()))


def _rms_fwd(name, h, g):
    Lp = h.shape[0]
    tm = _rt(Lp, 384)

    def body(h_ref, g_ref, n_ref, nt_ref):
        x = h_ref[...]
        r = lax.rsqrt(jnp.mean(x * x, axis=-1, keepdims=True) + EPS)
        y = x * r * g_ref[...]
        n_ref[...] = y.astype(BF16)
        nt_ref[...] = y.T.astype(BF16)

    return pl.pallas_call(
        body, name=name, grid=(Lp // tm,),
        in_specs=[pl.BlockSpec((tm, D), lambda i: (i, 0)), pl.BlockSpec((1, D), lambda i: (0, 0))],
        out_specs=[pl.BlockSpec((tm, D), lambda i: (i, 0)), pl.BlockSpec((D, tm), lambda i: (0, i))],
        out_shape=[SDS((Lp, D), BF16), SDS((D, Lp), BF16)])(h, g)


def _rms_bwd(name, dn, h, g, dh):
    Lp = h.shape[0]
    tm = _rt(Lp, 384)

    def body(dn_ref, h_ref, g_ref, dh_ref, o_ref, dg_ref):
        x = h_ref[...]
        r = lax.rsqrt(jnp.mean(x * x, axis=-1, keepdims=True) + EPS)
        xh = x * r
        dn = dn_ref[...]
        u = dn * g_ref[...]
        o_ref[...] = dh_ref[...] + r * (u - xh * jnp.mean(u * xh, axis=-1, keepdims=True))

        @pl.when(pl.program_id(0) == 0)
        def _():
            dg_ref[...] = jnp.zeros_like(dg_ref)

        dg_ref[...] += jnp.sum(dn * xh, axis=0, keepdims=True)

    row = pl.BlockSpec((tm, D), lambda i: (i, 0))
    vec = pl.BlockSpec((1, D), lambda i: (0, 0))
    return pl.pallas_call(
        body, name=name, grid=(Lp // tm,), in_specs=[row, row, vec, row], out_specs=[row, vec],
        out_shape=[SDS((Lp, D), F32), SDS((1, D), F32)])(dn, h, g, dh)


def _loss_head(name, h, g, tgt):
    Lp = h.shape[0]
    tm = LANE

    def body(h_ref, g_ref, t_ref, dh_ref, dg_ref, loss_ref):
        i = pl.program_id(0)
        x = h_ref[...]
        r = lax.rsqrt(jnp.mean(x * x, axis=-1, keepdims=True) + EPS)
        xh = x * r
        gg = g_ref[...]
        err = jnp.where(i >= 1, xh * gg - t_ref[...], 0.0)
        dy = err * (1.0 / D)
        u = dy * gg
        dh_ref[...] = r * (u - xh * jnp.mean(u * xh, axis=-1, keepdims=True))

        @pl.when(i == 0)
        def _():
            dg_ref[...] = jnp.zeros_like(dg_ref)
            loss_ref[...] = jnp.zeros_like(loss_ref)

        dg_ref[...] += jnp.sum(dy * xh, axis=0, keepdims=True)
        loss_ref[...] += (0.5 / D) * jnp.sum(err * err)

    row = pl.BlockSpec((tm, D), lambda i: (i, 0))
    vec = pl.BlockSpec((1, D), lambda i: (0, 0))
    return pl.pallas_call(
        body, name=name, grid=(Lp // tm,),
        in_specs=[row, vec, pl.BlockSpec((tm, D), lambda i: (jnp.maximum(i - 1, 0), 0))],
        out_specs=[row, vec, pl.BlockSpec((8, LANE), lambda i: (0, 0))],
        out_shape=[SDS((Lp, D), F32), SDS((1, D), F32), SDS((8, LANE), F32)])(h, g, tgt)


def _swiglu_fwd(name, a, b):
    Lp = a.shape[1]
    tm = _rt(Lp, 384)

    def body(a_ref, b_ref, s_ref, st_ref):
        av = a_ref[...]
        s = av * jax.nn.sigmoid(av) * b_ref[...]
        s_ref[...] = s.astype(BF16)
        st_ref[...] = s.T.astype(BF16)

    slab = pl.BlockSpec((None, tm, FF_P), lambda j, i: (j, i, 0))
    return pl.pallas_call(
        body, name=name, grid=(NSH, Lp // tm), in_specs=[slab, slab],
        out_specs=[slab, pl.BlockSpec((None, FF_P, tm), lambda j, i: (j, 0, i))],
        out_shape=[SDS((NSH, Lp, FF_P), BF16), SDS((NSH, FF_P, Lp), BF16)])(a, b)


def _swiglu_bwd(name, a, b, ds):
    Lp = a.shape[1]
    tm = _rt(Lp, 384)

    def body(a_ref, b_ref, ds_ref, da_ref, db_ref):
        av = a_ref[...]
        sg = jax.nn.sigmoid(av)
        dsv = ds_ref[...]
        da_ref[...] = (dsv * b_ref[...] * (sg * (1.0 + av * (1.0 - sg)))).astype(BF16)
        db_ref[...] = (dsv * (av * sg)).astype(BF16)

    slab = pl.BlockSpec((None, tm, FF_P), lambda j, i: (j, i, 0))
    return pl.pallas_call(
        body, name=name, grid=(NSH, Lp // tm), in_specs=[slab, slab, slab], out_specs=[slab, slab],
        out_shape=[SDS((NSH, Lp, FF_P), BF16), SDS((NSH, Lp, FF_P), BF16)])(a, b, ds)


def _merge_fwd(name, H, pa, pb, bm):
    Lp = H.shape[0]
    tm = _rt(Lp, 384)

    def body(g_ref, pa_ref, pb_ref, bm_ref, y_ref, yt_ref):
        gv = g_ref[...]
        y = (jax.nn.sigmoid(gv[:, :D] + bm_ref[0:1, :]) * pa_ref[...]
             + jax.nn.sigmoid(gv[:, D:] + bm_ref[1:2, :]) * pb_ref[...])
        y_ref[...] = y.astype(BF16)
        yt_ref[...] = y.T.astype(BF16)

    row = pl.BlockSpec((tm, D), lambda i: (i, 0))
    return pl.pallas_call(
        body, name=name, grid=(Lp // tm,),
        in_specs=[pl.BlockSpec((tm, 2 * D), lambda i: (i, 0)), row, row, pl.BlockSpec((2, D), lambda i: (0, 0))],
        out_specs=[row, pl.BlockSpec((D, tm), lambda i: (0, i))],
        out_shape=[SDS((Lp, D), BF16), SDS((D, Lp), BF16)])(H, pa, pb, bm)


def _merge_bwd(name, H, pa, pb, bm, dy):
    Lp = H.shape[0]
    tm = _rt(Lp, 384)

    def body(g_ref, pa_ref, pb_ref, bm_ref, dy_ref, dpa_ref, dpb_ref, dh_ref, dbm_ref):
        gv = g_ref[...]
        dyv = dy_ref[...]
        sa = jax.nn.sigmoid(gv[:, :D] + bm_ref[0:1, :])
        sb = jax.nn.sigmoid(gv[:, D:] + bm_ref[1:2, :])
        dpa_ref[...] = (dyv * sa).astype(BF16)
        dpb_ref[...] = (dyv * sb).astype(BF16)
        dga = dyv * pa_ref[...] * (sa * (1.0 - sa))
        dgb = dyv * pb_ref[...] * (sb * (1.0 - sb))
        dh_ref[:, :D] = dga.astype(BF16)
        dh_ref[:, D:] = dgb.astype(BF16)

        @pl.when(pl.program_id(0) == 0)
        def _():
            dbm_ref[...] = jnp.zeros_like(dbm_ref)

        dbm_ref[0:1, :] += jnp.sum(dga, axis=0, keepdims=True)
        dbm_ref[1:2, :] += jnp.sum(dgb, axis=0, keepdims=True)

    row = pl.BlockSpec((tm, D), lambda i: (i, 0))
    two = pl.BlockSpec((2, D), lambda i: (0, 0))
    gate = pl.BlockSpec((tm, 2 * D), lambda i: (i, 0))
    return pl.pallas_call(
        body, name=name, grid=(Lp // tm,), in_specs=[gate, row, row, two, row], out_specs=[row, row, gate, two],
        out_shape=[SDS((Lp, D), BF16), SDS((Lp, D), BF16), SDS((Lp, HP), BF16), SDS((2, D), F32)])(H, pa, pb, bm, dy)


def _gla_prep(name, H, w2p, b2p):
    Lp = H.shape[0]
    tm = _rt(Lp, 384)

    def body(qk_ref, v_ref, lr_ref, w_ref, b_ref, q_o, k_o, v_o, gf_o, gb_o):
        valid = _row_ids(pl.program_id(0), tm) >= PAD
        qk = qk_ref[...]
        vv = v_ref[...]
        pre = jnp.dot(lr_ref[...].astype(BF16), w_ref[...], preferred_element_type=F32) + b_ref[...]
        g = jnp.where(valid, jax.nn.log_sigmoid(pre) * (1.0 / GLA_TAU), 0.0)
        for hh in range(GLA_H):
            q_o[hh] = qk[:, 64 * hh:64 * hh + 64] * (GLA_DK ** -0.5)
            k_o[hh] = qk[:, 256 + 64 * hh:256 + 64 * hh + 64]
            v_o[hh] = vv[:, 128 * hh:128 * hh + 128].astype(BF16)
            gf_o[hh] = g[:, 64 * hh:64 * hh + 64]
            gb_o[hh] = g[:, 256 + 64 * hh:256 + 64 * hh + 64]

    h64 = pl.BlockSpec((GLA_H, tm, 64), lambda i: (0, i, 0))
    h128 = pl.BlockSpec((GLA_H, tm, 128), lambda i: (0, i, 0))
    return pl.pallas_call(
        body, name=name, grid=(Lp // tm,),
        in_specs=[pl.BlockSpec((tm, 512), lambda i: (i, C_QA // 512)), pl.BlockSpec((tm, 512), lambda i: (i, C_VA // 512)),
                  pl.BlockSpec((tm, LANE), lambda i: (i, C_LR // LANE)), pl.BlockSpec((LANE, 512), lambda i: (0, 0)),
                  pl.BlockSpec((1, 512), lambda i: (0, 0))],
        out_specs=[h64, h64, h128, h64, h64],
        out_shape=[SDS((GLA_H, Lp, 64), F32), SDS((GLA_H, Lp, 64), F32), SDS((GLA_H, Lp, 128), BF16),
                   SDS((GLA_H, Lp, 64), F32), SDS((GLA_H, Lp, 64), F32)])(H, H, H, w2p, b2p)


def _bdot(a, b, ca, cb, precision=None):
    return lax.dot_general(a, b, ((ca, cb), ((0,), (0,))), precision=precision, preferred_element_type=F32)


def _gla_chunk_terms(q_ref, k_ref, g_ref, v_ref, G, rev):
    B = GLA_H * G
    qv = q_ref[...].reshape(B, CHUNK, GLA_DK)
    kv = k_ref[...].reshape(B, CHUNK, GLA_DK)
    gv = g_ref[...].reshape(B, CHUNK, GLA_DK)
    vv = v_ref[...].reshape(B, CHUNK, GLA_DV)
    ii = lax.broadcasted_iota(jnp.int32, (CHUNK, CHUNK), 0)
    jj = lax.broadcasted_iota(jnp.int32, (CHUNK, CHUNK), 1)
    tri = (jj >= ii) if rev else (jj <= ii)
    tb = jnp.broadcast_to(tri.astype(F32)[None], (B, CHUNK, CHUNK))
    bc = _bdot(tb, gv, (2,), (1,), HIGHEST)
    bt = bc[:, 0:1, :] if rev else bc[:, CHUNK - 1:CHUNK, :]
    eq, eki, eke = jnp.exp(bc), jnp.exp(-bc), jnp.exp(bt - bc)
    qd, ki, ke = qv * eq, kv * eki, kv * eke
    att = jnp.where(tri[None], _bdot(qd.astype(BF16), ki.astype(BF16), (2,), (2,)), 0.0)
    dm = jnp.exp(_bdot(gv, jnp.ones((B, CHUNK, GLA_DV), F32), (1,), (1,), HIGHEST))
    return dict(B=B, vv=vv, tri=tri, tb=tb, bt=bt, eq=eq, eki=eki, eke=eke, qd=qd, ki=ki, ke=ke, att=att, dm=dm)


def _gla_fwd(name, q, k, g, v, rev, G):
    Lp = q.shape[1]
    tg = G * CHUNK
    ng = Lp // tg

    def body(q_ref, k_ref, g_ref, v_ref, o_ref, ss_ref, s_scr):
        @pl.when(pl.program_id(0) == 0)
        def _():
            s_scr[...] = jnp.zeros_like(s_scr)

        t = _gla_chunk_terms(q_ref, k_ref, g_ref, v_ref, G, rev)
        B, vv = t["B"], t["vv"]
        qd = t["qd"].astype(BF16)
        oi = _bdot(t["att"].astype(BF16), vv, (2,), (1,))
        kvc = _bdot(t["ke"].astype(BF16), vv, (1,), (1,)).reshape(GLA_H, G, GLA_DK, GLA_DV)
        dm = t["dm"].reshape(GLA_H, G, GLA_DK, GLA_DV)
        s = s_scr[...]
        sp = [None] * G
        for c in (range(G - 1, -1, -1) if rev else range(G)):
            sp[c] = s
            ss_ref[c] = s
            s = dm[:, c] * s + kvc[:, c]
        s_scr[...] = s
        spb = jnp.stack(sp, axis=1).reshape(B, GLA_DK, GLA_DV).astype(BF16)
        o_ref[...] = (oi + _bdot(qd, spb, (2,), (1,))).reshape(GLA_H, tg, GLA_DV)

    blk = (lambda i: (0, ng - 1 - i, 0)) if rev else (lambda i: (0, i, 0))
    sblk = (lambda i: (ng - 1 - i, 0, 0, 0)) if rev else (lambda i: (i, 0, 0, 0))
    h64 = pl.BlockSpec((GLA_H, tg, 64), blk)
    h128 = pl.BlockSpec((GLA_H, tg, 128), blk)
    return pl.pallas_call(
        body, name=name, grid=(ng,), in_specs=[h64, h64, h64, h128],
        out_specs=[h128, pl.BlockSpec((G, GLA_H, GLA_DK, GLA_DV), sblk)],
        out_shape=[SDS((GLA_H, Lp, GLA_DV), F32), SDS((Lp // CHUNK, GLA_H, GLA_DK, GLA_DV), F32)],
        scratch_shapes=[pltpu.VMEM((GLA_H, GLA_DK, GLA_DV), F32)], compiler_params=_cp(True))(q, k, g, v)


def _gla_bwd(name, q, k, g, v, ss, do, rev, G):
    Lp = q.shape[1]
    tg = G * CHUNK
    ng = Lp // tg

    def body(q_ref, k_ref, g_ref, v_ref, ss_ref, do_ref, dq_ref, dk_ref, dg_ref, dv_ref, ds_scr):
        @pl.when(pl.program_id(0) == 0)
        def _():
            ds_scr[...] = jnp.zeros_like(ds_scr)

        t = _gla_chunk_terms(q_ref, k_ref, g_ref, v_ref, G, rev)
        B, vv, tri = t["B"], t["vv"], t["tri"]
        qd, ki, ke = t["qd"], t["ki"], t["ke"]
        qdb, kib, keb = qd.astype(BF16), ki.astype(BF16), ke.astype(BF16)
        sp = jnp.stack([ss_ref[c] for c in range(G)], axis=1).reshape(B, GLA_DK, GLA_DV)
        dob = do_ref[...].reshape(B, CHUNK, GLA_DV).astype(BF16)
        da = jnp.where(tri[None], _bdot(dob, vv, (2,), (2,)), 0.0).astype(BF16)
        dqd = _bdot(da, kib, (2,), (1,)) + _bdot(dob, sp.astype(BF16), (2,), (2,))
        dki = _bdot(da, qdb, (1,), (1,))
        dv = _bdot(t["att"].astype(BF16), dob, (1,), (1,))
        cc = _bdot(qdb, dob, (1,), (1,)).reshape(GLA_H, G, GLA_DK, GLA_DV)
        dm = t["dm"].reshape(GLA_H, G, GLA_DK, GLA_DV)
        dsc = ds_scr[...]
        dsn = [None] * G
        for c in (range(G) if rev else range(G - 1, -1, -1)):
            dsn[c] = dsc
            dsc = dm[:, c] * dsc + cc[:, c]
        ds_scr[...] = dsc
        dsn = jnp.stack(dsn, axis=1).reshape(B, GLA_DK, GLA_DV)
        dsnb = dsn.astype(BF16)
        dv = dv + _bdot(keb, dsnb, (2,), (1,))
        dke = _bdot(vv, dsnb, (2,), (2,))
        ddrow = _bdot(jnp.ones((B, CHUNK, GLA_DV), F32), dsn * sp, (2,), (2,), HIGHEST)
        dbt = ddrow * jnp.exp(t["bt"]) + jnp.sum(dke * ke, axis=1, keepdims=True)
        db = dqd * qd - dki * ki - dke * ke
        dq_ref[...] = (dqd * t["eq"]).reshape(GLA_H, tg, GLA_DK)
        dk_ref[...] = (dki * t["eki"] + dke * t["eke"]).reshape(GLA_H, tg, GLA_DK)
        dg_ref[...] = (_bdot(t["tb"], db, (1,), (1,), HIGHEST) + dbt).reshape(GLA_H, tg, GLA_DK)
        dv_ref[...] = dv.reshape(GLA_H, tg, GLA_DV)

    blk = (lambda i: (0, i, 0)) if rev else (lambda i: (0, ng - 1 - i, 0))
    sblk = (lambda i: (i, 0, 0, 0)) if rev else (lambda i: (ng - 1 - i, 0, 0, 0))
    h64 = pl.BlockSpec((GLA_H, tg, 64), blk)
    h128 = pl.BlockSpec((GLA_H, tg, 128), blk)
    return pl.pallas_call(
        body, name=name, grid=(ng,),
        in_specs=[h64, h64, h64, h128, pl.BlockSpec((G, GLA_H, GLA_DK, GLA_DV), sblk), h128],
        out_specs=[h64, h64, h64, h128],
        out_shape=[SDS((GLA_H, Lp, 64), F32), SDS((GLA_H, Lp, 64), F32), SDS((GLA_H, Lp, 64), F32),
                   SDS((GLA_H, Lp, GLA_DV), F32)],
        scratch_shapes=[pltpu.VMEM((GLA_H, GLA_DK, GLA_DV), F32)], compiler_params=_cp(True))(q, k, g, v, ss, do)


def _gla_post(name, of, ob, H, gn):
    Lp = H.shape[0]
    tm = _rt(Lp, 384)

    def body(of_ref, ob_ref, r_ref, gn_ref, a_ref, at_ref):
        parts = []
        for hh in range(GLA_H):
            o = of_ref[hh] + ob_ref[hh]
            parts.append(o * lax.rsqrt(jnp.mean(o * o, axis=-1, keepdims=True) + EPS))
        rv = r_ref[...]
        a = (jnp.concatenate(parts, axis=1) * gn_ref[...]) * (rv * jax.nn.sigmoid(rv))
        a_ref[...] = a.astype(BF16)
        at_ref[...] = a.T.astype(BF16)

    h128 = pl.BlockSpec((GLA_H, tm, 128), lambda i: (0, i, 0))
    return pl.pallas_call(
        body, name=name, grid=(Lp // tm,),
        in_specs=[h128, h128, pl.BlockSpec((tm, 512), lambda i: (i, C_RA // 512)), pl.BlockSpec((1, 512), lambda i: (0, 0))],
        out_specs=[pl.BlockSpec((tm, 512), lambda i: (i, 0)), pl.BlockSpec((512, tm), lambda i: (0, i))],
        out_shape=[SDS((Lp, 512), BF16), SDS((512, Lp), BF16)])(of, ob, H, gn)


def _gla_post_bwd(name, of, ob, H, gn, da, dH):
    Lp = H.shape[0]
    tm = _rt(Lp, 384)

    def body(of_ref, ob_ref, r_ref, gn_ref, da_ref, dh_in, do_ref, dh_ref, dgn_ref):
        rv = r_ref[...]
        sg = jax.nn.sigmoid(rv)
        dav = da_ref[...]
        gnv = gn_ref[...]
        ons, rs = [], []
        for hh in range(GLA_H):
            o = of_ref[hh] + ob_ref[hh]
            r = lax.rsqrt(jnp.mean(o * o, axis=-1, keepdims=True) + EPS)
            rs.append(r)
            ons.append(o * r)
        on = jnp.concatenate(ons, axis=1)
        dw = dav * (rv * sg)
        dh_ref[...] = (dav * (on * gnv) * (sg * (1.0 + rv * (1.0 - sg)))).astype(BF16)

        @pl.when(pl.program_id(0) == 0)
        def _():
            dgn_ref[...] = jnp.zeros_like(dgn_ref)

        dgn_ref[...] += jnp.sum(dw * on, axis=0, keepdims=True)
        don = dw * gnv
        for hh in range(GLA_H):
            dd = don[:, 128 * hh:128 * hh + 128]
            do_ref[hh] = rs[hh] * (dd - ons[hh] * jnp.mean(dd * ons[hh], axis=-1, keepdims=True))

    h128 = pl.BlockSpec((GLA_H, tm, 128), lambda i: (0, i, 0))
    rblk = pl.BlockSpec((tm, 512), lambda i: (i, C_RA // 512))
    vec = pl.BlockSpec((1, 512), lambda i: (0, 0))
    return pl.pallas_call(
        body, name=name, grid=(Lp // tm,),
        in_specs=[h128, h128, rblk, vec, pl.BlockSpec((tm, 512), lambda i: (i, 0)), pl.BlockSpec(memory_space=pl.ANY)],
        out_specs=[h128, rblk, vec],
        out_shape=[SDS((GLA_H, Lp, 128), F32), SDS(dH.shape, BF16), SDS((1, 512), F32)],
        input_output_aliases={5: 1})(of, ob, H, gn, da, dH)


def _gla_qkv_bwd(name, dqf, dqb, dkf, dkb, dvf, dvb, dH):
    Lp = dqf.shape[1]
    tm = _rt(Lp, 384)

    def body(dqf_ref, dqb_ref, dkf_ref, dkb_ref, dvf_ref, dvb_ref, dh_in, dh_ref):
        valid = _row_ids(pl.program_id(0), tm) >= PAD
        for hh in range(GLA_H):
            dq = (dqf_ref[hh] + dqb_ref[hh]) * (GLA_DK ** -0.5)
            dh_ref[:, 64 * hh:64 * hh + 64] = jnp.where(valid, dq, 0.0).astype(BF16)
            dh_ref[:, 256 + 64 * hh:256 + 64 * hh + 64] = jnp.where(valid, dkf_ref[hh] + dkb_ref[hh], 0.0).astype(BF16)
            dh_ref[:, 512 + 128 * hh:512 + 128 * hh + 128] = jnp.where(valid, dvf_ref[hh] + dvb_ref[hh], 0.0).astype(BF16)

    h64 = pl.BlockSpec((GLA_H, tm, 64), lambda i: (0, i, 0))
    h128 = pl.BlockSpec((GLA_H, tm, 128), lambda i: (0, i, 0))
    return pl.pallas_call(
        body, name=name, grid=(Lp // tm,),
        in_specs=[h64, h64, h64, h64, h128, h128, pl.BlockSpec(memory_space=pl.ANY)],
        out_specs=pl.BlockSpec((tm, 1024), lambda i: (i, C_QA // 1024)), out_shape=SDS(dH.shape, BF16),
        input_output_aliases={6: 0})(dqf, dqb, dkf, dkb, dvf, dvb, dH)


def _gla_gate_bwd(name, H, w2p, b2p, dgf, dgb, dH):
    Lp = H.shape[0]
    tm = _rt(Lp, 384)

    def body(lr_ref, w_ref, b_ref, dgf_ref, dgb_ref, dh_in, dh_ref, dw_ref, db_ref, dg_scr):
        valid = _row_ids(pl.program_id(0), tm) >= PAD
        for hh in range(GLA_H):
            dg_scr[:, 64 * hh:64 * hh + 64] = dgf_ref[hh]
            dg_scr[:, 256 + 64 * hh:256 + 64 * hh + 64] = dgb_ref[hh]
        lrb = lr_ref[...].astype(BF16)
        wv = w_ref[...]
        pre = jnp.dot(lrb, wv, preferred_element_type=F32) + b_ref[...]
        dpre = jnp.where(valid, dg_scr[...] * (1.0 / GLA_TAU) * jax.nn.sigmoid(-pre), 0.0)
        dpb = dpre.astype(BF16)
        dh_ref[...] = lax.dot_general(dpb, wv, NT, preferred_element_type=F32).astype(BF16)

        @pl.when(pl.program_id(0) == 0)
        def _():
            dw_ref[...] = jnp.zeros_like(dw_ref)
            db_ref[...] = jnp.zeros_like(db_ref)

        dw_ref[...] += lax.dot_general(lrb, dpb, TN, preferred_element_type=F32)
        db_ref[...] += jnp.sum(dpre, axis=0, keepdims=True)

    h64 = pl.BlockSpec((GLA_H, tm, 64), lambda i: (0, i, 0))
    lrblk = pl.BlockSpec((tm, LANE), lambda i: (i, C_LR // LANE))
    wblk = pl.BlockSpec((LANE, 512), lambda i: (0, 0))
    vec = pl.BlockSpec((1, 512), lambda i: (0, 0))
    return pl.pallas_call(
        body, name=name, grid=(Lp // tm,),
        in_specs=[lrblk, wblk, vec, h64, h64, pl.BlockSpec(memory_space=pl.ANY)],
        out_specs=[lrblk, wblk, vec],
        out_shape=[SDS(dH.shape, BF16), SDS((LANE, 512), F32), SDS((1, 512), F32)],
        scratch_shapes=[pltpu.VMEM((tm, 512), F32)], input_output_aliases={5: 0})(H, w2p, b2p, dgf, dgb, dH)


def _swap16(x):
    n = x.shape[1]
    lane = lax.broadcasted_iota(jnp.int32, x.shape, 1)
    return jnp.where(lane % 32 < 16, pltpu.roll(x, n - 16, 1), pltpu.roll(x, 16, 1))


def _headnorm_rope(x, gain, cos, sin, bd):
    r = lax.rsqrt(jnp.dot(x * x, bd, precision=HIGHEST, preferred_element_type=F32) + EPS)
    xh = x * r
    xn = xh * gain
    return xn * cos + _swap16(xn) * sin, xh, r


def _headnorm_rope_bwd(dxr, xh, r, gain, cos, sin, bd):
    dxn = cos * dxr + _swap16(sin * dxr)
    u = dxn * gain
    dx = r * (u - xh * jnp.dot(u * xh, bd, precision=HIGHEST, preferred_element_type=F32))
    return dx, jnp.sum(dxn * xh, axis=0, keepdims=True)


def _att_prep(name, H, gq, gk, cos, sin, bd):
    Lp = H.shape[0]
    tm = _rt(Lp, 384)

    def body(q_ref, kv_ref, gq_ref, gk_ref, c_ref, s_ref, bd_ref, qp_ref, k_ref, v_ref):
        c1, s1 = c_ref[...], s_ref[...]
        c4, s4 = jnp.concatenate([c1] * 4, axis=1), jnp.concatenate([s1] * 4, axis=1)
        xr, _, _ = _headnorm_rope(q_ref[...], gq_ref[...], c4, s4, bd_ref[...])
        xr = xr * (HD ** -0.5)
        lane = lax.broadcasted_iota(jnp.int32, (tm, LANE), 1)
        for hh in range(ATT_H):
            grp = xr[:, LANE * (hh // 2):LANE * (hh // 2) + LANE]
            e, gi = hh % 2, hh // 4
            if e != gi:
                grp = pltpu.roll(grp, 64, 1)
            keep = (lane < 64) if gi == 0 else (lane >= 64)
            qp_ref[hh] = jnp.where(keep, grp, 0.0).astype(BF16)
        kv = kv_ref[...]
        kr, _, _ = _headnorm_rope(kv[:, :LANE], gk_ref[...], c1, s1, bd_ref[0:LANE, 0:LANE])
        k_ref[...] = kr.astype(BF16)
        v_ref[...] = kv[:, LANE:].astype(BF16)

    row128 = pl.BlockSpec((tm, LANE), lambda i: (i, 0))
    return pl.pallas_call(
        body, name=name, grid=(Lp // tm,),
        in_specs=[pl.BlockSpec((tm, 512), lambda i: (i, C_QB // 512)), pl.BlockSpec((tm, 256), lambda i: (i, C_KB // 256)),
                  pl.BlockSpec((1, 512), lambda i: (0, 0)), pl.BlockSpec((1, LANE), lambda i: (0, 0)), row128, row128,
                  pl.BlockSpec((512, 512), lambda i: (0, 0))],
        out_specs=[pl.BlockSpec((ATT_H, tm, LANE), lambda i: (0, i, 0)), row128, row128],
        out_shape=[SDS((ATT_H, Lp, LANE), BF16), SDS((Lp, LANE), BF16), SDS((Lp, LANE), BF16)])(H, H, gq, gk, cos, sin, bd)


def _att_q_bwd(name, H, gq, cos, sin, bd, dqp, dH):
    Lp = H.shape[0]
    tm = _rt(Lp, 384)

    def body(q_ref, gq_ref, c_ref, s_ref, bd_ref, dqp_ref, dh_in, dh_ref, dg_ref):
        c1, s1 = c_ref[...], s_ref[...]
        c4, s4 = jnp.concatenate([c1] * 4, axis=1), jnp.concatenate([s1] * 4, axis=1)
        gqv = gq_ref[...]
        _, xh, r = _headnorm_rope(q_ref[...], gqv, c4, s4, bd_ref[...])
        lane = lax.broadcasted_iota(jnp.int32, (tm, LANE), 1)
        groups = []
        for j in range(ATT_H // 2):
            pieces = []
            for e in range(2):
                hh = 2 * j + e
                piece = dqp_ref[hh]
                if e != hh // 4:
                    piece = pltpu.roll(piece, 64, 1)
                pieces.append(piece)
            groups.append(jnp.where(lane < 64, pieces[0], pieces[1]))
        dxr = jnp.concatenate(groups, axis=1) * (HD ** -0.5)
        dx, dg = _headnorm_rope_bwd(dxr, xh, r, gqv, c4, s4, bd_ref[...])
        dh_ref[...] = dx.astype(BF16)

        @pl.when(pl.program_id(0) == 0)
        def _():
            dg_ref[...] = jnp.zeros_like(dg_ref)

        dg_ref[...] += dg

    row128 = pl.BlockSpec((tm, LANE), lambda i: (i, 0))
    qblk = pl.BlockSpec((tm, 512), lambda i: (i, C_QB // 512))
    vec = pl.BlockSpec((1, 512), lambda i: (0, 0))
    return pl.pallas_call(
        body, name=name, grid=(Lp // tm,),
        in_specs=[qblk, vec, row128, row128, pl.BlockSpec((512, 512), lambda i: (0, 0)),
                  pl.BlockSpec((ATT_H, tm, LANE), lambda i: (0, i, 0)), pl.BlockSpec(memory_space=pl.ANY)],
        out_specs=[qblk, vec], out_shape=[SDS(dH.shape, BF16), SDS((1, 512), F32)],
        input_output_aliases={6: 0})(H, gq, cos, sin, bd, dqp, dH)


def _att_kv_bwd(name, H, gk, cos, sin, bd, dkr, dvb, dH):
    Lp = H.shape[0]
    tm = _rt(Lp, 384)

    def body(kv_ref, gk_ref, c_ref, s_ref, bd_ref, dk_ref, dv_ref, dh_in, dh_ref, dg_ref):
        c1, s1 = c_ref[...], s_ref[...]
        gkv = gk_ref[...]
        bdv = bd_ref[0:LANE, 0:LANE]
        _, xh, r = _headnorm_rope(kv_ref[:, :LANE], gkv, c1, s1, bdv)
        dx, dg = _headnorm_rope_bwd(dk_ref[...], xh, r, gkv, c1, s1, bdv)
        dh_ref[:, :LANE] = dx.astype(BF16)
        dh_ref[:, LANE:] = dv_ref[...].astype(BF16)

        @pl.when(pl.program_id(0) == 0)
        def _():
            dg_ref[...] = jnp.zeros_like(dg_ref)

        dg_ref[...] += dg

    row128 = pl.BlockSpec((tm, LANE), lambda i: (i, 0))
    kvblk = pl.BlockSpec((tm, 256), lambda i: (i, C_KB // 256))
    vec = pl.BlockSpec((1, LANE), lambda i: (0, 0))
    return pl.pallas_call(
        body, name=name, grid=(Lp // tm,),
        in_specs=[kvblk, vec, row128, row128, pl.BlockSpec((512, 512), lambda i: (0, 0)), row128, row128,
                  pl.BlockSpec(memory_space=pl.ANY)],
        out_specs=[kvblk, vec], out_shape=[SDS(dH.shape, BF16), SDS((1, LANE), F32)],
        input_output_aliases={7: 0})(H, gk, cos, sin, bd, dkr, dvb, dH)


def _ridden_call(core, rider, *, name, grid, in_specs, out_specs, out_shape, scratch_shapes, args):
    n_in, n_out, n_scr = len(in_specs), len(out_specs), len(scratch_shapes)
    r_in = len(rider.ins) if rider else 0
    r_out = len(rider.out_shape) if rider else 0
    total = int(np.prod(grid))
    mid_step = max(total - 1 - RIDER_MID_BACK, 0)

    def body(*refs):
        ins, r_ins = refs[:n_in], refs[n_in:n_in + r_in]
        o0 = n_in + r_in
        outs, r_outs = refs[o0:o0 + n_out], refs[o0 + n_out:o0 + n_out + r_out]
        s0 = o0 + n_out + r_out
        scr, r_sems = refs[s0:s0 + n_scr], refs[s0 + n_scr:]
        if rider is None:
            core(ins, outs, scr)
            return
        step = pl.program_id(0)
        for ax in range(1, len(grid)):
            step = step * grid[ax] + pl.program_id(ax)
        start, mid, end = rider.hooks(r_ins, r_outs, r_sems)
        pl.when(step == 0)(start)
        core(ins, outs, scr)
        pl.when(step == mid_step)(mid)
        pl.when(step == total - 1)(end)

    any_spec = pl.BlockSpec(memory_space=pl.ANY)
    res = pl.pallas_call(
        body, name=name, grid=grid, in_specs=list(in_specs) + [any_spec] * r_in, out_specs=list(out_specs) + [any_spec] * r_out,
        out_shape=list(out_shape) + (list(rider.out_shape) if rider else []),
        scratch_shapes=list(scratch_shapes) + (list(rider.sems) if rider else []),
        input_output_aliases={n_in + a: n_out + b for a, b in rider.aliases.items()} if rider else {},
        compiler_params=_cp(True))(*args, *(rider.ins if rider else []))
    return res[:n_out], res[n_out:]


def _att_fwd(name, qp, kr, vb, bias, rider=None):
    Lp = kr.shape[0]
    tq = _rt(Lp, 384)

    def core(ins, outs, scr):
        q_ref, k_ref, v_ref, bias_ref = ins
        b_ref, bt_ref, lse_ref = outs
        j, i = pl.program_id(0), pl.program_id(1)
        valid = _row_ids(i, tq) >= PAD
        kk, vv, bb = k_ref[...], v_ref[...], bias_ref[...]
        outs = []
        for e in range(2):
            s = lax.dot_general(q_ref[e], kk, NT, preferred_element_type=F32) + bb
            m = jnp.max(s, axis=-1, keepdims=True)
            p = jnp.exp(s - m)
            l = jnp.sum(p, axis=-1, keepdims=True)
            p = p * (1.0 / l)
            o = jnp.dot(p.astype(BF16), vv, preferred_element_type=F32)
            outs.append(jnp.where(valid, o, 0.0))
            lse_ref[e] = m + jnp.log(l)
        lane = lax.broadcasted_iota(jnp.int32, (tq, LANE), 1)
        low = j < 2
        o0 = jnp.where(low, outs[0], pltpu.roll(outs[0], 64, 1))
        o1 = jnp.where(low, pltpu.roll(outs[1], 64, 1), outs[1])
        blk = jnp.where(lane < 64, o0, o1)
        b_ref[...] = blk.astype(BF16)
        bt_ref[...] = blk.T.astype(BF16)

    full = pl.BlockSpec((Lp, LANE), lambda j, i: (0, 0))
    return _ridden_call(
        core, rider, name=name, grid=(ATT_H // 2, Lp // tq),
        in_specs=[pl.BlockSpec((2, tq, LANE), lambda j, i: (j, i, 0)), full, full, pl.BlockSpec((1, Lp), lambda j, i: (0, 0))],
        out_specs=[pl.BlockSpec((tq, LANE), lambda j, i: (i, j)), pl.BlockSpec((LANE, tq), lambda j, i: (j, i)),
                   pl.BlockSpec((2, tq, 1), lambda j, i: (j, i, 0))],
        out_shape=[SDS((Lp, 512), BF16), SDS((512, Lp), BF16), SDS((ATT_H, Lp, 1), F32)],
        scratch_shapes=[], args=(qp, kr, vb, bias))


def _att_bwd(name, qp, kr, vb, bias, lse, db, rider=None):
    Lp = kr.shape[0]
    tq = _rt(Lp, 384)
    nq = Lp // tq

    def core(ins, outs, scr):
        q_ref, k_ref, v_ref, bias_ref, lse_ref, db_ref = ins
        dq_ref, dk_ref, dv_ref = outs
        dkt_scr, dvt_scr = scr
        j, i = pl.program_id(0), pl.program_id(1)

        @pl.when((j == 0) & (i == 0))
        def _():
            dkt_scr[...] = jnp.zeros_like(dkt_scr)
            dvt_scr[...] = jnp.zeros_like(dvt_scr)

        kk, vv, bb = k_ref[...], v_ref[...], bias_ref[...]
        dbv = db_ref[...]
        rolled = pltpu.roll(dbv, 64, 1)
        lane = lax.broadcasted_iota(jnp.int32, (tq, LANE), 1)
        low = j < 2
        first = jnp.where(low, 0, 64)
        keep = (lane >= first) & (lane < first + 64)
        for e in range(2):
            qe = q_ref[e]
            s = lax.dot_general(qe, kk, NT, preferred_element_type=F32) + bb
            p = jnp.exp(s - lse_ref[e])
            src = jnp.where(low, dbv, rolled) if e == 0 else jnp.where(low, rolled, dbv)
            dop = jnp.where(keep, src, 0.0).astype(BF16)
            dp = lax.dot_general(dop, vv, NT, preferred_element_type=F32)
            ds = (p * (dp - jnp.sum(p * dp, axis=-1, keepdims=True))).astype(BF16)
            dq_ref[e] = jnp.where(keep, jnp.dot(ds, kk, preferred_element_type=F32), 0.0)
            dkt_scr[...] += lax.dot_general(qe, ds, TN, preferred_element_type=F32)
            dvt_scr[...] += lax.dot_general(dop, p.astype(BF16), TN, preferred_element_type=F32)

        @pl.when((j == ATT_H // 2 - 1) & (i == nq - 1))
        def _():
            dk_ref[...] = dkt_scr[...].T
            dv_ref[...] = dvt_scr[...].T

    full = pl.BlockSpec((Lp, LANE), lambda j, i: (0, 0))
    pair = pl.BlockSpec((2, tq, LANE), lambda j, i: (j, i, 0))
    return _ridden_call(
        core, rider, name=name, grid=(ATT_H // 2, nq),
        in_specs=[pair, full, full, pl.BlockSpec((1, Lp), lambda j, i: (0, 0)), pl.BlockSpec((2, tq, 1), lambda j, i: (j, i, 0)),
                  pl.BlockSpec((tq, LANE), lambda j, i: (i, j))],
        out_specs=[pair, full, full],
        out_shape=[SDS((ATT_H, Lp, LANE), F32), SDS((Lp, LANE), F32), SDS((Lp, LANE), F32)],
        scratch_shapes=[pltpu.VMEM((LANE, Lp), F32), pltpu.VMEM((LANE, Lp), F32)], args=(qp, kr, vb, bias, lse, db))


def _rope_tables(n_tok):
    rows = n_tok // GRID_W
    row = jnp.repeat(jnp.arange(rows), GRID_W).astype(F32)
    col = jnp.tile(jnp.arange(GRID_W), rows).astype(F32)
    axis_dim = HD // 2
    inv = ROPE_THETA ** (-jnp.arange(0, axis_dim, 2, dtype=F32) / axis_dim)
    ang = jnp.concatenate([row[:, None] * inv, col[:, None] * inv], axis=-1)
    ang = jnp.concatenate([jnp.zeros((LANE, axis_dim), F32), ang], axis=0)
    c, s = jnp.cos(ang), jnp.sin(ang)
    c64 = jnp.concatenate([c[:, :16], c[:, :16], c[:, 16:], c[:, 16:]], axis=1)
    s64 = jnp.concatenate([-s[:, :16], s[:, :16], -s[:, 16:], s[:, 16:]], axis=1)
    return jnp.concatenate([c64, c64], axis=1), jnp.concatenate([s64, s64], axis=1)


def _ffn_fwd(tag, h, W, l, j):
    Lp = h.shape[0]
    tm = _rt(Lp, 384)
    ni = Lp // tm
    g = W["norm_gains"][l, 2 * j].reshape(1, D)
    last = NSH - 1

    def body(h_ref, g_ref, wg_ref, wu_ref, wd_ref, h2_ref, a_ref, b_ref, st_ref, nt_ref, n_scr, acc_scr, pad_scr):
        s, i = pl.program_id(0), pl.program_id(1)
        rows = pl.ds(pl.multiple_of(i * tm, tm), tm)

        @pl.when(s == 0)
        def _():
            x = h_ref[...]
            y = x * lax.rsqrt(jnp.mean(x * x, axis=-1, keepdims=True) + EPS) * g_ref[...]
            n_scr[rows, :] = y.astype(BF16)
            nt_ref[...] = y.T.astype(BF16)

        nv = n_scr[rows, :]
        a = jnp.dot(nv, wg_ref[...], preferred_element_type=F32)
        b = jnp.dot(nv, wu_ref[...], preferred_element_type=F32)
        sv = a * jax.nn.sigmoid(a) * b
        a_ref[:, :FF_S] = a.astype(BF16)
        a_ref[:, FF_S:] = jnp.zeros((tm, FF_P - FF_S), BF16)
        b_ref[:, :FF_S] = b.astype(BF16)
        b_ref[:, FF_S:] = jnp.zeros((tm, FF_P - FF_S), BF16)
        pad_scr[:, :FF_S] = sv
        pad_scr[:, FF_S:] = jnp.zeros((tm, FF_P - FF_S), F32)
        st_ref[...] = pad_scr[...].T.astype(BF16)
        part = jnp.dot(sv.astype(BF16), wd_ref[...], preferred_element_type=F32)

        @pl.when(s == 0)
        def _():
            acc_scr[rows, :] = part

        @pl.when(s > 0)
        def _():
            acc_scr[rows, :] += part

        @pl.when(s == last)
        def _():
            h2_ref[...] = h_ref[...] + 0.5 * acc_scr[rows, :]

    slab = pl.BlockSpec((None, tm, FF_P), lambda s, i: (s, i, 0))
    wup = pl.BlockSpec((None, None, D, FF_S), lambda s, i: (s, l, j, 0))
    h2, a, b, st, nt = pl.pallas_call(
        body, name=f"ffn_{tag}", grid=(NSH, ni),
        in_specs=[pl.BlockSpec((tm, D), lambda s, i: (jnp.where((s == 0) | (s == last), i, ni - 1), 0)),
                  pl.BlockSpec((1, D), lambda s, i: (0, 0)), wup, wup,
                  pl.BlockSpec((None, None, FF_S, D), lambda s, i: (s, l, j, 0))],
        out_specs=[pl.BlockSpec((tm, D), lambda s, i: (jnp.where(s == last, i, 0), 0)), slab, slab,
                   pl.BlockSpec((None, FF_P, tm), lambda s, i: (s, 0, i)),
                   pl.BlockSpec((D, tm), lambda s, i: (0, jnp.where(s == 0, i, ni - 1)))],
        out_shape=[SDS((Lp, D), F32), SDS((NSH, Lp, FF_P), BF16), SDS((NSH, Lp, FF_P), BF16), SDS((NSH, FF_P, Lp), BF16),
                   SDS((D, Lp), BF16)],
        scratch_shapes=[pltpu.VMEM((Lp, D), BF16), pltpu.VMEM((Lp, D), F32), pltpu.VMEM((tm, FF_P), F32)],
        compiler_params=_cp(True))(h, g, W["gate"], W["up"], W["down"])
    return h2, dict(h=h, g=g, nt=nt, a=a, b=b, st=st)


def _ffn_bwd(tag, dh, sv, W, G, l, j):
    Lp = dh.shape[0]
    tn = 512
    da, db, dh_prev, dg = _ffn_bwd_acts(tag, dh, sv, W, l, j)
    G["down"] = _mm(f"bwd_{tag}", sv["st"], dh, grid=(NSH, D // tn),
                    a_spec=pl.BlockSpec((None, FF_P, Lp), lambda s, n: (s, 0, 0)), a_sl=(slice(0, FF_S), slice(None)),
                    b_spec=pl.BlockSpec((Lp, tn), lambda s, n: (0, n)),
                    o_spec=pl.BlockSpec((None, None, FF_S, tn), lambda s, n: (s, j, 0, n)), o_shape=(NSH, 2, FF_S, D), o_dtype=F32,
                    dims=NN, acc_shape=None, scale=0.5, into=G.get("down"))
    for key, dact in (("gate", da), ("up", db)):
        G[key] = _mm(f"bw{key[0]}_{tag}", sv["nt"], dact, grid=(NSH, D // tn),
                     a_spec=pl.BlockSpec((tn, Lp), lambda s, m: (m, 0)),
                     b_spec=pl.BlockSpec((None, Lp, FF_P), lambda s, m: (s, 0, 0)), b_sl=(slice(None), slice(0, FF_S)),
                     o_spec=pl.BlockSpec((None, None, tn, FF_S), lambda s, m: (s, j, m, 0)), o_shape=(NSH, 2, D, FF_S),
                     o_dtype=F32, dims=NN, acc_shape=None, into=G.get(key))
    return dh_prev, dg


def _ffn_bwd_acts(tag, dh, sv, W, l, j):
    Lp = dh.shape[0]
    tm = _rt(Lp, 384)
    last = NSH - 1

    def body(dh_ref, h_ref, g_ref, wd_ref, wg_ref, wu_ref, a_ref, b_ref, da_ref, db_ref, dho_ref, dg_ref, acc_scr):
        s, i = pl.program_id(0), pl.program_id(1)
        rows = pl.ds(pl.multiple_of(i * tm, tm), tm)
        dhv = dh_ref[...]
        ds = 0.5 * lax.dot_general(dhv.astype(BF16), wd_ref[...], NT, preferred_element_type=F32)
        av = a_ref[:, :FF_S].astype(F32)
        bv = b_ref[:, :FF_S].astype(F32)
        sg = jax.nn.sigmoid(av)
        da = (ds * bv * (sg * (1.0 + av * (1.0 - sg)))).astype(BF16)
        db = (ds * (av * sg)).astype(BF16)
        da_ref[:, :FF_S] = da
        da_ref[:, FF_S:] = jnp.zeros((tm, FF_P - FF_S), BF16)
        db_ref[:, :FF_S] = db
        db_ref[:, FF_S:] = jnp.zeros((tm, FF_P - FF_S), BF16)
        part = (lax.dot_general(da, wg_ref[...], NT, preferred_element_type=F32)
                + lax.dot_general(db, wu_ref[...], NT, preferred_element_type=F32))

        @pl.when(s == 0)
        def _():
            acc_scr[rows, :] = part

        @pl.when(s > 0)
        def _():
            acc_scr[rows, :] += part

        @pl.when((s == 0) & (i == 0))
        def _():
            dg_ref[...] = jnp.zeros_like(dg_ref)

        @pl.when(s == last)
        def _():
            x = h_ref[...]
            r = lax.rsqrt(jnp.mean(x * x, axis=-1, keepdims=True) + EPS)
            xh = x * r
            dn = acc_scr[rows, :]
            u = dn * g_ref[...]
            dho_ref[...] = dhv + r * (u - xh * jnp.mean(u * xh, axis=-1, keepdims=True))
            dg_ref[...] += jnp.sum(dn * xh, axis=0, keepdims=True)

    slab = pl.BlockSpec((None, tm, FF_P), lambda s, i: (s, i, 0))
    wup = pl.BlockSpec((None, None, D, FF_S), lambda s, i: (s, l, j, 0))
    at_last = pl.BlockSpec((tm, D), lambda s, i: (jnp.where(s == last, i, 0), 0))
    vec = pl.BlockSpec((1, D), lambda s, i: (0, 0))
    return pl.pallas_call(
        body, name=f"bffn_{tag}", grid=(NSH, Lp // tm),
        in_specs=[pl.BlockSpec((tm, D), lambda s, i: (i, 0)), at_last, vec,
                  pl.BlockSpec((None, None, FF_S, D), lambda s, i: (s, l, j, 0)), wup, wup, slab, slab],
        out_specs=[slab, slab, at_last, vec],
        out_shape=[SDS((NSH, Lp, FF_P), BF16), SDS((NSH, Lp, FF_P), BF16), SDS((Lp, D), F32), SDS((1, D), F32)],
        scratch_shapes=[pltpu.VMEM((Lp, D), F32)], compiler_params=_cp(True))(
            dh, sv["h"], sv["g"], W["down"], W["gate"], W["up"], sv["a"], sv["b"])


def _mixer_fwd(tag, h, W, winp, C, l, G_chunks, rider=None):
    Lp = h.shape[0]
    tm = _rt(Lp, 1408)
    g = W["norm_gains"][l, 1].reshape(1, D)
    z, zt = _rms_fwd(f"rmsf_{tag}", h, g)
    tn = 896
    H = _mm(f"win_{tag}", z, winp, grid=(HP // tn, Lp // tm), a_spec=pl.BlockSpec((tm, D), lambda n, i: (i, 0)),
            b_spec=pl.BlockSpec((D, tn), lambda n, i: (0, n)), o_spec=pl.BlockSpec((tm, tn), lambda n, i: (i, n)),
            o_shape=(Lp, HP), o_dtype=F32, dims=NN, acc_shape=None)
    w2p, b2p = C["w2p"][l], C["b2p"][l]
    qh, kh, vh, gf, gb = _gla_prep(f"glap_{tag}", H, w2p, b2p)
    of, sf = _gla_fwd(f"glaf_{tag}", qh, kh, gf, vh, False, G_chunks)
    ob, sb = _gla_fwd(f"glar_{tag}", qh, kh, gb, vh, True, G_chunks)
    gn = W["gn"][l].reshape(1, 512)
    a, at = _gla_post(f"glao_{tag}", of, ob, H, gn)
    gq, gk = C["gq"][l], C["gk"][l]
    qp, kr, vb = _att_prep(f"attp_{tag}", H, gq, gk, C["cos"], C["sin"], C["bd"])
    (b, bt, lse), carried = _att_fwd(f"attf_{tag}", qp, kr, vb, C["bias"], rider)
    if rider is not None:
        W = rider.updated(W, carried)
    proj = functools.partial(
        _mm, grid=(NSH, Lp // tm), a_spec=pl.BlockSpec((tm, 512), lambda s, i: (i, 0)),
        b_spec=pl.BlockSpec((None, None, 512, 256), lambda s, i: (s, l, 0, 0)), o_spec=pl.BlockSpec((tm, 256), lambda s, i: (i, s)),
        o_shape=(Lp, D), o_dtype=F32, dims=NN, acc_shape=None)
    pa = proj(f"pa_{tag}", a, W["wpa"])
    pb = proj(f"pb_{tag}", b, W["wpb"])
    bm = W["bm"][l]
    y, yt = _merge_fwd(f"mrg_{tag}", H, pa, pb, bm)
    row = pl.BlockSpec((tm, D), lambda i, s: (i, 0))
    h2 = _mm(f"wout_{tag}", y, W["wout"], grid=(Lp // tm, NSH), a_spec=pl.BlockSpec((tm, 256), lambda i, s: (i, s)),
             b_spec=pl.BlockSpec((None, None, 256, D), lambda i, s: (s, l, 0, 0)), o_spec=row, o_shape=(Lp, D), o_dtype=F32,
             dims=NN, acc_shape=(tm, D), nk=NSH, res=h, res_spec=row)
    sv = dict(h=h, g=g, zt=zt, H=H, w2p=w2p, b2p=b2p, qh=qh, kh=kh, vh=vh, gf=gf, gb=gb, of=of, ob=ob, sf=sf, sb=sb, gn=gn,
              at=at, gq=gq, gk=gk, qp=qp, kr=kr, vb=vb, bt=bt, lse=lse, pa=pa, pb=pb, bm=bm, yt=yt, winp=winp)
    return h2, sv, W


def _mixer_bwd(tag, dh, sv, W, C, G, S, l, G_chunks, rider=None):
    Lp = dh.shape[0]
    tm = _rt(Lp, 1408)
    H = sv["H"]
    dy = _mm(f"bdy_{tag}", dh, W["wout"], grid=(NSH, Lp // tm), a_spec=pl.BlockSpec((tm, D), lambda s, i: (i, 0)),
             b_spec=pl.BlockSpec((None, None, 256, D), lambda s, i: (s, l, 0, 0)), o_spec=pl.BlockSpec((tm, 256), lambda s, i: (i, s)),
             o_shape=(Lp, D), o_dtype=F32, dims=NT, acc_shape=None)
    tn = 512
    G["wout"] = _mm(f"bwo_{tag}", sv["yt"], dh, grid=(NSH, D // tn), a_spec=pl.BlockSpec((256, Lp), lambda s, n: (s, 0)),
                    b_spec=pl.BlockSpec((Lp, tn), lambda s, n: (0, n)),
                    o_spec=pl.BlockSpec((None, 256, tn), lambda s, n: (s, 0, n)), o_shape=(NSH, 256, D), o_dtype=F32, dims=NN,
                    acc_shape=None)
    dpa, dpb, dH, S["bm"][l] = _merge_bwd(f"bmrg_{tag}", H, sv["pa"], sv["pb"], sv["bm"], dy)
    dbranch = {}
    for key, dp, xt in (("wpa", dpa, sv["at"]), ("wpb", dpb, sv["bt"])):
        dbranch[key] = _mm(f"bx{key[2]}_{tag}", dp, W[key], grid=(Lp // tm, NSH),
                           a_spec=pl.BlockSpec((tm, 256), lambda i, s: (i, s)),
                           b_spec=pl.BlockSpec((None, None, 512, 256), lambda i, s: (s, l, 0, 0)),
                           o_spec=pl.BlockSpec((tm, 512), lambda i, s: (i, 0)), o_shape=(Lp, 512), o_dtype=F32, dims=NT,
                           acc_shape=(tm, 512), nk=NSH)
        G[key] = _mm(f"bw{key[2]}_{tag}", xt, dp, grid=(NSH,), a_spec=pl.BlockSpec((512, Lp), lambda s: (0, 0)),
                     b_spec=pl.BlockSpec((Lp, 256), lambda s: (0, s)),
                     o_spec=pl.BlockSpec((None, 512, 256), lambda s: (s, 0, 0)), o_shape=(NSH, 512, 256), o_dtype=F32, dims=NN,
                     acc_shape=None)
    (dqp, dkr, dvb), carried = _att_bwd(f"attb_{tag}", sv["qp"], sv["kr"], sv["vb"], C["bias"], sv["lse"], dbranch["wpb"],
                                        rider)
    dH, S["gq"][l] = _att_q_bwd(f"attq_{tag}", H, sv["gq"], C["cos"], C["sin"], C["bd"], dqp, dH)
    dH, S["gk"][l] = _att_kv_bwd(f"attk_{tag}", H, sv["gk"], C["cos"], C["sin"], C["bd"], dkr, dvb, dH)
    do, dH, S["gn"][l] = _gla_post_bwd(f"bglo_{tag}", sv["of"], sv["ob"], H, sv["gn"], dbranch["wpa"], dH)
    dqf, dkf, dgf, dvf = _gla_bwd(f"bglf_{tag}", sv["qh"], sv["kh"], sv["gf"], sv["vh"], sv["sf"], do, False, G_chunks)
    dqb, dkb, dgb, dvr = _gla_bwd(f"bglr_{tag}", sv["qh"], sv["kh"], sv["gb"], sv["vh"], sv["sb"], do, True, G_chunks)
    dH = _gla_qkv_bwd(f"bglq_{tag}", dqf, dqb, dkf, dkb, dvf, dvr, dH)
    dH, S["w2p"][l], S["b2p"][l] = _gla_gate_bwd(f"bglg_{tag}", H, sv["w2p"], sv["b2p"], dgf, dgb, dH)
    tmm = 256
    G["winp"] = _mm(f"bwi_{tag}", sv["zt"], dH, grid=(D // tmm, HP // 896), a_spec=pl.BlockSpec((tmm, Lp), lambda m, n: (m, 0)),
                    b_spec=pl.BlockSpec((Lp, 896), lambda m, n: (0, n)), o_spec=pl.BlockSpec((tmm, 896), lambda m, n: (m, n)),
                    o_shape=(D, HP), o_dtype=F32, dims=NN, acc_shape=None)
    dz = _mm(f"bdz_{tag}", dH, sv["winp"], grid=(D // 512, Lp // tm), a_spec=pl.BlockSpec((tm, HP), lambda n, i: (i, 0)),
             b_spec=pl.BlockSpec((512, HP), lambda n, i: (n, 0)), o_spec=pl.BlockSpec((tm, 512), lambda n, i: (i, n)),
             o_shape=(Lp, D), o_dtype=F32, dims=NT, acc_shape=None)
    dh_prev, dg = _rms_bwd(f"rmsb_{tag}", dz, sv["h"], sv["g"], dh)
    return dh_prev, dg, carried


def _winp_layer(win_g, l):
    return _win_to_padded(jnp.transpose(win_g[:, l], (1, 0, 2)).reshape(D, D_IN))


def _local_step(x2, tgt2, W, comm=None):
    n_tok = x2.shape[0]
    Lp = n_tok + LANE
    nc = Lp // CHUNK
    g_chunks = max(g for g in (1, 2, 3, 6) if nc % g == 0)
    cos, sin = _rope_tables(n_tok)
    bd = jnp.asarray(np.kron(np.eye(ATT_H, dtype=np.float32), np.full((HD, HD), 1.0 / HD, np.float32)))
    bias = jnp.where(jnp.arange(Lp) >= PAD, 0.0, -1e30).astype(F32).reshape(1, Lp)
    w2, b2 = W["w2"], W["b2"]
    w2p = jnp.zeros((DEPTH, LANE, 512), F32)
    w2p = w2p.at[:, 0:GLA_RANK, 0:256].set(w2[:, 0]).at[:, GLA_RANK:2 * GLA_RANK, 256:512].set(w2[:, 1]).astype(BF16)
    C = dict(cos=cos, sin=sin, bd=bd, bias=bias, w2p=w2p, b2p=b2.reshape(DEPTH, 1, 512),
             gq=jnp.tile(W["qn"], (1, ATT_H)).reshape(DEPTH, 1, 512), gk=jnp.tile(W["kn"], (1, ATT_KV)).reshape(DEPTH, 1, LANE))
    h = jnp.concatenate([jnp.zeros((PAD, D), F32), W["meta"], x2], axis=0)
    saved = []
    for l in range(DEPTH):
        h, s0 = _ffn_fwd(f"l{l}a", h, W, l, 0)
        rider = comm.fwd_rider(W, l) if comm else None
        h, sm, W = _mixer_fwd(f"l{l}m", h, W, _winp_layer(W["win"], l), C, l, g_chunks, rider)
        h, s1 = _ffn_fwd(f"l{l}b", h, W, l, 1)
        saved.append((s0, sm, s1))
    dh, dfin, loss = _loss_head("loss_head", h, W["fin"].reshape(1, D), tgt2)
    S = dict(bm=[None] * DEPTH, gq=[None] * DEPTH, gk=[None] * DEPTH, gn=[None] * DEPTH, w2p=[None] * DEPTH,
             b2p=[None] * DEPTH, ng=[[None] * 3 for _ in range(DEPTH)])
    Gs = [None] * DEPTH
    for l in reversed(range(DEPTH)):
        s0, sm, s1 = saved[l]
        G = {}
        dh, S["ng"][l][2] = _ffn_bwd(f"l{l}b", dh, s1, W, G, l, 1)
        rider = comm.bwd_rider(l) if comm else None
        dh, S["ng"][l][1], carried = _mixer_bwd(f"l{l}m", dh, sm, W, C, G, S, l, g_chunks, rider)
        if rider is not None:
            comm.bwd_carried(l, carried)
        dh, S["ng"][l][0] = _ffn_bwd(f"l{l}a", dh, s0, W, G, l, 0)
        Gs[l] = G
        if comm:
            comm.layer_done(l, G)
    small = dict(
        meta=dh[PAD:LANE],
        norm_gains=jnp.stack([jnp.concatenate(S["ng"][l], axis=0) for l in range(DEPTH)]),
        w2=jnp.stack([jnp.stack([S["w2p"][l][0:GLA_RANK, 0:256], S["w2p"][l][GLA_RANK:2 * GLA_RANK, 256:512]]) for l in range(DEPTH)]),
        b2=jnp.stack([S["b2p"][l].reshape(2, 256) for l in range(DEPTH)]),
        gn=jnp.concatenate(S["gn"], axis=0),
        qn=jnp.stack([S["gq"][l].reshape(ATT_H, HD).sum(0) for l in range(DEPTH)]),
        kn=jnp.stack([S["gk"][l].reshape(ATT_KV, HD).sum(0) for l in range(DEPTH)]),
        bm=jnp.stack(S["bm"]),
        fin=dfin.reshape(D),
    )
    return loss[0, 0], dh, Gs, small


def _win_to_padded(w):
    pad = jnp.zeros(w.shape[:-1] + (HP - D_IN,), w.dtype)
    return jnp.concatenate([w[..., 2336:4384], w[..., 0:1536], w[..., 1568:2336], w[..., 1536:1568], pad], axis=-1)


def _win_from_padded(w):
    return jnp.concatenate([w[..., 2048:3584], w[..., 4352:4384], w[..., 3584:4352], w[..., 0:2048]], axis=-1)


def _assemble(g):
    W = dict(gate=g["gate"], up=g["up"], down=g["down"], win=g["win"], wpa=g["wpa"], wpb=g["wpb"], wout=g["wout"])
    sm = g["small"]
    parts = _unpack(sm, SHARDED_SMALL)
    W["meta"] = jnp.transpose(parts["meta"], (1, 0, 2)).reshape(N_META, D)
    W["norm_gains"] = jnp.transpose(parts["norm_gains"], (1, 2, 0, 3)).reshape(DEPTH, 3, D)
    W["w2"] = jnp.transpose(parts["w2"], (1, 2, 3, 0, 4)).reshape(DEPTH, 2, GLA_RANK, 256)
    W["b2"] = jnp.transpose(parts["b2"], (1, 2, 0, 3)).reshape(DEPTH, 2, 256)
    W["bm"] = jnp.transpose(parts["bm"], (1, 2, 0, 3)).reshape(DEPTH, 2, D)
    return W


SHARDED_SMALL = dict(meta=(N_META, 256), norm_gains=(DEPTH, 3, 256), w2=(DEPTH, 2, GLA_RANK, 64), b2=(DEPTH, 2, 64),
                     bm=(DEPTH, 2, 256))
FULL_SMALL = dict(meta=(N_META, D), norm_gains=(DEPTH, 3, D), w2=(DEPTH, 2, GLA_RANK, 256), b2=(DEPTH, 2, 256),
                  gn=(DEPTH, 512), qn=(DEPTH, HD), kn=(DEPTH, HD), bm=(DEPTH, 2, D), fin=(D,))


def _pack(parts, table, rows):
    flat = jnp.concatenate([parts[k].reshape(-1).astype(F32) for k in table])
    return jnp.pad(flat, (0, rows * LANE - flat.shape[0])).reshape(rows, LANE)


def _unpack(packed, table):
    lead = packed.shape[:-2]
    flat = packed.reshape(lead + (-1,))
    out, off = {}, 0
    for k, shp in table.items():
        n = int(np.prod(shp))
        out[k] = flat[..., off:off + n].reshape(lead + tuple(shp))
        off += n
    return out


def _rows_for(table, mult):
    n = sum(int(np.prod(s)) for s in table.values())
    return -(-n // (LANE * mult)) * mult


SMALL_ROWS = _rows_for(SHARDED_SMALL, 16)
GRAD_ROWS = _rows_for(dict(FULL_SMALL, loss=(1,)), 8)


def _place():
    x, y, c = lax.axis_index("x"), lax.axis_index("y"), lax.axis_index("c")
    return x, y, c


def _other_chips(x, y):
    return [(1 - x, y), (x, 1 - y), (1 - x, 1 - y)]


def _cast_place(name, w3, slot):
    A, R, Cc = w3.shape
    tr = _rt(R, 512, 16)

    def body(p_ref, w_ref, o_ref):
        o_ref[...] = w_ref[...].astype(BF16)

    return pl.pallas_call(
        body, name=name,
        grid_spec=pltpu.PrefetchScalarGridSpec(
            num_scalar_prefetch=1, grid=(A, R // tr),
            in_specs=[pl.BlockSpec((None, tr, Cc), lambda a, r, p_ref: (a, r, 0))],
            out_specs=pl.BlockSpec((None, None, tr, Cc), lambda a, r, p_ref: (p_ref[0], a, r, 0))),
        out_shape=SDS((NSH, A, R, Cc), BF16))(slot, w3)


class _GatherRider:
    def __init__(self, keys, bufs, layers):
        n = len(bufs)
        self.keys, self.ins, self.layers = keys, list(bufs), layers
        self.out_shape = [SDS(b.shape, b.dtype) for b in bufs]
        self.aliases = {a: a for a in range(n)}
        self.sems = [pltpu.SemaphoreType.DMA((n, 3)) for _ in range(4)]

    def updated(self, W, carried):
        return {**W, **dict(zip(self.keys, carried))}

    def hooks(self, ins, outs, sems):
        send, recv, fsend, frecv = sems
        n = len(outs)

        def rows(a, slot, core):
            ref, l = outs[a], self.layers[a]
            half = ref.shape[-2] // 2
            return ref.at[slot, pl.ds(core * half, half)] if l is None else ref.at[slot, l, pl.ds(core * half, half)]

        def ici(a, k, part, px, py, c):
            return pltpu.make_async_remote_copy(src_ref=part, dst_ref=part, send_sem=send.at[a, k], recv_sem=recv.at[a, k],
                                                device_id=(px, py, c), device_id_type=MESH)

        def d2d(a, k, part, x, y, c):
            return pltpu.make_async_remote_copy(src_ref=part, dst_ref=part, send_sem=fsend.at[a, k], recv_sem=frecv.at[a, k],
                                                device_id=(x, y, 1 - c), device_id_type=MESH)

        def start():
            x, y, c = _place()
            for a in range(n):
                for k, (px, py) in enumerate(_other_chips(x, y)):
                    ici(a, k, rows(a, 2 * x + y, c), px, py, c).start()

        def mid():
            x, y, c = _place()
            for a in range(n):
                for k, (px, py) in enumerate(_other_chips(x, y)):
                    landed = rows(a, 2 * px + py, c)
                    ici(a, k, landed, px, py, c).wait_recv()
                    d2d(a, k, landed, x, y, c).start()

        def end():
            x, y, c = _place()
            for a in range(n):
                for k, (px, py) in enumerate(_other_chips(x, y)):
                    d2d(a, k, rows(a, 2 * px + py, 1 - c), x, y, c).wait_recv()
                    ici(a, k, rows(a, 2 * x + y, c), px, py, c).wait_send()
                    d2d(a, k, rows(a, 2 * px + py, c), x, y, c).wait_send()

        return start, mid, end


class _ChipExchangeRider:
    def __init__(self, arrs):
        n = len(arrs)
        self.ins = list(arrs)
        self.out_shape = [SDS((3,) + a.shape[1:], a.dtype) for a in arrs]
        self.aliases = {}
        self.sems = [pltpu.SemaphoreType.DMA((n, 3)), pltpu.SemaphoreType.DMA((n, 3))]

    def hooks(self, ins, outs, sems):
        send, recv = sems

        def copy(a, k, px, py, c):
            return pltpu.make_async_remote_copy(src_ref=ins[a].at[2 * px + py], dst_ref=outs[a].at[k], send_sem=send.at[a, k],
                                                recv_sem=recv.at[a, k], device_id=(px, py, c), device_id_type=MESH)

        def start():
            x, y, c = _place()
            for a in range(len(ins)):
                for k, (px, py) in enumerate(_other_chips(x, y)):
                    copy(a, k, px, py, c).start()

        def mid():
            pass

        def end():
            x, y, c = _place()
            for a in range(len(ins)):
                for k, (px, py) in enumerate(_other_chips(x, y)):
                    copy(a, k, px, py, c).wait()

        return start, mid, end


def _run_rider(name, rider):
    r_in, r_out = len(rider.ins), len(rider.out_shape)

    def body(*refs):
        start, mid, end = rider.hooks(refs[:r_in], refs[r_in:r_in + r_out], refs[r_in + r_out:])
        start()
        mid()
        end()

    any_spec = pl.BlockSpec(memory_space=pl.ANY)
    return pl.pallas_call(
        body, name=name, in_specs=[any_spec] * r_in, out_specs=[any_spec] * r_out, out_shape=list(rider.out_shape),
        scratch_shapes=list(rider.sems), input_output_aliases=dict(rider.aliases))(*rider.ins)


def _pair_exchange(name, arrs):
    n = len(arrs)

    def body(*refs):
        ins, outs = refs[:n], refs[n:2 * n]
        send, recv = refs[2 * n:]
        x, y, c = _place()
        cps = []
        for a in range(n):
            half = ins[a].shape[1] // 2
            cp = pltpu.make_async_remote_copy(
                src_ref=ins[a].at[:, pl.ds((1 - c) * half, half)], dst_ref=outs[a], send_sem=send.at[a], recv_sem=recv.at[a],
                device_id=(x, y, 1 - c), device_id_type=MESH)
            cp.start()
            cps.append(cp)
        for cp in cps:
            cp.wait()

    any_spec = pl.BlockSpec(memory_space=pl.ANY)
    return pl.pallas_call(
        body, name=name, in_specs=[any_spec] * n, out_specs=[any_spec] * n,
        out_shape=[SDS((NSH, a.shape[1] // 2, a.shape[2]), a.dtype) for a in arrs],
        scratch_shapes=[pltpu.SemaphoreType.DMA((n,)), pltpu.SemaphoreType.DMA((n,))])(*arrs)


def _pair_add(name, g, p, core):
    _, Rh, Cc = p.shape
    tr = _rt(Rh, 512, 16)
    nr = Rh // tr

    def body(c_ref, g_ref, p_ref, o_ref, ob_ref):
        v = g_ref[...] + p_ref[...]
        o_ref[...] = v
        ob_ref[...] = v.astype(BF16)

    blk = pl.BlockSpec((None, tr, Cc), lambda s, r, c_ref: (s, r, 0))
    return pl.pallas_call(
        body, name=name,
        grid_spec=pltpu.PrefetchScalarGridSpec(
            num_scalar_prefetch=1, grid=(NSH, nr),
            in_specs=[pl.BlockSpec((None, tr, Cc), lambda s, r, c_ref: (s, c_ref[0] * nr + r, 0)), blk],
            out_specs=[blk, blk]),
        out_shape=[SDS((NSH, Rh, Cc), F32), SDS((NSH, Rh, Cc), BF16)])(core, g, p)


def _chip_add(name, hsum, q, chip, core, l, into):
    _, Rh, Cc = hsum.shape
    tr = _rt(Rh, 512, 8)
    nr = Rh // tr

    def body(*refs):
        h_ref, q_ref, o_ref = refs[2], refs[3], refs[-1]
        o_ref[...] = ((h_ref[...] + q_ref[0].astype(F32)) + q_ref[1].astype(F32)) + q_ref[2].astype(F32)

    in_specs = [pl.BlockSpec((None, tr, Cc), lambda r, p_ref, c_ref: (p_ref[0], r, 0)),
                pl.BlockSpec((3, tr, Cc), lambda r, p_ref, c_ref: (0, r, 0))]
    args = [chip, core, hsum, q]
    aliases = {}
    if into is not None:
        in_specs.append(pl.BlockSpec(memory_space=pl.ANY))
        args.append(into)
        aliases = {4: 0}
    return pl.pallas_call(
        body, name=name,
        grid_spec=pltpu.PrefetchScalarGridSpec(
            num_scalar_prefetch=2, grid=(nr,), in_specs=in_specs,
            out_specs=pl.BlockSpec((None, tr, Cc), lambda r, p_ref, c_ref: (l, c_ref[0] * nr + r, 0))),
        out_shape=SDS((DEPTH, 2 * Rh, Cc), F32), input_output_aliases=aliases)(*args)


def _pair_share(arrs):
    n = len(arrs)

    def body(*refs):
        outs = refs[n:2 * n]
        send, recv = refs[2 * n:]
        x, y, c = _place()
        cps = []
        for a in range(n):
            half = outs[a].shape[1] // 2
            mine = outs[a].at[:, pl.ds(c * half, half)]
            cp = pltpu.make_async_remote_copy(
                src_ref=mine, dst_ref=mine, send_sem=send.at[a], recv_sem=recv.at[a],
                device_id=(x, y, 1 - c), device_id_type=MESH)
            cp.start()
            cps.append(cp)
        for a, cp in enumerate(cps):
            cp.wait_send()
            half = outs[a].shape[1] // 2
            theirs = outs[a].at[:, pl.ds((1 - c) * half, half)]
            pltpu.make_async_remote_copy(
                src_ref=theirs, dst_ref=theirs, send_sem=send.at[a], recv_sem=recv.at[a],
                device_id=(x, y, 1 - c), device_id_type=MESH).wait_recv()

    any_spec = pl.BlockSpec(memory_space=pl.ANY)
    return pl.pallas_call(
        body, name="pair_share", in_specs=[any_spec] * n, out_specs=[any_spec] * n,
        out_shape=[SDS(a.shape, a.dtype) for a in arrs], input_output_aliases={a: a for a in range(n)},
        scratch_shapes=[pltpu.SemaphoreType.DMA((n,)), pltpu.SemaphoreType.DMA((n,))])(*arrs)


def _allreduce_small(v):
    rows = v.shape[0]

    def body(v_ref, o_ref, buf, send, recv):
        x, y, c = _place()
        me = 4 * x + 2 * y + c
        buf[me] = v_ref[...]
        cps = []
        k = 0
        for dx in range(2):
            for dy in range(2):
                for dc in range(2):
                    if dx + dy + dc == 0:
                        continue
                    cp = pltpu.make_async_remote_copy(
                        src_ref=v_ref, dst_ref=buf.at[me], send_sem=send.at[k], recv_sem=recv.at[k],
                        device_id=(jnp.bitwise_xor(x, dx), jnp.bitwise_xor(y, dy), jnp.bitwise_xor(c, dc)), device_id_type=MESH)
                    cp.start()
                    cps.append((cp, dx, dy, dc))
                    k += 1
        for k, (cp, dx, dy, dc) in enumerate(cps):
            cp.wait_send()
            src = 4 * jnp.bitwise_xor(x, dx) + 2 * jnp.bitwise_xor(y, dy) + jnp.bitwise_xor(c, dc)
            pltpu.make_async_remote_copy(
                src_ref=v_ref, dst_ref=buf.at[src], send_sem=send.at[k], recv_sem=recv.at[k],
                device_id=(x, y, c), device_id_type=MESH).wait_recv()
        acc = buf[0]
        for d in range(1, 8):
            acc = acc + buf[d]
        o_ref[...] = acc

    vm = pl.BlockSpec(memory_space=pltpu.VMEM)
    return pl.pallas_call(
        body, name="allreduce_small", in_specs=[vm], out_specs=vm, out_shape=SDS((rows, LANE), F32),
        scratch_shapes=[pltpu.VMEM((8, rows, LANE), F32), pltpu.SemaphoreType.DMA((7,)), pltpu.SemaphoreType.DMA((7,))])(v)


def _adamw(name, w, g, m, v):
    A, R, Cc = w.shape
    tr = _rt(R, 512, 8)

    def body(w_ref, g_ref, m_ref, v_ref, d_ref, mo_ref, vo_ref):
        gv = g_ref[...]
        mn = ADAM_B1 * m_ref[...] + (1.0 - ADAM_B1) * gv
        vn = ADAM_B2 * v_ref[...] + (1.0 - ADAM_B2) * (gv * gv)
        m_hat = mn / (1.0 - ADAM_B1 ** ADAM_STEP)
        v_hat = vn / (1.0 - ADAM_B2 ** ADAM_STEP)
        d_ref[...] = -ADAM_LR * (m_hat / (jnp.sqrt(v_hat) + ADAM_EPS) + ADAM_WD * w_ref[...])
        mo_ref[...] = mn
        vo_ref[...] = vn

    blk = pl.BlockSpec((None, tr, Cc), lambda a, r: (a, r, 0))
    return pl.pallas_call(
        body, name=name, grid=(A, R // tr), in_specs=[blk] * 4, out_specs=[blk] * 3,
        out_shape=[SDS(w.shape, F32)] * 3)(w, g, m, v)


BIG = ("gate", "up", "down", "win", "wpa", "wpb", "wout")
SMALL = ("meta", "norm_gains", "w2", "b2", "gn", "qn", "kn", "bm", "fin")


def _view3(a):
    return a.reshape(a.shape[0], -1, a.shape[-1])


class _StepComm:
    def __init__(self, chip, core):
        self.pvec, self.cvec = chip.reshape(1), core.reshape(1)
        self.sums = {}
        self.red = {k: None for k in BIG}

    def fwd_rider(self, W, l):
        if l + 1 >= DEPTH:
            return None
        return _GatherRider(BIG, [W[k] for k in BIG], [l + 1] * len(BIG))

    def layer_done(self, l, G):
        dwin = jnp.transpose(_win_from_padded(G["winp"]).reshape(D, NSH, D_IN // NSH), (1, 0, 2))
        loc = [dwin if k == "win" else _view3(G[k]) for k in BIG]
        got = _pair_exchange(f"pair_exchange_l{l}", loc)
        self.sums[l] = [_pair_add(f"pair_add_{k}_l{l}", a, p, self.cvec) for k, a, p in zip(BIG, loc, got)]

    def bwd_rider(self, l):
        if l + 1 >= DEPTH:
            return None
        return _ChipExchangeRider([s[1] for s in self.sums[l + 1]])

    def bwd_carried(self, l, arrived):
        self._chip_add(l + 1, arrived)

    def _chip_add(self, l, arrived):
        for k, s, q in zip(BIG, self.sums.pop(l), arrived):
            self.red[k] = _chip_add(f"chip_add_{k}_l{l}", s[0], q, self.pvec, self.cvec, l, self.red[k])

    def finish(self):
        self._chip_add(0, _run_rider("chip_exchange_l0", _ChipExchangeRider([s[1] for s in self.sums[0]])))
        return dict(zip(BIG, _pair_share([self.red[k] for k in BIG])))


def kernel(x, meta_tokens, norm_gains, ffn_w_gate, ffn_w_up, ffn_w_down, w_in, gla_w2, gla_b2, gla_gn, q_norm, k_norm, w_pa, w_pb, b_merge, w_out, final_norm, loss_target, m_meta_tokens, m_norm_gains, m_ffn_w_gate, m_ffn_w_up, m_ffn_w_down, m_w_in, m_gla_w2, m_gla_b2, m_gla_gn, m_q_norm, m_k_norm, m_w_pa, m_w_pb, m_b_merge, m_w_out, m_final_norm, v_meta_tokens, v_norm_gains, v_ffn_w_gate, v_ffn_w_up, v_ffn_w_down, v_w_in, v_gla_w2, v_gla_b2, v_gla_gn, v_q_norm, v_k_norm, v_w_pa, v_w_pb, v_b_merge, v_w_out, v_final_norm):
    big_w = dict(gate=ffn_w_gate, up=ffn_w_up, down=ffn_w_down, win=w_in, wpa=w_pa, wpb=w_pb, wout=w_out)
    big_m = dict(gate=m_ffn_w_gate, up=m_ffn_w_up, down=m_ffn_w_down, win=m_w_in, wpa=m_w_pa, wpb=m_w_pb, wout=m_w_out)
    big_v = dict(gate=v_ffn_w_gate, up=v_ffn_w_up, down=v_ffn_w_down, win=v_w_in, wpa=v_w_pa, wpb=v_w_pb, wout=v_w_out)
    small_w = dict(meta=meta_tokens, norm_gains=norm_gains, w2=gla_w2, b2=gla_b2, gn=gla_gn, qn=q_norm, kn=k_norm,
                   bm=b_merge, fin=final_norm)
    small_m = dict(meta=m_meta_tokens, norm_gains=m_norm_gains, w2=m_gla_w2, b2=m_gla_b2, gn=m_gla_gn, qn=m_q_norm,
                   kn=m_k_norm, bm=m_b_merge, fin=m_final_norm)
    small_v = dict(meta=v_meta_tokens, norm_gains=v_norm_gains, w2=v_gla_w2, b2=v_gla_b2, gn=v_gla_gn, qn=v_q_norm,
                   kn=v_k_norm, bm=v_b_merge, fin=v_final_norm)
    xi, yi, ci = _place()
    chip = (2 * xi + yi).astype(jnp.int32)

    comm = _StepComm(chip, ci.astype(jnp.int32))
    shard_pack = _pack({k: small_w[k] for k in SHARDED_SMALL}, SHARDED_SMALL, SMALL_ROWS)
    placed = [_cast_place(f"cast_{k}", _view3(big_w[k]), comm.pvec) for k in BIG]
    placed.append(lax.dynamic_update_slice(jnp.zeros((NSH, SMALL_ROWS, LANE), F32), shard_pack[None], (chip, 0, 0)))
    gathered = _run_rider("gather_l0", _GatherRider(BIG + ("small",), placed, [0] * len(BIG) + [None]))
    W = _assemble(dict(zip(BIG + ("small",), gathered)))
    W.update(gn=gla_gn, qn=q_norm, kn=k_norm, fin=final_norm)

    loss, dh0, _, gs = _local_step(x[0], loss_target[0], W, comm)
    grad_x = dh0[LANE:][None]
    shared = comm.finish()
    grads, deltas, new_m, new_v = {}, {}, {}, {}
    for k in BIG:
        r = shared[k]
        shp = big_w[k].shape
        grads[k] = r.reshape(shp)
        d, mn, vn = _adamw(f"adamw_{k}", _view3(big_w[k]), r, _view3(big_m[k]), _view3(big_v[k]))
        deltas[k], new_m[k], new_v[k] = d.reshape(shp), mn.reshape(shp), vn.reshape(shp)

    gs["loss"] = loss.reshape(1)
    table = dict(FULL_SMALL, loss=(1,))
    tot = _unpack(_allreduce_small(_pack(gs, table, GRAD_ROWS)), table)
    loss_out = tot["loss"][0]
    sl = dict(meta=(1, 256), norm_gains=(2, 256), w2=(3, 64), b2=(2, 64), bm=(2, 256))
    for k in SMALL:
        gk = tot[k]
        if k in sl:
            ax, width = sl[k]
            gk = lax.dynamic_slice_in_dim(gk, chip * width, width, axis=ax)
        grads[k] = gk
    tbl = {k: small_w[k].shape for k in SMALL}
    rows = _rows_for(tbl, 8)
    packs = [_pack(src, tbl, rows)[None] for src in (small_w, grads, small_m, small_v)]
    d, mn, vn = _adamw("adamw_small", *packs)
    for dst, packed in ((deltas, d), (new_m, mn), (new_v, vn)):
        dst.update(_unpack(packed[0], tbl))

    order = ("meta", "norm_gains", "gate", "up", "down", "win", "w2", "b2", "gn", "qn", "kn", "wpa", "wpb", "bm", "wout", "fin")
    return (loss_out, grad_x, *[grads[k] for k in order], *[deltas[k] for k in order], *[new_m[k] for k in order],
            *[new_v[k] for k in order])
```

```python
import functools

import numpy as np
import jax
import jax.numpy as jnp
from jax import lax
from jax.experimental import pallas as pl
from jax.experimental.pallas import tpu as pltpu

F32, BF16 = jnp.float32, jnp.bfloat16
SDS = jax.ShapeDtypeStruct
HIGHEST = lax.Precision.HIGHEST
MESH = pl.DeviceIdType.MESH

D = 1024
DEPTH = 4
N_META = 16
GRID_W = 64
GLA_H, GLA_DK, GLA_DV, GLA_RANK, GLA_TAU, CHUNK = 4, 64, 128, 16, 16.0, 64
ATT_H, ATT_KV, HD = 8, 2, 64
D_FF = 2816
EPS = 1e-6
ROPE_THETA = 10000.0
ADAM_LR, ADAM_B1, ADAM_B2, ADAM_EPS, ADAM_WD, ADAM_STEP = 0.001, 0.9, 0.999, 1e-08, 0.01, 10

NSH = 4
FF_S = D_FF // NSH
FF_P = 768
LANE = 128
PAD = LANE - N_META
D_IN = 4384
C_GA, C_GB, C_QA, C_KA, C_VA, C_RA, C_QB, C_KB, C_VB, C_LR, HP = 0, 1024, 2048, 2304, 2560, 3072, 3584, 4096, 4224, 4352, 4480
VMEM_BIG = 56 * 2 ** 20
ATT_SUB = 192
FFN_SUB = 192
RIDER_MID_BACK = 5


def _rt(n, cap, mult=LANE):
    best = None
    t = mult
    while t <= min(n, cap):
        if n % t == 0:
            best = t
        t += mult
    assert best is not None, (n, cap, mult)
    return best


def _cp(big=False):
    return pltpu.CompilerParams(vmem_limit_bytes=VMEM_BIG) if big else None


def _row_ids(i, tm):
    return i * tm + lax.broadcasted_iota(jnp.int32, (tm, 1), 0)


def _mm(name, a, b, *, grid, a_spec, b_spec, o_spec, o_shape, o_dtype, dims, acc_shape, nk=1, scale=None,
        res=None, res_spec=None, a_sl=None, b_sl=None, pad_w=None, into=None):
    has_res, has_into = res is not None, into is not None

    def body(*refs):
        a_ref, b_ref = refs[0], refs[1]
        p = 2
        res_ref = None
        if has_res:
            res_ref = refs[p]
            p += 1
        if has_into:
            p += 1
        o_ref = refs[p]
        acc_ref = refs[p + 1] if nk > 1 else None
        av = (a_ref[a_sl] if a_sl is not None else a_ref[...]).astype(BF16)
        bv = (b_ref[b_sl] if b_sl is not None else b_ref[...]).astype(BF16)
        prod = lax.dot_general(av, bv, dims, preferred_element_type=F32)

        def finish(v):
            if scale is not None:
                v = v * scale
            if has_res:
                v = v + res_ref[...]
            v = v.astype(o_dtype)
            if pad_w is None:
                o_ref[...] = v
            else:
                w = v.shape[-1]
                o_ref[:, :w] = v
                o_ref[:, w:] = jnp.zeros((v.shape[0], pad_w - w), o_dtype)

        if nk == 1:
            finish(prod)
        else:
            k = pl.program_id(len(grid) - 1)

            @pl.when(k == 0)
            def _():
                acc_ref[...] = prod

            @pl.when(k > 0)
            def _():
                acc_ref[...] += prod

            @pl.when(k == nk - 1)
            def _():
                finish(acc_ref[...])

    in_specs = [a_spec, b_spec]
    args = [a, b]
    if has_res:
        in_specs.append(res_spec)
        args.append(res)
    aliases = {}
    if has_into:
        aliases = {len(args): 0}
        in_specs.append(pl.BlockSpec(memory_space=pl.ANY))
        args.append(into)
        o_shape = into.shape
    return pl.pallas_call(
        body, name=name, grid=grid, in_specs=in_specs, out_specs=o_spec, out_shape=SDS(tuple(o_shape), o_dtype),
        scratch_shapes=[pltpu.VMEM(acc_shape, F32)] if nk > 1 else [], input_output_aliases=aliases,
        compiler_params=_cp(True))(*args)


NN = (((1,), (0,)), ((), ()))
NT = (((1,), (1,)), ((), ()))
TN = (((0,), (0,)), ((), ()))


def _rms_fwd(name, h, g):
    Lp = h.shape[0]
    tm = _rt(Lp, 384)

    def body(h_ref, g_ref, n_ref, nt_ref):
        x = h_ref[...]
        r = lax.rsqrt(jnp.mean(x * x, axis=-1, keepdims=True) + EPS)
        y = x * r * g_ref[...]
        n_ref[...] = y.astype(BF16)
        nt_ref[...] = y.T.astype(BF16)

    return pl.pallas_call(
        body, name=name, grid=(Lp // tm,),
        in_specs=[pl.BlockSpec((tm, D), lambda i: (i, 0)), pl.BlockSpec((1, D), lambda i: (0, 0))],
        out_specs=[pl.BlockSpec((tm, D), lambda i: (i, 0)), pl.BlockSpec((D, tm), lambda i: (0, i))],
        out_shape=[SDS((Lp, D), BF16), SDS((D, Lp), BF16)])(h, g)


def _rms_bwd(name, dn, h, g, dh):
    Lp = h.shape[0]
    tm = _rt(Lp, 384)

    def body(dn_ref, h_ref, g_ref, dh_ref, o_ref, dg_ref):
        x = h_ref[...]
        r = lax.rsqrt(jnp.mean(x * x, axis=-1, keepdims=True) + EPS)
        xh = x * r
        dn = dn_ref[...]
        u = dn * g_ref[...]
        o_ref[...] = dh_ref[...] + r * (u - xh * jnp.mean(u * xh, axis=-1, keepdims=True))

        @pl.when(pl.program_id(0) == 0)
        def _():
            dg_ref[...] = jnp.zeros_like(dg_ref)

        dg_ref[...] += jnp.sum(dn * xh, axis=0, keepdims=True)

    row = pl.BlockSpec((tm, D), lambda i: (i, 0))
    vec = pl.BlockSpec((1, D), lambda i: (0, 0))
    return pl.pallas_call(
        body, name=name, grid=(Lp // tm,), in_specs=[row, row, vec, row], out_specs=[row, vec],
        out_shape=[SDS((Lp, D), F32), SDS((1, D), F32)])(dn, h, g, dh)


def _loss_head(name, h, g, tgt):
    Lp = h.shape[0]
    tm = LANE

    def body(h_ref, g_ref, t_ref, dh_ref, dg_ref, loss_ref):
        i = pl.program_id(0)
        x = h_ref[...]
        r = lax.rsqrt(jnp.mean(x * x, axis=-1, keepdims=True) + EPS)
        xh = x * r
        gg = g_ref[...]
        err = jnp.where(i >= 1, xh * gg - t_ref[...], 0.0)
        dy = err * (1.0 / D)
        u = dy * gg
        dh_ref[...] = r * (u - xh * jnp.mean(u * xh, axis=-1, keepdims=True))

        @pl.when(i == 0)
        def _():
            dg_ref[...] = jnp.zeros_like(dg_ref)
            loss_ref[...] = jnp.zeros_like(loss_ref)

        dg_ref[...] += jnp.sum(dy * xh, axis=0, keepdims=True)
        loss_ref[...] += (0.5 / D) * jnp.sum(err * err)

    row = pl.BlockSpec((tm, D), lambda i: (i, 0))
    vec = pl.BlockSpec((1, D), lambda i: (0, 0))
    return pl.pallas_call(
        body, name=name, grid=(Lp // tm,),
        in_specs=[row, vec, pl.BlockSpec((tm, D), lambda i: (jnp.maximum(i - 1, 0), 0))],
        out_specs=[row, vec, pl.BlockSpec((8, LANE), lambda i: (0, 0))],
        out_shape=[SDS((Lp, D), F32), SDS((1, D), F32), SDS((8, LANE), F32)])(h, g, tgt)


def _swiglu_fwd(name, a, b):
    Lp = a.shape[1]
    tm = _rt(Lp, 384)

    def body(a_ref, b_ref, s_ref, st_ref):
        av = a_ref[...]
        s = av * jax.nn.sigmoid(av) * b_ref[...]
        s_ref[...] = s.astype(BF16)
        st_ref[...] = s.T.astype(BF16)

    slab = pl.BlockSpec((None, tm, FF_P), lambda j, i: (j, i, 0))
    return pl.pallas_call(
        body, name=name, grid=(NSH, Lp // tm), in_specs=[slab, slab],
        out_specs=[slab, pl.BlockSpec((None, FF_P, tm), lambda j, i: (j, 0, i))],
        out_shape=[SDS((NSH, Lp, FF_P), BF16), SDS((NSH, FF_P, Lp), BF16)])(a, b)


def _swiglu_bwd(name, a, b, ds):
    Lp = a.shape[1]
    tm = _rt(Lp, 384)

    def body(a_ref, b_ref, ds_ref, da_ref, db_ref):
        av = a_ref[...]
        sg = jax.nn.sigmoid(av)
        dsv = ds_ref[...]
        da_ref[...] = (dsv * b_ref[...] * (sg * (1.0 + av * (1.0 - sg)))).astype(BF16)
        db_ref[...] = (dsv * (av * sg)).astype(BF16)

    slab = pl.BlockSpec((None, tm, FF_P), lambda j, i: (j, i, 0))
    return pl.pallas_call(
        body, name=name, grid=(NSH, Lp // tm), in_specs=[slab, slab, slab], out_specs=[slab, slab],
        out_shape=[SDS((NSH, Lp, FF_P), BF16), SDS((NSH, Lp, FF_P), BF16)])(a, b, ds)


def _merge_fwd(name, H, pa, pb, bm):
    Lp = H.shape[0]
    tm = _rt(Lp, 384)

    def body(g_ref, pa_ref, pb_ref, bm_ref, y_ref, yt_ref):
        gv = g_ref[...]
        y = (jax.nn.sigmoid(gv[:, :D] + bm_ref[0:1, :]) * pa_ref[...]
             + jax.nn.sigmoid(gv[:, D:] + bm_ref[1:2, :]) * pb_ref[...])
        y_ref[...] = y.astype(BF16)
        yt_ref[...] = y.T.astype(BF16)

    row = pl.BlockSpec((tm, D), lambda i: (i, 0))
    return pl.pallas_call(
        body, name=name, grid=(Lp // tm,),
        in_specs=[pl.BlockSpec((tm, 2 * D), lambda i: (i, 0)), row, row, pl.BlockSpec((2, D), lambda i: (0, 0))],
        out_specs=[row, pl.BlockSpec((D, tm), lambda i: (0, i))],
        out_shape=[SDS((Lp, D), BF16), SDS((D, Lp), BF16)])(H, pa, pb, bm)


def _merge_bwd(name, H, pa, pb, bm, dy):
    Lp = H.shape[0]
    tm = _rt(Lp, 384)

    def body(g_ref, pa_ref, pb_ref, bm_ref, dy_ref, dpa_ref, dpb_ref, dh_ref, dbm_ref):
        gv = g_ref[...]
        dyv = dy_ref[...]
        sa = jax.nn.sigmoid(gv[:, :D] + bm_ref[0:1, :])
        sb = jax.nn.sigmoid(gv[:, D:] + bm_ref[1:2, :])
        dpa_ref[...] = (dyv * sa).astype(BF16)
        dpb_ref[...] = (dyv * sb).astype(BF16)
        dga = dyv * pa_ref[...] * (sa * (1.0 - sa))
        dgb = dyv * pb_ref[...] * (sb * (1.0 - sb))
        dh_ref[:, :D] = dga.astype(BF16)
        dh_ref[:, D:] = dgb.astype(BF16)

        @pl.when(pl.program_id(0) == 0)
        def _():
            dbm_ref[...] = jnp.zeros_like(dbm_ref)

        dbm_ref[0:1, :] += jnp.sum(dga, axis=0, keepdims=True)
        dbm_ref[1:2, :] += jnp.sum(dgb, axis=0, keepdims=True)

    row = pl.BlockSpec((tm, D), lambda i: (i, 0))
    two = pl.BlockSpec((2, D), lambda i: (0, 0))
    gate = pl.BlockSpec((tm, 2 * D), lambda i: (i, 0))
    return pl.pallas_call(
        body, name=name, grid=(Lp // tm,), in_specs=[gate, row, row, two, row], out_specs=[row, row, gate, two],
        out_shape=[SDS((Lp, D), BF16), SDS((Lp, D), BF16), SDS((Lp, HP), BF16), SDS((2, D), F32)])(H, pa, pb, bm, dy)


def _gla_prep(name, H, w2p, b2p):
    Lp = H.shape[0]
    tm = _rt(Lp, 384)

    def body(qk_ref, v_ref, lr_ref, w_ref, b_ref, q_o, k_o, v_o, gf_o, gb_o):
        valid = _row_ids(pl.program_id(0), tm) >= PAD
        qk = qk_ref[...]
        vv = v_ref[...]
        pre = jnp.dot(lr_ref[...].astype(BF16), w_ref[...], preferred_element_type=F32) + b_ref[...]
        g = jnp.where(valid, jax.nn.log_sigmoid(pre) * (1.0 / GLA_TAU), 0.0)
        for hh in range(GLA_H):
            q_o[hh] = qk[:, 64 * hh:64 * hh + 64] * (GLA_DK ** -0.5)
            k_o[hh] = qk[:, 256 + 64 * hh:256 + 64 * hh + 64]
            v_o[hh] = vv[:, 128 * hh:128 * hh + 128].astype(BF16)
            gf_o[hh] = g[:, 64 * hh:64 * hh + 64]
            gb_o[hh] = g[:, 256 + 64 * hh:256 + 64 * hh + 64]

    h64 = pl.BlockSpec((GLA_H, tm, 64), lambda i: (0, i, 0))
    h128 = pl.BlockSpec((GLA_H, tm, 128), lambda i: (0, i, 0))
    return pl.pallas_call(
        body, name=name, grid=(Lp // tm,),
        in_specs=[pl.BlockSpec((tm, 512), lambda i: (i, C_QA // 512)), pl.BlockSpec((tm, 512), lambda i: (i, C_VA // 512)),
                  pl.BlockSpec((tm, LANE), lambda i: (i, C_LR // LANE)), pl.BlockSpec((LANE, 512), lambda i: (0, 0)),
                  pl.BlockSpec((1, 512), lambda i: (0, 0))],
        out_specs=[h64, h64, h128, h64, h64],
        out_shape=[SDS((GLA_H, Lp, 64), F32), SDS((GLA_H, Lp, 64), F32), SDS((GLA_H, Lp, 128), BF16),
                   SDS((GLA_H, Lp, 64), F32), SDS((GLA_H, Lp, 64), F32)])(H, H, H, w2p, b2p)


def _bdot(a, b, ca, cb, precision=None):
    return lax.dot_general(a, b, ((ca, cb), ((0,), (0,))), precision=precision, preferred_element_type=F32)


def _gla_chunk_terms(q_ref, k_ref, g_ref, v_ref, G, rev):
    B = GLA_H * G
    qv = q_ref[...].reshape(B, CHUNK, GLA_DK)
    kv = k_ref[...].reshape(B, CHUNK, GLA_DK)
    gv = g_ref[...].reshape(B, CHUNK, GLA_DK)
    vv = v_ref[...].reshape(B, CHUNK, GLA_DV)
    ii = lax.broadcasted_iota(jnp.int32, (CHUNK, CHUNK), 0)
    jj = lax.broadcasted_iota(jnp.int32, (CHUNK, CHUNK), 1)
    tri = (jj >= ii) if rev else (jj <= ii)
    tb = jnp.broadcast_to(tri.astype(F32)[None], (B, CHUNK, CHUNK))
    bc = _bdot(tb, gv, (2,), (1,), HIGHEST)
    bt = bc[:, 0:1, :] if rev else bc[:, CHUNK - 1:CHUNK, :]
    eq, eki, eke = jnp.exp(bc), jnp.exp(-bc), jnp.exp(bt - bc)
    qd, ki, ke = qv * eq, kv * eki, kv * eke
    att = jnp.where(tri[None], _bdot(qd.astype(BF16), ki.astype(BF16), (2,), (2,)), 0.0)
    dm = jnp.exp(_bdot(gv, jnp.ones((B, CHUNK, GLA_DV), F32), (1,), (1,), HIGHEST))
    return dict(B=B, vv=vv, tri=tri, tb=tb, bt=bt, eq=eq, eki=eki, eke=eke, qd=qd, ki=ki, ke=ke, att=att, dm=dm)


def _gla_fwd(name, q, k, g, v, rev, G):
    Lp = q.shape[1]
    tg = G * CHUNK
    ng = Lp // tg

    def body(q_ref, k_ref, g_ref, v_ref, o_ref, ss_ref, s_scr):
        @pl.when(pl.program_id(0) == 0)
        def _():
            s_scr[...] = jnp.zeros_like(s_scr)

        t = _gla_chunk_terms(q_ref, k_ref, g_ref, v_ref, G, rev)
        B, vv = t["B"], t["vv"]
        qd = t["qd"].astype(BF16)
        oi = _bdot(t["att"].astype(BF16), vv, (2,), (1,))
        kvc = _bdot(t["ke"].astype(BF16), vv, (1,), (1,)).reshape(GLA_H, G, GLA_DK, GLA_DV)
        dm = t["dm"].reshape(GLA_H, G, GLA_DK, GLA_DV)
        s = s_scr[...]
        sp = [None] * G
        for c in (range(G - 1, -1, -1) if rev else range(G)):
            sp[c] = s
            ss_ref[c] = s
            s = dm[:, c] * s + kvc[:, c]
        s_scr[...] = s
        spb = jnp.stack(sp, axis=1).reshape(B, GLA_DK, GLA_DV).astype(BF16)
        o_ref[...] = (oi + _bdot(qd, spb, (2,), (1,))).reshape(GLA_H, tg, GLA_DV)

    blk = (lambda i: (0, ng - 1 - i, 0)) if rev else (lambda i: (0, i, 0))
    sblk = (lambda i: (ng - 1 - i, 0, 0, 0)) if rev else (lambda i: (i, 0, 0, 0))
    h64 = pl.BlockSpec((GLA_H, tg, 64), blk)
    h128 = pl.BlockSpec((GLA_H, tg, 128), blk)
    return pl.pallas_call(
        body, name=name, grid=(ng,), in_specs=[h64, h64, h64, h128],
        out_specs=[h128, pl.BlockSpec((G, GLA_H, GLA_DK, GLA_DV), sblk)],
        out_shape=[SDS((GLA_H, Lp, GLA_DV), F32), SDS((Lp // CHUNK, GLA_H, GLA_DK, GLA_DV), F32)],
        scratch_shapes=[pltpu.VMEM((GLA_H, GLA_DK, GLA_DV), F32)], compiler_params=_cp(True))(q, k, g, v)


def _gla_bwd(name, q, k, g, v, ss, do, rev, G):
    Lp = q.shape[1]
    tg = G * CHUNK
    ng = Lp // tg

    def body(q_ref, k_ref, g_ref, v_ref, ss_ref, do_ref, dq_ref, dk_ref, dg_ref, dv_ref, ds_scr):
        @pl.when(pl.program_id(0) == 0)
        def _():
            ds_scr[...] = jnp.zeros_like(ds_scr)

        t = _gla_chunk_terms(q_ref, k_ref, g_ref, v_ref, G, rev)
        B, vv, tri = t["B"], t["vv"], t["tri"]
        qd, ki, ke = t["qd"], t["ki"], t["ke"]
        qdb, kib, keb = qd.astype(BF16), ki.astype(BF16), ke.astype(BF16)
        sp = jnp.stack([ss_ref[c] for c in range(G)], axis=1).reshape(B, GLA_DK, GLA_DV)
        dob = do_ref[...].reshape(B, CHUNK, GLA_DV).astype(BF16)
        da = jnp.where(tri[None], _bdot(dob, vv, (2,), (2,)), 0.0).astype(BF16)
        dqd = _bdot(da, kib, (2,), (1,)) + _bdot(dob, sp.astype(BF16), (2,), (2,))
        dki = _bdot(da, qdb, (1,), (1,))
        dv = _bdot(t["att"].astype(BF16), dob, (1,), (1,))
        cc = _bdot(qdb, dob, (1,), (1,)).reshape(GLA_H, G, GLA_DK, GLA_DV)
        dm = t["dm"].reshape(GLA_H, G, GLA_DK, GLA_DV)
        dsc = ds_scr[...]
        dsn = [None] * G
        for c in (range(G) if rev else range(G - 1, -1, -1)):
            dsn[c] = dsc
            dsc = dm[:, c] * dsc + cc[:, c]
        ds_scr[...] = dsc
        dsn = jnp.stack(dsn, axis=1).reshape(B, GLA_DK, GLA_DV)
        dsnb = dsn.astype(BF16)
        dv = dv + _bdot(keb, dsnb, (2,), (1,))
        dke = _bdot(vv, dsnb, (2,), (2,))
        ddrow = _bdot(jnp.ones((B, CHUNK, GLA_DV), F32), dsn * sp, (2,), (2,), HIGHEST)
        dbt = ddrow * jnp.exp(t["bt"]) + jnp.sum(dke * ke, axis=1, keepdims=True)
        db = dqd * qd - dki * ki - dke * ke
        dq_ref[...] = (dqd * t["eq"]).reshape(GLA_H, tg, GLA_DK)
        dk_ref[...] = (dki * t["eki"] + dke * t["eke"]).reshape(GLA_H, tg, GLA_DK)
        dg_ref[...] = (_bdot(t["tb"], db, (1,), (1,), HIGHEST) + dbt).reshape(GLA_H, tg, GLA_DK)
        dv_ref[...] = dv.reshape(GLA_H, tg, GLA_DV)

    blk = (lambda i: (0, i, 0)) if rev else (lambda i: (0, ng - 1 - i, 0))
    sblk = (lambda i: (i, 0, 0, 0)) if rev else (lambda i: (ng - 1 - i, 0, 0, 0))
    h64 = pl.BlockSpec((GLA_H, tg, 64), blk)
    h128 = pl.BlockSpec((GLA_H, tg, 128), blk)
    return pl.pallas_call(
        body, name=name, grid=(ng,),
        in_specs=[h64, h64, h64, h128, pl.BlockSpec((G, GLA_H, GLA_DK, GLA_DV), sblk), h128],
        out_specs=[h64, h64, h64, h128],
        out_shape=[SDS((GLA_H, Lp, 64), F32), SDS((GLA_H, Lp, 64), F32), SDS((GLA_H, Lp, 64), F32),
                   SDS((GLA_H, Lp, GLA_DV), F32)],
        scratch_shapes=[pltpu.VMEM((GLA_H, GLA_DK, GLA_DV), F32)], compiler_params=_cp(True))(q, k, g, v, ss, do)


def _gla_post(name, of, ob, H, gn):
    Lp = H.shape[0]
    tm = _rt(Lp, 384)

    def body(of_ref, ob_ref, r_ref, gn_ref, a_ref, at_ref):
        parts = []
        for hh in range(GLA_H):
            o = of_ref[hh] + ob_ref[hh]
            parts.append(o * lax.rsqrt(jnp.mean(o * o, axis=-1, keepdims=True) + EPS))
        rv = r_ref[...]
        a = (jnp.concatenate(parts, axis=1) * gn_ref[...]) * (rv * jax.nn.sigmoid(rv))
        a_ref[...] = a.astype(BF16)
        at_ref[...] = a.T.astype(BF16)

    h128 = pl.BlockSpec((GLA_H, tm, 128), lambda i: (0, i, 0))
    return pl.pallas_call(
        body, name=name, grid=(Lp // tm,),
        in_specs=[h128, h128, pl.BlockSpec((tm, 512), lambda i: (i, C_RA // 512)), pl.BlockSpec((1, 512), lambda i: (0, 0))],
        out_specs=[pl.BlockSpec((tm, 512), lambda i: (i, 0)), pl.BlockSpec((512, tm), lambda i: (0, i))],
        out_shape=[SDS((Lp, 512), BF16), SDS((512, Lp), BF16)])(of, ob, H, gn)


def _gla_post_bwd(name, of, ob, H, gn, da, dH):
    Lp = H.shape[0]
    tm = _rt(Lp, 384)

    def body(of_ref, ob_ref, r_ref, gn_ref, da_ref, dh_in, do_ref, dh_ref, dgn_ref):
        rv = r_ref[...]
        sg = jax.nn.sigmoid(rv)
        dav = da_ref[...]
        gnv = gn_ref[...]
        ons, rs = [], []
        for hh in range(GLA_H):
            o = of_ref[hh] + ob_ref[hh]
            r = lax.rsqrt(jnp.mean(o * o, axis=-1, keepdims=True) + EPS)
            rs.append(r)
            ons.append(o * r)
        on = jnp.concatenate(ons, axis=1)
        dw = dav * (rv * sg)
        dh_ref[...] = (dav * (on * gnv) * (sg * (1.0 + rv * (1.0 - sg)))).astype(BF16)

        @pl.when(pl.program_id(0) == 0)
        def _():
            dgn_ref[...] = jnp.zeros_like(dgn_ref)

        dgn_ref[...] += jnp.sum(dw * on, axis=0, keepdims=True)
        don = dw * gnv
        for hh in range(GLA_H):
            dd = don[:, 128 * hh:128 * hh + 128]
            do_ref[hh] = rs[hh] * (dd - ons[hh] * jnp.mean(dd * ons[hh], axis=-1, keepdims=True))

    h128 = pl.BlockSpec((GLA_H, tm, 128), lambda i: (0, i, 0))
    rblk = pl.BlockSpec((tm, 512), lambda i: (i, C_RA // 512))
    vec = pl.BlockSpec((1, 512), lambda i: (0, 0))
    return pl.pallas_call(
        body, name=name, grid=(Lp // tm,),
        in_specs=[h128, h128, rblk, vec, pl.BlockSpec((tm, 512), lambda i: (i, 0)), pl.BlockSpec(memory_space=pl.ANY)],
        out_specs=[h128, rblk, vec],
        out_shape=[SDS((GLA_H, Lp, 128), F32), SDS(dH.shape, BF16), SDS((1, 512), F32)],
        input_output_aliases={5: 1})(of, ob, H, gn, da, dH)


def _gla_qkv_bwd(name, dqf, dqb, dkf, dkb, dvf, dvb, dH):
    Lp = dqf.shape[1]
    tm = _rt(Lp, 384)

    def body(dqf_ref, dqb_ref, dkf_ref, dkb_ref, dvf_ref, dvb_ref, dh_in, dh_ref):
        valid = _row_ids(pl.program_id(0), tm) >= PAD
        for hh in range(GLA_H):
            dq = (dqf_ref[hh] + dqb_ref[hh]) * (GLA_DK ** -0.5)
            dh_ref[:, 64 * hh:64 * hh + 64] = jnp.where(valid, dq, 0.0).astype(BF16)
            dh_ref[:, 256 + 64 * hh:256 + 64 * hh + 64] = jnp.where(valid, dkf_ref[hh] + dkb_ref[hh], 0.0).astype(BF16)
            dh_ref[:, 512 + 128 * hh:512 + 128 * hh + 128] = jnp.where(valid, dvf_ref[hh] + dvb_ref[hh], 0.0).astype(BF16)

    h64 = pl.BlockSpec((GLA_H, tm, 64), lambda i: (0, i, 0))
    h128 = pl.BlockSpec((GLA_H, tm, 128), lambda i: (0, i, 0))
    return pl.pallas_call(
        body, name=name, grid=(Lp // tm,),
        in_specs=[h64, h64, h64, h64, h128, h128, pl.BlockSpec(memory_space=pl.ANY)],
        out_specs=pl.BlockSpec((tm, 1024), lambda i: (i, C_QA // 1024)), out_shape=SDS(dH.shape, BF16),
        input_output_aliases={6: 0})(dqf, dqb, dkf, dkb, dvf, dvb, dH)


def _gla_gate_bwd(name, H, w2p, b2p, dgf, dgb, dH):
    Lp = H.shape[0]
    tm = _rt(Lp, 384)

    def body(lr_ref, w_ref, b_ref, dgf_ref, dgb_ref, dh_in, dh_ref, dw_ref, db_ref, dg_scr):
        valid = _row_ids(pl.program_id(0), tm) >= PAD
        for hh in range(GLA_H):
            dg_scr[:, 64 * hh:64 * hh + 64] = dgf_ref[hh]
            dg_scr[:, 256 + 64 * hh:256 + 64 * hh + 64] = dgb_ref[hh]
        lrb = lr_ref[...].astype(BF16)
        wv = w_ref[...]
        pre = jnp.dot(lrb, wv, preferred_element_type=F32) + b_ref[...]
        dpre = jnp.where(valid, dg_scr[...] * (1.0 / GLA_TAU) * jax.nn.sigmoid(-pre), 0.0)
        dpb = dpre.astype(BF16)
        dh_ref[...] = lax.dot_general(dpb, wv, NT, preferred_element_type=F32).astype(BF16)

        @pl.when(pl.program_id(0) == 0)
        def _():
            dw_ref[...] = jnp.zeros_like(dw_ref)
            db_ref[...] = jnp.zeros_like(db_ref)

        dw_ref[...] += lax.dot_general(lrb, dpb, TN, preferred_element_type=F32)
        db_ref[...] += jnp.sum(dpre, axis=0, keepdims=True)

    h64 = pl.BlockSpec((GLA_H, tm, 64), lambda i: (0, i, 0))
    lrblk = pl.BlockSpec((tm, LANE), lambda i: (i, C_LR // LANE))
    wblk = pl.BlockSpec((LANE, 512), lambda i: (0, 0))
    vec = pl.BlockSpec((1, 512), lambda i: (0, 0))
    return pl.pallas_call(
        body, name=name, grid=(Lp // tm,),
        in_specs=[lrblk, wblk, vec, h64, h64, pl.BlockSpec(memory_space=pl.ANY)],
        out_specs=[lrblk, wblk, vec],
        out_shape=[SDS(dH.shape, BF16), SDS((LANE, 512), F32), SDS((1, 512), F32)],
        scratch_shapes=[pltpu.VMEM((tm, 512), F32)], input_output_aliases={5: 0})(H, w2p, b2p, dgf, dgb, dH)


def _swap16(x):
    n = x.shape[1]
    lane = lax.broadcasted_iota(jnp.int32, x.shape, 1)
    return jnp.where(lane % 32 < 16, pltpu.roll(x, n - 16, 1), pltpu.roll(x, 16, 1))


def _headnorm_rope(x, gain, cos, sin, bd):
    r = lax.rsqrt(jnp.dot(x * x, bd, precision=HIGHEST, preferred_element_type=F32) + EPS)
    xh = x * r
    xn = xh * gain
    return xn * cos + _swap16(xn) * sin, xh, r


def _headnorm_rope_bwd(dxr, xh, r, gain, cos, sin, bd):
    dxn = cos * dxr + _swap16(sin * dxr)
    u = dxn * gain
    dx = r * (u - xh * jnp.dot(u * xh, bd, precision=HIGHEST, preferred_element_type=F32))
    return dx, jnp.sum(dxn * xh, axis=0, keepdims=True)


def _att_prep(name, H, gq, gk, cos, sin, bd):
    Lp = H.shape[0]
    tm = _rt(Lp, 384)

    def body(q_ref, kv_ref, gq_ref, gk_ref, c_ref, s_ref, bd_ref, qp_ref, k_ref, v_ref):
        c1, s1 = c_ref[...], s_ref[...]
        c4, s4 = jnp.concatenate([c1] * 4, axis=1), jnp.concatenate([s1] * 4, axis=1)
        xr, _, _ = _headnorm_rope(q_ref[...], gq_ref[...], c4, s4, bd_ref[...])
        xr = xr * (HD ** -0.5)
        lane = lax.broadcasted_iota(jnp.int32, (tm, LANE), 1)
        for hh in range(ATT_H):
            grp = xr[:, LANE * (hh // 2):LANE * (hh // 2) + LANE]
            e, gi = hh % 2, hh // 4
            if e != gi:
                grp = pltpu.roll(grp, 64, 1)
            keep = (lane < 64) if gi == 0 else (lane >= 64)
            qp_ref[hh] = jnp.where(keep, grp, 0.0).astype(BF16)
        kv = kv_ref[...]
        kr, _, _ = _headnorm_rope(kv[:, :LANE], gk_ref[...], c1, s1, bd_ref[0:LANE, 0:LANE])
        k_ref[...] = kr.astype(BF16)
        v_ref[...] = kv[:, LANE:].astype(BF16)

    row128 = pl.BlockSpec((tm, LANE), lambda i: (i, 0))
    return pl.pallas_call(
        body, name=name, grid=(Lp // tm,),
        in_specs=[pl.BlockSpec((tm, 512), lambda i: (i, C_QB // 512)), pl.BlockSpec((tm, 256), lambda i: (i, C_KB // 256)),
                  pl.BlockSpec((1, 512), lambda i: (0, 0)), pl.BlockSpec((1, LANE), lambda i: (0, 0)), row128, row128,
                  pl.BlockSpec((512, 512), lambda i: (0, 0))],
        out_specs=[pl.BlockSpec((ATT_H, tm, LANE), lambda i: (0, i, 0)), row128, row128],
        out_shape=[SDS((ATT_H, Lp, LANE), BF16), SDS((Lp, LANE), BF16), SDS((Lp, LANE), BF16)])(H, H, gq, gk, cos, sin, bd)


def _att_q_bwd(name, H, gq, cos, sin, bd, dqp, dH):
    Lp = H.shape[0]
    tm = _rt(Lp, 384)

    def body(q_ref, gq_ref, c_ref, s_ref, bd_ref, dqp_ref, dh_in, dh_ref, dg_ref):
        c1, s1 = c_ref[...], s_ref[...]
        c4, s4 = jnp.concatenate([c1] * 4, axis=1), jnp.concatenate([s1] * 4, axis=1)
        gqv = gq_ref[...]
        _, xh, r = _headnorm_rope(q_ref[...], gqv, c4, s4, bd_ref[...])
        lane = lax.broadcasted_iota(jnp.int32, (tm, LANE), 1)
        groups = []
        for j in range(ATT_H // 2):
            pieces = []
            for e in range(2):
                hh = 2 * j + e
                piece = dqp_ref[hh]
                if e != hh // 4:
                    piece = pltpu.roll(piece, 64, 1)
                pieces.append(piece)
            groups.append(jnp.where(lane < 64, pieces[0], pieces[1]))
        dxr = jnp.concatenate(groups, axis=1) * (HD ** -0.5)
        dx, dg = _headnorm_rope_bwd(dxr, xh, r, gqv, c4, s4, bd_ref[...])
        dh_ref[...] = dx.astype(BF16)

        @pl.when(pl.program_id(0) == 0)
        def _():
            dg_ref[...] = jnp.zeros_like(dg_ref)

        dg_ref[...] += dg

    row128 = pl.BlockSpec((tm, LANE), lambda i: (i, 0))
    qblk = pl.BlockSpec((tm, 512), lambda i: (i, C_QB // 512))
    vec = pl.BlockSpec((1, 512), lambda i: (0, 0))
    return pl.pallas_call(
        body, name=name, grid=(Lp // tm,),
        in_specs=[qblk, vec, row128, row128, pl.BlockSpec((512, 512), lambda i: (0, 0)),
                  pl.BlockSpec((ATT_H, tm, LANE), lambda i: (0, i, 0)), pl.BlockSpec(memory_space=pl.ANY)],
        out_specs=[qblk, vec], out_shape=[SDS(dH.shape, BF16), SDS((1, 512), F32)],
        input_output_aliases={6: 0})(H, gq, cos, sin, bd, dqp, dH)


def _att_kv_bwd(name, H, gk, cos, sin, bd, dkr, dvb, dH):
    Lp = H.shape[0]
    tm = _rt(Lp, 384)

    def body(kv_ref, gk_ref, c_ref, s_ref, bd_ref, dk_ref, dv_ref, dh_in, dh_ref, dg_ref):
        c1, s1 = c_ref[...], s_ref[...]
        gkv = gk_ref[...]
        bdv = bd_ref[0:LANE, 0:LANE]
        _, xh, r = _headnorm_rope(kv_ref[:, :LANE], gkv, c1, s1, bdv)
        dx, dg = _headnorm_rope_bwd(dk_ref[...], xh, r, gkv, c1, s1, bdv)
        dh_ref[:, :LANE] = dx.astype(BF16)
        dh_ref[:, LANE:] = dv_ref[...].astype(BF16)

        @pl.when(pl.program_id(0) == 0)
        def _():
            dg_ref[...] = jnp.zeros_like(dg_ref)

        dg_ref[...] += dg

    row128 = pl.BlockSpec((tm, LANE), lambda i: (i, 0))
    kvblk = pl.BlockSpec((tm, 256), lambda i: (i, C_KB // 256))
    vec = pl.BlockSpec((1, LANE), lambda i: (0, 0))
    return pl.pallas_call(
        body, name=name, grid=(Lp // tm,),
        in_specs=[kvblk, vec, row128, row128, pl.BlockSpec((512, 512), lambda i: (0, 0)), row128, row128,
                  pl.BlockSpec(memory_space=pl.ANY)],
        out_specs=[kvblk, vec], out_shape=[SDS(dH.shape, BF16), SDS((1, LANE), F32)],
        input_output_aliases={7: 0})(H, gk, cos, sin, bd, dkr, dvb, dH)


def _ridden_call(core, rider, *, name, grid, in_specs, out_specs, out_shape, scratch_shapes, args):
    n_in, n_out, n_scr = len(in_specs), len(out_specs), len(scratch_shapes)
    r_in = len(rider.ins) if rider else 0
    r_out = len(rider.out_shape) if rider else 0
    total = int(np.prod(grid))
    mid_step = max(total - 1 - RIDER_MID_BACK, 0)

    def body(*refs):
        ins, r_ins = refs[:n_in], refs[n_in:n_in + r_in]
        o0 = n_in + r_in
        outs, r_outs = refs[o0:o0 + n_out], refs[o0 + n_out:o0 + n_out + r_out]
        s0 = o0 + n_out + r_out
        scr, r_sems = refs[s0:s0 + n_scr], refs[s0 + n_scr:]
        if rider is None:
            core(ins, outs, scr)
            return
        step = pl.program_id(0)
        for ax in range(1, len(grid)):
            step = step * grid[ax] + pl.program_id(ax)
        start, mid, end = rider.hooks(r_ins, r_outs, r_sems)
        pl.when(step == 0)(start)
        core(ins, outs, scr)
        pl.when(step == mid_step)(mid)
        pl.when(step == total - 1)(end)

    any_spec = pl.BlockSpec(memory_space=pl.ANY)
    res = pl.pallas_call(
        body, name=name, grid=grid, in_specs=list(in_specs) + [any_spec] * r_in, out_specs=list(out_specs) + [any_spec] * r_out,
        out_shape=list(out_shape) + (list(rider.out_shape) if rider else []),
        scratch_shapes=list(scratch_shapes) + (list(rider.sems) if rider else []),
        input_output_aliases={n_in + a: n_out + b for a, b in rider.aliases.items()} if rider else {},
        compiler_params=_cp(True))(*args, *(rider.ins if rider else []))
    return res[:n_out], res[n_out:]


def _att_fwd(name, qp, kr, vb, bias, rider=None):
    Lp = kr.shape[0]
    tq = _rt(Lp, 384)
    sub = ATT_SUB if tq % ATT_SUB == 0 else tq

    def core(ins, outs, scr):
        q_ref, k_ref, v_ref, bias_ref = ins
        b_ref, bt_ref, lse_ref = outs
        j, i = pl.program_id(0), pl.program_id(1)
        valid = _row_ids(i, tq) >= PAD
        kk, vv, bb = k_ref[...], v_ref[...], bias_ref[...]
        outs = []
        for e in range(2):
            parts = []
            for r in range(tq // sub):
                rs = slice(r * sub, (r + 1) * sub)
                s = lax.dot_general(q_ref[e, rs, :], kk, NT, preferred_element_type=F32) + bb
                m = jnp.max(s, axis=-1, keepdims=True)
                p = jnp.exp(s - m)
                l = jnp.sum(p, axis=-1, keepdims=True)
                parts.append(jnp.dot(p.astype(BF16), vv, preferred_element_type=F32) * (1.0 / l))
                lse_ref[e, rs, :] = m + jnp.log(l)
            outs.append(jnp.where(valid, jnp.concatenate(parts, axis=0), 0.0))
        lane = lax.broadcasted_iota(jnp.int32, (tq, LANE), 1)
        low = j < 2
        o0 = jnp.where(low, outs[0], pltpu.roll(outs[0], 64, 1))
        o1 = jnp.where(low, pltpu.roll(outs[1], 64, 1), outs[1])
        blk = jnp.where(lane < 64, o0, o1)
        b_ref[...] = blk.astype(BF16)
        bt_ref[...] = blk.T.astype(BF16)

    full = pl.BlockSpec((Lp, LANE), lambda j, i: (0, 0))
    return _ridden_call(
        core, rider, name=name, grid=(ATT_H // 2, Lp // tq),
        in_specs=[pl.BlockSpec((2, tq, LANE), lambda j, i: (j, i, 0)), full, full, pl.BlockSpec((1, Lp), lambda j, i: (0, 0))],
        out_specs=[pl.BlockSpec((tq, LANE), lambda j, i: (i, j)), pl.BlockSpec((LANE, tq), lambda j, i: (j, i)),
                   pl.BlockSpec((2, tq, 1), lambda j, i: (j, i, 0))],
        out_shape=[SDS((Lp, 512), BF16), SDS((512, Lp), BF16), SDS((ATT_H, Lp, 1), F32)],
        scratch_shapes=[], args=(qp, kr, vb, bias))


def _att_bwd(name, qp, kr, vb, bias, lse, db, rider=None):
    Lp = kr.shape[0]
    tq = _rt(Lp, 384)
    sub = ATT_SUB if tq % ATT_SUB == 0 else tq
    nq = Lp // tq

    def core(ins, outs, scr):
        q_ref, k_ref, v_ref, bias_ref, lse_ref, db_ref = ins
        dq_ref, dk_ref, dv_ref = outs
        dkt_scr, dvt_scr = scr
        j, i = pl.program_id(0), pl.program_id(1)

        @pl.when((j == 0) & (i == 0))
        def _():
            dkt_scr[...] = jnp.zeros_like(dkt_scr)
            dvt_scr[...] = jnp.zeros_like(dvt_scr)

        kk, vv, bb = k_ref[...], v_ref[...], bias_ref[...]
        dbv = db_ref[...]
        rolled = pltpu.roll(dbv, 64, 1)
        lane = lax.broadcasted_iota(jnp.int32, (tq, LANE), 1)
        low = j < 2
        first = jnp.where(low, 0, 64)
        keep = (lane >= first) & (lane < first + 64)
        dkt, dvt = 0.0, 0.0
        for e in range(2):
            src = jnp.where(low, dbv, rolled) if e == 0 else jnp.where(low, rolled, dbv)
            dop = jnp.where(keep, src, 0.0).astype(BF16)
            pbs, dss = [], []
            for r in range(tq // sub):
                rs = slice(r * sub, (r + 1) * sub)
                s = lax.dot_general(q_ref[e, rs, :], kk, NT, preferred_element_type=F32) + bb
                p = jnp.exp(s - lse_ref[e, rs, :])
                dp = lax.dot_general(dop[rs], vv, NT, preferred_element_type=F32)
                ds = (p * (dp - jnp.sum(p * dp, axis=-1, keepdims=True))).astype(BF16)
                dq_ref[e, rs, :] = jnp.where(keep[rs], jnp.dot(ds, kk, preferred_element_type=F32), 0.0)
                pbs.append(p.astype(BF16))
                dss.append(ds)
            dkt = dkt + lax.dot_general(q_ref[e], jnp.concatenate(dss, axis=0), TN, preferred_element_type=F32)
            dvt = dvt + lax.dot_general(dop, jnp.concatenate(pbs, axis=0), TN, preferred_element_type=F32)
        dkt_scr[...] += dkt
        dvt_scr[...] += dvt

        @pl.when((j == ATT_H // 2 - 1) & (i == nq - 1))
        def _():
            dk_ref[...] = dkt_scr[...].T
            dv_ref[...] = dvt_scr[...].T

    full = pl.BlockSpec((Lp, LANE), lambda j, i: (0, 0))
    pair = pl.BlockSpec((2, tq, LANE), lambda j, i: (j, i, 0))
    return _ridden_call(
        core, rider, name=name, grid=(ATT_H // 2, nq),
        in_specs=[pair, full, full, pl.BlockSpec((1, Lp), lambda j, i: (0, 0)), pl.BlockSpec((2, tq, 1), lambda j, i: (j, i, 0)),
                  pl.BlockSpec((tq, LANE), lambda j, i: (i, j))],
        out_specs=[pair, full, full],
        out_shape=[SDS((ATT_H, Lp, LANE), F32), SDS((Lp, LANE), F32), SDS((Lp, LANE), F32)],
        scratch_shapes=[pltpu.VMEM((LANE, Lp), F32), pltpu.VMEM((LANE, Lp), F32)], args=(qp, kr, vb, bias, lse, db))


def _rope_tables(n_tok):
    rows = n_tok // GRID_W
    row = jnp.repeat(jnp.arange(rows), GRID_W).astype(F32)
    col = jnp.tile(jnp.arange(GRID_W), rows).astype(F32)
    axis_dim = HD // 2
    inv = ROPE_THETA ** (-jnp.arange(0, axis_dim, 2, dtype=F32) / axis_dim)
    ang = jnp.concatenate([row[:, None] * inv, col[:, None] * inv], axis=-1)
    ang = jnp.concatenate([jnp.zeros((LANE, axis_dim), F32), ang], axis=0)
    c, s = jnp.cos(ang), jnp.sin(ang)
    c64 = jnp.concatenate([c[:, :16], c[:, :16], c[:, 16:], c[:, 16:]], axis=1)
    s64 = jnp.concatenate([-s[:, :16], s[:, :16], -s[:, 16:], s[:, 16:]], axis=1)
    return jnp.concatenate([c64, c64], axis=1), jnp.concatenate([s64, s64], axis=1)


def _ffn_fwd(tag, h, W, l, j):
    Lp = h.shape[0]
    tm = _rt(Lp, 384)
    sub = FFN_SUB if tm % FFN_SUB == 0 else tm
    ni = Lp // tm
    g = W["norm_gains"][l, 2 * j].reshape(1, D)
    last = NSH - 1

    def body(h_ref, g_ref, wg_ref, wu_ref, wd_ref, h2_ref, a_ref, b_ref, st_ref, nt_ref, n_scr, acc_scr, pad_scr):
        s, i = pl.program_id(0), pl.program_id(1)
        rows = pl.ds(pl.multiple_of(i * tm, tm), tm)

        @pl.when(s == 0)
        def _():
            x = h_ref[...]
            y = x * lax.rsqrt(jnp.mean(x * x, axis=-1, keepdims=True) + EPS) * g_ref[...]
            n_scr[rows, :] = y.astype(BF16)
            nt_ref[...] = y.T.astype(BF16)
            acc_scr[rows, :] = jnp.zeros((tm, D), F32)

        a_ref[:, FF_S:] = jnp.zeros((tm, FF_P - FF_S), BF16)
        b_ref[:, FF_S:] = jnp.zeros((tm, FF_P - FF_S), BF16)
        pad_scr[:, FF_S:] = jnp.zeros((tm, FF_P - FF_S), F32)
        for r in range(tm // sub):
            rs = slice(r * sub, (r + 1) * sub)
            rr = pl.ds(pl.multiple_of(i * tm + r * sub, sub), sub)
            nv = n_scr[rr, :]
            a = jnp.dot(nv, wg_ref[...], preferred_element_type=F32)
            b = jnp.dot(nv, wu_ref[...], preferred_element_type=F32)
            sv = a * jax.nn.sigmoid(a) * b
            a_ref[rs, :FF_S] = a.astype(BF16)
            b_ref[rs, :FF_S] = b.astype(BF16)
            pad_scr[rs, :FF_S] = sv
            acc_scr[rr, :] += jnp.dot(sv.astype(BF16), wd_ref[...], preferred_element_type=F32)
        st_ref[...] = pad_scr[...].T.astype(BF16)

        @pl.when(s == last)
        def _():
            h2_ref[...] = h_ref[...] + 0.5 * acc_scr[rows, :]

    slab = pl.BlockSpec((None, tm, FF_P), lambda s, i: (s, i, 0))
    wup = pl.BlockSpec((None, None, D, FF_S), lambda s, i: (s, l, j, 0))
    h2, a, b, st, nt = pl.pallas_call(
        body, name=f"ffn_{tag}", grid=(NSH, ni),
        in_specs=[pl.BlockSpec((tm, D), lambda s, i: (jnp.where((s == 0) | (s == last), i, ni - 1), 0)),
                  pl.BlockSpec((1, D), lambda s, i: (0, 0)), wup, wup,
                  pl.BlockSpec((None, None, FF_S, D), lambda s, i: (s, l, j, 0))],
        out_specs=[pl.BlockSpec((tm, D), lambda s, i: (jnp.where(s == last, i, 0), 0)), slab, slab,
                   pl.BlockSpec((None, FF_P, tm), lambda s, i: (s, 0, i)),
                   pl.BlockSpec((D, tm), lambda s, i: (0, jnp.where(s == 0, i, ni - 1)))],
        out_shape=[SDS((Lp, D), F32), SDS((NSH, Lp, FF_P), BF16), SDS((NSH, Lp, FF_P), BF16), SDS((NSH, FF_P, Lp), BF16),
                   SDS((D, Lp), BF16)],
        scratch_shapes=[pltpu.VMEM((Lp, D), BF16), pltpu.VMEM((Lp, D), F32), pltpu.VMEM((tm, FF_P), F32)],
        compiler_params=_cp(True))(h, g, W["gate"], W["up"], W["down"])
    return h2, dict(h=h, g=g, nt=nt, a=a, b=b, st=st)


def _ffn_bwd(tag, dh, sv, W, G, l, j):
    Lp = dh.shape[0]
    tn = 512
    da, db, dh_prev, dg = _ffn_bwd_acts(tag, dh, sv, W, l, j)
    G["down"] = _mm(f"bwd_{tag}", sv["st"], dh, grid=(NSH, D // tn),
                    a_spec=pl.BlockSpec((None, FF_P, Lp), lambda s, n: (s, 0, 0)), a_sl=(slice(0, FF_S), slice(None)),
                    b_spec=pl.BlockSpec((Lp, tn), lambda s, n: (0, n)),
                    o_spec=pl.BlockSpec((None, None, FF_S, tn), lambda s, n: (s, j, 0, n)), o_shape=(NSH, 2, FF_S, D), o_dtype=F32,
                    dims=NN, acc_shape=None, scale=0.5, into=G.get("down"))
    for key, dact in (("gate", da), ("up", db)):
        G[key] = _mm(f"bw{key[0]}_{tag}", sv["nt"], dact, grid=(NSH, D // tn),
                     a_spec=pl.BlockSpec((tn, Lp), lambda s, m: (m, 0)),
                     b_spec=pl.BlockSpec((None, Lp, FF_P), lambda s, m: (s, 0, 0)), b_sl=(slice(None), slice(0, FF_S)),
                     o_spec=pl.BlockSpec((None, None, tn, FF_S), lambda s, m: (s, j, m, 0)), o_shape=(NSH, 2, D, FF_S),
                     o_dtype=F32, dims=NN, acc_shape=None, into=G.get(key))
    return dh_prev, dg


def _ffn_bwd_acts(tag, dh, sv, W, l, j):
    Lp = dh.shape[0]
    tm = _rt(Lp, 384)
    sub = FFN_SUB if tm % FFN_SUB == 0 else tm
    last = NSH - 1

    def body(dh_ref, h_ref, g_ref, wd_ref, wg_ref, wu_ref, a_ref, b_ref, da_ref, db_ref, dho_ref, dg_ref, acc_scr):
        s, i = pl.program_id(0), pl.program_id(1)
        rows = pl.ds(pl.multiple_of(i * tm, tm), tm)
        @pl.when(s == 0)
        def _():
            acc_scr[rows, :] = jnp.zeros((tm, D), F32)

        @pl.when((s == 0) & (i == 0))
        def _():
            dg_ref[...] = jnp.zeros_like(dg_ref)

        da_ref[:, FF_S:] = jnp.zeros((tm, FF_P - FF_S), BF16)
        db_ref[:, FF_S:] = jnp.zeros((tm, FF_P - FF_S), BF16)
        for r in range(tm // sub):
            rs = slice(r * sub, (r + 1) * sub)
            rr = pl.ds(pl.multiple_of(i * tm + r * sub, sub), sub)
            ds = 0.5 * lax.dot_general(dh_ref[rs, :].astype(BF16), wd_ref[...], NT, preferred_element_type=F32)
            av = a_ref[rs, :FF_S].astype(F32)
            bv = b_ref[rs, :FF_S].astype(F32)
            sg = jax.nn.sigmoid(av)
            da = (ds * bv * (sg * (1.0 + av * (1.0 - sg)))).astype(BF16)
            db = (ds * (av * sg)).astype(BF16)
            da_ref[rs, :FF_S] = da
            db_ref[rs, :FF_S] = db
            acc_scr[rr, :] += (lax.dot_general(da, wg_ref[...], NT, preferred_element_type=F32)
                               + lax.dot_general(db, wu_ref[...], NT, preferred_element_type=F32))

        @pl.when(s == last)
        def _():
            x = h_ref[...]
            r = lax.rsqrt(jnp.mean(x * x, axis=-1, keepdims=True) + EPS)
            xh = x * r
            dn = acc_scr[rows, :]
            u = dn * g_ref[...]
            dho_ref[...] = dh_ref[...] + r * (u - xh * jnp.mean(u * xh, axis=-1, keepdims=True))
            dg_ref[...] += jnp.sum(dn * xh, axis=0, keepdims=True)

    slab = pl.BlockSpec((None, tm, FF_P), lambda s, i: (s, i, 0))
    wup = pl.BlockSpec((None, None, D, FF_S), lambda s, i: (s, l, j, 0))
    at_last = pl.BlockSpec((tm, D), lambda s, i: (jnp.where(s == last, i, 0), 0))
    vec = pl.BlockSpec((1, D), lambda s, i: (0, 0))
    return pl.pallas_call(
        body, name=f"bffn_{tag}", grid=(NSH, Lp // tm),
        in_specs=[pl.BlockSpec((tm, D), lambda s, i: (i, 0)), at_last, vec,
                  pl.BlockSpec((None, None, FF_S, D), lambda s, i: (s, l, j, 0)), wup, wup, slab, slab],
        out_specs=[slab, slab, at_last, vec],
        out_shape=[SDS((NSH, Lp, FF_P), BF16), SDS((NSH, Lp, FF_P), BF16), SDS((Lp, D), F32), SDS((1, D), F32)],
        scratch_shapes=[pltpu.VMEM((Lp, D), F32)], compiler_params=_cp(True))(
            dh, sv["h"], sv["g"], W["down"], W["gate"], W["up"], sv["a"], sv["b"])


def _mixer_fwd(tag, h, W, winp, C, l, G_chunks, rider=None):
    Lp = h.shape[0]
    tm = _rt(Lp, 1408)
    g = W["norm_gains"][l, 1].reshape(1, D)
    z, zt = _rms_fwd(f"rmsf_{tag}", h, g)
    tn = 896
    H = _mm(f"win_{tag}", z, winp, grid=(HP // tn, Lp // tm), a_spec=pl.BlockSpec((tm, D), lambda n, i: (i, 0)),
            b_spec=pl.BlockSpec((D, tn), lambda n, i: (0, n)), o_spec=pl.BlockSpec((tm, tn), lambda n, i: (i, n)),
            o_shape=(Lp, HP), o_dtype=F32, dims=NN, acc_shape=None)
    w2p, b2p = C["w2p"][l], C["b2p"][l]
    qh, kh, vh, gf, gb = _gla_prep(f"glap_{tag}", H, w2p, b2p)
    of, sf = _gla_fwd(f"glaf_{tag}", qh, kh, gf, vh, False, G_chunks)
    ob, sb = _gla_fwd(f"glar_{tag}", qh, kh, gb, vh, True, G_chunks)
    gn = W["gn"][l].reshape(1, 512)
    a, at = _gla_post(f"glao_{tag}", of, ob, H, gn)
    gq, gk = C["gq"][l], C["gk"][l]
    qp, kr, vb = _att_prep(f"attp_{tag}", H, gq, gk, C["cos"], C["sin"], C["bd"])
    (b, bt, lse), carried = _att_fwd(f"attf_{tag}", qp, kr, vb, C["bias"], rider)
    if rider is not None:
        W = rider.updated(W, carried)
    proj = functools.partial(
        _mm, grid=(NSH, Lp // tm), a_spec=pl.BlockSpec((tm, 512), lambda s, i: (i, 0)),
        b_spec=pl.BlockSpec((None, None, 512, 256), lambda s, i: (s, l, 0, 0)), o_spec=pl.BlockSpec((tm, 256), lambda s, i: (i, s)),
        o_shape=(Lp, D), o_dtype=F32, dims=NN, acc_shape=None)
    pa = proj(f"pa_{tag}", a, W["wpa"])
    pb = proj(f"pb_{tag}", b, W["wpb"])
    bm = W["bm"][l]
    y, yt = _merge_fwd(f"mrg_{tag}", H, pa, pb, bm)
    row = pl.BlockSpec((tm, D), lambda i, s: (i, 0))
    h2 = _mm(f"wout_{tag}", y, W["wout"], grid=(Lp // tm, NSH), a_spec=pl.BlockSpec((tm, 256), lambda i, s: (i, s)),
             b_spec=pl.BlockSpec((None, None, 256, D), lambda i, s: (s, l, 0, 0)), o_spec=row, o_shape=(Lp, D), o_dtype=F32,
             dims=NN, acc_shape=(tm, D), nk=NSH, res=h, res_spec=row)
    sv = dict(h=h, g=g, zt=zt, H=H, w2p=w2p, b2p=b2p, qh=qh, kh=kh, vh=vh, gf=gf, gb=gb, of=of, ob=ob, sf=sf, sb=sb, gn=gn,
              at=at, gq=gq, gk=gk, qp=qp, kr=kr, vb=vb, bt=bt, lse=lse, pa=pa, pb=pb, bm=bm, yt=yt, winp=winp)
    return h2, sv, W


def _mixer_bwd(tag, dh, sv, W, C, G, S, l, G_chunks, rider=None):
    Lp = dh.shape[0]
    tm = _rt(Lp, 1408)
    H = sv["H"]
    dy = _mm(f"bdy_{tag}", dh, W["wout"], grid=(NSH, Lp // tm), a_spec=pl.BlockSpec((tm, D), lambda s, i: (i, 0)),
             b_spec=pl.BlockSpec((None, None, 256, D), lambda s, i: (s, l, 0, 0)), o_spec=pl.BlockSpec((tm, 256), lambda s, i: (i, s)),
             o_shape=(Lp, D), o_dtype=F32, dims=NT, acc_shape=None)
    tn = 512
    G["wout"] = _mm(f"bwo_{tag}", sv["yt"], dh, grid=(NSH, D // tn), a_spec=pl.BlockSpec((256, Lp), lambda s, n: (s, 0)),
                    b_spec=pl.BlockSpec((Lp, tn), lambda s, n: (0, n)),
                    o_spec=pl.BlockSpec((None, 256, tn), lambda s, n: (s, 0, n)), o_shape=(NSH, 256, D), o_dtype=F32, dims=NN,
                    acc_shape=None)
    dpa, dpb, dH, S["bm"][l] = _merge_bwd(f"bmrg_{tag}", H, sv["pa"], sv["pb"], sv["bm"], dy)
    dbranch = {}
    for key, dp, xt in (("wpa", dpa, sv["at"]), ("wpb", dpb, sv["bt"])):
        dbranch[key] = _mm(f"bx{key[2]}_{tag}", dp, W[key], grid=(Lp // tm, NSH),
                           a_spec=pl.BlockSpec((tm, 256), lambda i, s: (i, s)),
                           b_spec=pl.BlockSpec((None, None, 512, 256), lambda i, s: (s, l, 0, 0)),
                           o_spec=pl.BlockSpec((tm, 512), lambda i, s: (i, 0)), o_shape=(Lp, 512), o_dtype=F32, dims=NT,
                           acc_shape=(tm, 512), nk=NSH)
        G[key] = _mm(f"bw{key[2]}_{tag}", xt, dp, grid=(NSH,), a_spec=pl.BlockSpec((512, Lp), lambda s: (0, 0)),
                     b_spec=pl.BlockSpec((Lp, 256), lambda s: (0, s)),
                     o_spec=pl.BlockSpec((None, 512, 256), lambda s: (s, 0, 0)), o_shape=(NSH, 512, 256), o_dtype=F32, dims=NN,
                     acc_shape=None)
    (dqp, dkr, dvb), carried = _att_bwd(f"attb_{tag}", sv["qp"], sv["kr"], sv["vb"], C["bias"], sv["lse"], dbranch["wpb"],
                                        rider)
    dH, S["gq"][l] = _att_q_bwd(f"attq_{tag}", H, sv["gq"], C["cos"], C["sin"], C["bd"], dqp, dH)
    dH, S["gk"][l] = _att_kv_bwd(f"attk_{tag}", H, sv["gk"], C["cos"], C["sin"], C["bd"], dkr, dvb, dH)
    do, dH, S["gn"][l] = _gla_post_bwd(f"bglo_{tag}", sv["of"], sv["ob"], H, sv["gn"], dbranch["wpa"], dH)
    dqf, dkf, dgf, dvf = _gla_bwd(f"bglf_{tag}", sv["qh"], sv["kh"], sv["gf"], sv["vh"], sv["sf"], do, False, G_chunks)
    dqb, dkb, dgb, dvr = _gla_bwd(f"bglr_{tag}", sv["qh"], sv["kh"], sv["gb"], sv["vh"], sv["sb"], do, True, G_chunks)
    dH = _gla_qkv_bwd(f"bglq_{tag}", dqf, dqb, dkf, dkb, dvf, dvr, dH)
    dH, S["w2p"][l], S["b2p"][l] = _gla_gate_bwd(f"bglg_{tag}", H, sv["w2p"], sv["b2p"], dgf, dgb, dH)
    tmm = 256
    G["winp"] = _mm(f"bwi_{tag}", sv["zt"], dH, grid=(D // tmm, HP // 896), a_spec=pl.BlockSpec((tmm, Lp), lambda m, n: (m, 0)),
                    b_spec=pl.BlockSpec((Lp, 896), lambda m, n: (0, n)), o_spec=pl.BlockSpec((tmm, 896), lambda m, n: (m, n)),
                    o_shape=(D, HP), o_dtype=F32, dims=NN, acc_shape=None)
    dz = _mm(f"bdz_{tag}", dH, sv["winp"], grid=(D // 512, Lp // tm), a_spec=pl.BlockSpec((tm, HP), lambda n, i: (i, 0)),
             b_spec=pl.BlockSpec((512, HP), lambda n, i: (n, 0)), o_spec=pl.BlockSpec((tm, 512), lambda n, i: (i, n)),
             o_shape=(Lp, D), o_dtype=F32, dims=NT, acc_shape=None)
    dh_prev, dg = _rms_bwd(f"rmsb_{tag}", dz, sv["h"], sv["g"], dh)
    return dh_prev, dg, carried


def _winp_layer(win_g, l):
    return _win_to_padded(jnp.transpose(win_g[:, l], (1, 0, 2)).reshape(D, D_IN))


def _local_step(x2, tgt2, W, comm=None):
    n_tok = x2.shape[0]
    Lp = n_tok + LANE
    nc = Lp // CHUNK
    g_chunks = max(g for g in (1, 2, 3, 6) if nc % g == 0)
    cos, sin = _rope_tables(n_tok)
    bd = jnp.asarray(np.kron(np.eye(ATT_H, dtype=np.float32), np.full((HD, HD), 1.0 / HD, np.float32)))
    bias = jnp.where(jnp.arange(Lp) >= PAD, 0.0, -1e30).astype(F32).reshape(1, Lp)
    w2, b2 = W["w2"], W["b2"]
    w2p = jnp.zeros((DEPTH, LANE, 512), F32)
    w2p = w2p.at[:, 0:GLA_RANK, 0:256].set(w2[:, 0]).at[:, GLA_RANK:2 * GLA_RANK, 256:512].set(w2[:, 1]).astype(BF16)
    C = dict(cos=cos, sin=sin, bd=bd, bias=bias, w2p=w2p, b2p=b2.reshape(DEPTH, 1, 512),
             gq=jnp.tile(W["qn"], (1, ATT_H)).reshape(DEPTH, 1, 512), gk=jnp.tile(W["kn"], (1, ATT_KV)).reshape(DEPTH, 1, LANE))
    h = jnp.concatenate([jnp.zeros((PAD, D), F32), W["meta"], x2], axis=0)
    saved = []
    for l in range(DEPTH):
        h, s0 = _ffn_fwd(f"l{l}a", h, W, l, 0)
        rider = comm.fwd_rider(W, l) if comm else None
        h, sm, W = _mixer_fwd(f"l{l}m", h, W, _winp_layer(W["win"], l), C, l, g_chunks, rider)
        h, s1 = _ffn_fwd(f"l{l}b", h, W, l, 1)
        saved.append((s0, sm, s1))
    dh, dfin, loss = _loss_head("loss_head", h, W["fin"].reshape(1, D), tgt2)
    S = dict(bm=[None] * DEPTH, gq=[None] * DEPTH, gk=[None] * DEPTH, gn=[None] * DEPTH, w2p=[None] * DEPTH,
             b2p=[None] * DEPTH, ng=[[None] * 3 for _ in range(DEPTH)])
    Gs = [None] * DEPTH
    for l in reversed(range(DEPTH)):
        s0, sm, s1 = saved[l]
        G = {}
        dh, S["ng"][l][2] = _ffn_bwd(f"l{l}b", dh, s1, W, G, l, 1)
        rider = comm.bwd_rider(l) if comm else None
        dh, S["ng"][l][1], carried = _mixer_bwd(f"l{l}m", dh, sm, W, C, G, S, l, g_chunks, rider)
        if rider is not None:
            comm.bwd_carried(l, carried)
        dh, S["ng"][l][0] = _ffn_bwd(f"l{l}a", dh, s0, W, G, l, 0)
        Gs[l] = G
        if comm:
            comm.layer_done(l, G)
    small = dict(
        meta=dh[PAD:LANE],
        norm_gains=jnp.stack([jnp.concatenate(S["ng"][l], axis=0) for l in range(DEPTH)]),
        w2=jnp.stack([jnp.stack([S["w2p"][l][0:GLA_RANK, 0:256], S["w2p"][l][GLA_RANK:2 * GLA_RANK, 256:512]]) for l in range(DEPTH)]),
        b2=jnp.stack([S["b2p"][l].reshape(2, 256) for l in range(DEPTH)]),
        gn=jnp.concatenate(S["gn"], axis=0),
        qn=jnp.stack([S["gq"][l].reshape(ATT_H, HD).sum(0) for l in range(DEPTH)]),
        kn=jnp.stack([S["gk"][l].reshape(ATT_KV, HD).sum(0) for l in range(DEPTH)]),
        bm=jnp.stack(S["bm"]),
        fin=dfin.reshape(D),
    )
    return loss[0, 0], dh, Gs, small


def _win_to_padded(w):
    pad = jnp.zeros(w.shape[:-1] + (HP - D_IN,), w.dtype)
    return jnp.concatenate([w[..., 2336:4384], w[..., 0:1536], w[..., 1568:2336], w[..., 1536:1568], pad], axis=-1)


def _win_from_padded(w):
    return jnp.concatenate([w[..., 2048:3584], w[..., 4352:4384], w[..., 3584:4352], w[..., 0:2048]], axis=-1)


def _assemble(g):
    W = dict(gate=g["gate"], up=g["up"], down=g["down"], win=g["win"], wpa=g["wpa"], wpb=g["wpb"], wout=g["wout"])
    sm = g["small"]
    parts = _unpack(sm, SHARDED_SMALL)
    W["meta"] = jnp.transpose(parts["meta"], (1, 0, 2)).reshape(N_META, D)
    W["norm_gains"] = jnp.transpose(parts["norm_gains"], (1, 2, 0, 3)).reshape(DEPTH, 3, D)
    W["w2"] = jnp.transpose(parts["w2"], (1, 2, 3, 0, 4)).reshape(DEPTH, 2, GLA_RANK, 256)
    W["b2"] = jnp.transpose(parts["b2"], (1, 2, 0, 3)).reshape(DEPTH, 2, 256)
    W["bm"] = jnp.transpose(parts["bm"], (1, 2, 0, 3)).reshape(DEPTH, 2, D)
    return W


SHARDED_SMALL = dict(meta=(N_META, 256), norm_gains=(DEPTH, 3, 256), w2=(DEPTH, 2, GLA_RANK, 64), b2=(DEPTH, 2, 64),
                     bm=(DEPTH, 2, 256))
FULL_SMALL = dict(meta=(N_META, D), norm_gains=(DEPTH, 3, D), w2=(DEPTH, 2, GLA_RANK, 256), b2=(DEPTH, 2, 256),
                  gn=(DEPTH, 512), qn=(DEPTH, HD), kn=(DEPTH, HD), bm=(DEPTH, 2, D), fin=(D,))


def _pack(parts, table, rows):
    flat = jnp.concatenate([parts[k].reshape(-1).astype(F32) for k in table])
    return jnp.pad(flat, (0, rows * LANE - flat.shape[0])).reshape(rows, LANE)


def _unpack(packed, table):
    lead = packed.shape[:-2]
    flat = packed.reshape(lead + (-1,))
    out, off = {}, 0
    for k, shp in table.items():
        n = int(np.prod(shp))
        out[k] = flat[..., off:off + n].reshape(lead + tuple(shp))
        off += n
    return out


def _rows_for(table, mult):
    n = sum(int(np.prod(s)) for s in table.values())
    return -(-n // (LANE * mult)) * mult


SMALL_ROWS = _rows_for(SHARDED_SMALL, 16)
GRAD_ROWS = _rows_for(dict(FULL_SMALL, loss=(1,)), 8)


def _place():
    x, y, c = lax.axis_index("x"), lax.axis_index("y"), lax.axis_index("c")
    return x, y, c


def _other_chips(x, y):
    return [(1 - x, y), (x, 1 - y), (1 - x, 1 - y)]


def _cast_place(name, w3, slot):
    A, R, Cc = w3.shape
    tr = _rt(R, 512, 16)

    def body(p_ref, w_ref, o_ref):
        o_ref[...] = w_ref[...].astype(BF16)

    return pl.pallas_call(
        body, name=name,
        grid_spec=pltpu.PrefetchScalarGridSpec(
            num_scalar_prefetch=1, grid=(A, R // tr),
            in_specs=[pl.BlockSpec((None, tr, Cc), lambda a, r, p_ref: (a, r, 0))],
            out_specs=pl.BlockSpec((None, None, tr, Cc), lambda a, r, p_ref: (p_ref[0], a, r, 0))),
        out_shape=SDS((NSH, A, R, Cc), BF16))(slot, w3)


class _GatherRider:
    def __init__(self, keys, bufs, layers):
        n = len(bufs)
        self.keys, self.ins, self.layers = keys, list(bufs), layers
        self.out_shape = [SDS(b.shape, b.dtype) for b in bufs]
        self.aliases = {a: a for a in range(n)}
        self.sems = [pltpu.SemaphoreType.DMA((n, 3)) for _ in range(4)]

    def updated(self, W, carried):
        return {**W, **dict(zip(self.keys, carried))}

    def hooks(self, ins, outs, sems):
        send, recv, fsend, frecv = sems
        n = len(outs)

        def rows(a, slot, core):
            ref, l = outs[a], self.layers[a]
            half = ref.shape[-2] // 2
            return ref.at[slot, pl.ds(core * half, half)] if l is None else ref.at[slot, l, pl.ds(core * half, half)]

        def ici(a, k, part, px, py, c):
            return pltpu.make_async_remote_copy(src_ref=part, dst_ref=part, send_sem=send.at[a, k], recv_sem=recv.at[a, k],
                                                device_id=(px, py, c), device_id_type=MESH)

        def d2d(a, k, part, x, y, c):
            return pltpu.make_async_remote_copy(src_ref=part, dst_ref=part, send_sem=fsend.at[a, k], recv_sem=frecv.at[a, k],
                                                device_id=(x, y, 1 - c), device_id_type=MESH)

        def start():
            x, y, c = _place()
            for a in range(n):
                for k, (px, py) in enumerate(_other_chips(x, y)):
                    ici(a, k, rows(a, 2 * x + y, c), px, py, c).start()

        def mid():
            x, y, c = _place()
            for a in range(n):
                for k, (px, py) in enumerate(_other_chips(x, y)):
                    landed = rows(a, 2 * px + py, c)
                    ici(a, k, landed, px, py, c).wait_recv()
                    d2d(a, k, landed, x, y, c).start()

        def end():
            x, y, c = _place()
            for a in range(n):
                for k, (px, py) in enumerate(_other_chips(x, y)):
                    d2d(a, k, rows(a, 2 * px + py, 1 - c), x, y, c).wait_recv()
                    ici(a, k, rows(a, 2 * x + y, c), px, py, c).wait_send()
                    d2d(a, k, rows(a, 2 * px + py, c), x, y, c).wait_send()

        return start, mid, end


class _ChipExchangeRider:
    def __init__(self, arrs):
        n = len(arrs)
        self.ins = list(arrs)
        self.out_shape = [SDS((3,) + a.shape[1:], a.dtype) for a in arrs]
        self.aliases = {}
        self.sems = [pltpu.SemaphoreType.DMA((n, 3)), pltpu.SemaphoreType.DMA((n, 3))]

    def hooks(self, ins, outs, sems):
        send, recv = sems

        def copy(a, k, px, py, c):
            return pltpu.make_async_remote_copy(src_ref=ins[a].at[2 * px + py], dst_ref=outs[a].at[k], send_sem=send.at[a, k],
                                                recv_sem=recv.at[a, k], device_id=(px, py, c), device_id_type=MESH)

        def start():
            x, y, c = _place()
            for a in range(len(ins)):
                for k, (px, py) in enumerate(_other_chips(x, y)):
                    copy(a, k, px, py, c).start()

        def mid():
            pass

        def end():
            x, y, c = _place()
            for a in range(len(ins)):
                for k, (px, py) in enumerate(_other_chips(x, y)):
                    copy(a, k, px, py, c).wait()

        return start, mid, end


def _run_rider(name, rider):
    r_in, r_out = len(rider.ins), len(rider.out_shape)

    def body(*refs):
        start, mid, end = rider.hooks(refs[:r_in], refs[r_in:r_in + r_out], refs[r_in + r_out:])
        start()
        mid()
        end()

    any_spec = pl.BlockSpec(memory_space=pl.ANY)
    return pl.pallas_call(
        body, name=name, in_specs=[any_spec] * r_in, out_specs=[any_spec] * r_out, out_shape=list(rider.out_shape),
        scratch_shapes=list(rider.sems), input_output_aliases=dict(rider.aliases))(*rider.ins)


def _pair_exchange(name, arrs):
    n = len(arrs)

    def body(*refs):
        ins, outs = refs[:n], refs[n:2 * n]
        send, recv = refs[2 * n:]
        x, y, c = _place()
        cps = []
        for a in range(n):
            half = ins[a].shape[1] // 2
            cp = pltpu.make_async_remote_copy(
                src_ref=ins[a].at[:, pl.ds((1 - c) * half, half)], dst_ref=outs[a], send_sem=send.at[a], recv_sem=recv.at[a],
                device_id=(x, y, 1 - c), device_id_type=MESH)
            cp.start()
            cps.append(cp)
        for cp in cps:
            cp.wait()

    any_spec = pl.BlockSpec(memory_space=pl.ANY)
    return pl.pallas_call(
        body, name=name, in_specs=[any_spec] * n, out_specs=[any_spec] * n,
        out_shape=[SDS((NSH, a.shape[1] // 2, a.shape[2]), a.dtype) for a in arrs],
        scratch_shapes=[pltpu.SemaphoreType.DMA((n,)), pltpu.SemaphoreType.DMA((n,))])(*arrs)


def _pair_add(name, g, p, core):
    _, Rh, Cc = p.shape
    tr = _rt(Rh, 512, 16)
    nr = Rh // tr

    def body(c_ref, g_ref, p_ref, o_ref, ob_ref):
        v = g_ref[...] + p_ref[...]
        o_ref[...] = v
        ob_ref[...] = v.astype(BF16)

    blk = pl.BlockSpec((None, tr, Cc), lambda s, r, c_ref: (s, r, 0))
    return pl.pallas_call(
        body, name=name,
        grid_spec=pltpu.PrefetchScalarGridSpec(
            num_scalar_prefetch=1, grid=(NSH, nr),
            in_specs=[pl.BlockSpec((None, tr, Cc), lambda s, r, c_ref: (s, c_ref[0] * nr + r, 0)), blk],
            out_specs=[blk, blk]),
        out_shape=[SDS((NSH, Rh, Cc), F32), SDS((NSH, Rh, Cc), BF16)])(core, g, p)


def _chip_add(name, hsum, q, chip, core, l, into):
    _, Rh, Cc = hsum.shape
    tr = _rt(Rh, 512, 8)
    nr = Rh // tr

    def body(*refs):
        h_ref, q_ref, o_ref = refs[2], refs[3], refs[-1]
        o_ref[...] = ((h_ref[...] + q_ref[0].astype(F32)) + q_ref[1].astype(F32)) + q_ref[2].astype(F32)

    in_specs = [pl.BlockSpec((None, tr, Cc), lambda r, p_ref, c_ref: (p_ref[0], r, 0)),
                pl.BlockSpec((3, tr, Cc), lambda r, p_ref, c_ref: (0, r, 0))]
    args = [chip, core, hsum, q]
    aliases = {}
    if into is not None:
        in_specs.append(pl.BlockSpec(memory_space=pl.ANY))
        args.append(into)
        aliases = {4: 0}
    return pl.pallas_call(
        body, name=name,
        grid_spec=pltpu.PrefetchScalarGridSpec(
            num_scalar_prefetch=2, grid=(nr,), in_specs=in_specs,
            out_specs=pl.BlockSpec((None, tr, Cc), lambda r, p_ref, c_ref: (l, c_ref[0] * nr + r, 0))),
        out_shape=SDS((DEPTH, 2 * Rh, Cc), F32), input_output_aliases=aliases)(*args)


def _pair_share(arrs):
    n = len(arrs)

    def body(*refs):
        outs = refs[n:2 * n]
        send, recv = refs[2 * n:]
        x, y, c = _place()
        cps = []
        for a in range(n):
            half = outs[a].shape[1] // 2
            mine = outs[a].at[:, pl.ds(c * half, half)]
            cp = pltpu.make_async_remote_copy(
                src_ref=mine, dst_ref=mine, send_sem=send.at[a], recv_sem=recv.at[a],
                device_id=(x, y, 1 - c), device_id_type=MESH)
            cp.start()
            cps.append(cp)
        for a, cp in enumerate(cps):
            cp.wait_send()
            half = outs[a].shape[1] // 2
            theirs = outs[a].at[:, pl.ds((1 - c) * half, half)]
            pltpu.make_async_remote_copy(
                src_ref=theirs, dst_ref=theirs, send_sem=send.at[a], recv_sem=recv.at[a],
                device_id=(x, y, 1 - c), device_id_type=MESH).wait_recv()

    any_spec = pl.BlockSpec(memory_space=pl.ANY)
    return pl.pallas_call(
        body, name="pair_share", in_specs=[any_spec] * n, out_specs=[any_spec] * n,
        out_shape=[SDS(a.shape, a.dtype) for a in arrs], input_output_aliases={a: a for a in range(n)},
        scratch_shapes=[pltpu.SemaphoreType.DMA((n,)), pltpu.SemaphoreType.DMA((n,))])(*arrs)


def _allreduce_small(v):
    rows = v.shape[0]

    def body(v_ref, o_ref, buf, send, recv):
        x, y, c = _place()
        me = 4 * x + 2 * y + c
        buf[me] = v_ref[...]
        cps = []
        k = 0
        for dx in range(2):
            for dy in range(2):
                for dc in range(2):
                    if dx + dy + dc == 0:
                        continue
                    cp = pltpu.make_async_remote_copy(
                        src_ref=v_ref, dst_ref=buf.at[me], send_sem=send.at[k], recv_sem=recv.at[k],
                        device_id=(jnp.bitwise_xor(x, dx), jnp.bitwise_xor(y, dy), jnp.bitwise_xor(c, dc)), device_id_type=MESH)
                    cp.start()
                    cps.append((cp, dx, dy, dc))
                    k += 1
        for k, (cp, dx, dy, dc) in enumerate(cps):
            cp.wait_send()
            src = 4 * jnp.bitwise_xor(x, dx) + 2 * jnp.bitwise_xor(y, dy) + jnp.bitwise_xor(c, dc)
            pltpu.make_async_remote_copy(
                src_ref=v_ref, dst_ref=buf.at[src], send_sem=send.at[k], recv_sem=recv.at[k],
                device_id=(x, y, c), device_id_type=MESH).wait_recv()
        acc = buf[0]
        for d in range(1, 8):
            acc = acc + buf[d]
        o_ref[...] = acc

    vm = pl.BlockSpec(memory_space=pltpu.VMEM)
    return pl.pallas_call(
        body, name="allreduce_small", in_specs=[vm], out_specs=vm, out_shape=SDS((rows, LANE), F32),
        scratch_shapes=[pltpu.VMEM((8, rows, LANE), F32), pltpu.SemaphoreType.DMA((7,)), pltpu.SemaphoreType.DMA((7,))])(v)


def _adamw(name, w, g, m, v):
    A, R, Cc = w.shape
    tr = _rt(R, 512, 8)

    def body(w_ref, g_ref, m_ref, v_ref, d_ref, mo_ref, vo_ref):
        gv = g_ref[...]
        mn = ADAM_B1 * m_ref[...] + (1.0 - ADAM_B1) * gv
        vn = ADAM_B2 * v_ref[...] + (1.0 - ADAM_B2) * (gv * gv)
        m_hat = mn / (1.0 - ADAM_B1 ** ADAM_STEP)
        v_hat = vn / (1.0 - ADAM_B2 ** ADAM_STEP)
        d_ref[...] = -ADAM_LR * (m_hat / (jnp.sqrt(v_hat) + ADAM_EPS) + ADAM_WD * w_ref[...])
        mo_ref[...] = mn
        vo_ref[...] = vn

    blk = pl.BlockSpec((None, tr, Cc), lambda a, r: (a, r, 0))
    return pl.pallas_call(
        body, name=name, grid=(A, R // tr), in_specs=[blk] * 4, out_specs=[blk] * 3,
        out_shape=[SDS(w.shape, F32)] * 3)(w, g, m, v)


BIG = ("gate", "up", "down", "win", "wpa", "wpb", "wout")
SMALL = ("meta", "norm_gains", "w2", "b2", "gn", "qn", "kn", "bm", "fin")


def _view3(a):
    return a.reshape(a.shape[0], -1, a.shape[-1])


class _StepComm:
    def __init__(self, chip, core):
        self.pvec, self.cvec = chip.reshape(1), core.reshape(1)
        self.sums = {}
        self.red = {k: None for k in BIG}

    def fwd_rider(self, W, l):
        if l + 1 >= DEPTH:
            return None
        return _GatherRider(BIG, [W[k] for k in BIG], [l + 1] * len(BIG))

    def layer_done(self, l, G):
        dwin = jnp.transpose(_win_from_padded(G["winp"]).reshape(D, NSH, D_IN // NSH), (1, 0, 2))
        loc = [dwin if k == "win" else _view3(G[k]) for k in BIG]
        got = _pair_exchange(f"pair_exchange_l{l}", loc)
        self.sums[l] = [_pair_add(f"pair_add_{k}_l{l}", a, p, self.cvec) for k, a, p in zip(BIG, loc, got)]

    def bwd_rider(self, l):
        if l + 1 >= DEPTH:
            return None
        return _ChipExchangeRider([s[1] for s in self.sums[l + 1]])

    def bwd_carried(self, l, arrived):
        self._chip_add(l + 1, arrived)

    def _chip_add(self, l, arrived):
        for k, s, q in zip(BIG, self.sums.pop(l), arrived):
            self.red[k] = _chip_add(f"chip_add_{k}_l{l}", s[0], q, self.pvec, self.cvec, l, self.red[k])

    def finish(self):
        self._chip_add(0, _run_rider("chip_exchange_l0", _ChipExchangeRider([s[1] for s in self.sums[0]])))
        return dict(zip(BIG, _pair_share([self.red[k] for k in BIG])))


def kernel(x, meta_tokens, norm_gains, ffn_w_gate, ffn_w_up, ffn_w_down, w_in, gla_w2, gla_b2, gla_gn, q_norm, k_norm, w_pa, w_pb, b_merge, w_out, final_norm, loss_target, m_meta_tokens, m_norm_gains, m_ffn_w_gate, m_ffn_w_up, m_ffn_w_down, m_w_in, m_gla_w2, m_gla_b2, m_gla_gn, m_q_norm, m_k_norm, m_w_pa, m_w_pb, m_b_merge, m_w_out, m_final_norm, v_meta_tokens, v_norm_gains, v_ffn_w_gate, v_ffn_w_up, v_ffn_w_down, v_w_in, v_gla_w2, v_gla_b2, v_gla_gn, v_q_norm, v_k_norm, v_w_pa, v_w_pb, v_b_merge, v_w_out, v_final_norm):
    big_w = dict(gate=ffn_w_gate, up=ffn_w_up, down=ffn_w_down, win=w_in, wpa=w_pa, wpb=w_pb, wout=w_out)
    big_m = dict(gate=m_ffn_w_gate, up=m_ffn_w_up, down=m_ffn_w_down, win=m_w_in, wpa=m_w_pa, wpb=m_w_pb, wout=m_w_out)
    big_v = dict(gate=v_ffn_w_gate, up=v_ffn_w_up, down=v_ffn_w_down, win=v_w_in, wpa=v_w_pa, wpb=v_w_pb, wout=v_w_out)
    small_w = dict(meta=meta_tokens, norm_gains=norm_gains, w2=gla_w2, b2=gla_b2, gn=gla_gn, qn=q_norm, kn=k_norm,
                   bm=b_merge, fin=final_norm)
    small_m = dict(meta=m_meta_tokens, norm_gains=m_norm_gains, w2=m_gla_w2, b2=m_gla_b2, gn=m_gla_gn, qn=m_q_norm,
                   kn=m_k_norm, bm=m_b_merge, fin=m_final_norm)
    small_v = dict(meta=v_meta_tokens, norm_gains=v_norm_gains, w2=v_gla_w2, b2=v_gla_b2, gn=v_gla_gn, qn=v_q_norm,
                   kn=v_k_norm, bm=v_b_merge, fin=v_final_norm)
    xi, yi, ci = _place()
    chip = (2 * xi + yi).astype(jnp.int32)

    comm = _StepComm(chip, ci.astype(jnp.int32))
    shard_pack = _pack({k: small_w[k] for k in SHARDED_SMALL}, SHARDED_SMALL, SMALL_ROWS)
    placed = [_cast_place(f"cast_{k}", _view3(big_w[k]), comm.pvec) for k in BIG]
    placed.append(lax.dynamic_update_slice(jnp.zeros((NSH, SMALL_ROWS, LANE), F32), shard_pack[None], (chip, 0, 0)))
    gathered = _run_rider("gather_l0", _GatherRider(BIG + ("small",), placed, [0] * len(BIG) + [None]))
    W = _assemble(dict(zip(BIG + ("small",), gathered)))
    W.update(gn=gla_gn, qn=q_norm, kn=k_norm, fin=final_norm)

    loss, dh0, _, gs = _local_step(x[0], loss_target[0], W, comm)
    grad_x = dh0[LANE:][None]
    shared = comm.finish()
    grads, deltas, new_m, new_v = {}, {}, {}, {}
    for k in BIG:
        r = shared[k]
        shp = big_w[k].shape
        grads[k] = r.reshape(shp)
        d, mn, vn = _adamw(f"adamw_{k}", _view3(big_w[k]), r, _view3(big_m[k]), _view3(big_v[k]))
        deltas[k], new_m[k], new_v[k] = d.reshape(shp), mn.reshape(shp), vn.reshape(shp)

    gs["loss"] = loss.reshape(1)
    table = dict(FULL_SMALL, loss=(1,))
    tot = _unpack(_allreduce_small(_pack(gs, table, GRAD_ROWS)), table)
    loss_out = tot["loss"][0]
    sl = dict(meta=(1, 256), norm_gains=(2, 256), w2=(3, 64), b2=(2, 64), bm=(2, 256))
    for k in SMALL:
        gk = tot[k]
        if k in sl:
            ax, width = sl[k]
            gk = lax.dynamic_slice_in_dim(gk, chip * width, width, axis=ax)
        grads[k] = gk
    tbl = {k: small_w[k].shape for k in SMALL}
    rows = _rows_for(tbl, 8)
    packs = [_pack(src, tbl, rows)[None] for src in (small_w, grads, small_m, small_v)]
    d, mn, vn = _adamw("adamw_small", *packs)
    for dst, packed in ((deltas, d), (new_m, mn), (new_v, vn)):
        dst.update(_unpack(packed[0], tbl))

    order = ("meta", "norm_gains", "gate", "up", "down", "win", "w2", "b2", "gn", "qn", "kn", "wpa", "wpb", "bm", "wout", "fin")
    return (loss_out, grad_x, *[grads[k] for k in order], *[deltas[k] for k in order], *[new_m[k] for k in order],
            *[new_v[k] for k in order])
```

```python
import functools

import numpy as np
import jax
import jax.numpy as jnp
from jax import lax
from jax.experimental import pallas as pl
from jax.experimental.pallas import tpu as pltpu

F32, BF16 = jnp.float32, jnp.bfloat16
SDS = jax.ShapeDtypeStruct
HIGHEST = lax.Precision.HIGHEST
MESH = pl.DeviceIdType.MESH

D = 1024
DEPTH = 4
N_META = 16
GRID_W = 64
GLA_H, GLA_DK, GLA_DV, GLA_RANK, GLA_TAU, CHUNK = 4, 64, 128, 16, 16.0, 64
ATT_H, ATT_KV, HD = 8, 2, 64
D_FF = 2816
EPS = 1e-6
ROPE_THETA = 10000.0
ADAM_LR, ADAM_B1, ADAM_B2, ADAM_EPS, ADAM_WD, ADAM_STEP = 0.001, 0.9, 0.999, 1e-08, 0.01, 10

NSH = 4
FF_S = D_FF // NSH
FF_P = 768
LANE = 128
PAD = LANE - N_META
D_IN = 4384
C_GA, C_GB, C_QA, C_KA, C_VA, C_RA, C_QB, C_KB, C_VB, C_LR, HP = 0, 1024, 2048, 2304, 2560, 3072, 3584, 4096, 4224, 4352, 4480
VMEM_BIG = 56 * 2 ** 20
ATT_SUB = 192
FFN_SUB = 192
RIDER_MID_BACK = 5


def _rt(n, cap, mult=LANE):
    best = None
    t = mult
    while t <= min(n, cap):
        if n % t == 0:
            best = t
        t += mult
    assert best is not None, (n, cap, mult)
    return best


def _cp(big=False):
    return pltpu.CompilerParams(vmem_limit_bytes=VMEM_BIG) if big else None


def _row_ids(i, tm):
    return i * tm + lax.broadcasted_iota(jnp.int32, (tm, 1), 0)


def _mm(name, a, b, *, grid, a_spec, b_spec, o_spec, o_shape, o_dtype, dims, acc_shape, nk=1, scale=None,
        res=None, res_spec=None, a_sl=None, b_sl=None, pad_w=None, into=None, shards=None):
    has_res, has_into = res is not None, into is not None

    def body(*refs):
        a_ref, b_ref = refs[0], refs[1]
        p = 2
        res_ref = None
        if has_res:
            res_ref = refs[p]
            p += 1
        if has_into:
            p += 1
        o_ref = refs[p]
        acc_ref = refs[p + 1] if nk > 1 else None
        av = (a_ref[a_sl] if a_sl is not None else a_ref[...]).astype(BF16)
        bv = (b_ref[b_sl] if b_sl is not None else b_ref[...]).astype(BF16)
        if shards == "rows":
            bv = bv.reshape(bv.shape[0] * bv.shape[1], bv.shape[2])
        if shards == "cols":
            w = bv.shape[2]
            if dims == NN:
                prod = jnp.concatenate([lax.dot_general(av, bv[s], NN, preferred_element_type=F32) for s in range(NSH)], axis=1)
            else:
                prod = sum(lax.dot_general(av[:, s * w:(s + 1) * w], bv[s], NT, preferred_element_type=F32) for s in range(NSH))
        else:
            prod = lax.dot_general(av, bv, dims, preferred_element_type=F32)

        def finish(v):
            if scale is not None:
                v = v * scale
            if has_res:
                v = v + res_ref[...]
            v = v.astype(o_dtype)
            if pad_w is None:
                o_ref[...] = v
            else:
                w = v.shape[-1]
                o_ref[:, :w] = v
                o_ref[:, w:] = jnp.zeros((v.shape[0], pad_w - w), o_dtype)

        if nk == 1:
            finish(prod)
        else:
            k = pl.program_id(len(grid) - 1)

            @pl.when(k == 0)
            def _():
                acc_ref[...] = prod

            @pl.when(k > 0)
            def _():
                acc_ref[...] += prod

            @pl.when(k == nk - 1)
            def _():
                finish(acc_ref[...])

    in_specs = [a_spec, b_spec]
    args = [a, b]
    if has_res:
        in_specs.append(res_spec)
        args.append(res)
    aliases = {}
    if has_into:
        aliases = {len(args): 0}
        in_specs.append(pl.BlockSpec(memory_space=pl.ANY))
        args.append(into)
        o_shape = into.shape
    return pl.pallas_call(
        body, name=name, grid=grid, in_specs=in_specs, out_specs=o_spec, out_shape=SDS(tuple(o_shape), o_dtype),
        scratch_shapes=[pltpu.VMEM(acc_shape, F32)] if nk > 1 else [], input_output_aliases=aliases,
        compiler_params=_cp(True))(*args)


NN = (((1,), (0,)), ((), ()))
NT = (((1,), (1,)), ((), ()))
TN = (((0,), (0,)), ((), ()))


def _rms_fwd(name, h, g):
    Lp = h.shape[0]
    tm = _rt(Lp, 384)

    def body(h_ref, g_ref, n_ref, nt_ref):
        x = h_ref[...]
        r = lax.rsqrt(jnp.mean(x * x, axis=-1, keepdims=True) + EPS)
        y = x * r * g_ref[...]
        n_ref[...] = y.astype(BF16)
        nt_ref[...] = y.T.astype(BF16)

    return pl.pallas_call(
        body, name=name, grid=(Lp // tm,),
        in_specs=[pl.BlockSpec((tm, D), lambda i: (i, 0)), pl.BlockSpec((1, D), lambda i: (0, 0))],
        out_specs=[pl.BlockSpec((tm, D), lambda i: (i, 0)), pl.BlockSpec((D, tm), lambda i: (0, i))],
        out_shape=[SDS((Lp, D), BF16), SDS((D, Lp), BF16)])(h, g)


def _rms_bwd(name, dn, h, g, dh):
    Lp = h.shape[0]
    tm = _rt(Lp, 384)

    def body(dn_ref, h_ref, g_ref, dh_ref, o_ref, dg_ref):
        x = h_ref[...]
        r = lax.rsqrt(jnp.mean(x * x, axis=-1, keepdims=True) + EPS)
        xh = x * r
        dn = dn_ref[...]
        u = dn * g_ref[...]
        o_ref[...] = dh_ref[...] + r * (u - xh * jnp.mean(u * xh, axis=-1, keepdims=True))

        @pl.when(pl.program_id(0) == 0)
        def _():
            dg_ref[...] = jnp.zeros_like(dg_ref)

        dg_ref[...] += jnp.sum(dn * xh, axis=0, keepdims=True)

    row = pl.BlockSpec((tm, D), lambda i: (i, 0))
    vec = pl.BlockSpec((1, D), lambda i: (0, 0))
    return pl.pallas_call(
        body, name=name, grid=(Lp // tm,), in_specs=[row, row, vec, row], out_specs=[row, vec],
        out_shape=[SDS((Lp, D), F32), SDS((1, D), F32)])(dn, h, g, dh)


def _loss_head(name, h, g, tgt):
    Lp = h.shape[0]
    tm = LANE

    def body(h_ref, g_ref, t_ref, dh_ref, dg_ref, loss_ref):
        i = pl.program_id(0)
        x = h_ref[...]
        r = lax.rsqrt(jnp.mean(x * x, axis=-1, keepdims=True) + EPS)
        xh = x * r
        gg = g_ref[...]
        err = jnp.where(i >= 1, xh * gg - t_ref[...], 0.0)
        dy = err * (1.0 / D)
        u = dy * gg
        dh_ref[...] = r * (u - xh * jnp.mean(u * xh, axis=-1, keepdims=True))

        @pl.when(i == 0)
        def _():
            dg_ref[...] = jnp.zeros_like(dg_ref)
            loss_ref[...] = jnp.zeros_like(loss_ref)

        dg_ref[...] += jnp.sum(dy * xh, axis=0, keepdims=True)
        loss_ref[...] += (0.5 / D) * jnp.sum(err * err)

    row = pl.BlockSpec((tm, D), lambda i: (i, 0))
    vec = pl.BlockSpec((1, D), lambda i: (0, 0))
    return pl.pallas_call(
        body, name=name, grid=(Lp // tm,),
        in_specs=[row, vec, pl.BlockSpec((tm, D), lambda i: (jnp.maximum(i - 1, 0), 0))],
        out_specs=[row, vec, pl.BlockSpec((8, LANE), lambda i: (0, 0))],
        out_shape=[SDS((Lp, D), F32), SDS((1, D), F32), SDS((8, LANE), F32)])(h, g, tgt)


def _swiglu_fwd(name, a, b):
    Lp = a.shape[1]
    tm = _rt(Lp, 384)

    def body(a_ref, b_ref, s_ref, st_ref):
        av = a_ref[...]
        s = av * jax.nn.sigmoid(av) * b_ref[...]
        s_ref[...] = s.astype(BF16)
        st_ref[...] = s.T.astype(BF16)

    slab = pl.BlockSpec((None, tm, FF_P), lambda j, i: (j, i, 0))
    return pl.pallas_call(
        body, name=name, grid=(NSH, Lp // tm), in_specs=[slab, slab],
        out_specs=[slab, pl.BlockSpec((None, FF_P, tm), lambda j, i: (j, 0, i))],
        out_shape=[SDS((NSH, Lp, FF_P), BF16), SDS((NSH, FF_P, Lp), BF16)])(a, b)


def _swiglu_bwd(name, a, b, ds):
    Lp = a.shape[1]
    tm = _rt(Lp, 384)

    def body(a_ref, b_ref, ds_ref, da_ref, db_ref):
        av = a_ref[...]
        sg = jax.nn.sigmoid(av)
        dsv = ds_ref[...]
        da_ref[...] = (dsv * b_ref[...] * (sg * (1.0 + av * (1.0 - sg)))).astype(BF16)
        db_ref[...] = (dsv * (av * sg)).astype(BF16)

    slab = pl.BlockSpec((None, tm, FF_P), lambda j, i: (j, i, 0))
    return pl.pallas_call(
        body, name=name, grid=(NSH, Lp // tm), in_specs=[slab, slab, slab], out_specs=[slab, slab],
        out_shape=[SDS((NSH, Lp, FF_P), BF16), SDS((NSH, Lp, FF_P), BF16)])(a, b, ds)


def _merge_fwd(name, H, pa, pb, bm):
    Lp = H.shape[0]
    tm = _rt(Lp, 384)

    def body(g_ref, pa_ref, pb_ref, bm_ref, y_ref, yt_ref):
        gv = g_ref[...]
        y = (jax.nn.sigmoid(gv[:, :D] + bm_ref[0:1, :]) * pa_ref[...]
             + jax.nn.sigmoid(gv[:, D:] + bm_ref[1:2, :]) * pb_ref[...])
        y_ref[...] = y.astype(BF16)
        yt_ref[...] = y.T.astype(BF16)

    row = pl.BlockSpec((tm, D), lambda i: (i, 0))
    return pl.pallas_call(
        body, name=name, grid=(Lp // tm,),
        in_specs=[pl.BlockSpec((tm, 2 * D), lambda i: (i, 0)), row, row, pl.BlockSpec((2, D), lambda i: (0, 0))],
        out_specs=[row, pl.BlockSpec((D, tm), lambda i: (0, i))],
        out_shape=[SDS((Lp, D), BF16), SDS((D, Lp), BF16)])(H, pa, pb, bm)


def _merge_bwd(name, H, pa, pb, bm, dy):
    Lp = H.shape[0]
    tm = _rt(Lp, 384)

    def body(g_ref, pa_ref, pb_ref, bm_ref, dy_ref, dpa_ref, dpb_ref, dh_ref, dbm_ref):
        gv = g_ref[...]
        dyv = dy_ref[...]
        sa = jax.nn.sigmoid(gv[:, :D] + bm_ref[0:1, :])
        sb = jax.nn.sigmoid(gv[:, D:] + bm_ref[1:2, :])
        dpa_ref[...] = (dyv * sa).astype(BF16)
        dpb_ref[...] = (dyv * sb).astype(BF16)
        dga = dyv * pa_ref[...] * (sa * (1.0 - sa))
        dgb = dyv * pb_ref[...] * (sb * (1.0 - sb))
        dh_ref[:, :D] = dga.astype(BF16)
        dh_ref[:, D:] = dgb.astype(BF16)

        @pl.when(pl.program_id(0) == 0)
        def _():
            dbm_ref[...] = jnp.zeros_like(dbm_ref)

        dbm_ref[0:1, :] += jnp.sum(dga, axis=0, keepdims=True)
        dbm_ref[1:2, :] += jnp.sum(dgb, axis=0, keepdims=True)

    row = pl.BlockSpec((tm, D), lambda i: (i, 0))
    two = pl.BlockSpec((2, D), lambda i: (0, 0))
    gate = pl.BlockSpec((tm, 2 * D), lambda i: (i, 0))
    return pl.pallas_call(
        body, name=name, grid=(Lp // tm,), in_specs=[gate, row, row, two, row], out_specs=[row, row, gate, two],
        out_shape=[SDS((Lp, D), BF16), SDS((Lp, D), BF16), SDS((Lp, HP), BF16), SDS((2, D), F32)])(H, pa, pb, bm, dy)


def _gla_prep(name, H, w2p, b2p):
    Lp = H.shape[0]
    tm = _rt(Lp, 384)

    def body(qk_ref, v_ref, lr_ref, w_ref, b_ref, q_o, k_o, v_o, gf_o, gb_o):
        valid = _row_ids(pl.program_id(0), tm) >= PAD
        qk = qk_ref[...]
        vv = v_ref[...]
        pre = jnp.dot(lr_ref[...].astype(BF16), w_ref[...], preferred_element_type=F32) + b_ref[...]
        g = jnp.where(valid, jax.nn.log_sigmoid(pre) * (1.0 / GLA_TAU), 0.0)
        for hh in range(GLA_H):
            q_o[hh] = qk[:, 64 * hh:64 * hh + 64] * (GLA_DK ** -0.5)
            k_o[hh] = qk[:, 256 + 64 * hh:256 + 64 * hh + 64]
            v_o[hh] = vv[:, 128 * hh:128 * hh + 128].astype(BF16)
            gf_o[hh] = g[:, 64 * hh:64 * hh + 64]
            gb_o[hh] = g[:, 256 + 64 * hh:256 + 64 * hh + 64]

    h64 = pl.BlockSpec((GLA_H, tm, 64), lambda i: (0, i, 0))
    h128 = pl.BlockSpec((GLA_H, tm, 128), lambda i: (0, i, 0))
    return pl.pallas_call(
        body, name=name, grid=(Lp // tm,),
        in_specs=[pl.BlockSpec((tm, 512), lambda i: (i, C_QA // 512)), pl.BlockSpec((tm, 512), lambda i: (i, C_VA // 512)),
                  pl.BlockSpec((tm, LANE), lambda i: (i, C_LR // LANE)), pl.BlockSpec((LANE, 512), lambda i: (0, 0)),
                  pl.BlockSpec((1, 512), lambda i: (0, 0))],
        out_specs=[h64, h64, h128, h64, h64],
        out_shape=[SDS((GLA_H, Lp, 64), F32), SDS((GLA_H, Lp, 64), F32), SDS((GLA_H, Lp, 128), BF16),
                   SDS((GLA_H, Lp, 64), F32), SDS((GLA_H, Lp, 64), F32)])(H, H, H, w2p, b2p)


def _bdot(a, b, ca, cb, precision=None):
    return lax.dot_general(a, b, ((ca, cb), ((0,), (0,))), precision=precision, preferred_element_type=F32)


def _gla_chunk_terms(q_ref, k_ref, g_ref, v_ref, G, rev):
    B = GLA_H * G
    qv = q_ref[...].reshape(B, CHUNK, GLA_DK)
    kv = k_ref[...].reshape(B, CHUNK, GLA_DK)
    gv = g_ref[...].reshape(B, CHUNK, GLA_DK)
    vv = v_ref[...].reshape(B, CHUNK, GLA_DV)
    ii = lax.broadcasted_iota(jnp.int32, (CHUNK, CHUNK), 0)
    jj = lax.broadcasted_iota(jnp.int32, (CHUNK, CHUNK), 1)
    tri = (jj >= ii) if rev else (jj <= ii)
    tb = jnp.broadcast_to(tri.astype(F32)[None], (B, CHUNK, CHUNK))
    bc = _bdot(tb, gv, (2,), (1,), HIGHEST)
    bt = bc[:, 0:1, :] if rev else bc[:, CHUNK - 1:CHUNK, :]
    eq, eki, eke = jnp.exp(bc), jnp.exp(-bc), jnp.exp(bt - bc)
    qd, ki, ke = qv * eq, kv * eki, kv * eke
    att = jnp.where(tri[None], _bdot(qd.astype(BF16), ki.astype(BF16), (2,), (2,)), 0.0)
    dm = jnp.exp(_bdot(gv, jnp.ones((B, CHUNK, GLA_DV), F32), (1,), (1,), HIGHEST))
    return dict(B=B, vv=vv, tri=tri, tb=tb, bt=bt, eq=eq, eki=eki, eke=eke, qd=qd, ki=ki, ke=ke, att=att, dm=dm)


def _gla_fwd(name, q, k, g, v, rev, G):
    Lp = q.shape[1]
    tg = G * CHUNK
    ng = Lp // tg

    def body(q_ref, k_ref, g_ref, v_ref, o_ref, ss_ref, s_scr):
        @pl.when(pl.program_id(0) == 0)
        def _():
            s_scr[...] = jnp.zeros_like(s_scr)

        t = _gla_chunk_terms(q_ref, k_ref, g_ref, v_ref, G, rev)
        B, vv = t["B"], t["vv"]
        qd = t["qd"].astype(BF16)
        oi = _bdot(t["att"].astype(BF16), vv, (2,), (1,))
        kvc = _bdot(t["ke"].astype(BF16), vv, (1,), (1,)).reshape(GLA_H, G, GLA_DK, GLA_DV)
        dm = t["dm"].reshape(GLA_H, G, GLA_DK, GLA_DV)
        s = s_scr[...]
        sp = [None] * G
        for c in (range(G - 1, -1, -1) if rev else range(G)):
            sp[c] = s
            ss_ref[c] = s
            s = dm[:, c] * s + kvc[:, c]
        s_scr[...] = s
        spb = jnp.stack(sp, axis=1).reshape(B, GLA_DK, GLA_DV).astype(BF16)
        o_ref[...] = (oi + _bdot(qd, spb, (2,), (1,))).reshape(GLA_H, tg, GLA_DV)

    blk = (lambda i: (0, ng - 1 - i, 0)) if rev else (lambda i: (0, i, 0))
    sblk = (lambda i: (ng - 1 - i, 0, 0, 0)) if rev else (lambda i: (i, 0, 0, 0))
    h64 = pl.BlockSpec((GLA_H, tg, 64), blk)
    h128 = pl.BlockSpec((GLA_H, tg, 128), blk)
    return pl.pallas_call(
        body, name=name, grid=(ng,), in_specs=[h64, h64, h64, h128],
        out_specs=[h128, pl.BlockSpec((G, GLA_H, GLA_DK, GLA_DV), sblk)],
        out_shape=[SDS((GLA_H, Lp, GLA_DV), F32), SDS((Lp // CHUNK, GLA_H, GLA_DK, GLA_DV), F32)],
        scratch_shapes=[pltpu.VMEM((GLA_H, GLA_DK, GLA_DV), F32)], compiler_params=_cp(True))(q, k, g, v)


def _gla_bwd(name, q, k, g, v, ss, do, rev, G):
    Lp = q.shape[1]
    tg = G * CHUNK
    ng = Lp // tg

    def body(q_ref, k_ref, g_ref, v_ref, ss_ref, do_ref, dq_ref, dk_ref, dg_ref, dv_ref, ds_scr):
        @pl.when(pl.program_id(0) == 0)
        def _():
            ds_scr[...] = jnp.zeros_like(ds_scr)

        t = _gla_chunk_terms(q_ref, k_ref, g_ref, v_ref, G, rev)
        B, vv, tri = t["B"], t["vv"], t["tri"]
        qd, ki, ke = t["qd"], t["ki"], t["ke"]
        qdb, kib, keb = qd.astype(BF16), ki.astype(BF16), ke.astype(BF16)
        sp = jnp.stack([ss_ref[c] for c in range(G)], axis=1).reshape(B, GLA_DK, GLA_DV)
        dob = do_ref[...].reshape(B, CHUNK, GLA_DV).astype(BF16)
        da = jnp.where(tri[None], _bdot(dob, vv, (2,), (2,)), 0.0).astype(BF16)
        dqd = _bdot(da, kib, (2,), (1,)) + _bdot(dob, sp.astype(BF16), (2,), (2,))
        dki = _bdot(da, qdb, (1,), (1,))
        dv = _bdot(t["att"].astype(BF16), dob, (1,), (1,))
        cc = _bdot(qdb, dob, (1,), (1,)).reshape(GLA_H, G, GLA_DK, GLA_DV)
        dm = t["dm"].reshape(GLA_H, G, GLA_DK, GLA_DV)
        dsc = ds_scr[...]
        dsn = [None] * G
        for c in (range(G) if rev else range(G - 1, -1, -1)):
            dsn[c] = dsc
            dsc = dm[:, c] * dsc + cc[:, c]
        ds_scr[...] = dsc
        dsn = jnp.stack(dsn, axis=1).reshape(B, GLA_DK, GLA_DV)
        dsnb = dsn.astype(BF16)
        dv = dv + _bdot(keb, dsnb, (2,), (1,))
        dke = _bdot(vv, dsnb, (2,), (2,))
        ddrow = _bdot(jnp.ones((B, CHUNK, GLA_DV), F32), dsn * sp, (2,), (2,), HIGHEST)
        dbt = ddrow * jnp.exp(t["bt"]) + jnp.sum(dke * ke, axis=1, keepdims=True)
        db = dqd * qd - dki * ki - dke * ke
        dq_ref[...] = (dqd * t["eq"]).reshape(GLA_H, tg, GLA_DK)
        dk_ref[...] = (dki * t["eki"] + dke * t["eke"]).reshape(GLA_H, tg, GLA_DK)
        dg_ref[...] = (_bdot(t["tb"], db, (1,), (1,), HIGHEST) + dbt).reshape(GLA_H, tg, GLA_DK)
        dv_ref[...] = dv.reshape(GLA_H, tg, GLA_DV)

    blk = (lambda i: (0, i, 0)) if rev else (lambda i: (0, ng - 1 - i, 0))
    sblk = (lambda i: (i, 0, 0, 0)) if rev else (lambda i: (ng - 1 - i, 0, 0, 0))
    h64 = pl.BlockSpec((GLA_H, tg, 64), blk)
    h128 = pl.BlockSpec((GLA_H, tg, 128), blk)
    return pl.pallas_call(
        body, name=name, grid=(ng,),
        in_specs=[h64, h64, h64, h128, pl.BlockSpec((G, GLA_H, GLA_DK, GLA_DV), sblk), h128],
        out_specs=[h64, h64, h64, h128],
        out_shape=[SDS((GLA_H, Lp, 64), F32), SDS((GLA_H, Lp, 64), F32), SDS((GLA_H, Lp, 64), F32),
                   SDS((GLA_H, Lp, GLA_DV), F32)],
        scratch_shapes=[pltpu.VMEM((GLA_H, GLA_DK, GLA_DV), F32)], compiler_params=_cp(True))(q, k, g, v, ss, do)


def _gla_post(name, of, ob, H, gn):
    Lp = H.shape[0]
    tm = _rt(Lp, 384)

    def body(of_ref, ob_ref, r_ref, gn_ref, a_ref, at_ref):
        parts = []
        for hh in range(GLA_H):
            o = of_ref[hh] + ob_ref[hh]
            parts.append(o * lax.rsqrt(jnp.mean(o * o, axis=-1, keepdims=True) + EPS))
        rv = r_ref[...]
        a = (jnp.concatenate(parts, axis=1) * gn_ref[...]) * (rv * jax.nn.sigmoid(rv))
        a_ref[...] = a.astype(BF16)
        at_ref[...] = a.T.astype(BF16)

    h128 = pl.BlockSpec((GLA_H, tm, 128), lambda i: (0, i, 0))
    return pl.pallas_call(
        body, name=name, grid=(Lp // tm,),
        in_specs=[h128, h128, pl.BlockSpec((tm, 512), lambda i: (i, C_RA // 512)), pl.BlockSpec((1, 512), lambda i: (0, 0))],
        out_specs=[pl.BlockSpec((tm, 512), lambda i: (i, 0)), pl.BlockSpec((512, tm), lambda i: (0, i))],
        out_shape=[SDS((Lp, 512), BF16), SDS((512, Lp), BF16)])(of, ob, H, gn)


def _gla_post_bwd(name, of, ob, H, gn, da, dH):
    Lp = H.shape[0]
    tm = _rt(Lp, 384)

    def body(of_ref, ob_ref, r_ref, gn_ref, da_ref, dh_in, do_ref, dh_ref, dgn_ref):
        rv = r_ref[...]
        sg = jax.nn.sigmoid(rv)
        dav = da_ref[...]
        gnv = gn_ref[...]
        ons, rs = [], []
        for hh in range(GLA_H):
            o = of_ref[hh] + ob_ref[hh]
            r = lax.rsqrt(jnp.mean(o * o, axis=-1, keepdims=True) + EPS)
            rs.append(r)
            ons.append(o * r)
        on = jnp.concatenate(ons, axis=1)
        dw = dav * (rv * sg)
        dh_ref[...] = (dav * (on * gnv) * (sg * (1.0 + rv * (1.0 - sg)))).astype(BF16)

        @pl.when(pl.program_id(0) == 0)
        def _():
            dgn_ref[...] = jnp.zeros_like(dgn_ref)

        dgn_ref[...] += jnp.sum(dw * on, axis=0, keepdims=True)
        don = dw * gnv
        for hh in range(GLA_H):
            dd = don[:, 128 * hh:128 * hh + 128]
            do_ref[hh] = rs[hh] * (dd - ons[hh] * jnp.mean(dd * ons[hh], axis=-1, keepdims=True))

    h128 = pl.BlockSpec((GLA_H, tm, 128), lambda i: (0, i, 0))
    rblk = pl.BlockSpec((tm, 512), lambda i: (i, C_RA // 512))
    vec = pl.BlockSpec((1, 512), lambda i: (0, 0))
    return pl.pallas_call(
        body, name=name, grid=(Lp // tm,),
        in_specs=[h128, h128, rblk, vec, pl.BlockSpec((tm, 512), lambda i: (i, 0)), pl.BlockSpec(memory_space=pl.ANY)],
        out_specs=[h128, rblk, vec],
        out_shape=[SDS((GLA_H, Lp, 128), F32), SDS(dH.shape, BF16), SDS((1, 512), F32)],
        input_output_aliases={5: 1})(of, ob, H, gn, da, dH)


def _gla_qkv_bwd(name, dqf, dqb, dkf, dkb, dvf, dvb, dH):
    Lp = dqf.shape[1]
    tm = _rt(Lp, 384)

    def body(dqf_ref, dqb_ref, dkf_ref, dkb_ref, dvf_ref, dvb_ref, dh_in, dh_ref):
        valid = _row_ids(pl.program_id(0), tm) >= PAD
        for hh in range(GLA_H):
            dq = (dqf_ref[hh] + dqb_ref[hh]) * (GLA_DK ** -0.5)
            dh_ref[:, 64 * hh:64 * hh + 64] = jnp.where(valid, dq, 0.0).astype(BF16)
            dh_ref[:, 256 + 64 * hh:256 + 64 * hh + 64] = jnp.where(valid, dkf_ref[hh] + dkb_ref[hh], 0.0).astype(BF16)
            dh_ref[:, 512 + 128 * hh:512 + 128 * hh + 128] = jnp.where(valid, dvf_ref[hh] + dvb_ref[hh], 0.0).astype(BF16)

    h64 = pl.BlockSpec((GLA_H, tm, 64), lambda i: (0, i, 0))
    h128 = pl.BlockSpec((GLA_H, tm, 128), lambda i: (0, i, 0))
    return pl.pallas_call(
        body, name=name, grid=(Lp // tm,),
        in_specs=[h64, h64, h64, h64, h128, h128, pl.BlockSpec(memory_space=pl.ANY)],
        out_specs=pl.BlockSpec((tm, 1024), lambda i: (i, C_QA // 1024)), out_shape=SDS(dH.shape, BF16),
        input_output_aliases={6: 0})(dqf, dqb, dkf, dkb, dvf, dvb, dH)


def _gla_gate_bwd(name, H, w2p, b2p, dgf, dgb, dH):
    Lp = H.shape[0]
    tm = _rt(Lp, 384)

    def body(lr_ref, w_ref, b_ref, dgf_ref, dgb_ref, dh_in, dh_ref, dw_ref, db_ref, dg_scr):
        valid = _row_ids(pl.program_id(0), tm) >= PAD
        for hh in range(GLA_H):
            dg_scr[:, 64 * hh:64 * hh + 64] = dgf_ref[hh]
            dg_scr[:, 256 + 64 * hh:256 + 64 * hh + 64] = dgb_ref[hh]
        lrb = lr_ref[...].astype(BF16)
        wv = w_ref[...]
        pre = jnp.dot(lrb, wv, preferred_element_type=F32) + b_ref[...]
        dpre = jnp.where(valid, dg_scr[...] * (1.0 / GLA_TAU) * jax.nn.sigmoid(-pre), 0.0)
        dpb = dpre.astype(BF16)
        dh_ref[...] = lax.dot_general(dpb, wv, NT, preferred_element_type=F32).astype(BF16)

        @pl.when(pl.program_id(0) == 0)
        def _():
            dw_ref[...] = jnp.zeros_like(dw_ref)
            db_ref[...] = jnp.zeros_like(db_ref)

        dw_ref[...] += lax.dot_general(lrb, dpb, TN, preferred_element_type=F32)
        db_ref[...] += jnp.sum(dpre, axis=0, keepdims=True)

    h64 = pl.BlockSpec((GLA_H, tm, 64), lambda i: (0, i, 0))
    lrblk = pl.BlockSpec((tm, LANE), lambda i: (i, C_LR // LANE))
    wblk = pl.BlockSpec((LANE, 512), lambda i: (0, 0))
    vec = pl.BlockSpec((1, 512), lambda i: (0, 0))
    return pl.pallas_call(
        body, name=name, grid=(Lp // tm,),
        in_specs=[lrblk, wblk, vec, h64, h64, pl.BlockSpec(memory_space=pl.ANY)],
        out_specs=[lrblk, wblk, vec],
        out_shape=[SDS(dH.shape, BF16), SDS((LANE, 512), F32), SDS((1, 512), F32)],
        scratch_shapes=[pltpu.VMEM((tm, 512), F32)], input_output_aliases={5: 0})(H, w2p, b2p, dgf, dgb, dH)


def _swap16(x):
    n = x.shape[1]
    lane = lax.broadcasted_iota(jnp.int32, x.shape, 1)
    return jnp.where(lane % 32 < 16, pltpu.roll(x, n - 16, 1), pltpu.roll(x, 16, 1))


def _headnorm_rope(x, gain, cos, sin, bd):
    r = lax.rsqrt(jnp.dot(x * x, bd, precision=HIGHEST, preferred_element_type=F32) + EPS)
    xh = x * r
    xn = xh * gain
    return xn * cos + _swap16(xn) * sin, xh, r


def _headnorm_rope_bwd(dxr, xh, r, gain, cos, sin, bd):
    dxn = cos * dxr + _swap16(sin * dxr)
    u = dxn * gain
    dx = r * (u - xh * jnp.dot(u * xh, bd, precision=HIGHEST, preferred_element_type=F32))
    return dx, jnp.sum(dxn * xh, axis=0, keepdims=True)


def _att_prep(name, H, gq, gk, cos, sin, bd):
    Lp = H.shape[0]
    tm = _rt(Lp, 384)

    def body(q_ref, kv_ref, gq_ref, gk_ref, c_ref, s_ref, bd_ref, qp_ref, k_ref, v_ref):
        c1, s1 = c_ref[...], s_ref[...]
        c4, s4 = jnp.concatenate([c1] * 4, axis=1), jnp.concatenate([s1] * 4, axis=1)
        xr, _, _ = _headnorm_rope(q_ref[...], gq_ref[...], c4, s4, bd_ref[...])
        xr = xr * (HD ** -0.5)
        lane = lax.broadcasted_iota(jnp.int32, (tm, LANE), 1)
        for hh in range(ATT_H):
            grp = xr[:, LANE * (hh // 2):LANE * (hh // 2) + LANE]
            e, gi = hh % 2, hh // 4
            if e != gi:
                grp = pltpu.roll(grp, 64, 1)
            keep = (lane < 64) if gi == 0 else (lane >= 64)
            qp_ref[hh] = jnp.where(keep, grp, 0.0).astype(BF16)
        kv = kv_ref[...]
        kr, _, _ = _headnorm_rope(kv[:, :LANE], gk_ref[...], c1, s1, bd_ref[0:LANE, 0:LANE])
        k_ref[...] = kr.astype(BF16)
        v_ref[...] = kv[:, LANE:].astype(BF16)

    row128 = pl.BlockSpec((tm, LANE), lambda i: (i, 0))
    return pl.pallas_call(
        body, name=name, grid=(Lp // tm,),
        in_specs=[pl.BlockSpec((tm, 512), lambda i: (i, C_QB // 512)), pl.BlockSpec((tm, 256), lambda i: (i, C_KB // 256)),
                  pl.BlockSpec((1, 512), lambda i: (0, 0)), pl.BlockSpec((1, LANE), lambda i: (0, 0)), row128, row128,
                  pl.BlockSpec((512, 512), lambda i: (0, 0))],
        out_specs=[pl.BlockSpec((ATT_H, tm, LANE), lambda i: (0, i, 0)), row128, row128],
        out_shape=[SDS((ATT_H, Lp, LANE), BF16), SDS((Lp, LANE), BF16), SDS((Lp, LANE), BF16)])(H, H, gq, gk, cos, sin, bd)


def _att_q_bwd(name, H, gq, cos, sin, bd, dqp, dH):
    Lp = H.shape[0]
    tm = _rt(Lp, 384)

    def body(q_ref, gq_ref, c_ref, s_ref, bd_ref, dqp_ref, dh_in, dh_ref, dg_ref):
        c1, s1 = c_ref[...], s_ref[...]
        c4, s4 = jnp.concatenate([c1] * 4, axis=1), jnp.concatenate([s1] * 4, axis=1)
        gqv = gq_ref[...]
        _, xh, r = _headnorm_rope(q_ref[...], gqv, c4, s4, bd_ref[...])
        lane = lax.broadcasted_iota(jnp.int32, (tm, LANE), 1)
        groups = []
        for j in range(ATT_H // 2):
            pieces = []
            for e in range(2):
                hh = 2 * j + e
                piece = dqp_ref[hh]
                if e != hh // 4:
                    piece = pltpu.roll(piece, 64, 1)
                pieces.append(piece)
            groups.append(jnp.where(lane < 64, pieces[0], pieces[1]))
        dxr = jnp.concatenate(groups, axis=1) * (HD ** -0.5)
        dx, dg = _headnorm_rope_bwd(dxr, xh, r, gqv, c4, s4, bd_ref[...])
        dh_ref[...] = dx.astype(BF16)

        @pl.when(pl.program_id(0) == 0)
        def _():
            dg_ref[...] = jnp.zeros_like(dg_ref)

        dg_ref[...] += dg

    row128 = pl.BlockSpec((tm, LANE), lambda i: (i, 0))
    qblk = pl.BlockSpec((tm, 512), lambda i: (i, C_QB // 512))
    vec = pl.BlockSpec((1, 512), lambda i: (0, 0))
    return pl.pallas_call(
        body, name=name, grid=(Lp // tm,),
        in_specs=[qblk, vec, row128, row128, pl.BlockSpec((512, 512), lambda i: (0, 0)),
                  pl.BlockSpec((ATT_H, tm, LANE), lambda i: (0, i, 0)), pl.BlockSpec(memory_space=pl.ANY)],
        out_specs=[qblk, vec], out_shape=[SDS(dH.shape, BF16), SDS((1, 512), F32)],
        input_output_aliases={6: 0})(H, gq, cos, sin, bd, dqp, dH)


def _att_kv_bwd(name, H, gk, cos, sin, bd, dkr, dvb, dH):
    Lp = H.shape[0]
    tm = _rt(Lp, 384)

    def body(kv_ref, gk_ref, c_ref, s_ref, bd_ref, dk_ref, dv_ref, dh_in, dh_ref, dg_ref):
        c1, s1 = c_ref[...], s_ref[...]
        gkv = gk_ref[...]
        bdv = bd_ref[0:LANE, 0:LANE]
        _, xh, r = _headnorm_rope(kv_ref[:, :LANE], gkv, c1, s1, bdv)
        dx, dg = _headnorm_rope_bwd(dk_ref[...], xh, r, gkv, c1, s1, bdv)
        dh_ref[:, :LANE] = dx.astype(BF16)
        dh_ref[:, LANE:] = dv_ref[...].astype(BF16)

        @pl.when(pl.program_id(0) == 0)
        def _():
            dg_ref[...] = jnp.zeros_like(dg_ref)

        dg_ref[...] += dg

    row128 = pl.BlockSpec((tm, LANE), lambda i: (i, 0))
    kvblk = pl.BlockSpec((tm, 256), lambda i: (i, C_KB // 256))
    vec = pl.BlockSpec((1, LANE), lambda i: (0, 0))
    return pl.pallas_call(
        body, name=name, grid=(Lp // tm,),
        in_specs=[kvblk, vec, row128, row128, pl.BlockSpec((512, 512), lambda i: (0, 0)), row128, row128,
                  pl.BlockSpec(memory_space=pl.ANY)],
        out_specs=[kvblk, vec], out_shape=[SDS(dH.shape, BF16), SDS((1, LANE), F32)],
        input_output_aliases={7: 0})(H, gk, cos, sin, bd, dkr, dvb, dH)


def _ridden_call(core, rider, *, name, grid, in_specs, out_specs, out_shape, scratch_shapes, args):
    n_in, n_out, n_scr = len(in_specs), len(out_specs), len(scratch_shapes)
    r_in = len(rider.ins) if rider else 0
    r_out = len(rider.out_shape) if rider else 0
    total = int(np.prod(grid))
    mid_step = max(total - 1 - RIDER_MID_BACK, 0)

    def body(*refs):
        ins, r_ins = refs[:n_in], refs[n_in:n_in + r_in]
        o0 = n_in + r_in
        outs, r_outs = refs[o0:o0 + n_out], refs[o0 + n_out:o0 + n_out + r_out]
        s0 = o0 + n_out + r_out
        scr, r_sems = refs[s0:s0 + n_scr], refs[s0 + n_scr:]
        if rider is None:
            core(ins, outs, scr)
            return
        step = pl.program_id(0)
        for ax in range(1, len(grid)):
            step = step * grid[ax] + pl.program_id(ax)
        start, mid, end = rider.hooks(r_ins, r_outs, r_sems)
        pl.when(step == 0)(start)
        core(ins, outs, scr)
        pl.when(step == mid_step)(mid)
        pl.when(step == total - 1)(end)

    any_spec = pl.BlockSpec(memory_space=pl.ANY)
    res = pl.pallas_call(
        body, name=name, grid=grid, in_specs=list(in_specs) + [any_spec] * r_in, out_specs=list(out_specs) + [any_spec] * r_out,
        out_shape=list(out_shape) + (list(rider.out_shape) if rider else []),
        scratch_shapes=list(scratch_shapes) + (list(rider.sems) if rider else []),
        input_output_aliases={n_in + a: n_out + b for a, b in rider.aliases.items()} if rider else {},
        compiler_params=_cp(True))(*args, *(rider.ins if rider else []))
    return res[:n_out], res[n_out:]


def _att_fwd(name, qp, kr, vb, bias, rider=None):
    Lp = kr.shape[0]
    tq = _rt(Lp, 384)
    sub = ATT_SUB if tq % ATT_SUB == 0 else tq

    def core(ins, outs, scr):
        q_ref, k_ref, v_ref, bias_ref = ins
        b_ref, bt_ref, lse_ref = outs
        j, i = pl.program_id(0), pl.program_id(1)
        valid = _row_ids(i, tq) >= PAD
        kk, vv, bb = k_ref[...], v_ref[...], bias_ref[...]
        outs = []
        for e in range(2):
            parts = []
            for r in range(tq // sub):
                rs = slice(r * sub, (r + 1) * sub)
                s = lax.dot_general(q_ref[e, rs, :], kk, NT, preferred_element_type=F32) + bb
                m = jnp.max(s, axis=-1, keepdims=True)
                p = jnp.exp(s - m)
                l = jnp.sum(p, axis=-1, keepdims=True)
                parts.append(jnp.dot(p.astype(BF16), vv, preferred_element_type=F32) * (1.0 / l))
                lse_ref[e, rs, :] = m + jnp.log(l)
            outs.append(jnp.where(valid, jnp.concatenate(parts, axis=0), 0.0))
        lane = lax.broadcasted_iota(jnp.int32, (tq, LANE), 1)
        low = j < 2
        o0 = jnp.where(low, outs[0], pltpu.roll(outs[0], 64, 1))
        o1 = jnp.where(low, pltpu.roll(outs[1], 64, 1), outs[1])
        blk = jnp.where(lane < 64, o0, o1)
        b_ref[...] = blk.astype(BF16)
        bt_ref[...] = blk.T.astype(BF16)

    full = pl.BlockSpec((Lp, LANE), lambda j, i: (0, 0))
    return _ridden_call(
        core, rider, name=name, grid=(ATT_H // 2, Lp // tq),
        in_specs=[pl.BlockSpec((2, tq, LANE), lambda j, i: (j, i, 0)), full, full, pl.BlockSpec((1, Lp), lambda j, i: (0, 0))],
        out_specs=[pl.BlockSpec((tq, LANE), lambda j, i: (i, j)), pl.BlockSpec((LANE, tq), lambda j, i: (j, i)),
                   pl.BlockSpec((2, tq, 1), lambda j, i: (j, i, 0))],
        out_shape=[SDS((Lp, 512), BF16), SDS((512, Lp), BF16), SDS((ATT_H, Lp, 1), F32)],
        scratch_shapes=[], args=(qp, kr, vb, bias))


def _att_bwd(name, qp, kr, vb, bias, lse, db, b, rider=None):
    Lp = kr.shape[0]
    tq = _rt(Lp, 384)
    sub = ATT_SUB if tq % ATT_SUB == 0 else tq
    nq = Lp // tq

    def core(ins, outs, scr):
        q_ref, k_ref, v_ref, bias_ref, lse_ref, db_ref, b_ref = ins
        dq_ref, dk_ref, dv_ref = outs
        dkt_scr, dvt_scr = scr
        j, i = pl.program_id(0), pl.program_id(1)

        @pl.when((j == 0) & (i == 0))
        def _():
            dkt_scr[...] = jnp.zeros_like(dkt_scr)
            dvt_scr[...] = jnp.zeros_like(dvt_scr)

        kk, vv, bb = k_ref[...], v_ref[...], bias_ref[...]
        dbv = db_ref[...]
        rolled = pltpu.roll(dbv, 64, 1)
        lane = lax.broadcasted_iota(jnp.int32, (tq, LANE), 1)
        low = j < 2
        first = jnp.where(low, 0, 64)
        keep = (lane >= first) & (lane < first + 64)
        prod = dbv * b_ref[...].astype(F32)
        deltas = [jnp.sum(jnp.where(lane < 64, prod, 0.0), axis=-1, keepdims=True),
                  jnp.sum(jnp.where(lane >= 64, prod, 0.0), axis=-1, keepdims=True)]
        dkt, dvt = 0.0, 0.0
        for e in range(2):
            src = jnp.where(low, dbv, rolled) if e == 0 else jnp.where(low, rolled, dbv)
            dop = jnp.where(keep, src, 0.0).astype(BF16)
            pbs, dss = [], []
            for r in range(tq // sub):
                rs = slice(r * sub, (r + 1) * sub)
                s = lax.dot_general(q_ref[e, rs, :], kk, NT, preferred_element_type=F32) + bb
                p = jnp.exp(s - lse_ref[e, rs, :])
                dp = lax.dot_general(dop[rs], vv, NT, preferred_element_type=F32)
                ds = (p * (dp - deltas[e][rs])).astype(BF16)
                dq_ref[e, rs, :] = jnp.where(keep[rs], jnp.dot(ds, kk, preferred_element_type=F32), 0.0)
                pbs.append(p.astype(BF16))
                dss.append(ds)
            dkt = dkt + lax.dot_general(q_ref[e], jnp.concatenate(dss, axis=0), TN, preferred_element_type=F32)
            dvt = dvt + lax.dot_general(dop, jnp.concatenate(pbs, axis=0), TN, preferred_element_type=F32)
        dkt_scr[...] += dkt
        dvt_scr[...] += dvt

        @pl.when((j == ATT_H // 2 - 1) & (i == nq - 1))
        def _():
            dk_ref[...] = dkt_scr[...].T
            dv_ref[...] = dvt_scr[...].T

    full = pl.BlockSpec((Lp, LANE), lambda j, i: (0, 0))
    pair = pl.BlockSpec((2, tq, LANE), lambda j, i: (j, i, 0))
    return _ridden_call(
        core, rider, name=name, grid=(ATT_H // 2, nq),
        in_specs=[pair, full, full, pl.BlockSpec((1, Lp), lambda j, i: (0, 0)), pl.BlockSpec((2, tq, 1), lambda j, i: (j, i, 0)),
                  pl.BlockSpec((tq, LANE), lambda j, i: (i, j)), pl.BlockSpec((tq, LANE), lambda j, i: (i, j))],
        out_specs=[pair, full, full],
        out_shape=[SDS((ATT_H, Lp, LANE), F32), SDS((Lp, LANE), F32), SDS((Lp, LANE), F32)],
        scratch_shapes=[pltpu.VMEM((LANE, Lp), F32), pltpu.VMEM((LANE, Lp), F32)], args=(qp, kr, vb, bias, lse, db, b))


def _rope_tables(n_tok):
    rows = n_tok // GRID_W
    row = jnp.repeat(jnp.arange(rows), GRID_W).astype(F32)
    col = jnp.tile(jnp.arange(GRID_W), rows).astype(F32)
    axis_dim = HD // 2
    inv = ROPE_THETA ** (-jnp.arange(0, axis_dim, 2, dtype=F32) / axis_dim)
    ang = jnp.concatenate([row[:, None] * inv, col[:, None] * inv], axis=-1)
    ang = jnp.concatenate([jnp.zeros((LANE, axis_dim), F32), ang], axis=0)
    c, s = jnp.cos(ang), jnp.sin(ang)
    c64 = jnp.concatenate([c[:, :16], c[:, :16], c[:, 16:], c[:, 16:]], axis=1)
    s64 = jnp.concatenate([-s[:, :16], s[:, :16], -s[:, 16:], s[:, 16:]], axis=1)
    return jnp.concatenate([c64, c64], axis=1), jnp.concatenate([s64, s64], axis=1)


def _ffn_fwd(tag, h, W, l, j):
    Lp = h.shape[0]
    tm = _rt(Lp, 384)
    sub = FFN_SUB if tm % FFN_SUB == 0 else tm
    ni = Lp // tm
    g = W["norm_gains"][l, 2 * j].reshape(1, D)
    last = NSH - 1

    def body(h_ref, g_ref, wg_ref, wu_ref, wd_ref, h2_ref, a_ref, b_ref, st_ref, nt_ref, n_scr, acc_scr, pad_scr):
        s, i = pl.program_id(0), pl.program_id(1)
        rows = pl.ds(pl.multiple_of(i * tm, tm), tm)

        @pl.when(s == 0)
        def _():
            x = h_ref[...]
            y = x * lax.rsqrt(jnp.mean(x * x, axis=-1, keepdims=True) + EPS) * g_ref[...]
            n_scr[rows, :] = y.astype(BF16)
            nt_ref[...] = y.T.astype(BF16)
            acc_scr[rows, :] = jnp.zeros((tm, D), F32)

        a_ref[:, FF_S:] = jnp.zeros((tm, FF_P - FF_S), BF16)
        b_ref[:, FF_S:] = jnp.zeros((tm, FF_P - FF_S), BF16)
        pad_scr[:, FF_S:] = jnp.zeros((tm, FF_P - FF_S), F32)
        for r in range(tm // sub):
            rs = slice(r * sub, (r + 1) * sub)
            rr = pl.ds(pl.multiple_of(i * tm + r * sub, sub), sub)
            nv = n_scr[rr, :]
            a = jnp.dot(nv, wg_ref[...], preferred_element_type=F32)
            b = jnp.dot(nv, wu_ref[...], preferred_element_type=F32)
            sv = a * jax.nn.sigmoid(a) * b
            a_ref[rs, :FF_S] = a.astype(BF16)
            b_ref[rs, :FF_S] = b.astype(BF16)
            pad_scr[rs, :FF_S] = sv
            acc_scr[rr, :] += jnp.dot(sv.astype(BF16), wd_ref[...], preferred_element_type=F32)
        st_ref[...] = pad_scr[...].T.astype(BF16)

        @pl.when(s == last)
        def _():
            h2_ref[...] = h_ref[...] + 0.5 * acc_scr[rows, :]

    slab = pl.BlockSpec((None, tm, FF_P), lambda s, i: (s, i, 0))
    wup = pl.BlockSpec((None, None, D, FF_S), lambda s, i: (s, l, j, 0))
    h2, a, b, st, nt = pl.pallas_call(
        body, name=f"ffn_{tag}", grid=(NSH, ni),
        in_specs=[pl.BlockSpec((tm, D), lambda s, i: (jnp.where((s == 0) | (s == last), i, ni - 1), 0)),
                  pl.BlockSpec((1, D), lambda s, i: (0, 0)), wup, wup,
                  pl.BlockSpec((None, None, FF_S, D), lambda s, i: (s, l, j, 0))],
        out_specs=[pl.BlockSpec((tm, D), lambda s, i: (jnp.where(s == last, i, 0), 0)), slab, slab,
                   pl.BlockSpec((None, FF_P, tm), lambda s, i: (s, 0, i)),
                   pl.BlockSpec((D, tm), lambda s, i: (0, jnp.where(s == 0, i, ni - 1)))],
        out_shape=[SDS((Lp, D), F32), SDS((NSH, Lp, FF_P), BF16), SDS((NSH, Lp, FF_P), BF16), SDS((NSH, FF_P, Lp), BF16),
                   SDS((D, Lp), BF16)],
        scratch_shapes=[pltpu.VMEM((Lp, D), BF16), pltpu.VMEM((Lp, D), F32), pltpu.VMEM((tm, FF_P), F32)],
        compiler_params=_cp(True))(h, g, W["gate"], W["up"], W["down"])
    return h2, dict(h=h, g=g, nt=nt, a=a, b=b, st=st)


def _ffn_bwd(tag, dh, sv, W, G, l, j, rider=None):
    Lp = dh.shape[0]
    tn = 512
    (da, db, dh_prev, dg), carried = _ffn_bwd_acts(tag, dh, sv, W, l, j, rider)
    G["down"] = _mm(f"bwd_{tag}", sv["st"], dh, grid=(NSH, D // tn),
                    a_spec=pl.BlockSpec((None, FF_P, Lp), lambda s, n: (s, 0, 0)), a_sl=(slice(0, FF_S), slice(None)),
                    b_spec=pl.BlockSpec((Lp, tn), lambda s, n: (0, n)),
                    o_spec=pl.BlockSpec((None, None, FF_S, tn), lambda s, n: (s, j, 0, n)), o_shape=(NSH, 2, FF_S, D), o_dtype=F32,
                    dims=NN, acc_shape=None, scale=0.5, into=G.get("down"))
    for key, dact in (("gate", da), ("up", db)):
        G[key] = _mm(f"bw{key[0]}_{tag}", sv["nt"], dact, grid=(NSH, D // tn),
                     a_spec=pl.BlockSpec((tn, Lp), lambda s, m: (m, 0)),
                     b_spec=pl.BlockSpec((None, Lp, FF_P), lambda s, m: (s, 0, 0)), b_sl=(slice(None), slice(0, FF_S)),
                     o_spec=pl.BlockSpec((None, None, tn, FF_S), lambda s, m: (s, j, m, 0)), o_shape=(NSH, 2, D, FF_S),
                     o_dtype=F32, dims=NN, acc_shape=None, into=G.get(key))
    return dh_prev, dg, carried


def _ffn_bwd_acts(tag, dh, sv, W, l, j, rider=None):
    Lp = dh.shape[0]
    tm = _rt(Lp, 384)
    sub = FFN_SUB if tm % FFN_SUB == 0 else tm
    last = NSH - 1

    def core(ins, outs, scr):
        dh_ref, h_ref, g_ref, wd_ref, wg_ref, wu_ref, a_ref, b_ref = ins
        da_ref, db_ref, dho_ref, dg_ref = outs
        acc_scr, = scr
        s, i = pl.program_id(0), pl.program_id(1)
        rows = pl.ds(pl.multiple_of(i * tm, tm), tm)

        @pl.when(s == 0)
        def _():
            acc_scr[rows, :] = jnp.zeros((tm, D), F32)

        @pl.when((s == 0) & (i == 0))
        def _():
            dg_ref[...] = jnp.zeros_like(dg_ref)

        da_ref[:, FF_S:] = jnp.zeros((tm, FF_P - FF_S), BF16)
        db_ref[:, FF_S:] = jnp.zeros((tm, FF_P - FF_S), BF16)
        for r in range(tm // sub):
            rs = slice(r * sub, (r + 1) * sub)
            rr = pl.ds(pl.multiple_of(i * tm + r * sub, sub), sub)
            ds = 0.5 * lax.dot_general(dh_ref[rs, :].astype(BF16), wd_ref[...], NT, preferred_element_type=F32)
            av = a_ref[rs, :FF_S].astype(F32)
            bv = b_ref[rs, :FF_S].astype(F32)
            sg = jax.nn.sigmoid(av)
            da = (ds * bv * (sg * (1.0 + av * (1.0 - sg)))).astype(BF16)
            db = (ds * (av * sg)).astype(BF16)
            da_ref[rs, :FF_S] = da
            db_ref[rs, :FF_S] = db
            acc_scr[rr, :] += (lax.dot_general(da, wg_ref[...], NT, preferred_element_type=F32)
                               + lax.dot_general(db, wu_ref[...], NT, preferred_element_type=F32))

        @pl.when(s == last)
        def _():
            x = h_ref[...]
            r = lax.rsqrt(jnp.mean(x * x, axis=-1, keepdims=True) + EPS)
            xh = x * r
            dn = acc_scr[rows, :]
            u = dn * g_ref[...]
            dho_ref[...] = dh_ref[...] + r * (u - xh * jnp.mean(u * xh, axis=-1, keepdims=True))
            dg_ref[...] += jnp.sum(dn * xh, axis=0, keepdims=True)

    slab = pl.BlockSpec((None, tm, FF_P), lambda s, i: (s, i, 0))
    wup = pl.BlockSpec((None, None, D, FF_S), lambda s, i: (s, l, j, 0))
    at_last = pl.BlockSpec((tm, D), lambda s, i: (jnp.where(s == last, i, 0), 0))
    vec = pl.BlockSpec((1, D), lambda s, i: (0, 0))
    return _ridden_call(
        core, rider, name=f"bffn_{tag}", grid=(NSH, Lp // tm),
        in_specs=[pl.BlockSpec((tm, D), lambda s, i: (i, 0)), at_last, vec,
                  pl.BlockSpec((None, None, FF_S, D), lambda s, i: (s, l, j, 0)), wup, wup, slab, slab],
        out_specs=[slab, slab, at_last, vec],
        out_shape=[SDS((NSH, Lp, FF_P), BF16), SDS((NSH, Lp, FF_P), BF16), SDS((Lp, D), F32), SDS((1, D), F32)],
        scratch_shapes=[pltpu.VMEM((Lp, D), F32)],
        args=(dh, sv["h"], sv["g"], W["down"], W["gate"], W["up"], sv["a"], sv["b"]))


def _mixer_fwd(tag, h, W, winp, C, l, G_chunks, rider=None):
    Lp = h.shape[0]
    tm = _rt(Lp, 1408)
    g = W["norm_gains"][l, 1].reshape(1, D)
    z, zt = _rms_fwd(f"rmsf_{tag}", h, g)
    tn = 896
    H = _mm(f"win_{tag}", z, winp, grid=(HP // tn, Lp // tm), a_spec=pl.BlockSpec((tm, D), lambda n, i: (i, 0)),
            b_spec=pl.BlockSpec((D, tn), lambda n, i: (0, n)), o_spec=pl.BlockSpec((tm, tn), lambda n, i: (i, n)),
            o_shape=(Lp, HP), o_dtype=F32, dims=NN, acc_shape=None)
    w2p, b2p = C["w2p"][l], C["b2p"][l]
    qh, kh, vh, gf, gb = _gla_prep(f"glap_{tag}", H, w2p, b2p)
    of, sf = _gla_fwd(f"glaf_{tag}", qh, kh, gf, vh, False, G_chunks)
    ob, sb = _gla_fwd(f"glar_{tag}", qh, kh, gb, vh, True, G_chunks)
    gn = W["gn"][l].reshape(1, 512)
    a, at = _gla_post(f"glao_{tag}", of, ob, H, gn)
    gq, gk = C["gq"][l], C["gk"][l]
    qp, kr, vb = _att_prep(f"attp_{tag}", H, gq, gk, C["cos"], C["sin"], C["bd"])
    (b, bt, lse), carried = _att_fwd(f"attf_{tag}", qp, kr, vb, C["bias"], rider)
    if rider is not None:
        W = rider.updated(W, carried)
    row = pl.BlockSpec((tm, D), lambda i: (i, 0))
    proj = functools.partial(
        _mm, grid=(Lp // tm,), a_spec=pl.BlockSpec((tm, 512), lambda i: (i, 0)),
        b_spec=pl.BlockSpec((NSH, None, 512, 256), lambda i: (0, l, 0, 0)), o_spec=row, o_shape=(Lp, D), o_dtype=F32, dims=NN,
        acc_shape=None, shards="cols")
    pa = proj(f"pa_{tag}", a, W["wpa"])
    pb = proj(f"pb_{tag}", b, W["wpb"])
    bm = W["bm"][l]
    y, yt = _merge_fwd(f"mrg_{tag}", H, pa, pb, bm)
    h2 = _mm(f"wout_{tag}", y, W["wout"], grid=(Lp // tm,), a_spec=row,
             b_spec=pl.BlockSpec((NSH, None, 256, D), lambda i: (0, l, 0, 0)), o_spec=row, o_shape=(Lp, D), o_dtype=F32,
             dims=NN, acc_shape=None, res=h, res_spec=row, shards="rows")
    sv = dict(h=h, g=g, zt=zt, H=H, w2p=w2p, b2p=b2p, qh=qh, kh=kh, vh=vh, gf=gf, gb=gb, of=of, ob=ob, sf=sf, sb=sb, gn=gn,
              at=at, gq=gq, gk=gk, qp=qp, kr=kr, vb=vb, b=b, bt=bt, lse=lse, pa=pa, pb=pb, bm=bm, yt=yt, winp=winp)
    return h2, sv, W


def _mixer_bwd(tag, dh, sv, W, C, G, S, l, G_chunks, rider=None):
    Lp = dh.shape[0]
    tm = _rt(Lp, 1408)
    H = sv["H"]
    row = pl.BlockSpec((tm, D), lambda i: (i, 0))
    dy = _mm(f"bdy_{tag}", dh, W["wout"], grid=(Lp // tm,), a_spec=row,
             b_spec=pl.BlockSpec((NSH, None, 256, D), lambda i: (0, l, 0, 0)), o_spec=row, o_shape=(Lp, D), o_dtype=F32, dims=NT,
             acc_shape=None, shards="rows")
    tn = 512
    G["wout"] = _mm(f"bwo_{tag}", sv["yt"], dh, grid=(NSH, D // tn), a_spec=pl.BlockSpec((256, Lp), lambda s, n: (s, 0)),
                    b_spec=pl.BlockSpec((Lp, tn), lambda s, n: (0, n)),
                    o_spec=pl.BlockSpec((None, 256, tn), lambda s, n: (s, 0, n)), o_shape=(NSH, 256, D), o_dtype=F32, dims=NN,
                    acc_shape=None)
    dpa, dpb, dH, S["bm"][l] = _merge_bwd(f"bmrg_{tag}", H, sv["pa"], sv["pb"], sv["bm"], dy)
    dbranch = {}
    for key, dp, xt in (("wpa", dpa, sv["at"]), ("wpb", dpb, sv["bt"])):
        dbranch[key] = _mm(f"bx{key[2]}_{tag}", dp, W[key], grid=(Lp // tm,), a_spec=row,
                           b_spec=pl.BlockSpec((NSH, None, 512, 256), lambda i: (0, l, 0, 0)),
                           o_spec=pl.BlockSpec((tm, 512), lambda i: (i, 0)), o_shape=(Lp, 512), o_dtype=F32, dims=NT,
                           acc_shape=None, shards="cols")
        G[key] = _mm(f"bw{key[2]}_{tag}", xt, dp, grid=(NSH,), a_spec=pl.BlockSpec((512, Lp), lambda s: (0, 0)),
                     b_spec=pl.BlockSpec((Lp, 256), lambda s: (0, s)),
                     o_spec=pl.BlockSpec((None, 512, 256), lambda s: (s, 0, 0)), o_shape=(NSH, 512, 256), o_dtype=F32, dims=NN,
                     acc_shape=None)
    (dqp, dkr, dvb), carried = _att_bwd(f"attb_{tag}", sv["qp"], sv["kr"], sv["vb"], C["bias"], sv["lse"], dbranch["wpb"],
                                        sv["b"], rider)
    dH, S["gq"][l] = _att_q_bwd(f"attq_{tag}", H, sv["gq"], C["cos"], C["sin"], C["bd"], dqp, dH)
    dH, S["gk"][l] = _att_kv_bwd(f"attk_{tag}", H, sv["gk"], C["cos"], C["sin"], C["bd"], dkr, dvb, dH)
    do, dH, S["gn"][l] = _gla_post_bwd(f"bglo_{tag}", sv["of"], sv["ob"], H, sv["gn"], dbranch["wpa"], dH)
    dqf, dkf, dgf, dvf = _gla_bwd(f"bglf_{tag}", sv["qh"], sv["kh"], sv["gf"], sv["vh"], sv["sf"], do, False, G_chunks)
    dqb, dkb, dgb, dvr = _gla_bwd(f"bglr_{tag}", sv["qh"], sv["kh"], sv["gb"], sv["vh"], sv["sb"], do, True, G_chunks)
    dH = _gla_qkv_bwd(f"bglq_{tag}", dqf, dqb, dkf, dkb, dvf, dvr, dH)
    dH, S["w2p"][l], S["b2p"][l] = _gla_gate_bwd(f"bglg_{tag}", H, sv["w2p"], sv["b2p"], dgf, dgb, dH)
    tmm = 256
    G["winp"] = _mm(f"bwi_{tag}", sv["zt"], dH, grid=(D // tmm, HP // 896), a_spec=pl.BlockSpec((tmm, Lp), lambda m, n: (m, 0)),
                    b_spec=pl.BlockSpec((Lp, 896), lambda m, n: (0, n)), o_spec=pl.BlockSpec((tmm, 896), lambda m, n: (m, n)),
                    o_shape=(D, HP), o_dtype=F32, dims=NN, acc_shape=None)
    dz = _mm(f"bdz_{tag}", dH, sv["winp"], grid=(D // 512, Lp // tm), a_spec=pl.BlockSpec((tm, HP), lambda n, i: (i, 0)),
             b_spec=pl.BlockSpec((512, HP), lambda n, i: (n, 0)), o_spec=pl.BlockSpec((tm, 512), lambda n, i: (i, n)),
             o_shape=(Lp, D), o_dtype=F32, dims=NT, acc_shape=None)
    dh_prev, dg = _rms_bwd(f"rmsb_{tag}", dz, sv["h"], sv["g"], dh)
    return dh_prev, dg, carried


def _winp_layer(win_g, l):
    return _win_to_padded(jnp.transpose(win_g[:, l], (1, 0, 2)).reshape(D, D_IN))


def _local_step(x2, tgt2, W, comm=None):
    n_tok = x2.shape[0]
    Lp = n_tok + LANE
    nc = Lp // CHUNK
    g_chunks = max(g for g in (1, 2, 3, 6) if nc % g == 0)
    cos, sin = _rope_tables(n_tok)
    bd = jnp.asarray(np.kron(np.eye(ATT_H, dtype=np.float32), np.full((HD, HD), 1.0 / HD, np.float32)))
    bias = jnp.where(jnp.arange(Lp) >= PAD, 0.0, -1e30).astype(F32).reshape(1, Lp)
    w2, b2 = W["w2"], W["b2"]
    w2p = jnp.zeros((DEPTH, LANE, 512), F32)
    w2p = w2p.at[:, 0:GLA_RANK, 0:256].set(w2[:, 0]).at[:, GLA_RANK:2 * GLA_RANK, 256:512].set(w2[:, 1]).astype(BF16)
    C = dict(cos=cos, sin=sin, bd=bd, bias=bias, w2p=w2p, b2p=b2.reshape(DEPTH, 1, 512),
             gq=jnp.tile(W["qn"], (1, ATT_H)).reshape(DEPTH, 1, 512), gk=jnp.tile(W["kn"], (1, ATT_KV)).reshape(DEPTH, 1, LANE))
    h = jnp.concatenate([jnp.zeros((PAD, D), F32), W["meta"], x2], axis=0)
    saved = []
    for l in range(DEPTH):
        h, s0 = _ffn_fwd(f"l{l}a", h, W, l, 0)
        rider = comm.fwd_rider(W, l) if comm else None
        h, sm, W = _mixer_fwd(f"l{l}m", h, W, _winp_layer(W["win"], l), C, l, g_chunks, rider)
        h, s1 = _ffn_fwd(f"l{l}b", h, W, l, 1)
        saved.append((s0, sm, s1))
    dh, dfin, loss = _loss_head("loss_head", h, W["fin"].reshape(1, D), tgt2)
    S = dict(bm=[None] * DEPTH, gq=[None] * DEPTH, gk=[None] * DEPTH, gn=[None] * DEPTH, w2p=[None] * DEPTH,
             b2p=[None] * DEPTH, ng=[[None] * 3 for _ in range(DEPTH)])
    Gs = [None] * DEPTH
    for l in reversed(range(DEPTH)):
        s0, sm, s1 = saved[l]
        G = {}
        rider = comm.ffn_rider(l) if comm else None
        dh, S["ng"][l][2], carried = _ffn_bwd(f"l{l}b", dh, s1, W, G, l, 1, rider)
        if rider is not None:
            comm.ffn_carried(l, carried)
        rider = comm.bwd_rider(l) if comm else None
        dh, S["ng"][l][1], carried = _mixer_bwd(f"l{l}m", dh, sm, W, C, G, S, l, g_chunks, rider)
        if rider is not None:
            comm.bwd_carried(l, carried)
        dh, S["ng"][l][0], _ = _ffn_bwd(f"l{l}a", dh, s0, W, G, l, 0)
        Gs[l] = G
        if comm:
            comm.layer_done(l, G)
    small = dict(
        meta=dh[PAD:LANE],
        norm_gains=jnp.stack([jnp.concatenate(S["ng"][l], axis=0) for l in range(DEPTH)]),
        w2=jnp.stack([jnp.stack([S["w2p"][l][0:GLA_RANK, 0:256], S["w2p"][l][GLA_RANK:2 * GLA_RANK, 256:512]]) for l in range(DEPTH)]),
        b2=jnp.stack([S["b2p"][l].reshape(2, 256) for l in range(DEPTH)]),
        gn=jnp.concatenate(S["gn"], axis=0),
        qn=jnp.stack([S["gq"][l].reshape(ATT_H, HD).sum(0) for l in range(DEPTH)]),
        kn=jnp.stack([S["gk"][l].reshape(ATT_KV, HD).sum(0) for l in range(DEPTH)]),
        bm=jnp.stack(S["bm"]),
        fin=dfin.reshape(D),
    )
    return loss[0, 0], dh, Gs, small


def _win_to_padded(w):
    pad = jnp.zeros(w.shape[:-1] + (HP - D_IN,), w.dtype)
    return jnp.concatenate([w[..., 2336:4384], w[..., 0:1536], w[..., 1568:2336], w[..., 1536:1568], pad], axis=-1)


def _win_from_padded(w):
    return jnp.concatenate([w[..., 2048:3584], w[..., 4352:4384], w[..., 3584:4352], w[..., 0:2048]], axis=-1)


def _assemble(g):
    W = dict(gate=g["gate"], up=g["up"], down=g["down"], win=g["win"], wpa=g["wpa"], wpb=g["wpb"], wout=g["wout"])
    sm = g["small"]
    parts = _unpack(sm, SHARDED_SMALL)
    W["meta"] = jnp.transpose(parts["meta"], (1, 0, 2)).reshape(N_META, D)
    W["norm_gains"] = jnp.transpose(parts["norm_gains"], (1, 2, 0, 3)).reshape(DEPTH, 3, D)
    W["w2"] = jnp.transpose(parts["w2"], (1, 2, 3, 0, 4)).reshape(DEPTH, 2, GLA_RANK, 256)
    W["b2"] = jnp.transpose(parts["b2"], (1, 2, 0, 3)).reshape(DEPTH, 2, 256)
    W["bm"] = jnp.transpose(parts["bm"], (1, 2, 0, 3)).reshape(DEPTH, 2, D)
    return W


SHARDED_SMALL = dict(meta=(N_META, 256), norm_gains=(DEPTH, 3, 256), w2=(DEPTH, 2, GLA_RANK, 64), b2=(DEPTH, 2, 64),
                     bm=(DEPTH, 2, 256))
FULL_SMALL = dict(meta=(N_META, D), norm_gains=(DEPTH, 3, D), w2=(DEPTH, 2, GLA_RANK, 256), b2=(DEPTH, 2, 256),
                  gn=(DEPTH, 512), qn=(DEPTH, HD), kn=(DEPTH, HD), bm=(DEPTH, 2, D), fin=(D,))


def _pack(parts, table, rows):
    flat = jnp.concatenate([parts[k].reshape(-1).astype(F32) for k in table])
    return jnp.pad(flat, (0, rows * LANE - flat.shape[0])).reshape(rows, LANE)


def _unpack(packed, table):
    lead = packed.shape[:-2]
    flat = packed.reshape(lead + (-1,))
    out, off = {}, 0
    for k, shp in table.items():
        n = int(np.prod(shp))
        out[k] = flat[..., off:off + n].reshape(lead + tuple(shp))
        off += n
    return out


def _rows_for(table, mult):
    n = sum(int(np.prod(s)) for s in table.values())
    return -(-n // (LANE * mult)) * mult


SMALL_ROWS = _rows_for(SHARDED_SMALL, 16)
GRAD_ROWS = _rows_for(dict(FULL_SMALL, loss=(1,)), 8)


def _place():
    x, y, c = lax.axis_index("x"), lax.axis_index("y"), lax.axis_index("c")
    return x, y, c


def _other_chips(x, y):
    return [(1 - x, y), (x, 1 - y), (1 - x, 1 - y)]


def _cast_place(name, w3, slot):
    A, R, Cc = w3.shape
    tr = _rt(R, 512, 16)

    def body(p_ref, w_ref, o_ref):
        o_ref[...] = w_ref[...].astype(BF16)

    return pl.pallas_call(
        body, name=name,
        grid_spec=pltpu.PrefetchScalarGridSpec(
            num_scalar_prefetch=1, grid=(A, R // tr),
            in_specs=[pl.BlockSpec((None, tr, Cc), lambda a, r, p_ref: (a, r, 0))],
            out_specs=pl.BlockSpec((None, None, tr, Cc), lambda a, r, p_ref: (p_ref[0], a, r, 0))),
        out_shape=SDS((NSH, A, R, Cc), BF16))(slot, w3)


class _GatherRider:
    def __init__(self, keys, bufs, layers):
        n = len(bufs)
        self.keys, self.ins, self.layers = keys, list(bufs), layers
        self.out_shape = [SDS(b.shape, b.dtype) for b in bufs]
        self.aliases = {a: a for a in range(n)}
        self.sems = [pltpu.SemaphoreType.DMA((n, 3)) for _ in range(4)]

    def updated(self, W, carried):
        return {**W, **dict(zip(self.keys, carried))}

    def hooks(self, ins, outs, sems):
        send, recv, fsend, frecv = sems
        n = len(outs)

        def rows(a, slot, core):
            ref, l = outs[a], self.layers[a]
            half = ref.shape[-2] // 2
            return ref.at[slot, pl.ds(core * half, half)] if l is None else ref.at[slot, l, pl.ds(core * half, half)]

        def ici(a, k, part, px, py, c):
            return pltpu.make_async_remote_copy(src_ref=part, dst_ref=part, send_sem=send.at[a, k], recv_sem=recv.at[a, k],
                                                device_id=(px, py, c), device_id_type=MESH)

        def d2d(a, k, part, x, y, c):
            return pltpu.make_async_remote_copy(src_ref=part, dst_ref=part, send_sem=fsend.at[a, k], recv_sem=frecv.at[a, k],
                                                device_id=(x, y, 1 - c), device_id_type=MESH)

        def start():
            x, y, c = _place()
            for a in range(n):
                for k, (px, py) in enumerate(_other_chips(x, y)):
                    ici(a, k, rows(a, 2 * x + y, c), px, py, c).start()

        def mid():
            x, y, c = _place()
            for a in range(n):
                for k, (px, py) in enumerate(_other_chips(x, y)):
                    landed = rows(a, 2 * px + py, c)
                    ici(a, k, landed, px, py, c).wait_recv()
                    d2d(a, k, landed, x, y, c).start()

        def end():
            x, y, c = _place()
            for a in range(n):
                for k, (px, py) in enumerate(_other_chips(x, y)):
                    d2d(a, k, rows(a, 2 * px + py, 1 - c), x, y, c).wait_recv()
                    ici(a, k, rows(a, 2 * x + y, c), px, py, c).wait_send()
                    d2d(a, k, rows(a, 2 * px + py, c), x, y, c).wait_send()

        return start, mid, end


class _ChipExchangeRider:
    def __init__(self, arrs):
        n = len(arrs)
        self.ins = list(arrs)
        self.out_shape = [SDS((3,) + a.shape[1:], a.dtype) for a in arrs]
        self.aliases = {}
        self.sems = [pltpu.SemaphoreType.DMA((n, 3)), pltpu.SemaphoreType.DMA((n, 3))]

    def hooks(self, ins, outs, sems):
        send, recv = sems

        def copy(a, k, px, py, c):
            return pltpu.make_async_remote_copy(src_ref=ins[a].at[2 * px + py], dst_ref=outs[a].at[k], send_sem=send.at[a, k],
                                                recv_sem=recv.at[a, k], device_id=(px, py, c), device_id_type=MESH)

        def start():
            x, y, c = _place()
            for a in range(len(ins)):
                for k, (px, py) in enumerate(_other_chips(x, y)):
                    copy(a, k, px, py, c).start()

        def mid():
            pass

        def end():
            x, y, c = _place()
            for a in range(len(ins)):
                for k, (px, py) in enumerate(_other_chips(x, y)):
                    copy(a, k, px, py, c).wait()

        return start, mid, end


def _run_rider(name, rider):
    r_in, r_out = len(rider.ins), len(rider.out_shape)

    def body(*refs):
        start, mid, end = rider.hooks(refs[:r_in], refs[r_in:r_in + r_out], refs[r_in + r_out:])
        start()
        mid()
        end()

    any_spec = pl.BlockSpec(memory_space=pl.ANY)
    return pl.pallas_call(
        body, name=name, in_specs=[any_spec] * r_in, out_specs=[any_spec] * r_out, out_shape=list(rider.out_shape),
        scratch_shapes=list(rider.sems), input_output_aliases=dict(rider.aliases))(*rider.ins)


class _PairExchangeRider:
    def __init__(self, arrs):
        n = len(arrs)
        self.ins = list(arrs)
        self.out_shape = [SDS((NSH, a.shape[1] // 2, a.shape[2]), a.dtype) for a in arrs]
        self.aliases = {}
        self.sems = [pltpu.SemaphoreType.DMA((n,)), pltpu.SemaphoreType.DMA((n,))]

    def hooks(self, ins, outs, sems):
        send, recv = sems

        def copy(a):
            x, y, c = _place()
            half = ins[a].shape[1] // 2
            return pltpu.make_async_remote_copy(
                src_ref=ins[a].at[:, pl.ds((1 - c) * half, half)], dst_ref=outs[a], send_sem=send.at[a], recv_sem=recv.at[a],
                device_id=(x, y, 1 - c), device_id_type=MESH)

        def start():
            for a in range(len(ins)):
                copy(a).start()

        def mid():
            pass

        def end():
            for a in range(len(ins)):
                copy(a).wait()

        return start, mid, end


def _pair_add(name, g, p, core):
    _, Rh, Cc = p.shape
    tr = _rt(Rh, 512, 16)
    nr = Rh // tr

    def body(c_ref, g_ref, p_ref, o_ref, ob_ref):
        v = g_ref[...] + p_ref[...]
        o_ref[...] = v
        ob_ref[...] = v.astype(BF16)

    blk = pl.BlockSpec((None, tr, Cc), lambda s, r, c_ref: (s, r, 0))
    return pl.pallas_call(
        body, name=name,
        grid_spec=pltpu.PrefetchScalarGridSpec(
            num_scalar_prefetch=1, grid=(NSH, nr),
            in_specs=[pl.BlockSpec((None, tr, Cc), lambda s, r, c_ref: (s, c_ref[0] * nr + r, 0)), blk],
            out_specs=[blk, blk]),
        out_shape=[SDS((NSH, Rh, Cc), F32), SDS((NSH, Rh, Cc), BF16)])(core, g, p)


def _chip_add(name, hsum, q, chip, core, l, into):
    _, Rh, Cc = hsum.shape
    tr = _rt(Rh, 512, 8)
    nr = Rh // tr

    def body(*refs):
        h_ref, q_ref, o_ref = refs[2], refs[3], refs[-1]
        o_ref[...] = ((h_ref[...] + q_ref[0].astype(F32)) + q_ref[1].astype(F32)) + q_ref[2].astype(F32)

    in_specs = [pl.BlockSpec((None, tr, Cc), lambda r, p_ref, c_ref: (p_ref[0], r, 0)),
                pl.BlockSpec((3, tr, Cc), lambda r, p_ref, c_ref: (0, r, 0))]
    args = [chip, core, hsum, q]
    aliases = {}
    if into is not None:
        in_specs.append(pl.BlockSpec(memory_space=pl.ANY))
        args.append(into)
        aliases = {4: 0}
    return pl.pallas_call(
        body, name=name,
        grid_spec=pltpu.PrefetchScalarGridSpec(
            num_scalar_prefetch=2, grid=(nr,), in_specs=in_specs,
            out_specs=pl.BlockSpec((None, tr, Cc), lambda r, p_ref, c_ref: (l, c_ref[0] * nr + r, 0))),
        out_shape=SDS((DEPTH, 2 * Rh, Cc), F32), input_output_aliases=aliases)(*args)


def _pair_share(arrs):
    n = len(arrs)

    def body(*refs):
        outs = refs[n:2 * n]
        send, recv = refs[2 * n:]
        x, y, c = _place()
        cps = []
        for a in range(n):
            half = outs[a].shape[1] // 2
            mine = outs[a].at[:, pl.ds(c * half, half)]
            cp = pltpu.make_async_remote_copy(
                src_ref=mine, dst_ref=mine, send_sem=send.at[a], recv_sem=recv.at[a],
                device_id=(x, y, 1 - c), device_id_type=MESH)
            cp.start()
            cps.append(cp)
        for a, cp in enumerate(cps):
            cp.wait_send()
            half = outs[a].shape[1] // 2
            theirs = outs[a].at[:, pl.ds((1 - c) * half, half)]
            pltpu.make_async_remote_copy(
                src_ref=theirs, dst_ref=theirs, send_sem=send.at[a], recv_sem=recv.at[a],
                device_id=(x, y, 1 - c), device_id_type=MESH).wait_recv()

    any_spec = pl.BlockSpec(memory_space=pl.ANY)
    return pl.pallas_call(
        body, name="pair_share", in_specs=[any_spec] * n, out_specs=[any_spec] * n,
        out_shape=[SDS(a.shape, a.dtype) for a in arrs], input_output_aliases={a: a for a in range(n)},
        scratch_shapes=[pltpu.SemaphoreType.DMA((n,)), pltpu.SemaphoreType.DMA((n,))])(*arrs)


def _allreduce_small(v):
    rows = v.shape[0]

    def body(v_ref, o_ref, buf, send, recv):
        x, y, c = _place()
        me = 4 * x + 2 * y + c
        buf[me] = v_ref[...]
        cps = []
        k = 0
        for dx in range(2):
            for dy in range(2):
                for dc in range(2):
                    if dx + dy + dc == 0:
                        continue
                    cp = pltpu.make_async_remote_copy(
                        src_ref=v_ref, dst_ref=buf.at[me], send_sem=send.at[k], recv_sem=recv.at[k],
                        device_id=(jnp.bitwise_xor(x, dx), jnp.bitwise_xor(y, dy), jnp.bitwise_xor(c, dc)), device_id_type=MESH)
                    cp.start()
                    cps.append((cp, dx, dy, dc))
                    k += 1
        for k, (cp, dx, dy, dc) in enumerate(cps):
            cp.wait_send()
            src = 4 * jnp.bitwise_xor(x, dx) + 2 * jnp.bitwise_xor(y, dy) + jnp.bitwise_xor(c, dc)
            pltpu.make_async_remote_copy(
                src_ref=v_ref, dst_ref=buf.at[src], send_sem=send.at[k], recv_sem=recv.at[k],
                device_id=(x, y, c), device_id_type=MESH).wait_recv()
        acc = buf[0]
        for d in range(1, 8):
            acc = acc + buf[d]
        o_ref[...] = acc

    vm = pl.BlockSpec(memory_space=pltpu.VMEM)
    return pl.pallas_call(
        body, name="allreduce_small", in_specs=[vm], out_specs=vm, out_shape=SDS((rows, LANE), F32),
        scratch_shapes=[pltpu.VMEM((8, rows, LANE), F32), pltpu.SemaphoreType.DMA((7,)), pltpu.SemaphoreType.DMA((7,))])(v)


def _adamw(name, w, g, m, v):
    A, R, Cc = w.shape
    tr = _rt(R, 512, 8)

    def body(w_ref, g_ref, m_ref, v_ref, d_ref, mo_ref, vo_ref):
        gv = g_ref[...]
        mn = ADAM_B1 * m_ref[...] + (1.0 - ADAM_B1) * gv
        vn = ADAM_B2 * v_ref[...] + (1.0 - ADAM_B2) * (gv * gv)
        m_hat = mn / (1.0 - ADAM_B1 ** ADAM_STEP)
        v_hat = vn / (1.0 - ADAM_B2 ** ADAM_STEP)
        d_ref[...] = -ADAM_LR * (m_hat / (jnp.sqrt(v_hat) + ADAM_EPS) + ADAM_WD * w_ref[...])
        mo_ref[...] = mn
        vo_ref[...] = vn

    blk = pl.BlockSpec((None, tr, Cc), lambda a, r: (a, r, 0))
    return pl.pallas_call(
        body, name=name, grid=(A, R // tr), in_specs=[blk] * 4, out_specs=[blk] * 3,
        out_shape=[SDS(w.shape, F32)] * 3)(w, g, m, v)


BIG = ("gate", "up", "down", "win", "wpa", "wpb", "wout")
SMALL = ("meta", "norm_gains", "w2", "b2", "gn", "qn", "kn", "bm", "fin")


def _view3(a):
    return a.reshape(a.shape[0], -1, a.shape[-1])


class _StepComm:
    def __init__(self, chip, core):
        self.pvec, self.cvec = chip.reshape(1), core.reshape(1)
        self.loc, self.sums = {}, {}
        self.red = {k: None for k in BIG}

    def fwd_rider(self, W, l):
        if l + 1 >= DEPTH:
            return None
        return _GatherRider(BIG, [W[k] for k in BIG], [l + 1] * len(BIG))

    def layer_done(self, l, G):
        dwin = jnp.transpose(_win_from_padded(G["winp"]).reshape(D, NSH, D_IN // NSH), (1, 0, 2))
        self.loc[l] = [dwin if k == "win" else _view3(G[k]) for k in BIG]

    def ffn_rider(self, l):
        return _PairExchangeRider(self.loc[l + 1]) if l + 1 < DEPTH else None

    def ffn_carried(self, l, got):
        self._pair_add(l + 1, got)

    def _pair_add(self, l, got):
        self.sums[l] = [_pair_add(f"pair_add_{k}_l{l}", a, p, self.cvec) for k, a, p in zip(BIG, self.loc.pop(l), got)]

    def bwd_rider(self, l):
        if l + 1 >= DEPTH:
            return None
        return _ChipExchangeRider([s[1] for s in self.sums[l + 1]])

    def bwd_carried(self, l, arrived):
        self._chip_add(l + 1, arrived)

    def _chip_add(self, l, arrived):
        for k, s, q in zip(BIG, self.sums.pop(l), arrived):
            self.red[k] = _chip_add(f"chip_add_{k}_l{l}", s[0], q, self.pvec, self.cvec, l, self.red[k])

    def finish(self):
        self._pair_add(0, _run_rider("pair_exchange_l0", _PairExchangeRider(self.loc[0])))
        self._chip_add(0, _run_rider("chip_exchange_l0", _ChipExchangeRider([s[1] for s in self.sums[0]])))
        return dict(zip(BIG, _pair_share([self.red[k] for k in BIG])))


def kernel(x, meta_tokens, norm_gains, ffn_w_gate, ffn_w_up, ffn_w_down, w_in, gla_w2, gla_b2, gla_gn, q_norm, k_norm, w_pa, w_pb, b_merge, w_out, final_norm, loss_target, m_meta_tokens, m_norm_gains, m_ffn_w_gate, m_ffn_w_up, m_ffn_w_down, m_w_in, m_gla_w2, m_gla_b2, m_gla_gn, m_q_norm, m_k_norm, m_w_pa, m_w_pb, m_b_merge, m_w_out, m_final_norm, v_meta_tokens, v_norm_gains, v_ffn_w_gate, v_ffn_w_up, v_ffn_w_down, v_w_in, v_gla_w2, v_gla_b2, v_gla_gn, v_q_norm, v_k_norm, v_w_pa, v_w_pb, v_b_merge, v_w_out, v_final_norm):
    big_w = dict(gate=ffn_w_gate, up=ffn_w_up, down=ffn_w_down, win=w_in, wpa=w_pa, wpb=w_pb, wout=w_out)
    big_m = dict(gate=m_ffn_w_gate, up=m_ffn_w_up, down=m_ffn_w_down, win=m_w_in, wpa=m_w_pa, wpb=m_w_pb, wout=m_w_out)
    big_v = dict(gate=v_ffn_w_gate, up=v_ffn_w_up, down=v_ffn_w_down, win=v_w_in, wpa=v_w_pa, wpb=v_w_pb, wout=v_w_out)
    small_w = dict(meta=meta_tokens, norm_gains=norm_gains, w2=gla_w2, b2=gla_b2, gn=gla_gn, qn=q_norm, kn=k_norm,
                   bm=b_merge, fin=final_norm)
    small_m = dict(meta=m_meta_tokens, norm_gains=m_norm_gains, w2=m_gla_w2, b2=m_gla_b2, gn=m_gla_gn, qn=m_q_norm,
                   kn=m_k_norm, bm=m_b_merge, fin=m_final_norm)
    small_v = dict(meta=v_meta_tokens, norm_gains=v_norm_gains, w2=v_gla_w2, b2=v_gla_b2, gn=v_gla_gn, qn=v_q_norm,
                   kn=v_k_norm, bm=v_b_merge, fin=v_final_norm)
    xi, yi, ci = _place()
    chip = (2 * xi + yi).astype(jnp.int32)

    comm = _StepComm(chip, ci.astype(jnp.int32))
    shard_pack = _pack({k: small_w[k] for k in SHARDED_SMALL}, SHARDED_SMALL, SMALL_ROWS)
    placed = [_cast_place(f"cast_{k}", _view3(big_w[k]), comm.pvec) for k in BIG]
    placed.append(lax.dynamic_update_slice(jnp.zeros((NSH, SMALL_ROWS, LANE), F32), shard_pack[None], (chip, 0, 0)))
    gathered = _run_rider("gather_l0", _GatherRider(BIG + ("small",), placed, [0] * len(BIG) + [None]))
    W = _assemble(dict(zip(BIG + ("small",), gathered)))
    W.update(gn=gla_gn, qn=q_norm, kn=k_norm, fin=final_norm)

    loss, dh0, _, gs = _local_step(x[0], loss_target[0], W, comm)
    grad_x = dh0[LANE:][None]
    shared = comm.finish()
    grads, deltas, new_m, new_v = {}, {}, {}, {}
    for k in BIG:
        r = shared[k]
        shp = big_w[k].shape
        grads[k] = r.reshape(shp)
        d, mn, vn = _adamw(f"adamw_{k}", _view3(big_w[k]), r, _view3(big_m[k]), _view3(big_v[k]))
        deltas[k], new_m[k], new_v[k] = d.reshape(shp), mn.reshape(shp), vn.reshape(shp)

    gs["loss"] = loss.reshape(1)
    table = dict(FULL_SMALL, loss=(1,))
    tot = _unpack(_allreduce_small(_pack(gs, table, GRAD_ROWS)), table)
    loss_out = tot["loss"][0]
    sl = dict(meta=(1, 256), norm_gains=(2, 256), w2=(3, 64), b2=(2, 64), bm=(2, 256))
    for k in SMALL:
        gk = tot[k]
        if k in sl:
            ax, width = sl[k]
            gk = lax.dynamic_slice_in_dim(gk, chip * width, width, axis=ax)
        grads[k] = gk
    tbl = {k: small_w[k].shape for k in SMALL}
    rows = _rows_for(tbl, 8)
    packs = [_pack(src, tbl, rows)[None] for src in (small_w, grads, small_m, small_v)]
    d, mn, vn = _adamw("adamw_small", *packs)
    for dst, packed in ((deltas, d), (new_m, mn), (new_v, vn)):
        dst.update(_unpack(packed[0], tbl))

    order = ("meta", "norm_gains", "gate", "up", "down", "win", "w2", "b2", "gn", "qn", "kn", "wpa", "wpb", "bm", "wout", "fin")
    return (loss_out, grad_x, *[grads[k] for k in order], *[deltas[k] for k in order], *[new_m[k] for k in order],
            *[new_v[k] for k in order])
```

```python
import functools

import numpy as np
import jax
import jax.numpy as jnp
from jax import lax
from jax.experimental import pallas as pl
from jax.experimental.pallas import tpu as pltpu

F32, BF16 = jnp.float32, jnp.bfloat16
SDS = jax.ShapeDtypeStruct
HIGHEST = lax.Precision.HIGHEST
MESH = pl.DeviceIdType.MESH

D = 1024
DEPTH = 4
N_META = 16
GRID_W = 64
GLA_H, GLA_DK, GLA_DV, GLA_RANK, GLA_TAU, CHUNK = 4, 64, 128, 16, 16.0, 64
ATT_H, ATT_KV, HD = 8, 2, 64
D_FF = 2816
EPS = 1e-6
ROPE_THETA = 10000.0
ADAM_LR, ADAM_B1, ADAM_B2, ADAM_EPS, ADAM_WD, ADAM_STEP = 0.001, 0.9, 0.999, 1e-08, 0.01, 10

NSH = 4
FF_S = D_FF // NSH
FF_P = 768
LANE = 128
PAD = LANE - N_META
D_IN = 4384
C_GA, C_GB, C_QA, C_KA, C_VA, C_RA, C_QB, C_KB, C_VB, C_LR, HP = 0, 1024, 2048, 2304, 2560, 3072, 3584, 4096, 4224, 4352, 4480
VMEM_BIG = 56 * 2 ** 20
ATT_SUB = 192
RIDER_MID_BACK = 5


def _rt(n, cap, mult=LANE):
    best = None
    t = mult
    while t <= min(n, cap):
        if n % t == 0:
            best = t
        t += mult
    assert best is not None, (n, cap, mult)
    return best


def _cp(big=False):
    return pltpu.CompilerParams(vmem_limit_bytes=VMEM_BIG) if big else None


def _row_ids(i, tm):
    return i * tm + lax.broadcasted_iota(jnp.int32, (tm, 1), 0)


def _mm(name, a, b, *, grid, a_spec, b_spec, o_spec, o_shape, o_dtype, dims, acc_shape, nk=1, scale=None,
        res=None, res_spec=None, a_sl=None, b_sl=None, pad_w=None, into=None, shards=None):
    has_res, has_into = res is not None, into is not None

    def body(*refs):
        a_ref, b_ref = refs[0], refs[1]
        p = 2
        res_ref = None
        if has_res:
            res_ref = refs[p]
            p += 1
        if has_into:
            p += 1
        o_ref = refs[p]
        acc_ref = refs[p + 1] if nk > 1 else None
        av = (a_ref[a_sl] if a_sl is not None else a_ref[...]).astype(BF16)
        bv = (b_ref[b_sl] if b_sl is not None else b_ref[...]).astype(BF16)
        if shards == "rows":
            bv = bv.reshape(bv.shape[0] * bv.shape[1], bv.shape[2])
        if shards == "cols":
            w = bv.shape[2]
            if dims == NN:
                prod = jnp.concatenate([lax.dot_general(av, bv[s], NN, preferred_element_type=F32) for s in range(NSH)], axis=1)
            else:
                prod = sum(lax.dot_general(av[:, s * w:(s + 1) * w], bv[s], NT, preferred_element_type=F32) for s in range(NSH))
        else:
            prod = lax.dot_general(av, bv, dims, preferred_element_type=F32)

        def finish(v):
            if scale is not None:
                v = v * scale
            if has_res:
                v = v + res_ref[...]
            v = v.astype(o_dtype)
            if pad_w is None:
                o_ref[...] = v
            else:
                w = v.shape[-1]
                o_ref[:, :w] = v
                o_ref[:, w:] = jnp.zeros((v.shape[0], pad_w - w), o_dtype)

        if nk == 1:
            finish(prod)
        else:
            k = pl.program_id(len(grid) - 1)

            @pl.when(k == 0)
            def _():
                acc_ref[...] = prod

            @pl.when(k > 0)
            def _():
                acc_ref[...] += prod

            @pl.when(k == nk - 1)
            def _():
                finish(acc_ref[...])

    in_specs = [a_spec, b_spec]
    args = [a, b]
    if has_res:
        in_specs.append(res_spec)
        args.append(res)
    aliases = {}
    if has_into:
        aliases = {len(args): 0}
        in_specs.append(pl.BlockSpec(memory_space=pl.ANY))
        args.append(into)
        o_shape = into.shape
    return pl.pallas_call(
        body, name=name, grid=grid, in_specs=in_specs, out_specs=o_spec, out_shape=SDS(tuple(o_shape), o_dtype),
        scratch_shapes=[pltpu.VMEM(acc_shape, F32)] if nk > 1 else [], input_output_aliases=aliases,
        compiler_params=_cp(True))(*args)


NN = (((1,), (0,)), ((), ()))
NT = (((1,), (1,)), ((), ()))
TN = (((0,), (0,)), ((), ()))


def _rms_fwd(name, h, g):
    Lp = h.shape[0]
    tm = _rt(Lp, 384)

    def body(h_ref, g_ref, n_ref, nt_ref):
        x = h_ref[...]
        r = lax.rsqrt(jnp.mean(x * x, axis=-1, keepdims=True) + EPS)
        y = x * r * g_ref[...]
        n_ref[...] = y.astype(BF16)
        nt_ref[...] = y.T.astype(BF16)

    return pl.pallas_call(
        body, name=name, grid=(Lp // tm,),
        in_specs=[pl.BlockSpec((tm, D), lambda i: (i, 0)), pl.BlockSpec((1, D), lambda i: (0, 0))],
        out_specs=[pl.BlockSpec((tm, D), lambda i: (i, 0)), pl.BlockSpec((D, tm), lambda i: (0, i))],
        out_shape=[SDS((Lp, D), BF16), SDS((D, Lp), BF16)])(h, g)


def _rms_bwd(name, dn, h, g, dh):
    Lp = h.shape[0]
    tm = _rt(Lp, 384)

    def body(dn_ref, h_ref, g_ref, dh_ref, o_ref, dg_ref):
        x = h_ref[...]
        r = lax.rsqrt(jnp.mean(x * x, axis=-1, keepdims=True) + EPS)
        xh = x * r
        dn = dn_ref[...]
        u = dn * g_ref[...]
        o_ref[...] = dh_ref[...] + r * (u - xh * jnp.mean(u * xh, axis=-1, keepdims=True))

        @pl.when(pl.program_id(0) == 0)
        def _():
            dg_ref[...] = jnp.zeros_like(dg_ref)

        dg_ref[...] += jnp.sum(dn * xh, axis=0, keepdims=True)

    row = pl.BlockSpec((tm, D), lambda i: (i, 0))
    vec = pl.BlockSpec((1, D), lambda i: (0, 0))
    return pl.pallas_call(
        body, name=name, grid=(Lp // tm,), in_specs=[row, row, vec, row], out_specs=[row, vec],
        out_shape=[SDS((Lp, D), F32), SDS((1, D), F32)])(dn, h, g, dh)


def _loss_head(name, h, g, tgt):
    Lp = h.shape[0]
    tm = LANE

    def body(h_ref, g_ref, t_ref, dh_ref, dg_ref, loss_ref):
        i = pl.program_id(0)
        x = h_ref[...]
        r = lax.rsqrt(jnp.mean(x * x, axis=-1, keepdims=True) + EPS)
        xh = x * r
        gg = g_ref[...]
        err = jnp.where(i >= 1, xh * gg - t_ref[...], 0.0)
        dy = err * (1.0 / D)
        u = dy * gg
        dh_ref[...] = r * (u - xh * jnp.mean(u * xh, axis=-1, keepdims=True))

        @pl.when(i == 0)
        def _():
            dg_ref[...] = jnp.zeros_like(dg_ref)
            loss_ref[...] = jnp.zeros_like(loss_ref)

        dg_ref[...] += jnp.sum(dy * xh, axis=0, keepdims=True)
        loss_ref[...] += (0.5 / D) * jnp.sum(err * err)

    row = pl.BlockSpec((tm, D), lambda i: (i, 0))
    vec = pl.BlockSpec((1, D), lambda i: (0, 0))
    return pl.pallas_call(
        body, name=name, grid=(Lp // tm,),
        in_specs=[row, vec, pl.BlockSpec((tm, D), lambda i: (jnp.maximum(i - 1, 0), 0))],
        out_specs=[row, vec, pl.BlockSpec((8, LANE), lambda i: (0, 0))],
        out_shape=[SDS((Lp, D), F32), SDS((1, D), F32), SDS((8, LANE), F32)])(h, g, tgt)


def _swiglu_fwd(name, a, b):
    Lp = a.shape[1]
    tm = _rt(Lp, 384)

    def body(a_ref, b_ref, s_ref, st_ref):
        av = a_ref[...]
        s = av * jax.nn.sigmoid(av) * b_ref[...]
        s_ref[...] = s.astype(BF16)
        st_ref[...] = s.T.astype(BF16)

    slab = pl.BlockSpec((None, tm, FF_P), lambda j, i: (j, i, 0))
    return pl.pallas_call(
        body, name=name, grid=(NSH, Lp // tm), in_specs=[slab, slab],
        out_specs=[slab, pl.BlockSpec((None, FF_P, tm), lambda j, i: (j, 0, i))],
        out_shape=[SDS((NSH, Lp, FF_P), BF16), SDS((NSH, FF_P, Lp), BF16)])(a, b)


def _swiglu_bwd(name, a, b, ds):
    Lp = a.shape[1]
    tm = _rt(Lp, 384)

    def body(a_ref, b_ref, ds_ref, da_ref, db_ref):
        av = a_ref[...]
        sg = jax.nn.sigmoid(av)
        dsv = ds_ref[...]
        da_ref[...] = (dsv * b_ref[...] * (sg * (1.0 + av * (1.0 - sg)))).astype(BF16)
        db_ref[...] = (dsv * (av * sg)).astype(BF16)

    slab = pl.BlockSpec((None, tm, FF_P), lambda j, i: (j, i, 0))
    return pl.pallas_call(
        body, name=name, grid=(NSH, Lp // tm), in_specs=[slab, slab, slab], out_specs=[slab, slab],
        out_shape=[SDS((NSH, Lp, FF_P), BF16), SDS((NSH, Lp, FF_P), BF16)])(a, b, ds)


def _merge_fwd(name, H, pa, pb, bm):
    Lp = H.shape[0]
    tm = _rt(Lp, 384)

    def body(g_ref, pa_ref, pb_ref, bm_ref, y_ref, yt_ref):
        gv = g_ref[...]
        y = (jax.nn.sigmoid(gv[:, :D] + bm_ref[0:1, :]) * pa_ref[...]
             + jax.nn.sigmoid(gv[:, D:] + bm_ref[1:2, :]) * pb_ref[...])
        y_ref[...] = y.astype(BF16)
        yt_ref[...] = y.T.astype(BF16)

    row = pl.BlockSpec((tm, D), lambda i: (i, 0))
    return pl.pallas_call(
        body, name=name, grid=(Lp // tm,),
        in_specs=[pl.BlockSpec((tm, 2 * D), lambda i: (i, 0)), row, row, pl.BlockSpec((2, D), lambda i: (0, 0))],
        out_specs=[row, pl.BlockSpec((D, tm), lambda i: (0, i))],
        out_shape=[SDS((Lp, D), BF16), SDS((D, Lp), BF16)])(H, pa, pb, bm)


def _merge_bwd(name, H, pa, pb, bm, dy):
    Lp = H.shape[0]
    tm = _rt(Lp, 384)

    def body(g_ref, pa_ref, pb_ref, bm_ref, dy_ref, dpa_ref, dpb_ref, dh_ref, dbm_ref):
        gv = g_ref[...]
        dyv = dy_ref[...]
        sa = jax.nn.sigmoid(gv[:, :D] + bm_ref[0:1, :])
        sb = jax.nn.sigmoid(gv[:, D:] + bm_ref[1:2, :])
        dpa_ref[...] = (dyv * sa).astype(BF16)
        dpb_ref[...] = (dyv * sb).astype(BF16)
        dga = dyv * pa_ref[...] * (sa * (1.0 - sa))
        dgb = dyv * pb_ref[...] * (sb * (1.0 - sb))
        dh_ref[:, :D] = dga.astype(BF16)
        dh_ref[:, D:] = dgb.astype(BF16)

        @pl.when(pl.program_id(0) == 0)
        def _():
            dbm_ref[...] = jnp.zeros_like(dbm_ref)

        dbm_ref[0:1, :] += jnp.sum(dga, axis=0, keepdims=True)
        dbm_ref[1:2, :] += jnp.sum(dgb, axis=0, keepdims=True)

    row = pl.BlockSpec((tm, D), lambda i: (i, 0))
    two = pl.BlockSpec((2, D), lambda i: (0, 0))
    gate = pl.BlockSpec((tm, 2 * D), lambda i: (i, 0))
    return pl.pallas_call(
        body, name=name, grid=(Lp // tm,), in_specs=[gate, row, row, two, row], out_specs=[row, row, gate, two],
        out_shape=[SDS((Lp, D), BF16), SDS((Lp, D), BF16), SDS((Lp, HP), BF16), SDS((2, D), F32)])(H, pa, pb, bm, dy)


def _gla_prep(name, H, w2p, b2p):
    Lp = H.shape[0]
    tm = _rt(Lp, 384)

    def body(qk_ref, v_ref, lr_ref, w_ref, b_ref, q_o, k_o, v_o, gf_o, gb_o):
        valid = _row_ids(pl.program_id(0), tm) >= PAD
        qk = qk_ref[...]
        vv = v_ref[...]
        pre = jnp.dot(lr_ref[...].astype(BF16), w_ref[...], preferred_element_type=F32) + b_ref[...]
        g = jnp.where(valid, jax.nn.log_sigmoid(pre) * (1.0 / GLA_TAU), 0.0)
        for hh in range(GLA_H):
            q_o[hh] = qk[:, 64 * hh:64 * hh + 64] * (GLA_DK ** -0.5)
            k_o[hh] = qk[:, 256 + 64 * hh:256 + 64 * hh + 64]
            v_o[hh] = vv[:, 128 * hh:128 * hh + 128].astype(BF16)
            gf_o[hh] = g[:, 64 * hh:64 * hh + 64]
            gb_o[hh] = g[:, 256 + 64 * hh:256 + 64 * hh + 64]

    h64 = pl.BlockSpec((GLA_H, tm, 64), lambda i: (0, i, 0))
    h128 = pl.BlockSpec((GLA_H, tm, 128), lambda i: (0, i, 0))
    return pl.pallas_call(
        body, name=name, grid=(Lp // tm,),
        in_specs=[pl.BlockSpec((tm, 512), lambda i: (i, C_QA // 512)), pl.BlockSpec((tm, 512), lambda i: (i, C_VA // 512)),
                  pl.BlockSpec((tm, LANE), lambda i: (i, C_LR // LANE)), pl.BlockSpec((LANE, 512), lambda i: (0, 0)),
                  pl.BlockSpec((1, 512), lambda i: (0, 0))],
        out_specs=[h64, h64, h128, h64, h64],
        out_shape=[SDS((GLA_H, Lp, 64), F32), SDS((GLA_H, Lp, 64), F32), SDS((GLA_H, Lp, 128), BF16),
                   SDS((GLA_H, Lp, 64), F32), SDS((GLA_H, Lp, 64), F32)])(H, H, H, w2p, b2p)


def _bdot(a, b, ca, cb, precision=None):
    return lax.dot_general(a, b, ((ca, cb), ((0,), (0,))), precision=precision, preferred_element_type=F32)


def _gla_chunk_terms(q_ref, k_ref, g_ref, v_ref, G, rev):
    B = GLA_H * G
    qv = q_ref[...].reshape(B, CHUNK, GLA_DK)
    kv = k_ref[...].reshape(B, CHUNK, GLA_DK)
    gv = g_ref[...].reshape(B, CHUNK, GLA_DK)
    vv = v_ref[...].reshape(B, CHUNK, GLA_DV)
    ii = lax.broadcasted_iota(jnp.int32, (CHUNK, CHUNK), 0)
    jj = lax.broadcasted_iota(jnp.int32, (CHUNK, CHUNK), 1)
    tri = (jj >= ii) if rev else (jj <= ii)
    tb = jnp.broadcast_to(tri.astype(F32)[None], (B, CHUNK, CHUNK))
    bc = _bdot(tb, gv, (2,), (1,), HIGHEST)
    bt = bc[:, 0:1, :] if rev else bc[:, CHUNK - 1:CHUNK, :]
    eq, eki, eke = jnp.exp(bc), jnp.exp(-bc), jnp.exp(bt - bc)
    qd, ki, ke = qv * eq, kv * eki, kv * eke
    att = jnp.where(tri[None], _bdot(qd.astype(BF16), ki.astype(BF16), (2,), (2,)), 0.0)
    dm = jnp.exp(_bdot(gv, jnp.ones((B, CHUNK, GLA_DV), F32), (1,), (1,), HIGHEST))
    return dict(B=B, vv=vv, tri=tri, tb=tb, bt=bt, eq=eq, eki=eki, eke=eke, qd=qd, ki=ki, ke=ke, att=att, dm=dm)


def _gla_fwd(name, q, k, g, v, rev, G):
    Lp = q.shape[1]
    tg = G * CHUNK
    ng = Lp // tg

    def body(q_ref, k_ref, g_ref, v_ref, o_ref, ss_ref, s_scr):
        @pl.when(pl.program_id(0) == 0)
        def _():
            s_scr[...] = jnp.zeros_like(s_scr)

        t = _gla_chunk_terms(q_ref, k_ref, g_ref, v_ref, G, rev)
        B, vv = t["B"], t["vv"]
        qd = t["qd"].astype(BF16)
        oi = _bdot(t["att"].astype(BF16), vv, (2,), (1,))
        kvc = _bdot(t["ke"].astype(BF16), vv, (1,), (1,)).reshape(GLA_H, G, GLA_DK, GLA_DV)
        dm = t["dm"].reshape(GLA_H, G, GLA_DK, GLA_DV)
        s = s_scr[...]
        sp = [None] * G
        for c in (range(G - 1, -1, -1) if rev else range(G)):
            sp[c] = s
            ss_ref[c] = s
            s = dm[:, c] * s + kvc[:, c]
        s_scr[...] = s
        spb = jnp.stack(sp, axis=1).reshape(B, GLA_DK, GLA_DV).astype(BF16)
        o_ref[...] = (oi + _bdot(qd, spb, (2,), (1,))).reshape(GLA_H, tg, GLA_DV)

    blk = (lambda i: (0, ng - 1 - i, 0)) if rev else (lambda i: (0, i, 0))
    sblk = (lambda i: (ng - 1 - i, 0, 0, 0)) if rev else (lambda i: (i, 0, 0, 0))
    h64 = pl.BlockSpec((GLA_H, tg, 64), blk)
    h128 = pl.BlockSpec((GLA_H, tg, 128), blk)
    return pl.pallas_call(
        body, name=name, grid=(ng,), in_specs=[h64, h64, h64, h128],
        out_specs=[h128, pl.BlockSpec((G, GLA_H, GLA_DK, GLA_DV), sblk)],
        out_shape=[SDS((GLA_H, Lp, GLA_DV), F32), SDS((Lp // CHUNK, GLA_H, GLA_DK, GLA_DV), F32)],
        scratch_shapes=[pltpu.VMEM((GLA_H, GLA_DK, GLA_DV), F32)], compiler_params=_cp(True))(q, k, g, v)


def _gla_bwd(name, q, k, g, v, ss, do, rev, G):
    Lp = q.shape[1]
    tg = G * CHUNK
    ng = Lp // tg

    def body(q_ref, k_ref, g_ref, v_ref, ss_ref, do_ref, dq_ref, dk_ref, dg_ref, dv_ref, ds_scr):
        @pl.when(pl.program_id(0) == 0)
        def _():
            ds_scr[...] = jnp.zeros_like(ds_scr)

        t = _gla_chunk_terms(q_ref, k_ref, g_ref, v_ref, G, rev)
        B, vv, tri = t["B"], t["vv"], t["tri"]
        qd, ki, ke = t["qd"], t["ki"], t["ke"]
        qdb, kib, keb = qd.astype(BF16), ki.astype(BF16), ke.astype(BF16)
        sp = jnp.stack([ss_ref[c] for c in range(G)], axis=1).reshape(B, GLA_DK, GLA_DV)
        dob = do_ref[...].reshape(B, CHUNK, GLA_DV).astype(BF16)
        da = jnp.where(tri[None], _bdot(dob, vv, (2,), (2,)), 0.0).astype(BF16)
        dqd = _bdot(da, kib, (2,), (1,)) + _bdot(dob, sp.astype(BF16), (2,), (2,))
        dki = _bdot(da, qdb, (1,), (1,))
        dv = _bdot(t["att"].astype(BF16), dob, (1,), (1,))
        cc = _bdot(qdb, dob, (1,), (1,)).reshape(GLA_H, G, GLA_DK, GLA_DV)
        dm = t["dm"].reshape(GLA_H, G, GLA_DK, GLA_DV)
        dsc = ds_scr[...]
        dsn = [None] * G
        for c in (range(G) if rev else range(G - 1, -1, -1)):
            dsn[c] = dsc
            dsc = dm[:, c] * dsc + cc[:, c]
        ds_scr[...] = dsc
        dsn = jnp.stack(dsn, axis=1).reshape(B, GLA_DK, GLA_DV)
        dsnb = dsn.astype(BF16)
        dv = dv + _bdot(keb, dsnb, (2,), (1,))
        dke = _bdot(vv, dsnb, (2,), (2,))
        ddrow = _bdot(jnp.ones((B, CHUNK, GLA_DV), F32), dsn * sp, (2,), (2,), HIGHEST)
        dbt = ddrow * jnp.exp(t["bt"]) + jnp.sum(dke * ke, axis=1, keepdims=True)
        db = dqd * qd - dki * ki - dke * ke
        dq_ref[...] = (dqd * t["eq"]).reshape(GLA_H, tg, GLA_DK)
        dk_ref[...] = (dki * t["eki"] + dke * t["eke"]).reshape(GLA_H, tg, GLA_DK)
        dg_ref[...] = (_bdot(t["tb"], db, (1,), (1,), HIGHEST) + dbt).reshape(GLA_H, tg, GLA_DK)
        dv_ref[...] = dv.reshape(GLA_H, tg, GLA_DV)

    blk = (lambda i: (0, i, 0)) if rev else (lambda i: (0, ng - 1 - i, 0))
    sblk = (lambda i: (i, 0, 0, 0)) if rev else (lambda i: (ng - 1 - i, 0, 0, 0))
    h64 = pl.BlockSpec((GLA_H, tg, 64), blk)
    h128 = pl.BlockSpec((GLA_H, tg, 128), blk)
    return pl.pallas_call(
        body, name=name, grid=(ng,),
        in_specs=[h64, h64, h64, h128, pl.BlockSpec((G, GLA_H, GLA_DK, GLA_DV), sblk), h128],
        out_specs=[h64, h64, h64, h128],
        out_shape=[SDS((GLA_H, Lp, 64), F32), SDS((GLA_H, Lp, 64), F32), SDS((GLA_H, Lp, 64), F32),
                   SDS((GLA_H, Lp, GLA_DV), F32)],
        scratch_shapes=[pltpu.VMEM((GLA_H, GLA_DK, GLA_DV), F32)], compiler_params=_cp(True))(q, k, g, v, ss, do)


def _gla_post(name, of, ob, H, gn):
    Lp = H.shape[0]
    tm = _rt(Lp, 384)

    def body(of_ref, ob_ref, r_ref, gn_ref, a_ref, at_ref):
        parts = []
        for hh in range(GLA_H):
            o = of_ref[hh] + ob_ref[hh]
            parts.append(o * lax.rsqrt(jnp.mean(o * o, axis=-1, keepdims=True) + EPS))
        rv = r_ref[...]
        a = (jnp.concatenate(parts, axis=1) * gn_ref[...]) * (rv * jax.nn.sigmoid(rv))
        a_ref[...] = a.astype(BF16)
        at_ref[...] = a.T.astype(BF16)

    h128 = pl.BlockSpec((GLA_H, tm, 128), lambda i: (0, i, 0))
    return pl.pallas_call(
        body, name=name, grid=(Lp // tm,),
        in_specs=[h128, h128, pl.BlockSpec((tm, 512), lambda i: (i, C_RA // 512)), pl.BlockSpec((1, 512), lambda i: (0, 0))],
        out_specs=[pl.BlockSpec((tm, 512), lambda i: (i, 0)), pl.BlockSpec((512, tm), lambda i: (0, i))],
        out_shape=[SDS((Lp, 512), BF16), SDS((512, Lp), BF16)])(of, ob, H, gn)


def _gla_post_bwd(name, of, ob, H, gn, da, dH):
    Lp = H.shape[0]
    tm = _rt(Lp, 384)

    def body(of_ref, ob_ref, r_ref, gn_ref, da_ref, dh_in, do_ref, dh_ref, dgn_ref):
        rv = r_ref[...]
        sg = jax.nn.sigmoid(rv)
        dav = da_ref[...]
        gnv = gn_ref[...]
        ons, rs = [], []
        for hh in range(GLA_H):
            o = of_ref[hh] + ob_ref[hh]
            r = lax.rsqrt(jnp.mean(o * o, axis=-1, keepdims=True) + EPS)
            rs.append(r)
            ons.append(o * r)
        on = jnp.concatenate(ons, axis=1)
        dw = dav * (rv * sg)
        dh_ref[...] = (dav * (on * gnv) * (sg * (1.0 + rv * (1.0 - sg)))).astype(BF16)

        @pl.when(pl.program_id(0) == 0)
        def _():
            dgn_ref[...] = jnp.zeros_like(dgn_ref)

        dgn_ref[...] += jnp.sum(dw * on, axis=0, keepdims=True)
        don = dw * gnv
        for hh in range(GLA_H):
            dd = don[:, 128 * hh:128 * hh + 128]
            do_ref[hh] = rs[hh] * (dd - ons[hh] * jnp.mean(dd * ons[hh], axis=-1, keepdims=True))

    h128 = pl.BlockSpec((GLA_H, tm, 128), lambda i: (0, i, 0))
    rblk = pl.BlockSpec((tm, 512), lambda i: (i, C_RA // 512))
    vec = pl.BlockSpec((1, 512), lambda i: (0, 0))
    return pl.pallas_call(
        body, name=name, grid=(Lp // tm,),
        in_specs=[h128, h128, rblk, vec, pl.BlockSpec((tm, 512), lambda i: (i, 0)), pl.BlockSpec(memory_space=pl.ANY)],
        out_specs=[h128, rblk, vec],
        out_shape=[SDS((GLA_H, Lp, 128), F32), SDS(dH.shape, BF16), SDS((1, 512), F32)],
        input_output_aliases={5: 1})(of, ob, H, gn, da, dH)


def _gla_qkv_bwd(name, dqf, dqb, dkf, dkb, dvf, dvb, dH):
    Lp = dqf.shape[1]
    tm = _rt(Lp, 384)

    def body(dqf_ref, dqb_ref, dkf_ref, dkb_ref, dvf_ref, dvb_ref, dh_in, dh_ref):
        valid = _row_ids(pl.program_id(0), tm) >= PAD
        for hh in range(GLA_H):
            dq = (dqf_ref[hh] + dqb_ref[hh]) * (GLA_DK ** -0.5)
            dh_ref[:, 64 * hh:64 * hh + 64] = jnp.where(valid, dq, 0.0).astype(BF16)
            dh_ref[:, 256 + 64 * hh:256 + 64 * hh + 64] = jnp.where(valid, dkf_ref[hh] + dkb_ref[hh], 0.0).astype(BF16)
            dh_ref[:, 512 + 128 * hh:512 + 128 * hh + 128] = jnp.where(valid, dvf_ref[hh] + dvb_ref[hh], 0.0).astype(BF16)

    h64 = pl.BlockSpec((GLA_H, tm, 64), lambda i: (0, i, 0))
    h128 = pl.BlockSpec((GLA_H, tm, 128), lambda i: (0, i, 0))
    return pl.pallas_call(
        body, name=name, grid=(Lp // tm,),
        in_specs=[h64, h64, h64, h64, h128, h128, pl.BlockSpec(memory_space=pl.ANY)],
        out_specs=pl.BlockSpec((tm, 1024), lambda i: (i, C_QA // 1024)), out_shape=SDS(dH.shape, BF16),
        input_output_aliases={6: 0})(dqf, dqb, dkf, dkb, dvf, dvb, dH)


def _gla_gate_bwd(name, H, w2p, b2p, dgf, dgb, dH):
    Lp = H.shape[0]
    tm = _rt(Lp, 384)

    def body(lr_ref, w_ref, b_ref, dgf_ref, dgb_ref, dh_in, dh_ref, dw_ref, db_ref, dg_scr):
        valid = _row_ids(pl.program_id(0), tm) >= PAD
        for hh in range(GLA_H):
            dg_scr[:, 64 * hh:64 * hh + 64] = dgf_ref[hh]
            dg_scr[:, 256 + 64 * hh:256 + 64 * hh + 64] = dgb_ref[hh]
        lrb = lr_ref[...].astype(BF16)
        wv = w_ref[...]
        pre = jnp.dot(lrb, wv, preferred_element_type=F32) + b_ref[...]
        dpre = jnp.where(valid, dg_scr[...] * (1.0 / GLA_TAU) * jax.nn.sigmoid(-pre), 0.0)
        dpb = dpre.astype(BF16)
        dh_ref[...] = lax.dot_general(dpb, wv, NT, preferred_element_type=F32).astype(BF16)

        @pl.when(pl.program_id(0) == 0)
        def _():
            dw_ref[...] = jnp.zeros_like(dw_ref)
            db_ref[...] = jnp.zeros_like(db_ref)

        dw_ref[...] += lax.dot_general(lrb, dpb, TN, preferred_element_type=F32)
        db_ref[...] += jnp.sum(dpre, axis=0, keepdims=True)

    h64 = pl.BlockSpec((GLA_H, tm, 64), lambda i: (0, i, 0))
    lrblk = pl.BlockSpec((tm, LANE), lambda i: (i, C_LR // LANE))
    wblk = pl.BlockSpec((LANE, 512), lambda i: (0, 0))
    vec = pl.BlockSpec((1, 512), lambda i: (0, 0))
    return pl.pallas_call(
        body, name=name, grid=(Lp // tm,),
        in_specs=[lrblk, wblk, vec, h64, h64, pl.BlockSpec(memory_space=pl.ANY)],
        out_specs=[lrblk, wblk, vec],
        out_shape=[SDS(dH.shape, BF16), SDS((LANE, 512), F32), SDS((1, 512), F32)],
        scratch_shapes=[pltpu.VMEM((tm, 512), F32)], input_output_aliases={5: 0})(H, w2p, b2p, dgf, dgb, dH)


def _swap16(x):
    n = x.shape[1]
    lane = lax.broadcasted_iota(jnp.int32, x.shape, 1)
    return jnp.where(lane % 32 < 16, pltpu.roll(x, n - 16, 1), pltpu.roll(x, 16, 1))


def _headnorm_rope(x, gain, cos, sin, bd):
    r = lax.rsqrt(jnp.dot(x * x, bd, precision=HIGHEST, preferred_element_type=F32) + EPS)
    xh = x * r
    xn = xh * gain
    return xn * cos + _swap16(xn) * sin, xh, r


def _headnorm_rope_bwd(dxr, xh, r, gain, cos, sin, bd):
    dxn = cos * dxr + _swap16(sin * dxr)
    u = dxn * gain
    dx = r * (u - xh * jnp.dot(u * xh, bd, precision=HIGHEST, preferred_element_type=F32))
    return dx, jnp.sum(dxn * xh, axis=0, keepdims=True)


def _att_prep(name, H, gq, gk, cos, sin, bd):
    Lp = H.shape[0]
    tm = _rt(Lp, 384)

    def body(q_ref, kv_ref, gq_ref, gk_ref, c_ref, s_ref, bd_ref, qp_ref, k_ref, v_ref):
        c1, s1 = c_ref[...], s_ref[...]
        c4, s4 = jnp.concatenate([c1] * 4, axis=1), jnp.concatenate([s1] * 4, axis=1)
        xr, _, _ = _headnorm_rope(q_ref[...], gq_ref[...], c4, s4, bd_ref[...])
        xr = xr * (HD ** -0.5)
        lane = lax.broadcasted_iota(jnp.int32, (tm, LANE), 1)
        for hh in range(ATT_H):
            grp = xr[:, LANE * (hh // 2):LANE * (hh // 2) + LANE]
            e, gi = hh % 2, hh // 4
            if e != gi:
                grp = pltpu.roll(grp, 64, 1)
            keep = (lane < 64) if gi == 0 else (lane >= 64)
            qp_ref[hh] = jnp.where(keep, grp, 0.0).astype(BF16)
        kv = kv_ref[...]
        kr, _, _ = _headnorm_rope(kv[:, :LANE], gk_ref[...], c1, s1, bd_ref[0:LANE, 0:LANE])
        k_ref[...] = kr.astype(BF16)
        v_ref[...] = kv[:, LANE:].astype(BF16)

    row128 = pl.BlockSpec((tm, LANE), lambda i: (i, 0))
    return pl.pallas_call(
        body, name=name, grid=(Lp // tm,),
        in_specs=[pl.BlockSpec((tm, 512), lambda i: (i, C_QB // 512)), pl.BlockSpec((tm, 256), lambda i: (i, C_KB // 256)),
                  pl.BlockSpec((1, 512), lambda i: (0, 0)), pl.BlockSpec((1, LANE), lambda i: (0, 0)), row128, row128,
                  pl.BlockSpec((512, 512), lambda i: (0, 0))],
        out_specs=[pl.BlockSpec((ATT_H, tm, LANE), lambda i: (0, i, 0)), row128, row128],
        out_shape=[SDS((ATT_H, Lp, LANE), BF16), SDS((Lp, LANE), BF16), SDS((Lp, LANE), BF16)])(H, H, gq, gk, cos, sin, bd)


def _att_q_bwd(name, H, gq, cos, sin, bd, dqp, dH):
    Lp = H.shape[0]
    tm = _rt(Lp, 384)

    def body(q_ref, gq_ref, c_ref, s_ref, bd_ref, dqp_ref, dh_in, dh_ref, dg_ref):
        c1, s1 = c_ref[...], s_ref[...]
        c4, s4 = jnp.concatenate([c1] * 4, axis=1), jnp.concatenate([s1] * 4, axis=1)
        gqv = gq_ref[...]
        _, xh, r = _headnorm_rope(q_ref[...], gqv, c4, s4, bd_ref[...])
        lane = lax.broadcasted_iota(jnp.int32, (tm, LANE), 1)
        groups = []
        for j in range(ATT_H // 2):
            pieces = []
            for e in range(2):
                hh = 2 * j + e
                piece = dqp_ref[hh]
                if e != hh // 4:
                    piece = pltpu.roll(piece, 64, 1)
                pieces.append(piece)
            groups.append(jnp.where(lane < 64, pieces[0], pieces[1]))
        dxr = jnp.concatenate(groups, axis=1) * (HD ** -0.5)
        dx, dg = _headnorm_rope_bwd(dxr, xh, r, gqv, c4, s4, bd_ref[...])
        dh_ref[...] = dx.astype(BF16)

        @pl.when(pl.program_id(0) == 0)
        def _():
            dg_ref[...] = jnp.zeros_like(dg_ref)

        dg_ref[...] += dg

    row128 = pl.BlockSpec((tm, LANE), lambda i: (i, 0))
    qblk = pl.BlockSpec((tm, 512), lambda i: (i, C_QB // 512))
    vec = pl.BlockSpec((1, 512), lambda i: (0, 0))
    return pl.pallas_call(
        body, name=name, grid=(Lp // tm,),
        in_specs=[qblk, vec, row128, row128, pl.BlockSpec((512, 512), lambda i: (0, 0)),
                  pl.BlockSpec((ATT_H, tm, LANE), lambda i: (0, i, 0)), pl.BlockSpec(memory_space=pl.ANY)],
        out_specs=[qblk, vec], out_shape=[SDS(dH.shape, BF16), SDS((1, 512), F32)],
        input_output_aliases={6: 0})(H, gq, cos, sin, bd, dqp, dH)


def _att_kv_bwd(name, H, gk, cos, sin, bd, dkr, dvb, dH):
    Lp = H.shape[0]
    tm = _rt(Lp, 384)

    def body(kv_ref, gk_ref, c_ref, s_ref, bd_ref, dk_ref, dv_ref, dh_in, dh_ref, dg_ref):
        c1, s1 = c_ref[...], s_ref[...]
        gkv = gk_ref[...]
        bdv = bd_ref[0:LANE, 0:LANE]
        _, xh, r = _headnorm_rope(kv_ref[:, :LANE], gkv, c1, s1, bdv)
        dx, dg = _headnorm_rope_bwd(dk_ref[...], xh, r, gkv, c1, s1, bdv)
        dh_ref[:, :LANE] = dx.astype(BF16)
        dh_ref[:, LANE:] = dv_ref[...].astype(BF16)

        @pl.when(pl.program_id(0) == 0)
        def _():
            dg_ref[...] = jnp.zeros_like(dg_ref)

        dg_ref[...] += dg

    row128 = pl.BlockSpec((tm, LANE), lambda i: (i, 0))
    kvblk = pl.BlockSpec((tm, 256), lambda i: (i, C_KB // 256))
    vec = pl.BlockSpec((1, LANE), lambda i: (0, 0))
    return pl.pallas_call(
        body, name=name, grid=(Lp // tm,),
        in_specs=[kvblk, vec, row128, row128, pl.BlockSpec((512, 512), lambda i: (0, 0)), row128, row128,
                  pl.BlockSpec(memory_space=pl.ANY)],
        out_specs=[kvblk, vec], out_shape=[SDS(dH.shape, BF16), SDS((1, LANE), F32)],
        input_output_aliases={7: 0})(H, gk, cos, sin, bd, dkr, dvb, dH)


def _ridden_call(core, rider, *, name, grid, in_specs, out_specs, out_shape, scratch_shapes, args):
    n_in, n_out, n_scr = len(in_specs), len(out_specs), len(scratch_shapes)
    r_in = len(rider.ins) if rider else 0
    r_out = len(rider.out_shape) if rider else 0
    total = int(np.prod(grid))
    mid_step = max(total - 1 - RIDER_MID_BACK, 0)

    def body(*refs):
        ins, r_ins = refs[:n_in], refs[n_in:n_in + r_in]
        o0 = n_in + r_in
        outs, r_outs = refs[o0:o0 + n_out], refs[o0 + n_out:o0 + n_out + r_out]
        s0 = o0 + n_out + r_out
        scr, r_sems = refs[s0:s0 + n_scr], refs[s0 + n_scr:]
        if rider is None:
            core(ins, outs, scr)
            return
        step = pl.program_id(0)
        for ax in range(1, len(grid)):
            step = step * grid[ax] + pl.program_id(ax)
        start, mid, end = rider.hooks(r_ins, r_outs, r_sems)
        pl.when(step == 0)(start)
        core(ins, outs, scr)
        pl.when(step == mid_step)(mid)
        pl.when(step == total - 1)(end)

    any_spec = pl.BlockSpec(memory_space=pl.ANY)
    res = pl.pallas_call(
        body, name=name, grid=grid, in_specs=list(in_specs) + [any_spec] * r_in, out_specs=list(out_specs) + [any_spec] * r_out,
        out_shape=list(out_shape) + (list(rider.out_shape) if rider else []),
        scratch_shapes=list(scratch_shapes) + (list(rider.sems) if rider else []),
        input_output_aliases={n_in + a: n_out + b for a, b in rider.aliases.items()} if rider else {},
        compiler_params=_cp(True))(*args, *(rider.ins if rider else []))
    return res[:n_out], res[n_out:]


def _att_fwd(name, qp, kr, vb, bias, rider=None):
    Lp = kr.shape[0]
    tq = _rt(Lp, 384)
    sub = ATT_SUB if tq % ATT_SUB == 0 else tq

    def core(ins, outs, scr):
        q_ref, k_ref, v_ref, bias_ref = ins
        b_ref, bt_ref, lse_ref = outs
        j, i = pl.program_id(0), pl.program_id(1)
        valid = _row_ids(i, tq) >= PAD
        kk, vv, bb = k_ref[...], v_ref[...], bias_ref[...]
        outs = []
        for e in range(2):
            parts = []
            for r in range(tq // sub):
                rs = slice(r * sub, (r + 1) * sub)
                s = lax.dot_general(q_ref[e, rs, :], kk, NT, preferred_element_type=F32) + bb
                m = jnp.max(s, axis=-1, keepdims=True)
                p = jnp.exp(s - m)
                l = jnp.sum(p, axis=-1, keepdims=True)
                parts.append(jnp.dot(p.astype(BF16), vv, preferred_element_type=F32) * (1.0 / l))
                lse_ref[e, rs, :] = m + jnp.log(l)
            outs.append(jnp.where(valid, jnp.concatenate(parts, axis=0), 0.0))
        lane = lax.broadcasted_iota(jnp.int32, (tq, LANE), 1)
        low = j < 2
        o0 = jnp.where(low, outs[0], pltpu.roll(outs[0], 64, 1))
        o1 = jnp.where(low, pltpu.roll(outs[1], 64, 1), outs[1])
        blk = jnp.where(lane < 64, o0, o1)
        b_ref[...] = blk.astype(BF16)
        bt_ref[...] = blk.T.astype(BF16)

    full = pl.BlockSpec((Lp, LANE), lambda j, i: (0, 0))
    return _ridden_call(
        core, rider, name=name, grid=(ATT_H // 2, Lp // tq),
        in_specs=[pl.BlockSpec((2, tq, LANE), lambda j, i: (j, i, 0)), full, full, pl.BlockSpec((1, Lp), lambda j, i: (0, 0))],
        out_specs=[pl.BlockSpec((tq, LANE), lambda j, i: (i, j)), pl.BlockSpec((LANE, tq), lambda j, i: (j, i)),
                   pl.BlockSpec((2, tq, 1), lambda j, i: (j, i, 0))],
        out_shape=[SDS((Lp, 512), BF16), SDS((512, Lp), BF16), SDS((ATT_H, Lp, 1), F32)],
        scratch_shapes=[], args=(qp, kr, vb, bias))


def _att_bwd(name, qp, kr, vb, bias, lse, db, b, rider=None):
    Lp = kr.shape[0]
    tq = _rt(Lp, 384)
    sub = ATT_SUB if tq % ATT_SUB == 0 else tq
    nq = Lp // tq

    def core(ins, outs, scr):
        q_ref, k_ref, v_ref, bias_ref, lse_ref, db_ref, b_ref = ins
        dq_ref, dk_ref, dv_ref = outs
        dkt_scr, dvt_scr = scr
        j, i = pl.program_id(0), pl.program_id(1)

        @pl.when((j == 0) & (i == 0))
        def _():
            dkt_scr[...] = jnp.zeros_like(dkt_scr)
            dvt_scr[...] = jnp.zeros_like(dvt_scr)

        kk, vv, bb = k_ref[...], v_ref[...], bias_ref[...]
        dbv = db_ref[...]
        rolled = pltpu.roll(dbv, 64, 1)
        lane = lax.broadcasted_iota(jnp.int32, (tq, LANE), 1)
        low = j < 2
        first = jnp.where(low, 0, 64)
        keep = (lane >= first) & (lane < first + 64)
        prod = dbv * b_ref[...].astype(F32)
        deltas = [jnp.sum(jnp.where(lane < 64, prod, 0.0), axis=-1, keepdims=True),
                  jnp.sum(jnp.where(lane >= 64, prod, 0.0), axis=-1, keepdims=True)]
        dkt, dvt = 0.0, 0.0
        for e in range(2):
            src = jnp.where(low, dbv, rolled) if e == 0 else jnp.where(low, rolled, dbv)
            dop = jnp.where(keep, src, 0.0).astype(BF16)
            pbs, dss = [], []
            for r in range(tq // sub):
                rs = slice(r * sub, (r + 1) * sub)
                s = lax.dot_general(q_ref[e, rs, :], kk, NT, preferred_element_type=F32) + bb
                p = jnp.exp(s - lse_ref[e, rs, :])
                dp = lax.dot_general(dop[rs], vv, NT, preferred_element_type=F32)
                ds = (p * (dp - deltas[e][rs])).astype(BF16)
                dq_ref[e, rs, :] = jnp.where(keep[rs], jnp.dot(ds, kk, preferred_element_type=F32), 0.0)
                pbs.append(p.astype(BF16))
                dss.append(ds)
            dkt = dkt + lax.dot_general(q_ref[e], jnp.concatenate(dss, axis=0), TN, preferred_element_type=F32)
            dvt = dvt + lax.dot_general(dop, jnp.concatenate(pbs, axis=0), TN, preferred_element_type=F32)
        dkt_scr[...] += dkt
        dvt_scr[...] += dvt

        @pl.when((j == ATT_H // 2 - 1) & (i == nq - 1))
        def _():
            dk_ref[...] = dkt_scr[...].T
            dv_ref[...] = dvt_scr[...].T

    full = pl.BlockSpec((Lp, LANE), lambda j, i: (0, 0))
    pair = pl.BlockSpec((2, tq, LANE), lambda j, i: (j, i, 0))
    return _ridden_call(
        core, rider, name=name, grid=(ATT_H // 2, nq),
        in_specs=[pair, full, full, pl.BlockSpec((1, Lp), lambda j, i: (0, 0)), pl.BlockSpec((2, tq, 1), lambda j, i: (j, i, 0)),
                  pl.BlockSpec((tq, LANE), lambda j, i: (i, j)), pl.BlockSpec((tq, LANE), lambda j, i: (i, j))],
        out_specs=[pair, full, full],
        out_shape=[SDS((ATT_H, Lp, LANE), F32), SDS((Lp, LANE), F32), SDS((Lp, LANE), F32)],
        scratch_shapes=[pltpu.VMEM((LANE, Lp), F32), pltpu.VMEM((LANE, Lp), F32)], args=(qp, kr, vb, bias, lse, db, b))


def _rope_tables(n_tok):
    rows = n_tok // GRID_W
    row = jnp.repeat(jnp.arange(rows), GRID_W).astype(F32)
    col = jnp.tile(jnp.arange(GRID_W), rows).astype(F32)
    axis_dim = HD // 2
    inv = ROPE_THETA ** (-jnp.arange(0, axis_dim, 2, dtype=F32) / axis_dim)
    ang = jnp.concatenate([row[:, None] * inv, col[:, None] * inv], axis=-1)
    ang = jnp.concatenate([jnp.zeros((LANE, axis_dim), F32), ang], axis=0)
    c, s = jnp.cos(ang), jnp.sin(ang)
    c64 = jnp.concatenate([c[:, :16], c[:, :16], c[:, 16:], c[:, 16:]], axis=1)
    s64 = jnp.concatenate([-s[:, :16], s[:, :16], -s[:, 16:], s[:, 16:]], axis=1)
    return jnp.concatenate([c64, c64], axis=1), jnp.concatenate([s64, s64], axis=1)


def _ffn_fwd(tag, h, W, l, j):
    Lp = h.shape[0]
    tm = _rt(Lp, 384)
    g = W["norm_gains"][l, 2 * j].reshape(1, D)

    def body(h_ref, g_ref, wg_ref, wu_ref, wd_ref, h2_ref, a_ref, b_ref, st_ref, nt_ref, pad_scr):
        x = h_ref[...]
        y = x * lax.rsqrt(jnp.mean(x * x, axis=-1, keepdims=True) + EPS) * g_ref[...]
        nt_ref[...] = y.T.astype(BF16)
        nv = y.astype(BF16)
        pad_scr[:, FF_S:] = jnp.zeros((tm, FF_P - FF_S), F32)
        acc = 0.0
        for s in range(NSH):
            a = jnp.dot(nv, wg_ref[s], preferred_element_type=F32)
            b = jnp.dot(nv, wu_ref[s], preferred_element_type=F32)
            sv = a * jax.nn.sigmoid(a) * b
            a_ref[s, :, :FF_S] = a.astype(BF16)
            a_ref[s, :, FF_S:] = jnp.zeros((tm, FF_P - FF_S), BF16)
            b_ref[s, :, :FF_S] = b.astype(BF16)
            b_ref[s, :, FF_S:] = jnp.zeros((tm, FF_P - FF_S), BF16)
            pad_scr[:, :FF_S] = sv
            st_ref[s] = pad_scr[...].T.astype(BF16)
            acc = acc + jnp.dot(sv.astype(BF16), wd_ref[s], preferred_element_type=F32)
        h2_ref[...] = x + 0.5 * acc

    once = dict(pipeline_mode=pl.Buffered(1))
    wup = pl.BlockSpec((NSH, None, D, FF_S), lambda i: (0, l, j, 0), **once)
    row = pl.BlockSpec((tm, D), lambda i: (i, 0))
    slab = pl.BlockSpec((NSH, tm, FF_P), lambda i: (0, i, 0))
    h2, a, b, st, nt = pl.pallas_call(
        body, name=f"ffn_{tag}", grid=(Lp // tm,),
        in_specs=[row, pl.BlockSpec((1, D), lambda i: (0, 0)), wup, wup,
                  pl.BlockSpec((NSH, None, FF_S, D), lambda i: (0, l, j, 0), **once)],
        out_specs=[row, slab, slab, pl.BlockSpec((NSH, FF_P, tm), lambda i: (0, 0, i)), pl.BlockSpec((D, tm), lambda i: (0, i))],
        out_shape=[SDS((Lp, D), F32), SDS((NSH, Lp, FF_P), BF16), SDS((NSH, Lp, FF_P), BF16), SDS((NSH, FF_P, Lp), BF16),
                   SDS((D, Lp), BF16)],
        scratch_shapes=[pltpu.VMEM((tm, FF_P), F32)], compiler_params=_cp(True))(h, g, W["gate"], W["up"], W["down"])
    return h2, dict(h=h, g=g, nt=nt, a=a, b=b, st=st)


def _ffn_bwd(tag, dh, sv, W, G, l, j, rider=None):
    Lp = dh.shape[0]
    tn = 512
    (da, db, dh_prev, dg), carried = _ffn_bwd_acts(tag, dh, sv, W, l, j, rider)
    G["down"] = _mm(f"bwd_{tag}", sv["st"], dh, grid=(NSH, D // tn),
                    a_spec=pl.BlockSpec((None, FF_P, Lp), lambda s, n: (s, 0, 0)), a_sl=(slice(0, FF_S), slice(None)),
                    b_spec=pl.BlockSpec((Lp, tn), lambda s, n: (0, n)),
                    o_spec=pl.BlockSpec((None, None, FF_S, tn), lambda s, n: (s, j, 0, n)), o_shape=(NSH, 2, FF_S, D), o_dtype=F32,
                    dims=NN, acc_shape=None, scale=0.5, into=G.get("down"))
    for key, dact in (("gate", da), ("up", db)):
        G[key] = _mm(f"bw{key[0]}_{tag}", sv["nt"], dact, grid=(NSH, D // tn),
                     a_spec=pl.BlockSpec((tn, Lp), lambda s, m: (m, 0)),
                     b_spec=pl.BlockSpec((None, Lp, FF_P), lambda s, m: (s, 0, 0)), b_sl=(slice(None), slice(0, FF_S)),
                     o_spec=pl.BlockSpec((None, None, tn, FF_S), lambda s, m: (s, j, m, 0)), o_shape=(NSH, 2, D, FF_S),
                     o_dtype=F32, dims=NN, acc_shape=None, into=G.get(key))
    return dh_prev, dg, carried


def _ffn_bwd_acts(tag, dh, sv, W, l, j, rider=None):
    Lp = dh.shape[0]
    tm = _rt(Lp, 384)

    def core(ins, outs, scr):
        dh_ref, h_ref, g_ref, wd_ref, wg_ref, wu_ref, a_ref, b_ref = ins
        da_ref, db_ref, dho_ref, dg_ref = outs
        dhv = dh_ref[...]
        dhb = dhv.astype(BF16)
        dn = 0.0
        for s in range(NSH):
            ds = 0.5 * lax.dot_general(dhb, wd_ref[s], NT, preferred_element_type=F32)
            av = a_ref[s, :, :FF_S].astype(F32)
            bv = b_ref[s, :, :FF_S].astype(F32)
            sg = jax.nn.sigmoid(av)
            da = (ds * bv * (sg * (1.0 + av * (1.0 - sg)))).astype(BF16)
            db = (ds * (av * sg)).astype(BF16)
            da_ref[s, :, :FF_S] = da
            da_ref[s, :, FF_S:] = jnp.zeros((tm, FF_P - FF_S), BF16)
            db_ref[s, :, :FF_S] = db
            db_ref[s, :, FF_S:] = jnp.zeros((tm, FF_P - FF_S), BF16)
            dn = dn + (lax.dot_general(da, wg_ref[s], NT, preferred_element_type=F32)
                       + lax.dot_general(db, wu_ref[s], NT, preferred_element_type=F32))
        x = h_ref[...]
        r = lax.rsqrt(jnp.mean(x * x, axis=-1, keepdims=True) + EPS)
        xh = x * r
        u = dn * g_ref[...]
        dho_ref[...] = dhv + r * (u - xh * jnp.mean(u * xh, axis=-1, keepdims=True))

        @pl.when(pl.program_id(0) == 0)
        def _():
            dg_ref[...] = jnp.zeros_like(dg_ref)

        dg_ref[...] += jnp.sum(dn * xh, axis=0, keepdims=True)

    once = dict(pipeline_mode=pl.Buffered(1))
    wup = pl.BlockSpec((NSH, None, D, FF_S), lambda i: (0, l, j, 0), **once)
    row = pl.BlockSpec((tm, D), lambda i: (i, 0))
    slab = pl.BlockSpec((NSH, tm, FF_P), lambda i: (0, i, 0))
    vec = pl.BlockSpec((1, D), lambda i: (0, 0))
    return _ridden_call(
        core, rider, name=f"bffn_{tag}", grid=(Lp // tm,),
        in_specs=[row, row, vec, pl.BlockSpec((NSH, None, FF_S, D), lambda i: (0, l, j, 0), **once), wup, wup, slab, slab],
        out_specs=[slab, slab, row, vec],
        out_shape=[SDS((NSH, Lp, FF_P), BF16), SDS((NSH, Lp, FF_P), BF16), SDS((Lp, D), F32), SDS((1, D), F32)],
        scratch_shapes=[], args=(dh, sv["h"], sv["g"], W["down"], W["gate"], W["up"], sv["a"], sv["b"]))


def _mixer_fwd(tag, h, W, winp, C, l, G_chunks, rider=None):
    Lp = h.shape[0]
    tm = _rt(Lp, 1408)
    g = W["norm_gains"][l, 1].reshape(1, D)
    z, zt = _rms_fwd(f"rmsf_{tag}", h, g)
    tn = 896
    H = _mm(f"win_{tag}", z, winp, grid=(HP // tn, Lp // tm), a_spec=pl.BlockSpec((tm, D), lambda n, i: (i, 0)),
            b_spec=pl.BlockSpec((D, tn), lambda n, i: (0, n)), o_spec=pl.BlockSpec((tm, tn), lambda n, i: (i, n)),
            o_shape=(Lp, HP), o_dtype=F32, dims=NN, acc_shape=None)
    w2p, b2p = C["w2p"][l], C["b2p"][l]
    qh, kh, vh, gf, gb = _gla_prep(f"glap_{tag}", H, w2p, b2p)
    of, sf = _gla_fwd(f"glaf_{tag}", qh, kh, gf, vh, False, G_chunks)
    ob, sb = _gla_fwd(f"glar_{tag}", qh, kh, gb, vh, True, G_chunks)
    gn = W["gn"][l].reshape(1, 512)
    a, at = _gla_post(f"glao_{tag}", of, ob, H, gn)
    gq, gk = C["gq"][l], C["gk"][l]
    qp, kr, vb = _att_prep(f"attp_{tag}", H, gq, gk, C["cos"], C["sin"], C["bd"])
    (b, bt, lse), carried = _att_fwd(f"attf_{tag}", qp, kr, vb, C["bias"], rider)
    if rider is not None:
        W = rider.updated(W, carried)
    row = pl.BlockSpec((tm, D), lambda i: (i, 0))
    proj = functools.partial(
        _mm, grid=(Lp // tm,), a_spec=pl.BlockSpec((tm, 512), lambda i: (i, 0)),
        b_spec=pl.BlockSpec((NSH, None, 512, 256), lambda i: (0, l, 0, 0)), o_spec=row, o_shape=(Lp, D), o_dtype=F32, dims=NN,
        acc_shape=None, shards="cols")
    pa = proj(f"pa_{tag}", a, W["wpa"])
    pb = proj(f"pb_{tag}", b, W["wpb"])
    bm = W["bm"][l]
    y, yt = _merge_fwd(f"mrg_{tag}", H, pa, pb, bm)
    h2 = _mm(f"wout_{tag}", y, W["wout"], grid=(Lp // tm,), a_spec=row,
             b_spec=pl.BlockSpec((NSH, None, 256, D), lambda i: (0, l, 0, 0)), o_spec=row, o_shape=(Lp, D), o_dtype=F32,
             dims=NN, acc_shape=None, res=h, res_spec=row, shards="rows")
    sv = dict(h=h, g=g, zt=zt, H=H, w2p=w2p, b2p=b2p, qh=qh, kh=kh, vh=vh, gf=gf, gb=gb, of=of, ob=ob, sf=sf, sb=sb, gn=gn,
              at=at, gq=gq, gk=gk, qp=qp, kr=kr, vb=vb, b=b, bt=bt, lse=lse, pa=pa, pb=pb, bm=bm, yt=yt, winp=winp)
    return h2, sv, W


def _mixer_bwd(tag, dh, sv, W, C, G, S, l, G_chunks, rider=None):
    Lp = dh.shape[0]
    tm = _rt(Lp, 1408)
    H = sv["H"]
    row = pl.BlockSpec((tm, D), lambda i: (i, 0))
    dy = _mm(f"bdy_{tag}", dh, W["wout"], grid=(Lp // tm,), a_spec=row,
             b_spec=pl.BlockSpec((NSH, None, 256, D), lambda i: (0, l, 0, 0)), o_spec=row, o_shape=(Lp, D), o_dtype=F32, dims=NT,
             acc_shape=None, shards="rows")
    tn = 512
    G["wout"] = _mm(f"bwo_{tag}", sv["yt"], dh, grid=(NSH, D // tn), a_spec=pl.BlockSpec((256, Lp), lambda s, n: (s, 0)),
                    b_spec=pl.BlockSpec((Lp, tn), lambda s, n: (0, n)),
                    o_spec=pl.BlockSpec((None, 256, tn), lambda s, n: (s, 0, n)), o_shape=(NSH, 256, D), o_dtype=F32, dims=NN,
                    acc_shape=None)
    dpa, dpb, dH, S["bm"][l] = _merge_bwd(f"bmrg_{tag}", H, sv["pa"], sv["pb"], sv["bm"], dy)
    dbranch = {}
    for key, dp, xt in (("wpa", dpa, sv["at"]), ("wpb", dpb, sv["bt"])):
        dbranch[key] = _mm(f"bx{key[2]}_{tag}", dp, W[key], grid=(Lp // tm,), a_spec=row,
                           b_spec=pl.BlockSpec((NSH, None, 512, 256), lambda i: (0, l, 0, 0)),
                           o_spec=pl.BlockSpec((tm, 512), lambda i: (i, 0)), o_shape=(Lp, 512), o_dtype=F32, dims=NT,
                           acc_shape=None, shards="cols")
        G[key] = _mm(f"bw{key[2]}_{tag}", xt, dp, grid=(NSH,), a_spec=pl.BlockSpec((512, Lp), lambda s: (0, 0)),
                     b_spec=pl.BlockSpec((Lp, 256), lambda s: (0, s)),
                     o_spec=pl.BlockSpec((None, 512, 256), lambda s: (s, 0, 0)), o_shape=(NSH, 512, 256), o_dtype=F32, dims=NN,
                     acc_shape=None)
    (dqp, dkr, dvb), carried = _att_bwd(f"attb_{tag}", sv["qp"], sv["kr"], sv["vb"], C["bias"], sv["lse"], dbranch["wpb"],
                                        sv["b"], rider)
    dH, S["gq"][l] = _att_q_bwd(f"attq_{tag}", H, sv["gq"], C["cos"], C["sin"], C["bd"], dqp, dH)
    dH, S["gk"][l] = _att_kv_bwd(f"attk_{tag}", H, sv["gk"], C["cos"], C["sin"], C["bd"], dkr, dvb, dH)
    do, dH, S["gn"][l] = _gla_post_bwd(f"bglo_{tag}", sv["of"], sv["ob"], H, sv["gn"], dbranch["wpa"], dH)
    dqf, dkf, dgf, dvf = _gla_bwd(f"bglf_{tag}", sv["qh"], sv["kh"], sv["gf"], sv["vh"], sv["sf"], do, False, G_chunks)
    dqb, dkb, dgb, dvr = _gla_bwd(f"bglr_{tag}", sv["qh"], sv["kh"], sv["gb"], sv["vh"], sv["sb"], do, True, G_chunks)
    dH = _gla_qkv_bwd(f"bglq_{tag}", dqf, dqb, dkf, dkb, dvf, dvr, dH)
    dH, S["w2p"][l], S["b2p"][l] = _gla_gate_bwd(f"bglg_{tag}", H, sv["w2p"], sv["b2p"], dgf, dgb, dH)
    tmm = 256
    G["winp"] = _mm(f"bwi_{tag}", sv["zt"], dH, grid=(D // tmm, HP // 896), a_spec=pl.BlockSpec((tmm, Lp), lambda m, n: (m, 0)),
                    b_spec=pl.BlockSpec((Lp, 896), lambda m, n: (0, n)), o_spec=pl.BlockSpec((tmm, 896), lambda m, n: (m, n)),
                    o_shape=(D, HP), o_dtype=F32, dims=NN, acc_shape=None)
    dz = _mm(f"bdz_{tag}", dH, sv["winp"], grid=(D // 512, Lp // tm), a_spec=pl.BlockSpec((tm, HP), lambda n, i: (i, 0)),
             b_spec=pl.BlockSpec((512, HP), lambda n, i: (n, 0)), o_spec=pl.BlockSpec((tm, 512), lambda n, i: (i, n)),
             o_shape=(Lp, D), o_dtype=F32, dims=NT, acc_shape=None)
    dh_prev, dg = _rms_bwd(f"rmsb_{tag}", dz, sv["h"], sv["g"], dh)
    return dh_prev, dg, carried


def _winp_layer(win_g, l):
    return _win_to_padded(jnp.transpose(win_g[:, l], (1, 0, 2)).reshape(D, D_IN))


def _local_step(x2, tgt2, W, comm=None):
    n_tok = x2.shape[0]
    Lp = n_tok + LANE
    nc = Lp // CHUNK
    g_chunks = max(g for g in (1, 2, 3, 6) if nc % g == 0)
    cos, sin = _rope_tables(n_tok)
    bd = jnp.asarray(np.kron(np.eye(ATT_H, dtype=np.float32), np.full((HD, HD), 1.0 / HD, np.float32)))
    bias = jnp.where(jnp.arange(Lp) >= PAD, 0.0, -1e30).astype(F32).reshape(1, Lp)
    w2, b2 = W["w2"], W["b2"]
    w2p = jnp.zeros((DEPTH, LANE, 512), F32)
    w2p = w2p.at[:, 0:GLA_RANK, 0:256].set(w2[:, 0]).at[:, GLA_RANK:2 * GLA_RANK, 256:512].set(w2[:, 1]).astype(BF16)
    C = dict(cos=cos, sin=sin, bd=bd, bias=bias, w2p=w2p, b2p=b2.reshape(DEPTH, 1, 512),
             gq=jnp.tile(W["qn"], (1, ATT_H)).reshape(DEPTH, 1, 512), gk=jnp.tile(W["kn"], (1, ATT_KV)).reshape(DEPTH, 1, LANE))
    h = jnp.concatenate([jnp.zeros((PAD, D), F32), W["meta"], x2], axis=0)
    saved = []
    for l in range(DEPTH):
        h, s0 = _ffn_fwd(f"l{l}a", h, W, l, 0)
        rider = comm.fwd_rider(W, l) if comm else None
        h, sm, W = _mixer_fwd(f"l{l}m", h, W, _winp_layer(W["win"], l), C, l, g_chunks, rider)
        h, s1 = _ffn_fwd(f"l{l}b", h, W, l, 1)
        saved.append((s0, sm, s1))
    dh, dfin, loss = _loss_head("loss_head", h, W["fin"].reshape(1, D), tgt2)
    S = dict(bm=[None] * DEPTH, gq=[None] * DEPTH, gk=[None] * DEPTH, gn=[None] * DEPTH, w2p=[None] * DEPTH,
             b2p=[None] * DEPTH, ng=[[None] * 3 for _ in range(DEPTH)])
    Gs = [None] * DEPTH
    for l in reversed(range(DEPTH)):
        s0, sm, s1 = saved[l]
        G = {}
        rider = comm.ffn_rider(l) if comm else None
        dh, S["ng"][l][2], carried = _ffn_bwd(f"l{l}b", dh, s1, W, G, l, 1, rider)
        if rider is not None:
            comm.ffn_carried(l, carried)
        rider = comm.bwd_rider(l) if comm else None
        dh, S["ng"][l][1], carried = _mixer_bwd(f"l{l}m", dh, sm, W, C, G, S, l, g_chunks, rider)
        if rider is not None:
            comm.bwd_carried(l, carried)
        dh, S["ng"][l][0], _ = _ffn_bwd(f"l{l}a", dh, s0, W, G, l, 0)
        Gs[l] = G
        if comm:
            comm.layer_done(l, G)
    small = dict(
        meta=dh[PAD:LANE],
        norm_gains=jnp.stack([jnp.concatenate(S["ng"][l], axis=0) for l in range(DEPTH)]),
        w2=jnp.stack([jnp.stack([S["w2p"][l][0:GLA_RANK, 0:256], S["w2p"][l][GLA_RANK:2 * GLA_RANK, 256:512]]) for l in range(DEPTH)]),
        b2=jnp.stack([S["b2p"][l].reshape(2, 256) for l in range(DEPTH)]),
        gn=jnp.concatenate(S["gn"], axis=0),
        qn=jnp.stack([S["gq"][l].reshape(ATT_H, HD).sum(0) for l in range(DEPTH)]),
        kn=jnp.stack([S["gk"][l].reshape(ATT_KV, HD).sum(0) for l in range(DEPTH)]),
        bm=jnp.stack(S["bm"]),
        fin=dfin.reshape(D),
    )
    return loss[0, 0], dh, Gs, small


def _win_to_padded(w):
    pad = jnp.zeros(w.shape[:-1] + (HP - D_IN,), w.dtype)
    return jnp.concatenate([w[..., 2336:4384], w[..., 0:1536], w[..., 1568:2336], w[..., 1536:1568], pad], axis=-1)


def _win_from_padded(w):
    return jnp.concatenate([w[..., 2048:3584], w[..., 4352:4384], w[..., 3584:4352], w[..., 0:2048]], axis=-1)


def _assemble(g):
    W = dict(gate=g["gate"], up=g["up"], down=g["down"], win=g["win"], wpa=g["wpa"], wpb=g["wpb"], wout=g["wout"])
    sm = g["small"]
    parts = _unpack(sm, SHARDED_SMALL)
    W["meta"] = jnp.transpose(parts["meta"], (1, 0, 2)).reshape(N_META, D)
    W["norm_gains"] = jnp.transpose(parts["norm_gains"], (1, 2, 0, 3)).reshape(DEPTH, 3, D)
    W["w2"] = jnp.transpose(parts["w2"], (1, 2, 3, 0, 4)).reshape(DEPTH, 2, GLA_RANK, 256)
    W["b2"] = jnp.transpose(parts["b2"], (1, 2, 0, 3)).reshape(DEPTH, 2, 256)
    W["bm"] = jnp.transpose(parts["bm"], (1, 2, 0, 3)).reshape(DEPTH, 2, D)
    return W


SHARDED_SMALL = dict(meta=(N_META, 256), norm_gains=(DEPTH, 3, 256), w2=(DEPTH, 2, GLA_RANK, 64), b2=(DEPTH, 2, 64),
                     bm=(DEPTH, 2, 256))
FULL_SMALL = dict(meta=(N_META, D), norm_gains=(DEPTH, 3, D), w2=(DEPTH, 2, GLA_RANK, 256), b2=(DEPTH, 2, 256),
                  gn=(DEPTH, 512), qn=(DEPTH, HD), kn=(DEPTH, HD), bm=(DEPTH, 2, D), fin=(D,))


def _pack(parts, table, rows):
    flat = jnp.concatenate([parts[k].reshape(-1).astype(F32) for k in table])
    return jnp.pad(flat, (0, rows * LANE - flat.shape[0])).reshape(rows, LANE)


def _unpack(packed, table):
    lead = packed.shape[:-2]
    flat = packed.reshape(lead + (-1,))
    out, off = {}, 0
    for k, shp in table.items():
        n = int(np.prod(shp))
        out[k] = flat[..., off:off + n].reshape(lead + tuple(shp))
        off += n
    return out


def _rows_for(table, mult):
    n = sum(int(np.prod(s)) for s in table.values())
    return -(-n // (LANE * mult)) * mult


SMALL_ROWS = _rows_for(SHARDED_SMALL, 16)
GRAD_ROWS = _rows_for(dict(FULL_SMALL, loss=(1,)), 8)


def _place():
    x, y, c = lax.axis_index("x"), lax.axis_index("y"), lax.axis_index("c")
    return x, y, c


def _other_chips(x, y):
    return [(1 - x, y), (x, 1 - y), (1 - x, 1 - y)]


def _cast_place(name, w3, slot):
    A, R, Cc = w3.shape
    tr = _rt(R, 512, 16)

    def body(p_ref, w_ref, o_ref):
        o_ref[...] = w_ref[...].astype(BF16)

    return pl.pallas_call(
        body, name=name,
        grid_spec=pltpu.PrefetchScalarGridSpec(
            num_scalar_prefetch=1, grid=(A, R // tr),
            in_specs=[pl.BlockSpec((None, tr, Cc), lambda a, r, p_ref: (a, r, 0))],
            out_specs=pl.BlockSpec((None, None, tr, Cc), lambda a, r, p_ref: (p_ref[0], a, r, 0))),
        out_shape=SDS((NSH, A, R, Cc), BF16))(slot, w3)


class _GatherRider:
    def __init__(self, keys, bufs, layers):
        n = len(bufs)
        self.keys, self.ins, self.layers = keys, list(bufs), layers
        self.out_shape = [SDS(b.shape, b.dtype) for b in bufs]
        self.aliases = {a: a for a in range(n)}
        self.sems = [pltpu.SemaphoreType.DMA((n, 3)) for _ in range(4)]

    def updated(self, W, carried):
        return {**W, **dict(zip(self.keys, carried))}

    def hooks(self, ins, outs, sems):
        send, recv, fsend, frecv = sems
        n = len(outs)

        def rows(a, slot, core):
            ref, l = outs[a], self.layers[a]
            half = ref.shape[-2] // 2
            return ref.at[slot, pl.ds(core * half, half)] if l is None else ref.at[slot, l, pl.ds(core * half, half)]

        def ici(a, k, part, px, py, c):
            return pltpu.make_async_remote_copy(src_ref=part, dst_ref=part, send_sem=send.at[a, k], recv_sem=recv.at[a, k],
                                                device_id=(px, py, c), device_id_type=MESH)

        def d2d(a, k, part, x, y, c):
            return pltpu.make_async_remote_copy(src_ref=part, dst_ref=part, send_sem=fsend.at[a, k], recv_sem=frecv.at[a, k],
                                                device_id=(x, y, 1 - c), device_id_type=MESH)

        def start():
            x, y, c = _place()
            for a in range(n):
                for k, (px, py) in enumerate(_other_chips(x, y)):
                    ici(a, k, rows(a, 2 * x + y, c), px, py, c).start()

        def mid():
            x, y, c = _place()
            for a in range(n):
                for k, (px, py) in enumerate(_other_chips(x, y)):
                    landed = rows(a, 2 * px + py, c)
                    ici(a, k, landed, px, py, c).wait_recv()
                    d2d(a, k, landed, x, y, c).start()

        def end():
            x, y, c = _place()
            for a in range(n):
                for k, (px, py) in enumerate(_other_chips(x, y)):
                    d2d(a, k, rows(a, 2 * px + py, 1 - c), x, y, c).wait_recv()
                    ici(a, k, rows(a, 2 * x + y, c), px, py, c).wait_send()
                    d2d(a, k, rows(a, 2 * px + py, c), x, y, c).wait_send()

        return start, mid, end


class _ChipExchangeRider:
    def __init__(self, arrs):
        n = len(arrs)
        self.ins = list(arrs)
        self.out_shape = [SDS((3,) + a.shape[1:], a.dtype) for a in arrs]
        self.aliases = {}
        self.sems = [pltpu.SemaphoreType.DMA((n, 3)), pltpu.SemaphoreType.DMA((n, 3))]

    def hooks(self, ins, outs, sems):
        send, recv = sems

        def copy(a, k, px, py, c):
            return pltpu.make_async_remote_copy(src_ref=ins[a].at[2 * px + py], dst_ref=outs[a].at[k], send_sem=send.at[a, k],
                                                recv_sem=recv.at[a, k], device_id=(px, py, c), device_id_type=MESH)

        def start():
            x, y, c = _place()
            for a in range(len(ins)):
                for k, (px, py) in enumerate(_other_chips(x, y)):
                    copy(a, k, px, py, c).start()

        def mid():
            pass

        def end():
            x, y, c = _place()
            for a in range(len(ins)):
                for k, (px, py) in enumerate(_other_chips(x, y)):
                    copy(a, k, px, py, c).wait()

        return start, mid, end


def _run_rider(name, rider):
    r_in, r_out = len(rider.ins), len(rider.out_shape)

    def body(*refs):
        start, mid, end = rider.hooks(refs[:r_in], refs[r_in:r_in + r_out], refs[r_in + r_out:])
        start()
        mid()
        end()

    any_spec = pl.BlockSpec(memory_space=pl.ANY)
    return pl.pallas_call(
        body, name=name, in_specs=[any_spec] * r_in, out_specs=[any_spec] * r_out, out_shape=list(rider.out_shape),
        scratch_shapes=list(rider.sems), input_output_aliases=dict(rider.aliases))(*rider.ins)


class _PairExchangeRider:
    def __init__(self, arrs):
        n = len(arrs)
        self.ins = list(arrs)
        self.out_shape = [SDS((NSH, a.shape[1] // 2, a.shape[2]), a.dtype) for a in arrs]
        self.aliases = {}
        self.sems = [pltpu.SemaphoreType.DMA((n,)), pltpu.SemaphoreType.DMA((n,))]

    def hooks(self, ins, outs, sems):
        send, recv = sems

        def copy(a):
            x, y, c = _place()
            half = ins[a].shape[1] // 2
            return pltpu.make_async_remote_copy(
                src_ref=ins[a].at[:, pl.ds((1 - c) * half, half)], dst_ref=outs[a], send_sem=send.at[a], recv_sem=recv.at[a],
                device_id=(x, y, 1 - c), device_id_type=MESH)

        def start():
            for a in range(len(ins)):
                copy(a).start()

        def mid():
            pass

        def end():
            for a in range(len(ins)):
                copy(a).wait()

        return start, mid, end


def _pair_add(name, g, p, core):
    _, Rh, Cc = p.shape
    tr = _rt(Rh, 512, 16)
    nr = Rh // tr

    def body(c_ref, g_ref, p_ref, o_ref, ob_ref):
        v = g_ref[...] + p_ref[...]
        o_ref[...] = v
        ob_ref[...] = v.astype(BF16)

    blk = pl.BlockSpec((None, tr, Cc), lambda s, r, c_ref: (s, r, 0))
    return pl.pallas_call(
        body, name=name,
        grid_spec=pltpu.PrefetchScalarGridSpec(
            num_scalar_prefetch=1, grid=(NSH, nr),
            in_specs=[pl.BlockSpec((None, tr, Cc), lambda s, r, c_ref: (s, c_ref[0] * nr + r, 0)), blk],
            out_specs=[blk, blk]),
        out_shape=[SDS((NSH, Rh, Cc), F32), SDS((NSH, Rh, Cc), BF16)])(core, g, p)


def _chip_add(name, hsum, q, chip, core, l, into):
    _, Rh, Cc = hsum.shape
    tr = _rt(Rh, 512, 8)
    nr = Rh // tr

    def body(*refs):
        h_ref, q_ref, o_ref = refs[2], refs[3], refs[-1]
        o_ref[...] = ((h_ref[...] + q_ref[0].astype(F32)) + q_ref[1].astype(F32)) + q_ref[2].astype(F32)

    in_specs = [pl.BlockSpec((None, tr, Cc), lambda r, p_ref, c_ref: (p_ref[0], r, 0)),
                pl.BlockSpec((3, tr, Cc), lambda r, p_ref, c_ref: (0, r, 0))]
    args = [chip, core, hsum, q]
    aliases = {}
    if into is not None:
        in_specs.append(pl.BlockSpec(memory_space=pl.ANY))
        args.append(into)
        aliases = {4: 0}
    return pl.pallas_call(
        body, name=name,
        grid_spec=pltpu.PrefetchScalarGridSpec(
            num_scalar_prefetch=2, grid=(nr,), in_specs=in_specs,
            out_specs=pl.BlockSpec((None, tr, Cc), lambda r, p_ref, c_ref: (l, c_ref[0] * nr + r, 0))),
        out_shape=SDS((DEPTH, 2 * Rh, Cc), F32), input_output_aliases=aliases)(*args)


def _pair_share(arrs):
    n = len(arrs)

    def body(*refs):
        outs = refs[n:2 * n]
        send, recv = refs[2 * n:]
        x, y, c = _place()
        cps = []
        for a in range(n):
            half = outs[a].shape[1] // 2
            mine = outs[a].at[:, pl.ds(c * half, half)]
            cp = pltpu.make_async_remote_copy(
                src_ref=mine, dst_ref=mine, send_sem=send.at[a], recv_sem=recv.at[a],
                device_id=(x, y, 1 - c), device_id_type=MESH)
            cp.start()
            cps.append(cp)
        for a, cp in enumerate(cps):
            cp.wait_send()
            half = outs[a].shape[1] // 2
            theirs = outs[a].at[:, pl.ds((1 - c) * half, half)]
            pltpu.make_async_remote_copy(
                src_ref=theirs, dst_ref=theirs, send_sem=send.at[a], recv_sem=recv.at[a],
                device_id=(x, y, 1 - c), device_id_type=MESH).wait_recv()

    any_spec = pl.BlockSpec(memory_space=pl.ANY)
    return pl.pallas_call(
        body, name="pair_share", in_specs=[any_spec] * n, out_specs=[any_spec] * n,
        out_shape=[SDS(a.shape, a.dtype) for a in arrs], input_output_aliases={a: a for a in range(n)},
        scratch_shapes=[pltpu.SemaphoreType.DMA((n,)), pltpu.SemaphoreType.DMA((n,))])(*arrs)


def _allreduce_small(v):
    rows = v.shape[0]

    def body(v_ref, o_ref, buf, send, recv):
        x, y, c = _place()
        me = 4 * x + 2 * y + c
        buf[me] = v_ref[...]
        cps = []
        k = 0
        for dx in range(2):
            for dy in range(2):
                for dc in range(2):
                    if dx + dy + dc == 0:
                        continue
                    cp = pltpu.make_async_remote_copy(
                        src_ref=v_ref, dst_ref=buf.at[me], send_sem=send.at[k], recv_sem=recv.at[k],
                        device_id=(jnp.bitwise_xor(x, dx), jnp.bitwise_xor(y, dy), jnp.bitwise_xor(c, dc)), device_id_type=MESH)
                    cp.start()
                    cps.append((cp, dx, dy, dc))
                    k += 1
        for k, (cp, dx, dy, dc) in enumerate(cps):
            cp.wait_send()
            src = 4 * jnp.bitwise_xor(x, dx) + 2 * jnp.bitwise_xor(y, dy) + jnp.bitwise_xor(c, dc)
            pltpu.make_async_remote_copy(
                src_ref=v_ref, dst_ref=buf.at[src], send_sem=send.at[k], recv_sem=recv.at[k],
                device_id=(x, y, c), device_id_type=MESH).wait_recv()
        acc = buf[0]
        for d in range(1, 8):
            acc = acc + buf[d]
        o_ref[...] = acc

    vm = pl.BlockSpec(memory_space=pltpu.VMEM)
    return pl.pallas_call(
        body, name="allreduce_small", in_specs=[vm], out_specs=vm, out_shape=SDS((rows, LANE), F32),
        scratch_shapes=[pltpu.VMEM((8, rows, LANE), F32), pltpu.SemaphoreType.DMA((7,)), pltpu.SemaphoreType.DMA((7,))])(v)


def _adamw(name, w, g, m, v):
    A, R, Cc = w.shape
    tr = _rt(R, 512, 8)

    def body(w_ref, g_ref, m_ref, v_ref, d_ref, mo_ref, vo_ref):
        gv = g_ref[...]
        mn = ADAM_B1 * m_ref[...] + (1.0 - ADAM_B1) * gv
        vn = ADAM_B2 * v_ref[...] + (1.0 - ADAM_B2) * (gv * gv)
        m_hat = mn / (1.0 - ADAM_B1 ** ADAM_STEP)
        v_hat = vn / (1.0 - ADAM_B2 ** ADAM_STEP)
        d_ref[...] = -ADAM_LR * (m_hat / (jnp.sqrt(v_hat) + ADAM_EPS) + ADAM_WD * w_ref[...])
        mo_ref[...] = mn
        vo_ref[...] = vn

    blk = pl.BlockSpec((None, tr, Cc), lambda a, r: (a, r, 0))
    return pl.pallas_call(
        body, name=name, grid=(A, R // tr), in_specs=[blk] * 4, out_specs=[blk] * 3,
        out_shape=[SDS(w.shape, F32)] * 3)(w, g, m, v)


BIG = ("gate", "up", "down", "win", "wpa", "wpb", "wout")
SMALL = ("meta", "norm_gains", "w2", "b2", "gn", "qn", "kn", "bm", "fin")


def _view3(a):
    return a.reshape(a.shape[0], -1, a.shape[-1])


class _StepComm:
    def __init__(self, chip, core):
        self.pvec, self.cvec = chip.reshape(1), core.reshape(1)
        self.loc, self.sums = {}, {}
        self.red = {k: None for k in BIG}

    def fwd_rider(self, W, l):
        if l + 1 >= DEPTH:
            return None
        return _GatherRider(BIG, [W[k] for k in BIG], [l + 1] * len(BIG))

    def layer_done(self, l, G):
        dwin = jnp.transpose(_win_from_padded(G["winp"]).reshape(D, NSH, D_IN // NSH), (1, 0, 2))
        self.loc[l] = [dwin if k == "win" else _view3(G[k]) for k in BIG]

    def ffn_rider(self, l):
        return _PairExchangeRider(self.loc[l + 1]) if l + 1 < DEPTH else None

    def ffn_carried(self, l, got):
        self._pair_add(l + 1, got)

    def _pair_add(self, l, got):
        self.sums[l] = [_pair_add(f"pair_add_{k}_l{l}", a, p, self.cvec) for k, a, p in zip(BIG, self.loc.pop(l), got)]

    def bwd_rider(self, l):
        if l + 1 >= DEPTH:
            return None
        return _ChipExchangeRider([s[1] for s in self.sums[l + 1]])

    def bwd_carried(self, l, arrived):
        self._chip_add(l + 1, arrived)

    def _chip_add(self, l, arrived):
        for k, s, q in zip(BIG, self.sums.pop(l), arrived):
            self.red[k] = _chip_add(f"chip_add_{k}_l{l}", s[0], q, self.pvec, self.cvec, l, self.red[k])

    def finish(self):
        self._pair_add(0, _run_rider("pair_exchange_l0", _PairExchangeRider(self.loc[0])))
        self._chip_add(0, _run_rider("chip_exchange_l0", _ChipExchangeRider([s[1] for s in self.sums[0]])))
        return dict(zip(BIG, _pair_share([self.red[k] for k in BIG])))


def kernel(x, meta_tokens, norm_gains, ffn_w_gate, ffn_w_up, ffn_w_down, w_in, gla_w2, gla_b2, gla_gn, q_norm, k_norm, w_pa, w_pb, b_merge, w_out, final_norm, loss_target, m_meta_tokens, m_norm_gains, m_ffn_w_gate, m_ffn_w_up, m_ffn_w_down, m_w_in, m_gla_w2, m_gla_b2, m_gla_gn, m_q_norm, m_k_norm, m_w_pa, m_w_pb, m_b_merge, m_w_out, m_final_norm, v_meta_tokens, v_norm_gains, v_ffn_w_gate, v_ffn_w_up, v_ffn_w_down, v_w_in, v_gla_w2, v_gla_b2, v_gla_gn, v_q_norm, v_k_norm, v_w_pa, v_w_pb, v_b_merge, v_w_out, v_final_norm):
    big_w = dict(gate=ffn_w_gate, up=ffn_w_up, down=ffn_w_down, win=w_in, wpa=w_pa, wpb=w_pb, wout=w_out)
    big_m = dict(gate=m_ffn_w_gate, up=m_ffn_w_up, down=m_ffn_w_down, win=m_w_in, wpa=m_w_pa, wpb=m_w_pb, wout=m_w_out)
    big_v = dict(gate=v_ffn_w_gate, up=v_ffn_w_up, down=v_ffn_w_down, win=v_w_in, wpa=v_w_pa, wpb=v_w_pb, wout=v_w_out)
    small_w = dict(meta=meta_tokens, norm_gains=norm_gains, w2=gla_w2, b2=gla_b2, gn=gla_gn, qn=q_norm, kn=k_norm,
                   bm=b_merge, fin=final_norm)
    small_m = dict(meta=m_meta_tokens, norm_gains=m_norm_gains, w2=m_gla_w2, b2=m_gla_b2, gn=m_gla_gn, qn=m_q_norm,
                   kn=m_k_norm, bm=m_b_merge, fin=m_final_norm)
    small_v = dict(meta=v_meta_tokens, norm_gains=v_norm_gains, w2=v_gla_w2, b2=v_gla_b2, gn=v_gla_gn, qn=v_q_norm,
                   kn=v_k_norm, bm=v_b_merge, fin=v_final_norm)
    xi, yi, ci = _place()
    chip = (2 * xi + yi).astype(jnp.int32)

    comm = _StepComm(chip, ci.astype(jnp.int32))
    shard_pack = _pack({k: small_w[k] for k in SHARDED_SMALL}, SHARDED_SMALL, SMALL_ROWS)
    placed = [_cast_place(f"cast_{k}", _view3(big_w[k]), comm.pvec) for k in BIG]
    placed.append(lax.dynamic_update_slice(jnp.zeros((NSH, SMALL_ROWS, LANE), F32), shard_pack[None], (chip, 0, 0)))
    gathered = _run_rider("gather_l0", _GatherRider(BIG + ("small",), placed, [0] * len(BIG) + [None]))
    W = _assemble(dict(zip(BIG + ("small",), gathered)))
    W.update(gn=gla_gn, qn=q_norm, kn=k_norm, fin=final_norm)

    loss, dh0, _, gs = _local_step(x[0], loss_target[0], W, comm)
    grad_x = dh0[LANE:][None]
    shared = comm.finish()
    grads, deltas, new_m, new_v = {}, {}, {}, {}
    for k in BIG:
        r = shared[k]
        shp = big_w[k].shape
        grads[k] = r.reshape(shp)
        d, mn, vn = _adamw(f"adamw_{k}", _view3(big_w[k]), r, _view3(big_m[k]), _view3(big_v[k]))
        deltas[k], new_m[k], new_v[k] = d.reshape(shp), mn.reshape(shp), vn.reshape(shp)

    gs["loss"] = loss.reshape(1)
    table = dict(FULL_SMALL, loss=(1,))
    tot = _unpack(_allreduce_small(_pack(gs, table, GRAD_ROWS)), table)
    loss_out = tot["loss"][0]
    sl = dict(meta=(1, 256), norm_gains=(2, 256), w2=(3, 64), b2=(2, 64), bm=(2, 256))
    for k in SMALL:
        gk = tot[k]
        if k in sl:
            ax, width = sl[k]
            gk = lax.dynamic_slice_in_dim(gk, chip * width, width, axis=ax)
        grads[k] = gk
    tbl = {k: small_w[k].shape for k in SMALL}
    rows = _rows_for(tbl, 8)
    packs = [_pack(src, tbl, rows)[None] for src in (small_w, grads, small_m, small_v)]
    d, mn, vn = _adamw("adamw_small", *packs)
    for dst, packed in ((deltas, d), (new_m, mn), (new_v, vn)):
        dst.update(_unpack(packed[0], tbl))

    order = ("meta", "norm_gains", "gate", "up", "down", "win", "w2", "b2", "gn", "qn", "kn", "wpa", "wpb", "bm", "wout", "fin")
    return (loss_out, grad_x, *[grads[k] for k in order], *[deltas[k] for k in order], *[new_m[k] for k in order],
            *[new_v[k] for k in order])
```

```python
import functools

import numpy as np
import jax
import jax.numpy as jnp
from jax import lax
from jax.experimental import pallas as pl
from jax.experimental.pallas import tpu as pltpu

F32, BF16 = jnp.float32, jnp.bfloat16
SDS = jax.ShapeDtypeStruct
HIGHEST = lax.Precision.HIGHEST
MESH = pl.DeviceIdType.MESH

D = 1024
DEPTH = 4
N_META = 16
GRID_W = 64
GLA_H, GLA_DK, GLA_DV, GLA_RANK, GLA_TAU, CHUNK = 4, 64, 128, 16, 16.0, 64
ATT_H, ATT_KV, HD = 8, 2, 64
D_FF = 2816
EPS = 1e-6
ROPE_THETA = 10000.0
ADAM_LR, ADAM_B1, ADAM_B2, ADAM_EPS, ADAM_WD, ADAM_STEP = 0.001, 0.9, 0.999, 1e-08, 0.01, 10

NSH = 4
FF_S = D_FF // NSH
FF_P = 768
LANE = 128
PAD = LANE - N_META
D_IN = 4384
C_GA, C_GB, C_QA, C_KA, C_VA, C_RA, C_QB, C_KB, C_VB, C_LR, HP = 0, 1024, 2048, 2304, 2560, 3072, 3584, 4096, 4224, 4352, 4480
VMEM_BIG = 56 * 2 ** 20
ATT_SUB_FWD, ATT_SUB_BWD = 128, 192
RIDER_MID_BACK = 5


def _rt(n, cap, mult=LANE):
    best = None
    t = mult
    while t <= min(n, cap):
        if n % t == 0:
            best = t
        t += mult
    assert best is not None, (n, cap, mult)
    return best


def _cp(big=False):
    return pltpu.CompilerParams(vmem_limit_bytes=VMEM_BIG) if big else None


def _row_ids(i, tm):
    return i * tm + lax.broadcasted_iota(jnp.int32, (tm, 1), 0)


def _mm(name, a, b, *, grid, a_spec, b_spec, o_spec, o_shape, o_dtype, dims, acc_shape, nk=1, scale=None,
        res=None, res_spec=None, a_sl=None, b_sl=None, pad_w=None, into=None, shards=None):
    has_res, has_into = res is not None, into is not None

    def body(*refs):
        a_ref, b_ref = refs[0], refs[1]
        p = 2
        res_ref = None
        if has_res:
            res_ref = refs[p]
            p += 1
        if has_into:
            p += 1
        o_ref = refs[p]
        acc_ref = refs[p + 1] if nk > 1 else None
        av = (a_ref[a_sl] if a_sl is not None else a_ref[...]).astype(BF16)
        bv = (b_ref[b_sl] if b_sl is not None else b_ref[...]).astype(BF16)
        if shards == "rows":
            bv = bv.reshape(bv.shape[0] * bv.shape[1], bv.shape[2])
        if shards == "cols":
            w = bv.shape[2]
            if dims == NN:
                prod = jnp.concatenate([lax.dot_general(av, bv[s], NN, preferred_element_type=F32) for s in range(NSH)], axis=1)
            else:
                prod = sum(lax.dot_general(av[:, s * w:(s + 1) * w], bv[s], NT, preferred_element_type=F32) for s in range(NSH))
        else:
            prod = lax.dot_general(av, bv, dims, preferred_element_type=F32)

        def finish(v):
            if scale is not None:
                v = v * scale
            if has_res:
                v = v + res_ref[...]
            v = v.astype(o_dtype)
            if pad_w is None:
                o_ref[...] = v
            else:
                w = v.shape[-1]
                o_ref[:, :w] = v
                o_ref[:, w:] = jnp.zeros((v.shape[0], pad_w - w), o_dtype)

        if nk == 1:
            finish(prod)
        else:
            k = pl.program_id(len(grid) - 1)

            @pl.when(k == 0)
            def _():
                acc_ref[...] = prod

            @pl.when(k > 0)
            def _():
                acc_ref[...] += prod

            @pl.when(k == nk - 1)
            def _():
                finish(acc_ref[...])

    in_specs = [a_spec, b_spec]
    args = [a, b]
    if has_res:
        in_specs.append(res_spec)
        args.append(res)
    aliases = {}
    if has_into:
        aliases = {len(args): 0}
        in_specs.append(pl.BlockSpec(memory_space=pl.ANY))
        args.append(into)
        o_shape = into.shape
    return pl.pallas_call(
        body, name=name, grid=grid, in_specs=in_specs, out_specs=o_spec, out_shape=SDS(tuple(o_shape), o_dtype),
        scratch_shapes=[pltpu.VMEM(acc_shape, F32)] if nk > 1 else [], input_output_aliases=aliases,
        compiler_params=_cp(True))(*args)


NN = (((1,), (0,)), ((), ()))
NT = (((1,), (1,)), ((), ()))
TN = (((0,), (0,)), ((), ()))


def _rms_fwd(name, h, g):
    Lp = h.shape[0]
    tm = _rt(Lp, 384)

    def body(h_ref, g_ref, n_ref, nt_ref):
        x = h_ref[...]
        r = lax.rsqrt(jnp.mean(x * x, axis=-1, keepdims=True) + EPS)
        y = x * r * g_ref[...]
        n_ref[...] = y.astype(BF16)
        nt_ref[...] = y.T.astype(BF16)

    return pl.pallas_call(
        body, name=name, grid=(Lp // tm,),
        in_specs=[pl.BlockSpec((tm, D), lambda i: (i, 0)), pl.BlockSpec((1, D), lambda i: (0, 0))],
        out_specs=[pl.BlockSpec((tm, D), lambda i: (i, 0)), pl.BlockSpec((D, tm), lambda i: (0, i))],
        out_shape=[SDS((Lp, D), BF16), SDS((D, Lp), BF16)])(h, g)


def _rms_bwd(name, dn, h, g, dh):
    Lp = h.shape[0]
    tm = _rt(Lp, 384)

    def body(dn_ref, h_ref, g_ref, dh_ref, o_ref, dg_ref):
        x = h_ref[...]
        r = lax.rsqrt(jnp.mean(x * x, axis=-1, keepdims=True) + EPS)
        xh = x * r
        dn = dn_ref[...]
        u = dn * g_ref[...]
        o_ref[...] = dh_ref[...] + r * (u - xh * jnp.mean(u * xh, axis=-1, keepdims=True))

        @pl.when(pl.program_id(0) == 0)
        def _():
            dg_ref[...] = jnp.zeros_like(dg_ref)

        dg_ref[...] += jnp.sum(dn * xh, axis=0, keepdims=True)

    row = pl.BlockSpec((tm, D), lambda i: (i, 0))
    vec = pl.BlockSpec((1, D), lambda i: (0, 0))
    return pl.pallas_call(
        body, name=name, grid=(Lp // tm,), in_specs=[row, row, vec, row], out_specs=[row, vec],
        out_shape=[SDS((Lp, D), F32), SDS((1, D), F32)])(dn, h, g, dh)


def _loss_head(name, h, g, tgt):
    Lp = h.shape[0]
    tm = LANE

    def body(h_ref, g_ref, t_ref, dh_ref, dg_ref, loss_ref):
        i = pl.program_id(0)
        x = h_ref[...]
        r = lax.rsqrt(jnp.mean(x * x, axis=-1, keepdims=True) + EPS)
        xh = x * r
        gg = g_ref[...]
        err = jnp.where(i >= 1, xh * gg - t_ref[...], 0.0)
        dy = err * (1.0 / D)
        u = dy * gg
        dh_ref[...] = r * (u - xh * jnp.mean(u * xh, axis=-1, keepdims=True))

        @pl.when(i == 0)
        def _():
            dg_ref[...] = jnp.zeros_like(dg_ref)
            loss_ref[...] = jnp.zeros_like(loss_ref)

        dg_ref[...] += jnp.sum(dy * xh, axis=0, keepdims=True)
        loss_ref[...] += (0.5 / D) * jnp.sum(err * err)

    row = pl.BlockSpec((tm, D), lambda i: (i, 0))
    vec = pl.BlockSpec((1, D), lambda i: (0, 0))
    return pl.pallas_call(
        body, name=name, grid=(Lp // tm,),
        in_specs=[row, vec, pl.BlockSpec((tm, D), lambda i: (jnp.maximum(i - 1, 0), 0))],
        out_specs=[row, vec, pl.BlockSpec((8, LANE), lambda i: (0, 0))],
        out_shape=[SDS((Lp, D), F32), SDS((1, D), F32), SDS((8, LANE), F32)])(h, g, tgt)


def _merge_fwd(name, H, pa, pb, bm):
    Lp = H.shape[0]
    tm = _rt(Lp, 384)

    def body(g_ref, pa_ref, pb_ref, bm_ref, y_ref, yt_ref):
        gv = g_ref[...]
        y = (jax.nn.sigmoid(gv[:, :D] + bm_ref[0:1, :]) * pa_ref[...]
             + jax.nn.sigmoid(gv[:, D:] + bm_ref[1:2, :]) * pb_ref[...])
        y_ref[...] = y.astype(BF16)
        yt_ref[...] = y.T.astype(BF16)

    row = pl.BlockSpec((tm, D), lambda i: (i, 0))
    return pl.pallas_call(
        body, name=name, grid=(Lp // tm,),
        in_specs=[pl.BlockSpec((tm, 2 * D), lambda i: (i, 0)), row, row, pl.BlockSpec((2, D), lambda i: (0, 0))],
        out_specs=[row, pl.BlockSpec((D, tm), lambda i: (0, i))],
        out_shape=[SDS((Lp, D), BF16), SDS((D, Lp), BF16)])(H, pa, pb, bm)


def _merge_bwd(name, H, pa, pb, bm, dy):
    Lp = H.shape[0]
    tm = _rt(Lp, 384)

    def body(g_ref, pa_ref, pb_ref, bm_ref, dy_ref, dpa_ref, dpb_ref, dh_ref, dbm_ref):
        gv = g_ref[...]
        dyv = dy_ref[...]
        sa = jax.nn.sigmoid(gv[:, :D] + bm_ref[0:1, :])
        sb = jax.nn.sigmoid(gv[:, D:] + bm_ref[1:2, :])
        dpa_ref[...] = (dyv * sa).astype(BF16)
        dpb_ref[...] = (dyv * sb).astype(BF16)
        dga = dyv * pa_ref[...] * (sa * (1.0 - sa))
        dgb = dyv * pb_ref[...] * (sb * (1.0 - sb))
        dh_ref[:, :D] = dga.astype(BF16)
        dh_ref[:, D:] = dgb.astype(BF16)

        @pl.when(pl.program_id(0) == 0)
        def _():
            dbm_ref[...] = jnp.zeros_like(dbm_ref)

        dbm_ref[0:1, :] += jnp.sum(dga, axis=0, keepdims=True)
        dbm_ref[1:2, :] += jnp.sum(dgb, axis=0, keepdims=True)

    row = pl.BlockSpec((tm, D), lambda i: (i, 0))
    two = pl.BlockSpec((2, D), lambda i: (0, 0))
    gate = pl.BlockSpec((tm, 2 * D), lambda i: (i, 0))
    return pl.pallas_call(
        body, name=name, grid=(Lp // tm,), in_specs=[gate, row, row, two, row], out_specs=[row, row, gate, two],
        out_shape=[SDS((Lp, D), BF16), SDS((Lp, D), BF16), SDS((Lp, HP), BF16), SDS((2, D), F32)])(H, pa, pb, bm, dy)


def _gla_prep(name, H, w2p, b2p):
    Lp = H.shape[0]
    tm = _rt(Lp, 384)

    def body(qk_ref, v_ref, lr_ref, w_ref, b_ref, q_o, k_o, v_o, gf_o, gb_o):
        valid = _row_ids(pl.program_id(0), tm) >= PAD
        qk = qk_ref[...]
        vv = v_ref[...]
        pre = jnp.dot(lr_ref[...].astype(BF16), w_ref[...], preferred_element_type=F32) + b_ref[...]
        g = jnp.where(valid, jax.nn.log_sigmoid(pre) * (1.0 / GLA_TAU), 0.0)
        for hh in range(GLA_H):
            q_o[hh] = qk[:, 64 * hh:64 * hh + 64] * (GLA_DK ** -0.5)
            k_o[hh] = qk[:, 256 + 64 * hh:256 + 64 * hh + 64]
            v_o[hh] = vv[:, 128 * hh:128 * hh + 128].astype(BF16)
            gf_o[hh] = g[:, 64 * hh:64 * hh + 64]
            gb_o[hh] = g[:, 256 + 64 * hh:256 + 64 * hh + 64]

    h64 = pl.BlockSpec((GLA_H, tm, 64), lambda i: (0, i, 0))
    h128 = pl.BlockSpec((GLA_H, tm, 128), lambda i: (0, i, 0))
    return pl.pallas_call(
        body, name=name, grid=(Lp // tm,),
        in_specs=[pl.BlockSpec((tm, 512), lambda i: (i, C_QA // 512)), pl.BlockSpec((tm, 512), lambda i: (i, C_VA // 512)),
                  pl.BlockSpec((tm, LANE), lambda i: (i, C_LR // LANE)), pl.BlockSpec((LANE, 512), lambda i: (0, 0)),
                  pl.BlockSpec((1, 512), lambda i: (0, 0))],
        out_specs=[h64, h64, h128, h64, h64],
        out_shape=[SDS((GLA_H, Lp, 64), F32), SDS((GLA_H, Lp, 64), F32), SDS((GLA_H, Lp, 128), BF16),
                   SDS((GLA_H, Lp, 64), F32), SDS((GLA_H, Lp, 64), F32)])(H, H, H, w2p, b2p)


def _bdot(a, b, ca, cb, precision=None):
    return lax.dot_general(a, b, ((ca, cb), ((0,), (0,))), precision=precision, preferred_element_type=F32)


def _gla_chunk_terms(q_ref, k_ref, g_ref, v_ref, G, rev):
    B = GLA_H * G
    qv = q_ref[...].reshape(B, CHUNK, GLA_DK)
    kv = k_ref[...].reshape(B, CHUNK, GLA_DK)
    gv = g_ref[...].reshape(B, CHUNK, GLA_DK)
    vv = v_ref[...].reshape(B, CHUNK, GLA_DV)
    ii = lax.broadcasted_iota(jnp.int32, (CHUNK, CHUNK), 0)
    jj = lax.broadcasted_iota(jnp.int32, (CHUNK, CHUNK), 1)
    tri = (jj >= ii) if rev else (jj <= ii)
    tb = jnp.broadcast_to(tri.astype(F32)[None], (B, CHUNK, CHUNK))
    bc = _bdot(tb, gv, (2,), (1,), HIGHEST)
    bt = bc[:, 0:1, :] if rev else bc[:, CHUNK - 1:CHUNK, :]
    eq, eki, eke = jnp.exp(bc), jnp.exp(-bc), jnp.exp(bt - bc)
    qd, ki, ke = qv * eq, kv * eki, kv * eke
    att = jnp.where(tri[None], _bdot(qd.astype(BF16), ki.astype(BF16), (2,), (2,)), 0.0)
    dm = jnp.exp(_bdot(gv, jnp.ones((B, CHUNK, GLA_DV), F32), (1,), (1,), HIGHEST))
    return dict(B=B, vv=vv, tri=tri, tb=tb, bt=bt, eq=eq, eki=eki, eke=eke, qd=qd, ki=ki, ke=ke, att=att, dm=dm)


def _gla_fwd(name, q, k, g, v, rev, G):
    Lp = q.shape[1]
    tg = G * CHUNK
    ng = Lp // tg

    def body(q_ref, k_ref, g_ref, v_ref, o_ref, ss_ref, s_scr):
        @pl.when(pl.program_id(0) == 0)
        def _():
            s_scr[...] = jnp.zeros_like(s_scr)

        t = _gla_chunk_terms(q_ref, k_ref, g_ref, v_ref, G, rev)
        B, vv = t["B"], t["vv"]
        qd = t["qd"].astype(BF16)
        oi = _bdot(t["att"].astype(BF16), vv, (2,), (1,))
        kvc = _bdot(t["ke"].astype(BF16), vv, (1,), (1,)).reshape(GLA_H, G, GLA_DK, GLA_DV)
        dm = t["dm"].reshape(GLA_H, G, GLA_DK, GLA_DV)
        s = s_scr[...]
        sp = [None] * G
        for c in (range(G - 1, -1, -1) if rev else range(G)):
            sp[c] = s
            ss_ref[c] = s
            s = dm[:, c] * s + kvc[:, c]
        s_scr[...] = s
        spb = jnp.stack(sp, axis=1).reshape(B, GLA_DK, GLA_DV).astype(BF16)
        o_ref[...] = (oi + _bdot(qd, spb, (2,), (1,))).reshape(GLA_H, tg, GLA_DV)

    blk = (lambda i: (0, ng - 1 - i, 0)) if rev else (lambda i: (0, i, 0))
    sblk = (lambda i: (ng - 1 - i, 0, 0, 0)) if rev else (lambda i: (i, 0, 0, 0))
    h64 = pl.BlockSpec((GLA_H, tg, 64), blk)
    h128 = pl.BlockSpec((GLA_H, tg, 128), blk)
    return pl.pallas_call(
        body, name=name, grid=(ng,), in_specs=[h64, h64, h64, h128],
        out_specs=[h128, pl.BlockSpec((G, GLA_H, GLA_DK, GLA_DV), sblk)],
        out_shape=[SDS((GLA_H, Lp, GLA_DV), F32), SDS((Lp // CHUNK, GLA_H, GLA_DK, GLA_DV), F32)],
        scratch_shapes=[pltpu.VMEM((GLA_H, GLA_DK, GLA_DV), F32)], compiler_params=_cp(True))(q, k, g, v)


def _gla_bwd(name, q, k, g, v, ss, do, rev, G):
    Lp = q.shape[1]
    tg = G * CHUNK
    ng = Lp // tg

    def body(q_ref, k_ref, g_ref, v_ref, ss_ref, do_ref, dq_ref, dk_ref, dg_ref, dv_ref, ds_scr):
        @pl.when(pl.program_id(0) == 0)
        def _():
            ds_scr[...] = jnp.zeros_like(ds_scr)

        t = _gla_chunk_terms(q_ref, k_ref, g_ref, v_ref, G, rev)
        B, vv, tri = t["B"], t["vv"], t["tri"]
        qd, ki, ke = t["qd"], t["ki"], t["ke"]
        qdb, kib, keb = qd.astype(BF16), ki.astype(BF16), ke.astype(BF16)
        sp = jnp.stack([ss_ref[c] for c in range(G)], axis=1).reshape(B, GLA_DK, GLA_DV)
        dob = do_ref[...].reshape(B, CHUNK, GLA_DV).astype(BF16)
        da = jnp.where(tri[None], _bdot(dob, vv, (2,), (2,)), 0.0).astype(BF16)
        dqd = _bdot(da, kib, (2,), (1,)) + _bdot(dob, sp.astype(BF16), (2,), (2,))
        dki = _bdot(da, qdb, (1,), (1,))
        dv = _bdot(t["att"].astype(BF16), dob, (1,), (1,))
        cc = _bdot(qdb, dob, (1,), (1,)).reshape(GLA_H, G, GLA_DK, GLA_DV)
        dm = t["dm"].reshape(GLA_H, G, GLA_DK, GLA_DV)
        dsc = ds_scr[...]
        dsn = [None] * G
        for c in (range(G) if rev else range(G - 1, -1, -1)):
            dsn[c] = dsc
            dsc = dm[:, c] * dsc + cc[:, c]
        ds_scr[...] = dsc
        dsn = jnp.stack(dsn, axis=1).reshape(B, GLA_DK, GLA_DV)
        dsnb = dsn.astype(BF16)
        dv = dv + _bdot(keb, dsnb, (2,), (1,))
        dke = _bdot(vv, dsnb, (2,), (2,))
        ddrow = _bdot(jnp.ones((B, CHUNK, GLA_DV), F32), dsn * sp, (2,), (2,), HIGHEST)
        dbt = ddrow * jnp.exp(t["bt"]) + jnp.sum(dke * ke, axis=1, keepdims=True)
        db = dqd * qd - dki * ki - dke * ke
        dq_ref[...] = (dqd * t["eq"]).reshape(GLA_H, tg, GLA_DK)
        dk_ref[...] = (dki * t["eki"] + dke * t["eke"]).reshape(GLA_H, tg, GLA_DK)
        dg_ref[...] = (_bdot(t["tb"], db, (1,), (1,), HIGHEST) + dbt).reshape(GLA_H, tg, GLA_DK)
        dv_ref[...] = dv.reshape(GLA_H, tg, GLA_DV)

    blk = (lambda i: (0, i, 0)) if rev else (lambda i: (0, ng - 1 - i, 0))
    sblk = (lambda i: (i, 0, 0, 0)) if rev else (lambda i: (ng - 1 - i, 0, 0, 0))
    h64 = pl.BlockSpec((GLA_H, tg, 64), blk)
    h128 = pl.BlockSpec((GLA_H, tg, 128), blk)
    return pl.pallas_call(
        body, name=name, grid=(ng,),
        in_specs=[h64, h64, h64, h128, pl.BlockSpec((G, GLA_H, GLA_DK, GLA_DV), sblk), h128],
        out_specs=[h64, h64, h64, h128],
        out_shape=[SDS((GLA_H, Lp, 64), F32), SDS((GLA_H, Lp, 64), F32), SDS((GLA_H, Lp, 64), F32),
                   SDS((GLA_H, Lp, GLA_DV), F32)],
        scratch_shapes=[pltpu.VMEM((GLA_H, GLA_DK, GLA_DV), F32)], compiler_params=_cp(True))(q, k, g, v, ss, do)


def _gla_post(name, of, ob, H, gn):
    Lp = H.shape[0]
    tm = _rt(Lp, 384)

    def body(of_ref, ob_ref, r_ref, gn_ref, a_ref, at_ref):
        parts = []
        for hh in range(GLA_H):
            o = of_ref[hh] + ob_ref[hh]
            parts.append(o * lax.rsqrt(jnp.mean(o * o, axis=-1, keepdims=True) + EPS))
        rv = r_ref[...]
        a = (jnp.concatenate(parts, axis=1) * gn_ref[...]) * (rv * jax.nn.sigmoid(rv))
        a_ref[...] = a.astype(BF16)
        at_ref[...] = a.T.astype(BF16)

    h128 = pl.BlockSpec((GLA_H, tm, 128), lambda i: (0, i, 0))
    return pl.pallas_call(
        body, name=name, grid=(Lp // tm,),
        in_specs=[h128, h128, pl.BlockSpec((tm, 512), lambda i: (i, C_RA // 512)), pl.BlockSpec((1, 512), lambda i: (0, 0))],
        out_specs=[pl.BlockSpec((tm, 512), lambda i: (i, 0)), pl.BlockSpec((512, tm), lambda i: (0, i))],
        out_shape=[SDS((Lp, 512), BF16), SDS((512, Lp), BF16)])(of, ob, H, gn)


def _gla_post_bwd(name, of, ob, H, gn, da, dH):
    Lp = H.shape[0]
    tm = _rt(Lp, 384)

    def body(of_ref, ob_ref, r_ref, gn_ref, da_ref, dh_in, do_ref, dh_ref, dgn_ref):
        rv = r_ref[...]
        sg = jax.nn.sigmoid(rv)
        dav = da_ref[...]
        gnv = gn_ref[...]
        ons, rs = [], []
        for hh in range(GLA_H):
            o = of_ref[hh] + ob_ref[hh]
            r = lax.rsqrt(jnp.mean(o * o, axis=-1, keepdims=True) + EPS)
            rs.append(r)
            ons.append(o * r)
        on = jnp.concatenate(ons, axis=1)
        dw = dav * (rv * sg)
        dh_ref[...] = (dav * (on * gnv) * (sg * (1.0 + rv * (1.0 - sg)))).astype(BF16)

        @pl.when(pl.program_id(0) == 0)
        def _():
            dgn_ref[...] = jnp.zeros_like(dgn_ref)

        dgn_ref[...] += jnp.sum(dw * on, axis=0, keepdims=True)
        don = dw * gnv
        for hh in range(GLA_H):
            dd = don[:, 128 * hh:128 * hh + 128]
            do_ref[hh] = rs[hh] * (dd - ons[hh] * jnp.mean(dd * ons[hh], axis=-1, keepdims=True))

    h128 = pl.BlockSpec((GLA_H, tm, 128), lambda i: (0, i, 0))
    rblk = pl.BlockSpec((tm, 512), lambda i: (i, C_RA // 512))
    vec = pl.BlockSpec((1, 512), lambda i: (0, 0))
    return pl.pallas_call(
        body, name=name, grid=(Lp // tm,),
        in_specs=[h128, h128, rblk, vec, pl.BlockSpec((tm, 512), lambda i: (i, 0)), pl.BlockSpec(memory_space=pl.ANY)],
        out_specs=[h128, rblk, vec],
        out_shape=[SDS((GLA_H, Lp, 128), F32), SDS(dH.shape, BF16), SDS((1, 512), F32)],
        input_output_aliases={5: 1})(of, ob, H, gn, da, dH)


def _gla_qkv_bwd(name, dqf, dqb, dkf, dkb, dvf, dvb, dH):
    Lp = dqf.shape[1]
    tm = _rt(Lp, 384)

    def body(dqf_ref, dqb_ref, dkf_ref, dkb_ref, dvf_ref, dvb_ref, dh_in, dh_ref):
        valid = _row_ids(pl.program_id(0), tm) >= PAD
        for hh in range(GLA_H):
            dq = (dqf_ref[hh] + dqb_ref[hh]) * (GLA_DK ** -0.5)
            dh_ref[:, 64 * hh:64 * hh + 64] = jnp.where(valid, dq, 0.0).astype(BF16)
            dh_ref[:, 256 + 64 * hh:256 + 64 * hh + 64] = jnp.where(valid, dkf_ref[hh] + dkb_ref[hh], 0.0).astype(BF16)
            dh_ref[:, 512 + 128 * hh:512 + 128 * hh + 128] = jnp.where(valid, dvf_ref[hh] + dvb_ref[hh], 0.0).astype(BF16)

    h64 = pl.BlockSpec((GLA_H, tm, 64), lambda i: (0, i, 0))
    h128 = pl.BlockSpec((GLA_H, tm, 128), lambda i: (0, i, 0))
    return pl.pallas_call(
        body, name=name, grid=(Lp // tm,),
        in_specs=[h64, h64, h64, h64, h128, h128, pl.BlockSpec(memory_space=pl.ANY)],
        out_specs=pl.BlockSpec((tm, 1024), lambda i: (i, C_QA // 1024)), out_shape=SDS(dH.shape, BF16),
        input_output_aliases={6: 0})(dqf, dqb, dkf, dkb, dvf, dvb, dH)


def _gla_gate_bwd(name, H, w2p, b2p, dgf, dgb, dH):
    Lp = H.shape[0]
    tm = _rt(Lp, 384)

    def body(lr_ref, w_ref, b_ref, dgf_ref, dgb_ref, dh_in, dh_ref, dw_ref, db_ref, dg_scr):
        valid = _row_ids(pl.program_id(0), tm) >= PAD
        for hh in range(GLA_H):
            dg_scr[:, 64 * hh:64 * hh + 64] = dgf_ref[hh]
            dg_scr[:, 256 + 64 * hh:256 + 64 * hh + 64] = dgb_ref[hh]
        lrb = lr_ref[...].astype(BF16)
        wv = w_ref[...]
        pre = jnp.dot(lrb, wv, preferred_element_type=F32) + b_ref[...]
        dpre = jnp.where(valid, dg_scr[...] * (1.0 / GLA_TAU) * jax.nn.sigmoid(-pre), 0.0)
        dpb = dpre.astype(BF16)
        dh_ref[...] = lax.dot_general(dpb, wv, NT, preferred_element_type=F32).astype(BF16)

        @pl.when(pl.program_id(0) == 0)
        def _():
            dw_ref[...] = jnp.zeros_like(dw_ref)
            db_ref[...] = jnp.zeros_like(db_ref)

        dw_ref[...] += lax.dot_general(lrb, dpb, TN, preferred_element_type=F32)
        db_ref[...] += jnp.sum(dpre, axis=0, keepdims=True)

    h64 = pl.BlockSpec((GLA_H, tm, 64), lambda i: (0, i, 0))
    lrblk = pl.BlockSpec((tm, LANE), lambda i: (i, C_LR // LANE))
    wblk = pl.BlockSpec((LANE, 512), lambda i: (0, 0))
    vec = pl.BlockSpec((1, 512), lambda i: (0, 0))
    return pl.pallas_call(
        body, name=name, grid=(Lp // tm,),
        in_specs=[lrblk, wblk, vec, h64, h64, pl.BlockSpec(memory_space=pl.ANY)],
        out_specs=[lrblk, wblk, vec],
        out_shape=[SDS(dH.shape, BF16), SDS((LANE, 512), F32), SDS((1, 512), F32)],
        scratch_shapes=[pltpu.VMEM((tm, 512), F32)], input_output_aliases={5: 0})(H, w2p, b2p, dgf, dgb, dH)


def _swap16(x):
    n = x.shape[1]
    lane = lax.broadcasted_iota(jnp.int32, x.shape, 1)
    return jnp.where(lane % 32 < 16, pltpu.roll(x, n - 16, 1), pltpu.roll(x, 16, 1))


def _head_mean(v, bd):
    hi = v.astype(BF16)
    lo = (v - hi.astype(F32)).astype(BF16)
    bdb = bd.astype(BF16)
    return jnp.dot(hi, bdb, preferred_element_type=F32) + jnp.dot(lo, bdb, preferred_element_type=F32)


def _headnorm_rope(x, gain, cos, sin, bd):
    r = lax.rsqrt(_head_mean(x * x, bd) + EPS)
    xh = x * r
    xn = xh * gain
    return xn * cos + _swap16(xn) * sin, xh, r


def _headnorm_rope_bwd(dxr, xh, r, gain, cos, sin, bd):
    dxn = cos * dxr + _swap16(sin * dxr)
    u = dxn * gain
    dx = r * (u - xh * _head_mean(u * xh, bd))
    return dx, jnp.sum(dxn * xh, axis=0, keepdims=True)


def _att_prep(name, H, gq, gk, cos, sin, bd):
    Lp = H.shape[0]
    tm = _rt(Lp, 384)

    def body(q_ref, kv_ref, gq_ref, gk_ref, c_ref, s_ref, bd_ref, qp_ref, k_ref, v_ref):
        c1, s1 = c_ref[...], s_ref[...]
        c4, s4 = jnp.concatenate([c1] * 4, axis=1), jnp.concatenate([s1] * 4, axis=1)
        xr, _, _ = _headnorm_rope(q_ref[...], gq_ref[...], c4, s4, bd_ref[...])
        xr = xr * (HD ** -0.5)
        lane = lax.broadcasted_iota(jnp.int32, (tm, LANE), 1)
        for hh in range(ATT_H):
            grp = xr[:, LANE * (hh // 2):LANE * (hh // 2) + LANE]
            e, gi = hh % 2, hh // 4
            if e != gi:
                grp = pltpu.roll(grp, 64, 1)
            keep = (lane < 64) if gi == 0 else (lane >= 64)
            qp_ref[hh] = jnp.where(keep, grp, 0.0).astype(BF16)
        kv = kv_ref[...]
        kr, _, _ = _headnorm_rope(kv[:, :LANE], gk_ref[...], c1, s1, bd_ref[0:LANE, 0:LANE])
        k_ref[...] = kr.astype(BF16)
        v_ref[...] = kv[:, LANE:].astype(BF16)

    row128 = pl.BlockSpec((tm, LANE), lambda i: (i, 0))
    return pl.pallas_call(
        body, name=name, grid=(Lp // tm,),
        in_specs=[pl.BlockSpec((tm, 512), lambda i: (i, C_QB // 512)), pl.BlockSpec((tm, 256), lambda i: (i, C_KB // 256)),
                  pl.BlockSpec((1, 512), lambda i: (0, 0)), pl.BlockSpec((1, LANE), lambda i: (0, 0)), row128, row128,
                  pl.BlockSpec((512, 512), lambda i: (0, 0))],
        out_specs=[pl.BlockSpec((ATT_H, tm, LANE), lambda i: (0, i, 0)), row128, row128],
        out_shape=[SDS((ATT_H, Lp, LANE), BF16), SDS((Lp, LANE), BF16), SDS((Lp, LANE), BF16)])(H, H, gq, gk, cos, sin, bd)


def _att_q_bwd(name, H, gq, cos, sin, bd, dqp, dH):
    Lp = H.shape[0]
    tm = _rt(Lp, 384)

    def body(q_ref, gq_ref, c_ref, s_ref, bd_ref, dqp_ref, dh_in, dh_ref, dg_ref):
        c1, s1 = c_ref[...], s_ref[...]
        c4, s4 = jnp.concatenate([c1] * 4, axis=1), jnp.concatenate([s1] * 4, axis=1)
        gqv = gq_ref[...]
        _, xh, r = _headnorm_rope(q_ref[...], gqv, c4, s4, bd_ref[...])
        lane = lax.broadcasted_iota(jnp.int32, (tm, LANE), 1)
        groups = []
        for j in range(ATT_H // 2):
            pieces = []
            for e in range(2):
                hh = 2 * j + e
                piece = dqp_ref[hh]
                if e != hh // 4:
                    piece = pltpu.roll(piece, 64, 1)
                pieces.append(piece)
            groups.append(jnp.where(lane < 64, pieces[0], pieces[1]))
        dxr = jnp.concatenate(groups, axis=1) * (HD ** -0.5)
        dx, dg = _headnorm_rope_bwd(dxr, xh, r, gqv, c4, s4, bd_ref[...])
        dh_ref[...] = dx.astype(BF16)

        @pl.when(pl.program_id(0) == 0)
        def _():
            dg_ref[...] = jnp.zeros_like(dg_ref)

        dg_ref[...] += dg

    row128 = pl.BlockSpec((tm, LANE), lambda i: (i, 0))
    qblk = pl.BlockSpec((tm, 512), lambda i: (i, C_QB // 512))
    vec = pl.BlockSpec((1, 512), lambda i: (0, 0))
    return pl.pallas_call(
        body, name=name, grid=(Lp // tm,),
        in_specs=[qblk, vec, row128, row128, pl.BlockSpec((512, 512), lambda i: (0, 0)),
                  pl.BlockSpec((ATT_H, tm, LANE), lambda i: (0, i, 0)), pl.BlockSpec(memory_space=pl.ANY)],
        out_specs=[qblk, vec], out_shape=[SDS(dH.shape, BF16), SDS((1, 512), F32)],
        input_output_aliases={6: 0})(H, gq, cos, sin, bd, dqp, dH)


def _att_kv_bwd(name, H, gk, cos, sin, bd, dkr, dvb, dH):
    Lp = H.shape[0]
    tm = _rt(Lp, 384)

    def body(kv_ref, gk_ref, c_ref, s_ref, bd_ref, dk_ref, dv_ref, dh_in, dh_ref, dg_ref):
        c1, s1 = c_ref[...], s_ref[...]
        gkv = gk_ref[...]
        bdv = bd_ref[0:LANE, 0:LANE]
        _, xh, r = _headnorm_rope(kv_ref[:, :LANE], gkv, c1, s1, bdv)
        dx, dg = _headnorm_rope_bwd(dk_ref[...], xh, r, gkv, c1, s1, bdv)
        dh_ref[:, :LANE] = dx.astype(BF16)
        dh_ref[:, LANE:] = dv_ref[...].astype(BF16)

        @pl.when(pl.program_id(0) == 0)
        def _():
            dg_ref[...] = jnp.zeros_like(dg_ref)

        dg_ref[...] += dg

    row128 = pl.BlockSpec((tm, LANE), lambda i: (i, 0))
    kvblk = pl.BlockSpec((tm, 256), lambda i: (i, C_KB // 256))
    vec = pl.BlockSpec((1, LANE), lambda i: (0, 0))
    return pl.pallas_call(
        body, name=name, grid=(Lp // tm,),
        in_specs=[kvblk, vec, row128, row128, pl.BlockSpec((512, 512), lambda i: (0, 0)), row128, row128,
                  pl.BlockSpec(memory_space=pl.ANY)],
        out_specs=[kvblk, vec], out_shape=[SDS(dH.shape, BF16), SDS((1, LANE), F32)],
        input_output_aliases={7: 0})(H, gk, cos, sin, bd, dkr, dvb, dH)


def _ridden_call(core, rider, *, name, grid, in_specs, out_specs, out_shape, scratch_shapes, args):
    n_in, n_out, n_scr = len(in_specs), len(out_specs), len(scratch_shapes)
    r_in = len(rider.ins) if rider else 0
    r_out = len(rider.out_shape) if rider else 0
    total = int(np.prod(grid))
    mid_step = max(total - 1 - RIDER_MID_BACK, 0)

    def body(*refs):
        ins, r_ins = refs[:n_in], refs[n_in:n_in + r_in]
        o0 = n_in + r_in
        outs, r_outs = refs[o0:o0 + n_out], refs[o0 + n_out:o0 + n_out + r_out]
        s0 = o0 + n_out + r_out
        scr, r_sems = refs[s0:s0 + n_scr], refs[s0 + n_scr:]
        if rider is None:
            core(ins, outs, scr)
            return
        step = pl.program_id(0)
        for ax in range(1, len(grid)):
            step = step * grid[ax] + pl.program_id(ax)
        start, mid, end = rider.hooks(r_ins, r_outs, r_sems)
        pl.when(step == 0)(start)
        core(ins, outs, scr)
        pl.when(step == mid_step)(mid)
        pl.when(step == total - 1)(end)

    any_spec = pl.BlockSpec(memory_space=pl.ANY)
    res = pl.pallas_call(
        body, name=name, grid=grid, in_specs=list(in_specs) + [any_spec] * r_in, out_specs=list(out_specs) + [any_spec] * r_out,
        out_shape=list(out_shape) + (list(rider.out_shape) if rider else []),
        scratch_shapes=list(scratch_shapes) + (list(rider.sems) if rider else []),
        input_output_aliases={n_in + a: n_out + b for a, b in rider.aliases.items()} if rider else {},
        compiler_params=_cp(True))(*args, *(rider.ins if rider else []))
    return res[:n_out], res[n_out:]


def _att_fwd(name, qp, kr, vb, bias, rider=None):
    Lp = kr.shape[0]
    tq = _rt(Lp, 384)
    sub = ATT_SUB_FWD if tq % ATT_SUB_FWD == 0 else tq

    def core(ins, outs, scr):
        q_ref, k_ref, v_ref, bias_ref = ins
        b_ref, bt_ref, lse_ref = outs
        j, i = pl.program_id(0), pl.program_id(1)
        valid = _row_ids(i, tq) >= PAD
        kk, vv, bb = k_ref[...], v_ref[...], bias_ref[...]
        chains = [(e, r) for e in range(2) for r in range(tq // sub)]
        ss = [lax.dot_general(q_ref[e, r * sub:(r + 1) * sub, :], kk, NT, preferred_element_type=F32) for e, r in chains]
        ps, ls = [], []
        for (e, r), s in zip(chains, ss):
            s = jnp.concatenate([s[:, :LANE] + bb[:, :LANE], s[:, LANE:]], axis=1)
            m = jnp.max(s, axis=-1, keepdims=True)
            p = jnp.exp(s - m)
            l = jnp.sum(p, axis=-1, keepdims=True)
            ps.append(p.astype(BF16))
            ls.append(l)
            lse_ref[e, r * sub:(r + 1) * sub, :] = m + jnp.log(l)
        os_ = [jnp.dot(p, vv, preferred_element_type=F32) * (1.0 / l) for p, l in zip(ps, ls)]
        n_sub = tq // sub
        outs = [jnp.where(valid, jnp.concatenate(os_[e * n_sub:(e + 1) * n_sub], axis=0), 0.0) for e in range(2)]
        lane = lax.broadcasted_iota(jnp.int32, (tq, LANE), 1)
        low = j < 2
        o0 = jnp.where(low, outs[0], pltpu.roll(outs[0], 64, 1))
        o1 = jnp.where(low, pltpu.roll(outs[1], 64, 1), outs[1])
        blk = jnp.where(lane < 64, o0, o1)
        b_ref[...] = blk.astype(BF16)
        bt_ref[...] = blk.T.astype(BF16)

    full = pl.BlockSpec((Lp, LANE), lambda j, i: (0, 0))
    return _ridden_call(
        core, rider, name=name, grid=(ATT_H // 2, Lp // tq),
        in_specs=[pl.BlockSpec((2, tq, LANE), lambda j, i: (j, i, 0)), full, full, pl.BlockSpec((1, Lp), lambda j, i: (0, 0))],
        out_specs=[pl.BlockSpec((tq, LANE), lambda j, i: (i, j)), pl.BlockSpec((LANE, tq), lambda j, i: (j, i)),
                   pl.BlockSpec((2, tq, 1), lambda j, i: (j, i, 0))],
        out_shape=[SDS((Lp, 512), BF16), SDS((512, Lp), BF16), SDS((ATT_H, Lp, 1), F32)],
        scratch_shapes=[], args=(qp, kr, vb, bias))


def _att_bwd(name, qp, kr, vb, bias, lse, db, b, rider=None):
    Lp = kr.shape[0]
    tq = _rt(Lp, 384)
    sub = ATT_SUB_BWD if tq % ATT_SUB_BWD == 0 else tq
    nq = Lp // tq

    def core(ins, outs, scr):
        q_ref, k_ref, v_ref, bias_ref, lse_ref, db_ref, b_ref = ins
        dq_ref, dk_ref, dv_ref = outs
        dkt_scr, dvt_scr = scr
        j, i = pl.program_id(0), pl.program_id(1)

        @pl.when((j == 0) & (i == 0))
        def _():
            dkt_scr[...] = jnp.zeros_like(dkt_scr)
            dvt_scr[...] = jnp.zeros_like(dvt_scr)

        kk, vv, bb = k_ref[...], v_ref[...], bias_ref[...]
        dbv = db_ref[...]
        rolled = pltpu.roll(dbv, 64, 1)
        lane = lax.broadcasted_iota(jnp.int32, (tq, LANE), 1)
        low = j < 2
        first = jnp.where(low, 0, 64)
        keep = (lane >= first) & (lane < first + 64)
        prod = dbv * b_ref[...].astype(F32)
        deltas = [jnp.sum(jnp.where(lane < 64, prod, 0.0), axis=-1, keepdims=True),
                  jnp.sum(jnp.where(lane >= 64, prod, 0.0), axis=-1, keepdims=True)]
        dkt, dvt = 0.0, 0.0
        for e in range(2):
            src = jnp.where(low, dbv, rolled) if e == 0 else jnp.where(low, rolled, dbv)
            dop = jnp.where(keep, src, 0.0).astype(BF16)
            pbs, dss = [], []
            for r in range(tq // sub):
                rs = slice(r * sub, (r + 1) * sub)
                s = lax.dot_general(q_ref[e, rs, :], kk, NT, preferred_element_type=F32)
                s = jnp.concatenate([s[:, :LANE] + bb[:, :LANE], s[:, LANE:]], axis=1)
                p = jnp.exp(s - lse_ref[e, rs, :])
                dp = lax.dot_general(dop[rs], vv, NT, preferred_element_type=F32)
                ds = (p * (dp - deltas[e][rs])).astype(BF16)
                dq_ref[e, rs, :] = jnp.where(keep[rs], jnp.dot(ds, kk, preferred_element_type=F32), 0.0)
                pbs.append(p.astype(BF16))
                dss.append(ds)
            dkt = dkt + lax.dot_general(q_ref[e], jnp.concatenate(dss, axis=0), TN, preferred_element_type=F32)
            dvt = dvt + lax.dot_general(dop, jnp.concatenate(pbs, axis=0), TN, preferred_element_type=F32)
        dkt_scr[...] += dkt
        dvt_scr[...] += dvt

        @pl.when((j == ATT_H // 2 - 1) & (i == nq - 1))
        def _():
            dk_ref[...] = dkt_scr[...].T
            dv_ref[...] = dvt_scr[...].T

    full = pl.BlockSpec((Lp, LANE), lambda j, i: (0, 0))
    pair = pl.BlockSpec((2, tq, LANE), lambda j, i: (j, i, 0))
    return _ridden_call(
        core, rider, name=name, grid=(ATT_H // 2, nq),
        in_specs=[pair, full, full, pl.BlockSpec((1, Lp), lambda j, i: (0, 0)), pl.BlockSpec((2, tq, 1), lambda j, i: (j, i, 0)),
                  pl.BlockSpec((tq, LANE), lambda j, i: (i, j)), pl.BlockSpec((tq, LANE), lambda j, i: (i, j))],
        out_specs=[pair, full, full],
        out_shape=[SDS((ATT_H, Lp, LANE), F32), SDS((Lp, LANE), F32), SDS((Lp, LANE), F32)],
        scratch_shapes=[pltpu.VMEM((LANE, Lp), F32), pltpu.VMEM((LANE, Lp), F32)], args=(qp, kr, vb, bias, lse, db, b))


def _rope_tables(n_tok):
    rows = n_tok // GRID_W
    row = jnp.repeat(jnp.arange(rows), GRID_W).astype(F32)
    col = jnp.tile(jnp.arange(GRID_W), rows).astype(F32)
    axis_dim = HD // 2
    inv = ROPE_THETA ** (-jnp.arange(0, axis_dim, 2, dtype=F32) / axis_dim)
    ang = jnp.concatenate([row[:, None] * inv, col[:, None] * inv], axis=-1)
    ang = jnp.concatenate([jnp.zeros((LANE, axis_dim), F32), ang], axis=0)
    c, s = jnp.cos(ang), jnp.sin(ang)
    c64 = jnp.concatenate([c[:, :16], c[:, :16], c[:, 16:], c[:, 16:]], axis=1)
    s64 = jnp.concatenate([-s[:, :16], s[:, :16], -s[:, 16:], s[:, 16:]], axis=1)
    return jnp.concatenate([c64, c64], axis=1), jnp.concatenate([s64, s64], axis=1)


def _ffn_fwd(tag, h, W, l, j):
    Lp = h.shape[0]
    tm = _rt(Lp, 384)
    g = W["norm_gains"][l, 2 * j].reshape(1, D)

    def body(h_ref, g_ref, wg_ref, wu_ref, wd_ref, h2_ref, a_ref, b_ref, st_ref, nt_ref, pad_scr):
        x = h_ref[...]
        y = x * lax.rsqrt(jnp.mean(x * x, axis=-1, keepdims=True) + EPS) * g_ref[...]
        nt_ref[...] = y.T.astype(BF16)
        nv = y.astype(BF16)
        pad_scr[:, FF_S:] = jnp.zeros((tm, FF_P - FF_S), F32)
        acc = 0.0
        for s in range(NSH):
            a = jnp.dot(nv, wg_ref[s], preferred_element_type=F32)
            b = jnp.dot(nv, wu_ref[s], preferred_element_type=F32)
            sv = a * jax.nn.sigmoid(a) * b
            a_ref[s, :, :FF_S] = a.astype(BF16)
            a_ref[s, :, FF_S:] = jnp.zeros((tm, FF_P - FF_S), BF16)
            b_ref[s, :, :FF_S] = b.astype(BF16)
            b_ref[s, :, FF_S:] = jnp.zeros((tm, FF_P - FF_S), BF16)
            pad_scr[:, :FF_S] = sv
            st_ref[s] = pad_scr[...].T.astype(BF16)
            acc = acc + jnp.dot(sv.astype(BF16), wd_ref[s], preferred_element_type=F32)
        h2_ref[...] = x + 0.5 * acc

    once = dict(pipeline_mode=pl.Buffered(1))
    wup = pl.BlockSpec((NSH, None, D, FF_S), lambda i: (0, l, j, 0), **once)
    row = pl.BlockSpec((tm, D), lambda i: (i, 0))
    slab = pl.BlockSpec((NSH, tm, FF_P), lambda i: (0, i, 0))
    h2, a, b, st, nt = pl.pallas_call(
        body, name=f"ffn_{tag}", grid=(Lp // tm,),
        in_specs=[row, pl.BlockSpec((1, D), lambda i: (0, 0)), wup, wup,
                  pl.BlockSpec((NSH, None, FF_S, D), lambda i: (0, l, j, 0), **once)],
        out_specs=[row, slab, slab, pl.BlockSpec((NSH, FF_P, tm), lambda i: (0, 0, i)), pl.BlockSpec((D, tm), lambda i: (0, i))],
        out_shape=[SDS((Lp, D), F32), SDS((NSH, Lp, FF_P), BF16), SDS((NSH, Lp, FF_P), BF16), SDS((NSH, FF_P, Lp), BF16),
                   SDS((D, Lp), BF16)],
        scratch_shapes=[pltpu.VMEM((tm, FF_P), F32)], compiler_params=_cp(True))(h, g, W["gate"], W["up"], W["down"])
    return h2, dict(h=h, g=g, nt=nt, a=a, b=b, st=st)


def _ffn_bwd(tag, dh, sv, W, G, l, j, rider=None):
    Lp = dh.shape[0]
    tn = 512
    (da, db, dh_prev, dg), carried = _ffn_bwd_acts(tag, dh, sv, W, l, j, rider)
    G["down"] = _mm(f"bwd_{tag}", sv["st"], dh, grid=(NSH, D // tn),
                    a_spec=pl.BlockSpec((None, FF_P, Lp), lambda s, n: (s, 0, 0)), a_sl=(slice(0, FF_S), slice(None)),
                    b_spec=pl.BlockSpec((Lp, tn), lambda s, n: (0, n)),
                    o_spec=pl.BlockSpec((None, None, FF_S, tn), lambda s, n: (s, j, 0, n)), o_shape=(NSH, 2, FF_S, D), o_dtype=F32,
                    dims=NN, acc_shape=None, scale=0.5, into=G.get("down"))
    for key, dact in (("gate", da), ("up", db)):
        G[key] = _mm(f"bw{key[0]}_{tag}", sv["nt"], dact, grid=(NSH, D // tn),
                     a_spec=pl.BlockSpec((tn, Lp), lambda s, m: (m, 0)),
                     b_spec=pl.BlockSpec((None, Lp, FF_P), lambda s, m: (s, 0, 0)), b_sl=(slice(None), slice(0, FF_S)),
                     o_spec=pl.BlockSpec((None, None, tn, FF_S), lambda s, m: (s, j, m, 0)), o_shape=(NSH, 2, D, FF_S),
                     o_dtype=F32, dims=NN, acc_shape=None, into=G.get(key))
    return dh_prev, dg, carried


def _ffn_bwd_acts(tag, dh, sv, W, l, j, rider=None):
    Lp = dh.shape[0]
    tm = _rt(Lp, 384)

    def core(ins, outs, scr):
        dh_ref, h_ref, g_ref, wd_ref, wg_ref, wu_ref, a_ref, b_ref = ins
        da_ref, db_ref, dho_ref, dg_ref = outs
        dhv = dh_ref[...]
        dhb = dhv.astype(BF16)
        dn = 0.0
        for s in range(NSH):
            ds = 0.5 * lax.dot_general(dhb, wd_ref[s], NT, preferred_element_type=F32)
            av = a_ref[s, :, :FF_S].astype(F32)
            bv = b_ref[s, :, :FF_S].astype(F32)
            sg = jax.nn.sigmoid(av)
            da = (ds * bv * (sg * (1.0 + av * (1.0 - sg)))).astype(BF16)
            db = (ds * (av * sg)).astype(BF16)
            da_ref[s, :, :FF_S] = da
            da_ref[s, :, FF_S:] = jnp.zeros((tm, FF_P - FF_S), BF16)
            db_ref[s, :, :FF_S] = db
            db_ref[s, :, FF_S:] = jnp.zeros((tm, FF_P - FF_S), BF16)
            dn = dn + (lax.dot_general(da, wg_ref[s], NT, preferred_element_type=F32)
                       + lax.dot_general(db, wu_ref[s], NT, preferred_element_type=F32))
        x = h_ref[...]
        r = lax.rsqrt(jnp.mean(x * x, axis=-1, keepdims=True) + EPS)
        xh = x * r
        u = dn * g_ref[...]
        dho_ref[...] = dhv + r * (u - xh * jnp.mean(u * xh, axis=-1, keepdims=True))

        @pl.when(pl.program_id(0) == 0)
        def _():
            dg_ref[...] = jnp.zeros_like(dg_ref)

        dg_ref[...] += jnp.sum(dn * xh, axis=0, keepdims=True)

    once = dict(pipeline_mode=pl.Buffered(1))
    wup = pl.BlockSpec((NSH, None, D, FF_S), lambda i: (0, l, j, 0), **once)
    row = pl.BlockSpec((tm, D), lambda i: (i, 0))
    slab = pl.BlockSpec((NSH, tm, FF_P), lambda i: (0, i, 0))
    vec = pl.BlockSpec((1, D), lambda i: (0, 0))
    return _ridden_call(
        core, rider, name=f"bffn_{tag}", grid=(Lp // tm,),
        in_specs=[row, row, vec, pl.BlockSpec((NSH, None, FF_S, D), lambda i: (0, l, j, 0), **once), wup, wup, slab, slab],
        out_specs=[slab, slab, row, vec],
        out_shape=[SDS((NSH, Lp, FF_P), BF16), SDS((NSH, Lp, FF_P), BF16), SDS((Lp, D), F32), SDS((1, D), F32)],
        scratch_shapes=[], args=(dh, sv["h"], sv["g"], W["down"], W["gate"], W["up"], sv["a"], sv["b"]))


def _mixer_fwd(tag, h, W, winp, C, l, G_chunks, rider=None):
    Lp = h.shape[0]
    tm = _rt(Lp, 1408)
    g = W["norm_gains"][l, 1].reshape(1, D)
    z, zt = _rms_fwd(f"rmsf_{tag}", h, g)
    tn = 896
    H = _mm(f"win_{tag}", z, winp, grid=(HP // tn, Lp // tm), a_spec=pl.BlockSpec((tm, D), lambda n, i: (i, 0)),
            b_spec=pl.BlockSpec((D, tn), lambda n, i: (0, n)), o_spec=pl.BlockSpec((tm, tn), lambda n, i: (i, n)),
            o_shape=(Lp, HP), o_dtype=F32, dims=NN, acc_shape=None)
    w2p, b2p = C["w2p"][l], C["b2p"][l]
    qh, kh, vh, gf, gb = _gla_prep(f"glap_{tag}", H, w2p, b2p)
    of, sf = _gla_fwd(f"glaf_{tag}", qh, kh, gf, vh, False, G_chunks)
    ob, sb = _gla_fwd(f"glar_{tag}", qh, kh, gb, vh, True, G_chunks)
    gn = W["gn"][l].reshape(1, 512)
    a, at = _gla_post(f"glao_{tag}", of, ob, H, gn)
    gq, gk = C["gq"][l], C["gk"][l]
    qp, kr, vb = _att_prep(f"attp_{tag}", H, gq, gk, C["cos"], C["sin"], C["bd"])
    (b, bt, lse), carried = _att_fwd(f"attf_{tag}", qp, kr, vb, C["bias"], rider)
    if rider is not None:
        W = rider.updated(W, carried)
    row = pl.BlockSpec((tm, D), lambda i: (i, 0))
    proj = functools.partial(
        _mm, grid=(Lp // tm,), a_spec=pl.BlockSpec((tm, 512), lambda i: (i, 0)),
        b_spec=pl.BlockSpec((NSH, None, 512, 256), lambda i: (0, l, 0, 0)), o_spec=row, o_shape=(Lp, D), o_dtype=F32, dims=NN,
        acc_shape=None, shards="cols")
    pa = proj(f"pa_{tag}", a, W["wpa"])
    pb = proj(f"pb_{tag}", b, W["wpb"])
    bm = W["bm"][l]
    y, yt = _merge_fwd(f"mrg_{tag}", H, pa, pb, bm)
    h2 = _mm(f"wout_{tag}", y, W["wout"], grid=(Lp // tm,), a_spec=row,
             b_spec=pl.BlockSpec((NSH, None, 256, D), lambda i: (0, l, 0, 0)), o_spec=row, o_shape=(Lp, D), o_dtype=F32,
             dims=NN, acc_shape=None, res=h, res_spec=row, shards="rows")
    sv = dict(h=h, g=g, zt=zt, H=H, w2p=w2p, b2p=b2p, qh=qh, kh=kh, vh=vh, gf=gf, gb=gb, of=of, ob=ob, sf=sf, sb=sb, gn=gn,
              at=at, gq=gq, gk=gk, qp=qp, kr=kr, vb=vb, b=b, bt=bt, lse=lse, pa=pa, pb=pb, bm=bm, yt=yt, winp=winp)
    return h2, sv, W


def _mixer_bwd(tag, dh, sv, W, C, G, S, l, G_chunks, rider=None):
    Lp = dh.shape[0]
    tm = _rt(Lp, 1408)
    H = sv["H"]
    row = pl.BlockSpec((tm, D), lambda i: (i, 0))
    dy = _mm(f"bdy_{tag}", dh, W["wout"], grid=(Lp // tm,), a_spec=row,
             b_spec=pl.BlockSpec((NSH, None, 256, D), lambda i: (0, l, 0, 0)), o_spec=row, o_shape=(Lp, D), o_dtype=F32, dims=NT,
             acc_shape=None, shards="rows")
    tn = 512
    G["wout"] = _mm(f"bwo_{tag}", sv["yt"], dh, grid=(NSH, D // tn), a_spec=pl.BlockSpec((256, Lp), lambda s, n: (s, 0)),
                    b_spec=pl.BlockSpec((Lp, tn), lambda s, n: (0, n)),
                    o_spec=pl.BlockSpec((None, 256, tn), lambda s, n: (s, 0, n)), o_shape=(NSH, 256, D), o_dtype=F32, dims=NN,
                    acc_shape=None)
    dpa, dpb, dH, S["bm"][l] = _merge_bwd(f"bmrg_{tag}", H, sv["pa"], sv["pb"], sv["bm"], dy)
    dbranch = {}
    for key, dp, xt in (("wpa", dpa, sv["at"]), ("wpb", dpb, sv["bt"])):
        dbranch[key] = _mm(f"bx{key[2]}_{tag}", dp, W[key], grid=(Lp // tm,), a_spec=row,
                           b_spec=pl.BlockSpec((NSH, None, 512, 256), lambda i: (0, l, 0, 0)),
                           o_spec=pl.BlockSpec((tm, 512), lambda i: (i, 0)), o_shape=(Lp, 512), o_dtype=F32, dims=NT,
                           acc_shape=None, shards="cols")
        G[key] = _mm(f"bw{key[2]}_{tag}", xt, dp, grid=(NSH,), a_spec=pl.BlockSpec((512, Lp), lambda s: (0, 0)),
                     b_spec=pl.BlockSpec((Lp, 256), lambda s: (0, s)),
                     o_spec=pl.BlockSpec((None, 512, 256), lambda s: (s, 0, 0)), o_shape=(NSH, 512, 256), o_dtype=F32, dims=NN,
                     acc_shape=None)
    (dqp, dkr, dvb), carried = _att_bwd(f"attb_{tag}", sv["qp"], sv["kr"], sv["vb"], C["bias"], sv["lse"], dbranch["wpb"],
                                        sv["b"], rider)
    dH, S["gq"][l] = _att_q_bwd(f"attq_{tag}", H, sv["gq"], C["cos"], C["sin"], C["bd"], dqp, dH)
    dH, S["gk"][l] = _att_kv_bwd(f"attk_{tag}", H, sv["gk"], C["cos"], C["sin"], C["bd"], dkr, dvb, dH)
    do, dH, S["gn"][l] = _gla_post_bwd(f"bglo_{tag}", sv["of"], sv["ob"], H, sv["gn"], dbranch["wpa"], dH)
    dqf, dkf, dgf, dvf = _gla_bwd(f"bglf_{tag}", sv["qh"], sv["kh"], sv["gf"], sv["vh"], sv["sf"], do, False, G_chunks)
    dqb, dkb, dgb, dvr = _gla_bwd(f"bglr_{tag}", sv["qh"], sv["kh"], sv["gb"], sv["vh"], sv["sb"], do, True, G_chunks)
    dH = _gla_qkv_bwd(f"bglq_{tag}", dqf, dqb, dkf, dkb, dvf, dvr, dH)
    dH, S["w2p"][l], S["b2p"][l] = _gla_gate_bwd(f"bglg_{tag}", H, sv["w2p"], sv["b2p"], dgf, dgb, dH)
    tmm = 256
    G["winp"] = _mm(f"bwi_{tag}", sv["zt"], dH, grid=(D // tmm, HP // 896), a_spec=pl.BlockSpec((tmm, Lp), lambda m, n: (m, 0)),
                    b_spec=pl.BlockSpec((Lp, 896), lambda m, n: (0, n)), o_spec=pl.BlockSpec((tmm, 896), lambda m, n: (m, n)),
                    o_shape=(D, HP), o_dtype=F32, dims=NN, acc_shape=None)
    dz = _mm(f"bdz_{tag}", dH, sv["winp"], grid=(D // 512, Lp // tm), a_spec=pl.BlockSpec((tm, HP), lambda n, i: (i, 0)),
             b_spec=pl.BlockSpec((512, HP), lambda n, i: (n, 0)), o_spec=pl.BlockSpec((tm, 512), lambda n, i: (i, n)),
             o_shape=(Lp, D), o_dtype=F32, dims=NT, acc_shape=None)
    dh_prev, dg = _rms_bwd(f"rmsb_{tag}", dz, sv["h"], sv["g"], dh)
    return dh_prev, dg, carried


def _winp_layer(win_g, l):
    return _win_to_padded(jnp.transpose(win_g[:, l], (1, 0, 2)).reshape(D, D_IN))


def _local_step(x2, tgt2, W, comm=None):
    n_tok = x2.shape[0]
    Lp = n_tok + LANE
    nc = Lp // CHUNK
    g_chunks = max(g for g in (1, 2, 3, 6) if nc % g == 0)
    cos, sin = _rope_tables(n_tok)
    bd = jnp.asarray(np.kron(np.eye(ATT_H, dtype=np.float32), np.full((HD, HD), 1.0 / HD, np.float32)))
    bias = jnp.where(jnp.arange(Lp) >= PAD, 0.0, -1e30).astype(F32).reshape(1, Lp)
    w2, b2 = W["w2"], W["b2"]
    w2p = jnp.zeros((DEPTH, LANE, 512), F32)
    w2p = w2p.at[:, 0:GLA_RANK, 0:256].set(w2[:, 0]).at[:, GLA_RANK:2 * GLA_RANK, 256:512].set(w2[:, 1]).astype(BF16)
    C = dict(cos=cos, sin=sin, bd=bd, bias=bias, w2p=w2p, b2p=b2.reshape(DEPTH, 1, 512),
             gq=jnp.tile(W["qn"], (1, ATT_H)).reshape(DEPTH, 1, 512), gk=jnp.tile(W["kn"], (1, ATT_KV)).reshape(DEPTH, 1, LANE))
    h = jnp.concatenate([jnp.zeros((PAD, D), F32), W["meta"], x2], axis=0)
    saved = []
    for l in range(DEPTH):
        h, s0 = _ffn_fwd(f"l{l}a", h, W, l, 0)
        rider = comm.fwd_rider(W, l) if comm else None
        h, sm, W = _mixer_fwd(f"l{l}m", h, W, _winp_layer(W["win"], l), C, l, g_chunks, rider)
        h, s1 = _ffn_fwd(f"l{l}b", h, W, l, 1)
        saved.append((s0, sm, s1))
    dh, dfin, loss = _loss_head("loss_head", h, W["fin"].reshape(1, D), tgt2)
    S = dict(bm=[None] * DEPTH, gq=[None] * DEPTH, gk=[None] * DEPTH, gn=[None] * DEPTH, w2p=[None] * DEPTH,
             b2p=[None] * DEPTH, ng=[[None] * 3 for _ in range(DEPTH)])
    Gs = [None] * DEPTH
    for l in reversed(range(DEPTH)):
        s0, sm, s1 = saved[l]
        G = {}
        rider = comm.ffn_rider(l) if comm else None
        dh, S["ng"][l][2], carried = _ffn_bwd(f"l{l}b", dh, s1, W, G, l, 1, rider)
        if rider is not None:
            comm.ffn_carried(l, carried)
        rider = comm.bwd_rider(l) if comm else None
        dh, S["ng"][l][1], carried = _mixer_bwd(f"l{l}m", dh, sm, W, C, G, S, l, g_chunks, rider)
        if rider is not None:
            comm.bwd_carried(l, carried)
        dh, S["ng"][l][0], _ = _ffn_bwd(f"l{l}a", dh, s0, W, G, l, 0)
        Gs[l] = G
        if comm:
            comm.layer_done(l, G)
    small = dict(
        meta=dh[PAD:LANE],
        norm_gains=jnp.stack([jnp.concatenate(S["ng"][l], axis=0) for l in range(DEPTH)]),
        w2=jnp.stack([jnp.stack([S["w2p"][l][0:GLA_RANK, 0:256], S["w2p"][l][GLA_RANK:2 * GLA_RANK, 256:512]]) for l in range(DEPTH)]),
        b2=jnp.stack([S["b2p"][l].reshape(2, 256) for l in range(DEPTH)]),
        gn=jnp.concatenate(S["gn"], axis=0),
        qn=jnp.stack([S["gq"][l].reshape(ATT_H, HD).sum(0) for l in range(DEPTH)]),
        kn=jnp.stack([S["gk"][l].reshape(ATT_KV, HD).sum(0) for l in range(DEPTH)]),
        bm=jnp.stack(S["bm"]),
        fin=dfin.reshape(D),
    )
    return loss[0, 0], dh, Gs, small


def _win_to_padded(w):
    pad = jnp.zeros(w.shape[:-1] + (HP - D_IN,), w.dtype)
    return jnp.concatenate([w[..., 2336:4384], w[..., 0:1536], w[..., 1568:2336], w[..., 1536:1568], pad], axis=-1)


def _win_from_padded(w):
    return jnp.concatenate([w[..., 2048:3584], w[..., 4352:4384], w[..., 3584:4352], w[..., 0:2048]], axis=-1)


def _assemble(g):
    W = dict(gate=g["gate"], up=g["up"], down=g["down"], win=g["win"], wpa=g["wpa"], wpb=g["wpb"], wout=g["wout"])
    sm = g["small"]
    parts = _unpack(sm, SHARDED_SMALL)
    W["meta"] = jnp.transpose(parts["meta"], (1, 0, 2)).reshape(N_META, D)
    W["norm_gains"] = jnp.transpose(parts["norm_gains"], (1, 2, 0, 3)).reshape(DEPTH, 3, D)
    W["w2"] = jnp.transpose(parts["w2"], (1, 2, 3, 0, 4)).reshape(DEPTH, 2, GLA_RANK, 256)
    W["b2"] = jnp.transpose(parts["b2"], (1, 2, 0, 3)).reshape(DEPTH, 2, 256)
    W["bm"] = jnp.transpose(parts["bm"], (1, 2, 0, 3)).reshape(DEPTH, 2, D)
    return W


SHARDED_SMALL = dict(meta=(N_META, 256), norm_gains=(DEPTH, 3, 256), w2=(DEPTH, 2, GLA_RANK, 64), b2=(DEPTH, 2, 64),
                     bm=(DEPTH, 2, 256))
FULL_SMALL = dict(meta=(N_META, D), norm_gains=(DEPTH, 3, D), w2=(DEPTH, 2, GLA_RANK, 256), b2=(DEPTH, 2, 256),
                  gn=(DEPTH, 512), qn=(DEPTH, HD), kn=(DEPTH, HD), bm=(DEPTH, 2, D), fin=(D,))


def _pack(parts, table, rows):
    flat = jnp.concatenate([parts[k].reshape(-1).astype(F32) for k in table])
    return jnp.pad(flat, (0, rows * LANE - flat.shape[0])).reshape(rows, LANE)


def _unpack(packed, table):
    lead = packed.shape[:-2]
    flat = packed.reshape(lead + (-1,))
    out, off = {}, 0
    for k, shp in table.items():
        n = int(np.prod(shp))
        out[k] = flat[..., off:off + n].reshape(lead + tuple(shp))
        off += n
    return out


def _rows_for(table, mult):
    n = sum(int(np.prod(s)) for s in table.values())
    return -(-n // (LANE * mult)) * mult


SMALL_ROWS = _rows_for(SHARDED_SMALL, 16)
GRAD_ROWS = _rows_for(dict(FULL_SMALL, loss=(1,)), 8)


def _place():
    x, y, c = lax.axis_index("x"), lax.axis_index("y"), lax.axis_index("c")
    return x, y, c


def _other_chips(x, y):
    return [(1 - x, y), (x, 1 - y), (1 - x, 1 - y)]


def _cast_place(name, w3, slot):
    A, R, Cc = w3.shape
    tr = _rt(R, 512, 16)

    def body(p_ref, w_ref, o_ref):
        o_ref[...] = w_ref[...].astype(BF16)

    return pl.pallas_call(
        body, name=name,
        grid_spec=pltpu.PrefetchScalarGridSpec(
            num_scalar_prefetch=1, grid=(A, R // tr),
            in_specs=[pl.BlockSpec((None, tr, Cc), lambda a, r, p_ref: (a, r, 0))],
            out_specs=pl.BlockSpec((None, None, tr, Cc), lambda a, r, p_ref: (p_ref[0], a, r, 0))),
        out_shape=SDS((NSH, A, R, Cc), BF16))(slot, w3)


class _GatherRider:
    def __init__(self, keys, bufs, layers):
        n = len(bufs)
        self.keys, self.ins, self.layers = keys, list(bufs), layers
        self.out_shape = [SDS(b.shape, b.dtype) for b in bufs]
        self.aliases = {a: a for a in range(n)}
        self.sems = [pltpu.SemaphoreType.DMA((n, 3)) for _ in range(4)]

    def updated(self, W, carried):
        return {**W, **dict(zip(self.keys, carried))}

    def hooks(self, ins, outs, sems):
        send, recv, fsend, frecv = sems
        n = len(outs)

        def rows(a, slot, core):
            ref, l = outs[a], self.layers[a]
            half = ref.shape[-2] // 2
            return ref.at[slot, pl.ds(core * half, half)] if l is None else ref.at[slot, l, pl.ds(core * half, half)]

        def ici(a, k, part, px, py, c):
            return pltpu.make_async_remote_copy(src_ref=part, dst_ref=part, send_sem=send.at[a, k], recv_sem=recv.at[a, k],
                                                device_id=(px, py, c), device_id_type=MESH)

        def d2d(a, k, part, x, y, c):
            return pltpu.make_async_remote_copy(src_ref=part, dst_ref=part, send_sem=fsend.at[a, k], recv_sem=frecv.at[a, k],
                                                device_id=(x, y, 1 - c), device_id_type=MESH)

        def start():
            x, y, c = _place()
            for a in range(n):
                for k, (px, py) in enumerate(_other_chips(x, y)):
                    ici(a, k, rows(a, 2 * x + y, c), px, py, c).start()

        def mid():
            x, y, c = _place()
            for a in range(n):
                for k, (px, py) in enumerate(_other_chips(x, y)):
                    landed = rows(a, 2 * px + py, c)
                    ici(a, k, landed, px, py, c).wait_recv()
                    d2d(a, k, landed, x, y, c).start()

        def end():
            x, y, c = _place()
            for a in range(n):
                for k, (px, py) in enumerate(_other_chips(x, y)):
                    d2d(a, k, rows(a, 2 * px + py, 1 - c), x, y, c).wait_recv()
                    ici(a, k, rows(a, 2 * x + y, c), px, py, c).wait_send()
                    d2d(a, k, rows(a, 2 * px + py, c), x, y, c).wait_send()

        return start, mid, end


class _ChipExchangeRider:
    def __init__(self, arrs):
        n = len(arrs)
        self.ins = list(arrs)
        self.out_shape = [SDS((3,) + a.shape[1:], a.dtype) for a in arrs]
        self.aliases = {}
        self.sems = [pltpu.SemaphoreType.DMA((n, 3)), pltpu.SemaphoreType.DMA((n, 3))]

    def hooks(self, ins, outs, sems):
        send, recv = sems

        def copy(a, k, px, py, c):
            return pltpu.make_async_remote_copy(src_ref=ins[a].at[2 * px + py], dst_ref=outs[a].at[k], send_sem=send.at[a, k],
                                                recv_sem=recv.at[a, k], device_id=(px, py, c), device_id_type=MESH)

        def start():
            x, y, c = _place()
            for a in range(len(ins)):
                for k, (px, py) in enumerate(_other_chips(x, y)):
                    copy(a, k, px, py, c).start()

        def mid():
            pass

        def end():
            x, y, c = _place()
            for a in range(len(ins)):
                for k, (px, py) in enumerate(_other_chips(x, y)):
                    copy(a, k, px, py, c).wait()

        return start, mid, end


def _run_rider(name, rider):
    r_in, r_out = len(rider.ins), len(rider.out_shape)

    def body(*refs):
        start, mid, end = rider.hooks(refs[:r_in], refs[r_in:r_in + r_out], refs[r_in + r_out:])
        start()
        mid()
        end()

    any_spec = pl.BlockSpec(memory_space=pl.ANY)
    return pl.pallas_call(
        body, name=name, in_specs=[any_spec] * r_in, out_specs=[any_spec] * r_out, out_shape=list(rider.out_shape),
        scratch_shapes=list(rider.sems), input_output_aliases=dict(rider.aliases))(*rider.ins)


class _PairExchangeRider:
    def __init__(self, arrs):
        n = len(arrs)
        self.ins = list(arrs)
        self.out_shape = [SDS((NSH, a.shape[1] // 2, a.shape[2]), a.dtype) for a in arrs]
        self.aliases = {}
        self.sems = [pltpu.SemaphoreType.DMA((n,)), pltpu.SemaphoreType.DMA((n,))]

    def hooks(self, ins, outs, sems):
        send, recv = sems

        def copy(a):
            x, y, c = _place()
            half = ins[a].shape[1] // 2
            return pltpu.make_async_remote_copy(
                src_ref=ins[a].at[:, pl.ds((1 - c) * half, half)], dst_ref=outs[a], send_sem=send.at[a], recv_sem=recv.at[a],
                device_id=(x, y, 1 - c), device_id_type=MESH)

        def start():
            for a in range(len(ins)):
                copy(a).start()

        def mid():
            pass

        def end():
            for a in range(len(ins)):
                copy(a).wait()

        return start, mid, end


def _pair_add(name, g, p, core):
    _, Rh, Cc = p.shape
    tr = _rt(Rh, 512, 16)
    nr = Rh // tr

    def body(c_ref, g_ref, p_ref, o_ref, ob_ref):
        v = g_ref[...] + p_ref[...]
        o_ref[...] = v
        ob_ref[...] = v.astype(BF16)

    blk = pl.BlockSpec((None, tr, Cc), lambda s, r, c_ref: (s, r, 0))
    return pl.pallas_call(
        body, name=name,
        grid_spec=pltpu.PrefetchScalarGridSpec(
            num_scalar_prefetch=1, grid=(NSH, nr),
            in_specs=[pl.BlockSpec((None, tr, Cc), lambda s, r, c_ref: (s, c_ref[0] * nr + r, 0)), blk],
            out_specs=[blk, blk]),
        out_shape=[SDS((NSH, Rh, Cc), F32), SDS((NSH, Rh, Cc), BF16)])(core, g, p)


def _chip_add(name, hsum, q, chip, core, l, into):
    _, Rh, Cc = hsum.shape
    tr = _rt(Rh, 512, 8)
    nr = Rh // tr

    def body(*refs):
        h_ref, q_ref, o_ref = refs[2], refs[3], refs[-1]
        o_ref[...] = ((h_ref[...] + q_ref[0].astype(F32)) + q_ref[1].astype(F32)) + q_ref[2].astype(F32)

    in_specs = [pl.BlockSpec((None, tr, Cc), lambda r, p_ref, c_ref: (p_ref[0], r, 0)),
                pl.BlockSpec((3, tr, Cc), lambda r, p_ref, c_ref: (0, r, 0))]
    args = [chip, core, hsum, q]
    aliases = {}
    if into is not None:
        in_specs.append(pl.BlockSpec(memory_space=pl.ANY))
        args.append(into)
        aliases = {4: 0}
    return pl.pallas_call(
        body, name=name,
        grid_spec=pltpu.PrefetchScalarGridSpec(
            num_scalar_prefetch=2, grid=(nr,), in_specs=in_specs,
            out_specs=pl.BlockSpec((None, tr, Cc), lambda r, p_ref, c_ref: (l, c_ref[0] * nr + r, 0))),
        out_shape=SDS((DEPTH, 2 * Rh, Cc), F32), input_output_aliases=aliases)(*args)


def _pair_share(arrs):
    n = len(arrs)

    def body(*refs):
        outs = refs[n:2 * n]
        send, recv = refs[2 * n:]
        x, y, c = _place()
        cps = []
        for a in range(n):
            half = outs[a].shape[1] // 2
            mine = outs[a].at[:, pl.ds(c * half, half)]
            cp = pltpu.make_async_remote_copy(
                src_ref=mine, dst_ref=mine, send_sem=send.at[a], recv_sem=recv.at[a],
                device_id=(x, y, 1 - c), device_id_type=MESH)
            cp.start()
            cps.append(cp)
        for a, cp in enumerate(cps):
            cp.wait_send()
            half = outs[a].shape[1] // 2
            theirs = outs[a].at[:, pl.ds((1 - c) * half, half)]
            pltpu.make_async_remote_copy(
                src_ref=theirs, dst_ref=theirs, send_sem=send.at[a], recv_sem=recv.at[a],
                device_id=(x, y, 1 - c), device_id_type=MESH).wait_recv()

    any_spec = pl.BlockSpec(memory_space=pl.ANY)
    return pl.pallas_call(
        body, name="pair_share", in_specs=[any_spec] * n, out_specs=[any_spec] * n,
        out_shape=[SDS(a.shape, a.dtype) for a in arrs], input_output_aliases={a: a for a in range(n)},
        scratch_shapes=[pltpu.SemaphoreType.DMA((n,)), pltpu.SemaphoreType.DMA((n,))])(*arrs)


def _allreduce_small(v):
    rows = v.shape[0]

    def body(v_ref, o_ref, buf, send, recv):
        x, y, c = _place()
        me = 4 * x + 2 * y + c
        buf[me] = v_ref[...]
        cps = []
        k = 0
        for dx in range(2):
            for dy in range(2):
                for dc in range(2):
                    if dx + dy + dc == 0:
                        continue
                    cp = pltpu.make_async_remote_copy(
                        src_ref=v_ref, dst_ref=buf.at[me], send_sem=send.at[k], recv_sem=recv.at[k],
                        device_id=(jnp.bitwise_xor(x, dx), jnp.bitwise_xor(y, dy), jnp.bitwise_xor(c, dc)), device_id_type=MESH)
                    cp.start()
                    cps.append((cp, dx, dy, dc))
                    k += 1
        for k, (cp, dx, dy, dc) in enumerate(cps):
            cp.wait_send()
            src = 4 * jnp.bitwise_xor(x, dx) + 2 * jnp.bitwise_xor(y, dy) + jnp.bitwise_xor(c, dc)
            pltpu.make_async_remote_copy(
                src_ref=v_ref, dst_ref=buf.at[src], send_sem=send.at[k], recv_sem=recv.at[k],
                device_id=(x, y, c), device_id_type=MESH).wait_recv()
        acc = buf[0]
        for d in range(1, 8):
            acc = acc + buf[d]
        o_ref[...] = acc

    vm = pl.BlockSpec(memory_space=pltpu.VMEM)
    return pl.pallas_call(
        body, name="allreduce_small", in_specs=[vm], out_specs=vm, out_shape=SDS((rows, LANE), F32),
        scratch_shapes=[pltpu.VMEM((8, rows, LANE), F32), pltpu.SemaphoreType.DMA((7,)), pltpu.SemaphoreType.DMA((7,))])(v)


def _adamw(name, w, g, m, v):
    A, R, Cc = w.shape
    tr = _rt(R, 512, 8)

    def body(w_ref, g_ref, m_ref, v_ref, d_ref, mo_ref, vo_ref):
        gv = g_ref[...]
        mn = ADAM_B1 * m_ref[...] + (1.0 - ADAM_B1) * gv
        vn = ADAM_B2 * v_ref[...] + (1.0 - ADAM_B2) * (gv * gv)
        m_hat = mn / (1.0 - ADAM_B1 ** ADAM_STEP)
        v_hat = vn / (1.0 - ADAM_B2 ** ADAM_STEP)
        d_ref[...] = -ADAM_LR * (m_hat / (jnp.sqrt(v_hat) + ADAM_EPS) + ADAM_WD * w_ref[...])
        mo_ref[...] = mn
        vo_ref[...] = vn

    blk = pl.BlockSpec((None, tr, Cc), lambda a, r: (a, r, 0))
    return pl.pallas_call(
        body, name=name, grid=(A, R // tr), in_specs=[blk] * 4, out_specs=[blk] * 3,
        out_shape=[SDS(w.shape, F32)] * 3)(w, g, m, v)


BIG = ("gate", "up", "down", "win", "wpa", "wpb", "wout")
SMALL = ("meta", "norm_gains", "w2", "b2", "gn", "qn", "kn", "bm", "fin")


def _view3(a):
    return a.reshape(a.shape[0], -1, a.shape[-1])


class _StepComm:
    def __init__(self, chip, core):
        self.pvec, self.cvec = chip.reshape(1), core.reshape(1)
        self.loc, self.sums = {}, {}
        self.red = {k: None for k in BIG}

    def fwd_rider(self, W, l):
        if l + 1 >= DEPTH:
            return None
        return _GatherRider(BIG, [W[k] for k in BIG], [l + 1] * len(BIG))

    def layer_done(self, l, G):
        dwin = jnp.transpose(_win_from_padded(G["winp"]).reshape(D, NSH, D_IN // NSH), (1, 0, 2))
        self.loc[l] = [dwin if k == "win" else _view3(G[k]) for k in BIG]

    def ffn_rider(self, l):
        return _PairExchangeRider(self.loc[l + 1]) if l + 1 < DEPTH else None

    def ffn_carried(self, l, got):
        self._pair_add(l + 1, got)

    def _pair_add(self, l, got):
        self.sums[l] = [_pair_add(f"pair_add_{k}_l{l}", a, p, self.cvec) for k, a, p in zip(BIG, self.loc.pop(l), got)]

    def bwd_rider(self, l):
        if l + 1 >= DEPTH:
            return None
        return _ChipExchangeRider([s[1] for s in self.sums[l + 1]])

    def bwd_carried(self, l, arrived):
        self._chip_add(l + 1, arrived)

    def _chip_add(self, l, arrived):
        for k, s, q in zip(BIG, self.sums.pop(l), arrived):
            self.red[k] = _chip_add(f"chip_add_{k}_l{l}", s[0], q, self.pvec, self.cvec, l, self.red[k])

    def finish(self):
        self._pair_add(0, _run_rider("pair_exchange_l0", _PairExchangeRider(self.loc[0])))
        self._chip_add(0, _run_rider("chip_exchange_l0", _ChipExchangeRider([s[1] for s in self.sums[0]])))
        return dict(zip(BIG, _pair_share([self.red[k] for k in BIG])))


def kernel(x, meta_tokens, norm_gains, ffn_w_gate, ffn_w_up, ffn_w_down, w_in, gla_w2, gla_b2, gla_gn, q_norm, k_norm, w_pa, w_pb, b_merge, w_out, final_norm, loss_target, m_meta_tokens, m_norm_gains, m_ffn_w_gate, m_ffn_w_up, m_ffn_w_down, m_w_in, m_gla_w2, m_gla_b2, m_gla_gn, m_q_norm, m_k_norm, m_w_pa, m_w_pb, m_b_merge, m_w_out, m_final_norm, v_meta_tokens, v_norm_gains, v_ffn_w_gate, v_ffn_w_up, v_ffn_w_down, v_w_in, v_gla_w2, v_gla_b2, v_gla_gn, v_q_norm, v_k_norm, v_w_pa, v_w_pb, v_b_merge, v_w_out, v_final_norm):
    big_w = dict(gate=ffn_w_gate, up=ffn_w_up, down=ffn_w_down, win=w_in, wpa=w_pa, wpb=w_pb, wout=w_out)
    big_m = dict(gate=m_ffn_w_gate, up=m_ffn_w_up, down=m_ffn_w_down, win=m_w_in, wpa=m_w_pa, wpb=m_w_pb, wout=m_w_out)
    big_v = dict(gate=v_ffn_w_gate, up=v_ffn_w_up, down=v_ffn_w_down, win=v_w_in, wpa=v_w_pa, wpb=v_w_pb, wout=v_w_out)
    small_w = dict(meta=meta_tokens, norm_gains=norm_gains, w2=gla_w2, b2=gla_b2, gn=gla_gn, qn=q_norm, kn=k_norm,
                   bm=b_merge, fin=final_norm)
    small_m = dict(meta=m_meta_tokens, norm_gains=m_norm_gains, w2=m_gla_w2, b2=m_gla_b2, gn=m_gla_gn, qn=m_q_norm,
                   kn=m_k_norm, bm=m_b_merge, fin=m_final_norm)
    small_v = dict(meta=v_meta_tokens, norm_gains=v_norm_gains, w2=v_gla_w2, b2=v_gla_b2, gn=v_gla_gn, qn=v_q_norm,
                   kn=v_k_norm, bm=v_b_merge, fin=v_final_norm)
    xi, yi, ci = _place()
    chip = (2 * xi + yi).astype(jnp.int32)

    comm = _StepComm(chip, ci.astype(jnp.int32))
    shard_pack = _pack({k: small_w[k] for k in SHARDED_SMALL}, SHARDED_SMALL, SMALL_ROWS)
    placed = [_cast_place(f"cast_{k}", _view3(big_w[k]), comm.pvec) for k in BIG]
    placed.append(lax.dynamic_update_slice(jnp.zeros((NSH, SMALL_ROWS, LANE), F32), shard_pack[None], (chip, 0, 0)))
    gathered = _run_rider("gather_l0", _GatherRider(BIG + ("small",), placed, [0] * len(BIG) + [None]))
    W = _assemble(dict(zip(BIG + ("small",), gathered)))
    W.update(gn=gla_gn, qn=q_norm, kn=k_norm, fin=final_norm)

    loss, dh0, _, gs = _local_step(x[0], loss_target[0], W, comm)
    grad_x = dh0[LANE:][None]
    shared = comm.finish()
    grads, deltas, new_m, new_v = {}, {}, {}, {}
    for k in BIG:
        r = shared[k]
        shp = big_w[k].shape
        grads[k] = r.reshape(shp)
        d, mn, vn = _adamw(f"adamw_{k}", _view3(big_w[k]), r, _view3(big_m[k]), _view3(big_v[k]))
        deltas[k], new_m[k], new_v[k] = d.reshape(shp), mn.reshape(shp), vn.reshape(shp)

    gs["loss"] = loss.reshape(1)
    table = dict(FULL_SMALL, loss=(1,))
    tot = _unpack(_allreduce_small(_pack(gs, table, GRAD_ROWS)), table)
    loss_out = tot["loss"][0]
    sl = dict(meta=(1, 256), norm_gains=(2, 256), w2=(3, 64), b2=(2, 64), bm=(2, 256))
    for k in SMALL:
        gk = tot[k]
        if k in sl:
            ax, width = sl[k]
            gk = lax.dynamic_slice_in_dim(gk, chip * width, width, axis=ax)
        grads[k] = gk
    tbl = {k: small_w[k].shape for k in SMALL}
    rows = _rows_for(tbl, 8)
    packs = [_pack(src, tbl, rows)[None] for src in (small_w, grads, small_m, small_v)]
    d, mn, vn = _adamw("adamw_small", *packs)
    for dst, packed in ((deltas, d), (new_m, mn), (new_v, vn)):
        dst.update(_unpack(packed[0], tbl))

    order = ("meta", "norm_gains", "gate", "up", "down", "win", "w2", "b2", "gn", "qn", "kn", "wpa", "wpb", "bm", "wout", "fin")
    return (loss_out, grad_x, *[grads[k] for k in order], *[deltas[k] for k in order], *[new_m[k] for k in order],
            *[new_v[k] for k in order])
```

```python
import functools

import numpy as np
import jax
import jax.numpy as jnp
from jax import lax
from jax.experimental import pallas as pl
from jax.experimental.pallas import tpu as pltpu

F32, BF16 = jnp.float32, jnp.bfloat16
SDS = jax.ShapeDtypeStruct
HIGHEST = lax.Precision.HIGHEST
MESH = pl.DeviceIdType.MESH

D = 1024
DEPTH = 4
N_META = 16
GRID_W = 64
GLA_H, GLA_DK, GLA_DV, GLA_RANK, GLA_TAU, CHUNK = 4, 64, 128, 16, 16.0, 64
ATT_H, ATT_KV, HD = 8, 2, 64
D_FF = 2816
EPS = 1e-6
ROPE_THETA = 10000.0
ADAM_LR, ADAM_B1, ADAM_B2, ADAM_EPS, ADAM_WD, ADAM_STEP = 0.001, 0.9, 0.999, 1e-08, 0.01, 10

NSH = 4
FF_S = D_FF // NSH
FF_P = 768
LANE = 128
PAD = LANE - N_META
D_IN = 4384
C_GA, C_GB, C_QA, C_KA, C_VA, C_RA, C_QB, C_KB, C_VB, C_LR, HP = 0, 1024, 2048, 2304, 2560, 3072, 3584, 4096, 4224, 4352, 4480
VMEM_BIG = 56 * 2 ** 20
ATT_SUB_FWD, ATT_SUB_BWD = 128, 192
RIDER_MID_BACK = 5


def _rt(n, cap, mult=LANE):
    best = None
    t = mult
    while t <= min(n, cap):
        if n % t == 0:
            best = t
        t += mult
    assert best is not None, (n, cap, mult)
    return best


def _cp(big=False):
    return pltpu.CompilerParams(vmem_limit_bytes=VMEM_BIG) if big else None


def _row_ids(i, tm):
    return i * tm + lax.broadcasted_iota(jnp.int32, (tm, 1), 0)


def _mm(name, a, b, *, grid, a_spec, b_spec, o_spec, o_shape, o_dtype, dims, acc_shape, nk=1, scale=None,
        res=None, res_spec=None, a_sl=None, b_sl=None, pad_w=None, into=None, shards=None):
    has_res, has_into = res is not None, into is not None

    def body(*refs):
        a_ref, b_ref = refs[0], refs[1]
        p = 2
        res_ref = None
        if has_res:
            res_ref = refs[p]
            p += 1
        if has_into:
            p += 1
        o_ref = refs[p]
        acc_ref = refs[p + 1] if nk > 1 else None
        av = (a_ref[a_sl] if a_sl is not None else a_ref[...]).astype(BF16)
        bv = (b_ref[b_sl] if b_sl is not None else b_ref[...]).astype(BF16)
        if shards == "rows":
            bv = bv.reshape(bv.shape[0] * bv.shape[1], bv.shape[2])
        if shards == "cols":
            w = bv.shape[2]
            if dims == NN:
                prod = jnp.concatenate([lax.dot_general(av, bv[s], NN, preferred_element_type=F32) for s in range(NSH)], axis=1)
            else:
                prod = sum(lax.dot_general(av[:, s * w:(s + 1) * w], bv[s], NT, preferred_element_type=F32) for s in range(NSH))
        else:
            prod = lax.dot_general(av, bv, dims, preferred_element_type=F32)

        def finish(v):
            if scale is not None:
                v = v * scale
            if has_res:
                v = v + res_ref[...]
            v = v.astype(o_dtype)
            if pad_w is None:
                o_ref[...] = v
            else:
                w = v.shape[-1]
                o_ref[:, :w] = v
                o_ref[:, w:] = jnp.zeros((v.shape[0], pad_w - w), o_dtype)

        if nk == 1:
            finish(prod)
        else:
            k = pl.program_id(len(grid) - 1)

            @pl.when(k == 0)
            def _():
                acc_ref[...] = prod

            @pl.when(k > 0)
            def _():
                acc_ref[...] += prod

            @pl.when(k == nk - 1)
            def _():
                finish(acc_ref[...])

    in_specs = [a_spec, b_spec]
    args = [a, b]
    if has_res:
        in_specs.append(res_spec)
        args.append(res)
    aliases = {}
    if has_into:
        aliases = {len(args): 0}
        in_specs.append(pl.BlockSpec(memory_space=pl.ANY))
        args.append(into)
        o_shape = into.shape
    return pl.pallas_call(
        body, name=name, grid=grid, in_specs=in_specs, out_specs=o_spec, out_shape=SDS(tuple(o_shape), o_dtype),
        scratch_shapes=[pltpu.VMEM(acc_shape, F32)] if nk > 1 else [], input_output_aliases=aliases,
        compiler_params=_cp(True))(*args)


NN = (((1,), (0,)), ((), ()))
NT = (((1,), (1,)), ((), ()))
TN = (((0,), (0,)), ((), ()))


W_CHUNK = 896


def _norm_project(name, h, g, winp):
    Lp = h.shape[0]
    tm = _rt(Lp, 384)

    def body(h_ref, g_ref, w_ref, o_ref, zt_ref):
        x = h_ref[...]
        y = x * lax.rsqrt(jnp.mean(x * x, axis=-1, keepdims=True) + EPS) * g_ref[...]
        zt_ref[...] = y.T.astype(BF16)
        z = y.astype(BF16)
        for n in range(HP // W_CHUNK):
            cs = slice(n * W_CHUNK, (n + 1) * W_CHUNK)
            o_ref[:, cs] = jnp.dot(z, w_ref[:, cs], preferred_element_type=F32)

    return pl.pallas_call(
        body, name=name, grid=(Lp // tm,),
        in_specs=[pl.BlockSpec((tm, D), lambda i: (i, 0)), pl.BlockSpec((1, D), lambda i: (0, 0)),
                  pl.BlockSpec((D, HP), lambda i: (0, 0), pipeline_mode=pl.Buffered(1))],
        out_specs=[pl.BlockSpec((tm, HP), lambda i: (i, 0)), pl.BlockSpec((D, tm), lambda i: (0, i))],
        out_shape=[SDS((Lp, HP), F32), SDS((D, Lp), BF16)], compiler_params=_cp(True))(h, g, winp)


def _norm_project_bwd(name, dH, winp, h, g, dh):
    Lp = h.shape[0]
    tm = _rt(Lp, 384)

    def body(dhh_ref, w_ref, h_ref, g_ref, dh_ref, o_ref, dg_ref):
        dn = lax.dot_general(dhh_ref[...], w_ref[...], NT, preferred_element_type=F32)
        x = h_ref[...]
        r = lax.rsqrt(jnp.mean(x * x, axis=-1, keepdims=True) + EPS)
        xh = x * r
        u = dn * g_ref[...]
        o_ref[...] = dh_ref[...] + r * (u - xh * jnp.mean(u * xh, axis=-1, keepdims=True))

        @pl.when(pl.program_id(0) == 0)
        def _():
            dg_ref[...] = jnp.zeros_like(dg_ref)

        dg_ref[...] += jnp.sum(dn * xh, axis=0, keepdims=True)

    row = pl.BlockSpec((tm, D), lambda i: (i, 0))
    vec = pl.BlockSpec((1, D), lambda i: (0, 0))
    return pl.pallas_call(
        body, name=name, grid=(Lp // tm,),
        in_specs=[pl.BlockSpec((tm, HP), lambda i: (i, 0)), pl.BlockSpec((D, HP), lambda i: (0, 0), pipeline_mode=pl.Buffered(1)),
                  row, vec, row],
        out_specs=[row, vec], out_shape=[SDS((Lp, D), F32), SDS((1, D), F32)], compiler_params=_cp(True))(dH, winp, h, g, dh)


def _loss_head(name, h, g, tgt):
    Lp = h.shape[0]
    tm = LANE

    def body(h_ref, g_ref, t_ref, dh_ref, dg_ref, loss_ref):
        i = pl.program_id(0)
        x = h_ref[...]
        r = lax.rsqrt(jnp.mean(x * x, axis=-1, keepdims=True) + EPS)
        xh = x * r
        gg = g_ref[...]
        err = jnp.where(i >= 1, xh * gg - t_ref[...], 0.0)
        dy = err * (1.0 / D)
        u = dy * gg
        dh_ref[...] = r * (u - xh * jnp.mean(u * xh, axis=-1, keepdims=True))

        @pl.when(i == 0)
        def _():
            dg_ref[...] = jnp.zeros_like(dg_ref)
            loss_ref[...] = jnp.zeros_like(loss_ref)

        dg_ref[...] += jnp.sum(dy * xh, axis=0, keepdims=True)
        loss_ref[...] += (0.5 / D) * jnp.sum(err * err)

    row = pl.BlockSpec((tm, D), lambda i: (i, 0))
    vec = pl.BlockSpec((1, D), lambda i: (0, 0))
    return pl.pallas_call(
        body, name=name, grid=(Lp // tm,),
        in_specs=[row, vec, pl.BlockSpec((tm, D), lambda i: (jnp.maximum(i - 1, 0), 0))],
        out_specs=[row, vec, pl.BlockSpec((8, LANE), lambda i: (0, 0))],
        out_shape=[SDS((Lp, D), F32), SDS((1, D), F32), SDS((8, LANE), F32)])(h, g, tgt)


def _merge_fwd(name, H, pa, pb, bm):
    Lp = H.shape[0]
    tm = _rt(Lp, 384)

    def body(g_ref, pa_ref, pb_ref, bm_ref, y_ref, yt_ref):
        gv = g_ref[...]
        y = (jax.nn.sigmoid(gv[:, :D] + bm_ref[0:1, :]) * pa_ref[...]
             + jax.nn.sigmoid(gv[:, D:] + bm_ref[1:2, :]) * pb_ref[...])
        y_ref[...] = y.astype(BF16)
        yt_ref[...] = y.T.astype(BF16)

    row = pl.BlockSpec((tm, D), lambda i: (i, 0))
    return pl.pallas_call(
        body, name=name, grid=(Lp // tm,),
        in_specs=[pl.BlockSpec((tm, 2 * D), lambda i: (i, 0)), row, row, pl.BlockSpec((2, D), lambda i: (0, 0))],
        out_specs=[row, pl.BlockSpec((D, tm), lambda i: (0, i))],
        out_shape=[SDS((Lp, D), BF16), SDS((D, Lp), BF16)])(H, pa, pb, bm)


def _merge_bwd(name, H, pa, pb, bm, dy):
    Lp = H.shape[0]
    tm = _rt(Lp, 384)

    def body(g_ref, pa_ref, pb_ref, bm_ref, dy_ref, dpa_ref, dpb_ref, dh_ref, dbm_ref):
        gv = g_ref[...]
        dyv = dy_ref[...]
        sa = jax.nn.sigmoid(gv[:, :D] + bm_ref[0:1, :])
        sb = jax.nn.sigmoid(gv[:, D:] + bm_ref[1:2, :])
        dpa_ref[...] = (dyv * sa).astype(BF16)
        dpb_ref[...] = (dyv * sb).astype(BF16)
        dga = dyv * pa_ref[...] * (sa * (1.0 - sa))
        dgb = dyv * pb_ref[...] * (sb * (1.0 - sb))
        dh_ref[:, :D] = dga.astype(BF16)
        dh_ref[:, D:] = dgb.astype(BF16)

        @pl.when(pl.program_id(0) == 0)
        def _():
            dbm_ref[...] = jnp.zeros_like(dbm_ref)

        dbm_ref[0:1, :] += jnp.sum(dga, axis=0, keepdims=True)
        dbm_ref[1:2, :] += jnp.sum(dgb, axis=0, keepdims=True)

    row = pl.BlockSpec((tm, D), lambda i: (i, 0))
    two = pl.BlockSpec((2, D), lambda i: (0, 0))
    gate = pl.BlockSpec((tm, 2 * D), lambda i: (i, 0))
    return pl.pallas_call(
        body, name=name, grid=(Lp // tm,), in_specs=[gate, row, row, two, row], out_specs=[row, row, gate, two],
        out_shape=[SDS((Lp, D), BF16), SDS((Lp, D), BF16), SDS((Lp, HP), BF16), SDS((2, D), F32)])(H, pa, pb, bm, dy)


def _gla_prep(name, H, w2p, b2p):
    Lp = H.shape[0]
    tm = _rt(Lp, 384)

    def body(qk_ref, v_ref, lr_ref, w_ref, b_ref, q_o, k_o, v_o, gf_o, gb_o):
        valid = _row_ids(pl.program_id(0), tm) >= PAD
        qk = qk_ref[...]
        vv = v_ref[...]
        pre = jnp.dot(lr_ref[...].astype(BF16), w_ref[...], preferred_element_type=F32) + b_ref[...]
        g = jnp.where(valid, jax.nn.log_sigmoid(pre) * (1.0 / GLA_TAU), 0.0)
        for hh in range(GLA_H):
            q_o[hh] = qk[:, 64 * hh:64 * hh + 64] * (GLA_DK ** -0.5)
            k_o[hh] = qk[:, 256 + 64 * hh:256 + 64 * hh + 64]
            v_o[hh] = vv[:, 128 * hh:128 * hh + 128].astype(BF16)
            gf_o[hh] = g[:, 64 * hh:64 * hh + 64]
            gb_o[hh] = g[:, 256 + 64 * hh:256 + 64 * hh + 64]

    h64 = pl.BlockSpec((GLA_H, tm, 64), lambda i: (0, i, 0))
    h128 = pl.BlockSpec((GLA_H, tm, 128), lambda i: (0, i, 0))
    return pl.pallas_call(
        body, name=name, grid=(Lp // tm,),
        in_specs=[pl.BlockSpec((tm, 512), lambda i: (i, C_QA // 512)), pl.BlockSpec((tm, 512), lambda i: (i, C_VA // 512)),
                  pl.BlockSpec((tm, LANE), lambda i: (i, C_LR // LANE)), pl.BlockSpec((LANE, 512), lambda i: (0, 0)),
                  pl.BlockSpec((1, 512), lambda i: (0, 0))],
        out_specs=[h64, h64, h128, h64, h64],
        out_shape=[SDS((GLA_H, Lp, 64), F32), SDS((GLA_H, Lp, 64), F32), SDS((GLA_H, Lp, 128), BF16),
                   SDS((GLA_H, Lp, 64), F32), SDS((GLA_H, Lp, 64), F32)])(H, H, H, w2p, b2p)


def _bdot(a, b, ca, cb, precision=None):
    return lax.dot_general(a, b, ((ca, cb), ((0,), (0,))), precision=precision, preferred_element_type=F32)


def _gla_chunk_terms(q_ref, k_ref, g_ref, v_ref, G, rev):
    B = GLA_H * G
    qv = q_ref[...].reshape(B, CHUNK, GLA_DK)
    kv = k_ref[...].reshape(B, CHUNK, GLA_DK)
    gv = g_ref[...].reshape(B, CHUNK, GLA_DK)
    vv = v_ref[...].reshape(B, CHUNK, GLA_DV)
    ii = lax.broadcasted_iota(jnp.int32, (CHUNK, CHUNK), 0)
    jj = lax.broadcasted_iota(jnp.int32, (CHUNK, CHUNK), 1)
    tri = (jj >= ii) if rev else (jj <= ii)
    tb = jnp.broadcast_to(tri.astype(F32)[None], (B, CHUNK, CHUNK))
    bc = _bdot(tb, gv, (2,), (1,), HIGHEST)
    bt = bc[:, 0:1, :] if rev else bc[:, CHUNK - 1:CHUNK, :]
    eq, eki, eke = jnp.exp(bc), jnp.exp(-bc), jnp.exp(bt - bc)
    qd, ki, ke = qv * eq, kv * eki, kv * eke
    att = jnp.where(tri[None], _bdot(qd.astype(BF16), ki.astype(BF16), (2,), (2,)), 0.0)
    dm = jnp.exp(_bdot(gv, jnp.ones((B, CHUNK, GLA_DV), F32), (1,), (1,), HIGHEST))
    return dict(B=B, vv=vv, tri=tri, tb=tb, bt=bt, eq=eq, eki=eki, eke=eke, qd=qd, ki=ki, ke=ke, att=att, dm=dm)


def _gla_fwd(name, q, k, g, v, rev, G):
    Lp = q.shape[1]
    tg = G * CHUNK
    ng = Lp // tg

    def body(q_ref, k_ref, g_ref, v_ref, o_ref, ss_ref, s_scr):
        @pl.when(pl.program_id(0) == 0)
        def _():
            s_scr[...] = jnp.zeros_like(s_scr)

        t = _gla_chunk_terms(q_ref, k_ref, g_ref, v_ref, G, rev)
        B, vv = t["B"], t["vv"]
        qd = t["qd"].astype(BF16)
        oi = _bdot(t["att"].astype(BF16), vv, (2,), (1,))
        kvc = _bdot(t["ke"].astype(BF16), vv, (1,), (1,)).reshape(GLA_H, G, GLA_DK, GLA_DV)
        dm = t["dm"].reshape(GLA_H, G, GLA_DK, GLA_DV)
        s = s_scr[...]
        sp = [None] * G
        for c in (range(G - 1, -1, -1) if rev else range(G)):
            sp[c] = s
            ss_ref[c] = s
            s = dm[:, c] * s + kvc[:, c]
        s_scr[...] = s
        spb = jnp.stack(sp, axis=1).reshape(B, GLA_DK, GLA_DV).astype(BF16)
        o_ref[...] = (oi + _bdot(qd, spb, (2,), (1,))).reshape(GLA_H, tg, GLA_DV)

    blk = (lambda i: (0, ng - 1 - i, 0)) if rev else (lambda i: (0, i, 0))
    sblk = (lambda i: (ng - 1 - i, 0, 0, 0)) if rev else (lambda i: (i, 0, 0, 0))
    h64 = pl.BlockSpec((GLA_H, tg, 64), blk)
    h128 = pl.BlockSpec((GLA_H, tg, 128), blk)
    return pl.pallas_call(
        body, name=name, grid=(ng,), in_specs=[h64, h64, h64, h128],
        out_specs=[h128, pl.BlockSpec((G, GLA_H, GLA_DK, GLA_DV), sblk)],
        out_shape=[SDS((GLA_H, Lp, GLA_DV), F32), SDS((Lp // CHUNK, GLA_H, GLA_DK, GLA_DV), F32)],
        scratch_shapes=[pltpu.VMEM((GLA_H, GLA_DK, GLA_DV), F32)], compiler_params=_cp(True))(q, k, g, v)


def _gla_bwd(name, q, k, g, v, ss, do, rev, G):
    Lp = q.shape[1]
    tg = G * CHUNK
    ng = Lp // tg

    def body(q_ref, k_ref, g_ref, v_ref, ss_ref, do_ref, dq_ref, dk_ref, dg_ref, dv_ref, ds_scr):
        @pl.when(pl.program_id(0) == 0)
        def _():
            ds_scr[...] = jnp.zeros_like(ds_scr)

        t = _gla_chunk_terms(q_ref, k_ref, g_ref, v_ref, G, rev)
        B, vv, tri = t["B"], t["vv"], t["tri"]
        qd, ki, ke = t["qd"], t["ki"], t["ke"]
        qdb, kib, keb = qd.astype(BF16), ki.astype(BF16), ke.astype(BF16)
        sp = jnp.stack([ss_ref[c] for c in range(G)], axis=1).reshape(B, GLA_DK, GLA_DV)
        dob = do_ref[...].reshape(B, CHUNK, GLA_DV).astype(BF16)
        da = jnp.where(tri[None], _bdot(dob, vv, (2,), (2,)), 0.0).astype(BF16)
        dqd = _bdot(da, kib, (2,), (1,)) + _bdot(dob, sp.astype(BF16), (2,), (2,))
        dki = _bdot(da, qdb, (1,), (1,))
        dv = _bdot(t["att"].astype(BF16), dob, (1,), (1,))
        cc = _bdot(qdb, dob, (1,), (1,)).reshape(GLA_H, G, GLA_DK, GLA_DV)
        dm = t["dm"].reshape(GLA_H, G, GLA_DK, GLA_DV)
        dsc = ds_scr[...]
        dsn = [None] * G
        for c in (range(G) if rev else range(G - 1, -1, -1)):
            dsn[c] = dsc
            dsc = dm[:, c] * dsc + cc[:, c]
        ds_scr[...] = dsc
        dsn = jnp.stack(dsn, axis=1).reshape(B, GLA_DK, GLA_DV)
        dsnb = dsn.astype(BF16)
        dv = dv + _bdot(keb, dsnb, (2,), (1,))
        dke = _bdot(vv, dsnb, (2,), (2,))
        ddrow = _bdot(jnp.ones((B, CHUNK, GLA_DV), F32), dsn * sp, (2,), (2,), HIGHEST)
        dbt = ddrow * jnp.exp(t["bt"]) + jnp.sum(dke * ke, axis=1, keepdims=True)
        db = dqd * qd - dki * ki - dke * ke
        dq_ref[...] = (dqd * t["eq"]).reshape(GLA_H, tg, GLA_DK)
        dk_ref[...] = (dki * t["eki"] + dke * t["eke"]).reshape(GLA_H, tg, GLA_DK)
        dg_ref[...] = (_bdot(t["tb"], db, (1,), (1,), HIGHEST) + dbt).reshape(GLA_H, tg, GLA_DK)
        dv_ref[...] = dv.reshape(GLA_H, tg, GLA_DV)

    blk = (lambda i: (0, i, 0)) if rev else (lambda i: (0, ng - 1 - i, 0))
    sblk = (lambda i: (i, 0, 0, 0)) if rev else (lambda i: (ng - 1 - i, 0, 0, 0))
    h64 = pl.BlockSpec((GLA_H, tg, 64), blk)
    h128 = pl.BlockSpec((GLA_H, tg, 128), blk)
    return pl.pallas_call(
        body, name=name, grid=(ng,),
        in_specs=[h64, h64, h64, h128, pl.BlockSpec((G, GLA_H, GLA_DK, GLA_DV), sblk), h128],
        out_specs=[h64, h64, h64, h128],
        out_shape=[SDS((GLA_H, Lp, 64), F32), SDS((GLA_H, Lp, 64), F32), SDS((GLA_H, Lp, 64), F32),
                   SDS((GLA_H, Lp, GLA_DV), F32)],
        scratch_shapes=[pltpu.VMEM((GLA_H, GLA_DK, GLA_DV), F32)], compiler_params=_cp(True))(q, k, g, v, ss, do)


def _gla_post(name, of, ob, H, gn):
    Lp = H.shape[0]
    tm = _rt(Lp, 384)

    def body(of_ref, ob_ref, r_ref, gn_ref, a_ref, at_ref):
        parts = []
        for hh in range(GLA_H):
            o = of_ref[hh] + ob_ref[hh]
            parts.append(o * lax.rsqrt(jnp.mean(o * o, axis=-1, keepdims=True) + EPS))
        rv = r_ref[...]
        a = (jnp.concatenate(parts, axis=1) * gn_ref[...]) * (rv * jax.nn.sigmoid(rv))
        a_ref[...] = a.astype(BF16)
        at_ref[...] = a.T.astype(BF16)

    h128 = pl.BlockSpec((GLA_H, tm, 128), lambda i: (0, i, 0))
    return pl.pallas_call(
        body, name=name, grid=(Lp // tm,),
        in_specs=[h128, h128, pl.BlockSpec((tm, 512), lambda i: (i, C_RA // 512)), pl.BlockSpec((1, 512), lambda i: (0, 0))],
        out_specs=[pl.BlockSpec((tm, 512), lambda i: (i, 0)), pl.BlockSpec((512, tm), lambda i: (0, i))],
        out_shape=[SDS((Lp, 512), BF16), SDS((512, Lp), BF16)])(of, ob, H, gn)


def _gla_post_bwd(name, of, ob, H, gn, da, dH):
    Lp = H.shape[0]
    tm = _rt(Lp, 384)

    def body(of_ref, ob_ref, r_ref, gn_ref, da_ref, dh_in, do_ref, dh_ref, dgn_ref):
        rv = r_ref[...]
        sg = jax.nn.sigmoid(rv)
        dav = da_ref[...]
        gnv = gn_ref[...]
        ons, rs = [], []
        for hh in range(GLA_H):
            o = of_ref[hh] + ob_ref[hh]
            r = lax.rsqrt(jnp.mean(o * o, axis=-1, keepdims=True) + EPS)
            rs.append(r)
            ons.append(o * r)
        on = jnp.concatenate(ons, axis=1)
        dw = dav * (rv * sg)
        dh_ref[...] = (dav * (on * gnv) * (sg * (1.0 + rv * (1.0 - sg)))).astype(BF16)

        @pl.when(pl.program_id(0) == 0)
        def _():
            dgn_ref[...] = jnp.zeros_like(dgn_ref)

        dgn_ref[...] += jnp.sum(dw * on, axis=0, keepdims=True)
        don = dw * gnv
        for hh in range(GLA_H):
            dd = don[:, 128 * hh:128 * hh + 128]
            do_ref[hh] = rs[hh] * (dd - ons[hh] * jnp.mean(dd * ons[hh], axis=-1, keepdims=True))

    h128 = pl.BlockSpec((GLA_H, tm, 128), lambda i: (0, i, 0))
    rblk = pl.BlockSpec((tm, 512), lambda i: (i, C_RA // 512))
    vec = pl.BlockSpec((1, 512), lambda i: (0, 0))
    return pl.pallas_call(
        body, name=name, grid=(Lp // tm,),
        in_specs=[h128, h128, rblk, vec, pl.BlockSpec((tm, 512), lambda i: (i, 0)), pl.BlockSpec(memory_space=pl.ANY)],
        out_specs=[h128, rblk, vec],
        out_shape=[SDS((GLA_H, Lp, 128), F32), SDS(dH.shape, BF16), SDS((1, 512), F32)],
        input_output_aliases={5: 1})(of, ob, H, gn, da, dH)


def _gla_qkv_bwd(name, dqf, dqb, dkf, dkb, dvf, dvb, dH):
    Lp = dqf.shape[1]
    tm = _rt(Lp, 384)

    def body(dqf_ref, dqb_ref, dkf_ref, dkb_ref, dvf_ref, dvb_ref, dh_in, dh_ref):
        valid = _row_ids(pl.program_id(0), tm) >= PAD
        for hh in range(GLA_H):
            dq = (dqf_ref[hh] + dqb_ref[hh]) * (GLA_DK ** -0.5)
            dh_ref[:, 64 * hh:64 * hh + 64] = jnp.where(valid, dq, 0.0).astype(BF16)
            dh_ref[:, 256 + 64 * hh:256 + 64 * hh + 64] = jnp.where(valid, dkf_ref[hh] + dkb_ref[hh], 0.0).astype(BF16)
            dh_ref[:, 512 + 128 * hh:512 + 128 * hh + 128] = jnp.where(valid, dvf_ref[hh] + dvb_ref[hh], 0.0).astype(BF16)

    h64 = pl.BlockSpec((GLA_H, tm, 64), lambda i: (0, i, 0))
    h128 = pl.BlockSpec((GLA_H, tm, 128), lambda i: (0, i, 0))
    return pl.pallas_call(
        body, name=name, grid=(Lp // tm,),
        in_specs=[h64, h64, h64, h64, h128, h128, pl.BlockSpec(memory_space=pl.ANY)],
        out_specs=pl.BlockSpec((tm, 1024), lambda i: (i, C_QA // 1024)), out_shape=SDS(dH.shape, BF16),
        input_output_aliases={6: 0})(dqf, dqb, dkf, dkb, dvf, dvb, dH)


def _gla_gate_bwd(name, H, w2p, b2p, dgf, dgb, dH):
    Lp = H.shape[0]
    tm = _rt(Lp, 384)

    def body(lr_ref, w_ref, b_ref, dgf_ref, dgb_ref, dh_in, dh_ref, dw_ref, db_ref, dg_scr):
        valid = _row_ids(pl.program_id(0), tm) >= PAD
        for hh in range(GLA_H):
            dg_scr[:, 64 * hh:64 * hh + 64] = dgf_ref[hh]
            dg_scr[:, 256 + 64 * hh:256 + 64 * hh + 64] = dgb_ref[hh]
        lrb = lr_ref[...].astype(BF16)
        wv = w_ref[...]
        pre = jnp.dot(lrb, wv, preferred_element_type=F32) + b_ref[...]
        dpre = jnp.where(valid, dg_scr[...] * (1.0 / GLA_TAU) * jax.nn.sigmoid(-pre), 0.0)
        dpb = dpre.astype(BF16)
        dh_ref[...] = lax.dot_general(dpb, wv, NT, preferred_element_type=F32).astype(BF16)

        @pl.when(pl.program_id(0) == 0)
        def _():
            dw_ref[...] = jnp.zeros_like(dw_ref)
            db_ref[...] = jnp.zeros_like(db_ref)

        dw_ref[...] += lax.dot_general(lrb, dpb, TN, preferred_element_type=F32)
        db_ref[...] += jnp.sum(dpre, axis=0, keepdims=True)

    h64 = pl.BlockSpec((GLA_H, tm, 64), lambda i: (0, i, 0))
    lrblk = pl.BlockSpec((tm, LANE), lambda i: (i, C_LR // LANE))
    wblk = pl.BlockSpec((LANE, 512), lambda i: (0, 0))
    vec = pl.BlockSpec((1, 512), lambda i: (0, 0))
    return pl.pallas_call(
        body, name=name, grid=(Lp // tm,),
        in_specs=[lrblk, wblk, vec, h64, h64, pl.BlockSpec(memory_space=pl.ANY)],
        out_specs=[lrblk, wblk, vec],
        out_shape=[SDS(dH.shape, BF16), SDS((LANE, 512), F32), SDS((1, 512), F32)],
        scratch_shapes=[pltpu.VMEM((tm, 512), F32)], input_output_aliases={5: 0})(H, w2p, b2p, dgf, dgb, dH)


def _swap16(x):
    n = x.shape[1]
    lane = lax.broadcasted_iota(jnp.int32, x.shape, 1)
    return jnp.where(lane % 32 < 16, pltpu.roll(x, n - 16, 1), pltpu.roll(x, 16, 1))


def _head_mean(v, bd):
    hi = v.astype(BF16)
    lo = (v - hi.astype(F32)).astype(BF16)
    bdb = bd.astype(BF16)
    return jnp.dot(hi, bdb, preferred_element_type=F32) + jnp.dot(lo, bdb, preferred_element_type=F32)


def _headnorm_rope(x, gain, cos, sin, bd):
    r = lax.rsqrt(_head_mean(x * x, bd) + EPS)
    xh = x * r
    xn = xh * gain
    return xn * cos + _swap16(xn) * sin, xh, r


def _headnorm_rope_bwd(dxr, xh, r, gain, cos, sin, bd):
    dxn = cos * dxr + _swap16(sin * dxr)
    u = dxn * gain
    dx = r * (u - xh * _head_mean(u * xh, bd))
    return dx, jnp.sum(dxn * xh, axis=0, keepdims=True)


def _att_prep(name, H, gq, gk, cos, sin, bd):
    Lp = H.shape[0]
    tm = _rt(Lp, 384)

    def body(q_ref, kv_ref, gq_ref, gk_ref, c_ref, s_ref, bd_ref, qp_ref, k_ref, v_ref):
        c1, s1 = c_ref[...], s_ref[...]
        c4, s4 = jnp.concatenate([c1] * 4, axis=1), jnp.concatenate([s1] * 4, axis=1)
        xr, _, _ = _headnorm_rope(q_ref[...], gq_ref[...], c4, s4, bd_ref[...])
        xr = xr * (HD ** -0.5)
        lane = lax.broadcasted_iota(jnp.int32, (tm, LANE), 1)
        for hh in range(ATT_H):
            grp = xr[:, LANE * (hh // 2):LANE * (hh // 2) + LANE]
            e, gi = hh % 2, hh // 4
            if e != gi:
                grp = pltpu.roll(grp, 64, 1)
            keep = (lane < 64) if gi == 0 else (lane >= 64)
            qp_ref[hh] = jnp.where(keep, grp, 0.0).astype(BF16)
        kv = kv_ref[...]
        kr, _, _ = _headnorm_rope(kv[:, :LANE], gk_ref[...], c1, s1, bd_ref[0:LANE, 0:LANE])
        k_ref[...] = kr.astype(BF16)
        v_ref[...] = kv[:, LANE:].astype(BF16)

    row128 = pl.BlockSpec((tm, LANE), lambda i: (i, 0))
    return pl.pallas_call(
        body, name=name, grid=(Lp // tm,),
        in_specs=[pl.BlockSpec((tm, 512), lambda i: (i, C_QB // 512)), pl.BlockSpec((tm, 256), lambda i: (i, C_KB // 256)),
                  pl.BlockSpec((1, 512), lambda i: (0, 0)), pl.BlockSpec((1, LANE), lambda i: (0, 0)), row128, row128,
                  pl.BlockSpec((512, 512), lambda i: (0, 0))],
        out_specs=[pl.BlockSpec((ATT_H, tm, LANE), lambda i: (0, i, 0)), row128, row128],
        out_shape=[SDS((ATT_H, Lp, LANE), BF16), SDS((Lp, LANE), BF16), SDS((Lp, LANE), BF16)])(H, H, gq, gk, cos, sin, bd)


def _att_q_bwd(name, H, gq, cos, sin, bd, dqp, dH):
    Lp = H.shape[0]
    tm = _rt(Lp, 384)

    def body(q_ref, gq_ref, c_ref, s_ref, bd_ref, dqp_ref, dh_in, dh_ref, dg_ref):
        c1, s1 = c_ref[...], s_ref[...]
        c4, s4 = jnp.concatenate([c1] * 4, axis=1), jnp.concatenate([s1] * 4, axis=1)
        gqv = gq_ref[...]
        _, xh, r = _headnorm_rope(q_ref[...], gqv, c4, s4, bd_ref[...])
        lane = lax.broadcasted_iota(jnp.int32, (tm, LANE), 1)
        groups = []
        for j in range(ATT_H // 2):
            pieces = []
            for e in range(2):
                hh = 2 * j + e
                piece = dqp_ref[hh]
                if e != hh // 4:
                    piece = pltpu.roll(piece, 64, 1)
                pieces.append(piece)
            groups.append(jnp.where(lane < 64, pieces[0], pieces[1]))
        dxr = jnp.concatenate(groups, axis=1) * (HD ** -0.5)
        dx, dg = _headnorm_rope_bwd(dxr, xh, r, gqv, c4, s4, bd_ref[...])
        dh_ref[...] = dx.astype(BF16)

        @pl.when(pl.program_id(0) == 0)
        def _():
            dg_ref[...] = jnp.zeros_like(dg_ref)

        dg_ref[...] += dg

    row128 = pl.BlockSpec((tm, LANE), lambda i: (i, 0))
    qblk = pl.BlockSpec((tm, 512), lambda i: (i, C_QB // 512))
    vec = pl.BlockSpec((1, 512), lambda i: (0, 0))
    return pl.pallas_call(
        body, name=name, grid=(Lp // tm,),
        in_specs=[qblk, vec, row128, row128, pl.BlockSpec((512, 512), lambda i: (0, 0)),
                  pl.BlockSpec((ATT_H, tm, LANE), lambda i: (0, i, 0)), pl.BlockSpec(memory_space=pl.ANY)],
        out_specs=[qblk, vec], out_shape=[SDS(dH.shape, BF16), SDS((1, 512), F32)],
        input_output_aliases={6: 0})(H, gq, cos, sin, bd, dqp, dH)


def _att_kv_bwd(name, H, gk, cos, sin, bd, dkr, dvb, dH):
    Lp = H.shape[0]
    tm = _rt(Lp, 384)

    def body(kv_ref, gk_ref, c_ref, s_ref, bd_ref, dk_ref, dv_ref, dh_in, dh_ref, dg_ref):
        c1, s1 = c_ref[...], s_ref[...]
        gkv = gk_ref[...]
        bdv = bd_ref[0:LANE, 0:LANE]
        _, xh, r = _headnorm_rope(kv_ref[:, :LANE], gkv, c1, s1, bdv)
        dx, dg = _headnorm_rope_bwd(dk_ref[...], xh, r, gkv, c1, s1, bdv)
        dh_ref[:, :LANE] = dx.astype(BF16)
        dh_ref[:, LANE:] = dv_ref[...].astype(BF16)

        @pl.when(pl.program_id(0) == 0)
        def _():
            dg_ref[...] = jnp.zeros_like(dg_ref)

        dg_ref[...] += dg

    row128 = pl.BlockSpec((tm, LANE), lambda i: (i, 0))
    kvblk = pl.BlockSpec((tm, 256), lambda i: (i, C_KB // 256))
    vec = pl.BlockSpec((1, LANE), lambda i: (0, 0))
    return pl.pallas_call(
        body, name=name, grid=(Lp // tm,),
        in_specs=[kvblk, vec, row128, row128, pl.BlockSpec((512, 512), lambda i: (0, 0)), row128, row128,
                  pl.BlockSpec(memory_space=pl.ANY)],
        out_specs=[kvblk, vec], out_shape=[SDS(dH.shape, BF16), SDS((1, LANE), F32)],
        input_output_aliases={7: 0})(H, gk, cos, sin, bd, dkr, dvb, dH)


def _ridden_call(core, rider, *, name, grid, in_specs, out_specs, out_shape, scratch_shapes, args):
    n_in, n_out, n_scr = len(in_specs), len(out_specs), len(scratch_shapes)
    r_in = len(rider.ins) if rider else 0
    r_out = len(rider.out_shape) if rider else 0
    total = int(np.prod(grid))
    mid_step = max(total - 1 - RIDER_MID_BACK, 0)

    def body(*refs):
        ins, r_ins = refs[:n_in], refs[n_in:n_in + r_in]
        o0 = n_in + r_in
        outs, r_outs = refs[o0:o0 + n_out], refs[o0 + n_out:o0 + n_out + r_out]
        s0 = o0 + n_out + r_out
        scr, r_sems = refs[s0:s0 + n_scr], refs[s0 + n_scr:]
        if rider is None:
            core(ins, outs, scr)
            return
        step = pl.program_id(0)
        for ax in range(1, len(grid)):
            step = step * grid[ax] + pl.program_id(ax)
        start, mid, end = rider.hooks(r_ins, r_outs, r_sems)
        pl.when(step == 0)(start)
        core(ins, outs, scr)
        pl.when(step == mid_step)(mid)
        pl.when(step == total - 1)(end)

    any_spec = pl.BlockSpec(memory_space=pl.ANY)
    res = pl.pallas_call(
        body, name=name, grid=grid, in_specs=list(in_specs) + [any_spec] * r_in, out_specs=list(out_specs) + [any_spec] * r_out,
        out_shape=list(out_shape) + (list(rider.out_shape) if rider else []),
        scratch_shapes=list(scratch_shapes) + (list(rider.sems) if rider else []),
        input_output_aliases={n_in + a: n_out + b for a, b in rider.aliases.items()} if rider else {},
        compiler_params=_cp(True))(*args, *(rider.ins if rider else []))
    return res[:n_out], res[n_out:]


def _att_fwd(name, qp, kr, vb, bias, rider=None):
    Lp = kr.shape[0]
    tq = _rt(Lp, 384)
    sub = ATT_SUB_FWD if tq % ATT_SUB_FWD == 0 else tq

    def core(ins, outs, scr):
        q_ref, k_ref, v_ref, bias_ref = ins
        b_ref, bt_ref, lse_ref = outs
        j, i = pl.program_id(0), pl.program_id(1)
        valid = _row_ids(i, tq) >= PAD
        kk, vv, bb = k_ref[...], v_ref[...], bias_ref[...]
        chains = [(e, r) for e in range(2) for r in range(tq // sub)]
        ss = [lax.dot_general(q_ref[e, r * sub:(r + 1) * sub, :], kk, NT, preferred_element_type=F32) for e, r in chains]
        ps, ls = [], []
        for (e, r), s in zip(chains, ss):
            s = jnp.concatenate([s[:, :LANE] + bb[:, :LANE], s[:, LANE:]], axis=1)
            m = jnp.max(s, axis=-1, keepdims=True)
            p = jnp.exp(s - m)
            l = jnp.sum(p, axis=-1, keepdims=True)
            ps.append(p.astype(BF16))
            ls.append(l)
            lse_ref[e, r * sub:(r + 1) * sub, :] = m + jnp.log(l)
        os_ = [jnp.dot(p, vv, preferred_element_type=F32) * (1.0 / l) for p, l in zip(ps, ls)]
        n_sub = tq // sub
        outs = [jnp.where(valid, jnp.concatenate(os_[e * n_sub:(e + 1) * n_sub], axis=0), 0.0) for e in range(2)]
        lane = lax.broadcasted_iota(jnp.int32, (tq, LANE), 1)
        low = j < 2
        o0 = jnp.where(low, outs[0], pltpu.roll(outs[0], 64, 1))
        o1 = jnp.where(low, pltpu.roll(outs[1], 64, 1), outs[1])
        blk = jnp.where(lane < 64, o0, o1)
        b_ref[...] = blk.astype(BF16)
        bt_ref[...] = blk.T.astype(BF16)

    full = pl.BlockSpec((Lp, LANE), lambda j, i: (0, 0))
    return _ridden_call(
        core, rider, name=name, grid=(ATT_H // 2, Lp // tq),
        in_specs=[pl.BlockSpec((2, tq, LANE), lambda j, i: (j, i, 0)), full, full, pl.BlockSpec((1, Lp), lambda j, i: (0, 0))],
        out_specs=[pl.BlockSpec((tq, LANE), lambda j, i: (i, j)), pl.BlockSpec((LANE, tq), lambda j, i: (j, i)),
                   pl.BlockSpec((2, tq, 1), lambda j, i: (j, i, 0))],
        out_shape=[SDS((Lp, 512), BF16), SDS((512, Lp), BF16), SDS((ATT_H, Lp, 1), F32)],
        scratch_shapes=[], args=(qp, kr, vb, bias))


def _att_bwd(name, qp, kr, vb, bias, lse, db, b, rider=None):
    Lp = kr.shape[0]
    tq = _rt(Lp, 384)
    sub = ATT_SUB_BWD if tq % ATT_SUB_BWD == 0 else tq
    nq = Lp // tq

    def core(ins, outs, scr):
        q_ref, k_ref, v_ref, bias_ref, lse_ref, db_ref, b_ref = ins
        dq_ref, dk_ref, dv_ref = outs
        dkt_scr, dvt_scr = scr
        j, i = pl.program_id(0), pl.program_id(1)

        @pl.when((j == 0) & (i == 0))
        def _():
            dkt_scr[...] = jnp.zeros_like(dkt_scr)
            dvt_scr[...] = jnp.zeros_like(dvt_scr)

        kk, vv, bb = k_ref[...], v_ref[...], bias_ref[...]
        dbv = db_ref[...]
        rolled = pltpu.roll(dbv, 64, 1)
        lane = lax.broadcasted_iota(jnp.int32, (tq, LANE), 1)
        low = j < 2
        first = jnp.where(low, 0, 64)
        keep = (lane >= first) & (lane < first + 64)
        prod = dbv * b_ref[...].astype(F32)
        deltas = [jnp.sum(jnp.where(lane < 64, prod, 0.0), axis=-1, keepdims=True),
                  jnp.sum(jnp.where(lane >= 64, prod, 0.0), axis=-1, keepdims=True)]
        dkt, dvt = 0.0, 0.0
        for e in range(2):
            src = jnp.where(low, dbv, rolled) if e == 0 else jnp.where(low, rolled, dbv)
            dop = jnp.where(keep, src, 0.0).astype(BF16)
            pbs, dss = [], []
            for r in range(tq // sub):
                rs = slice(r * sub, (r + 1) * sub)
                s = lax.dot_general(q_ref[e, rs, :], kk, NT, preferred_element_type=F32)
                s = jnp.concatenate([s[:, :LANE] + bb[:, :LANE], s[:, LANE:]], axis=1)
                p = jnp.exp(s - lse_ref[e, rs, :])
                dp = lax.dot_general(dop[rs], vv, NT, preferred_element_type=F32)
                ds = (p * (dp - deltas[e][rs])).astype(BF16)
                dq_ref[e, rs, :] = jnp.where(keep[rs], jnp.dot(ds, kk, preferred_element_type=F32), 0.0)
                pbs.append(p.astype(BF16))
                dss.append(ds)
            dkt = dkt + lax.dot_general(q_ref[e], jnp.concatenate(dss, axis=0), TN, preferred_element_type=F32)
            dvt = dvt + lax.dot_general(dop, jnp.concatenate(pbs, axis=0), TN, preferred_element_type=F32)
        dkt_scr[...] += dkt
        dvt_scr[...] += dvt

        @pl.when((j == ATT_H // 2 - 1) & (i == nq - 1))
        def _():
            dk_ref[...] = dkt_scr[...].T
            dv_ref[...] = dvt_scr[...].T

    full = pl.BlockSpec((Lp, LANE), lambda j, i: (0, 0))
    pair = pl.BlockSpec((2, tq, LANE), lambda j, i: (j, i, 0))
    return _ridden_call(
        core, rider, name=name, grid=(ATT_H // 2, nq),
        in_specs=[pair, full, full, pl.BlockSpec((1, Lp), lambda j, i: (0, 0)), pl.BlockSpec((2, tq, 1), lambda j, i: (j, i, 0)),
                  pl.BlockSpec((tq, LANE), lambda j, i: (i, j)), pl.BlockSpec((tq, LANE), lambda j, i: (i, j))],
        out_specs=[pair, full, full],
        out_shape=[SDS((ATT_H, Lp, LANE), F32), SDS((Lp, LANE), F32), SDS((Lp, LANE), F32)],
        scratch_shapes=[pltpu.VMEM((LANE, Lp), F32), pltpu.VMEM((LANE, Lp), F32)], args=(qp, kr, vb, bias, lse, db, b))


def _rope_tables(n_tok):
    f32 = np.float32
    rows = n_tok // GRID_W
    row = np.repeat(np.arange(rows), GRID_W).astype(f32)
    col = np.tile(np.arange(GRID_W), rows).astype(f32)
    axis_dim = HD // 2
    inv = np.power(f32(ROPE_THETA), -np.arange(0, axis_dim, 2, dtype=f32) / f32(axis_dim)).astype(f32)
    ang = np.concatenate([row[:, None] * inv, col[:, None] * inv], axis=-1).astype(f32)
    ang = np.concatenate([np.zeros((LANE, axis_dim), f32), ang], axis=0)
    c, s = np.cos(ang).astype(f32), np.sin(ang).astype(f32)
    c64 = np.concatenate([c[:, :16], c[:, :16], c[:, 16:], c[:, 16:]], axis=1)
    s64 = np.concatenate([-s[:, :16], s[:, :16], -s[:, 16:], s[:, 16:]], axis=1)
    return jnp.asarray(np.concatenate([c64, c64], axis=1)), jnp.asarray(np.concatenate([s64, s64], axis=1))


def _ffn_fwd(tag, h, W, l, j):
    Lp = h.shape[0]
    tm = _rt(Lp, 384)
    g = W["norm_gains"][l, 2 * j].reshape(1, D)

    def body(h_ref, g_ref, wg_ref, wu_ref, wd_ref, h2_ref, a_ref, b_ref, st_ref, nt_ref, pad_scr):
        x = h_ref[...]
        y = x * lax.rsqrt(jnp.mean(x * x, axis=-1, keepdims=True) + EPS) * g_ref[...]
        nt_ref[...] = y.T.astype(BF16)
        nv = y.astype(BF16)
        pad_scr[:, FF_S:] = jnp.zeros((tm, FF_P - FF_S), F32)
        acc = 0.0
        for s in range(NSH):
            a = jnp.dot(nv, wg_ref[s], preferred_element_type=F32)
            b = jnp.dot(nv, wu_ref[s], preferred_element_type=F32)
            sv = a * jax.nn.sigmoid(a) * b
            a_ref[s, :, :FF_S] = a.astype(BF16)
            a_ref[s, :, FF_S:] = jnp.zeros((tm, FF_P - FF_S), BF16)
            b_ref[s, :, :FF_S] = b.astype(BF16)
            b_ref[s, :, FF_S:] = jnp.zeros((tm, FF_P - FF_S), BF16)
            pad_scr[:, :FF_S] = sv
            st_ref[s] = pad_scr[...].T.astype(BF16)
            acc = acc + jnp.dot(sv.astype(BF16), wd_ref[s], preferred_element_type=F32)
        h2_ref[...] = x + 0.5 * acc

    once = dict(pipeline_mode=pl.Buffered(1))
    wup = pl.BlockSpec((NSH, None, D, FF_S), lambda i: (0, l, j, 0), **once)
    row = pl.BlockSpec((tm, D), lambda i: (i, 0))
    slab = pl.BlockSpec((NSH, tm, FF_P), lambda i: (0, i, 0))
    h2, a, b, st, nt = pl.pallas_call(
        body, name=f"ffn_{tag}", grid=(Lp // tm,),
        in_specs=[row, pl.BlockSpec((1, D), lambda i: (0, 0)), wup, wup,
                  pl.BlockSpec((NSH, None, FF_S, D), lambda i: (0, l, j, 0), **once)],
        out_specs=[row, slab, slab, pl.BlockSpec((NSH, FF_P, tm), lambda i: (0, 0, i)), pl.BlockSpec((D, tm), lambda i: (0, i))],
        out_shape=[SDS((Lp, D), F32), SDS((NSH, Lp, FF_P), BF16), SDS((NSH, Lp, FF_P), BF16), SDS((NSH, FF_P, Lp), BF16),
                   SDS((D, Lp), BF16)],
        scratch_shapes=[pltpu.VMEM((tm, FF_P), F32)], compiler_params=_cp(True))(h, g, W["gate"], W["up"], W["down"])
    return h2, dict(h=h, g=g, nt=nt, a=a, b=b, st=st)


def _ffn_bwd(tag, dh, sv, W, G, l, j, rider=None):
    Lp = dh.shape[0]
    tn = 512
    (da, db, dh_prev, dg), carried = _ffn_bwd_acts(tag, dh, sv, W, l, j, rider)
    G["down"] = _mm(f"bwd_{tag}", sv["st"], dh, grid=(NSH, D // tn),
                    a_spec=pl.BlockSpec((None, FF_P, Lp), lambda s, n: (s, 0, 0)), a_sl=(slice(0, FF_S), slice(None)),
                    b_spec=pl.BlockSpec((Lp, tn), lambda s, n: (0, n)),
                    o_spec=pl.BlockSpec((None, None, FF_S, tn), lambda s, n: (s, j, 0, n)), o_shape=(NSH, 2, FF_S, D), o_dtype=F32,
                    dims=NN, acc_shape=None, scale=0.5, into=G.get("down"))
    for key, dact in (("gate", da), ("up", db)):
        G[key] = _mm(f"bw{key[0]}_{tag}", sv["nt"], dact, grid=(NSH, D // tn),
                     a_spec=pl.BlockSpec((tn, Lp), lambda s, m: (m, 0)),
                     b_spec=pl.BlockSpec((None, Lp, FF_P), lambda s, m: (s, 0, 0)), b_sl=(slice(None), slice(0, FF_S)),
                     o_spec=pl.BlockSpec((None, None, tn, FF_S), lambda s, m: (s, j, m, 0)), o_shape=(NSH, 2, D, FF_S),
                     o_dtype=F32, dims=NN, acc_shape=None, into=G.get(key))
    return dh_prev, dg, carried


def _ffn_bwd_acts(tag, dh, sv, W, l, j, rider=None):
    Lp = dh.shape[0]
    tm = _rt(Lp, 384)

    def core(ins, outs, scr):
        dh_ref, h_ref, g_ref, wd_ref, wg_ref, wu_ref, a_ref, b_ref = ins
        da_ref, db_ref, dho_ref, dg_ref = outs
        dhv = dh_ref[...]
        dhb = dhv.astype(BF16)
        dn = 0.0
        for s in range(NSH):
            ds = 0.5 * lax.dot_general(dhb, wd_ref[s], NT, preferred_element_type=F32)
            av = a_ref[s, :, :FF_S].astype(F32)
            bv = b_ref[s, :, :FF_S].astype(F32)
            sg = jax.nn.sigmoid(av)
            da = (ds * bv * (sg * (1.0 + av * (1.0 - sg)))).astype(BF16)
            db = (ds * (av * sg)).astype(BF16)
            da_ref[s, :, :FF_S] = da
            da_ref[s, :, FF_S:] = jnp.zeros((tm, FF_P - FF_S), BF16)
            db_ref[s, :, :FF_S] = db
            db_ref[s, :, FF_S:] = jnp.zeros((tm, FF_P - FF_S), BF16)
            dn = dn + (lax.dot_general(da, wg_ref[s], NT, preferred_element_type=F32)
                       + lax.dot_general(db, wu_ref[s], NT, preferred_element_type=F32))
        x = h_ref[...]
        r = lax.rsqrt(jnp.mean(x * x, axis=-1, keepdims=True) + EPS)
        xh = x * r
        u = dn * g_ref[...]
        dho_ref[...] = dhv + r * (u - xh * jnp.mean(u * xh, axis=-1, keepdims=True))

        @pl.when(pl.program_id(0) == 0)
        def _():
            dg_ref[...] = jnp.zeros_like(dg_ref)

        dg_ref[...] += jnp.sum(dn * xh, axis=0, keepdims=True)

    once = dict(pipeline_mode=pl.Buffered(1))
    wup = pl.BlockSpec((NSH, None, D, FF_S), lambda i: (0, l, j, 0), **once)
    row = pl.BlockSpec((tm, D), lambda i: (i, 0))
    slab = pl.BlockSpec((NSH, tm, FF_P), lambda i: (0, i, 0))
    vec = pl.BlockSpec((1, D), lambda i: (0, 0))
    return _ridden_call(
        core, rider, name=f"bffn_{tag}", grid=(Lp // tm,),
        in_specs=[row, row, vec, pl.BlockSpec((NSH, None, FF_S, D), lambda i: (0, l, j, 0), **once), wup, wup, slab, slab],
        out_specs=[slab, slab, row, vec],
        out_shape=[SDS((NSH, Lp, FF_P), BF16), SDS((NSH, Lp, FF_P), BF16), SDS((Lp, D), F32), SDS((1, D), F32)],
        scratch_shapes=[], args=(dh, sv["h"], sv["g"], W["down"], W["gate"], W["up"], sv["a"], sv["b"]))


def _mixer_fwd(tag, h, W, winp, C, l, G_chunks, rider=None):
    Lp = h.shape[0]
    tm = _rt(Lp, 1408)
    g = W["norm_gains"][l, 1].reshape(1, D)
    H, zt = _norm_project(f"win_{tag}", h, g, winp)
    w2p, b2p = C["w2p"][l], C["b2p"][l]
    qh, kh, vh, gf, gb = _gla_prep(f"glap_{tag}", H, w2p, b2p)
    of, sf = _gla_fwd(f"glaf_{tag}", qh, kh, gf, vh, False, G_chunks)
    ob, sb = _gla_fwd(f"glar_{tag}", qh, kh, gb, vh, True, G_chunks)
    gn = W["gn"][l].reshape(1, 512)
    a, at = _gla_post(f"glao_{tag}", of, ob, H, gn)
    gq, gk = C["gq"][l], C["gk"][l]
    qp, kr, vb = _att_prep(f"attp_{tag}", H, gq, gk, C["cos"], C["sin"], C["bd"])
    (b, bt, lse), carried = _att_fwd(f"attf_{tag}", qp, kr, vb, C["bias"], rider)
    if rider is not None:
        W = rider.updated(W, carried)
    row = pl.BlockSpec((tm, D), lambda i: (i, 0))
    proj = functools.partial(
        _mm, grid=(Lp // tm,), a_spec=pl.BlockSpec((tm, 512), lambda i: (i, 0)),
        b_spec=pl.BlockSpec((NSH, None, 512, 256), lambda i: (0, l, 0, 0)), o_spec=row, o_shape=(Lp, D), o_dtype=F32, dims=NN,
        acc_shape=None, shards="cols")
    pa = proj(f"pa_{tag}", a, W["wpa"])
    pb = proj(f"pb_{tag}", b, W["wpb"])
    bm = W["bm"][l]
    y, yt = _merge_fwd(f"mrg_{tag}", H, pa, pb, bm)
    h2 = _mm(f"wout_{tag}", y, W["wout"], grid=(Lp // tm,), a_spec=row,
             b_spec=pl.BlockSpec((NSH, None, 256, D), lambda i: (0, l, 0, 0)), o_spec=row, o_shape=(Lp, D), o_dtype=F32,
             dims=NN, acc_shape=None, res=h, res_spec=row, shards="rows")
    sv = dict(h=h, g=g, zt=zt, H=H, w2p=w2p, b2p=b2p, qh=qh, kh=kh, vh=vh, gf=gf, gb=gb, of=of, ob=ob, sf=sf, sb=sb, gn=gn,
              at=at, gq=gq, gk=gk, qp=qp, kr=kr, vb=vb, b=b, bt=bt, lse=lse, pa=pa, pb=pb, bm=bm, yt=yt, winp=winp)
    return h2, sv, W


def _mixer_bwd(tag, dh, sv, W, C, G, S, l, G_chunks, rider=None):
    Lp = dh.shape[0]
    tm = _rt(Lp, 1408)
    H = sv["H"]
    row = pl.BlockSpec((tm, D), lambda i: (i, 0))
    dy = _mm(f"bdy_{tag}", dh, W["wout"], grid=(Lp // tm,), a_spec=row,
             b_spec=pl.BlockSpec((NSH, None, 256, D), lambda i: (0, l, 0, 0)), o_spec=row, o_shape=(Lp, D), o_dtype=F32, dims=NT,
             acc_shape=None, shards="rows")
    tn = 512
    G["wout"] = _mm(f"bwo_{tag}", sv["yt"], dh, grid=(NSH, D // tn), a_spec=pl.BlockSpec((256, Lp), lambda s, n: (s, 0)),
                    b_spec=pl.BlockSpec((Lp, tn), lambda s, n: (0, n)),
                    o_spec=pl.BlockSpec((None, 256, tn), lambda s, n: (s, 0, n)), o_shape=(NSH, 256, D), o_dtype=F32, dims=NN,
                    acc_shape=None)
    dpa, dpb, dH, S["bm"][l] = _merge_bwd(f"bmrg_{tag}", H, sv["pa"], sv["pb"], sv["bm"], dy)
    dbranch = {}
    for key, dp, xt in (("wpa", dpa, sv["at"]), ("wpb", dpb, sv["bt"])):
        dbranch[key] = _mm(f"bx{key[2]}_{tag}", dp, W[key], grid=(Lp // tm,), a_spec=row,
                           b_spec=pl.BlockSpec((NSH, None, 512, 256), lambda i: (0, l, 0, 0)),
                           o_spec=pl.BlockSpec((tm, 512), lambda i: (i, 0)), o_shape=(Lp, 512), o_dtype=F32, dims=NT,
                           acc_shape=None, shards="cols")
        G[key] = _mm(f"bw{key[2]}_{tag}", xt, dp, grid=(NSH,), a_spec=pl.BlockSpec((512, Lp), lambda s: (0, 0)),
                     b_spec=pl.BlockSpec((Lp, 256), lambda s: (0, s)),
                     o_spec=pl.BlockSpec((None, 512, 256), lambda s: (s, 0, 0)), o_shape=(NSH, 512, 256), o_dtype=F32, dims=NN,
                     acc_shape=None)
    (dqp, dkr, dvb), carried = _att_bwd(f"attb_{tag}", sv["qp"], sv["kr"], sv["vb"], C["bias"], sv["lse"], dbranch["wpb"],
                                        sv["b"], rider)
    dH, S["gq"][l] = _att_q_bwd(f"attq_{tag}", H, sv["gq"], C["cos"], C["sin"], C["bd"], dqp, dH)
    dH, S["gk"][l] = _att_kv_bwd(f"attk_{tag}", H, sv["gk"], C["cos"], C["sin"], C["bd"], dkr, dvb, dH)
    do, dH, S["gn"][l] = _gla_post_bwd(f"bglo_{tag}", sv["of"], sv["ob"], H, sv["gn"], dbranch["wpa"], dH)
    dqf, dkf, dgf, dvf = _gla_bwd(f"bglf_{tag}", sv["qh"], sv["kh"], sv["gf"], sv["vh"], sv["sf"], do, False, G_chunks)
    dqb, dkb, dgb, dvr = _gla_bwd(f"bglr_{tag}", sv["qh"], sv["kh"], sv["gb"], sv["vh"], sv["sb"], do, True, G_chunks)
    dH = _gla_qkv_bwd(f"bglq_{tag}", dqf, dqb, dkf, dkb, dvf, dvr, dH)
    dH, S["w2p"][l], S["b2p"][l] = _gla_gate_bwd(f"bglg_{tag}", H, sv["w2p"], sv["b2p"], dgf, dgb, dH)
    tmm = 256
    G["winp"] = _mm(f"bwi_{tag}", sv["zt"], dH, grid=(HP // W_CHUNK, D // tmm), a_spec=pl.BlockSpec((tmm, Lp), lambda n, m: (m, 0)),
                    b_spec=pl.BlockSpec((Lp, W_CHUNK), lambda n, m: (0, n)), o_spec=pl.BlockSpec((tmm, W_CHUNK), lambda n, m: (m, n)),
                    o_shape=(D, HP), o_dtype=F32, dims=NN, acc_shape=None)
    dh_prev, dg = _norm_project_bwd(f"bdz_{tag}", dH, sv["winp"], sv["h"], sv["g"], dh)
    return dh_prev, dg, carried


def _winp_layer(win_g, l):
    return _win_to_padded(jnp.transpose(win_g[:, l], (1, 0, 2)).reshape(D, D_IN))


def _local_step(x2, tgt2, W, comm=None):
    n_tok = x2.shape[0]
    Lp = n_tok + LANE
    nc = Lp // CHUNK
    g_chunks = max(g for g in (1, 2, 3, 6) if nc % g == 0)
    cos, sin = _rope_tables(n_tok)
    bd = jnp.asarray(np.kron(np.eye(ATT_H, dtype=np.float32), np.full((HD, HD), 1.0 / HD, np.float32)))
    bias = jnp.where(jnp.arange(Lp) >= PAD, 0.0, -1e30).astype(F32).reshape(1, Lp)
    w2, b2 = W["w2"], W["b2"]
    w2p = jnp.zeros((DEPTH, LANE, 512), F32)
    w2p = w2p.at[:, 0:GLA_RANK, 0:256].set(w2[:, 0]).at[:, GLA_RANK:2 * GLA_RANK, 256:512].set(w2[:, 1]).astype(BF16)
    C = dict(cos=cos, sin=sin, bd=bd, bias=bias, w2p=w2p, b2p=b2.reshape(DEPTH, 1, 512),
             gq=jnp.tile(W["qn"], (1, ATT_H)).reshape(DEPTH, 1, 512), gk=jnp.tile(W["kn"], (1, ATT_KV)).reshape(DEPTH, 1, LANE))
    h = jnp.concatenate([jnp.zeros((PAD, D), F32), W["meta"], x2], axis=0)
    saved = []
    for l in range(DEPTH):
        h, s0 = _ffn_fwd(f"l{l}a", h, W, l, 0)
        rider = comm.fwd_rider(W, l) if comm else None
        h, sm, W = _mixer_fwd(f"l{l}m", h, W, _winp_layer(W["win"], l), C, l, g_chunks, rider)
        h, s1 = _ffn_fwd(f"l{l}b", h, W, l, 1)
        saved.append((s0, sm, s1))
    dh, dfin, loss = _loss_head("loss_head", h, W["fin"].reshape(1, D), tgt2)
    S = dict(bm=[None] * DEPTH, gq=[None] * DEPTH, gk=[None] * DEPTH, gn=[None] * DEPTH, w2p=[None] * DEPTH,
             b2p=[None] * DEPTH, ng=[[None] * 3 for _ in range(DEPTH)])
    Gs = [None] * DEPTH
    for l in reversed(range(DEPTH)):
        s0, sm, s1 = saved[l]
        G = {}
        rider = comm.ffn_rider(l) if comm else None
        dh, S["ng"][l][2], carried = _ffn_bwd(f"l{l}b", dh, s1, W, G, l, 1, rider)
        if rider is not None:
            comm.ffn_carried(l, carried)
        rider = comm.bwd_rider(l) if comm else None
        dh, S["ng"][l][1], carried = _mixer_bwd(f"l{l}m", dh, sm, W, C, G, S, l, g_chunks, rider)
        if rider is not None:
            comm.bwd_carried(l, carried)
        dh, S["ng"][l][0], _ = _ffn_bwd(f"l{l}a", dh, s0, W, G, l, 0)
        Gs[l] = G
        if comm:
            comm.layer_done(l, G)
    small = dict(
        meta=dh[PAD:LANE],
        norm_gains=jnp.stack([jnp.concatenate(S["ng"][l], axis=0) for l in range(DEPTH)]),
        w2=jnp.stack([jnp.stack([S["w2p"][l][0:GLA_RANK, 0:256], S["w2p"][l][GLA_RANK:2 * GLA_RANK, 256:512]]) for l in range(DEPTH)]),
        b2=jnp.stack([S["b2p"][l].reshape(2, 256) for l in range(DEPTH)]),
        gn=jnp.concatenate(S["gn"], axis=0),
        qn=jnp.stack([S["gq"][l].reshape(ATT_H, HD).sum(0) for l in range(DEPTH)]),
        kn=jnp.stack([S["gk"][l].reshape(ATT_KV, HD).sum(0) for l in range(DEPTH)]),
        bm=jnp.stack(S["bm"]),
        fin=dfin.reshape(D),
    )
    return loss[0, 0], dh, Gs, small


def _win_to_padded(w):
    pad = jnp.zeros(w.shape[:-1] + (HP - D_IN,), w.dtype)
    return jnp.concatenate([w[..., 2336:4384], w[..., 0:1536], w[..., 1568:2336], w[..., 1536:1568], pad], axis=-1)


def _win_from_padded(w):
    return jnp.concatenate([w[..., 2048:3584], w[..., 4352:4384], w[..., 3584:4352], w[..., 0:2048]], axis=-1)


def _assemble(g):
    W = dict(gate=g["gate"], up=g["up"], down=g["down"], win=g["win"], wpa=g["wpa"], wpb=g["wpb"], wout=g["wout"])
    sm = g["small"]
    parts = _unpack(sm, SHARDED_SMALL)
    W["meta"] = jnp.transpose(parts["meta"], (1, 0, 2)).reshape(N_META, D)
    W["norm_gains"] = jnp.transpose(parts["norm_gains"], (1, 2, 0, 3)).reshape(DEPTH, 3, D)
    W["w2"] = jnp.transpose(parts["w2"], (1, 2, 3, 0, 4)).reshape(DEPTH, 2, GLA_RANK, 256)
    W["b2"] = jnp.transpose(parts["b2"], (1, 2, 0, 3)).reshape(DEPTH, 2, 256)
    W["bm"] = jnp.transpose(parts["bm"], (1, 2, 0, 3)).reshape(DEPTH, 2, D)
    return W


SHARDED_SMALL = dict(meta=(N_META, 256), norm_gains=(DEPTH, 3, 256), w2=(DEPTH, 2, GLA_RANK, 64), b2=(DEPTH, 2, 64),
                     bm=(DEPTH, 2, 256))
FULL_SMALL = dict(meta=(N_META, D), norm_gains=(DEPTH, 3, D), w2=(DEPTH, 2, GLA_RANK, 256), b2=(DEPTH, 2, 256),
                  gn=(DEPTH, 512), qn=(DEPTH, HD), kn=(DEPTH, HD), bm=(DEPTH, 2, D), fin=(D,))


def _pack(parts, table, rows):
    flat = jnp.concatenate([parts[k].reshape(-1).astype(F32) for k in table])
    return jnp.pad(flat, (0, rows * LANE - flat.shape[0])).reshape(rows, LANE)


def _unpack(packed, table):
    lead = packed.shape[:-2]
    flat = packed.reshape(lead + (-1,))
    out, off = {}, 0
    for k, shp in table.items():
        n = int(np.prod(shp))
        out[k] = flat[..., off:off + n].reshape(lead + tuple(shp))
        off += n
    return out


def _rows_for(table, mult):
    n = sum(int(np.prod(s)) for s in table.values())
    return -(-n // (LANE * mult)) * mult


SMALL_ROWS = _rows_for(SHARDED_SMALL, 16)
GRAD_ROWS = _rows_for(dict(FULL_SMALL, loss=(1,)), 8)


def _place():
    x, y, c = lax.axis_index("x"), lax.axis_index("y"), lax.axis_index("c")
    return x, y, c


def _other_chips(x, y):
    return [(1 - x, y), (x, 1 - y), (1 - x, 1 - y)]


def _cast_place(name, w3, slot):
    A, R, Cc = w3.shape
    tr = _rt(R, 512, 16)

    def body(p_ref, w_ref, o_ref):
        o_ref[...] = w_ref[...].astype(BF16)

    return pl.pallas_call(
        body, name=name,
        grid_spec=pltpu.PrefetchScalarGridSpec(
            num_scalar_prefetch=1, grid=(A, R // tr),
            in_specs=[pl.BlockSpec((None, tr, Cc), lambda a, r, p_ref: (a, r, 0))],
            out_specs=pl.BlockSpec((None, None, tr, Cc), lambda a, r, p_ref: (p_ref[0], a, r, 0))),
        out_shape=SDS((NSH, A, R, Cc), BF16))(slot, w3)


class _GatherRider:
    def __init__(self, keys, bufs, layers):
        n = len(bufs)
        self.keys, self.ins, self.layers = keys, list(bufs), layers
        self.out_shape = [SDS(b.shape, b.dtype) for b in bufs]
        self.aliases = {a: a for a in range(n)}
        self.sems = [pltpu.SemaphoreType.DMA((n, 3)) for _ in range(4)]

    def updated(self, W, carried):
        return {**W, **dict(zip(self.keys, carried))}

    def hooks(self, ins, outs, sems):
        send, recv, fsend, frecv = sems
        n = len(outs)

        def rows(a, slot, core):
            ref, l = outs[a], self.layers[a]
            half = ref.shape[-2] // 2
            return ref.at[slot, pl.ds(core * half, half)] if l is None else ref.at[slot, l, pl.ds(core * half, half)]

        def ici(a, k, part, px, py, c):
            return pltpu.make_async_remote_copy(src_ref=part, dst_ref=part, send_sem=send.at[a, k], recv_sem=recv.at[a, k],
                                                device_id=(px, py, c), device_id_type=MESH)

        def d2d(a, k, part, x, y, c):
            return pltpu.make_async_remote_copy(src_ref=part, dst_ref=part, send_sem=fsend.at[a, k], recv_sem=frecv.at[a, k],
                                                device_id=(x, y, 1 - c), device_id_type=MESH)

        def start():
            x, y, c = _place()
            for a in range(n):
                for k, (px, py) in enumerate(_other_chips(x, y)):
                    ici(a, k, rows(a, 2 * x + y, c), px, py, c).start()

        def mid():
            x, y, c = _place()
            for a in range(n):
                for k, (px, py) in enumerate(_other_chips(x, y)):
                    landed = rows(a, 2 * px + py, c)
                    ici(a, k, landed, px, py, c).wait_recv()
                    d2d(a, k, landed, x, y, c).start()

        def end():
            x, y, c = _place()
            for a in range(n):
                for k, (px, py) in enumerate(_other_chips(x, y)):
                    d2d(a, k, rows(a, 2 * px + py, 1 - c), x, y, c).wait_recv()
                    ici(a, k, rows(a, 2 * x + y, c), px, py, c).wait_send()
                    d2d(a, k, rows(a, 2 * px + py, c), x, y, c).wait_send()

        return start, mid, end


class _ChipExchangeRider:
    def __init__(self, arrs):
        n = len(arrs)
        self.ins = list(arrs)
        self.out_shape = [SDS((3,) + a.shape[1:], a.dtype) for a in arrs]
        self.aliases = {}
        self.sems = [pltpu.SemaphoreType.DMA((n, 3)), pltpu.SemaphoreType.DMA((n, 3))]

    def hooks(self, ins, outs, sems):
        send, recv = sems

        def copy(a, k, px, py, c):
            return pltpu.make_async_remote_copy(src_ref=ins[a].at[2 * px + py], dst_ref=outs[a].at[k], send_sem=send.at[a, k],
                                                recv_sem=recv.at[a, k], device_id=(px, py, c), device_id_type=MESH)

        def start():
            x, y, c = _place()
            for a in range(len(ins)):
                for k, (px, py) in enumerate(_other_chips(x, y)):
                    copy(a, k, px, py, c).start()

        def mid():
            pass

        def end():
            x, y, c = _place()
            for a in range(len(ins)):
                for k, (px, py) in enumerate(_other_chips(x, y)):
                    copy(a, k, px, py, c).wait()

        return start, mid, end


def _run_rider(name, rider):
    r_in, r_out = len(rider.ins), len(rider.out_shape)

    def body(*refs):
        start, mid, end = rider.hooks(refs[:r_in], refs[r_in:r_in + r_out], refs[r_in + r_out:])
        start()
        mid()
        end()

    any_spec = pl.BlockSpec(memory_space=pl.ANY)
    return pl.pallas_call(
        body, name=name, in_specs=[any_spec] * r_in, out_specs=[any_spec] * r_out, out_shape=list(rider.out_shape),
        scratch_shapes=list(rider.sems), input_output_aliases=dict(rider.aliases))(*rider.ins)


class _PairExchangeRider:
    def __init__(self, arrs):
        n = len(arrs)
        self.ins = list(arrs)
        self.out_shape = [SDS((NSH, a.shape[1] // 2, a.shape[2]), a.dtype) for a in arrs]
        self.aliases = {}
        self.sems = [pltpu.SemaphoreType.DMA((n,)), pltpu.SemaphoreType.DMA((n,))]

    def hooks(self, ins, outs, sems):
        send, recv = sems

        def copy(a):
            x, y, c = _place()
            half = ins[a].shape[1] // 2
            return pltpu.make_async_remote_copy(
                src_ref=ins[a].at[:, pl.ds((1 - c) * half, half)], dst_ref=outs[a], send_sem=send.at[a], recv_sem=recv.at[a],
                device_id=(x, y, 1 - c), device_id_type=MESH)

        def start():
            for a in range(len(ins)):
                copy(a).start()

        def mid():
            pass

        def end():
            for a in range(len(ins)):
                copy(a).wait()

        return start, mid, end


def _pair_add(name, g, p, core):
    _, Rh, Cc = p.shape
    tr = _rt(Rh, 512, 16)
    nr = Rh // tr

    def body(c_ref, g_ref, p_ref, o_ref, ob_ref):
        v = g_ref[...] + p_ref[...]
        o_ref[...] = v
        ob_ref[...] = v.astype(BF16)

    blk = pl.BlockSpec((None, tr, Cc), lambda s, r, c_ref: (s, r, 0))
    return pl.pallas_call(
        body, name=name,
        grid_spec=pltpu.PrefetchScalarGridSpec(
            num_scalar_prefetch=1, grid=(NSH, nr),
            in_specs=[pl.BlockSpec((None, tr, Cc), lambda s, r, c_ref: (s, c_ref[0] * nr + r, 0)), blk],
            out_specs=[blk, blk]),
        out_shape=[SDS((NSH, Rh, Cc), F32), SDS((NSH, Rh, Cc), BF16)])(core, g, p)


def _chip_add(name, hsum, q, chip, core, l, into):
    _, Rh, Cc = hsum.shape
    tr = _rt(Rh, 512, 8)
    nr = Rh // tr

    def body(*refs):
        h_ref, q_ref, o_ref = refs[2], refs[3], refs[-1]
        o_ref[...] = ((h_ref[...] + q_ref[0].astype(F32)) + q_ref[1].astype(F32)) + q_ref[2].astype(F32)

    in_specs = [pl.BlockSpec((None, tr, Cc), lambda r, p_ref, c_ref: (p_ref[0], r, 0)),
                pl.BlockSpec((3, tr, Cc), lambda r, p_ref, c_ref: (0, r, 0))]
    args = [chip, core, hsum, q]
    aliases = {}
    if into is not None:
        in_specs.append(pl.BlockSpec(memory_space=pl.ANY))
        args.append(into)
        aliases = {4: 0}
    return pl.pallas_call(
        body, name=name,
        grid_spec=pltpu.PrefetchScalarGridSpec(
            num_scalar_prefetch=2, grid=(nr,), in_specs=in_specs,
            out_specs=pl.BlockSpec((None, tr, Cc), lambda r, p_ref, c_ref: (l, c_ref[0] * nr + r, 0))),
        out_shape=SDS((DEPTH, 2 * Rh, Cc), F32), input_output_aliases=aliases)(*args)


def _pair_share(arrs):
    n = len(arrs)

    def body(*refs):
        outs = refs[n:2 * n]
        send, recv = refs[2 * n:]
        x, y, c = _place()
        cps = []
        for a in range(n):
            half = outs[a].shape[1] // 2
            mine = outs[a].at[:, pl.ds(c * half, half)]
            cp = pltpu.make_async_remote_copy(
                src_ref=mine, dst_ref=mine, send_sem=send.at[a], recv_sem=recv.at[a],
                device_id=(x, y, 1 - c), device_id_type=MESH)
            cp.start()
            cps.append(cp)
        for a, cp in enumerate(cps):
            cp.wait_send()
            half = outs[a].shape[1] // 2
            theirs = outs[a].at[:, pl.ds((1 - c) * half, half)]
            pltpu.make_async_remote_copy(
                src_ref=theirs, dst_ref=theirs, send_sem=send.at[a], recv_sem=recv.at[a],
                device_id=(x, y, 1 - c), device_id_type=MESH).wait_recv()

    any_spec = pl.BlockSpec(memory_space=pl.ANY)
    return pl.pallas_call(
        body, name="pair_share", in_specs=[any_spec] * n, out_specs=[any_spec] * n,
        out_shape=[SDS(a.shape, a.dtype) for a in arrs], input_output_aliases={a: a for a in range(n)},
        scratch_shapes=[pltpu.SemaphoreType.DMA((n,)), pltpu.SemaphoreType.DMA((n,))])(*arrs)


def _allreduce_small(v):
    rows = v.shape[0]

    def body(v_ref, o_ref, buf, send, recv):
        x, y, c = _place()
        me = 4 * x + 2 * y + c
        buf[me] = v_ref[...]
        cps = []
        k = 0
        for dx in range(2):
            for dy in range(2):
                for dc in range(2):
                    if dx + dy + dc == 0:
                        continue
                    cp = pltpu.make_async_remote_copy(
                        src_ref=v_ref, dst_ref=buf.at[me], send_sem=send.at[k], recv_sem=recv.at[k],
                        device_id=(jnp.bitwise_xor(x, dx), jnp.bitwise_xor(y, dy), jnp.bitwise_xor(c, dc)), device_id_type=MESH)
                    cp.start()
                    cps.append((cp, dx, dy, dc))
                    k += 1
        for k, (cp, dx, dy, dc) in enumerate(cps):
            cp.wait_send()
            src = 4 * jnp.bitwise_xor(x, dx) + 2 * jnp.bitwise_xor(y, dy) + jnp.bitwise_xor(c, dc)
            pltpu.make_async_remote_copy(
                src_ref=v_ref, dst_ref=buf.at[src], send_sem=send.at[k], recv_sem=recv.at[k],
                device_id=(x, y, c), device_id_type=MESH).wait_recv()
        acc = buf[0]
        for d in range(1, 8):
            acc = acc + buf[d]
        o_ref[...] = acc

    vm = pl.BlockSpec(memory_space=pltpu.VMEM)
    return pl.pallas_call(
        body, name="allreduce_small", in_specs=[vm], out_specs=vm, out_shape=SDS((rows, LANE), F32),
        scratch_shapes=[pltpu.VMEM((8, rows, LANE), F32), pltpu.SemaphoreType.DMA((7,)), pltpu.SemaphoreType.DMA((7,))])(v)


def _adamw(name, w, g, m, v):
    A, R, Cc = w.shape
    tr = _rt(R, 512, 8)

    def body(w_ref, g_ref, m_ref, v_ref, d_ref, mo_ref, vo_ref):
        gv = g_ref[...]
        mn = ADAM_B1 * m_ref[...] + (1.0 - ADAM_B1) * gv
        vn = ADAM_B2 * v_ref[...] + (1.0 - ADAM_B2) * (gv * gv)
        m_hat = mn / (1.0 - ADAM_B1 ** ADAM_STEP)
        v_hat = vn / (1.0 - ADAM_B2 ** ADAM_STEP)
        d_ref[...] = -ADAM_LR * (m_hat / (jnp.sqrt(v_hat) + ADAM_EPS) + ADAM_WD * w_ref[...])
        mo_ref[...] = mn
        vo_ref[...] = vn

    blk = pl.BlockSpec((None, tr, Cc), lambda a, r: (a, r, 0))
    return pl.pallas_call(
        body, name=name, grid=(A, R // tr), in_specs=[blk] * 4, out_specs=[blk] * 3,
        out_shape=[SDS(w.shape, F32)] * 3)(w, g, m, v)


BIG = ("gate", "up", "down", "win", "wpa", "wpb", "wout")
SMALL = ("meta", "norm_gains", "w2", "b2", "gn", "qn", "kn", "bm", "fin")


def _view3(a):
    return a.reshape(a.shape[0], -1, a.shape[-1])


class _StepComm:
    def __init__(self, chip, core):
        self.pvec, self.cvec = chip.reshape(1), core.reshape(1)
        self.loc, self.sums = {}, {}
        self.red = {k: None for k in BIG}

    def fwd_rider(self, W, l):
        if l + 1 >= DEPTH:
            return None
        return _GatherRider(BIG, [W[k] for k in BIG], [l + 1] * len(BIG))

    def layer_done(self, l, G):
        dwin = jnp.transpose(_win_from_padded(G["winp"]).reshape(D, NSH, D_IN // NSH), (1, 0, 2))
        self.loc[l] = [dwin if k == "win" else _view3(G[k]) for k in BIG]

    def ffn_rider(self, l):
        return _PairExchangeRider(self.loc[l + 1]) if l + 1 < DEPTH else None

    def ffn_carried(self, l, got):
        self._pair_add(l + 1, got)

    def _pair_add(self, l, got):
        self.sums[l] = [_pair_add(f"pair_add_{k}_l{l}", a, p, self.cvec) for k, a, p in zip(BIG, self.loc.pop(l), got)]

    def bwd_rider(self, l):
        if l + 1 >= DEPTH:
            return None
        return _ChipExchangeRider([s[1] for s in self.sums[l + 1]])

    def bwd_carried(self, l, arrived):
        self._chip_add(l + 1, arrived)

    def _chip_add(self, l, arrived):
        for k, s, q in zip(BIG, self.sums.pop(l), arrived):
            self.red[k] = _chip_add(f"chip_add_{k}_l{l}", s[0], q, self.pvec, self.cvec, l, self.red[k])

    def finish(self):
        self._pair_add(0, _run_rider("pair_exchange_l0", _PairExchangeRider(self.loc[0])))
        self._chip_add(0, _run_rider("chip_exchange_l0", _ChipExchangeRider([s[1] for s in self.sums[0]])))
        return dict(zip(BIG, _pair_share([self.red[k] for k in BIG])))


def kernel(x, meta_tokens, norm_gains, ffn_w_gate, ffn_w_up, ffn_w_down, w_in, gla_w2, gla_b2, gla_gn, q_norm, k_norm, w_pa, w_pb, b_merge, w_out, final_norm, loss_target, m_meta_tokens, m_norm_gains, m_ffn_w_gate, m_ffn_w_up, m_ffn_w_down, m_w_in, m_gla_w2, m_gla_b2, m_gla_gn, m_q_norm, m_k_norm, m_w_pa, m_w_pb, m_b_merge, m_w_out, m_final_norm, v_meta_tokens, v_norm_gains, v_ffn_w_gate, v_ffn_w_up, v_ffn_w_down, v_w_in, v_gla_w2, v_gla_b2, v_gla_gn, v_q_norm, v_k_norm, v_w_pa, v_w_pb, v_b_merge, v_w_out, v_final_norm):
    big_w = dict(gate=ffn_w_gate, up=ffn_w_up, down=ffn_w_down, win=w_in, wpa=w_pa, wpb=w_pb, wout=w_out)
    big_m = dict(gate=m_ffn_w_gate, up=m_ffn_w_up, down=m_ffn_w_down, win=m_w_in, wpa=m_w_pa, wpb=m_w_pb, wout=m_w_out)
    big_v = dict(gate=v_ffn_w_gate, up=v_ffn_w_up, down=v_ffn_w_down, win=v_w_in, wpa=v_w_pa, wpb=v_w_pb, wout=v_w_out)
    small_w = dict(meta=meta_tokens, norm_gains=norm_gains, w2=gla_w2, b2=gla_b2, gn=gla_gn, qn=q_norm, kn=k_norm,
                   bm=b_merge, fin=final_norm)
    small_m = dict(meta=m_meta_tokens, norm_gains=m_norm_gains, w2=m_gla_w2, b2=m_gla_b2, gn=m_gla_gn, qn=m_q_norm,
                   kn=m_k_norm, bm=m_b_merge, fin=m_final_norm)
    small_v = dict(meta=v_meta_tokens, norm_gains=v_norm_gains, w2=v_gla_w2, b2=v_gla_b2, gn=v_gla_gn, qn=v_q_norm,
                   kn=v_k_norm, bm=v_b_merge, fin=v_final_norm)
    xi, yi, ci = _place()
    chip = (2 * xi + yi).astype(jnp.int32)

    comm = _StepComm(chip, ci.astype(jnp.int32))
    shard_pack = _pack({k: small_w[k] for k in SHARDED_SMALL}, SHARDED_SMALL, SMALL_ROWS)
    placed = [_cast_place(f"cast_{k}", _view3(big_w[k]), comm.pvec) for k in BIG]
    placed.append(lax.dynamic_update_slice(jnp.zeros((NSH, SMALL_ROWS, LANE), F32), shard_pack[None], (chip, 0, 0)))
    gathered = _run_rider("gather_l0", _GatherRider(BIG + ("small",), placed, [0] * len(BIG) + [None]))
    W = _assemble(dict(zip(BIG + ("small",), gathered)))
    W.update(gn=gla_gn, qn=q_norm, kn=k_norm, fin=final_norm)

    loss, dh0, _, gs = _local_step(x[0], loss_target[0], W, comm)
    grad_x = dh0[LANE:][None]
    shared = comm.finish()
    grads, deltas, new_m, new_v = {}, {}, {}, {}
    for k in BIG:
        r = shared[k]
        shp = big_w[k].shape
        grads[k] = r.reshape(shp)
        d, mn, vn = _adamw(f"adamw_{k}", _view3(big_w[k]), r, _view3(big_m[k]), _view3(big_v[k]))
        deltas[k], new_m[k], new_v[k] = d.reshape(shp), mn.reshape(shp), vn.reshape(shp)

    gs["loss"] = loss.reshape(1)
    table = dict(FULL_SMALL, loss=(1,))
    tot = _unpack(_allreduce_small(_pack(gs, table, GRAD_ROWS)), table)
    loss_out = tot["loss"][0]
    sl = dict(meta=(1, 256), norm_gains=(2, 256), w2=(3, 64), b2=(2, 64), bm=(2, 256))
    for k in SMALL:
        gk = tot[k]
        if k in sl:
            ax, width = sl[k]
            gk = lax.dynamic_slice_in_dim(gk, chip * width, width, axis=ax)
        grads[k] = gk
    tbl = {k: small_w[k].shape for k in SMALL}
    rows = _rows_for(tbl, 8)
    packs = [_pack(src, tbl, rows)[None] for src in (small_w, grads, small_m, small_v)]
    d, mn, vn = _adamw("adamw_small", *packs)
    for dst, packed in ((deltas, d), (new_m, mn), (new_v, vn)):
        dst.update(_unpack(packed[0], tbl))

    order = ("meta", "norm_gains", "gate", "up", "down", "win", "w2", "b2", "gn", "qn", "kn", "wpa", "wpb", "bm", "wout", "fin")
    return (loss_out, grad_x, *[grads[k] for k in order], *[deltas[k] for k in order], *[new_m[k] for k in order],
            *[new_v[k] for k in order])
```

```python
import functools

import numpy as np
import jax
import jax.numpy as jnp
from jax import lax
from jax.experimental import pallas as pl
from jax.experimental.pallas import tpu as pltpu

F32, BF16 = jnp.float32, jnp.bfloat16
SDS = jax.ShapeDtypeStruct
HIGHEST = lax.Precision.HIGHEST
MESH = pl.DeviceIdType.MESH

D = 1024
DEPTH = 4
N_META = 16
GRID_W = 64
GLA_H, GLA_DK, GLA_DV, GLA_RANK, GLA_TAU, CHUNK = 4, 64, 128, 16, 16.0, 64
ATT_H, ATT_KV, HD = 8, 2, 64
D_FF = 2816
EPS = 1e-6
ROPE_THETA = 10000.0
ADAM_LR, ADAM_B1, ADAM_B2, ADAM_EPS, ADAM_WD, ADAM_STEP = 0.001, 0.9, 0.999, 1e-08, 0.01, 10

NSH = 4
FF_S = D_FF // NSH
FF_P = 768
LANE = 128
PAD = LANE - N_META
D_IN = 4384
C_GA, C_GB, C_QA, C_KA, C_VA, C_RA, C_QB, C_KB, C_VB, C_LR, HP = 0, 1024, 2048, 2304, 2560, 3072, 3584, 4096, 4224, 4352, 4480
VMEM_BIG = 56 * 2 ** 20
ATT_SUB_FWD, ATT_SUB_BWD = 128, 192
RIDER_MID_BACK = 5


def _rt(n, cap, mult=LANE):
    best = None
    t = mult
    while t <= min(n, cap):
        if n % t == 0:
            best = t
        t += mult
    assert best is not None, (n, cap, mult)
    return best


def _cp(big=False):
    return pltpu.CompilerParams(vmem_limit_bytes=VMEM_BIG) if big else None


def _row_ids(i, tm):
    return i * tm + lax.broadcasted_iota(jnp.int32, (tm, 1), 0)


def _mm(name, a, b, *, grid, a_spec, b_spec, o_spec, o_shape, o_dtype, dims, acc_shape, nk=1, scale=None,
        res=None, res_spec=None, a_sl=None, b_sl=None, pad_w=None, into=None, shards=None):
    has_res, has_into = res is not None, into is not None

    def body(*refs):
        a_ref, b_ref = refs[0], refs[1]
        p = 2
        res_ref = None
        if has_res:
            res_ref = refs[p]
            p += 1
        if has_into:
            p += 1
        o_ref = refs[p]
        acc_ref = refs[p + 1] if nk > 1 else None
        av = (a_ref[a_sl] if a_sl is not None else a_ref[...]).astype(BF16)
        bv = (b_ref[b_sl] if b_sl is not None else b_ref[...]).astype(BF16)
        if shards == "rows":
            bv = bv.reshape(bv.shape[0] * bv.shape[1], bv.shape[2])
        if shards == "cols":
            w = bv.shape[2]
            if dims == NN:
                prod = jnp.concatenate([lax.dot_general(av, bv[s], NN, preferred_element_type=F32) for s in range(NSH)], axis=1)
            else:
                prod = sum(lax.dot_general(av[:, s * w:(s + 1) * w], bv[s], NT, preferred_element_type=F32) for s in range(NSH))
        else:
            prod = lax.dot_general(av, bv, dims, preferred_element_type=F32)

        def finish(v):
            if scale is not None:
                v = v * scale
            if has_res:
                v = v + res_ref[...]
            v = v.astype(o_dtype)
            if pad_w is None:
                o_ref[...] = v
            else:
                w = v.shape[-1]
                o_ref[:, :w] = v
                o_ref[:, w:] = jnp.zeros((v.shape[0], pad_w - w), o_dtype)

        if nk == 1:
            finish(prod)
        else:
            k = pl.program_id(len(grid) - 1)

            @pl.when(k == 0)
            def _():
                acc_ref[...] = prod

            @pl.when(k > 0)
            def _():
                acc_ref[...] += prod

            @pl.when(k == nk - 1)
            def _():
                finish(acc_ref[...])

    in_specs = [a_spec, b_spec]
    args = [a, b]
    if has_res:
        in_specs.append(res_spec)
        args.append(res)
    aliases = {}
    if has_into:
        aliases = {len(args): 0}
        in_specs.append(pl.BlockSpec(memory_space=pl.ANY))
        args.append(into)
        o_shape = into.shape
    return pl.pallas_call(
        body, name=name, grid=grid, in_specs=in_specs, out_specs=o_spec, out_shape=SDS(tuple(o_shape), o_dtype),
        scratch_shapes=[pltpu.VMEM(acc_shape, F32)] if nk > 1 else [], input_output_aliases=aliases,
        compiler_params=_cp(True))(*args)


NN = (((1,), (0,)), ((), ()))
NT = (((1,), (1,)), ((), ()))
TN = (((0,), (0,)), ((), ()))


W_CHUNK = 896


def _norm_project(name, h, g, winp):
    Lp = h.shape[0]
    tm = _rt(Lp, 384)

    def body(h_ref, g_ref, w_ref, o_ref, zt_ref):
        x = h_ref[...]
        y = x * lax.rsqrt(jnp.mean(x * x, axis=-1, keepdims=True) + EPS) * g_ref[...]
        zt_ref[...] = y.T.astype(BF16)
        z = y.astype(BF16)
        for n in range(HP // W_CHUNK):
            cs = slice(n * W_CHUNK, (n + 1) * W_CHUNK)
            o_ref[:, cs] = jnp.dot(z, w_ref[:, cs], preferred_element_type=F32)

    return pl.pallas_call(
        body, name=name, grid=(Lp // tm,),
        in_specs=[pl.BlockSpec((tm, D), lambda i: (i, 0)), pl.BlockSpec((1, D), lambda i: (0, 0)),
                  pl.BlockSpec((D, HP), lambda i: (0, 0), pipeline_mode=pl.Buffered(1))],
        out_specs=[pl.BlockSpec((tm, HP), lambda i: (i, 0)), pl.BlockSpec((D, tm), lambda i: (0, i))],
        out_shape=[SDS((Lp, HP), F32), SDS((D, Lp), BF16)], compiler_params=_cp(True))(h, g, winp)


def _norm_project_bwd(name, dH, winp, h, g, dh):
    Lp = h.shape[0]
    tm = _rt(Lp, 384)

    def body(dhh_ref, w_ref, h_ref, g_ref, dh_ref, o_ref, dg_ref):
        dn = lax.dot_general(dhh_ref[...], w_ref[...], NT, preferred_element_type=F32)
        x = h_ref[...]
        r = lax.rsqrt(jnp.mean(x * x, axis=-1, keepdims=True) + EPS)
        xh = x * r
        u = dn * g_ref[...]
        o_ref[...] = dh_ref[...] + r * (u - xh * jnp.mean(u * xh, axis=-1, keepdims=True))

        @pl.when(pl.program_id(0) == 0)
        def _():
            dg_ref[...] = jnp.zeros_like(dg_ref)

        dg_ref[...] += jnp.sum(dn * xh, axis=0, keepdims=True)

    row = pl.BlockSpec((tm, D), lambda i: (i, 0))
    vec = pl.BlockSpec((1, D), lambda i: (0, 0))
    return pl.pallas_call(
        body, name=name, grid=(Lp // tm,),
        in_specs=[pl.BlockSpec((tm, HP), lambda i: (i, 0)), pl.BlockSpec((D, HP), lambda i: (0, 0), pipeline_mode=pl.Buffered(1)),
                  row, vec, row],
        out_specs=[row, vec], out_shape=[SDS((Lp, D), F32), SDS((1, D), F32)], compiler_params=_cp(True))(dH, winp, h, g, dh)


def _loss_head(name, h, g, tgt):
    Lp = h.shape[0]
    tm = LANE

    def body(h_ref, g_ref, t_ref, dh_ref, dg_ref, loss_ref):
        i = pl.program_id(0)
        x = h_ref[...]
        r = lax.rsqrt(jnp.mean(x * x, axis=-1, keepdims=True) + EPS)
        xh = x * r
        gg = g_ref[...]
        err = jnp.where(i >= 1, xh * gg - t_ref[...], 0.0)
        dy = err * (1.0 / D)
        u = dy * gg
        dh_ref[...] = r * (u - xh * jnp.mean(u * xh, axis=-1, keepdims=True))

        @pl.when(i == 0)
        def _():
            dg_ref[...] = jnp.zeros_like(dg_ref)
            loss_ref[...] = jnp.zeros_like(loss_ref)

        dg_ref[...] += jnp.sum(dy * xh, axis=0, keepdims=True)
        loss_ref[...] += (0.5 / D) * jnp.sum(err * err)

    row = pl.BlockSpec((tm, D), lambda i: (i, 0))
    vec = pl.BlockSpec((1, D), lambda i: (0, 0))
    return pl.pallas_call(
        body, name=name, grid=(Lp // tm,),
        in_specs=[row, vec, pl.BlockSpec((tm, D), lambda i: (jnp.maximum(i - 1, 0), 0))],
        out_specs=[row, vec, pl.BlockSpec((8, LANE), lambda i: (0, 0))],
        out_shape=[SDS((Lp, D), F32), SDS((1, D), F32), SDS((8, LANE), F32)])(h, g, tgt)


def _merge_fwd(name, H, pa, pb, bm):
    Lp = H.shape[0]
    tm = _rt(Lp, 384)

    def body(g_ref, pa_ref, pb_ref, bm_ref, y_ref, yt_ref):
        gv = g_ref[...]
        y = (jax.nn.sigmoid(gv[:, :D] + bm_ref[0:1, :]) * pa_ref[...]
             + jax.nn.sigmoid(gv[:, D:] + bm_ref[1:2, :]) * pb_ref[...])
        y_ref[...] = y.astype(BF16)
        yt_ref[...] = y.T.astype(BF16)

    row = pl.BlockSpec((tm, D), lambda i: (i, 0))
    return pl.pallas_call(
        body, name=name, grid=(Lp // tm,),
        in_specs=[pl.BlockSpec((tm, 2 * D), lambda i: (i, 0)), row, row, pl.BlockSpec((2, D), lambda i: (0, 0))],
        out_specs=[row, pl.BlockSpec((D, tm), lambda i: (0, i))],
        out_shape=[SDS((Lp, D), BF16), SDS((D, Lp), BF16)])(H, pa, pb, bm)


def _merge_bwd(name, H, pa, pb, bm, dy):
    Lp = H.shape[0]
    tm = _rt(Lp, 384)

    def body(g_ref, pa_ref, pb_ref, bm_ref, dy_ref, dpa_ref, dpb_ref, dh_ref, dbm_ref):
        gv = g_ref[...]
        dyv = dy_ref[...]
        sa = jax.nn.sigmoid(gv[:, :D] + bm_ref[0:1, :])
        sb = jax.nn.sigmoid(gv[:, D:] + bm_ref[1:2, :])
        dpa_ref[...] = (dyv * sa).astype(BF16)
        dpb_ref[...] = (dyv * sb).astype(BF16)
        dga = dyv * pa_ref[...] * (sa * (1.0 - sa))
        dgb = dyv * pb_ref[...] * (sb * (1.0 - sb))
        dh_ref[:, :D] = dga.astype(BF16)
        dh_ref[:, D:] = dgb.astype(BF16)

        @pl.when(pl.program_id(0) == 0)
        def _():
            dbm_ref[...] = jnp.zeros_like(dbm_ref)

        dbm_ref[0:1, :] += jnp.sum(dga, axis=0, keepdims=True)
        dbm_ref[1:2, :] += jnp.sum(dgb, axis=0, keepdims=True)

    row = pl.BlockSpec((tm, D), lambda i: (i, 0))
    two = pl.BlockSpec((2, D), lambda i: (0, 0))
    gate = pl.BlockSpec((tm, 2 * D), lambda i: (i, 0))
    return pl.pallas_call(
        body, name=name, grid=(Lp // tm,), in_specs=[gate, row, row, two, row], out_specs=[row, row, gate, two],
        out_shape=[SDS((Lp, D), BF16), SDS((Lp, D), BF16), SDS((Lp, HP), BF16), SDS((2, D), F32)])(H, pa, pb, bm, dy)


def _gla_prep(name, H, w2p, b2p):
    Lp = H.shape[0]
    tm = _rt(Lp, 384)

    def body(qk_ref, v_ref, lr_ref, w_ref, b_ref, q_o, k_o, v_o, gf_o, gb_o):
        valid = _row_ids(pl.program_id(0), tm) >= PAD
        qk = qk_ref[...]
        vv = v_ref[...]
        pre = jnp.dot(lr_ref[...].astype(BF16), w_ref[...], preferred_element_type=F32) + b_ref[...]
        g = jnp.where(valid, jax.nn.log_sigmoid(pre) * (1.0 / GLA_TAU), 0.0)
        for hh in range(GLA_H):
            q_o[hh] = qk[:, 64 * hh:64 * hh + 64] * (GLA_DK ** -0.5)
            k_o[hh] = qk[:, 256 + 64 * hh:256 + 64 * hh + 64]
            v_o[hh] = vv[:, 128 * hh:128 * hh + 128].astype(BF16)
            gf_o[hh] = g[:, 64 * hh:64 * hh + 64]
            gb_o[hh] = g[:, 256 + 64 * hh:256 + 64 * hh + 64]

    h64 = pl.BlockSpec((GLA_H, tm, 64), lambda i: (0, i, 0))
    h128 = pl.BlockSpec((GLA_H, tm, 128), lambda i: (0, i, 0))
    return pl.pallas_call(
        body, name=name, grid=(Lp // tm,),
        in_specs=[pl.BlockSpec((tm, 512), lambda i: (i, C_QA // 512)), pl.BlockSpec((tm, 512), lambda i: (i, C_VA // 512)),
                  pl.BlockSpec((tm, LANE), lambda i: (i, C_LR // LANE)), pl.BlockSpec((LANE, 512), lambda i: (0, 0)),
                  pl.BlockSpec((1, 512), lambda i: (0, 0))],
        out_specs=[h64, h64, h128, h64, h64],
        out_shape=[SDS((GLA_H, Lp, 64), F32), SDS((GLA_H, Lp, 64), F32), SDS((GLA_H, Lp, 128), BF16),
                   SDS((GLA_H, Lp, 64), F32), SDS((GLA_H, Lp, 64), F32)])(H, H, H, w2p, b2p)


def _bdot(a, b, ca, cb, precision=None):
    return lax.dot_general(a, b, ((ca, cb), ((0,), (0,))), precision=precision, preferred_element_type=F32)


def _gla_chunk_terms(q_ref, k_ref, g_ref, v_ref, G, rev):
    B = GLA_H * G
    qv = q_ref[...].reshape(B, CHUNK, GLA_DK)
    kv = k_ref[...].reshape(B, CHUNK, GLA_DK)
    gv = g_ref[...].reshape(B, CHUNK, GLA_DK)
    vv = v_ref[...].reshape(B, CHUNK, GLA_DV)
    ii = lax.broadcasted_iota(jnp.int32, (CHUNK, CHUNK), 0)
    jj = lax.broadcasted_iota(jnp.int32, (CHUNK, CHUNK), 1)
    tri = (jj >= ii) if rev else (jj <= ii)
    tb = jnp.broadcast_to(tri.astype(F32)[None], (B, CHUNK, CHUNK))
    bc = _bdot(tb, gv, (2,), (1,), HIGHEST)
    bt = bc[:, 0:1, :] if rev else bc[:, CHUNK - 1:CHUNK, :]
    eq, eki, eke = jnp.exp(bc), jnp.exp(-bc), jnp.exp(bt - bc)
    qd, ki, ke = qv * eq, kv * eki, kv * eke
    att = jnp.where(tri[None], _bdot(qd.astype(BF16), ki.astype(BF16), (2,), (2,)), 0.0)
    dm = jnp.exp(_bdot(gv, jnp.ones((B, CHUNK, GLA_DV), F32), (1,), (1,), HIGHEST))
    return dict(B=B, vv=vv, tri=tri, tb=tb, bt=bt, eq=eq, eki=eki, eke=eke, qd=qd, ki=ki, ke=ke, att=att, dm=dm)


def _gla_fwd(name, q, k, g, v, rev, G):
    Lp = q.shape[1]
    tg = G * CHUNK
    ng = Lp // tg

    def body(q_ref, k_ref, g_ref, v_ref, o_ref, ss_ref, s_scr):
        @pl.when(pl.program_id(0) == 0)
        def _():
            s_scr[...] = jnp.zeros_like(s_scr)

        t = _gla_chunk_terms(q_ref, k_ref, g_ref, v_ref, G, rev)
        B, vv = t["B"], t["vv"]
        qd = t["qd"].astype(BF16)
        oi = _bdot(t["att"].astype(BF16), vv, (2,), (1,))
        kvc = _bdot(t["ke"].astype(BF16), vv, (1,), (1,)).reshape(GLA_H, G, GLA_DK, GLA_DV)
        dm = t["dm"].reshape(GLA_H, G, GLA_DK, GLA_DV)
        s = s_scr[...]
        sp = [None] * G
        for c in (range(G - 1, -1, -1) if rev else range(G)):
            sp[c] = s
            ss_ref[c] = s
            s = dm[:, c] * s + kvc[:, c]
        s_scr[...] = s
        spb = jnp.stack(sp, axis=1).reshape(B, GLA_DK, GLA_DV).astype(BF16)
        o_ref[...] = (oi + _bdot(qd, spb, (2,), (1,))).reshape(GLA_H, tg, GLA_DV)

    blk = (lambda i: (0, ng - 1 - i, 0)) if rev else (lambda i: (0, i, 0))
    sblk = (lambda i: (ng - 1 - i, 0, 0, 0)) if rev else (lambda i: (i, 0, 0, 0))
    h64 = pl.BlockSpec((GLA_H, tg, 64), blk)
    h128 = pl.BlockSpec((GLA_H, tg, 128), blk)
    return pl.pallas_call(
        body, name=name, grid=(ng,), in_specs=[h64, h64, h64, h128],
        out_specs=[h128, pl.BlockSpec((G, GLA_H, GLA_DK, GLA_DV), sblk)],
        out_shape=[SDS((GLA_H, Lp, GLA_DV), F32), SDS((Lp // CHUNK, GLA_H, GLA_DK, GLA_DV), F32)],
        scratch_shapes=[pltpu.VMEM((GLA_H, GLA_DK, GLA_DV), F32)], compiler_params=_cp(True))(q, k, g, v)


def _gla_bwd(name, q, k, g, v, ss, do, rev, G):
    Lp = q.shape[1]
    tg = G * CHUNK
    ng = Lp // tg

    def body(q_ref, k_ref, g_ref, v_ref, ss_ref, do_ref, dq_ref, dk_ref, dg_ref, dv_ref, ds_scr):
        @pl.when(pl.program_id(0) == 0)
        def _():
            ds_scr[...] = jnp.zeros_like(ds_scr)

        t = _gla_chunk_terms(q_ref, k_ref, g_ref, v_ref, G, rev)
        B, vv, tri = t["B"], t["vv"], t["tri"]
        qd, ki, ke = t["qd"], t["ki"], t["ke"]
        qdb, kib, keb = qd.astype(BF16), ki.astype(BF16), ke.astype(BF16)
        sp = jnp.stack([ss_ref[c] for c in range(G)], axis=1).reshape(B, GLA_DK, GLA_DV)
        dob = do_ref[...].reshape(B, CHUNK, GLA_DV).astype(BF16)
        da = jnp.where(tri[None], _bdot(dob, vv, (2,), (2,)), 0.0).astype(BF16)
        dqd = _bdot(da, kib, (2,), (1,)) + _bdot(dob, sp.astype(BF16), (2,), (2,))
        dki = _bdot(da, qdb, (1,), (1,))
        dv = _bdot(t["att"].astype(BF16), dob, (1,), (1,))
        cc = _bdot(qdb, dob, (1,), (1,)).reshape(GLA_H, G, GLA_DK, GLA_DV)
        dm = t["dm"].reshape(GLA_H, G, GLA_DK, GLA_DV)
        dsc = ds_scr[...]
        dsn = [None] * G
        for c in (range(G) if rev else range(G - 1, -1, -1)):
            dsn[c] = dsc
            dsc = dm[:, c] * dsc + cc[:, c]
        ds_scr[...] = dsc
        dsn = jnp.stack(dsn, axis=1).reshape(B, GLA_DK, GLA_DV)
        dsnb = dsn.astype(BF16)
        dv = dv + _bdot(keb, dsnb, (2,), (1,))
        dke = _bdot(vv, dsnb, (2,), (2,))
        ddrow = _bdot(jnp.ones((B, CHUNK, GLA_DV), F32), dsn * sp, (2,), (2,), HIGHEST)
        dbt = ddrow * jnp.exp(t["bt"]) + jnp.sum(dke * ke, axis=1, keepdims=True)
        db = dqd * qd - dki * ki - dke * ke
        dq_ref[...] = (dqd * t["eq"]).reshape(GLA_H, tg, GLA_DK)
        dk_ref[...] = (dki * t["eki"] + dke * t["eke"]).reshape(GLA_H, tg, GLA_DK)
        dg_ref[...] = (_bdot(t["tb"], db, (1,), (1,), HIGHEST) + dbt).reshape(GLA_H, tg, GLA_DK)
        dv_ref[...] = dv.reshape(GLA_H, tg, GLA_DV)

    blk = (lambda i: (0, i, 0)) if rev else (lambda i: (0, ng - 1 - i, 0))
    sblk = (lambda i: (i, 0, 0, 0)) if rev else (lambda i: (ng - 1 - i, 0, 0, 0))
    h64 = pl.BlockSpec((GLA_H, tg, 64), blk)
    h128 = pl.BlockSpec((GLA_H, tg, 128), blk)
    return pl.pallas_call(
        body, name=name, grid=(ng,),
        in_specs=[h64, h64, h64, h128, pl.BlockSpec((G, GLA_H, GLA_DK, GLA_DV), sblk), h128],
        out_specs=[h64, h64, h64, h128],
        out_shape=[SDS((GLA_H, Lp, 64), F32), SDS((GLA_H, Lp, 64), F32), SDS((GLA_H, Lp, 64), F32),
                   SDS((GLA_H, Lp, GLA_DV), F32)],
        scratch_shapes=[pltpu.VMEM((GLA_H, GLA_DK, GLA_DV), F32)], compiler_params=_cp(True))(q, k, g, v, ss, do)


def _gla_post(name, of, ob, H, gn):
    Lp = H.shape[0]
    tm = _rt(Lp, 384)

    def body(of_ref, ob_ref, r_ref, gn_ref, a_ref, at_ref):
        parts = []
        for hh in range(GLA_H):
            o = of_ref[hh] + ob_ref[hh]
            parts.append(o * lax.rsqrt(jnp.mean(o * o, axis=-1, keepdims=True) + EPS))
        rv = r_ref[...]
        a = (jnp.concatenate(parts, axis=1) * gn_ref[...]) * (rv * jax.nn.sigmoid(rv))
        a_ref[...] = a.astype(BF16)
        at_ref[...] = a.T.astype(BF16)

    h128 = pl.BlockSpec((GLA_H, tm, 128), lambda i: (0, i, 0))
    return pl.pallas_call(
        body, name=name, grid=(Lp // tm,),
        in_specs=[h128, h128, pl.BlockSpec((tm, 512), lambda i: (i, C_RA // 512)), pl.BlockSpec((1, 512), lambda i: (0, 0))],
        out_specs=[pl.BlockSpec((tm, 512), lambda i: (i, 0)), pl.BlockSpec((512, tm), lambda i: (0, i))],
        out_shape=[SDS((Lp, 512), BF16), SDS((512, Lp), BF16)])(of, ob, H, gn)


def _gla_post_bwd(name, of, ob, H, gn, da, dH):
    Lp = H.shape[0]
    tm = _rt(Lp, 384)

    def body(of_ref, ob_ref, r_ref, gn_ref, da_ref, dh_in, do_ref, dh_ref, dgn_ref):
        rv = r_ref[...]
        sg = jax.nn.sigmoid(rv)
        dav = da_ref[...]
        gnv = gn_ref[...]
        ons, rs = [], []
        for hh in range(GLA_H):
            o = of_ref[hh] + ob_ref[hh]
            r = lax.rsqrt(jnp.mean(o * o, axis=-1, keepdims=True) + EPS)
            rs.append(r)
            ons.append(o * r)
        on = jnp.concatenate(ons, axis=1)
        dw = dav * (rv * sg)
        dh_ref[...] = (dav * (on * gnv) * (sg * (1.0 + rv * (1.0 - sg)))).astype(BF16)

        @pl.when(pl.program_id(0) == 0)
        def _():
            dgn_ref[...] = jnp.zeros_like(dgn_ref)

        dgn_ref[...] += jnp.sum(dw * on, axis=0, keepdims=True)
        don = dw * gnv
        for hh in range(GLA_H):
            dd = don[:, 128 * hh:128 * hh + 128]
            do_ref[hh] = rs[hh] * (dd - ons[hh] * jnp.mean(dd * ons[hh], axis=-1, keepdims=True))

    h128 = pl.BlockSpec((GLA_H, tm, 128), lambda i: (0, i, 0))
    rblk = pl.BlockSpec((tm, 512), lambda i: (i, C_RA // 512))
    vec = pl.BlockSpec((1, 512), lambda i: (0, 0))
    return pl.pallas_call(
        body, name=name, grid=(Lp // tm,),
        in_specs=[h128, h128, rblk, vec, pl.BlockSpec((tm, 512), lambda i: (i, 0)), pl.BlockSpec(memory_space=pl.ANY)],
        out_specs=[h128, rblk, vec],
        out_shape=[SDS((GLA_H, Lp, 128), F32), SDS(dH.shape, BF16), SDS((1, 512), F32)],
        input_output_aliases={5: 1})(of, ob, H, gn, da, dH)


def _gla_qkv_bwd(name, dqf, dqb, dkf, dkb, dvf, dvb, dH):
    Lp = dqf.shape[1]
    tm = _rt(Lp, 384)

    def body(dqf_ref, dqb_ref, dkf_ref, dkb_ref, dvf_ref, dvb_ref, dh_in, dh_ref):
        valid = _row_ids(pl.program_id(0), tm) >= PAD
        for hh in range(GLA_H):
            dq = (dqf_ref[hh] + dqb_ref[hh]) * (GLA_DK ** -0.5)
            dh_ref[:, 64 * hh:64 * hh + 64] = jnp.where(valid, dq, 0.0).astype(BF16)
            dh_ref[:, 256 + 64 * hh:256 + 64 * hh + 64] = jnp.where(valid, dkf_ref[hh] + dkb_ref[hh], 0.0).astype(BF16)
            dh_ref[:, 512 + 128 * hh:512 + 128 * hh + 128] = jnp.where(valid, dvf_ref[hh] + dvb_ref[hh], 0.0).astype(BF16)

    h64 = pl.BlockSpec((GLA_H, tm, 64), lambda i: (0, i, 0))
    h128 = pl.BlockSpec((GLA_H, tm, 128), lambda i: (0, i, 0))
    return pl.pallas_call(
        body, name=name, grid=(Lp // tm,),
        in_specs=[h64, h64, h64, h64, h128, h128, pl.BlockSpec(memory_space=pl.ANY)],
        out_specs=pl.BlockSpec((tm, 1024), lambda i: (i, C_QA // 1024)), out_shape=SDS(dH.shape, BF16),
        input_output_aliases={6: 0})(dqf, dqb, dkf, dkb, dvf, dvb, dH)


def _gla_gate_bwd(name, H, w2p, b2p, dgf, dgb, dH):
    Lp = H.shape[0]
    tm = _rt(Lp, 384)

    def body(lr_ref, w_ref, b_ref, dgf_ref, dgb_ref, dh_in, dh_ref, dw_ref, db_ref, dg_scr):
        valid = _row_ids(pl.program_id(0), tm) >= PAD
        for hh in range(GLA_H):
            dg_scr[:, 64 * hh:64 * hh + 64] = dgf_ref[hh]
            dg_scr[:, 256 + 64 * hh:256 + 64 * hh + 64] = dgb_ref[hh]
        lrb = lr_ref[...].astype(BF16)
        wv = w_ref[...]
        pre = jnp.dot(lrb, wv, preferred_element_type=F32) + b_ref[...]
        dpre = jnp.where(valid, dg_scr[...] * (1.0 / GLA_TAU) * jax.nn.sigmoid(-pre), 0.0)
        dpb = dpre.astype(BF16)
        dh_ref[...] = lax.dot_general(dpb, wv, NT, preferred_element_type=F32).astype(BF16)

        @pl.when(pl.program_id(0) == 0)
        def _():
            dw_ref[...] = jnp.zeros_like(dw_ref)
            db_ref[...] = jnp.zeros_like(db_ref)

        dw_ref[...] += lax.dot_general(lrb, dpb, TN, preferred_element_type=F32)
        db_ref[...] += jnp.sum(dpre, axis=0, keepdims=True)

    h64 = pl.BlockSpec((GLA_H, tm, 64), lambda i: (0, i, 0))
    lrblk = pl.BlockSpec((tm, LANE), lambda i: (i, C_LR // LANE))
    wblk = pl.BlockSpec((LANE, 512), lambda i: (0, 0))
    vec = pl.BlockSpec((1, 512), lambda i: (0, 0))
    return pl.pallas_call(
        body, name=name, grid=(Lp // tm,),
        in_specs=[lrblk, wblk, vec, h64, h64, pl.BlockSpec(memory_space=pl.ANY)],
        out_specs=[lrblk, wblk, vec],
        out_shape=[SDS(dH.shape, BF16), SDS((LANE, 512), F32), SDS((1, 512), F32)],
        scratch_shapes=[pltpu.VMEM((tm, 512), F32)], input_output_aliases={5: 0})(H, w2p, b2p, dgf, dgb, dH)


def _swap16(x):
    n = x.shape[1]
    lane = lax.broadcasted_iota(jnp.int32, x.shape, 1)
    return jnp.where(lane % 32 < 16, pltpu.roll(x, n - 16, 1), pltpu.roll(x, 16, 1))


def _head_mean(v, bd):
    hi = v.astype(BF16)
    lo = (v - hi.astype(F32)).astype(BF16)
    bdb = bd.astype(BF16)
    return jnp.dot(hi, bdb, preferred_element_type=F32) + jnp.dot(lo, bdb, preferred_element_type=F32)


def _headnorm_rope(x, gain, cos, sin, bd):
    r = lax.rsqrt(_head_mean(x * x, bd) + EPS)
    xh = x * r
    xn = xh * gain
    return xn * cos + _swap16(xn) * sin, xh, r


def _headnorm_rope_bwd(dxr, xh, r, gain, cos, sin, bd):
    dxn = cos * dxr + _swap16(sin * dxr)
    u = dxn * gain
    dx = r * (u - xh * _head_mean(u * xh, bd))
    return dx, jnp.sum(dxn * xh, axis=0, keepdims=True)


def _att_prep(name, H, gq, gk, cos, sin, bd):
    Lp = H.shape[0]
    tm = _rt(Lp, 384)

    def body(q_ref, kv_ref, gq_ref, gk_ref, c_ref, s_ref, bd_ref, qp_ref, k_ref, v_ref):
        c1, s1 = c_ref[...], s_ref[...]
        c4, s4 = jnp.concatenate([c1] * 4, axis=1), jnp.concatenate([s1] * 4, axis=1)
        xr, _, _ = _headnorm_rope(q_ref[...], gq_ref[...], c4, s4, bd_ref[...])
        xr = xr * (HD ** -0.5)
        lane = lax.broadcasted_iota(jnp.int32, (tm, LANE), 1)
        for hh in range(ATT_H):
            grp = xr[:, LANE * (hh // 2):LANE * (hh // 2) + LANE]
            e, gi = hh % 2, hh // 4
            if e != gi:
                grp = pltpu.roll(grp, 64, 1)
            keep = (lane < 64) if gi == 0 else (lane >= 64)
            qp_ref[hh] = jnp.where(keep, grp, 0.0).astype(BF16)
        kv = kv_ref[...]
        kr, _, _ = _headnorm_rope(kv[:, :LANE], gk_ref[...], c1, s1, bd_ref[0:LANE, 0:LANE])
        k_ref[...] = kr.astype(BF16)
        v_ref[...] = kv[:, LANE:].astype(BF16)

    row128 = pl.BlockSpec((tm, LANE), lambda i: (i, 0))
    return pl.pallas_call(
        body, name=name, grid=(Lp // tm,),
        in_specs=[pl.BlockSpec((tm, 512), lambda i: (i, C_QB // 512)), pl.BlockSpec((tm, 256), lambda i: (i, C_KB // 256)),
                  pl.BlockSpec((1, 512), lambda i: (0, 0)), pl.BlockSpec((1, LANE), lambda i: (0, 0)), row128, row128,
                  pl.BlockSpec((512, 512), lambda i: (0, 0))],
        out_specs=[pl.BlockSpec((ATT_H, tm, LANE), lambda i: (0, i, 0)), row128, row128],
        out_shape=[SDS((ATT_H, Lp, LANE), BF16), SDS((Lp, LANE), BF16), SDS((Lp, LANE), BF16)])(H, H, gq, gk, cos, sin, bd)


def _att_q_bwd(name, H, gq, cos, sin, bd, dqp, dH):
    Lp = H.shape[0]
    tm = _rt(Lp, 384)

    def body(q_ref, gq_ref, c_ref, s_ref, bd_ref, dqp_ref, dh_in, dh_ref, dg_ref):
        c1, s1 = c_ref[...], s_ref[...]
        c4, s4 = jnp.concatenate([c1] * 4, axis=1), jnp.concatenate([s1] * 4, axis=1)
        gqv = gq_ref[...]
        _, xh, r = _headnorm_rope(q_ref[...], gqv, c4, s4, bd_ref[...])
        lane = lax.broadcasted_iota(jnp.int32, (tm, LANE), 1)
        groups = []
        for j in range(ATT_H // 2):
            pieces = []
            for e in range(2):
                hh = 2 * j + e
                piece = dqp_ref[hh]
                if e != hh // 4:
                    piece = pltpu.roll(piece, 64, 1)
                pieces.append(piece)
            groups.append(jnp.where(lane < 64, pieces[0], pieces[1]))
        dxr = jnp.concatenate(groups, axis=1) * (HD ** -0.5)
        dx, dg = _headnorm_rope_bwd(dxr, xh, r, gqv, c4, s4, bd_ref[...])
        dh_ref[...] = dx.astype(BF16)

        @pl.when(pl.program_id(0) == 0)
        def _():
            dg_ref[...] = jnp.zeros_like(dg_ref)

        dg_ref[...] += dg

    row128 = pl.BlockSpec((tm, LANE), lambda i: (i, 0))
    qblk = pl.BlockSpec((tm, 512), lambda i: (i, C_QB // 512))
    vec = pl.BlockSpec((1, 512), lambda i: (0, 0))
    return pl.pallas_call(
        body, name=name, grid=(Lp // tm,),
        in_specs=[qblk, vec, row128, row128, pl.BlockSpec((512, 512), lambda i: (0, 0)),
                  pl.BlockSpec((ATT_H, tm, LANE), lambda i: (0, i, 0)), pl.BlockSpec(memory_space=pl.ANY)],
        out_specs=[qblk, vec], out_shape=[SDS(dH.shape, BF16), SDS((1, 512), F32)],
        input_output_aliases={6: 0})(H, gq, cos, sin, bd, dqp, dH)


def _att_kv_bwd(name, H, gk, cos, sin, bd, dkr, dvb, dH):
    Lp = H.shape[0]
    tm = _rt(Lp, 384)

    def body(kv_ref, gk_ref, c_ref, s_ref, bd_ref, dk_ref, dv_ref, dh_in, dh_ref, dg_ref):
        c1, s1 = c_ref[...], s_ref[...]
        gkv = gk_ref[...]
        bdv = bd_ref[0:LANE, 0:LANE]
        _, xh, r = _headnorm_rope(kv_ref[:, :LANE], gkv, c1, s1, bdv)
        dx, dg = _headnorm_rope_bwd(dk_ref[...], xh, r, gkv, c1, s1, bdv)
        dh_ref[:, :LANE] = dx.astype(BF16)
        dh_ref[:, LANE:] = dv_ref[...].astype(BF16)

        @pl.when(pl.program_id(0) == 0)
        def _():
            dg_ref[...] = jnp.zeros_like(dg_ref)

        dg_ref[...] += dg

    row128 = pl.BlockSpec((tm, LANE), lambda i: (i, 0))
    kvblk = pl.BlockSpec((tm, 256), lambda i: (i, C_KB // 256))
    vec = pl.BlockSpec((1, LANE), lambda i: (0, 0))
    return pl.pallas_call(
        body, name=name, grid=(Lp // tm,),
        in_specs=[kvblk, vec, row128, row128, pl.BlockSpec((512, 512), lambda i: (0, 0)), row128, row128,
                  pl.BlockSpec(memory_space=pl.ANY)],
        out_specs=[kvblk, vec], out_shape=[SDS(dH.shape, BF16), SDS((1, LANE), F32)],
        input_output_aliases={7: 0})(H, gk, cos, sin, bd, dkr, dvb, dH)


def _ridden_call(core, rider, *, name, grid, in_specs, out_specs, out_shape, scratch_shapes, args, aliases=None):
    n_in, n_out, n_scr = len(in_specs), len(out_specs), len(scratch_shapes)
    r_in = len(rider.ins) if rider else 0
    r_out = len(rider.out_shape) if rider else 0
    total = int(np.prod(grid))
    mid_step = max(total - 1 - RIDER_MID_BACK, 0)

    def body(*refs):
        ins, r_ins = refs[:n_in], refs[n_in:n_in + r_in]
        o0 = n_in + r_in
        outs, r_outs = refs[o0:o0 + n_out], refs[o0 + n_out:o0 + n_out + r_out]
        s0 = o0 + n_out + r_out
        scr, r_sems = refs[s0:s0 + n_scr], refs[s0 + n_scr:]
        if rider is None:
            core(ins, outs, scr)
            return
        step = pl.program_id(0)
        for ax in range(1, len(grid)):
            step = step * grid[ax] + pl.program_id(ax)
        start, mid, end = rider.hooks(r_ins, r_outs, r_sems)
        pl.when(step == 0)(start)
        core(ins, outs, scr)
        pl.when(step == mid_step)(mid)
        pl.when(step == total - 1)(end)

    any_spec = pl.BlockSpec(memory_space=pl.ANY)
    res = pl.pallas_call(
        body, name=name, grid=grid, in_specs=list(in_specs) + [any_spec] * r_in, out_specs=list(out_specs) + [any_spec] * r_out,
        out_shape=list(out_shape) + (list(rider.out_shape) if rider else []),
        scratch_shapes=list(scratch_shapes) + (list(rider.sems) if rider else []),
        input_output_aliases={**(aliases or {}), **({n_in + a: n_out + b for a, b in rider.aliases.items()} if rider else {})},
        compiler_params=_cp(True))(*args, *(rider.ins if rider else []))
    return res[:n_out], res[n_out:]


def _att_fwd(name, qp, kr, vb, bias, rider=None):
    Lp = kr.shape[0]
    tq = _rt(Lp, 384)
    sub = ATT_SUB_FWD if tq % ATT_SUB_FWD == 0 else tq

    def core(ins, outs, scr):
        q_ref, k_ref, v_ref, bias_ref = ins
        b_ref, bt_ref, lse_ref = outs
        j, i = pl.program_id(0), pl.program_id(1)
        valid = _row_ids(i, tq) >= PAD
        kk, vv, bb = k_ref[...], v_ref[...], bias_ref[...]
        chains = [(e, r) for e in range(2) for r in range(tq // sub)]
        ss = [lax.dot_general(q_ref[e, r * sub:(r + 1) * sub, :], kk, NT, preferred_element_type=F32) for e, r in chains]
        ps, ls = [], []
        for (e, r), s in zip(chains, ss):
            s = jnp.concatenate([s[:, :LANE] + bb[:, :LANE], s[:, LANE:]], axis=1)
            m = jnp.max(s, axis=-1, keepdims=True)
            p = jnp.exp(s - m)
            l = jnp.sum(p, axis=-1, keepdims=True)
            ps.append(p.astype(BF16))
            ls.append(l)
            lse_ref[e, r * sub:(r + 1) * sub, :] = m + jnp.log(l)
        os_ = [jnp.dot(p, vv, preferred_element_type=F32) * (1.0 / l) for p, l in zip(ps, ls)]
        n_sub = tq // sub
        outs = [jnp.where(valid, jnp.concatenate(os_[e * n_sub:(e + 1) * n_sub], axis=0), 0.0) for e in range(2)]
        lane = lax.broadcasted_iota(jnp.int32, (tq, LANE), 1)
        low = j < 2
        o0 = jnp.where(low, outs[0], pltpu.roll(outs[0], 64, 1))
        o1 = jnp.where(low, pltpu.roll(outs[1], 64, 1), outs[1])
        blk = jnp.where(lane < 64, o0, o1)
        b_ref[...] = blk.astype(BF16)
        bt_ref[...] = blk.T.astype(BF16)

    full = pl.BlockSpec((Lp, LANE), lambda j, i: (0, 0))
    return _ridden_call(
        core, rider, name=name, grid=(ATT_H // 2, Lp // tq),
        in_specs=[pl.BlockSpec((2, tq, LANE), lambda j, i: (j, i, 0)), full, full, pl.BlockSpec((1, Lp), lambda j, i: (0, 0))],
        out_specs=[pl.BlockSpec((tq, LANE), lambda j, i: (i, j)), pl.BlockSpec((LANE, tq), lambda j, i: (j, i)),
                   pl.BlockSpec((2, tq, 1), lambda j, i: (j, i, 0))],
        out_shape=[SDS((Lp, 512), BF16), SDS((512, Lp), BF16), SDS((ATT_H, Lp, 1), F32)],
        scratch_shapes=[], args=(qp, kr, vb, bias))


def _att_bwd(name, qp, kr, vb, bias, lse, db, b, rider=None):
    Lp = kr.shape[0]
    tq = _rt(Lp, 384)
    sub = ATT_SUB_BWD if tq % ATT_SUB_BWD == 0 else tq
    nq = Lp // tq

    def core(ins, outs, scr):
        q_ref, k_ref, v_ref, bias_ref, lse_ref, db_ref, b_ref = ins
        dq_ref, dk_ref, dv_ref = outs
        dkt_scr, dvt_scr = scr
        j, i = pl.program_id(0), pl.program_id(1)

        @pl.when((j == 0) & (i == 0))
        def _():
            dkt_scr[...] = jnp.zeros_like(dkt_scr)
            dvt_scr[...] = jnp.zeros_like(dvt_scr)

        kk, vv, bb = k_ref[...], v_ref[...], bias_ref[...]
        dbv = db_ref[...]
        rolled = pltpu.roll(dbv, 64, 1)
        lane = lax.broadcasted_iota(jnp.int32, (tq, LANE), 1)
        low = j < 2
        first = jnp.where(low, 0, 64)
        keep = (lane >= first) & (lane < first + 64)
        prod = dbv * b_ref[...].astype(F32)
        deltas = [jnp.sum(jnp.where(lane < 64, prod, 0.0), axis=-1, keepdims=True),
                  jnp.sum(jnp.where(lane >= 64, prod, 0.0), axis=-1, keepdims=True)]
        dkt, dvt = 0.0, 0.0
        for e in range(2):
            src = jnp.where(low, dbv, rolled) if e == 0 else jnp.where(low, rolled, dbv)
            dop = jnp.where(keep, src, 0.0).astype(BF16)
            pbs, dss = [], []
            for r in range(tq // sub):
                rs = slice(r * sub, (r + 1) * sub)
                s = lax.dot_general(q_ref[e, rs, :], kk, NT, preferred_element_type=F32)
                s = jnp.concatenate([s[:, :LANE] + bb[:, :LANE], s[:, LANE:]], axis=1)
                p = jnp.exp(s - lse_ref[e, rs, :])
                dp = lax.dot_general(dop[rs], vv, NT, preferred_element_type=F32)
                ds = (p * (dp - deltas[e][rs])).astype(BF16)
                dq_ref[e, rs, :] = jnp.where(keep[rs], jnp.dot(ds, kk, preferred_element_type=F32), 0.0)
                pbs.append(p.astype(BF16))
                dss.append(ds)
            dkt = dkt + lax.dot_general(q_ref[e], jnp.concatenate(dss, axis=0), TN, preferred_element_type=F32)
            dvt = dvt + lax.dot_general(dop, jnp.concatenate(pbs, axis=0), TN, preferred_element_type=F32)
        dkt_scr[...] += dkt
        dvt_scr[...] += dvt

        @pl.when((j == ATT_H // 2 - 1) & (i == nq - 1))
        def _():
            dk_ref[...] = dkt_scr[...].T
            dv_ref[...] = dvt_scr[...].T

    full = pl.BlockSpec((Lp, LANE), lambda j, i: (0, 0))
    pair = pl.BlockSpec((2, tq, LANE), lambda j, i: (j, i, 0))
    return _ridden_call(
        core, rider, name=name, grid=(ATT_H // 2, nq),
        in_specs=[pair, full, full, pl.BlockSpec((1, Lp), lambda j, i: (0, 0)), pl.BlockSpec((2, tq, 1), lambda j, i: (j, i, 0)),
                  pl.BlockSpec((tq, LANE), lambda j, i: (i, j)), pl.BlockSpec((tq, LANE), lambda j, i: (i, j))],
        out_specs=[pair, full, full],
        out_shape=[SDS((ATT_H, Lp, LANE), F32), SDS((Lp, LANE), F32), SDS((Lp, LANE), F32)],
        scratch_shapes=[pltpu.VMEM((LANE, Lp), F32), pltpu.VMEM((LANE, Lp), F32)], args=(qp, kr, vb, bias, lse, db, b))


def _rope_tables(n_tok):
    f32 = np.float32
    rows = n_tok // GRID_W
    row = np.repeat(np.arange(rows), GRID_W).astype(f32)
    col = np.tile(np.arange(GRID_W), rows).astype(f32)
    axis_dim = HD // 2
    inv = np.power(f32(ROPE_THETA), -np.arange(0, axis_dim, 2, dtype=f32) / f32(axis_dim)).astype(f32)
    ang = np.concatenate([row[:, None] * inv, col[:, None] * inv], axis=-1).astype(f32)
    ang = np.concatenate([np.zeros((LANE, axis_dim), f32), ang], axis=0)
    c, s = np.cos(ang).astype(f32), np.sin(ang).astype(f32)
    c64 = np.concatenate([c[:, :16], c[:, :16], c[:, 16:], c[:, 16:]], axis=1)
    s64 = np.concatenate([-s[:, :16], s[:, :16], -s[:, 16:], s[:, 16:]], axis=1)
    return jnp.asarray(np.concatenate([c64, c64], axis=1)), jnp.asarray(np.concatenate([s64, s64], axis=1))


def _ffn_fwd(tag, h, W, l, j):
    Lp = h.shape[0]
    tm = _rt(Lp, 384)
    g = W["norm_gains"][l, 2 * j].reshape(1, D)

    def body(h_ref, g_ref, wg_ref, wu_ref, wd_ref, h2_ref, a_ref, b_ref, st_ref, nt_ref, pad_scr):
        x = h_ref[...]
        y = x * lax.rsqrt(jnp.mean(x * x, axis=-1, keepdims=True) + EPS) * g_ref[...]
        nt_ref[...] = y.T.astype(BF16)
        nv = y.astype(BF16)
        pad_scr[:, FF_S:] = jnp.zeros((tm, FF_P - FF_S), F32)
        acc = 0.0
        for s in range(NSH):
            a = jnp.dot(nv, wg_ref[s], preferred_element_type=F32)
            b = jnp.dot(nv, wu_ref[s], preferred_element_type=F32)
            sv = a * jax.nn.sigmoid(a) * b
            a_ref[s, :, :FF_S] = a.astype(BF16)
            a_ref[s, :, FF_S:] = jnp.zeros((tm, FF_P - FF_S), BF16)
            b_ref[s, :, :FF_S] = b.astype(BF16)
            b_ref[s, :, FF_S:] = jnp.zeros((tm, FF_P - FF_S), BF16)
            pad_scr[:, :FF_S] = sv
            st_ref[s] = pad_scr[...].T.astype(BF16)
            acc = acc + jnp.dot(sv.astype(BF16), wd_ref[s], preferred_element_type=F32)
        h2_ref[...] = x + 0.5 * acc

    once = dict(pipeline_mode=pl.Buffered(1))
    wup = pl.BlockSpec((NSH, None, D, FF_S), lambda i: (0, l, j, 0), **once)
    row = pl.BlockSpec((tm, D), lambda i: (i, 0))
    slab = pl.BlockSpec((NSH, tm, FF_P), lambda i: (0, i, 0))
    h2, a, b, st, nt = pl.pallas_call(
        body, name=f"ffn_{tag}", grid=(Lp // tm,),
        in_specs=[row, pl.BlockSpec((1, D), lambda i: (0, 0)), wup, wup,
                  pl.BlockSpec((NSH, None, FF_S, D), lambda i: (0, l, j, 0), **once)],
        out_specs=[row, slab, slab, pl.BlockSpec((NSH, FF_P, tm), lambda i: (0, 0, i)), pl.BlockSpec((D, tm), lambda i: (0, i))],
        out_shape=[SDS((Lp, D), F32), SDS((NSH, Lp, FF_P), BF16), SDS((NSH, Lp, FF_P), BF16), SDS((NSH, FF_P, Lp), BF16),
                   SDS((D, Lp), BF16)],
        scratch_shapes=[pltpu.VMEM((tm, FF_P), F32)], compiler_params=_cp(True))(h, g, W["gate"], W["up"], W["down"])
    return h2, dict(h=h, g=g, nt=nt, a=a, b=b, st=st)


def _ffn_bwd(tag, dh, sv, W, G, l, j, rider=None):
    Lp = dh.shape[0]
    tn = 512
    (da, db, dh_prev, dg), carried = _ffn_bwd_acts(tag, dh, sv, W, l, j, rider)
    G["down"] = _mm(f"bwd_{tag}", sv["st"], dh, grid=(NSH, D // tn),
                    a_spec=pl.BlockSpec((None, FF_P, Lp), lambda s, n: (s, 0, 0)), a_sl=(slice(0, FF_S), slice(None)),
                    b_spec=pl.BlockSpec((Lp, tn), lambda s, n: (0, n)),
                    o_spec=pl.BlockSpec((None, None, FF_S, tn), lambda s, n: (s, j, 0, n)), o_shape=(NSH, 2, FF_S, D), o_dtype=F32,
                    dims=NN, acc_shape=None, scale=0.5, into=G.get("down"))
    for key, dact in (("gate", da), ("up", db)):
        G[key] = _mm(f"bw{key[0]}_{tag}", sv["nt"], dact, grid=(NSH, D // tn),
                     a_spec=pl.BlockSpec((tn, Lp), lambda s, m: (m, 0)),
                     b_spec=pl.BlockSpec((None, Lp, FF_P), lambda s, m: (s, 0, 0)), b_sl=(slice(None), slice(0, FF_S)),
                     o_spec=pl.BlockSpec((None, None, tn, FF_S), lambda s, m: (s, j, m, 0)), o_shape=(NSH, 2, D, FF_S),
                     o_dtype=F32, dims=NN, acc_shape=None, into=G.get(key))
    return dh_prev, dg, carried


def _ffn_bwd_acts(tag, dh, sv, W, l, j, rider=None):
    Lp = dh.shape[0]
    tm = _rt(Lp, 384)

    def core(ins, outs, scr):
        dh_ref, h_ref, g_ref, wd_ref, wg_ref, wu_ref, a_ref, b_ref = ins
        da_ref, db_ref, dho_ref, dg_ref = outs
        dhv = dh_ref[...]
        dhb = dhv.astype(BF16)
        dn = 0.0
        for s in range(NSH):
            ds = 0.5 * lax.dot_general(dhb, wd_ref[s], NT, preferred_element_type=F32)
            av = a_ref[s, :, :FF_S].astype(F32)
            bv = b_ref[s, :, :FF_S].astype(F32)
            sg = jax.nn.sigmoid(av)
            da = (ds * bv * (sg * (1.0 + av * (1.0 - sg)))).astype(BF16)
            db = (ds * (av * sg)).astype(BF16)
            da_ref[s, :, :FF_S] = da
            da_ref[s, :, FF_S:] = jnp.zeros((tm, FF_P - FF_S), BF16)
            db_ref[s, :, :FF_S] = db
            db_ref[s, :, FF_S:] = jnp.zeros((tm, FF_P - FF_S), BF16)
            dn = dn + (lax.dot_general(da, wg_ref[s], NT, preferred_element_type=F32)
                       + lax.dot_general(db, wu_ref[s], NT, preferred_element_type=F32))
        x = h_ref[...]
        r = lax.rsqrt(jnp.mean(x * x, axis=-1, keepdims=True) + EPS)
        xh = x * r
        u = dn * g_ref[...]
        dho_ref[...] = dhv + r * (u - xh * jnp.mean(u * xh, axis=-1, keepdims=True))

        @pl.when(pl.program_id(0) == 0)
        def _():
            dg_ref[...] = jnp.zeros_like(dg_ref)

        dg_ref[...] += jnp.sum(dn * xh, axis=0, keepdims=True)

    once = dict(pipeline_mode=pl.Buffered(1))
    wup = pl.BlockSpec((NSH, None, D, FF_S), lambda i: (0, l, j, 0), **once)
    row = pl.BlockSpec((tm, D), lambda i: (i, 0))
    slab = pl.BlockSpec((NSH, tm, FF_P), lambda i: (0, i, 0))
    vec = pl.BlockSpec((1, D), lambda i: (0, 0))
    return _ridden_call(
        core, rider, name=f"bffn_{tag}", grid=(Lp // tm,),
        in_specs=[row, row, vec, pl.BlockSpec((NSH, None, FF_S, D), lambda i: (0, l, j, 0), **once), wup, wup, slab, slab],
        out_specs=[slab, slab, row, vec],
        out_shape=[SDS((NSH, Lp, FF_P), BF16), SDS((NSH, Lp, FF_P), BF16), SDS((Lp, D), F32), SDS((1, D), F32)],
        scratch_shapes=[], args=(dh, sv["h"], sv["g"], W["down"], W["gate"], W["up"], sv["a"], sv["b"]))


def _mixer_fwd(tag, h, W, winp, C, l, G_chunks, rider=None):
    Lp = h.shape[0]
    tm = _rt(Lp, 1408)
    g = W["norm_gains"][l, 1].reshape(1, D)
    H, zt = _norm_project(f"win_{tag}", h, g, winp)
    w2p, b2p = C["w2p"][l], C["b2p"][l]
    qh, kh, vh, gf, gb = _gla_prep(f"glap_{tag}", H, w2p, b2p)
    of, sf = _gla_fwd(f"glaf_{tag}", qh, kh, gf, vh, False, G_chunks)
    ob, sb = _gla_fwd(f"glar_{tag}", qh, kh, gb, vh, True, G_chunks)
    gn = W["gn"][l].reshape(1, 512)
    a, at = _gla_post(f"glao_{tag}", of, ob, H, gn)
    gq, gk = C["gq"][l], C["gk"][l]
    qp, kr, vb = _att_prep(f"attp_{tag}", H, gq, gk, C["cos"], C["sin"], C["bd"])
    (b, bt, lse), carried = _att_fwd(f"attf_{tag}", qp, kr, vb, C["bias"], rider)
    if rider is not None:
        W = rider.updated(W, carried)
    row = pl.BlockSpec((tm, D), lambda i: (i, 0))
    proj = functools.partial(
        _mm, grid=(Lp // tm,), a_spec=pl.BlockSpec((tm, 512), lambda i: (i, 0)),
        b_spec=pl.BlockSpec((NSH, None, 512, 256), lambda i: (0, l, 0, 0)), o_spec=row, o_shape=(Lp, D), o_dtype=F32, dims=NN,
        acc_shape=None, shards="cols")
    pa = proj(f"pa_{tag}", a, W["wpa"])
    pb = proj(f"pb_{tag}", b, W["wpb"])
    bm = W["bm"][l]
    y, yt = _merge_fwd(f"mrg_{tag}", H, pa, pb, bm)
    h2 = _mm(f"wout_{tag}", y, W["wout"], grid=(Lp // tm,), a_spec=row,
             b_spec=pl.BlockSpec((NSH, None, 256, D), lambda i: (0, l, 0, 0)), o_spec=row, o_shape=(Lp, D), o_dtype=F32,
             dims=NN, acc_shape=None, res=h, res_spec=row, shards="rows")
    sv = dict(h=h, g=g, zt=zt, H=H, w2p=w2p, b2p=b2p, qh=qh, kh=kh, vh=vh, gf=gf, gb=gb, of=of, ob=ob, sf=sf, sb=sb, gn=gn,
              at=at, gq=gq, gk=gk, qp=qp, kr=kr, vb=vb, b=b, bt=bt, lse=lse, pa=pa, pb=pb, bm=bm, yt=yt, winp=winp)
    return h2, sv, W


def _mixer_bwd(tag, dh, sv, W, C, G, S, l, G_chunks, rider=None):
    Lp = dh.shape[0]
    tm = _rt(Lp, 1408)
    H = sv["H"]
    row = pl.BlockSpec((tm, D), lambda i: (i, 0))
    dy = _mm(f"bdy_{tag}", dh, W["wout"], grid=(Lp // tm,), a_spec=row,
             b_spec=pl.BlockSpec((NSH, None, 256, D), lambda i: (0, l, 0, 0)), o_spec=row, o_shape=(Lp, D), o_dtype=F32, dims=NT,
             acc_shape=None, shards="rows")
    tn = 512
    G["wout"] = _mm(f"bwo_{tag}", sv["yt"], dh, grid=(NSH, D // tn), a_spec=pl.BlockSpec((256, Lp), lambda s, n: (s, 0)),
                    b_spec=pl.BlockSpec((Lp, tn), lambda s, n: (0, n)),
                    o_spec=pl.BlockSpec((None, 256, tn), lambda s, n: (s, 0, n)), o_shape=(NSH, 256, D), o_dtype=F32, dims=NN,
                    acc_shape=None)
    dpa, dpb, dH, S["bm"][l] = _merge_bwd(f"bmrg_{tag}", H, sv["pa"], sv["pb"], sv["bm"], dy)
    dbranch = {}
    for key, dp, xt in (("wpa", dpa, sv["at"]), ("wpb", dpb, sv["bt"])):
        dbranch[key] = _mm(f"bx{key[2]}_{tag}", dp, W[key], grid=(Lp // tm,), a_spec=row,
                           b_spec=pl.BlockSpec((NSH, None, 512, 256), lambda i: (0, l, 0, 0)),
                           o_spec=pl.BlockSpec((tm, 512), lambda i: (i, 0)), o_shape=(Lp, 512), o_dtype=F32, dims=NT,
                           acc_shape=None, shards="cols")
        G[key] = _mm(f"bw{key[2]}_{tag}", xt, dp, grid=(NSH,), a_spec=pl.BlockSpec((512, Lp), lambda s: (0, 0)),
                     b_spec=pl.BlockSpec((Lp, 256), lambda s: (0, s)),
                     o_spec=pl.BlockSpec((None, 512, 256), lambda s: (s, 0, 0)), o_shape=(NSH, 512, 256), o_dtype=F32, dims=NN,
                     acc_shape=None)
    (dqp, dkr, dvb), carried = _att_bwd(f"attb_{tag}", sv["qp"], sv["kr"], sv["vb"], C["bias"], sv["lse"], dbranch["wpb"],
                                        sv["b"], rider)
    dH, S["gq"][l] = _att_q_bwd(f"attq_{tag}", H, sv["gq"], C["cos"], C["sin"], C["bd"], dqp, dH)
    dH, S["gk"][l] = _att_kv_bwd(f"attk_{tag}", H, sv["gk"], C["cos"], C["sin"], C["bd"], dkr, dvb, dH)
    do, dH, S["gn"][l] = _gla_post_bwd(f"bglo_{tag}", sv["of"], sv["ob"], H, sv["gn"], dbranch["wpa"], dH)
    dqf, dkf, dgf, dvf = _gla_bwd(f"bglf_{tag}", sv["qh"], sv["kh"], sv["gf"], sv["vh"], sv["sf"], do, False, G_chunks)
    dqb, dkb, dgb, dvr = _gla_bwd(f"bglr_{tag}", sv["qh"], sv["kh"], sv["gb"], sv["vh"], sv["sb"], do, True, G_chunks)
    dH = _gla_qkv_bwd(f"bglq_{tag}", dqf, dqb, dkf, dkb, dvf, dvr, dH)
    dH, S["w2p"][l], S["b2p"][l] = _gla_gate_bwd(f"bglg_{tag}", H, sv["w2p"], sv["b2p"], dgf, dgb, dH)
    tmm = 256
    G["winp"] = _mm(f"bwi_{tag}", sv["zt"], dH, grid=(HP // W_CHUNK, D // tmm), a_spec=pl.BlockSpec((tmm, Lp), lambda n, m: (m, 0)),
                    b_spec=pl.BlockSpec((Lp, W_CHUNK), lambda n, m: (0, n)), o_spec=pl.BlockSpec((tmm, W_CHUNK), lambda n, m: (m, n)),
                    o_shape=(D, HP), o_dtype=F32, dims=NN, acc_shape=None)
    dh_prev, dg = _norm_project_bwd(f"bdz_{tag}", dH, sv["winp"], sv["h"], sv["g"], dh)
    return dh_prev, dg, carried


def _winp_layer(win_g, l):
    return _win_to_padded(jnp.transpose(win_g[:, l], (1, 0, 2)).reshape(D, D_IN))


def _local_step(x2, tgt2, W, comm=None):
    n_tok = x2.shape[0]
    Lp = n_tok + LANE
    nc = Lp // CHUNK
    g_chunks = max(g for g in (1, 2, 3, 6) if nc % g == 0)
    cos, sin = _rope_tables(n_tok)
    bd = jnp.asarray(np.kron(np.eye(ATT_H, dtype=np.float32), np.full((HD, HD), 1.0 / HD, np.float32)))
    bias = jnp.where(jnp.arange(Lp) >= PAD, 0.0, -1e30).astype(F32).reshape(1, Lp)
    w2, b2 = W["w2"], W["b2"]
    w2p = jnp.zeros((DEPTH, LANE, 512), F32)
    w2p = w2p.at[:, 0:GLA_RANK, 0:256].set(w2[:, 0]).at[:, GLA_RANK:2 * GLA_RANK, 256:512].set(w2[:, 1]).astype(BF16)
    C = dict(cos=cos, sin=sin, bd=bd, bias=bias, w2p=w2p, b2p=b2.reshape(DEPTH, 1, 512),
             gq=jnp.tile(W["qn"], (1, ATT_H)).reshape(DEPTH, 1, 512), gk=jnp.tile(W["kn"], (1, ATT_KV)).reshape(DEPTH, 1, LANE))
    h = jnp.concatenate([jnp.zeros((PAD, D), F32), W["meta"], x2], axis=0)
    saved = []
    for l in range(DEPTH):
        h, s0 = _ffn_fwd(f"l{l}a", h, W, l, 0)
        rider = comm.fwd_rider(W, l) if comm else None
        h, sm, W = _mixer_fwd(f"l{l}m", h, W, _winp_layer(W["win"], l), C, l, g_chunks, rider)
        h, s1 = _ffn_fwd(f"l{l}b", h, W, l, 1)
        saved.append((s0, sm, s1))
    dh, dfin, loss = _loss_head("loss_head", h, W["fin"].reshape(1, D), tgt2)
    S = dict(bm=[None] * DEPTH, gq=[None] * DEPTH, gk=[None] * DEPTH, gn=[None] * DEPTH, w2p=[None] * DEPTH,
             b2p=[None] * DEPTH, ng=[[None] * 3 for _ in range(DEPTH)])
    Gs = [None] * DEPTH
    for l in reversed(range(DEPTH)):
        s0, sm, s1 = saved[l]
        G = {}
        rider = comm.ffn_rider(l) if comm else None
        dh, S["ng"][l][2], carried = _ffn_bwd(f"l{l}b", dh, s1, W, G, l, 1, rider)
        if rider is not None:
            comm.ffn_carried(l, carried)
        rider = comm.bwd_rider(l) if comm else None
        dh, S["ng"][l][1], carried = _mixer_bwd(f"l{l}m", dh, sm, W, C, G, S, l, g_chunks, rider)
        if rider is not None:
            comm.bwd_carried(l, carried)
        rider = comm.last_rider(l) if comm else None
        dh, S["ng"][l][0], carried = _ffn_bwd(f"l{l}a", dh, s0, W, G, l, 0, rider)
        if rider is not None:
            comm.last_carried(carried)
        Gs[l] = G
        if comm:
            comm.layer_done(l, G)
    small = dict(
        meta=dh[PAD:LANE],
        norm_gains=jnp.stack([jnp.concatenate(S["ng"][l], axis=0) for l in range(DEPTH)]),
        w2=jnp.stack([jnp.stack([S["w2p"][l][0:GLA_RANK, 0:256], S["w2p"][l][GLA_RANK:2 * GLA_RANK, 256:512]]) for l in range(DEPTH)]),
        b2=jnp.stack([S["b2p"][l].reshape(2, 256) for l in range(DEPTH)]),
        gn=jnp.concatenate(S["gn"], axis=0),
        qn=jnp.stack([S["gq"][l].reshape(ATT_H, HD).sum(0) for l in range(DEPTH)]),
        kn=jnp.stack([S["gk"][l].reshape(ATT_KV, HD).sum(0) for l in range(DEPTH)]),
        bm=jnp.stack(S["bm"]),
        fin=dfin.reshape(D),
    )
    return loss[0, 0], dh, Gs, small


def _win_to_padded(w):
    pad = jnp.zeros(w.shape[:-1] + (HP - D_IN,), w.dtype)
    return jnp.concatenate([w[..., 2336:4384], w[..., 0:1536], w[..., 1568:2336], w[..., 1536:1568], pad], axis=-1)


def _win_from_padded(w):
    return jnp.concatenate([w[..., 2048:3584], w[..., 4352:4384], w[..., 3584:4352], w[..., 0:2048]], axis=-1)


def _assemble(g):
    W = dict(gate=g["gate"], up=g["up"], down=g["down"], win=g["win"], wpa=g["wpa"], wpb=g["wpb"], wout=g["wout"])
    sm = g["small"]
    parts = _unpack(sm, SHARDED_SMALL)
    W["meta"] = jnp.transpose(parts["meta"], (1, 0, 2)).reshape(N_META, D)
    W["norm_gains"] = jnp.transpose(parts["norm_gains"], (1, 2, 0, 3)).reshape(DEPTH, 3, D)
    W["w2"] = jnp.transpose(parts["w2"], (1, 2, 3, 0, 4)).reshape(DEPTH, 2, GLA_RANK, 256)
    W["b2"] = jnp.transpose(parts["b2"], (1, 2, 0, 3)).reshape(DEPTH, 2, 256)
    W["bm"] = jnp.transpose(parts["bm"], (1, 2, 0, 3)).reshape(DEPTH, 2, D)
    return W


SHARDED_SMALL = dict(meta=(N_META, 256), norm_gains=(DEPTH, 3, 256), w2=(DEPTH, 2, GLA_RANK, 64), b2=(DEPTH, 2, 64),
                     bm=(DEPTH, 2, 256))
FULL_SMALL = dict(meta=(N_META, D), norm_gains=(DEPTH, 3, D), w2=(DEPTH, 2, GLA_RANK, 256), b2=(DEPTH, 2, 256),
                  gn=(DEPTH, 512), qn=(DEPTH, HD), kn=(DEPTH, HD), bm=(DEPTH, 2, D), fin=(D,))


def _pack(parts, table, rows):
    flat = jnp.concatenate([parts[k].reshape(-1).astype(F32) for k in table])
    return jnp.pad(flat, (0, rows * LANE - flat.shape[0])).reshape(rows, LANE)


def _unpack(packed, table):
    lead = packed.shape[:-2]
    flat = packed.reshape(lead + (-1,))
    out, off = {}, 0
    for k, shp in table.items():
        n = int(np.prod(shp))
        out[k] = flat[..., off:off + n].reshape(lead + tuple(shp))
        off += n
    return out


def _rows_for(table, mult):
    n = sum(int(np.prod(s)) for s in table.values())
    return -(-n // (LANE * mult)) * mult


SMALL_ROWS = _rows_for(SHARDED_SMALL, 16)
GRAD_ROWS = _rows_for(dict(FULL_SMALL, loss=(1,)), 8)


def _place():
    x, y, c = lax.axis_index("x"), lax.axis_index("y"), lax.axis_index("c")
    return x, y, c


def _other_chips(x, y):
    return [(1 - x, y), (x, 1 - y), (1 - x, 1 - y)]


def _cast_place(name, w3, slot):
    A, R, Cc = w3.shape
    tr = _rt(R, 512, 16)

    def body(p_ref, w_ref, o_ref):
        o_ref[...] = w_ref[...].astype(BF16)

    return pl.pallas_call(
        body, name=name,
        grid_spec=pltpu.PrefetchScalarGridSpec(
            num_scalar_prefetch=1, grid=(A, R // tr),
            in_specs=[pl.BlockSpec((None, tr, Cc), lambda a, r, p_ref: (a, r, 0))],
            out_specs=pl.BlockSpec((None, None, tr, Cc), lambda a, r, p_ref: (p_ref[0], a, r, 0))),
        out_shape=SDS((NSH, A, R, Cc), BF16))(slot, w3)


class _GatherRider:
    def __init__(self, keys, bufs, layers):
        n = len(bufs)
        self.keys, self.ins, self.layers = keys, list(bufs), layers
        self.out_shape = [SDS(b.shape, b.dtype) for b in bufs]
        self.aliases = {a: a for a in range(n)}
        self.sems = [pltpu.SemaphoreType.DMA((n, 3)) for _ in range(4)]

    def updated(self, W, carried):
        return {**W, **dict(zip(self.keys, carried))}

    def hooks(self, ins, outs, sems):
        send, recv, fsend, frecv = sems
        n = len(outs)

        def rows(a, slot, core):
            ref, l = outs[a], self.layers[a]
            half = ref.shape[-2] // 2
            return ref.at[slot, pl.ds(core * half, half)] if l is None else ref.at[slot, l, pl.ds(core * half, half)]

        def ici(a, k, part, px, py, c):
            return pltpu.make_async_remote_copy(src_ref=part, dst_ref=part, send_sem=send.at[a, k], recv_sem=recv.at[a, k],
                                                device_id=(px, py, c), device_id_type=MESH)

        def d2d(a, k, part, x, y, c):
            return pltpu.make_async_remote_copy(src_ref=part, dst_ref=part, send_sem=fsend.at[a, k], recv_sem=frecv.at[a, k],
                                                device_id=(x, y, 1 - c), device_id_type=MESH)

        def start():
            x, y, c = _place()
            for a in range(n):
                for k, (px, py) in enumerate(_other_chips(x, y)):
                    ici(a, k, rows(a, 2 * x + y, c), px, py, c).start()

        def mid():
            x, y, c = _place()
            for a in range(n):
                for k, (px, py) in enumerate(_other_chips(x, y)):
                    landed = rows(a, 2 * px + py, c)
                    ici(a, k, landed, px, py, c).wait_recv()
                    d2d(a, k, landed, x, y, c).start()

        def end():
            x, y, c = _place()
            for a in range(n):
                for k, (px, py) in enumerate(_other_chips(x, y)):
                    d2d(a, k, rows(a, 2 * px + py, 1 - c), x, y, c).wait_recv()
                    ici(a, k, rows(a, 2 * x + y, c), px, py, c).wait_send()
                    d2d(a, k, rows(a, 2 * px + py, c), x, y, c).wait_send()

        return start, mid, end


class _ChipExchangeRider:
    def __init__(self, arrs):
        n = len(arrs)
        self.ins = list(arrs)
        self.out_shape = [SDS((3,) + a.shape[1:], a.dtype) for a in arrs]
        self.aliases = {}
        self.sems = [pltpu.SemaphoreType.DMA((n, 3)), pltpu.SemaphoreType.DMA((n, 3))]

    def hooks(self, ins, outs, sems):
        send, recv = sems

        def copy(a, k, px, py, c):
            return pltpu.make_async_remote_copy(src_ref=ins[a].at[2 * px + py], dst_ref=outs[a].at[k], send_sem=send.at[a, k],
                                                recv_sem=recv.at[a, k], device_id=(px, py, c), device_id_type=MESH)

        def start():
            x, y, c = _place()
            for a in range(len(ins)):
                for k, (px, py) in enumerate(_other_chips(x, y)):
                    copy(a, k, px, py, c).start()

        def mid():
            pass

        def end():
            x, y, c = _place()
            for a in range(len(ins)):
                for k, (px, py) in enumerate(_other_chips(x, y)):
                    copy(a, k, px, py, c).wait()

        return start, mid, end


def _run_rider(name, rider):
    r_in, r_out = len(rider.ins), len(rider.out_shape)

    def body(*refs):
        start, mid, end = rider.hooks(refs[:r_in], refs[r_in:r_in + r_out], refs[r_in + r_out:])
        start()
        mid()
        end()

    any_spec = pl.BlockSpec(memory_space=pl.ANY)
    return pl.pallas_call(
        body, name=name, in_specs=[any_spec] * r_in, out_specs=[any_spec] * r_out, out_shape=list(rider.out_shape),
        scratch_shapes=list(rider.sems), input_output_aliases=dict(rider.aliases))(*rider.ins)


class _PairExchangeRider:
    def __init__(self, arrs):
        n = len(arrs)
        self.ins = list(arrs)
        self.out_shape = [SDS((NSH, a.shape[1] // 2, a.shape[2]), a.dtype) for a in arrs]
        self.aliases = {}
        self.sems = [pltpu.SemaphoreType.DMA((n,)), pltpu.SemaphoreType.DMA((n,))]

    def hooks(self, ins, outs, sems):
        send, recv = sems

        def copy(a):
            x, y, c = _place()
            half = ins[a].shape[1] // 2
            return pltpu.make_async_remote_copy(
                src_ref=ins[a].at[:, pl.ds((1 - c) * half, half)], dst_ref=outs[a], send_sem=send.at[a], recv_sem=recv.at[a],
                device_id=(x, y, 1 - c), device_id_type=MESH)

        def start():
            for a in range(len(ins)):
                copy(a).start()

        def mid():
            pass

        def end():
            for a in range(len(ins)):
                copy(a).wait()

        return start, mid, end


def _pair_add(name, g, p, core):
    _, Rh, Cc = p.shape
    tr = _rt(Rh, 512, 16)
    nr = Rh // tr

    def body(c_ref, g_ref, p_ref, o_ref, ob_ref):
        v = g_ref[...] + p_ref[...]
        o_ref[...] = v
        ob_ref[...] = v.astype(BF16)

    blk = pl.BlockSpec((None, tr, Cc), lambda s, r, c_ref: (s, r, 0))
    return pl.pallas_call(
        body, name=name,
        grid_spec=pltpu.PrefetchScalarGridSpec(
            num_scalar_prefetch=1, grid=(NSH, nr),
            in_specs=[pl.BlockSpec((None, tr, Cc), lambda s, r, c_ref: (s, c_ref[0] * nr + r, 0)), blk],
            out_specs=[blk, blk]),
        out_shape=[SDS((NSH, Rh, Cc), F32), SDS((NSH, Rh, Cc), BF16)])(core, g, p)


def _chip_add(name, hsum, q, chip, core, l, into):
    _, Rh, Cc = hsum.shape
    tr = _rt(Rh, 512, 8)
    nr = Rh // tr

    def body(*refs):
        h_ref, q_ref, o_ref = refs[2], refs[3], refs[-1]
        o_ref[...] = ((h_ref[...] + q_ref[0].astype(F32)) + q_ref[1].astype(F32)) + q_ref[2].astype(F32)

    in_specs = [pl.BlockSpec((None, tr, Cc), lambda r, p_ref, c_ref: (p_ref[0], r, 0)),
                pl.BlockSpec((3, tr, Cc), lambda r, p_ref, c_ref: (0, r, 0))]
    args = [chip, core, hsum, q]
    aliases = {}
    if into is not None:
        in_specs.append(pl.BlockSpec(memory_space=pl.ANY))
        args.append(into)
        aliases = {4: 0}
    return pl.pallas_call(
        body, name=name,
        grid_spec=pltpu.PrefetchScalarGridSpec(
            num_scalar_prefetch=2, grid=(nr,), in_specs=in_specs,
            out_specs=pl.BlockSpec((None, tr, Cc), lambda r, p_ref, c_ref: (l, c_ref[0] * nr + r, 0))),
        out_shape=SDS((DEPTH, 2 * Rh, Cc), F32), input_output_aliases=aliases)(*args)


class _PairShareRider:
    def __init__(self, bufs, lo, hi):
        n = len(bufs)
        self.ins, self.lo, self.hi = list(bufs), lo, hi
        self.out_shape = [SDS(b.shape, b.dtype) for b in bufs]
        self.aliases = {a: a for a in range(n)}
        self.sems = [pltpu.SemaphoreType.DMA((n,)), pltpu.SemaphoreType.DMA((n,))]

    def hooks(self, ins, outs, sems):
        send, recv = sems
        lo, hi = self.lo, self.hi

        def copy(a, whose):
            x, y, c = _place()
            half = outs[a].shape[1] // 2
            part = outs[a].at[lo:hi, pl.ds((c if whose == 0 else 1 - c) * half, half)]
            return pltpu.make_async_remote_copy(src_ref=part, dst_ref=part, send_sem=send.at[a], recv_sem=recv.at[a],
                                                device_id=(x, y, 1 - c), device_id_type=MESH)

        def start():
            for a in range(len(outs)):
                copy(a, 0).start()

        def mid():
            pass

        def end():
            for a in range(len(outs)):
                copy(a, 0).wait_send()
                copy(a, 1).wait_recv()

        return start, mid, end


def _allreduce_small(v):
    rows = v.shape[0]

    def body(v_ref, o_ref, buf, send, recv):
        x, y, c = _place()
        me = 4 * x + 2 * y + c
        buf[me] = v_ref[...]
        cps = []
        k = 0
        for dx in range(2):
            for dy in range(2):
                for dc in range(2):
                    if dx + dy + dc == 0:
                        continue
                    cp = pltpu.make_async_remote_copy(
                        src_ref=v_ref, dst_ref=buf.at[me], send_sem=send.at[k], recv_sem=recv.at[k],
                        device_id=(jnp.bitwise_xor(x, dx), jnp.bitwise_xor(y, dy), jnp.bitwise_xor(c, dc)), device_id_type=MESH)
                    cp.start()
                    cps.append((cp, dx, dy, dc))
                    k += 1
        for k, (cp, dx, dy, dc) in enumerate(cps):
            cp.wait_send()
            src = 4 * jnp.bitwise_xor(x, dx) + 2 * jnp.bitwise_xor(y, dy) + jnp.bitwise_xor(c, dc)
            pltpu.make_async_remote_copy(
                src_ref=v_ref, dst_ref=buf.at[src], send_sem=send.at[k], recv_sem=recv.at[k],
                device_id=(x, y, c), device_id_type=MESH).wait_recv()
        acc = buf[0]
        for d in range(1, 8):
            acc = acc + buf[d]
        o_ref[...] = acc

    vm = pl.BlockSpec(memory_space=pltpu.VMEM)
    return pl.pallas_call(
        body, name="allreduce_small", in_specs=[vm], out_specs=vm, out_shape=SDS((rows, LANE), F32),
        scratch_shapes=[pltpu.VMEM((8, rows, LANE), F32), pltpu.SemaphoreType.DMA((7,)), pltpu.SemaphoreType.DMA((7,))])(v)


def _adamw(name, w, g, m, v, lo=0, hi=None, into=None, rider=None):
    A, R, Cc = w.shape
    hi = A if hi is None else hi
    tr = _rt(R, 512, 8)

    def core(ins, outs, scr):
        w_ref, g_ref, m_ref, v_ref = ins[:4]
        d_ref, mo_ref, vo_ref = outs
        gv = g_ref[...]
        mn = ADAM_B1 * m_ref[...] + (1.0 - ADAM_B1) * gv
        vn = ADAM_B2 * v_ref[...] + (1.0 - ADAM_B2) * (gv * gv)
        m_hat = mn / (1.0 - ADAM_B1 ** ADAM_STEP)
        v_hat = vn / (1.0 - ADAM_B2 ** ADAM_STEP)
        d_ref[...] = -ADAM_LR * (m_hat / (jnp.sqrt(v_hat) + ADAM_EPS) + ADAM_WD * w_ref[...])
        mo_ref[...] = mn
        vo_ref[...] = vn

    blk = pl.BlockSpec((None, tr, Cc), lambda a, r: (lo + a, r, 0))
    in_specs, args, aliases = [blk] * 4, [w, g, m, v], {}
    if into is not None:
        in_specs = in_specs + [pl.BlockSpec(memory_space=pl.ANY)] * 3
        args = args + list(into)
        aliases = {4: 0, 5: 1, 6: 2}
    return _ridden_call(core, rider, name=name, grid=(hi - lo, R // tr), in_specs=in_specs, out_specs=[blk] * 3,
                        out_shape=[SDS(w.shape, F32)] * 3, scratch_shapes=[], args=args, aliases=aliases)


BIG = ("gate", "up", "down", "win", "wpa", "wpb", "wout")
SMALL = ("meta", "norm_gains", "w2", "b2", "gn", "qn", "kn", "bm", "fin")


def _view3(a):
    return a.reshape(a.shape[0], -1, a.shape[-1])


class _StepComm:
    def __init__(self, chip, core):
        self.pvec, self.cvec = chip.reshape(1), core.reshape(1)
        self.loc, self.sums = {}, {}
        self.red = {k: None for k in BIG}

    def fwd_rider(self, W, l):
        if l + 1 >= DEPTH:
            return None
        return _GatherRider(BIG, [W[k] for k in BIG], [l + 1] * len(BIG))

    def layer_done(self, l, G):
        dwin = jnp.transpose(_win_from_padded(G["winp"]).reshape(D, NSH, D_IN // NSH), (1, 0, 2))
        self.loc[l] = [dwin if k == "win" else _view3(G[k]) for k in BIG]

    def ffn_rider(self, l):
        return _PairExchangeRider(self.loc[l + 1]) if l + 1 < DEPTH else None

    def ffn_carried(self, l, got):
        self._pair_add(l + 1, got)

    def _pair_add(self, l, got):
        self.sums[l] = [_pair_add(f"pair_add_{k}_l{l}", a, p, self.cvec) for k, a, p in zip(BIG, self.loc.pop(l), got)]

    def bwd_rider(self, l):
        if l + 1 >= DEPTH:
            return None
        return _ChipExchangeRider([s[1] for s in self.sums[l + 1]])

    def bwd_carried(self, l, arrived):
        self._chip_add(l + 1, arrived)

    def _chip_add(self, l, arrived):
        for k, s, q in zip(BIG, self.sums.pop(l), arrived):
            self.red[k] = _chip_add(f"chip_add_{k}_l{l}", s[0], q, self.pvec, self.cvec, l, self.red[k])

    def last_rider(self, l):
        return _PairShareRider([self.red[k] for k in BIG], 1, DEPTH) if l == 0 else None

    def last_carried(self, shared):
        self.red = dict(zip(BIG, shared))

    def finish(self, w, m, v):
        self._pair_add(0, _run_rider("pair_exchange_l0", _PairExchangeRider(self.loc[0])))
        sums = dict(zip(BIG, self.sums.pop(0)))
        upd = {}
        for k in BIG:
            upd[k], arrived = _adamw(f"adamw_{k}_l123", w[k], self.red[k], m[k], v[k], 1, DEPTH,
                                     rider=_ChipExchangeRider([sums[k][1]]))
            self.red[k] = _chip_add(f"chip_add_{k}_l0", sums[k][0], arrived[0], self.pvec, self.cvec, 0, self.red[k])
        red = dict(zip(BIG, _run_rider("pair_share_l0", _PairShareRider([self.red[k] for k in BIG], 0, 1))))
        for k in BIG:
            upd[k], _ = _adamw(f"adamw_{k}_l0", w[k], red[k], m[k], v[k], 0, 1, into=upd[k])
        return red, upd


def kernel(x, meta_tokens, norm_gains, ffn_w_gate, ffn_w_up, ffn_w_down, w_in, gla_w2, gla_b2, gla_gn, q_norm, k_norm, w_pa, w_pb, b_merge, w_out, final_norm, loss_target, m_meta_tokens, m_norm_gains, m_ffn_w_gate, m_ffn_w_up, m_ffn_w_down, m_w_in, m_gla_w2, m_gla_b2, m_gla_gn, m_q_norm, m_k_norm, m_w_pa, m_w_pb, m_b_merge, m_w_out, m_final_norm, v_meta_tokens, v_norm_gains, v_ffn_w_gate, v_ffn_w_up, v_ffn_w_down, v_w_in, v_gla_w2, v_gla_b2, v_gla_gn, v_q_norm, v_k_norm, v_w_pa, v_w_pb, v_b_merge, v_w_out, v_final_norm):
    big_w = dict(gate=ffn_w_gate, up=ffn_w_up, down=ffn_w_down, win=w_in, wpa=w_pa, wpb=w_pb, wout=w_out)
    big_m = dict(gate=m_ffn_w_gate, up=m_ffn_w_up, down=m_ffn_w_down, win=m_w_in, wpa=m_w_pa, wpb=m_w_pb, wout=m_w_out)
    big_v = dict(gate=v_ffn_w_gate, up=v_ffn_w_up, down=v_ffn_w_down, win=v_w_in, wpa=v_w_pa, wpb=v_w_pb, wout=v_w_out)
    small_w = dict(meta=meta_tokens, norm_gains=norm_gains, w2=gla_w2, b2=gla_b2, gn=gla_gn, qn=q_norm, kn=k_norm,
                   bm=b_merge, fin=final_norm)
    small_m = dict(meta=m_meta_tokens, norm_gains=m_norm_gains, w2=m_gla_w2, b2=m_gla_b2, gn=m_gla_gn, qn=m_q_norm,
                   kn=m_k_norm, bm=m_b_merge, fin=m_final_norm)
    small_v = dict(meta=v_meta_tokens, norm_gains=v_norm_gains, w2=v_gla_w2, b2=v_gla_b2, gn=v_gla_gn, qn=v_q_norm,
                   kn=v_k_norm, bm=v_b_merge, fin=v_final_norm)
    xi, yi, ci = _place()
    chip = (2 * xi + yi).astype(jnp.int32)

    comm = _StepComm(chip, ci.astype(jnp.int32))
    shard_pack = _pack({k: small_w[k] for k in SHARDED_SMALL}, SHARDED_SMALL, SMALL_ROWS)
    placed = [_cast_place(f"cast_{k}", _view3(big_w[k]), comm.pvec) for k in BIG]
    placed.append(lax.dynamic_update_slice(jnp.zeros((NSH, SMALL_ROWS, LANE), F32), shard_pack[None], (chip, 0, 0)))
    gathered = _run_rider("gather_l0", _GatherRider(BIG + ("small",), placed, [0] * len(BIG) + [None]))
    W = _assemble(dict(zip(BIG + ("small",), gathered)))
    W.update(gn=gla_gn, qn=q_norm, kn=k_norm, fin=final_norm)

    loss, dh0, _, gs = _local_step(x[0], loss_target[0], W, comm)
    grad_x = dh0[LANE:][None]
    red, upd = comm.finish(*({k: _view3(t[k]) for k in BIG} for t in (big_w, big_m, big_v)))
    grads, deltas, new_m, new_v = {}, {}, {}, {}
    for k in BIG:
        shp = big_w[k].shape
        grads[k] = red[k].reshape(shp)
        deltas[k], new_m[k], new_v[k] = (t.reshape(shp) for t in upd[k])

    gs["loss"] = loss.reshape(1)
    table = dict(FULL_SMALL, loss=(1,))
    tot = _unpack(_allreduce_small(_pack(gs, table, GRAD_ROWS)), table)
    loss_out = tot["loss"][0]
    sl = dict(meta=(1, 256), norm_gains=(2, 256), w2=(3, 64), b2=(2, 64), bm=(2, 256))
    for k in SMALL:
        gk = tot[k]
        if k in sl:
            ax, width = sl[k]
            gk = lax.dynamic_slice_in_dim(gk, chip * width, width, axis=ax)
        grads[k] = gk
    tbl = {k: small_w[k].shape for k in SMALL}
    rows = _rows_for(tbl, 8)
    packs = [_pack(src, tbl, rows)[None] for src in (small_w, grads, small_m, small_v)]
    (d, mn, vn), _ = _adamw("adamw_small", *packs)
    for dst, packed in ((deltas, d), (new_m, mn), (new_v, vn)):
        dst.update(_unpack(packed[0], tbl))

    order = ("meta", "norm_gains", "gate", "up", "down", "win", "w2", "b2", "gn", "qn", "kn", "wpa", "wpb", "bm", "wout", "fin")
    return (loss_out, grad_x, *[grads[k] for k in order], *[deltas[k] for k in order], *[new_m[k] for k in order],
            *[new_v[k] for k in order])
```

```python
import functools

import numpy as np
import jax
import jax.numpy as jnp
from jax import lax
from jax.experimental import pallas as pl
from jax.experimental.pallas import tpu as pltpu

F32, BF16 = jnp.float32, jnp.bfloat16
SDS = jax.ShapeDtypeStruct
HIGHEST = lax.Precision.HIGHEST
MESH = pl.DeviceIdType.MESH

D = 1024
DEPTH = 4
N_META = 16
GRID_W = 64
GLA_H, GLA_DK, GLA_DV, GLA_RANK, GLA_TAU, CHUNK = 4, 64, 128, 16, 16.0, 64
ATT_H, ATT_KV, HD = 8, 2, 64
D_FF = 2816
EPS = 1e-6
ROPE_THETA = 10000.0
ADAM_LR, ADAM_B1, ADAM_B2, ADAM_EPS, ADAM_WD, ADAM_STEP = 0.001, 0.9, 0.999, 1e-08, 0.01, 10

NSH = 4
FF_S = D_FF // NSH
FF_P = 768
LANE = 128
PAD = LANE - N_META
D_IN = 4384
C_GA, C_GB, C_QA, C_KA, C_VA, C_RA, C_QB, C_KB, C_VB, C_LR, HP = 0, 1024, 2048, 2304, 2560, 3072, 3584, 4096, 4224, 4352, 4480
VMEM_BIG = 56 * 2 ** 20
ATT_SUB_FWD, ATT_SUB_BWD = 128, 192
RIDER_MID_BACK = 5


def _rt(n, cap, mult=LANE):
    best = None
    t = mult
    while t <= min(n, cap):
        if n % t == 0:
            best = t
        t += mult
    assert best is not None, (n, cap, mult)
    return best


def _cp(big=False):
    return pltpu.CompilerParams(vmem_limit_bytes=VMEM_BIG) if big else None


def _row_ids(i, tm):
    return i * tm + lax.broadcasted_iota(jnp.int32, (tm, 1), 0)


def _mm(name, a, b, *, grid, a_spec, b_spec, o_spec, o_shape, o_dtype, dims, acc_shape, nk=1, scale=None,
        res=None, res_spec=None, a_sl=None, b_sl=None, pad_w=None, into=None, shards=None):
    has_res, has_into = res is not None, into is not None

    def body(*refs):
        a_ref, b_ref = refs[0], refs[1]
        p = 2
        res_ref = None
        if has_res:
            res_ref = refs[p]
            p += 1
        if has_into:
            p += 1
        o_ref = refs[p]
        acc_ref = refs[p + 1] if nk > 1 else None
        av = (a_ref[a_sl] if a_sl is not None else a_ref[...]).astype(BF16)
        bv = (b_ref[b_sl] if b_sl is not None else b_ref[...]).astype(BF16)
        if shards == "rows":
            bv = bv.reshape(bv.shape[0] * bv.shape[1], bv.shape[2])
        if shards == "cols":
            w = bv.shape[2]
            if dims == NN:
                prod = jnp.concatenate([lax.dot_general(av, bv[s], NN, preferred_element_type=F32) for s in range(NSH)], axis=1)
            else:
                prod = sum(lax.dot_general(av[:, s * w:(s + 1) * w], bv[s], NT, preferred_element_type=F32) for s in range(NSH))
        else:
            prod = lax.dot_general(av, bv, dims, preferred_element_type=F32)

        def finish(v):
            if scale is not None:
                v = v * scale
            if has_res:
                v = v + res_ref[...]
            v = v.astype(o_dtype)
            if pad_w is None:
                o_ref[...] = v
            else:
                w = v.shape[-1]
                o_ref[:, :w] = v
                o_ref[:, w:] = jnp.zeros((v.shape[0], pad_w - w), o_dtype)

        if nk == 1:
            finish(prod)
        else:
            k = pl.program_id(len(grid) - 1)

            @pl.when(k == 0)
            def _():
                acc_ref[...] = prod

            @pl.when(k > 0)
            def _():
                acc_ref[...] += prod

            @pl.when(k == nk - 1)
            def _():
                finish(acc_ref[...])

    in_specs = [a_spec, b_spec]
    args = [a, b]
    if has_res:
        in_specs.append(res_spec)
        args.append(res)
    aliases = {}
    if has_into:
        aliases = {len(args): 0}
        in_specs.append(pl.BlockSpec(memory_space=pl.ANY))
        args.append(into)
        o_shape = into.shape
    return pl.pallas_call(
        body, name=name, grid=grid, in_specs=in_specs, out_specs=o_spec, out_shape=SDS(tuple(o_shape), o_dtype),
        scratch_shapes=[pltpu.VMEM(acc_shape, F32)] if nk > 1 else [], input_output_aliases=aliases,
        compiler_params=_cp(True))(*args)


NN = (((1,), (0,)), ((), ()))
NT = (((1,), (1,)), ((), ()))
TN = (((0,), (0,)), ((), ()))


W_CHUNK = 896


def _norm_project(name, h, g, winp, rider=None):
    Lp = h.shape[0]
    tm = _rt(Lp, 384)

    def core(ins, outs, scr):
        h_ref, g_ref, w_ref = ins
        o_ref, zt_ref = outs
        x = h_ref[...]
        y = x * lax.rsqrt(jnp.mean(x * x, axis=-1, keepdims=True) + EPS) * g_ref[...]
        zt_ref[...] = y.T.astype(BF16)
        z = y.astype(BF16)
        for n in range(HP // W_CHUNK):
            cs = slice(n * W_CHUNK, (n + 1) * W_CHUNK)
            o_ref[:, cs] = jnp.dot(z, w_ref[:, cs], preferred_element_type=F32)

    return _ridden_call(
        core, rider, name=name, grid=(Lp // tm,),
        in_specs=[pl.BlockSpec((tm, D), lambda i: (i, 0)), pl.BlockSpec((1, D), lambda i: (0, 0)),
                  pl.BlockSpec((D, HP), lambda i: (0, 0), pipeline_mode=pl.Buffered(1))],
        out_specs=[pl.BlockSpec((tm, HP), lambda i: (i, 0)), pl.BlockSpec((D, tm), lambda i: (0, i))],
        out_shape=[SDS((Lp, HP), F32), SDS((D, Lp), BF16)], scratch_shapes=[], args=(h, g, winp), mid_back=1)


def _norm_project_bwd(name, dH, winp, h, g, dh):
    Lp = h.shape[0]
    tm = _rt(Lp, 384)

    def body(dhh_ref, w_ref, h_ref, g_ref, dh_ref, o_ref, dg_ref):
        dn = lax.dot_general(dhh_ref[...], w_ref[...], NT, preferred_element_type=F32)
        x = h_ref[...]
        r = lax.rsqrt(jnp.mean(x * x, axis=-1, keepdims=True) + EPS)
        xh = x * r
        u = dn * g_ref[...]
        o_ref[...] = dh_ref[...] + r * (u - xh * jnp.mean(u * xh, axis=-1, keepdims=True))

        @pl.when(pl.program_id(0) == 0)
        def _():
            dg_ref[...] = jnp.zeros_like(dg_ref)

        dg_ref[...] += jnp.sum(dn * xh, axis=0, keepdims=True)

    row = pl.BlockSpec((tm, D), lambda i: (i, 0))
    vec = pl.BlockSpec((1, D), lambda i: (0, 0))
    return pl.pallas_call(
        body, name=name, grid=(Lp // tm,),
        in_specs=[pl.BlockSpec((tm, HP), lambda i: (i, 0)), pl.BlockSpec((D, HP), lambda i: (0, 0), pipeline_mode=pl.Buffered(1)),
                  row, vec, row],
        out_specs=[row, vec], out_shape=[SDS((Lp, D), F32), SDS((1, D), F32)], compiler_params=_cp(True))(dH, winp, h, g, dh)


def _loss_head(name, h, g, tgt):
    Lp = h.shape[0]
    tm = LANE

    def body(h_ref, g_ref, t_ref, dh_ref, dg_ref, loss_ref):
        i = pl.program_id(0)
        x = h_ref[...]
        r = lax.rsqrt(jnp.mean(x * x, axis=-1, keepdims=True) + EPS)
        xh = x * r
        gg = g_ref[...]
        err = jnp.where(i >= 1, xh * gg - t_ref[...], 0.0)
        dy = err * (1.0 / D)
        u = dy * gg
        dh_ref[...] = r * (u - xh * jnp.mean(u * xh, axis=-1, keepdims=True))

        @pl.when(i == 0)
        def _():
            dg_ref[...] = jnp.zeros_like(dg_ref)
            loss_ref[...] = jnp.zeros_like(loss_ref)

        dg_ref[...] += jnp.sum(dy * xh, axis=0, keepdims=True)
        loss_ref[...] += (0.5 / D) * jnp.sum(err * err)

    row = pl.BlockSpec((tm, D), lambda i: (i, 0))
    vec = pl.BlockSpec((1, D), lambda i: (0, 0))
    return pl.pallas_call(
        body, name=name, grid=(Lp // tm,),
        in_specs=[row, vec, pl.BlockSpec((tm, D), lambda i: (jnp.maximum(i - 1, 0), 0))],
        out_specs=[row, vec, pl.BlockSpec((8, LANE), lambda i: (0, 0))],
        out_shape=[SDS((Lp, D), F32), SDS((1, D), F32), SDS((8, LANE), F32)])(h, g, tgt)


def _merge_fwd(name, H, pa, pb, bm):
    Lp = H.shape[0]
    tm = _rt(Lp, 384)

    def body(g_ref, pa_ref, pb_ref, bm_ref, y_ref, yt_ref):
        gv = g_ref[...]
        y = (jax.nn.sigmoid(gv[:, :D] + bm_ref[0:1, :]) * pa_ref[...]
             + jax.nn.sigmoid(gv[:, D:] + bm_ref[1:2, :]) * pb_ref[...])
        y_ref[...] = y.astype(BF16)
        yt_ref[...] = y.T.astype(BF16)

    row = pl.BlockSpec((tm, D), lambda i: (i, 0))
    return pl.pallas_call(
        body, name=name, grid=(Lp // tm,),
        in_specs=[pl.BlockSpec((tm, 2 * D), lambda i: (i, 0)), row, row, pl.BlockSpec((2, D), lambda i: (0, 0))],
        out_specs=[row, pl.BlockSpec((D, tm), lambda i: (0, i))],
        out_shape=[SDS((Lp, D), BF16), SDS((D, Lp), BF16)])(H, pa, pb, bm)


def _merge_bwd(name, H, pa, pb, bm, dy):
    Lp = H.shape[0]
    tm = _rt(Lp, 384)

    def body(g_ref, pa_ref, pb_ref, bm_ref, dy_ref, dpa_ref, dpb_ref, dh_ref, dbm_ref):
        gv = g_ref[...]
        dyv = dy_ref[...]
        sa = jax.nn.sigmoid(gv[:, :D] + bm_ref[0:1, :])
        sb = jax.nn.sigmoid(gv[:, D:] + bm_ref[1:2, :])
        dpa_ref[...] = (dyv * sa).astype(BF16)
        dpb_ref[...] = (dyv * sb).astype(BF16)
        dga = dyv * pa_ref[...] * (sa * (1.0 - sa))
        dgb = dyv * pb_ref[...] * (sb * (1.0 - sb))
        dh_ref[:, :D] = dga.astype(BF16)
        dh_ref[:, D:] = dgb.astype(BF16)

        @pl.when(pl.program_id(0) == 0)
        def _():
            dbm_ref[...] = jnp.zeros_like(dbm_ref)

        dbm_ref[0:1, :] += jnp.sum(dga, axis=0, keepdims=True)
        dbm_ref[1:2, :] += jnp.sum(dgb, axis=0, keepdims=True)

    row = pl.BlockSpec((tm, D), lambda i: (i, 0))
    two = pl.BlockSpec((2, D), lambda i: (0, 0))
    gate = pl.BlockSpec((tm, 2 * D), lambda i: (i, 0))
    return pl.pallas_call(
        body, name=name, grid=(Lp // tm,), in_specs=[gate, row, row, two, row], out_specs=[row, row, gate, two],
        out_shape=[SDS((Lp, D), BF16), SDS((Lp, D), BF16), SDS((Lp, HP), BF16), SDS((2, D), F32)])(H, pa, pb, bm, dy)


def _gla_prep(name, H, w2p, b2p):
    Lp = H.shape[0]
    tm = _rt(Lp, 384)

    def body(qk_ref, v_ref, lr_ref, w_ref, b_ref, q_o, k_o, v_o, gf_o, gb_o):
        valid = _row_ids(pl.program_id(0), tm) >= PAD
        qk = qk_ref[...]
        vv = v_ref[...]
        pre = jnp.dot(lr_ref[...].astype(BF16), w_ref[...], preferred_element_type=F32) + b_ref[...]
        g = jnp.where(valid, jax.nn.log_sigmoid(pre) * (1.0 / GLA_TAU), 0.0)
        for hh in range(GLA_H):
            q_o[hh] = qk[:, 64 * hh:64 * hh + 64] * (GLA_DK ** -0.5)
            k_o[hh] = qk[:, 256 + 64 * hh:256 + 64 * hh + 64]
            v_o[hh] = vv[:, 128 * hh:128 * hh + 128].astype(BF16)
            gf_o[hh] = g[:, 64 * hh:64 * hh + 64]
            gb_o[hh] = g[:, 256 + 64 * hh:256 + 64 * hh + 64]

    h64 = pl.BlockSpec((GLA_H, tm, 64), lambda i: (0, i, 0))
    h128 = pl.BlockSpec((GLA_H, tm, 128), lambda i: (0, i, 0))
    return pl.pallas_call(
        body, name=name, grid=(Lp // tm,),
        in_specs=[pl.BlockSpec((tm, 512), lambda i: (i, C_QA // 512)), pl.BlockSpec((tm, 512), lambda i: (i, C_VA // 512)),
                  pl.BlockSpec((tm, LANE), lambda i: (i, C_LR // LANE)), pl.BlockSpec((LANE, 512), lambda i: (0, 0)),
                  pl.BlockSpec((1, 512), lambda i: (0, 0))],
        out_specs=[h64, h64, h128, h64, h64],
        out_shape=[SDS((GLA_H, Lp, 64), F32), SDS((GLA_H, Lp, 64), F32), SDS((GLA_H, Lp, 128), BF16),
                   SDS((GLA_H, Lp, 64), F32), SDS((GLA_H, Lp, 64), F32)])(H, H, H, w2p, b2p)


def _bdot(a, b, ca, cb, precision=None):
    return lax.dot_general(a, b, ((ca, cb), ((0,), (0,))), precision=precision, preferred_element_type=F32)


def _gla_chunk_terms(q_ref, k_ref, g_ref, v_ref, G, rev):
    B = GLA_H * G
    qv = q_ref[...].reshape(B, CHUNK, GLA_DK)
    kv = k_ref[...].reshape(B, CHUNK, GLA_DK)
    gv = g_ref[...].reshape(B, CHUNK, GLA_DK)
    vv = v_ref[...].reshape(B, CHUNK, GLA_DV)
    ii = lax.broadcasted_iota(jnp.int32, (CHUNK, CHUNK), 0)
    jj = lax.broadcasted_iota(jnp.int32, (CHUNK, CHUNK), 1)
    tri = (jj >= ii) if rev else (jj <= ii)
    tb = jnp.broadcast_to(tri.astype(F32)[None], (B, CHUNK, CHUNK))
    bc = _bdot(tb, gv, (2,), (1,), HIGHEST)
    bt = bc[:, 0:1, :] if rev else bc[:, CHUNK - 1:CHUNK, :]
    eq, eki, eke = jnp.exp(bc), jnp.exp(-bc), jnp.exp(bt - bc)
    qd, ki, ke = qv * eq, kv * eki, kv * eke
    att = jnp.where(tri[None], _bdot(qd.astype(BF16), ki.astype(BF16), (2,), (2,)), 0.0)
    dm = jnp.exp(_bdot(gv, jnp.ones((B, CHUNK, GLA_DV), F32), (1,), (1,), HIGHEST))
    return dict(B=B, vv=vv, tri=tri, tb=tb, bt=bt, eq=eq, eki=eki, eke=eke, qd=qd, ki=ki, ke=ke, att=att, dm=dm)


def _gla_fwd(name, q, k, g, v, rev, G):
    Lp = q.shape[1]
    tg = G * CHUNK
    ng = Lp // tg

    def body(q_ref, k_ref, g_ref, v_ref, o_ref, ss_ref, s_scr):
        @pl.when(pl.program_id(0) == 0)
        def _():
            s_scr[...] = jnp.zeros_like(s_scr)

        t = _gla_chunk_terms(q_ref, k_ref, g_ref, v_ref, G, rev)
        B, vv = t["B"], t["vv"]
        qd = t["qd"].astype(BF16)
        oi = _bdot(t["att"].astype(BF16), vv, (2,), (1,))
        kvc = _bdot(t["ke"].astype(BF16), vv, (1,), (1,)).reshape(GLA_H, G, GLA_DK, GLA_DV)
        dm = t["dm"].reshape(GLA_H, G, GLA_DK, GLA_DV)
        s = s_scr[...]
        sp = [None] * G
        for c in (range(G - 1, -1, -1) if rev else range(G)):
            sp[c] = s
            ss_ref[c] = s
            s = dm[:, c] * s + kvc[:, c]
        s_scr[...] = s
        spb = jnp.stack(sp, axis=1).reshape(B, GLA_DK, GLA_DV).astype(BF16)
        o_ref[...] = (oi + _bdot(qd, spb, (2,), (1,))).reshape(GLA_H, tg, GLA_DV)

    blk = (lambda i: (0, ng - 1 - i, 0)) if rev else (lambda i: (0, i, 0))
    sblk = (lambda i: (ng - 1 - i, 0, 0, 0)) if rev else (lambda i: (i, 0, 0, 0))
    h64 = pl.BlockSpec((GLA_H, tg, 64), blk)
    h128 = pl.BlockSpec((GLA_H, tg, 128), blk)
    return pl.pallas_call(
        body, name=name, grid=(ng,), in_specs=[h64, h64, h64, h128],
        out_specs=[h128, pl.BlockSpec((G, GLA_H, GLA_DK, GLA_DV), sblk)],
        out_shape=[SDS((GLA_H, Lp, GLA_DV), F32), SDS((Lp // CHUNK, GLA_H, GLA_DK, GLA_DV), F32)],
        scratch_shapes=[pltpu.VMEM((GLA_H, GLA_DK, GLA_DV), F32)], compiler_params=_cp(True))(q, k, g, v)


def _gla_bwd(name, q, k, g, v, ss, do, rev, G):
    Lp = q.shape[1]
    tg = G * CHUNK
    ng = Lp // tg

    def body(q_ref, k_ref, g_ref, v_ref, ss_ref, do_ref, dq_ref, dk_ref, dg_ref, dv_ref, ds_scr):
        @pl.when(pl.program_id(0) == 0)
        def _():
            ds_scr[...] = jnp.zeros_like(ds_scr)

        t = _gla_chunk_terms(q_ref, k_ref, g_ref, v_ref, G, rev)
        B, vv, tri = t["B"], t["vv"], t["tri"]
        qd, ki, ke = t["qd"], t["ki"], t["ke"]
        qdb, kib, keb = qd.astype(BF16), ki.astype(BF16), ke.astype(BF16)
        sp = jnp.stack([ss_ref[c] for c in range(G)], axis=1).reshape(B, GLA_DK, GLA_DV)
        dob = do_ref[...].reshape(B, CHUNK, GLA_DV).astype(BF16)
        da = jnp.where(tri[None], _bdot(dob, vv, (2,), (2,)), 0.0).astype(BF16)
        dqd = _bdot(da, kib, (2,), (1,)) + _bdot(dob, sp.astype(BF16), (2,), (2,))
        dki = _bdot(da, qdb, (1,), (1,))
        dv = _bdot(t["att"].astype(BF16), dob, (1,), (1,))
        cc = _bdot(qdb, dob, (1,), (1,)).reshape(GLA_H, G, GLA_DK, GLA_DV)
        dm = t["dm"].reshape(GLA_H, G, GLA_DK, GLA_DV)
        dsc = ds_scr[...]
        dsn = [None] * G
        for c in (range(G) if rev else range(G - 1, -1, -1)):
            dsn[c] = dsc
            dsc = dm[:, c] * dsc + cc[:, c]
        ds_scr[...] = dsc
        dsn = jnp.stack(dsn, axis=1).reshape(B, GLA_DK, GLA_DV)
        dsnb = dsn.astype(BF16)
        dv = dv + _bdot(keb, dsnb, (2,), (1,))
        dke = _bdot(vv, dsnb, (2,), (2,))
        ddrow = _bdot(jnp.ones((B, CHUNK, GLA_DV), F32), dsn * sp, (2,), (2,), HIGHEST)
        dbt = ddrow * jnp.exp(t["bt"]) + jnp.sum(dke * ke, axis=1, keepdims=True)
        db = dqd * qd - dki * ki - dke * ke
        dq_ref[...] = (dqd * t["eq"]).reshape(GLA_H, tg, GLA_DK)
        dk_ref[...] = (dki * t["eki"] + dke * t["eke"]).reshape(GLA_H, tg, GLA_DK)
        dg_ref[...] = (_bdot(t["tb"], db, (1,), (1,), HIGHEST) + dbt).reshape(GLA_H, tg, GLA_DK)
        dv_ref[...] = dv.reshape(GLA_H, tg, GLA_DV)

    blk = (lambda i: (0, i, 0)) if rev else (lambda i: (0, ng - 1 - i, 0))
    sblk = (lambda i: (i, 0, 0, 0)) if rev else (lambda i: (ng - 1 - i, 0, 0, 0))
    h64 = pl.BlockSpec((GLA_H, tg, 64), blk)
    h128 = pl.BlockSpec((GLA_H, tg, 128), blk)
    return pl.pallas_call(
        body, name=name, grid=(ng,),
        in_specs=[h64, h64, h64, h128, pl.BlockSpec((G, GLA_H, GLA_DK, GLA_DV), sblk), h128],
        out_specs=[h64, h64, h64, h128],
        out_shape=[SDS((GLA_H, Lp, 64), F32), SDS((GLA_H, Lp, 64), F32), SDS((GLA_H, Lp, 64), F32),
                   SDS((GLA_H, Lp, GLA_DV), F32)],
        scratch_shapes=[pltpu.VMEM((GLA_H, GLA_DK, GLA_DV), F32)], compiler_params=_cp(True))(q, k, g, v, ss, do)


def _gla_post(name, of, ob, H, gn):
    Lp = H.shape[0]
    tm = _rt(Lp, 384)

    def body(of_ref, ob_ref, r_ref, gn_ref, a_ref, at_ref):
        parts = []
        for hh in range(GLA_H):
            o = of_ref[hh] + ob_ref[hh]
            parts.append(o * lax.rsqrt(jnp.mean(o * o, axis=-1, keepdims=True) + EPS))
        rv = r_ref[...]
        a = (jnp.concatenate(parts, axis=1) * gn_ref[...]) * (rv * jax.nn.sigmoid(rv))
        a_ref[...] = a.astype(BF16)
        at_ref[...] = a.T.astype(BF16)

    h128 = pl.BlockSpec((GLA_H, tm, 128), lambda i: (0, i, 0))
    return pl.pallas_call(
        body, name=name, grid=(Lp // tm,),
        in_specs=[h128, h128, pl.BlockSpec((tm, 512), lambda i: (i, C_RA // 512)), pl.BlockSpec((1, 512), lambda i: (0, 0))],
        out_specs=[pl.BlockSpec((tm, 512), lambda i: (i, 0)), pl.BlockSpec((512, tm), lambda i: (0, i))],
        out_shape=[SDS((Lp, 512), BF16), SDS((512, Lp), BF16)])(of, ob, H, gn)


def _gla_post_bwd(name, of, ob, H, gn, da, dH):
    Lp = H.shape[0]
    tm = _rt(Lp, 384)

    def body(of_ref, ob_ref, r_ref, gn_ref, da_ref, dh_in, do_ref, dh_ref, dgn_ref):
        rv = r_ref[...]
        sg = jax.nn.sigmoid(rv)
        dav = da_ref[...]
        gnv = gn_ref[...]
        ons, rs = [], []
        for hh in range(GLA_H):
            o = of_ref[hh] + ob_ref[hh]
            r = lax.rsqrt(jnp.mean(o * o, axis=-1, keepdims=True) + EPS)
            rs.append(r)
            ons.append(o * r)
        on = jnp.concatenate(ons, axis=1)
        dw = dav * (rv * sg)
        dh_ref[...] = (dav * (on * gnv) * (sg * (1.0 + rv * (1.0 - sg)))).astype(BF16)

        @pl.when(pl.program_id(0) == 0)
        def _():
            dgn_ref[...] = jnp.zeros_like(dgn_ref)

        dgn_ref[...] += jnp.sum(dw * on, axis=0, keepdims=True)
        don = dw * gnv
        for hh in range(GLA_H):
            dd = don[:, 128 * hh:128 * hh + 128]
            do_ref[hh] = rs[hh] * (dd - ons[hh] * jnp.mean(dd * ons[hh], axis=-1, keepdims=True))

    h128 = pl.BlockSpec((GLA_H, tm, 128), lambda i: (0, i, 0))
    rblk = pl.BlockSpec((tm, 512), lambda i: (i, C_RA // 512))
    vec = pl.BlockSpec((1, 512), lambda i: (0, 0))
    return pl.pallas_call(
        body, name=name, grid=(Lp // tm,),
        in_specs=[h128, h128, rblk, vec, pl.BlockSpec((tm, 512), lambda i: (i, 0)), pl.BlockSpec(memory_space=pl.ANY)],
        out_specs=[h128, rblk, vec],
        out_shape=[SDS((GLA_H, Lp, 128), F32), SDS(dH.shape, BF16), SDS((1, 512), F32)],
        input_output_aliases={5: 1})(of, ob, H, gn, da, dH)


def _gla_qkv_bwd(name, dqf, dqb, dkf, dkb, dvf, dvb, dH):
    Lp = dqf.shape[1]
    tm = _rt(Lp, 384)

    def body(dqf_ref, dqb_ref, dkf_ref, dkb_ref, dvf_ref, dvb_ref, dh_in, dh_ref):
        valid = _row_ids(pl.program_id(0), tm) >= PAD
        for hh in range(GLA_H):
            dq = (dqf_ref[hh] + dqb_ref[hh]) * (GLA_DK ** -0.5)
            dh_ref[:, 64 * hh:64 * hh + 64] = jnp.where(valid, dq, 0.0).astype(BF16)
            dh_ref[:, 256 + 64 * hh:256 + 64 * hh + 64] = jnp.where(valid, dkf_ref[hh] + dkb_ref[hh], 0.0).astype(BF16)
            dh_ref[:, 512 + 128 * hh:512 + 128 * hh + 128] = jnp.where(valid, dvf_ref[hh] + dvb_ref[hh], 0.0).astype(BF16)

    h64 = pl.BlockSpec((GLA_H, tm, 64), lambda i: (0, i, 0))
    h128 = pl.BlockSpec((GLA_H, tm, 128), lambda i: (0, i, 0))
    return pl.pallas_call(
        body, name=name, grid=(Lp // tm,),
        in_specs=[h64, h64, h64, h64, h128, h128, pl.BlockSpec(memory_space=pl.ANY)],
        out_specs=pl.BlockSpec((tm, 1024), lambda i: (i, C_QA // 1024)), out_shape=SDS(dH.shape, BF16),
        input_output_aliases={6: 0})(dqf, dqb, dkf, dkb, dvf, dvb, dH)


def _gla_gate_bwd(name, H, w2p, b2p, dgf, dgb, dH):
    Lp = H.shape[0]
    tm = _rt(Lp, 384)

    def body(lr_ref, w_ref, b_ref, dgf_ref, dgb_ref, dh_in, dh_ref, dw_ref, db_ref, dg_scr):
        valid = _row_ids(pl.program_id(0), tm) >= PAD
        for hh in range(GLA_H):
            dg_scr[:, 64 * hh:64 * hh + 64] = dgf_ref[hh]
            dg_scr[:, 256 + 64 * hh:256 + 64 * hh + 64] = dgb_ref[hh]
        lrb = lr_ref[...].astype(BF16)
        wv = w_ref[...]
        pre = jnp.dot(lrb, wv, preferred_element_type=F32) + b_ref[...]
        dpre = jnp.where(valid, dg_scr[...] * (1.0 / GLA_TAU) * jax.nn.sigmoid(-pre), 0.0)
        dpb = dpre.astype(BF16)
        dh_ref[...] = lax.dot_general(dpb, wv, NT, preferred_element_type=F32).astype(BF16)

        @pl.when(pl.program_id(0) == 0)
        def _():
            dw_ref[...] = jnp.zeros_like(dw_ref)
            db_ref[...] = jnp.zeros_like(db_ref)

        dw_ref[...] += lax.dot_general(lrb, dpb, TN, preferred_element_type=F32)
        db_ref[...] += jnp.sum(dpre, axis=0, keepdims=True)

    h64 = pl.BlockSpec((GLA_H, tm, 64), lambda i: (0, i, 0))
    lrblk = pl.BlockSpec((tm, LANE), lambda i: (i, C_LR // LANE))
    wblk = pl.BlockSpec((LANE, 512), lambda i: (0, 0))
    vec = pl.BlockSpec((1, 512), lambda i: (0, 0))
    return pl.pallas_call(
        body, name=name, grid=(Lp // tm,),
        in_specs=[lrblk, wblk, vec, h64, h64, pl.BlockSpec(memory_space=pl.ANY)],
        out_specs=[lrblk, wblk, vec],
        out_shape=[SDS(dH.shape, BF16), SDS((LANE, 512), F32), SDS((1, 512), F32)],
        scratch_shapes=[pltpu.VMEM((tm, 512), F32)], input_output_aliases={5: 0})(H, w2p, b2p, dgf, dgb, dH)


def _swap16(x):
    n = x.shape[1]
    lane = lax.broadcasted_iota(jnp.int32, x.shape, 1)
    return jnp.where(lane % 32 < 16, pltpu.roll(x, n - 16, 1), pltpu.roll(x, 16, 1))


def _head_mean(v, bd):
    hi = v.astype(BF16)
    lo = (v - hi.astype(F32)).astype(BF16)
    bdb = bd.astype(BF16)
    return jnp.dot(hi, bdb, preferred_element_type=F32) + jnp.dot(lo, bdb, preferred_element_type=F32)


def _headnorm_rope(x, gain, cos, sin, bd):
    r = lax.rsqrt(_head_mean(x * x, bd) + EPS)
    xh = x * r
    xn = xh * gain
    return xn * cos + _swap16(xn) * sin, xh, r


def _headnorm_rope_bwd(dxr, xh, r, gain, cos, sin, bd):
    dxn = cos * dxr + _swap16(sin * dxr)
    u = dxn * gain
    dx = r * (u - xh * _head_mean(u * xh, bd))
    return dx, jnp.sum(dxn * xh, axis=0, keepdims=True)


def _att_prep(name, H, gq, gk, cos, sin, bd):
    Lp = H.shape[0]
    tm = _rt(Lp, 384)

    def body(q_ref, kv_ref, gq_ref, gk_ref, c_ref, s_ref, bd_ref, qp_ref, k_ref, v_ref):
        c1, s1 = c_ref[...], s_ref[...]
        c4, s4 = jnp.concatenate([c1] * 4, axis=1), jnp.concatenate([s1] * 4, axis=1)
        xr, _, _ = _headnorm_rope(q_ref[...], gq_ref[...], c4, s4, bd_ref[...])
        xr = xr * (HD ** -0.5)
        lane = lax.broadcasted_iota(jnp.int32, (tm, LANE), 1)
        for hh in range(ATT_H):
            grp = xr[:, LANE * (hh // 2):LANE * (hh // 2) + LANE]
            e, gi = hh % 2, hh // 4
            if e != gi:
                grp = pltpu.roll(grp, 64, 1)
            keep = (lane < 64) if gi == 0 else (lane >= 64)
            qp_ref[hh] = jnp.where(keep, grp, 0.0).astype(BF16)
        kv = kv_ref[...]
        kr, _, _ = _headnorm_rope(kv[:, :LANE], gk_ref[...], c1, s1, bd_ref[0:LANE, 0:LANE])
        k_ref[...] = kr.astype(BF16)
        v_ref[...] = kv[:, LANE:].astype(BF16)

    row128 = pl.BlockSpec((tm, LANE), lambda i: (i, 0))
    return pl.pallas_call(
        body, name=name, grid=(Lp // tm,),
        in_specs=[pl.BlockSpec((tm, 512), lambda i: (i, C_QB // 512)), pl.BlockSpec((tm, 256), lambda i: (i, C_KB // 256)),
                  pl.BlockSpec((1, 512), lambda i: (0, 0)), pl.BlockSpec((1, LANE), lambda i: (0, 0)), row128, row128,
                  pl.BlockSpec((512, 512), lambda i: (0, 0))],
        out_specs=[pl.BlockSpec((ATT_H, tm, LANE), lambda i: (0, i, 0)), row128, row128],
        out_shape=[SDS((ATT_H, Lp, LANE), BF16), SDS((Lp, LANE), BF16), SDS((Lp, LANE), BF16)])(H, H, gq, gk, cos, sin, bd)


def _att_q_bwd(name, H, gq, cos, sin, bd, dqp, dH):
    Lp = H.shape[0]
    tm = _rt(Lp, 384)

    def body(q_ref, gq_ref, c_ref, s_ref, bd_ref, dqp_ref, dh_in, dh_ref, dg_ref):
        c1, s1 = c_ref[...], s_ref[...]
        c4, s4 = jnp.concatenate([c1] * 4, axis=1), jnp.concatenate([s1] * 4, axis=1)
        gqv = gq_ref[...]
        _, xh, r = _headnorm_rope(q_ref[...], gqv, c4, s4, bd_ref[...])
        lane = lax.broadcasted_iota(jnp.int32, (tm, LANE), 1)
        groups = []
        for j in range(ATT_H // 2):
            pieces = []
            for e in range(2):
                hh = 2 * j + e
                piece = dqp_ref[hh]
                if e != hh // 4:
                    piece = pltpu.roll(piece, 64, 1)
                pieces.append(piece)
            groups.append(jnp.where(lane < 64, pieces[0], pieces[1]))
        dxr = jnp.concatenate(groups, axis=1) * (HD ** -0.5)
        dx, dg = _headnorm_rope_bwd(dxr, xh, r, gqv, c4, s4, bd_ref[...])
        dh_ref[...] = dx.astype(BF16)

        @pl.when(pl.program_id(0) == 0)
        def _():
            dg_ref[...] = jnp.zeros_like(dg_ref)

        dg_ref[...] += dg

    row128 = pl.BlockSpec((tm, LANE), lambda i: (i, 0))
    qblk = pl.BlockSpec((tm, 512), lambda i: (i, C_QB // 512))
    vec = pl.BlockSpec((1, 512), lambda i: (0, 0))
    return pl.pallas_call(
        body, name=name, grid=(Lp // tm,),
        in_specs=[qblk, vec, row128, row128, pl.BlockSpec((512, 512), lambda i: (0, 0)),
                  pl.BlockSpec((ATT_H, tm, LANE), lambda i: (0, i, 0)), pl.BlockSpec(memory_space=pl.ANY)],
        out_specs=[qblk, vec], out_shape=[SDS(dH.shape, BF16), SDS((1, 512), F32)],
        input_output_aliases={6: 0})(H, gq, cos, sin, bd, dqp, dH)


def _att_kv_bwd(name, H, gk, cos, sin, bd, dkr, dvb, dH):
    Lp = H.shape[0]
    tm = _rt(Lp, 384)

    def body(kv_ref, gk_ref, c_ref, s_ref, bd_ref, dk_ref, dv_ref, dh_in, dh_ref, dg_ref):
        c1, s1 = c_ref[...], s_ref[...]
        gkv = gk_ref[...]
        bdv = bd_ref[0:LANE, 0:LANE]
        _, xh, r = _headnorm_rope(kv_ref[:, :LANE], gkv, c1, s1, bdv)
        dx, dg = _headnorm_rope_bwd(dk_ref[...], xh, r, gkv, c1, s1, bdv)
        dh_ref[:, :LANE] = dx.astype(BF16)
        dh_ref[:, LANE:] = dv_ref[...].astype(BF16)

        @pl.when(pl.program_id(0) == 0)
        def _():
            dg_ref[...] = jnp.zeros_like(dg_ref)

        dg_ref[...] += dg

    row128 = pl.BlockSpec((tm, LANE), lambda i: (i, 0))
    kvblk = pl.BlockSpec((tm, 256), lambda i: (i, C_KB // 256))
    vec = pl.BlockSpec((1, LANE), lambda i: (0, 0))
    return pl.pallas_call(
        body, name=name, grid=(Lp // tm,),
        in_specs=[kvblk, vec, row128, row128, pl.BlockSpec((512, 512), lambda i: (0, 0)), row128, row128,
                  pl.BlockSpec(memory_space=pl.ANY)],
        out_specs=[kvblk, vec], out_shape=[SDS(dH.shape, BF16), SDS((1, LANE), F32)],
        input_output_aliases={7: 0})(H, gk, cos, sin, bd, dkr, dvb, dH)


def _ridden_call(core, rider, *, name, grid, in_specs, out_specs, out_shape, scratch_shapes, args, aliases=None,
                 mid_back=RIDER_MID_BACK):
    n_in, n_out, n_scr = len(in_specs), len(out_specs), len(scratch_shapes)
    r_in = len(rider.ins) if rider else 0
    r_out = len(rider.out_shape) if rider else 0
    total = int(np.prod(grid))
    mid_step = max(total - 1 - mid_back, 0)

    def body(*refs):
        ins, r_ins = refs[:n_in], refs[n_in:n_in + r_in]
        o0 = n_in + r_in
        outs, r_outs = refs[o0:o0 + n_out], refs[o0 + n_out:o0 + n_out + r_out]
        s0 = o0 + n_out + r_out
        scr, r_sems = refs[s0:s0 + n_scr], refs[s0 + n_scr:]
        if rider is None:
            core(ins, outs, scr)
            return
        step = pl.program_id(0)
        for ax in range(1, len(grid)):
            step = step * grid[ax] + pl.program_id(ax)
        start, mid, end = rider.hooks(r_ins, r_outs, r_sems)
        pl.when(step == 0)(start)
        core(ins, outs, scr)
        pl.when(step == mid_step)(mid)
        pl.when(step == total - 1)(end)

    any_spec = pl.BlockSpec(memory_space=pl.ANY)
    res = pl.pallas_call(
        body, name=name, grid=grid, in_specs=list(in_specs) + [any_spec] * r_in, out_specs=list(out_specs) + [any_spec] * r_out,
        out_shape=list(out_shape) + (list(rider.out_shape) if rider else []),
        scratch_shapes=list(scratch_shapes) + (list(rider.sems) if rider else []),
        input_output_aliases={**(aliases or {}), **({n_in + a: n_out + b for a, b in rider.aliases.items()} if rider else {})},
        compiler_params=_cp(True))(*args, *(rider.ins if rider else []))
    return res[:n_out], res[n_out:]


def _att_fwd(name, qp, kr, vb, bias, rider=None):
    Lp = kr.shape[0]
    tq = _rt(Lp, 384)
    sub = ATT_SUB_FWD if tq % ATT_SUB_FWD == 0 else tq

    def core(ins, outs, scr):
        q_ref, k_ref, v_ref, bias_ref = ins
        b_ref, bt_ref, lse_ref = outs
        j, i = pl.program_id(0), pl.program_id(1)
        valid = _row_ids(i, tq) >= PAD
        kk, vv, bb = k_ref[...], v_ref[...], bias_ref[...]
        chains = [(e, r) for e in range(2) for r in range(tq // sub)]
        ss = [lax.dot_general(q_ref[e, r * sub:(r + 1) * sub, :], kk, NT, preferred_element_type=F32) for e, r in chains]
        ps, ls = [], []
        for (e, r), s in zip(chains, ss):
            s = jnp.concatenate([s[:, :LANE] + bb[:, :LANE], s[:, LANE:]], axis=1)
            m = jnp.max(s, axis=-1, keepdims=True)
            p = jnp.exp(s - m)
            l = jnp.sum(p, axis=-1, keepdims=True)
            ps.append(p.astype(BF16))
            ls.append(l)
            lse_ref[e, r * sub:(r + 1) * sub, :] = m + jnp.log(l)
        os_ = [jnp.dot(p, vv, preferred_element_type=F32) * (1.0 / l) for p, l in zip(ps, ls)]
        n_sub = tq // sub
        outs = [jnp.where(valid, jnp.concatenate(os_[e * n_sub:(e + 1) * n_sub], axis=0), 0.0) for e in range(2)]
        lane = lax.broadcasted_iota(jnp.int32, (tq, LANE), 1)
        low = j < 2
        o0 = jnp.where(low, outs[0], pltpu.roll(outs[0], 64, 1))
        o1 = jnp.where(low, pltpu.roll(outs[1], 64, 1), outs[1])
        blk = jnp.where(lane < 64, o0, o1)
        b_ref[...] = blk.astype(BF16)
        bt_ref[...] = blk.T.astype(BF16)

    full = pl.BlockSpec((Lp, LANE), lambda j, i: (0, 0))
    return _ridden_call(
        core, rider, name=name, grid=(ATT_H // 2, Lp // tq),
        in_specs=[pl.BlockSpec((2, tq, LANE), lambda j, i: (j, i, 0)), full, full, pl.BlockSpec((1, Lp), lambda j, i: (0, 0))],
        out_specs=[pl.BlockSpec((tq, LANE), lambda j, i: (i, j)), pl.BlockSpec((LANE, tq), lambda j, i: (j, i)),
                   pl.BlockSpec((2, tq, 1), lambda j, i: (j, i, 0))],
        out_shape=[SDS((Lp, 512), BF16), SDS((512, Lp), BF16), SDS((ATT_H, Lp, 1), F32)],
        scratch_shapes=[], args=(qp, kr, vb, bias))


def _att_bwd(name, qp, kr, vb, bias, lse, db, b, rider=None):
    Lp = kr.shape[0]
    tq = _rt(Lp, 384)
    sub = ATT_SUB_BWD if tq % ATT_SUB_BWD == 0 else tq
    nq = Lp // tq

    def core(ins, outs, scr):
        q_ref, k_ref, v_ref, bias_ref, lse_ref, db_ref, b_ref = ins
        dq_ref, dk_ref, dv_ref = outs
        dkt_scr, dvt_scr = scr
        j, i = pl.program_id(0), pl.program_id(1)

        @pl.when((j == 0) & (i == 0))
        def _():
            dkt_scr[...] = jnp.zeros_like(dkt_scr)
            dvt_scr[...] = jnp.zeros_like(dvt_scr)

        kk, vv, bb = k_ref[...], v_ref[...], bias_ref[...]
        dbv = db_ref[...]
        rolled = pltpu.roll(dbv, 64, 1)
        lane = lax.broadcasted_iota(jnp.int32, (tq, LANE), 1)
        low = j < 2
        first = jnp.where(low, 0, 64)
        keep = (lane >= first) & (lane < first + 64)
        prod = dbv * b_ref[...].astype(F32)
        deltas = [jnp.sum(jnp.where(lane < 64, prod, 0.0), axis=-1, keepdims=True),
                  jnp.sum(jnp.where(lane >= 64, prod, 0.0), axis=-1, keepdims=True)]
        dkt, dvt = 0.0, 0.0
        for e in range(2):
            src = jnp.where(low, dbv, rolled) if e == 0 else jnp.where(low, rolled, dbv)
            dop = jnp.where(keep, src, 0.0).astype(BF16)
            pbs, dss = [], []
            for r in range(tq // sub):
                rs = slice(r * sub, (r + 1) * sub)
                s = lax.dot_general(q_ref[e, rs, :], kk, NT, preferred_element_type=F32)
                s = jnp.concatenate([s[:, :LANE] + bb[:, :LANE], s[:, LANE:]], axis=1)
                p = jnp.exp(s - lse_ref[e, rs, :])
                dp = lax.dot_general(dop[rs], vv, NT, preferred_element_type=F32)
                ds = (p * (dp - deltas[e][rs])).astype(BF16)
                dq_ref[e, rs, :] = jnp.where(keep[rs], jnp.dot(ds, kk, preferred_element_type=F32), 0.0)
                pbs.append(p.astype(BF16))
                dss.append(ds)
            dkt = dkt + lax.dot_general(q_ref[e], jnp.concatenate(dss, axis=0), TN, preferred_element_type=F32)
            dvt = dvt + lax.dot_general(dop, jnp.concatenate(pbs, axis=0), TN, preferred_element_type=F32)
        dkt_scr[...] += dkt
        dvt_scr[...] += dvt

        @pl.when((j == ATT_H // 2 - 1) & (i == nq - 1))
        def _():
            dk_ref[...] = dkt_scr[...].T
            dv_ref[...] = dvt_scr[...].T

    full = pl.BlockSpec((Lp, LANE), lambda j, i: (0, 0))
    pair = pl.BlockSpec((2, tq, LANE), lambda j, i: (j, i, 0))
    return _ridden_call(
        core, rider, name=name, grid=(ATT_H // 2, nq),
        in_specs=[pair, full, full, pl.BlockSpec((1, Lp), lambda j, i: (0, 0)), pl.BlockSpec((2, tq, 1), lambda j, i: (j, i, 0)),
                  pl.BlockSpec((tq, LANE), lambda j, i: (i, j)), pl.BlockSpec((tq, LANE), lambda j, i: (i, j))],
        out_specs=[pair, full, full],
        out_shape=[SDS((ATT_H, Lp, LANE), F32), SDS((Lp, LANE), F32), SDS((Lp, LANE), F32)],
        scratch_shapes=[pltpu.VMEM((LANE, Lp), F32), pltpu.VMEM((LANE, Lp), F32)], args=(qp, kr, vb, bias, lse, db, b))


def _rope_tables(n_tok):
    f32 = np.float32
    rows = n_tok // GRID_W
    row = np.repeat(np.arange(rows), GRID_W).astype(f32)
    col = np.tile(np.arange(GRID_W), rows).astype(f32)
    axis_dim = HD // 2
    inv = np.power(f32(ROPE_THETA), -np.arange(0, axis_dim, 2, dtype=f32) / f32(axis_dim)).astype(f32)
    ang = np.concatenate([row[:, None] * inv, col[:, None] * inv], axis=-1).astype(f32)
    ang = np.concatenate([np.zeros((LANE, axis_dim), f32), ang], axis=0)
    c, s = np.cos(ang).astype(f32), np.sin(ang).astype(f32)
    c64 = np.concatenate([c[:, :16], c[:, :16], c[:, 16:], c[:, 16:]], axis=1)
    s64 = np.concatenate([-s[:, :16], s[:, :16], -s[:, 16:], s[:, 16:]], axis=1)
    return jnp.asarray(np.concatenate([c64, c64], axis=1)), jnp.asarray(np.concatenate([s64, s64], axis=1))


def _ffn_fwd(tag, h, W, l, j, rider=None):
    Lp = h.shape[0]
    tm = _rt(Lp, 384)
    g = W["norm_gains"][l, 2 * j].reshape(1, D)

    def core(ins, outs, scr):
        h_ref, g_ref, wg_ref, wu_ref, wd_ref = ins
        h2_ref, a_ref, b_ref, st_ref, nt_ref = outs
        pad_scr, = scr
        x = h_ref[...]
        y = x * lax.rsqrt(jnp.mean(x * x, axis=-1, keepdims=True) + EPS) * g_ref[...]
        nt_ref[...] = y.T.astype(BF16)
        nv = y.astype(BF16)
        pad_scr[:, FF_S:] = jnp.zeros((tm, FF_P - FF_S), F32)
        acc = 0.0
        for s in range(NSH):
            a = jnp.dot(nv, wg_ref[s], preferred_element_type=F32)
            b = jnp.dot(nv, wu_ref[s], preferred_element_type=F32)
            sv = a * jax.nn.sigmoid(a) * b
            a_ref[s, :, :FF_S] = a.astype(BF16)
            a_ref[s, :, FF_S:] = jnp.zeros((tm, FF_P - FF_S), BF16)
            b_ref[s, :, :FF_S] = b.astype(BF16)
            b_ref[s, :, FF_S:] = jnp.zeros((tm, FF_P - FF_S), BF16)
            pad_scr[:, :FF_S] = sv
            st_ref[s] = pad_scr[...].T.astype(BF16)
            acc = acc + jnp.dot(sv.astype(BF16), wd_ref[s], preferred_element_type=F32)
        h2_ref[...] = x + 0.5 * acc

    once = dict(pipeline_mode=pl.Buffered(1))
    wup = pl.BlockSpec((NSH, None, D, FF_S), lambda i: (0, l, j, 0), **once)
    row = pl.BlockSpec((tm, D), lambda i: (i, 0))
    slab = pl.BlockSpec((NSH, tm, FF_P), lambda i: (0, i, 0))
    (h2, a, b, st, nt), carried = _ridden_call(
        core, rider, name=f"ffn_{tag}", grid=(Lp // tm,),
        in_specs=[row, pl.BlockSpec((1, D), lambda i: (0, 0)), wup, wup,
                  pl.BlockSpec((NSH, None, FF_S, D), lambda i: (0, l, j, 0), **once)],
        out_specs=[row, slab, slab, pl.BlockSpec((NSH, FF_P, tm), lambda i: (0, 0, i)), pl.BlockSpec((D, tm), lambda i: (0, i))],
        out_shape=[SDS((Lp, D), F32), SDS((NSH, Lp, FF_P), BF16), SDS((NSH, Lp, FF_P), BF16), SDS((NSH, FF_P, Lp), BF16),
                   SDS((D, Lp), BF16)],
        scratch_shapes=[pltpu.VMEM((tm, FF_P), F32)], args=(h, g, W["gate"], W["up"], W["down"]), mid_back=1)
    if rider is not None:
        W = rider.updated(W, carried)
    return h2, dict(h=h, g=g, nt=nt, a=a, b=b, st=st), W


def _ffn_bwd(tag, dh, sv, W, G, l, j, rider=None):
    Lp = dh.shape[0]
    tn = 512
    (da, db, dh_prev, dg), carried = _ffn_bwd_acts(tag, dh, sv, W, l, j, rider)
    G["down"] = _mm(f"bwd_{tag}", sv["st"], dh, grid=(NSH, D // tn),
                    a_spec=pl.BlockSpec((None, FF_P, Lp), lambda s, n: (s, 0, 0)), a_sl=(slice(0, FF_S), slice(None)),
                    b_spec=pl.BlockSpec((Lp, tn), lambda s, n: (0, n)),
                    o_spec=pl.BlockSpec((None, None, FF_S, tn), lambda s, n: (s, j, 0, n)), o_shape=(NSH, 2, FF_S, D), o_dtype=F32,
                    dims=NN, acc_shape=None, scale=0.5, into=G.get("down"))
    for key, dact in (("gate", da), ("up", db)):
        G[key] = _mm(f"bw{key[0]}_{tag}", sv["nt"], dact, grid=(NSH, D // tn),
                     a_spec=pl.BlockSpec((tn, Lp), lambda s, m: (m, 0)),
                     b_spec=pl.BlockSpec((None, Lp, FF_P), lambda s, m: (s, 0, 0)), b_sl=(slice(None), slice(0, FF_S)),
                     o_spec=pl.BlockSpec((None, None, tn, FF_S), lambda s, m: (s, j, m, 0)), o_shape=(NSH, 2, D, FF_S),
                     o_dtype=F32, dims=NN, acc_shape=None, into=G.get(key))
    return dh_prev, dg, carried


def _ffn_bwd_acts(tag, dh, sv, W, l, j, rider=None):
    Lp = dh.shape[0]
    tm = _rt(Lp, 384)

    def core(ins, outs, scr):
        dh_ref, h_ref, g_ref, wd_ref, wg_ref, wu_ref, a_ref, b_ref = ins
        da_ref, db_ref, dho_ref, dg_ref = outs
        dhv = dh_ref[...]
        dhb = dhv.astype(BF16)
        dn = 0.0
        for s in range(NSH):
            ds = 0.5 * lax.dot_general(dhb, wd_ref[s], NT, preferred_element_type=F32)
            av = a_ref[s, :, :FF_S].astype(F32)
            bv = b_ref[s, :, :FF_S].astype(F32)
            sg = jax.nn.sigmoid(av)
            da = (ds * bv * (sg * (1.0 + av * (1.0 - sg)))).astype(BF16)
            db = (ds * (av * sg)).astype(BF16)
            da_ref[s, :, :FF_S] = da
            da_ref[s, :, FF_S:] = jnp.zeros((tm, FF_P - FF_S), BF16)
            db_ref[s, :, :FF_S] = db
            db_ref[s, :, FF_S:] = jnp.zeros((tm, FF_P - FF_S), BF16)
            dn = dn + (lax.dot_general(da, wg_ref[s], NT, preferred_element_type=F32)
                       + lax.dot_general(db, wu_ref[s], NT, preferred_element_type=F32))
        x = h_ref[...]
        r = lax.rsqrt(jnp.mean(x * x, axis=-1, keepdims=True) + EPS)
        xh = x * r
        u = dn * g_ref[...]
        dho_ref[...] = dhv + r * (u - xh * jnp.mean(u * xh, axis=-1, keepdims=True))

        @pl.when(pl.program_id(0) == 0)
        def _():
            dg_ref[...] = jnp.zeros_like(dg_ref)

        dg_ref[...] += jnp.sum(dn * xh, axis=0, keepdims=True)

    once = dict(pipeline_mode=pl.Buffered(1))
    wup = pl.BlockSpec((NSH, None, D, FF_S), lambda i: (0, l, j, 0), **once)
    row = pl.BlockSpec((tm, D), lambda i: (i, 0))
    slab = pl.BlockSpec((NSH, tm, FF_P), lambda i: (0, i, 0))
    vec = pl.BlockSpec((1, D), lambda i: (0, 0))
    return _ridden_call(
        core, rider, name=f"bffn_{tag}", grid=(Lp // tm,),
        in_specs=[row, row, vec, pl.BlockSpec((NSH, None, FF_S, D), lambda i: (0, l, j, 0), **once), wup, wup, slab, slab],
        out_specs=[slab, slab, row, vec],
        out_shape=[SDS((NSH, Lp, FF_P), BF16), SDS((NSH, Lp, FF_P), BF16), SDS((Lp, D), F32), SDS((1, D), F32)],
        scratch_shapes=[], args=(dh, sv["h"], sv["g"], W["down"], W["gate"], W["up"], sv["a"], sv["b"]))


def _mixer_fwd(tag, h, W, winp, C, l, G_chunks, riders=(None, None)):
    Lp = h.shape[0]
    tm = _rt(Lp, 1408)
    g = W["norm_gains"][l, 1].reshape(1, D)
    (H, zt), carried = _norm_project(f"win_{tag}", h, g, winp, riders[0])
    if riders[0] is not None:
        W = riders[0].updated(W, carried)
    rider = riders[1]
    w2p, b2p = C["w2p"][l], C["b2p"][l]
    qh, kh, vh, gf, gb = _gla_prep(f"glap_{tag}", H, w2p, b2p)
    of, sf = _gla_fwd(f"glaf_{tag}", qh, kh, gf, vh, False, G_chunks)
    ob, sb = _gla_fwd(f"glar_{tag}", qh, kh, gb, vh, True, G_chunks)
    gn = W["gn"][l].reshape(1, 512)
    a, at = _gla_post(f"glao_{tag}", of, ob, H, gn)
    gq, gk = C["gq"][l], C["gk"][l]
    qp, kr, vb = _att_prep(f"attp_{tag}", H, gq, gk, C["cos"], C["sin"], C["bd"])
    (b, bt, lse), carried = _att_fwd(f"attf_{tag}", qp, kr, vb, C["bias"], rider)
    if rider is not None:
        W = rider.updated(W, carried)
    row = pl.BlockSpec((tm, D), lambda i: (i, 0))
    proj = functools.partial(
        _mm, grid=(Lp // tm,), a_spec=pl.BlockSpec((tm, 512), lambda i: (i, 0)),
        b_spec=pl.BlockSpec((NSH, None, 512, 256), lambda i: (0, l, 0, 0)), o_spec=row, o_shape=(Lp, D), o_dtype=F32, dims=NN,
        acc_shape=None, shards="cols")
    pa = proj(f"pa_{tag}", a, W["wpa"])
    pb = proj(f"pb_{tag}", b, W["wpb"])
    bm = W["bm"][l]
    y, yt = _merge_fwd(f"mrg_{tag}", H, pa, pb, bm)
    h2 = _mm(f"wout_{tag}", y, W["wout"], grid=(Lp // tm,), a_spec=row,
             b_spec=pl.BlockSpec((NSH, None, 256, D), lambda i: (0, l, 0, 0)), o_spec=row, o_shape=(Lp, D), o_dtype=F32,
             dims=NN, acc_shape=None, res=h, res_spec=row, shards="rows")
    sv = dict(h=h, g=g, zt=zt, H=H, w2p=w2p, b2p=b2p, qh=qh, kh=kh, vh=vh, gf=gf, gb=gb, of=of, ob=ob, sf=sf, sb=sb, gn=gn,
              at=at, gq=gq, gk=gk, qp=qp, kr=kr, vb=vb, b=b, bt=bt, lse=lse, pa=pa, pb=pb, bm=bm, yt=yt, winp=winp)
    return h2, sv, W


def _mixer_bwd(tag, dh, sv, W, C, G, S, l, G_chunks, rider=None):
    Lp = dh.shape[0]
    tm = _rt(Lp, 1408)
    H = sv["H"]
    row = pl.BlockSpec((tm, D), lambda i: (i, 0))
    dy = _mm(f"bdy_{tag}", dh, W["wout"], grid=(Lp // tm,), a_spec=row,
             b_spec=pl.BlockSpec((NSH, None, 256, D), lambda i: (0, l, 0, 0)), o_spec=row, o_shape=(Lp, D), o_dtype=F32, dims=NT,
             acc_shape=None, shards="rows")
    tn = 512
    G["wout"] = _mm(f"bwo_{tag}", sv["yt"], dh, grid=(NSH, D // tn), a_spec=pl.BlockSpec((256, Lp), lambda s, n: (s, 0)),
                    b_spec=pl.BlockSpec((Lp, tn), lambda s, n: (0, n)),
                    o_spec=pl.BlockSpec((None, 256, tn), lambda s, n: (s, 0, n)), o_shape=(NSH, 256, D), o_dtype=F32, dims=NN,
                    acc_shape=None)
    dpa, dpb, dH, S["bm"][l] = _merge_bwd(f"bmrg_{tag}", H, sv["pa"], sv["pb"], sv["bm"], dy)
    dbranch = {}
    for key, dp, xt in (("wpa", dpa, sv["at"]), ("wpb", dpb, sv["bt"])):
        dbranch[key] = _mm(f"bx{key[2]}_{tag}", dp, W[key], grid=(Lp // tm,), a_spec=row,
                           b_spec=pl.BlockSpec((NSH, None, 512, 256), lambda i: (0, l, 0, 0)),
                           o_spec=pl.BlockSpec((tm, 512), lambda i: (i, 0)), o_shape=(Lp, 512), o_dtype=F32, dims=NT,
                           acc_shape=None, shards="cols")
        G[key] = _mm(f"bw{key[2]}_{tag}", xt, dp, grid=(NSH,), a_spec=pl.BlockSpec((512, Lp), lambda s: (0, 0)),
                     b_spec=pl.BlockSpec((Lp, 256), lambda s: (0, s)),
                     o_spec=pl.BlockSpec((None, 512, 256), lambda s: (s, 0, 0)), o_shape=(NSH, 512, 256), o_dtype=F32, dims=NN,
                     acc_shape=None)
    (dqp, dkr, dvb), carried = _att_bwd(f"attb_{tag}", sv["qp"], sv["kr"], sv["vb"], C["bias"], sv["lse"], dbranch["wpb"],
                                        sv["b"], rider)
    dH, S["gq"][l] = _att_q_bwd(f"attq_{tag}", H, sv["gq"], C["cos"], C["sin"], C["bd"], dqp, dH)
    dH, S["gk"][l] = _att_kv_bwd(f"attk_{tag}", H, sv["gk"], C["cos"], C["sin"], C["bd"], dkr, dvb, dH)
    do, dH, S["gn"][l] = _gla_post_bwd(f"bglo_{tag}", sv["of"], sv["ob"], H, sv["gn"], dbranch["wpa"], dH)
    dqf, dkf, dgf, dvf = _gla_bwd(f"bglf_{tag}", sv["qh"], sv["kh"], sv["gf"], sv["vh"], sv["sf"], do, False, G_chunks)
    dqb, dkb, dgb, dvr = _gla_bwd(f"bglr_{tag}", sv["qh"], sv["kh"], sv["gb"], sv["vh"], sv["sb"], do, True, G_chunks)
    dH = _gla_qkv_bwd(f"bglq_{tag}", dqf, dqb, dkf, dkb, dvf, dvr, dH)
    dH, S["w2p"][l], S["b2p"][l] = _gla_gate_bwd(f"bglg_{tag}", H, sv["w2p"], sv["b2p"], dgf, dgb, dH)
    tmm = 256
    G["winp"] = _mm(f"bwi_{tag}", sv["zt"], dH, grid=(HP // W_CHUNK, D // tmm), a_spec=pl.BlockSpec((tmm, Lp), lambda n, m: (m, 0)),
                    b_spec=pl.BlockSpec((Lp, W_CHUNK), lambda n, m: (0, n)), o_spec=pl.BlockSpec((tmm, W_CHUNK), lambda n, m: (m, n)),
                    o_shape=(D, HP), o_dtype=F32, dims=NN, acc_shape=None)
    dh_prev, dg = _norm_project_bwd(f"bdz_{tag}", dH, sv["winp"], sv["h"], sv["g"], dh)
    return dh_prev, dg, carried


def _winp_layer(win_g, l):
    return _win_to_padded(jnp.transpose(win_g[:, l], (1, 0, 2)).reshape(D, D_IN))


def _local_step(x2, tgt2, W, comm=None):
    n_tok = x2.shape[0]
    Lp = n_tok + LANE
    nc = Lp // CHUNK
    g_chunks = max(g for g in (1, 2, 3, 6) if nc % g == 0)
    cos, sin = _rope_tables(n_tok)
    bd = jnp.asarray(np.kron(np.eye(ATT_H, dtype=np.float32), np.full((HD, HD), 1.0 / HD, np.float32)))
    bias = jnp.where(jnp.arange(Lp) >= PAD, 0.0, -1e30).astype(F32).reshape(1, Lp)
    w2, b2 = W["w2"], W["b2"]
    w2p = jnp.zeros((DEPTH, LANE, 512), F32)
    w2p = w2p.at[:, 0:GLA_RANK, 0:256].set(w2[:, 0]).at[:, GLA_RANK:2 * GLA_RANK, 256:512].set(w2[:, 1]).astype(BF16)
    C = dict(cos=cos, sin=sin, bd=bd, bias=bias, w2p=w2p, b2p=b2.reshape(DEPTH, 1, 512),
             gq=jnp.tile(W["qn"], (1, ATT_H)).reshape(DEPTH, 1, 512), gk=jnp.tile(W["kn"], (1, ATT_KV)).reshape(DEPTH, 1, LANE))
    h = jnp.concatenate([jnp.zeros((PAD, D), F32), W["meta"], x2], axis=0)
    saved = []
    for l in range(DEPTH):
        h, s0, W = _ffn_fwd(f"l{l}a", h, W, l, 0)
        riders = (comm.fwd_rider(W, l, ("down",)), comm.fwd_rider(W, l, ("gate", "up"))) if comm else (None, None)
        h, sm, W = _mixer_fwd(f"l{l}m", h, W, _winp_layer(W["win"], l), C, l, g_chunks, riders)
        h, s1, W = _ffn_fwd(f"l{l}b", h, W, l, 1, comm.fwd_rider(W, l, ("win", "wpa", "wpb", "wout")) if comm else None)
        saved.append((s0, sm, s1))
    dh, dfin, loss = _loss_head("loss_head", h, W["fin"].reshape(1, D), tgt2)
    S = dict(bm=[None] * DEPTH, gq=[None] * DEPTH, gk=[None] * DEPTH, gn=[None] * DEPTH, w2p=[None] * DEPTH,
             b2p=[None] * DEPTH, ng=[[None] * 3 for _ in range(DEPTH)])
    Gs = [None] * DEPTH
    for l in reversed(range(DEPTH)):
        s0, sm, s1 = saved[l]
        G = {}
        rider = comm.ffn_rider(l) if comm else None
        dh, S["ng"][l][2], carried = _ffn_bwd(f"l{l}b", dh, s1, W, G, l, 1, rider)
        if rider is not None:
            comm.ffn_carried(l, carried)
        rider = comm.bwd_rider(l) if comm else None
        dh, S["ng"][l][1], carried = _mixer_bwd(f"l{l}m", dh, sm, W, C, G, S, l, g_chunks, rider)
        if rider is not None:
            comm.bwd_carried(l, carried)
        rider = comm.last_rider(l) if comm else None
        dh, S["ng"][l][0], carried = _ffn_bwd(f"l{l}a", dh, s0, W, G, l, 0, rider)
        if rider is not None:
            comm.last_carried(carried)
        Gs[l] = G
        if comm:
            comm.layer_done(l, G)
    small = dict(
        meta=dh[PAD:LANE],
        norm_gains=jnp.stack([jnp.concatenate(S["ng"][l], axis=0) for l in range(DEPTH)]),
        w2=jnp.stack([jnp.stack([S["w2p"][l][0:GLA_RANK, 0:256], S["w2p"][l][GLA_RANK:2 * GLA_RANK, 256:512]]) for l in range(DEPTH)]),
        b2=jnp.stack([S["b2p"][l].reshape(2, 256) for l in range(DEPTH)]),
        gn=jnp.concatenate(S["gn"], axis=0),
        qn=jnp.stack([S["gq"][l].reshape(ATT_H, HD).sum(0) for l in range(DEPTH)]),
        kn=jnp.stack([S["gk"][l].reshape(ATT_KV, HD).sum(0) for l in range(DEPTH)]),
        bm=jnp.stack(S["bm"]),
        fin=dfin.reshape(D),
    )
    return loss[0, 0], dh, Gs, small


def _win_to_padded(w):
    pad = jnp.zeros(w.shape[:-1] + (HP - D_IN,), w.dtype)
    return jnp.concatenate([w[..., 2336:4384], w[..., 0:1536], w[..., 1568:2336], w[..., 1536:1568], pad], axis=-1)


def _win_from_padded(w):
    return jnp.concatenate([w[..., 2048:3584], w[..., 4352:4384], w[..., 3584:4352], w[..., 0:2048]], axis=-1)


def _assemble(g):
    W = dict(gate=g["gate"], up=g["up"], down=g["down"], win=g["win"], wpa=g["wpa"], wpb=g["wpb"], wout=g["wout"])
    sm = g["small"]
    parts = _unpack(sm, SHARDED_SMALL)
    W["meta"] = jnp.transpose(parts["meta"], (1, 0, 2)).reshape(N_META, D)
    W["norm_gains"] = jnp.transpose(parts["norm_gains"], (1, 2, 0, 3)).reshape(DEPTH, 3, D)
    W["w2"] = jnp.transpose(parts["w2"], (1, 2, 3, 0, 4)).reshape(DEPTH, 2, GLA_RANK, 256)
    W["b2"] = jnp.transpose(parts["b2"], (1, 2, 0, 3)).reshape(DEPTH, 2, 256)
    W["bm"] = jnp.transpose(parts["bm"], (1, 2, 0, 3)).reshape(DEPTH, 2, D)
    return W


SHARDED_SMALL = dict(meta=(N_META, 256), norm_gains=(DEPTH, 3, 256), w2=(DEPTH, 2, GLA_RANK, 64), b2=(DEPTH, 2, 64),
                     bm=(DEPTH, 2, 256))
FULL_SMALL = dict(meta=(N_META, D), norm_gains=(DEPTH, 3, D), w2=(DEPTH, 2, GLA_RANK, 256), b2=(DEPTH, 2, 256),
                  gn=(DEPTH, 512), qn=(DEPTH, HD), kn=(DEPTH, HD), bm=(DEPTH, 2, D), fin=(D,))


def _pack(parts, table, rows):
    flat = jnp.concatenate([parts[k].reshape(-1).astype(F32) for k in table])
    return jnp.pad(flat, (0, rows * LANE - flat.shape[0])).reshape(rows, LANE)


def _unpack(packed, table):
    lead = packed.shape[:-2]
    flat = packed.reshape(lead + (-1,))
    out, off = {}, 0
    for k, shp in table.items():
        n = int(np.prod(shp))
        out[k] = flat[..., off:off + n].reshape(lead + tuple(shp))
        off += n
    return out


def _rows_for(table, mult):
    n = sum(int(np.prod(s)) for s in table.values())
    return -(-n // (LANE * mult)) * mult


SMALL_ROWS = _rows_for(SHARDED_SMALL, 16)
GRAD_ROWS = _rows_for(dict(FULL_SMALL, loss=(1,)), 8)


def _place():
    x, y, c = lax.axis_index("x"), lax.axis_index("y"), lax.axis_index("c")
    return x, y, c


def _other_chips(x, y):
    return [(1 - x, y), (x, 1 - y), (1 - x, 1 - y)]


def _cast_place(name, w3, slot):
    A, R, Cc = w3.shape
    tr = _rt(R, 512, 16)

    def body(p_ref, w_ref, o_ref):
        o_ref[...] = w_ref[...].astype(BF16)

    return pl.pallas_call(
        body, name=name,
        grid_spec=pltpu.PrefetchScalarGridSpec(
            num_scalar_prefetch=1, grid=(A, R // tr),
            in_specs=[pl.BlockSpec((None, tr, Cc), lambda a, r, p_ref: (a, r, 0))],
            out_specs=pl.BlockSpec((None, None, tr, Cc), lambda a, r, p_ref: (p_ref[0], a, r, 0))),
        out_shape=SDS((NSH, A, R, Cc), BF16))(slot, w3)


class _GatherRider:
    def __init__(self, keys, bufs, layers):
        n = len(bufs)
        self.keys, self.ins, self.layers = keys, list(bufs), layers
        self.out_shape = [SDS(b.shape, b.dtype) for b in bufs]
        self.aliases = {a: a for a in range(n)}
        self.sems = [pltpu.SemaphoreType.DMA((n, 3)) for _ in range(4)]

    def updated(self, W, carried):
        return {**W, **dict(zip(self.keys, carried))}

    def hooks(self, ins, outs, sems):
        send, recv, fsend, frecv = sems
        n = len(outs)

        def rows(a, slot, core):
            ref, l = outs[a], self.layers[a]
            half = ref.shape[-2] // 2
            return ref.at[slot, pl.ds(core * half, half)] if l is None else ref.at[slot, l, pl.ds(core * half, half)]

        def ici(a, k, part, px, py, c):
            return pltpu.make_async_remote_copy(src_ref=part, dst_ref=part, send_sem=send.at[a, k], recv_sem=recv.at[a, k],
                                                device_id=(px, py, c), device_id_type=MESH)

        def d2d(a, k, part, x, y, c):
            return pltpu.make_async_remote_copy(src_ref=part, dst_ref=part, send_sem=fsend.at[a, k], recv_sem=frecv.at[a, k],
                                                device_id=(x, y, 1 - c), device_id_type=MESH)

        def start():
            x, y, c = _place()
            for a in range(n):
                for k, (px, py) in enumerate(_other_chips(x, y)):
                    ici(a, k, rows(a, 2 * x + y, c), px, py, c).start()

        def mid():
            x, y, c = _place()
            for a in range(n):
                for k, (px, py) in enumerate(_other_chips(x, y)):
                    landed = rows(a, 2 * px + py, c)
                    ici(a, k, landed, px, py, c).wait_recv()
                    d2d(a, k, landed, x, y, c).start()

        def end():
            x, y, c = _place()
            for a in range(n):
                for k, (px, py) in enumerate(_other_chips(x, y)):
                    d2d(a, k, rows(a, 2 * px + py, 1 - c), x, y, c).wait_recv()
                    ici(a, k, rows(a, 2 * x + y, c), px, py, c).wait_send()
                    d2d(a, k, rows(a, 2 * px + py, c), x, y, c).wait_send()

        return start, mid, end


class _ChipExchangeRider:
    def __init__(self, arrs):
        n = len(arrs)
        self.ins = list(arrs)
        self.out_shape = [SDS((3,) + a.shape[1:], a.dtype) for a in arrs]
        self.aliases = {}
        self.sems = [pltpu.SemaphoreType.DMA((n, 3)), pltpu.SemaphoreType.DMA((n, 3))]

    def hooks(self, ins, outs, sems):
        send, recv = sems

        def copy(a, k, px, py, c):
            return pltpu.make_async_remote_copy(src_ref=ins[a].at[2 * px + py], dst_ref=outs[a].at[k], send_sem=send.at[a, k],
                                                recv_sem=recv.at[a, k], device_id=(px, py, c), device_id_type=MESH)

        def start():
            x, y, c = _place()
            for a in range(len(ins)):
                for k, (px, py) in enumerate(_other_chips(x, y)):
                    copy(a, k, px, py, c).start()

        def mid():
            pass

        def end():
            x, y, c = _place()
            for a in range(len(ins)):
                for k, (px, py) in enumerate(_other_chips(x, y)):
                    copy(a, k, px, py, c).wait()

        return start, mid, end


def _run_rider(name, rider):
    r_in, r_out = len(rider.ins), len(rider.out_shape)

    def body(*refs):
        start, mid, end = rider.hooks(refs[:r_in], refs[r_in:r_in + r_out], refs[r_in + r_out:])
        start()
        mid()
        end()

    any_spec = pl.BlockSpec(memory_space=pl.ANY)
    return pl.pallas_call(
        body, name=name, in_specs=[any_spec] * r_in, out_specs=[any_spec] * r_out, out_shape=list(rider.out_shape),
        scratch_shapes=list(rider.sems), input_output_aliases=dict(rider.aliases))(*rider.ins)


class _PairExchangeRider:
    def __init__(self, arrs):
        n = len(arrs)
        self.ins = list(arrs)
        self.out_shape = [SDS((NSH, a.shape[1] // 2, a.shape[2]), a.dtype) for a in arrs]
        self.aliases = {}
        self.sems = [pltpu.SemaphoreType.DMA((n,)), pltpu.SemaphoreType.DMA((n,))]

    def hooks(self, ins, outs, sems):
        send, recv = sems

        def copy(a):
            x, y, c = _place()
            half = ins[a].shape[1] // 2
            return pltpu.make_async_remote_copy(
                src_ref=ins[a].at[:, pl.ds((1 - c) * half, half)], dst_ref=outs[a], send_sem=send.at[a], recv_sem=recv.at[a],
                device_id=(x, y, 1 - c), device_id_type=MESH)

        def start():
            for a in range(len(ins)):
                copy(a).start()

        def mid():
            pass

        def end():
            for a in range(len(ins)):
                copy(a).wait()

        return start, mid, end


def _pair_add(name, g, p, core):
    _, Rh, Cc = p.shape
    tr = _rt(Rh, 512, 16)
    nr = Rh // tr

    def body(c_ref, g_ref, p_ref, o_ref, ob_ref):
        v = g_ref[...] + p_ref[...]
        o_ref[...] = v
        ob_ref[...] = v.astype(BF16)

    blk = pl.BlockSpec((None, tr, Cc), lambda s, r, c_ref: (s, r, 0))
    return pl.pallas_call(
        body, name=name,
        grid_spec=pltpu.PrefetchScalarGridSpec(
            num_scalar_prefetch=1, grid=(NSH, nr),
            in_specs=[pl.BlockSpec((None, tr, Cc), lambda s, r, c_ref: (s, c_ref[0] * nr + r, 0)), blk],
            out_specs=[blk, blk]),
        out_shape=[SDS((NSH, Rh, Cc), F32), SDS((NSH, Rh, Cc), BF16)])(core, g, p)


def _chip_add(name, hsum, q, chip, core, l, into):
    _, Rh, Cc = hsum.shape
    tr = _rt(Rh, 512, 8)
    nr = Rh // tr

    def body(*refs):
        h_ref, q_ref, o_ref = refs[2], refs[3], refs[-1]
        o_ref[...] = ((h_ref[...] + q_ref[0].astype(F32)) + q_ref[1].astype(F32)) + q_ref[2].astype(F32)

    in_specs = [pl.BlockSpec((None, tr, Cc), lambda r, p_ref, c_ref: (p_ref[0], r, 0)),
                pl.BlockSpec((3, tr, Cc), lambda r, p_ref, c_ref: (0, r, 0))]
    args = [chip, core, hsum, q]
    aliases = {}
    if into is not None:
        in_specs.append(pl.BlockSpec(memory_space=pl.ANY))
        args.append(into)
        aliases = {4: 0}
    return pl.pallas_call(
        body, name=name,
        grid_spec=pltpu.PrefetchScalarGridSpec(
            num_scalar_prefetch=2, grid=(nr,), in_specs=in_specs,
            out_specs=pl.BlockSpec((None, tr, Cc), lambda r, p_ref, c_ref: (l, c_ref[0] * nr + r, 0))),
        out_shape=SDS((DEPTH, 2 * Rh, Cc), F32), input_output_aliases=aliases)(*args)


class _PairShareRider:
    def __init__(self, bufs, lo, hi):
        n = len(bufs)
        self.ins, self.lo, self.hi = list(bufs), lo, hi
        self.out_shape = [SDS(b.shape, b.dtype) for b in bufs]
        self.aliases = {a: a for a in range(n)}
        self.sems = [pltpu.SemaphoreType.DMA((n,)), pltpu.SemaphoreType.DMA((n,))]

    def hooks(self, ins, outs, sems):
        send, recv = sems
        lo, hi = self.lo, self.hi

        def copy(a, whose):
            x, y, c = _place()
            half = outs[a].shape[1] // 2
            part = outs[a].at[lo:hi, pl.ds((c if whose == 0 else 1 - c) * half, half)]
            return pltpu.make_async_remote_copy(src_ref=part, dst_ref=part, send_sem=send.at[a], recv_sem=recv.at[a],
                                                device_id=(x, y, 1 - c), device_id_type=MESH)

        def start():
            for a in range(len(outs)):
                copy(a, 0).start()

        def mid():
            pass

        def end():
            for a in range(len(outs)):
                copy(a, 0).wait_send()
                copy(a, 1).wait_recv()

        return start, mid, end


def _allreduce_small(v):
    rows = v.shape[0]

    def body(v_ref, o_ref, buf, send, recv):
        x, y, c = _place()
        me = 4 * x + 2 * y + c
        buf[me] = v_ref[...]
        cps = []
        k = 0
        for dx in range(2):
            for dy in range(2):
                for dc in range(2):
                    if dx + dy + dc == 0:
                        continue
                    cp = pltpu.make_async_remote_copy(
                        src_ref=v_ref, dst_ref=buf.at[me], send_sem=send.at[k], recv_sem=recv.at[k],
                        device_id=(jnp.bitwise_xor(x, dx), jnp.bitwise_xor(y, dy), jnp.bitwise_xor(c, dc)), device_id_type=MESH)
                    cp.start()
                    cps.append((cp, dx, dy, dc))
                    k += 1
        for k, (cp, dx, dy, dc) in enumerate(cps):
            cp.wait_send()
            src = 4 * jnp.bitwise_xor(x, dx) + 2 * jnp.bitwise_xor(y, dy) + jnp.bitwise_xor(c, dc)
            pltpu.make_async_remote_copy(
                src_ref=v_ref, dst_ref=buf.at[src], send_sem=send.at[k], recv_sem=recv.at[k],
                device_id=(x, y, c), device_id_type=MESH).wait_recv()
        acc = buf[0]
        for d in range(1, 8):
            acc = acc + buf[d]
        o_ref[...] = acc

    vm = pl.BlockSpec(memory_space=pltpu.VMEM)
    return pl.pallas_call(
        body, name="allreduce_small", in_specs=[vm], out_specs=vm, out_shape=SDS((rows, LANE), F32),
        scratch_shapes=[pltpu.VMEM((8, rows, LANE), F32), pltpu.SemaphoreType.DMA((7,)), pltpu.SemaphoreType.DMA((7,))])(v)


def _adamw(name, w, g, m, v, lo=0, hi=None, into=None, rider=None):
    A, R, Cc = w.shape
    hi = A if hi is None else hi
    tr = _rt(R, 512, 8)

    def core(ins, outs, scr):
        w_ref, g_ref, m_ref, v_ref = ins[:4]
        d_ref, mo_ref, vo_ref = outs
        gv = g_ref[...]
        mn = ADAM_B1 * m_ref[...] + (1.0 - ADAM_B1) * gv
        vn = ADAM_B2 * v_ref[...] + (1.0 - ADAM_B2) * (gv * gv)
        m_hat = mn / (1.0 - ADAM_B1 ** ADAM_STEP)
        v_hat = vn / (1.0 - ADAM_B2 ** ADAM_STEP)
        d_ref[...] = -ADAM_LR * (m_hat / (jnp.sqrt(v_hat) + ADAM_EPS) + ADAM_WD * w_ref[...])
        mo_ref[...] = mn
        vo_ref[...] = vn

    blk = pl.BlockSpec((None, tr, Cc), lambda a, r: (lo + a, r, 0))
    in_specs, args, aliases = [blk] * 4, [w, g, m, v], {}
    if into is not None:
        in_specs = in_specs + [pl.BlockSpec(memory_space=pl.ANY)] * 3
        args = args + list(into)
        aliases = {4: 0, 5: 1, 6: 2}
    return _ridden_call(core, rider, name=name, grid=(hi - lo, R // tr), in_specs=in_specs, out_specs=[blk] * 3,
                        out_shape=[SDS(w.shape, F32)] * 3, scratch_shapes=[], args=args, aliases=aliases)


BIG = ("gate", "up", "down", "win", "wpa", "wpb", "wout")
SMALL = ("meta", "norm_gains", "w2", "b2", "gn", "qn", "kn", "bm", "fin")


def _view3(a):
    return a.reshape(a.shape[0], -1, a.shape[-1])


class _StepComm:
    def __init__(self, chip, core):
        self.pvec, self.cvec = chip.reshape(1), core.reshape(1)
        self.loc, self.sums = {}, {}
        self.red = {k: None for k in BIG}

    def fwd_rider(self, W, l, keys):
        if l + 1 >= DEPTH:
            return None
        return _GatherRider(keys, [W[k] for k in keys], [l + 1] * len(keys))

    def layer_done(self, l, G):
        dwin = jnp.transpose(_win_from_padded(G["winp"]).reshape(D, NSH, D_IN // NSH), (1, 0, 2))
        self.loc[l] = [dwin if k == "win" else _view3(G[k]) for k in BIG]

    def ffn_rider(self, l):
        return _PairExchangeRider(self.loc[l + 1]) if l + 1 < DEPTH else None

    def ffn_carried(self, l, got):
        self._pair_add(l + 1, got)

    def _pair_add(self, l, got):
        self.sums[l] = [_pair_add(f"pair_add_{k}_l{l}", a, p, self.cvec) for k, a, p in zip(BIG, self.loc.pop(l), got)]

    def bwd_rider(self, l):
        if l + 1 >= DEPTH:
            return None
        return _ChipExchangeRider([s[1] for s in self.sums[l + 1]])

    def bwd_carried(self, l, arrived):
        self._chip_add(l + 1, arrived)

    def _chip_add(self, l, arrived):
        for k, s, q in zip(BIG, self.sums.pop(l), arrived):
            self.red[k] = _chip_add(f"chip_add_{k}_l{l}", s[0], q, self.pvec, self.cvec, l, self.red[k])

    def last_rider(self, l):
        return _PairShareRider([self.red[k] for k in BIG], 1, DEPTH) if l == 0 else None

    def last_carried(self, shared):
        self.red = dict(zip(BIG, shared))

    def finish(self, w, m, v):
        self._pair_add(0, _run_rider("pair_exchange_l0", _PairExchangeRider(self.loc[0])))
        sums = dict(zip(BIG, self.sums.pop(0)))
        upd = {}
        for k in BIG:
            upd[k], arrived = _adamw(f"adamw_{k}_l123", w[k], self.red[k], m[k], v[k], 1, DEPTH,
                                     rider=_ChipExchangeRider([sums[k][1]]))
            self.red[k] = _chip_add(f"chip_add_{k}_l0", sums[k][0], arrived[0], self.pvec, self.cvec, 0, self.red[k])
        red = dict(zip(BIG, _run_rider("pair_share_l0", _PairShareRider([self.red[k] for k in BIG], 0, 1))))
        for k in BIG:
            upd[k], _ = _adamw(f"adamw_{k}_l0", w[k], red[k], m[k], v[k], 0, 1, into=upd[k])
        return red, upd


def kernel(x, meta_tokens, norm_gains, ffn_w_gate, ffn_w_up, ffn_w_down, w_in, gla_w2, gla_b2, gla_gn, q_norm, k_norm, w_pa, w_pb, b_merge, w_out, final_norm, loss_target, m_meta_tokens, m_norm_gains, m_ffn_w_gate, m_ffn_w_up, m_ffn_w_down, m_w_in, m_gla_w2, m_gla_b2, m_gla_gn, m_q_norm, m_k_norm, m_w_pa, m_w_pb, m_b_merge, m_w_out, m_final_norm, v_meta_tokens, v_norm_gains, v_ffn_w_gate, v_ffn_w_up, v_ffn_w_down, v_w_in, v_gla_w2, v_gla_b2, v_gla_gn, v_q_norm, v_k_norm, v_w_pa, v_w_pb, v_b_merge, v_w_out, v_final_norm):
    big_w = dict(gate=ffn_w_gate, up=ffn_w_up, down=ffn_w_down, win=w_in, wpa=w_pa, wpb=w_pb, wout=w_out)
    big_m = dict(gate=m_ffn_w_gate, up=m_ffn_w_up, down=m_ffn_w_down, win=m_w_in, wpa=m_w_pa, wpb=m_w_pb, wout=m_w_out)
    big_v = dict(gate=v_ffn_w_gate, up=v_ffn_w_up, down=v_ffn_w_down, win=v_w_in, wpa=v_w_pa, wpb=v_w_pb, wout=v_w_out)
    small_w = dict(meta=meta_tokens, norm_gains=norm_gains, w2=gla_w2, b2=gla_b2, gn=gla_gn, qn=q_norm, kn=k_norm,
                   bm=b_merge, fin=final_norm)
    small_m = dict(meta=m_meta_tokens, norm_gains=m_norm_gains, w2=m_gla_w2, b2=m_gla_b2, gn=m_gla_gn, qn=m_q_norm,
                   kn=m_k_norm, bm=m_b_merge, fin=m_final_norm)
    small_v = dict(meta=v_meta_tokens, norm_gains=v_norm_gains, w2=v_gla_w2, b2=v_gla_b2, gn=v_gla_gn, qn=v_q_norm,
                   kn=v_k_norm, bm=v_b_merge, fin=v_final_norm)
    xi, yi, ci = _place()
    chip = (2 * xi + yi).astype(jnp.int32)

    comm = _StepComm(chip, ci.astype(jnp.int32))
    shard_pack = _pack({k: small_w[k] for k in SHARDED_SMALL}, SHARDED_SMALL, SMALL_ROWS)
    placed = [_cast_place(f"cast_{k}", _view3(big_w[k]), comm.pvec) for k in BIG]
    placed.append(lax.dynamic_update_slice(jnp.zeros((NSH, SMALL_ROWS, LANE), F32), shard_pack[None], (chip, 0, 0)))
    gathered = _run_rider("gather_l0", _GatherRider(BIG + ("small",), placed, [0] * len(BIG) + [None]))
    W = _assemble(dict(zip(BIG + ("small",), gathered)))
    W.update(gn=gla_gn, qn=q_norm, kn=k_norm, fin=final_norm)

    loss, dh0, _, gs = _local_step(x[0], loss_target[0], W, comm)
    grad_x = dh0[LANE:][None]
    red, upd = comm.finish(*({k: _view3(t[k]) for k in BIG} for t in (big_w, big_m, big_v)))
    grads, deltas, new_m, new_v = {}, {}, {}, {}
    for k in BIG:
        shp = big_w[k].shape
        grads[k] = red[k].reshape(shp)
        deltas[k], new_m[k], new_v[k] = (t.reshape(shp) for t in upd[k])

    gs["loss"] = loss.reshape(1)
    table = dict(FULL_SMALL, loss=(1,))
    tot = _unpack(_allreduce_small(_pack(gs, table, GRAD_ROWS)), table)
    loss_out = tot["loss"][0]
    sl = dict(meta=(1, 256), norm_gains=(2, 256), w2=(3, 64), b2=(2, 64), bm=(2, 256))
    for k in SMALL:
        gk = tot[k]
        if k in sl:
            ax, width = sl[k]
            gk = lax.dynamic_slice_in_dim(gk, chip * width, width, axis=ax)
        grads[k] = gk
    tbl = {k: small_w[k].shape for k in SMALL}
    rows = _rows_for(tbl, 8)
    packs = [_pack(src, tbl, rows)[None] for src in (small_w, grads, small_m, small_v)]
    (d, mn, vn), _ = _adamw("adamw_small", *packs)
    for dst, packed in ((deltas, d), (new_m, mn), (new_v, vn)):
        dst.update(_unpack(packed[0], tbl))

    order = ("meta", "norm_gains", "gate", "up", "down", "win", "w2", "b2", "gn", "qn", "kn", "wpa", "wpb", "bm", "wout", "fin")
    return (loss_out, grad_x, *[grads[k] for k in order], *[deltas[k] for k in order], *[new_m[k] for k in order],
            *[new_v[k] for k in order])
```

```python
import functools

import numpy as np
import jax
import jax.numpy as jnp
from jax import lax
from jax.experimental import pallas as pl
from jax.experimental.pallas import tpu as pltpu

F32, BF16 = jnp.float32, jnp.bfloat16
SDS = jax.ShapeDtypeStruct
HIGHEST = lax.Precision.HIGHEST
MESH = pl.DeviceIdType.MESH

D = 1024
DEPTH = 4
N_META = 16
GRID_W = 64
GLA_H, GLA_DK, GLA_DV, GLA_RANK, GLA_TAU, CHUNK = 4, 64, 128, 16, 16.0, 64
ATT_H, ATT_KV, HD = 8, 2, 64
D_FF = 2816
EPS = 1e-6
ROPE_THETA = 10000.0
ADAM_LR, ADAM_B1, ADAM_B2, ADAM_EPS, ADAM_WD, ADAM_STEP = 0.001, 0.9, 0.999, 1e-08, 0.01, 10

NSH = 4
FF_S = D_FF // NSH
FF_P = 768
LANE = 128
PAD = LANE - N_META
D_IN = 4384
C_GA, C_GB, C_QA, C_KA, C_VA, C_RA, C_QB, C_KB, C_VB, C_LR, HP = 0, 1024, 2048, 2304, 2560, 3072, 3584, 4096, 4224, 4352, 4480
VMEM_BIG = 56 * 2 ** 20
ATT_SUB_FWD, ATT_SUB_BWD = 128, 192
RIDER_MID_BACK = 5


def _rt(n, cap, mult=LANE):
    best = None
    t = mult
    while t <= min(n, cap):
        if n % t == 0:
            best = t
        t += mult
    assert best is not None, (n, cap, mult)
    return best


def _cp(big=False):
    return pltpu.CompilerParams(vmem_limit_bytes=VMEM_BIG) if big else None


def _row_ids(i, tm):
    return i * tm + lax.broadcasted_iota(jnp.int32, (tm, 1), 0)


def _mm(name, a, b, *, grid, a_spec, b_spec, o_spec, o_shape, o_dtype, dims, acc_shape, nk=1, scale=None,
        res=None, res_spec=None, a_sl=None, b_sl=None, pad_w=None, into=None, shards=None):
    has_res, has_into = res is not None, into is not None

    def body(*refs):
        a_ref, b_ref = refs[0], refs[1]
        p = 2
        res_ref = None
        if has_res:
            res_ref = refs[p]
            p += 1
        if has_into:
            p += 1
        o_ref = refs[p]
        acc_ref = refs[p + 1] if nk > 1 else None
        av = (a_ref[a_sl] if a_sl is not None else a_ref[...]).astype(BF16)
        bv = (b_ref[b_sl] if b_sl is not None else b_ref[...]).astype(BF16)
        if shards == "rows":
            bv = bv.reshape(bv.shape[0] * bv.shape[1], bv.shape[2])
        if shards == "cols":
            w = bv.shape[2]
            if dims == NN:
                prod = jnp.concatenate([lax.dot_general(av, bv[s], NN, preferred_element_type=F32) for s in range(NSH)], axis=1)
            else:
                prod = sum(lax.dot_general(av[:, s * w:(s + 1) * w], bv[s], NT, preferred_element_type=F32) for s in range(NSH))
        else:
            prod = lax.dot_general(av, bv, dims, preferred_element_type=F32)

        def finish(v):
            if scale is not None:
                v = v * scale
            if has_res:
                v = v + res_ref[...]
            v = v.astype(o_dtype)
            if pad_w is None:
                o_ref[...] = v
            else:
                w = v.shape[-1]
                o_ref[:, :w] = v
                o_ref[:, w:] = jnp.zeros((v.shape[0], pad_w - w), o_dtype)

        if nk == 1:
            finish(prod)
        else:
            k = pl.program_id(len(grid) - 1)

            @pl.when(k == 0)
            def _():
                acc_ref[...] = prod

            @pl.when(k > 0)
            def _():
                acc_ref[...] += prod

            @pl.when(k == nk - 1)
            def _():
                finish(acc_ref[...])

    in_specs = [a_spec, b_spec]
    args = [a, b]
    if has_res:
        in_specs.append(res_spec)
        args.append(res)
    aliases = {}
    if has_into:
        aliases = {len(args): 0}
        in_specs.append(pl.BlockSpec(memory_space=pl.ANY))
        args.append(into)
        o_shape = into.shape
    return pl.pallas_call(
        body, name=name, grid=grid, in_specs=in_specs, out_specs=o_spec, out_shape=SDS(tuple(o_shape), o_dtype),
        scratch_shapes=[pltpu.VMEM(acc_shape, F32)] if nk > 1 else [], input_output_aliases=aliases,
        compiler_params=_cp(True))(*args)


NN = (((1,), (0,)), ((), ()))
NT = (((1,), (1,)), ((), ()))
TN = (((0,), (0,)), ((), ()))


W_CHUNK = 896


def _norm_project(name, h, g, winp, rider=None):
    Lp = h.shape[0]
    tm = _rt(Lp, 384)

    def core(ins, outs, scr):
        h_ref, g_ref, w_ref = ins
        o_ref, zt_ref = outs
        x = h_ref[...]
        y = x * lax.rsqrt(jnp.mean(x * x, axis=-1, keepdims=True) + EPS) * g_ref[...]
        zt_ref[...] = y.T.astype(BF16)
        z = y.astype(BF16)
        for n in range(HP // W_CHUNK):
            cs = slice(n * W_CHUNK, (n + 1) * W_CHUNK)
            o_ref[:, cs] = jnp.dot(z, w_ref[:, cs], preferred_element_type=F32)

    return _ridden_call(
        core, rider, name=name, grid=(Lp // tm,),
        in_specs=[pl.BlockSpec((tm, D), lambda i: (i, 0)), pl.BlockSpec((1, D), lambda i: (0, 0)),
                  pl.BlockSpec((D, HP), lambda i: (0, 0), pipeline_mode=pl.Buffered(1))],
        out_specs=[pl.BlockSpec((tm, HP), lambda i: (i, 0)), pl.BlockSpec((D, tm), lambda i: (0, i))],
        out_shape=[SDS((Lp, HP), F32), SDS((D, Lp), BF16)], scratch_shapes=[], args=(h, g, winp), mid_back=1)


def _norm_project_bwd(name, dH, winp, h, g, dh):
    Lp = h.shape[0]
    tm = _rt(Lp, 384)

    def body(dhh_ref, w_ref, h_ref, g_ref, dh_ref, o_ref, dg_ref):
        dn = lax.dot_general(dhh_ref[...], w_ref[...], NT, preferred_element_type=F32)
        x = h_ref[...]
        r = lax.rsqrt(jnp.mean(x * x, axis=-1, keepdims=True) + EPS)
        xh = x * r
        u = dn * g_ref[...]
        o_ref[...] = dh_ref[...] + r * (u - xh * jnp.mean(u * xh, axis=-1, keepdims=True))

        @pl.when(pl.program_id(0) == 0)
        def _():
            dg_ref[...] = jnp.zeros_like(dg_ref)

        dg_ref[...] += jnp.sum(dn * xh, axis=0, keepdims=True)

    row = pl.BlockSpec((tm, D), lambda i: (i, 0))
    vec = pl.BlockSpec((1, D), lambda i: (0, 0))
    return pl.pallas_call(
        body, name=name, grid=(Lp // tm,),
        in_specs=[pl.BlockSpec((tm, HP), lambda i: (i, 0)), pl.BlockSpec((D, HP), lambda i: (0, 0), pipeline_mode=pl.Buffered(1)),
                  row, vec, row],
        out_specs=[row, vec], out_shape=[SDS((Lp, D), F32), SDS((1, D), F32)], compiler_params=_cp(True))(dH, winp, h, g, dh)


def _loss_head(name, h, g, tgt):
    Lp = h.shape[0]
    tm = LANE

    def body(h_ref, g_ref, t_ref, dh_ref, dg_ref, loss_ref):
        i = pl.program_id(0)
        x = h_ref[...]
        r = lax.rsqrt(jnp.mean(x * x, axis=-1, keepdims=True) + EPS)
        xh = x * r
        gg = g_ref[...]
        err = jnp.where(i >= 1, xh * gg - t_ref[...], 0.0)
        dy = err * (1.0 / D)
        u = dy * gg
        dh_ref[...] = r * (u - xh * jnp.mean(u * xh, axis=-1, keepdims=True))

        @pl.when(i == 0)
        def _():
            dg_ref[...] = jnp.zeros_like(dg_ref)
            loss_ref[...] = jnp.zeros_like(loss_ref)

        dg_ref[...] += jnp.sum(dy * xh, axis=0, keepdims=True)
        loss_ref[...] += (0.5 / D) * jnp.sum(err * err)

    row = pl.BlockSpec((tm, D), lambda i: (i, 0))
    vec = pl.BlockSpec((1, D), lambda i: (0, 0))
    return pl.pallas_call(
        body, name=name, grid=(Lp // tm,),
        in_specs=[row, vec, pl.BlockSpec((tm, D), lambda i: (jnp.maximum(i - 1, 0), 0))],
        out_specs=[row, vec, pl.BlockSpec((8, LANE), lambda i: (0, 0))],
        out_shape=[SDS((Lp, D), F32), SDS((1, D), F32), SDS((8, LANE), F32)])(h, g, tgt)


def _merge_fwd(name, H, pa, pb, bm):
    Lp = H.shape[0]
    tm = _rt(Lp, 384)

    def body(g_ref, pa_ref, pb_ref, bm_ref, y_ref, yt_ref):
        gv = g_ref[...]
        y = (jax.nn.sigmoid(gv[:, :D] + bm_ref[0:1, :]) * pa_ref[...]
             + jax.nn.sigmoid(gv[:, D:] + bm_ref[1:2, :]) * pb_ref[...])
        y_ref[...] = y.astype(BF16)
        yt_ref[...] = y.T.astype(BF16)

    row = pl.BlockSpec((tm, D), lambda i: (i, 0))
    return pl.pallas_call(
        body, name=name, grid=(Lp // tm,),
        in_specs=[pl.BlockSpec((tm, 2 * D), lambda i: (i, 0)), row, row, pl.BlockSpec((2, D), lambda i: (0, 0))],
        out_specs=[row, pl.BlockSpec((D, tm), lambda i: (0, i))],
        out_shape=[SDS((Lp, D), BF16), SDS((D, Lp), BF16)])(H, pa, pb, bm)


def _merge_bwd(name, H, pa, pb, bm, dy):
    Lp = H.shape[0]
    tm = _rt(Lp, 384)

    def body(g_ref, pa_ref, pb_ref, bm_ref, dy_ref, dpa_ref, dpb_ref, dh_ref, dbm_ref):
        gv = g_ref[...]
        dyv = dy_ref[...]
        sa = jax.nn.sigmoid(gv[:, :D] + bm_ref[0:1, :])
        sb = jax.nn.sigmoid(gv[:, D:] + bm_ref[1:2, :])
        dpa_ref[...] = (dyv * sa).astype(BF16)
        dpb_ref[...] = (dyv * sb).astype(BF16)
        dga = dyv * pa_ref[...] * (sa * (1.0 - sa))
        dgb = dyv * pb_ref[...] * (sb * (1.0 - sb))
        dh_ref[:, :D] = dga.astype(BF16)
        dh_ref[:, D:] = dgb.astype(BF16)

        @pl.when(pl.program_id(0) == 0)
        def _():
            dbm_ref[...] = jnp.zeros_like(dbm_ref)

        dbm_ref[0:1, :] += jnp.sum(dga, axis=0, keepdims=True)
        dbm_ref[1:2, :] += jnp.sum(dgb, axis=0, keepdims=True)

    row = pl.BlockSpec((tm, D), lambda i: (i, 0))
    two = pl.BlockSpec((2, D), lambda i: (0, 0))
    gate = pl.BlockSpec((tm, 2 * D), lambda i: (i, 0))
    return pl.pallas_call(
        body, name=name, grid=(Lp // tm,), in_specs=[gate, row, row, two, row], out_specs=[row, row, gate, two],
        out_shape=[SDS((Lp, D), BF16), SDS((Lp, D), BF16), SDS((Lp, HP), BF16), SDS((2, D), F32)])(H, pa, pb, bm, dy)


def _gla_prep(name, H, w2p, b2p):
    Lp = H.shape[0]
    tm = _rt(Lp, 384)

    def body(qk_ref, v_ref, lr_ref, w_ref, b_ref, q_o, k_o, v_o, gf_o, gb_o):
        valid = _row_ids(pl.program_id(0), tm) >= PAD
        qk = qk_ref[...]
        vv = v_ref[...]
        pre = jnp.dot(lr_ref[...].astype(BF16), w_ref[...], preferred_element_type=F32) + b_ref[...]
        g = jnp.where(valid, jax.nn.log_sigmoid(pre) * (1.0 / GLA_TAU), 0.0)
        for hh in range(GLA_H):
            q_o[hh] = qk[:, 64 * hh:64 * hh + 64] * (GLA_DK ** -0.5)
            k_o[hh] = qk[:, 256 + 64 * hh:256 + 64 * hh + 64]
            v_o[hh] = vv[:, 128 * hh:128 * hh + 128].astype(BF16)
            gf_o[hh] = g[:, 64 * hh:64 * hh + 64]
            gb_o[hh] = g[:, 256 + 64 * hh:256 + 64 * hh + 64]

    h64 = pl.BlockSpec((GLA_H, tm, 64), lambda i: (0, i, 0))
    h128 = pl.BlockSpec((GLA_H, tm, 128), lambda i: (0, i, 0))
    return pl.pallas_call(
        body, name=name, grid=(Lp // tm,),
        in_specs=[pl.BlockSpec((tm, 512), lambda i: (i, C_QA // 512)), pl.BlockSpec((tm, 512), lambda i: (i, C_VA // 512)),
                  pl.BlockSpec((tm, LANE), lambda i: (i, C_LR // LANE)), pl.BlockSpec((LANE, 512), lambda i: (0, 0)),
                  pl.BlockSpec((1, 512), lambda i: (0, 0))],
        out_specs=[h64, h64, h128, h64, h64],
        out_shape=[SDS((GLA_H, Lp, 64), F32), SDS((GLA_H, Lp, 64), F32), SDS((GLA_H, Lp, 128), BF16),
                   SDS((GLA_H, Lp, 64), F32), SDS((GLA_H, Lp, 64), F32)])(H, H, H, w2p, b2p)


def _bdot(a, b, ca, cb, precision=None):
    return lax.dot_general(a, b, ((ca, cb), ((0,), (0,))), precision=precision, preferred_element_type=F32)


def _gla_chunk_terms(q_ref, k_ref, g_ref, v_ref, G, rev):
    B = GLA_H * G
    qv = q_ref[...].reshape(B, CHUNK, GLA_DK)
    kv = k_ref[...].reshape(B, CHUNK, GLA_DK)
    gv = g_ref[...].reshape(B, CHUNK, GLA_DK)
    vv = v_ref[...].reshape(B, CHUNK, GLA_DV)
    ii = lax.broadcasted_iota(jnp.int32, (CHUNK, CHUNK), 0)
    jj = lax.broadcasted_iota(jnp.int32, (CHUNK, CHUNK), 1)
    tri = (jj >= ii) if rev else (jj <= ii)
    tb = jnp.broadcast_to(tri.astype(F32)[None], (B, CHUNK, CHUNK))
    bc = _bdot(tb, gv, (2,), (1,), HIGHEST)
    bt = bc[:, 0:1, :] if rev else bc[:, CHUNK - 1:CHUNK, :]
    eq, eki, eke = jnp.exp(bc), jnp.exp(-bc), jnp.exp(bt - bc)
    qd, ki, ke = qv * eq, kv * eki, kv * eke
    att = jnp.where(tri[None], _bdot(qd.astype(BF16), ki.astype(BF16), (2,), (2,)), 0.0)
    dm = jnp.exp(_bdot(gv, jnp.ones((B, CHUNK, GLA_DV), F32), (1,), (1,), HIGHEST))
    return dict(B=B, vv=vv, tri=tri, tb=tb, bt=bt, eq=eq, eki=eki, eke=eke, qd=qd, ki=ki, ke=ke, att=att, dm=dm)


def _gla_fwd(name, q, k, g, v, rev, G):
    Lp = q.shape[1]
    tg = G * CHUNK
    ng = Lp // tg

    def body(q_ref, k_ref, g_ref, v_ref, o_ref, ss_ref, s_scr):
        @pl.when(pl.program_id(0) == 0)
        def _():
            s_scr[...] = jnp.zeros_like(s_scr)

        t = _gla_chunk_terms(q_ref, k_ref, g_ref, v_ref, G, rev)
        B, vv = t["B"], t["vv"]
        qd = t["qd"].astype(BF16)
        oi = _bdot(t["att"].astype(BF16), vv, (2,), (1,))
        kvc = _bdot(t["ke"].astype(BF16), vv, (1,), (1,)).reshape(GLA_H, G, GLA_DK, GLA_DV)
        dm = t["dm"].reshape(GLA_H, G, GLA_DK, GLA_DV)
        s = s_scr[...]
        sp = [None] * G
        for c in (range(G - 1, -1, -1) if rev else range(G)):
            sp[c] = s
            ss_ref[c] = s
            s = dm[:, c] * s + kvc[:, c]
        s_scr[...] = s
        spb = jnp.stack(sp, axis=1).reshape(B, GLA_DK, GLA_DV).astype(BF16)
        o_ref[...] = (oi + _bdot(qd, spb, (2,), (1,))).reshape(GLA_H, tg, GLA_DV)

    blk = (lambda i: (0, ng - 1 - i, 0)) if rev else (lambda i: (0, i, 0))
    sblk = (lambda i: (ng - 1 - i, 0, 0, 0)) if rev else (lambda i: (i, 0, 0, 0))
    h64 = pl.BlockSpec((GLA_H, tg, 64), blk)
    h128 = pl.BlockSpec((GLA_H, tg, 128), blk)
    return pl.pallas_call(
        body, name=name, grid=(ng,), in_specs=[h64, h64, h64, h128],
        out_specs=[h128, pl.BlockSpec((G, GLA_H, GLA_DK, GLA_DV), sblk)],
        out_shape=[SDS((GLA_H, Lp, GLA_DV), F32), SDS((Lp // CHUNK, GLA_H, GLA_DK, GLA_DV), F32)],
        scratch_shapes=[pltpu.VMEM((GLA_H, GLA_DK, GLA_DV), F32)], compiler_params=_cp(True))(q, k, g, v)


def _gla_bwd(name, q, k, g, v, ss, do, rev, G):
    Lp = q.shape[1]
    tg = G * CHUNK
    ng = Lp // tg

    def body(q_ref, k_ref, g_ref, v_ref, ss_ref, do_ref, dq_ref, dk_ref, dg_ref, dv_ref, ds_scr):
        @pl.when(pl.program_id(0) == 0)
        def _():
            ds_scr[...] = jnp.zeros_like(ds_scr)

        t = _gla_chunk_terms(q_ref, k_ref, g_ref, v_ref, G, rev)
        B, vv, tri = t["B"], t["vv"], t["tri"]
        qd, ki, ke = t["qd"], t["ki"], t["ke"]
        qdb, kib, keb = qd.astype(BF16), ki.astype(BF16), ke.astype(BF16)
        sp = jnp.stack([ss_ref[c] for c in range(G)], axis=1).reshape(B, GLA_DK, GLA_DV)
        dob = do_ref[...].reshape(B, CHUNK, GLA_DV).astype(BF16)
        da = jnp.where(tri[None], _bdot(dob, vv, (2,), (2,)), 0.0).astype(BF16)
        dqd = _bdot(da, kib, (2,), (1,)) + _bdot(dob, sp.astype(BF16), (2,), (2,))
        dki = _bdot(da, qdb, (1,), (1,))
        dv = _bdot(t["att"].astype(BF16), dob, (1,), (1,))
        cc = _bdot(qdb, dob, (1,), (1,)).reshape(GLA_H, G, GLA_DK, GLA_DV)
        dm = t["dm"].reshape(GLA_H, G, GLA_DK, GLA_DV)
        dsc = ds_scr[...]
        dsn = [None] * G
        for c in (range(G) if rev else range(G - 1, -1, -1)):
            dsn[c] = dsc
            dsc = dm[:, c] * dsc + cc[:, c]
        ds_scr[...] = dsc
        dsn = jnp.stack(dsn, axis=1).reshape(B, GLA_DK, GLA_DV)
        dsnb = dsn.astype(BF16)
        dv = dv + _bdot(keb, dsnb, (2,), (1,))
        dke = _bdot(vv, dsnb, (2,), (2,))
        ddrow = _bdot(jnp.ones((B, CHUNK, GLA_DV), F32), dsn * sp, (2,), (2,), HIGHEST)
        dbt = ddrow * jnp.exp(t["bt"]) + jnp.sum(dke * ke, axis=1, keepdims=True)
        db = dqd * qd - dki * ki - dke * ke
        dq_ref[...] = (dqd * t["eq"]).reshape(GLA_H, tg, GLA_DK)
        dk_ref[...] = (dki * t["eki"] + dke * t["eke"]).reshape(GLA_H, tg, GLA_DK)
        dg_ref[...] = (_bdot(t["tb"], db, (1,), (1,), HIGHEST) + dbt).reshape(GLA_H, tg, GLA_DK)
        dv_ref[...] = dv.reshape(GLA_H, tg, GLA_DV)

    blk = (lambda i: (0, i, 0)) if rev else (lambda i: (0, ng - 1 - i, 0))
    sblk = (lambda i: (i, 0, 0, 0)) if rev else (lambda i: (ng - 1 - i, 0, 0, 0))
    h64 = pl.BlockSpec((GLA_H, tg, 64), blk)
    h128 = pl.BlockSpec((GLA_H, tg, 128), blk)
    return pl.pallas_call(
        body, name=name, grid=(ng,),
        in_specs=[h64, h64, h64, h128, pl.BlockSpec((G, GLA_H, GLA_DK, GLA_DV), sblk), h128],
        out_specs=[h64, h64, h64, h128],
        out_shape=[SDS((GLA_H, Lp, 64), F32), SDS((GLA_H, Lp, 64), F32), SDS((GLA_H, Lp, 64), F32),
                   SDS((GLA_H, Lp, GLA_DV), F32)],
        scratch_shapes=[pltpu.VMEM((GLA_H, GLA_DK, GLA_DV), F32)], compiler_params=_cp(True))(q, k, g, v, ss, do)


def _gla_post(name, of, ob, H, gn):
    Lp = H.shape[0]
    tm = _rt(Lp, 384)

    def body(of_ref, ob_ref, r_ref, gn_ref, a_ref, at_ref):
        parts = []
        for hh in range(GLA_H):
            o = of_ref[hh] + ob_ref[hh]
            parts.append(o * lax.rsqrt(jnp.mean(o * o, axis=-1, keepdims=True) + EPS))
        rv = r_ref[...]
        a = (jnp.concatenate(parts, axis=1) * gn_ref[...]) * (rv * jax.nn.sigmoid(rv))
        a_ref[...] = a.astype(BF16)
        at_ref[...] = a.T.astype(BF16)

    h128 = pl.BlockSpec((GLA_H, tm, 128), lambda i: (0, i, 0))
    return pl.pallas_call(
        body, name=name, grid=(Lp // tm,),
        in_specs=[h128, h128, pl.BlockSpec((tm, 512), lambda i: (i, C_RA // 512)), pl.BlockSpec((1, 512), lambda i: (0, 0))],
        out_specs=[pl.BlockSpec((tm, 512), lambda i: (i, 0)), pl.BlockSpec((512, tm), lambda i: (0, i))],
        out_shape=[SDS((Lp, 512), BF16), SDS((512, Lp), BF16)])(of, ob, H, gn)


def _gla_post_bwd(name, of, ob, H, gn, da, dH):
    Lp = H.shape[0]
    tm = _rt(Lp, 384)

    def body(of_ref, ob_ref, r_ref, gn_ref, da_ref, dh_in, do_ref, dh_ref, dgn_ref):
        rv = r_ref[...]
        sg = jax.nn.sigmoid(rv)
        dav = da_ref[...]
        gnv = gn_ref[...]
        ons, rs = [], []
        for hh in range(GLA_H):
            o = of_ref[hh] + ob_ref[hh]
            r = lax.rsqrt(jnp.mean(o * o, axis=-1, keepdims=True) + EPS)
            rs.append(r)
            ons.append(o * r)
        on = jnp.concatenate(ons, axis=1)
        dw = dav * (rv * sg)
        dh_ref[...] = (dav * (on * gnv) * (sg * (1.0 + rv * (1.0 - sg)))).astype(BF16)

        @pl.when(pl.program_id(0) == 0)
        def _():
            dgn_ref[...] = jnp.zeros_like(dgn_ref)

        dgn_ref[...] += jnp.sum(dw * on, axis=0, keepdims=True)
        don = dw * gnv
        for hh in range(GLA_H):
            dd = don[:, 128 * hh:128 * hh + 128]
            do_ref[hh] = rs[hh] * (dd - ons[hh] * jnp.mean(dd * ons[hh], axis=-1, keepdims=True))

    h128 = pl.BlockSpec((GLA_H, tm, 128), lambda i: (0, i, 0))
    rblk = pl.BlockSpec((tm, 512), lambda i: (i, C_RA // 512))
    vec = pl.BlockSpec((1, 512), lambda i: (0, 0))
    return pl.pallas_call(
        body, name=name, grid=(Lp // tm,),
        in_specs=[h128, h128, rblk, vec, pl.BlockSpec((tm, 512), lambda i: (i, 0)), pl.BlockSpec(memory_space=pl.ANY)],
        out_specs=[h128, rblk, vec],
        out_shape=[SDS((GLA_H, Lp, 128), F32), SDS(dH.shape, BF16), SDS((1, 512), F32)],
        input_output_aliases={5: 1})(of, ob, H, gn, da, dH)


def _gla_qkv_bwd(name, dqf, dqb, dkf, dkb, dvf, dvb, dH):
    Lp = dqf.shape[1]
    tm = _rt(Lp, 384)

    def body(dqf_ref, dqb_ref, dkf_ref, dkb_ref, dvf_ref, dvb_ref, dh_in, dh_ref):
        valid = _row_ids(pl.program_id(0), tm) >= PAD
        for hh in range(GLA_H):
            dq = (dqf_ref[hh] + dqb_ref[hh]) * (GLA_DK ** -0.5)
            dh_ref[:, 64 * hh:64 * hh + 64] = jnp.where(valid, dq, 0.0).astype(BF16)
            dh_ref[:, 256 + 64 * hh:256 + 64 * hh + 64] = jnp.where(valid, dkf_ref[hh] + dkb_ref[hh], 0.0).astype(BF16)
            dh_ref[:, 512 + 128 * hh:512 + 128 * hh + 128] = jnp.where(valid, dvf_ref[hh] + dvb_ref[hh], 0.0).astype(BF16)

    h64 = pl.BlockSpec((GLA_H, tm, 64), lambda i: (0, i, 0))
    h128 = pl.BlockSpec((GLA_H, tm, 128), lambda i: (0, i, 0))
    return pl.pallas_call(
        body, name=name, grid=(Lp // tm,),
        in_specs=[h64, h64, h64, h64, h128, h128, pl.BlockSpec(memory_space=pl.ANY)],
        out_specs=pl.BlockSpec((tm, 1024), lambda i: (i, C_QA // 1024)), out_shape=SDS(dH.shape, BF16),
        input_output_aliases={6: 0})(dqf, dqb, dkf, dkb, dvf, dvb, dH)


def _gla_gate_bwd(name, H, w2p, b2p, dgf, dgb, dH):
    Lp = H.shape[0]
    tm = _rt(Lp, 384)

    def body(lr_ref, w_ref, b_ref, dgf_ref, dgb_ref, dh_in, dh_ref, dw_ref, db_ref, dg_scr):
        valid = _row_ids(pl.program_id(0), tm) >= PAD
        for hh in range(GLA_H):
            dg_scr[:, 64 * hh:64 * hh + 64] = dgf_ref[hh]
            dg_scr[:, 256 + 64 * hh:256 + 64 * hh + 64] = dgb_ref[hh]
        lrb = lr_ref[...].astype(BF16)
        wv = w_ref[...]
        pre = jnp.dot(lrb, wv, preferred_element_type=F32) + b_ref[...]
        dpre = jnp.where(valid, dg_scr[...] * (1.0 / GLA_TAU) * jax.nn.sigmoid(-pre), 0.0)
        dpb = dpre.astype(BF16)
        dh_ref[...] = lax.dot_general(dpb, wv, NT, preferred_element_type=F32).astype(BF16)

        @pl.when(pl.program_id(0) == 0)
        def _():
            dw_ref[...] = jnp.zeros_like(dw_ref)
            db_ref[...] = jnp.zeros_like(db_ref)

        dw_ref[...] += lax.dot_general(lrb, dpb, TN, preferred_element_type=F32)
        db_ref[...] += jnp.sum(dpre, axis=0, keepdims=True)

    h64 = pl.BlockSpec((GLA_H, tm, 64), lambda i: (0, i, 0))
    lrblk = pl.BlockSpec((tm, LANE), lambda i: (i, C_LR // LANE))
    wblk = pl.BlockSpec((LANE, 512), lambda i: (0, 0))
    vec = pl.BlockSpec((1, 512), lambda i: (0, 0))
    return pl.pallas_call(
        body, name=name, grid=(Lp // tm,),
        in_specs=[lrblk, wblk, vec, h64, h64, pl.BlockSpec(memory_space=pl.ANY)],
        out_specs=[lrblk, wblk, vec],
        out_shape=[SDS(dH.shape, BF16), SDS((LANE, 512), F32), SDS((1, 512), F32)],
        scratch_shapes=[pltpu.VMEM((tm, 512), F32)], input_output_aliases={5: 0})(H, w2p, b2p, dgf, dgb, dH)


def _swap16(x):
    n = x.shape[1]
    lane = lax.broadcasted_iota(jnp.int32, x.shape, 1)
    return jnp.where(lane % 32 < 16, pltpu.roll(x, n - 16, 1), pltpu.roll(x, 16, 1))


def _head_mean(v, bd):
    hi = v.astype(BF16)
    lo = (v - hi.astype(F32)).astype(BF16)
    bdb = bd.astype(BF16)
    return jnp.dot(hi, bdb, preferred_element_type=F32) + jnp.dot(lo, bdb, preferred_element_type=F32)


def _headnorm_rope(x, gain, cos, sin, bd):
    r = lax.rsqrt(_head_mean(x * x, bd) + EPS)
    xh = x * r
    xn = xh * gain
    return xn * cos + _swap16(xn) * sin, xh, r


def _headnorm_rope_bwd(dxr, xh, r, gain, cos, sin, bd):
    dxn = cos * dxr + _swap16(sin * dxr)
    u = dxn * gain
    dx = r * (u - xh * _head_mean(u * xh, bd))
    return dx, jnp.sum(dxn * xh, axis=0, keepdims=True)


def _att_prep(name, H, gq, gk, cos, sin, bd):
    Lp = H.shape[0]
    tm = _rt(Lp, 384)

    def body(q_ref, kv_ref, gq_ref, gk_ref, c_ref, s_ref, bd_ref, qp_ref, k_ref, v_ref):
        c1, s1 = c_ref[...], s_ref[...]
        c4, s4 = jnp.concatenate([c1] * 4, axis=1), jnp.concatenate([s1] * 4, axis=1)
        xr, _, _ = _headnorm_rope(q_ref[...], gq_ref[...], c4, s4, bd_ref[...])
        xr = xr * (HD ** -0.5)
        lane = lax.broadcasted_iota(jnp.int32, (tm, LANE), 1)
        for hh in range(ATT_H):
            grp = xr[:, LANE * (hh // 2):LANE * (hh // 2) + LANE]
            e, gi = hh % 2, hh // 4
            if e != gi:
                grp = pltpu.roll(grp, 64, 1)
            keep = (lane < 64) if gi == 0 else (lane >= 64)
            qp_ref[hh] = jnp.where(keep, grp, 0.0).astype(BF16)
        kv = kv_ref[...]
        kr, _, _ = _headnorm_rope(kv[:, :LANE], gk_ref[...], c1, s1, bd_ref[0:LANE, 0:LANE])
        k_ref[...] = kr.astype(BF16)
        v_ref[...] = kv[:, LANE:].astype(BF16)

    row128 = pl.BlockSpec((tm, LANE), lambda i: (i, 0))
    return pl.pallas_call(
        body, name=name, grid=(Lp // tm,),
        in_specs=[pl.BlockSpec((tm, 512), lambda i: (i, C_QB // 512)), pl.BlockSpec((tm, 256), lambda i: (i, C_KB // 256)),
                  pl.BlockSpec((1, 512), lambda i: (0, 0)), pl.BlockSpec((1, LANE), lambda i: (0, 0)), row128, row128,
                  pl.BlockSpec((512, 512), lambda i: (0, 0))],
        out_specs=[pl.BlockSpec((ATT_H, tm, LANE), lambda i: (0, i, 0)), row128, row128],
        out_shape=[SDS((ATT_H, Lp, LANE), BF16), SDS((Lp, LANE), BF16), SDS((Lp, LANE), BF16)])(H, H, gq, gk, cos, sin, bd)


def _att_q_bwd(name, H, gq, cos, sin, bd, dqp, dH):
    Lp = H.shape[0]
    tm = _rt(Lp, 384)

    def body(q_ref, gq_ref, c_ref, s_ref, bd_ref, dqp_ref, dh_in, dh_ref, dg_ref):
        c1, s1 = c_ref[...], s_ref[...]
        c4, s4 = jnp.concatenate([c1] * 4, axis=1), jnp.concatenate([s1] * 4, axis=1)
        gqv = gq_ref[...]
        _, xh, r = _headnorm_rope(q_ref[...], gqv, c4, s4, bd_ref[...])
        lane = lax.broadcasted_iota(jnp.int32, (tm, LANE), 1)
        groups = []
        for j in range(ATT_H // 2):
            pieces = []
            for e in range(2):
                hh = 2 * j + e
                piece = dqp_ref[hh]
                if e != hh // 4:
                    piece = pltpu.roll(piece, 64, 1)
                pieces.append(piece)
            groups.append(jnp.where(lane < 64, pieces[0], pieces[1]))
        dxr = jnp.concatenate(groups, axis=1) * (HD ** -0.5)
        dx, dg = _headnorm_rope_bwd(dxr, xh, r, gqv, c4, s4, bd_ref[...])
        dh_ref[...] = dx.astype(BF16)

        @pl.when(pl.program_id(0) == 0)
        def _():
            dg_ref[...] = jnp.zeros_like(dg_ref)

        dg_ref[...] += dg

    row128 = pl.BlockSpec((tm, LANE), lambda i: (i, 0))
    qblk = pl.BlockSpec((tm, 512), lambda i: (i, C_QB // 512))
    vec = pl.BlockSpec((1, 512), lambda i: (0, 0))
    return pl.pallas_call(
        body, name=name, grid=(Lp // tm,),
        in_specs=[qblk, vec, row128, row128, pl.BlockSpec((512, 512), lambda i: (0, 0)),
                  pl.BlockSpec((ATT_H, tm, LANE), lambda i: (0, i, 0)), pl.BlockSpec(memory_space=pl.ANY)],
        out_specs=[qblk, vec], out_shape=[SDS(dH.shape, BF16), SDS((1, 512), F32)],
        input_output_aliases={6: 0})(H, gq, cos, sin, bd, dqp, dH)


def _att_kv_bwd(name, H, gk, cos, sin, bd, dkr, dvb, dH):
    Lp = H.shape[0]
    tm = _rt(Lp, 384)

    def body(kv_ref, gk_ref, c_ref, s_ref, bd_ref, dk_ref, dv_ref, dh_in, dh_ref, dg_ref):
        c1, s1 = c_ref[...], s_ref[...]
        gkv = gk_ref[...]
        bdv = bd_ref[0:LANE, 0:LANE]
        _, xh, r = _headnorm_rope(kv_ref[:, :LANE], gkv, c1, s1, bdv)
        dx, dg = _headnorm_rope_bwd(dk_ref[...], xh, r, gkv, c1, s1, bdv)
        dh_ref[:, :LANE] = dx.astype(BF16)
        dh_ref[:, LANE:] = dv_ref[...].astype(BF16)

        @pl.when(pl.program_id(0) == 0)
        def _():
            dg_ref[...] = jnp.zeros_like(dg_ref)

        dg_ref[...] += dg

    row128 = pl.BlockSpec((tm, LANE), lambda i: (i, 0))
    kvblk = pl.BlockSpec((tm, 256), lambda i: (i, C_KB // 256))
    vec = pl.BlockSpec((1, LANE), lambda i: (0, 0))
    return pl.pallas_call(
        body, name=name, grid=(Lp // tm,),
        in_specs=[kvblk, vec, row128, row128, pl.BlockSpec((512, 512), lambda i: (0, 0)), row128, row128,
                  pl.BlockSpec(memory_space=pl.ANY)],
        out_specs=[kvblk, vec], out_shape=[SDS(dH.shape, BF16), SDS((1, LANE), F32)],
        input_output_aliases={7: 0})(H, gk, cos, sin, bd, dkr, dvb, dH)


def _ridden_call(core, rider, *, name, grid, in_specs, out_specs, out_shape, scratch_shapes, args, aliases=None,
                 mid_back=RIDER_MID_BACK):
    n_in, n_out, n_scr = len(in_specs), len(out_specs), len(scratch_shapes)
    r_in = len(rider.ins) if rider else 0
    r_out = len(rider.out_shape) if rider else 0
    total = int(np.prod(grid))
    mid_step = max(total - 1 - mid_back, 0)

    def body(*refs):
        ins, r_ins = refs[:n_in], refs[n_in:n_in + r_in]
        o0 = n_in + r_in
        outs, r_outs = refs[o0:o0 + n_out], refs[o0 + n_out:o0 + n_out + r_out]
        s0 = o0 + n_out + r_out
        scr, r_sems = refs[s0:s0 + n_scr], refs[s0 + n_scr:]
        if rider is None:
            core(ins, outs, scr)
            return
        step = pl.program_id(0)
        for ax in range(1, len(grid)):
            step = step * grid[ax] + pl.program_id(ax)
        start, mid, end = rider.hooks(r_ins, r_outs, r_sems)
        pl.when(step == 0)(start)
        core(ins, outs, scr)
        pl.when(step == mid_step)(mid)
        pl.when(step == total - 1)(end)

    any_spec = pl.BlockSpec(memory_space=pl.ANY)
    res = pl.pallas_call(
        body, name=name, grid=grid, in_specs=list(in_specs) + [any_spec] * r_in, out_specs=list(out_specs) + [any_spec] * r_out,
        out_shape=list(out_shape) + (list(rider.out_shape) if rider else []),
        scratch_shapes=list(scratch_shapes) + (list(rider.sems) if rider else []),
        input_output_aliases={**(aliases or {}), **({n_in + a: n_out + b for a, b in rider.aliases.items()} if rider else {})},
        compiler_params=_cp(True))(*args, *(rider.ins if rider else []))
    return res[:n_out], res[n_out:]


def _att_fwd(name, qp, kr, vb, bias, rider=None):
    Lp = kr.shape[0]
    tq = _rt(Lp, 384)
    sub = ATT_SUB_FWD if tq % ATT_SUB_FWD == 0 else tq

    def core(ins, outs, scr):
        q_ref, k_ref, v_ref, bias_ref = ins
        b_ref, bt_ref, lse_ref = outs
        j, i = pl.program_id(0), pl.program_id(1)
        valid = _row_ids(i, tq) >= PAD
        kk, vv, bb = k_ref[...], v_ref[...], bias_ref[...]
        chains = [(e, r) for e in range(2) for r in range(tq // sub)]
        ss = [lax.dot_general(q_ref[e, r * sub:(r + 1) * sub, :], kk, NT, preferred_element_type=F32) for e, r in chains]
        ps, ls = [], []
        for (e, r), s in zip(chains, ss):
            s = jnp.concatenate([s[:, :LANE] + bb[:, :LANE], s[:, LANE:]], axis=1)
            m = jnp.max(s, axis=-1, keepdims=True)
            p = jnp.exp(s - m)
            l = jnp.sum(p, axis=-1, keepdims=True)
            ps.append(p.astype(BF16))
            ls.append(l)
            lse_ref[e, r * sub:(r + 1) * sub, :] = m + jnp.log(l)
        os_ = [jnp.dot(p, vv, preferred_element_type=F32) * (1.0 / l) for p, l in zip(ps, ls)]
        n_sub = tq // sub
        outs = [jnp.where(valid, jnp.concatenate(os_[e * n_sub:(e + 1) * n_sub], axis=0), 0.0) for e in range(2)]
        lane = lax.broadcasted_iota(jnp.int32, (tq, LANE), 1)
        low = j < 2
        o0 = jnp.where(low, outs[0], pltpu.roll(outs[0], 64, 1))
        o1 = jnp.where(low, pltpu.roll(outs[1], 64, 1), outs[1])
        blk = jnp.where(lane < 64, o0, o1)
        b_ref[...] = blk.astype(BF16)
        bt_ref[...] = blk.T.astype(BF16)

    full = pl.BlockSpec((Lp, LANE), lambda j, i: (0, 0))
    return _ridden_call(
        core, rider, name=name, grid=(ATT_H // 2, Lp // tq),
        in_specs=[pl.BlockSpec((2, tq, LANE), lambda j, i: (j, i, 0)), full, full, pl.BlockSpec((1, Lp), lambda j, i: (0, 0))],
        out_specs=[pl.BlockSpec((tq, LANE), lambda j, i: (i, j)), pl.BlockSpec((LANE, tq), lambda j, i: (j, i)),
                   pl.BlockSpec((2, tq, 1), lambda j, i: (j, i, 0))],
        out_shape=[SDS((Lp, 512), BF16), SDS((512, Lp), BF16), SDS((ATT_H, Lp, 1), F32)],
        scratch_shapes=[], args=(qp, kr, vb, bias))


def _att_bwd(name, qp, kr, vb, bias, lse, db, b, rider=None):
    Lp = kr.shape[0]
    tq = _rt(Lp, 384)
    sub = ATT_SUB_BWD if tq % ATT_SUB_BWD == 0 else tq
    nq = Lp // tq

    def core(ins, outs, scr):
        q_ref, k_ref, v_ref, bias_ref, lse_ref, db_ref, b_ref = ins
        dq_ref, dk_ref, dv_ref = outs
        dkt_scr, dvt_scr = scr
        j, i = pl.program_id(0), pl.program_id(1)

        @pl.when((j == 0) & (i == 0))
        def _():
            dkt_scr[...] = jnp.zeros_like(dkt_scr)
            dvt_scr[...] = jnp.zeros_like(dvt_scr)

        kk, vv, bb = k_ref[...], v_ref[...], bias_ref[...]
        dbv = db_ref[...]
        rolled = pltpu.roll(dbv, 64, 1)
        lane = lax.broadcasted_iota(jnp.int32, (tq, LANE), 1)
        low = j < 2
        first = jnp.where(low, 0, 64)
        keep = (lane >= first) & (lane < first + 64)
        prod = dbv * b_ref[...].astype(F32)
        deltas = [jnp.sum(jnp.where(lane < 64, prod, 0.0), axis=-1, keepdims=True),
                  jnp.sum(jnp.where(lane >= 64, prod, 0.0), axis=-1, keepdims=True)]
        dkt, dvt = 0.0, 0.0
        for e in range(2):
            src = jnp.where(low, dbv, rolled) if e == 0 else jnp.where(low, rolled, dbv)
            dop = jnp.where(keep, src, 0.0).astype(BF16)
            pbs, dss = [], []
            for r in range(tq // sub):
                rs = slice(r * sub, (r + 1) * sub)
                s = lax.dot_general(q_ref[e, rs, :], kk, NT, preferred_element_type=F32)
                s = jnp.concatenate([s[:, :LANE] + bb[:, :LANE], s[:, LANE:]], axis=1)
                p = jnp.exp(s - lse_ref[e, rs, :])
                dp = lax.dot_general(dop[rs], vv, NT, preferred_element_type=F32)
                ds = (p * (dp - deltas[e][rs])).astype(BF16)
                dq_ref[e, rs, :] = jnp.where(keep[rs], jnp.dot(ds, kk, preferred_element_type=F32), 0.0)
                pbs.append(p.astype(BF16))
                dss.append(ds)
            dkt = dkt + lax.dot_general(q_ref[e], jnp.concatenate(dss, axis=0), TN, preferred_element_type=F32)
            dvt = dvt + lax.dot_general(dop, jnp.concatenate(pbs, axis=0), TN, preferred_element_type=F32)
        dkt_scr[...] += dkt
        dvt_scr[...] += dvt

        @pl.when((j == ATT_H // 2 - 1) & (i == nq - 1))
        def _():
            dk_ref[...] = dkt_scr[...].T
            dv_ref[...] = dvt_scr[...].T

    full = pl.BlockSpec((Lp, LANE), lambda j, i: (0, 0))
    pair = pl.BlockSpec((2, tq, LANE), lambda j, i: (j, i, 0))
    return _ridden_call(
        core, rider, name=name, grid=(ATT_H // 2, nq),
        in_specs=[pair, full, full, pl.BlockSpec((1, Lp), lambda j, i: (0, 0)), pl.BlockSpec((2, tq, 1), lambda j, i: (j, i, 0)),
                  pl.BlockSpec((tq, LANE), lambda j, i: (i, j)), pl.BlockSpec((tq, LANE), lambda j, i: (i, j))],
        out_specs=[pair, full, full],
        out_shape=[SDS((ATT_H, Lp, LANE), F32), SDS((Lp, LANE), F32), SDS((Lp, LANE), F32)],
        scratch_shapes=[pltpu.VMEM((LANE, Lp), F32), pltpu.VMEM((LANE, Lp), F32)], args=(qp, kr, vb, bias, lse, db, b))


def _rope_tables(n_tok):
    f32 = np.float32
    rows = n_tok // GRID_W
    row = np.repeat(np.arange(rows), GRID_W).astype(f32)
    col = np.tile(np.arange(GRID_W), rows).astype(f32)
    axis_dim = HD // 2
    inv = np.power(f32(ROPE_THETA), -np.arange(0, axis_dim, 2, dtype=f32) / f32(axis_dim)).astype(f32)
    ang = np.concatenate([row[:, None] * inv, col[:, None] * inv], axis=-1).astype(f32)
    ang = np.concatenate([np.zeros((LANE, axis_dim), f32), ang], axis=0)
    c, s = np.cos(ang).astype(f32), np.sin(ang).astype(f32)
    c64 = np.concatenate([c[:, :16], c[:, :16], c[:, 16:], c[:, 16:]], axis=1)
    s64 = np.concatenate([-s[:, :16], s[:, :16], -s[:, 16:], s[:, 16:]], axis=1)
    return jnp.asarray(np.concatenate([c64, c64], axis=1)), jnp.asarray(np.concatenate([s64, s64], axis=1))


def _ffn_fwd(tag, h, W, l, j, rider=None):
    Lp = h.shape[0]
    tm = _rt(Lp, 384)
    g = W["norm_gains"][l, 2 * j].reshape(1, D)

    def core(ins, outs, scr):
        h_ref, g_ref, wg_ref, wu_ref, wd_ref = ins
        h2_ref, a_ref, b_ref, st_ref, nt_ref = outs
        pad_scr, = scr
        x = h_ref[...]
        y = x * lax.rsqrt(jnp.mean(x * x, axis=-1, keepdims=True) + EPS) * g_ref[...]
        nt_ref[...] = y.T.astype(BF16)
        nv = y.astype(BF16)
        pad_scr[:, FF_S:] = jnp.zeros((tm, FF_P - FF_S), F32)
        acc = 0.0
        for s in range(NSH):
            a = jnp.dot(nv, wg_ref[s], preferred_element_type=F32)
            b = jnp.dot(nv, wu_ref[s], preferred_element_type=F32)
            sv = a * jax.nn.sigmoid(a) * b
            a_ref[s, :, :FF_S] = a.astype(BF16)
            a_ref[s, :, FF_S:] = jnp.zeros((tm, FF_P - FF_S), BF16)
            b_ref[s, :, :FF_S] = b.astype(BF16)
            b_ref[s, :, FF_S:] = jnp.zeros((tm, FF_P - FF_S), BF16)
            pad_scr[:, :FF_S] = sv
            st_ref[s] = pad_scr[...].T.astype(BF16)
            acc = acc + jnp.dot(sv.astype(BF16), wd_ref[s], preferred_element_type=F32)
        h2_ref[...] = x + 0.5 * acc

    once = dict(pipeline_mode=pl.Buffered(1))
    wup = pl.BlockSpec((NSH, None, D, FF_S), lambda i: (0, l, j, 0), **once)
    row = pl.BlockSpec((tm, D), lambda i: (i, 0))
    slab = pl.BlockSpec((NSH, tm, FF_P), lambda i: (0, i, 0))
    (h2, a, b, st, nt), carried = _ridden_call(
        core, rider, name=f"ffn_{tag}", grid=(Lp // tm,),
        in_specs=[row, pl.BlockSpec((1, D), lambda i: (0, 0)), wup, wup,
                  pl.BlockSpec((NSH, None, FF_S, D), lambda i: (0, l, j, 0), **once)],
        out_specs=[row, slab, slab, pl.BlockSpec((NSH, FF_P, tm), lambda i: (0, 0, i)), pl.BlockSpec((D, tm), lambda i: (0, i))],
        out_shape=[SDS((Lp, D), F32), SDS((NSH, Lp, FF_P), BF16), SDS((NSH, Lp, FF_P), BF16), SDS((NSH, FF_P, Lp), BF16),
                   SDS((D, Lp), BF16)],
        scratch_shapes=[pltpu.VMEM((tm, FF_P), F32)], args=(h, g, W["gate"], W["up"], W["down"]), mid_back=1)
    if rider is not None:
        W = rider.updated(W, carried)
    return h2, dict(h=h, g=g, nt=nt, a=a, b=b, st=st), W


def _ffn_bwd(tag, dh, sv, W, G, l, j, rider=None):
    Lp = dh.shape[0]
    tn = 512
    (da, db, dh_prev, dg), carried = _ffn_bwd_acts(tag, dh, sv, W, l, j, rider)
    G["down"] = _mm(f"bwd_{tag}", sv["st"], dh, grid=(NSH, D // tn),
                    a_spec=pl.BlockSpec((None, FF_P, Lp), lambda s, n: (s, 0, 0)), a_sl=(slice(0, FF_S), slice(None)),
                    b_spec=pl.BlockSpec((Lp, tn), lambda s, n: (0, n)),
                    o_spec=pl.BlockSpec((None, None, FF_S, tn), lambda s, n: (s, j, 0, n)), o_shape=(NSH, 2, FF_S, D), o_dtype=F32,
                    dims=NN, acc_shape=None, scale=0.5, into=G.get("down"))
    for key, dact in (("gate", da), ("up", db)):
        G[key] = _mm(f"bw{key[0]}_{tag}", sv["nt"], dact, grid=(NSH, D // tn),
                     a_spec=pl.BlockSpec((tn, Lp), lambda s, m: (m, 0)),
                     b_spec=pl.BlockSpec((None, Lp, FF_P), lambda s, m: (s, 0, 0)), b_sl=(slice(None), slice(0, FF_S)),
                     o_spec=pl.BlockSpec((None, None, tn, FF_S), lambda s, m: (s, j, m, 0)), o_shape=(NSH, 2, D, FF_S),
                     o_dtype=F32, dims=NN, acc_shape=None, into=G.get(key))
    return dh_prev, dg, carried


def _ffn_bwd_acts(tag, dh, sv, W, l, j, rider=None):
    Lp = dh.shape[0]
    tm = _rt(Lp, 384)

    def core(ins, outs, scr):
        dh_ref, h_ref, g_ref, wd_ref, wg_ref, wu_ref, a_ref, b_ref = ins
        da_ref, db_ref, dho_ref, dg_ref = outs
        dhv = dh_ref[...]
        dhb = dhv.astype(BF16)
        dn = 0.0
        for s in range(NSH):
            ds = 0.5 * lax.dot_general(dhb, wd_ref[s], NT, preferred_element_type=F32)
            av = a_ref[s, :, :FF_S].astype(F32)
            bv = b_ref[s, :, :FF_S].astype(F32)
            sg = jax.nn.sigmoid(av)
            da = (ds * bv * (sg * (1.0 + av * (1.0 - sg)))).astype(BF16)
            db = (ds * (av * sg)).astype(BF16)
            da_ref[s, :, :FF_S] = da
            da_ref[s, :, FF_S:] = jnp.zeros((tm, FF_P - FF_S), BF16)
            db_ref[s, :, :FF_S] = db
            db_ref[s, :, FF_S:] = jnp.zeros((tm, FF_P - FF_S), BF16)
            dn = dn + (lax.dot_general(da, wg_ref[s], NT, preferred_element_type=F32)
                       + lax.dot_general(db, wu_ref[s], NT, preferred_element_type=F32))
        x = h_ref[...]
        r = lax.rsqrt(jnp.mean(x * x, axis=-1, keepdims=True) + EPS)
        xh = x * r
        u = dn * g_ref[...]
        dho_ref[...] = dhv + r * (u - xh * jnp.mean(u * xh, axis=-1, keepdims=True))

        @pl.when(pl.program_id(0) == 0)
        def _():
            dg_ref[...] = jnp.zeros_like(dg_ref)

        dg_ref[...] += jnp.sum(dn * xh, axis=0, keepdims=True)

    once = dict(pipeline_mode=pl.Buffered(1))
    wup = pl.BlockSpec((NSH, None, D, FF_S), lambda i: (0, l, j, 0), **once)
    row = pl.BlockSpec((tm, D), lambda i: (i, 0))
    slab = pl.BlockSpec((NSH, tm, FF_P), lambda i: (0, i, 0))
    vec = pl.BlockSpec((1, D), lambda i: (0, 0))
    return _ridden_call(
        core, rider, name=f"bffn_{tag}", grid=(Lp // tm,),
        in_specs=[row, row, vec, pl.BlockSpec((NSH, None, FF_S, D), lambda i: (0, l, j, 0), **once), wup, wup, slab, slab],
        out_specs=[slab, slab, row, vec],
        out_shape=[SDS((NSH, Lp, FF_P), BF16), SDS((NSH, Lp, FF_P), BF16), SDS((Lp, D), F32), SDS((1, D), F32)],
        scratch_shapes=[], args=(dh, sv["h"], sv["g"], W["down"], W["gate"], W["up"], sv["a"], sv["b"]))


def _mixer_fwd(tag, h, W, winp, C, l, G_chunks, riders=(None, None)):
    Lp = h.shape[0]
    tm = _rt(Lp, 1408)
    g = W["norm_gains"][l, 1].reshape(1, D)
    (H, zt), carried = _norm_project(f"win_{tag}", h, g, winp, riders[0])
    if riders[0] is not None:
        W = riders[0].updated(W, carried)
    rider = riders[1]
    w2p, b2p = C["w2p"][l], C["b2p"][l]
    qh, kh, vh, gf, gb = _gla_prep(f"glap_{tag}", H, w2p, b2p)
    of, sf = _gla_fwd(f"glaf_{tag}", qh, kh, gf, vh, False, G_chunks)
    ob, sb = _gla_fwd(f"glar_{tag}", qh, kh, gb, vh, True, G_chunks)
    gn = W["gn"][l].reshape(1, 512)
    a, at = _gla_post(f"glao_{tag}", of, ob, H, gn)
    gq, gk = C["gq"][l], C["gk"][l]
    qp, kr, vb = _att_prep(f"attp_{tag}", H, gq, gk, C["cos"], C["sin"], C["bd"])
    (b, bt, lse), carried = _att_fwd(f"attf_{tag}", qp, kr, vb, C["bias"], rider)
    if rider is not None:
        W = rider.updated(W, carried)
    row = pl.BlockSpec((tm, D), lambda i: (i, 0))
    proj = functools.partial(
        _mm, grid=(Lp // tm,), a_spec=pl.BlockSpec((tm, 512), lambda i: (i, 0)),
        b_spec=pl.BlockSpec((NSH, None, 512, 256), lambda i: (0, l, 0, 0)), o_spec=row, o_shape=(Lp, D), o_dtype=F32, dims=NN,
        acc_shape=None, shards="cols")
    pa = proj(f"pa_{tag}", a, W["wpa"])
    pb = proj(f"pb_{tag}", b, W["wpb"])
    bm = W["bm"][l]
    y, yt = _merge_fwd(f"mrg_{tag}", H, pa, pb, bm)
    h2 = _mm(f"wout_{tag}", y, W["wout"], grid=(Lp // tm,), a_spec=row,
             b_spec=pl.BlockSpec((NSH, None, 256, D), lambda i: (0, l, 0, 0)), o_spec=row, o_shape=(Lp, D), o_dtype=F32,
             dims=NN, acc_shape=None, res=h, res_spec=row, shards="rows")
    sv = dict(h=h, g=g, zt=zt, H=H, w2p=w2p, b2p=b2p, qh=qh, kh=kh, vh=vh, gf=gf, gb=gb, of=of, ob=ob, sf=sf, sb=sb, gn=gn,
              at=at, gq=gq, gk=gk, qp=qp, kr=kr, vb=vb, b=b, bt=bt, lse=lse, pa=pa, pb=pb, bm=bm, yt=yt, winp=winp)
    return h2, sv, W


def _mixer_bwd(tag, dh, sv, W, C, G, S, l, G_chunks, rider=None):
    Lp = dh.shape[0]
    tm = _rt(Lp, 1408)
    H = sv["H"]
    row = pl.BlockSpec((tm, D), lambda i: (i, 0))
    dy = _mm(f"bdy_{tag}", dh, W["wout"], grid=(Lp // tm,), a_spec=row,
             b_spec=pl.BlockSpec((NSH, None, 256, D), lambda i: (0, l, 0, 0)), o_spec=row, o_shape=(Lp, D), o_dtype=F32, dims=NT,
             acc_shape=None, shards="rows")
    tn = 512
    G["wout"] = _mm(f"bwo_{tag}", sv["yt"], dh, grid=(NSH, D // tn), a_spec=pl.BlockSpec((256, Lp), lambda s, n: (s, 0)),
                    b_spec=pl.BlockSpec((Lp, tn), lambda s, n: (0, n)),
                    o_spec=pl.BlockSpec((None, 256, tn), lambda s, n: (s, 0, n)), o_shape=(NSH, 256, D), o_dtype=F32, dims=NN,
                    acc_shape=None)
    dpa, dpb, dH, S["bm"][l] = _merge_bwd(f"bmrg_{tag}", H, sv["pa"], sv["pb"], sv["bm"], dy)
    dbranch = {}
    for key, dp, xt in (("wpa", dpa, sv["at"]), ("wpb", dpb, sv["bt"])):
        dbranch[key] = _mm(f"bx{key[2]}_{tag}", dp, W[key], grid=(Lp // tm,), a_spec=row,
                           b_spec=pl.BlockSpec((NSH, None, 512, 256), lambda i: (0, l, 0, 0)),
                           o_spec=pl.BlockSpec((tm, 512), lambda i: (i, 0)), o_shape=(Lp, 512), o_dtype=F32, dims=NT,
                           acc_shape=None, shards="cols")
        G[key] = _mm(f"bw{key[2]}_{tag}", xt, dp, grid=(NSH,), a_spec=pl.BlockSpec((512, Lp), lambda s: (0, 0)),
                     b_spec=pl.BlockSpec((Lp, 256), lambda s: (0, s)),
                     o_spec=pl.BlockSpec((None, 512, 256), lambda s: (s, 0, 0)), o_shape=(NSH, 512, 256), o_dtype=F32, dims=NN,
                     acc_shape=None)
    (dqp, dkr, dvb), carried = _att_bwd(f"attb_{tag}", sv["qp"], sv["kr"], sv["vb"], C["bias"], sv["lse"], dbranch["wpb"],
                                        sv["b"], rider)
    dH, S["gq"][l] = _att_q_bwd(f"attq_{tag}", H, sv["gq"], C["cos"], C["sin"], C["bd"], dqp, dH)
    dH, S["gk"][l] = _att_kv_bwd(f"attk_{tag}", H, sv["gk"], C["cos"], C["sin"], C["bd"], dkr, dvb, dH)
    do, dH, S["gn"][l] = _gla_post_bwd(f"bglo_{tag}", sv["of"], sv["ob"], H, sv["gn"], dbranch["wpa"], dH)
    dqf, dkf, dgf, dvf = _gla_bwd(f"bglf_{tag}", sv["qh"], sv["kh"], sv["gf"], sv["vh"], sv["sf"], do, False, G_chunks)
    dqb, dkb, dgb, dvr = _gla_bwd(f"bglr_{tag}", sv["qh"], sv["kh"], sv["gb"], sv["vh"], sv["sb"], do, True, G_chunks)
    dH = _gla_qkv_bwd(f"bglq_{tag}", dqf, dqb, dkf, dkb, dvf, dvr, dH)
    dH, S["w2p"][l], S["b2p"][l] = _gla_gate_bwd(f"bglg_{tag}", H, sv["w2p"], sv["b2p"], dgf, dgb, dH)
    tmm = 256
    G["winp"] = _mm(f"bwi_{tag}", sv["zt"], dH, grid=(HP // W_CHUNK, D // tmm), a_spec=pl.BlockSpec((tmm, Lp), lambda n, m: (m, 0)),
                    b_spec=pl.BlockSpec((Lp, W_CHUNK), lambda n, m: (0, n)), o_spec=pl.BlockSpec((tmm, W_CHUNK), lambda n, m: (m, n)),
                    o_shape=(D, HP), o_dtype=F32, dims=NN, acc_shape=None)
    dh_prev, dg = _norm_project_bwd(f"bdz_{tag}", dH, sv["winp"], sv["h"], sv["g"], dh)
    return dh_prev, dg, carried


def _winp_layer(win_g, l):
    return _win_to_padded(jnp.transpose(win_g[:, l], (1, 0, 2)).reshape(D, D_IN))


def _local_step(x2, tgt2, W, comm=None):
    n_tok = x2.shape[0]
    Lp = n_tok + LANE
    nc = Lp // CHUNK
    g_chunks = max(g for g in (1, 2, 3, 6) if nc % g == 0)
    cos, sin = _rope_tables(n_tok)
    bd = jnp.asarray(np.kron(np.eye(ATT_H, dtype=np.float32), np.full((HD, HD), 1.0 / HD, np.float32)))
    bias = jnp.where(jnp.arange(Lp) >= PAD, 0.0, -1e30).astype(F32).reshape(1, Lp)
    w2, b2 = W["w2"], W["b2"]
    w2p = jnp.zeros((DEPTH, LANE, 512), F32)
    w2p = w2p.at[:, 0:GLA_RANK, 0:256].set(w2[:, 0]).at[:, GLA_RANK:2 * GLA_RANK, 256:512].set(w2[:, 1]).astype(BF16)
    C = dict(cos=cos, sin=sin, bd=bd, bias=bias, w2p=w2p, b2p=b2.reshape(DEPTH, 1, 512),
             gq=jnp.tile(W["qn"], (1, ATT_H)).reshape(DEPTH, 1, 512), gk=jnp.tile(W["kn"], (1, ATT_KV)).reshape(DEPTH, 1, LANE))
    h = jnp.concatenate([jnp.zeros((PAD, D), F32), W["meta"], x2], axis=0)
    saved = []
    for l in range(DEPTH):
        h, s0, W = _ffn_fwd(f"l{l}a", h, W, l, 0, comm.first_rider(W) if comm and l == 0 else None)
        riders = (comm.fwd_rider(W, l, ("down",)), comm.fwd_rider(W, l, ("gate", "up"))) if comm else (None, None)
        h, sm, W = _mixer_fwd(f"l{l}m", h, W, _winp_layer(W["win"], l), C, l, g_chunks, riders)
        h, s1, W = _ffn_fwd(f"l{l}b", h, W, l, 1, comm.fwd_rider(W, l, MIXER_W) if comm else None)
        saved.append((s0, sm, s1))
    dh, dfin, loss = _loss_head("loss_head", h, W["fin"].reshape(1, D), tgt2)
    S = dict(bm=[None] * DEPTH, gq=[None] * DEPTH, gk=[None] * DEPTH, gn=[None] * DEPTH, w2p=[None] * DEPTH,
             b2p=[None] * DEPTH, ng=[[None] * 3 for _ in range(DEPTH)])
    Gs = [None] * DEPTH
    for l in reversed(range(DEPTH)):
        s0, sm, s1 = saved[l]
        G = {}
        rider = comm.ffn_rider(l) if comm else None
        dh, S["ng"][l][2], carried = _ffn_bwd(f"l{l}b", dh, s1, W, G, l, 1, rider)
        if rider is not None:
            comm.ffn_carried(l, carried)
        rider = comm.bwd_rider(l) if comm else None
        dh, S["ng"][l][1], carried = _mixer_bwd(f"l{l}m", dh, sm, W, C, G, S, l, g_chunks, rider)
        if rider is not None:
            comm.bwd_carried(l, carried)
        rider = comm.last_rider(l) if comm else None
        dh, S["ng"][l][0], carried = _ffn_bwd(f"l{l}a", dh, s0, W, G, l, 0, rider)
        if rider is not None:
            comm.last_carried(carried)
        Gs[l] = G
        if comm:
            comm.layer_done(l, G)
    small = dict(
        meta=dh[PAD:LANE],
        norm_gains=jnp.stack([jnp.concatenate(S["ng"][l], axis=0) for l in range(DEPTH)]),
        w2=jnp.stack([jnp.stack([S["w2p"][l][0:GLA_RANK, 0:256], S["w2p"][l][GLA_RANK:2 * GLA_RANK, 256:512]]) for l in range(DEPTH)]),
        b2=jnp.stack([S["b2p"][l].reshape(2, 256) for l in range(DEPTH)]),
        gn=jnp.concatenate(S["gn"], axis=0),
        qn=jnp.stack([S["gq"][l].reshape(ATT_H, HD).sum(0) for l in range(DEPTH)]),
        kn=jnp.stack([S["gk"][l].reshape(ATT_KV, HD).sum(0) for l in range(DEPTH)]),
        bm=jnp.stack(S["bm"]),
        fin=dfin.reshape(D),
    )
    return loss[0, 0], dh, Gs, small


def _win_to_padded(w):
    pad = jnp.zeros(w.shape[:-1] + (HP - D_IN,), w.dtype)
    return jnp.concatenate([w[..., 2336:4384], w[..., 0:1536], w[..., 1568:2336], w[..., 1536:1568], pad], axis=-1)


def _win_from_padded(w):
    return jnp.concatenate([w[..., 2048:3584], w[..., 4352:4384], w[..., 3584:4352], w[..., 0:2048]], axis=-1)


def _assemble(g):
    W = dict(gate=g["gate"], up=g["up"], down=g["down"], win=g["win"], wpa=g["wpa"], wpb=g["wpb"], wout=g["wout"])
    sm = g["small"]
    parts = _unpack(sm, SHARDED_SMALL)
    W["meta"] = jnp.transpose(parts["meta"], (1, 0, 2)).reshape(N_META, D)
    W["norm_gains"] = jnp.transpose(parts["norm_gains"], (1, 2, 0, 3)).reshape(DEPTH, 3, D)
    W["w2"] = jnp.transpose(parts["w2"], (1, 2, 3, 0, 4)).reshape(DEPTH, 2, GLA_RANK, 256)
    W["b2"] = jnp.transpose(parts["b2"], (1, 2, 0, 3)).reshape(DEPTH, 2, 256)
    W["bm"] = jnp.transpose(parts["bm"], (1, 2, 0, 3)).reshape(DEPTH, 2, D)
    return W


SHARDED_SMALL = dict(meta=(N_META, 256), norm_gains=(DEPTH, 3, 256), w2=(DEPTH, 2, GLA_RANK, 64), b2=(DEPTH, 2, 64),
                     bm=(DEPTH, 2, 256))
FULL_SMALL = dict(meta=(N_META, D), norm_gains=(DEPTH, 3, D), w2=(DEPTH, 2, GLA_RANK, 256), b2=(DEPTH, 2, 256),
                  gn=(DEPTH, 512), qn=(DEPTH, HD), kn=(DEPTH, HD), bm=(DEPTH, 2, D), fin=(D,))


def _pack(parts, table, rows):
    flat = jnp.concatenate([parts[k].reshape(-1).astype(F32) for k in table])
    return jnp.pad(flat, (0, rows * LANE - flat.shape[0])).reshape(rows, LANE)


def _unpack(packed, table):
    lead = packed.shape[:-2]
    flat = packed.reshape(lead + (-1,))
    out, off = {}, 0
    for k, shp in table.items():
        n = int(np.prod(shp))
        out[k] = flat[..., off:off + n].reshape(lead + tuple(shp))
        off += n
    return out


def _rows_for(table, mult):
    n = sum(int(np.prod(s)) for s in table.values())
    return -(-n // (LANE * mult)) * mult


SMALL_ROWS = _rows_for(SHARDED_SMALL, 16)
GRAD_ROWS = _rows_for(dict(FULL_SMALL, loss=(1,)), 8)


def _place():
    x, y, c = lax.axis_index("x"), lax.axis_index("y"), lax.axis_index("c")
    return x, y, c


def _other_chips(x, y):
    return [(1 - x, y), (x, 1 - y), (1 - x, 1 - y)]


def _cast_place(name, w3, slot):
    A, R, Cc = w3.shape
    tr = _rt(R, 512, 16)

    def body(p_ref, w_ref, o_ref):
        o_ref[...] = w_ref[...].astype(BF16)

    return pl.pallas_call(
        body, name=name,
        grid_spec=pltpu.PrefetchScalarGridSpec(
            num_scalar_prefetch=1, grid=(A, R // tr),
            in_specs=[pl.BlockSpec((None, tr, Cc), lambda a, r, p_ref: (a, r, 0))],
            out_specs=pl.BlockSpec((None, None, tr, Cc), lambda a, r, p_ref: (p_ref[0], a, r, 0))),
        out_shape=SDS((NSH, A, R, Cc), BF16))(slot, w3)


class _GatherRider:
    def __init__(self, keys, bufs, layers):
        n = len(bufs)
        self.keys, self.ins, self.layers = keys, list(bufs), layers
        self.out_shape = [SDS(b.shape, b.dtype) for b in bufs]
        self.aliases = {a: a for a in range(n)}
        self.sems = [pltpu.SemaphoreType.DMA((n, 3)) for _ in range(4)]

    def updated(self, W, carried):
        return {**W, **dict(zip(self.keys, carried))}

    def hooks(self, ins, outs, sems):
        send, recv, fsend, frecv = sems
        n = len(outs)

        def rows(a, slot, core):
            ref, l = outs[a], self.layers[a]
            half = ref.shape[-2] // 2
            return ref.at[slot, pl.ds(core * half, half)] if l is None else ref.at[slot, l, pl.ds(core * half, half)]

        def ici(a, k, part, px, py, c):
            return pltpu.make_async_remote_copy(src_ref=part, dst_ref=part, send_sem=send.at[a, k], recv_sem=recv.at[a, k],
                                                device_id=(px, py, c), device_id_type=MESH)

        def d2d(a, k, part, x, y, c):
            return pltpu.make_async_remote_copy(src_ref=part, dst_ref=part, send_sem=fsend.at[a, k], recv_sem=frecv.at[a, k],
                                                device_id=(x, y, 1 - c), device_id_type=MESH)

        def start():
            x, y, c = _place()
            for a in range(n):
                for k, (px, py) in enumerate(_other_chips(x, y)):
                    ici(a, k, rows(a, 2 * x + y, c), px, py, c).start()

        def mid():
            x, y, c = _place()
            for a in range(n):
                for k, (px, py) in enumerate(_other_chips(x, y)):
                    landed = rows(a, 2 * px + py, c)
                    ici(a, k, landed, px, py, c).wait_recv()
                    d2d(a, k, landed, x, y, c).start()

        def end():
            x, y, c = _place()
            for a in range(n):
                for k, (px, py) in enumerate(_other_chips(x, y)):
                    d2d(a, k, rows(a, 2 * px + py, 1 - c), x, y, c).wait_recv()
                    ici(a, k, rows(a, 2 * x + y, c), px, py, c).wait_send()
                    d2d(a, k, rows(a, 2 * px + py, c), x, y, c).wait_send()

        return start, mid, end


class _ChipExchangeRider:
    def __init__(self, arrs):
        n = len(arrs)
        self.ins = list(arrs)
        self.out_shape = [SDS((3,) + a.shape[1:], a.dtype) for a in arrs]
        self.aliases = {}
        self.sems = [pltpu.SemaphoreType.DMA((n, 3)), pltpu.SemaphoreType.DMA((n, 3))]

    def hooks(self, ins, outs, sems):
        send, recv = sems

        def copy(a, k, px, py, c):
            return pltpu.make_async_remote_copy(src_ref=ins[a].at[2 * px + py], dst_ref=outs[a].at[k], send_sem=send.at[a, k],
                                                recv_sem=recv.at[a, k], device_id=(px, py, c), device_id_type=MESH)

        def start():
            x, y, c = _place()
            for a in range(len(ins)):
                for k, (px, py) in enumerate(_other_chips(x, y)):
                    copy(a, k, px, py, c).start()

        def mid():
            pass

        def end():
            x, y, c = _place()
            for a in range(len(ins)):
                for k, (px, py) in enumerate(_other_chips(x, y)):
                    copy(a, k, px, py, c).wait()

        return start, mid, end


def _run_rider(name, rider):
    r_in, r_out = len(rider.ins), len(rider.out_shape)

    def body(*refs):
        start, mid, end = rider.hooks(refs[:r_in], refs[r_in:r_in + r_out], refs[r_in + r_out:])
        start()
        mid()
        end()

    any_spec = pl.BlockSpec(memory_space=pl.ANY)
    return pl.pallas_call(
        body, name=name, in_specs=[any_spec] * r_in, out_specs=[any_spec] * r_out, out_shape=list(rider.out_shape),
        scratch_shapes=list(rider.sems), input_output_aliases=dict(rider.aliases))(*rider.ins)


class _PairExchangeRider:
    def __init__(self, arrs):
        n = len(arrs)
        self.ins = list(arrs)
        self.out_shape = [SDS((NSH, a.shape[1] // 2, a.shape[2]), a.dtype) for a in arrs]
        self.aliases = {}
        self.sems = [pltpu.SemaphoreType.DMA((n,)), pltpu.SemaphoreType.DMA((n,))]

    def hooks(self, ins, outs, sems):
        send, recv = sems

        def copy(a):
            x, y, c = _place()
            half = ins[a].shape[1] // 2
            return pltpu.make_async_remote_copy(
                src_ref=ins[a].at[:, pl.ds((1 - c) * half, half)], dst_ref=outs[a], send_sem=send.at[a], recv_sem=recv.at[a],
                device_id=(x, y, 1 - c), device_id_type=MESH)

        def start():
            for a in range(len(ins)):
                copy(a).start()

        def mid():
            pass

        def end():
            for a in range(len(ins)):
                copy(a).wait()

        return start, mid, end


def _pair_add(name, g, p, core):
    _, Rh, Cc = p.shape
    tr = _rt(Rh, 512, 16)
    nr = Rh // tr

    def body(c_ref, g_ref, p_ref, o_ref, ob_ref):
        v = g_ref[...] + p_ref[...]
        o_ref[...] = v
        ob_ref[...] = v.astype(BF16)

    blk = pl.BlockSpec((None, tr, Cc), lambda s, r, c_ref: (s, r, 0))
    return pl.pallas_call(
        body, name=name,
        grid_spec=pltpu.PrefetchScalarGridSpec(
            num_scalar_prefetch=1, grid=(NSH, nr),
            in_specs=[pl.BlockSpec((None, tr, Cc), lambda s, r, c_ref: (s, c_ref[0] * nr + r, 0)), blk],
            out_specs=[blk, blk]),
        out_shape=[SDS((NSH, Rh, Cc), F32), SDS((NSH, Rh, Cc), BF16)])(core, g, p)


def _chip_add(name, hsum, q, chip, core, l, into):
    _, Rh, Cc = hsum.shape
    tr = _rt(Rh, 512, 8)
    nr = Rh // tr

    def body(*refs):
        h_ref, q_ref, o_ref = refs[2], refs[3], refs[-1]
        o_ref[...] = ((h_ref[...] + q_ref[0].astype(F32)) + q_ref[1].astype(F32)) + q_ref[2].astype(F32)

    in_specs = [pl.BlockSpec((None, tr, Cc), lambda r, p_ref, c_ref: (p_ref[0], r, 0)),
                pl.BlockSpec((3, tr, Cc), lambda r, p_ref, c_ref: (0, r, 0))]
    args = [chip, core, hsum, q]
    aliases = {}
    if into is not None:
        in_specs.append(pl.BlockSpec(memory_space=pl.ANY))
        args.append(into)
        aliases = {4: 0}
    return pl.pallas_call(
        body, name=name,
        grid_spec=pltpu.PrefetchScalarGridSpec(
            num_scalar_prefetch=2, grid=(nr,), in_specs=in_specs,
            out_specs=pl.BlockSpec((None, tr, Cc), lambda r, p_ref, c_ref: (l, c_ref[0] * nr + r, 0))),
        out_shape=SDS((DEPTH, 2 * Rh, Cc), F32), input_output_aliases=aliases)(*args)


class _PairShareRider:
    def __init__(self, bufs, lo, hi):
        n = len(bufs)
        self.ins, self.lo, self.hi = list(bufs), lo, hi
        self.out_shape = [SDS(b.shape, b.dtype) for b in bufs]
        self.aliases = {a: a for a in range(n)}
        self.sems = [pltpu.SemaphoreType.DMA((n,)), pltpu.SemaphoreType.DMA((n,))]

    def hooks(self, ins, outs, sems):
        send, recv = sems
        lo, hi = self.lo, self.hi

        def copy(a, whose):
            x, y, c = _place()
            half = outs[a].shape[1] // 2
            part = outs[a].at[lo:hi, pl.ds((c if whose == 0 else 1 - c) * half, half)]
            return pltpu.make_async_remote_copy(src_ref=part, dst_ref=part, send_sem=send.at[a], recv_sem=recv.at[a],
                                                device_id=(x, y, 1 - c), device_id_type=MESH)

        def start():
            for a in range(len(outs)):
                copy(a, 0).start()

        def mid():
            pass

        def end():
            for a in range(len(outs)):
                copy(a, 0).wait_send()
                copy(a, 1).wait_recv()

        return start, mid, end


def _allreduce_small(v):
    rows = v.shape[0]

    def body(v_ref, o_ref, buf, send, recv):
        x, y, c = _place()
        me = 4 * x + 2 * y + c
        buf[me] = v_ref[...]
        cps = []
        k = 0
        for dx in range(2):
            for dy in range(2):
                for dc in range(2):
                    if dx + dy + dc == 0:
                        continue
                    cp = pltpu.make_async_remote_copy(
                        src_ref=v_ref, dst_ref=buf.at[me], send_sem=send.at[k], recv_sem=recv.at[k],
                        device_id=(jnp.bitwise_xor(x, dx), jnp.bitwise_xor(y, dy), jnp.bitwise_xor(c, dc)), device_id_type=MESH)
                    cp.start()
                    cps.append((cp, dx, dy, dc))
                    k += 1
        for k, (cp, dx, dy, dc) in enumerate(cps):
            cp.wait_send()
            src = 4 * jnp.bitwise_xor(x, dx) + 2 * jnp.bitwise_xor(y, dy) + jnp.bitwise_xor(c, dc)
            pltpu.make_async_remote_copy(
                src_ref=v_ref, dst_ref=buf.at[src], send_sem=send.at[k], recv_sem=recv.at[k],
                device_id=(x, y, c), device_id_type=MESH).wait_recv()
        acc = buf[0]
        for d in range(1, 8):
            acc = acc + buf[d]
        o_ref[...] = acc

    vm = pl.BlockSpec(memory_space=pltpu.VMEM)
    return pl.pallas_call(
        body, name="allreduce_small", in_specs=[vm], out_specs=vm, out_shape=SDS((rows, LANE), F32),
        scratch_shapes=[pltpu.VMEM((8, rows, LANE), F32), pltpu.SemaphoreType.DMA((7,)), pltpu.SemaphoreType.DMA((7,))])(v)


def _adamw(name, w, g, m, v, lo=0, hi=None, into=None, rider=None):
    A, R, Cc = w.shape
    hi = A if hi is None else hi
    tr = _rt(R, 512, 8)

    def core(ins, outs, scr):
        w_ref, g_ref, m_ref, v_ref = ins[:4]
        d_ref, mo_ref, vo_ref = outs
        gv = g_ref[...]
        mn = ADAM_B1 * m_ref[...] + (1.0 - ADAM_B1) * gv
        vn = ADAM_B2 * v_ref[...] + (1.0 - ADAM_B2) * (gv * gv)
        m_hat = mn / (1.0 - ADAM_B1 ** ADAM_STEP)
        v_hat = vn / (1.0 - ADAM_B2 ** ADAM_STEP)
        d_ref[...] = -ADAM_LR * (m_hat / (jnp.sqrt(v_hat) + ADAM_EPS) + ADAM_WD * w_ref[...])
        mo_ref[...] = mn
        vo_ref[...] = vn

    blk = pl.BlockSpec((None, tr, Cc), lambda a, r: (lo + a, r, 0))
    in_specs, args, aliases = [blk] * 4, [w, g, m, v], {}
    if into is not None:
        in_specs = in_specs + [pl.BlockSpec(memory_space=pl.ANY)] * 3
        args = args + list(into)
        aliases = {4: 0, 5: 1, 6: 2}
    return _ridden_call(core, rider, name=name, grid=(hi - lo, R // tr), in_specs=in_specs, out_specs=[blk] * 3,
                        out_shape=[SDS(w.shape, F32)] * 3, scratch_shapes=[], args=args, aliases=aliases)


BIG = ("gate", "up", "down", "win", "wpa", "wpb", "wout")
FFN_W, MIXER_W = BIG[:3], BIG[3:]
SMALL = ("meta", "norm_gains", "w2", "b2", "gn", "qn", "kn", "bm", "fin")


def _view3(a):
    return a.reshape(a.shape[0], -1, a.shape[-1])


class _StepComm:
    def __init__(self, chip, core):
        self.pvec, self.cvec = chip.reshape(1), core.reshape(1)
        self.loc, self.sums = {}, {}
        self.red = {k: None for k in BIG}

    def first_rider(self, W):
        return _GatherRider(MIXER_W, [W[k] for k in MIXER_W], [0] * len(MIXER_W))

    def fwd_rider(self, W, l, keys):
        if l + 1 >= DEPTH:
            return None
        return _GatherRider(keys, [W[k] for k in keys], [l + 1] * len(keys))

    def layer_done(self, l, G):
        dwin = jnp.transpose(_win_from_padded(G["winp"]).reshape(D, NSH, D_IN // NSH), (1, 0, 2))
        self.loc[l] = [dwin if k == "win" else _view3(G[k]) for k in BIG]

    def ffn_rider(self, l):
        return _PairExchangeRider(self.loc[l + 1]) if l + 1 < DEPTH else None

    def ffn_carried(self, l, got):
        self._pair_add(l + 1, got)

    def _pair_add(self, l, got):
        self.sums[l] = [_pair_add(f"pair_add_{k}_l{l}", a, p, self.cvec) for k, a, p in zip(BIG, self.loc.pop(l), got)]

    def bwd_rider(self, l):
        if l + 1 >= DEPTH:
            return None
        return _ChipExchangeRider([s[1] for s in self.sums[l + 1]])

    def bwd_carried(self, l, arrived):
        self._chip_add(l + 1, arrived)

    def _chip_add(self, l, arrived):
        for k, s, q in zip(BIG, self.sums.pop(l), arrived):
            self.red[k] = _chip_add(f"chip_add_{k}_l{l}", s[0], q, self.pvec, self.cvec, l, self.red[k])

    def last_rider(self, l):
        return _PairShareRider([self.red[k] for k in BIG], 1, DEPTH) if l == 0 else None

    def last_carried(self, shared):
        self.red = dict(zip(BIG, shared))

    def finish(self, w, m, v):
        self._pair_add(0, _run_rider("pair_exchange_l0", _PairExchangeRider(self.loc[0])))
        sums = dict(zip(BIG, self.sums.pop(0)))
        upd = {}
        for k in BIG:
            upd[k], arrived = _adamw(f"adamw_{k}_l123", w[k], self.red[k], m[k], v[k], 1, DEPTH,
                                     rider=_ChipExchangeRider([sums[k][1]]))
            self.red[k] = _chip_add(f"chip_add_{k}_l0", sums[k][0], arrived[0], self.pvec, self.cvec, 0, self.red[k])
        red = dict(zip(BIG, _run_rider("pair_share_l0", _PairShareRider([self.red[k] for k in BIG], 0, 1))))
        for k in BIG:
            upd[k], _ = _adamw(f"adamw_{k}_l0", w[k], red[k], m[k], v[k], 0, 1, into=upd[k])
        return red, upd


def kernel(x, meta_tokens, norm_gains, ffn_w_gate, ffn_w_up, ffn_w_down, w_in, gla_w2, gla_b2, gla_gn, q_norm, k_norm, w_pa, w_pb, b_merge, w_out, final_norm, loss_target, m_meta_tokens, m_norm_gains, m_ffn_w_gate, m_ffn_w_up, m_ffn_w_down, m_w_in, m_gla_w2, m_gla_b2, m_gla_gn, m_q_norm, m_k_norm, m_w_pa, m_w_pb, m_b_merge, m_w_out, m_final_norm, v_meta_tokens, v_norm_gains, v_ffn_w_gate, v_ffn_w_up, v_ffn_w_down, v_w_in, v_gla_w2, v_gla_b2, v_gla_gn, v_q_norm, v_k_norm, v_w_pa, v_w_pb, v_b_merge, v_w_out, v_final_norm):
    big_w = dict(gate=ffn_w_gate, up=ffn_w_up, down=ffn_w_down, win=w_in, wpa=w_pa, wpb=w_pb, wout=w_out)
    big_m = dict(gate=m_ffn_w_gate, up=m_ffn_w_up, down=m_ffn_w_down, win=m_w_in, wpa=m_w_pa, wpb=m_w_pb, wout=m_w_out)
    big_v = dict(gate=v_ffn_w_gate, up=v_ffn_w_up, down=v_ffn_w_down, win=v_w_in, wpa=v_w_pa, wpb=v_w_pb, wout=v_w_out)
    small_w = dict(meta=meta_tokens, norm_gains=norm_gains, w2=gla_w2, b2=gla_b2, gn=gla_gn, qn=q_norm, kn=k_norm,
                   bm=b_merge, fin=final_norm)
    small_m = dict(meta=m_meta_tokens, norm_gains=m_norm_gains, w2=m_gla_w2, b2=m_gla_b2, gn=m_gla_gn, qn=m_q_norm,
                   kn=m_k_norm, bm=m_b_merge, fin=m_final_norm)
    small_v = dict(meta=v_meta_tokens, norm_gains=v_norm_gains, w2=v_gla_w2, b2=v_gla_b2, gn=v_gla_gn, qn=v_q_norm,
                   kn=v_k_norm, bm=v_b_merge, fin=v_final_norm)
    xi, yi, ci = _place()
    chip = (2 * xi + yi).astype(jnp.int32)

    comm = _StepComm(chip, ci.astype(jnp.int32))
    shard_pack = _pack({k: small_w[k] for k in SHARDED_SMALL}, SHARDED_SMALL, SMALL_ROWS)
    placed = [_cast_place(f"cast_{k}", _view3(big_w[k]), comm.pvec) for k in BIG]
    placed.append(lax.dynamic_update_slice(jnp.zeros((NSH, SMALL_ROWS, LANE), F32), shard_pack[None], (chip, 0, 0)))
    first = FFN_W + ("small",)
    w0 = dict(zip(BIG + ("small",), placed))
    w0.update(zip(first, _run_rider("gather_l0", _GatherRider(first, [w0[k] for k in first], [0] * len(FFN_W) + [None]))))
    W = _assemble(w0)
    W.update(gn=gla_gn, qn=q_norm, kn=k_norm, fin=final_norm)

    loss, dh0, _, gs = _local_step(x[0], loss_target[0], W, comm)
    grad_x = dh0[LANE:][None]
    red, upd = comm.finish(*({k: _view3(t[k]) for k in BIG} for t in (big_w, big_m, big_v)))
    grads, deltas, new_m, new_v = {}, {}, {}, {}
    for k in BIG:
        shp = big_w[k].shape
        grads[k] = red[k].reshape(shp)
        deltas[k], new_m[k], new_v[k] = (t.reshape(shp) for t in upd[k])

    gs["loss"] = loss.reshape(1)
    table = dict(FULL_SMALL, loss=(1,))
    tot = _unpack(_allreduce_small(_pack(gs, table, GRAD_ROWS)), table)
    loss_out = tot["loss"][0]
    sl = dict(meta=(1, 256), norm_gains=(2, 256), w2=(3, 64), b2=(2, 64), bm=(2, 256))
    for k in SMALL:
        gk = tot[k]
        if k in sl:
            ax, width = sl[k]
            gk = lax.dynamic_slice_in_dim(gk, chip * width, width, axis=ax)
        grads[k] = gk
    tbl = {k: small_w[k].shape for k in SMALL}
    rows = _rows_for(tbl, 8)
    packs = [_pack(src, tbl, rows)[None] for src in (small_w, grads, small_m, small_v)]
    (d, mn, vn), _ = _adamw("adamw_small", *packs)
    for dst, packed in ((deltas, d), (new_m, mn), (new_v, vn)):
        dst.update(_unpack(packed[0], tbl))

    order = ("meta", "norm_gains", "gate", "up", "down", "win", "w2", "b2", "gn", "qn", "kn", "wpa", "wpb", "bm", "wout", "fin")
    return (loss_out, grad_x, *[grads[k] for k in order], *[deltas[k] for k in order], *[new_m[k] for k in order],
            *[new_v[k] for k in order])
```

```python
import functools

import numpy as np
import jax
import jax.numpy as jnp
from jax import lax
from jax.experimental import pallas as pl
from jax.experimental.pallas import tpu as pltpu

F32, BF16 = jnp.float32, jnp.bfloat16
SDS = jax.ShapeDtypeStruct
HIGHEST = lax.Precision.HIGHEST
MESH = pl.DeviceIdType.MESH

D = 1024
DEPTH = 4
N_META = 16
GRID_W = 64
GLA_H, GLA_DK, GLA_DV, GLA_RANK, GLA_TAU, CHUNK = 4, 64, 128, 16, 16.0, 64
ATT_H, ATT_KV, HD = 8, 2, 64
D_FF = 2816
EPS = 1e-6
ROPE_THETA = 10000.0
ADAM_LR, ADAM_B1, ADAM_B2, ADAM_EPS, ADAM_WD, ADAM_STEP = 0.001, 0.9, 0.999, 1e-08, 0.01, 10

NSH = 4
FF_S = D_FF // NSH
FF_P = 768
LANE = 128
PAD = LANE - N_META
D_IN = 4384
C_GA, C_GB, C_QA, C_KA, C_VA, C_RA, C_QB, C_KB, C_VB, C_LR, HP = 0, 1024, 2048, 2304, 2560, 3072, 3584, 4096, 4224, 4352, 4480
VMEM_BIG = 56 * 2 ** 20
ATT_SUB_FWD, ATT_SUB_BWD = 128, 192
RIDER_MID_BACK = 5


def _rt(n, cap, mult=LANE):
    best = None
    t = mult
    while t <= min(n, cap):
        if n % t == 0:
            best = t
        t += mult
    assert best is not None, (n, cap, mult)
    return best


def _cp(big=False):
    return pltpu.CompilerParams(vmem_limit_bytes=VMEM_BIG) if big else None


def _row_ids(i, tm):
    return i * tm + lax.broadcasted_iota(jnp.int32, (tm, 1), 0)


def _mm(name, a, b, *, grid, a_spec, b_spec, o_spec, o_shape, o_dtype, dims, acc_shape, nk=1, scale=None,
        res=None, res_spec=None, a_sl=None, b_sl=None, pad_w=None, into=None, shards=None):
    has_res, has_into = res is not None, into is not None

    def body(*refs):
        a_ref, b_ref = refs[0], refs[1]
        p = 2
        res_ref = None
        if has_res:
            res_ref = refs[p]
            p += 1
        if has_into:
            p += 1
        o_ref = refs[p]
        acc_ref = refs[p + 1] if nk > 1 else None
        av = (a_ref[a_sl] if a_sl is not None else a_ref[...]).astype(BF16)
        bv = (b_ref[b_sl] if b_sl is not None else b_ref[...]).astype(BF16)
        if shards == "rows":
            bv = bv.reshape(bv.shape[0] * bv.shape[1], bv.shape[2])
        if shards == "cols":
            w = bv.shape[2]
            if dims == NN:
                prod = jnp.concatenate([lax.dot_general(av, bv[s], NN, preferred_element_type=F32) for s in range(NSH)], axis=1)
            else:
                prod = sum(lax.dot_general(av[:, s * w:(s + 1) * w], bv[s], NT, preferred_element_type=F32) for s in range(NSH))
        else:
            prod = lax.dot_general(av, bv, dims, preferred_element_type=F32)

        def finish(v):
            if scale is not None:
                v = v * scale
            if has_res:
                v = v + res_ref[...]
            v = v.astype(o_dtype)
            if pad_w is None:
                o_ref[...] = v
            else:
                w = v.shape[-1]
                o_ref[:, :w] = v
                o_ref[:, w:] = jnp.zeros((v.shape[0], pad_w - w), o_dtype)

        if nk == 1:
            finish(prod)
        else:
            k = pl.program_id(len(grid) - 1)

            @pl.when(k == 0)
            def _():
                acc_ref[...] = prod

            @pl.when(k > 0)
            def _():
                acc_ref[...] += prod

            @pl.when(k == nk - 1)
            def _():
                finish(acc_ref[...])

    in_specs = [a_spec, b_spec]
    args = [a, b]
    if has_res:
        in_specs.append(res_spec)
        args.append(res)
    aliases = {}
    if has_into:
        aliases = {len(args): 0}
        in_specs.append(pl.BlockSpec(memory_space=pl.ANY))
        args.append(into)
        o_shape = into.shape
    return pl.pallas_call(
        body, name=name, grid=grid, in_specs=in_specs, out_specs=o_spec, out_shape=SDS(tuple(o_shape), o_dtype),
        scratch_shapes=[pltpu.VMEM(acc_shape, F32)] if nk > 1 else [], input_output_aliases=aliases,
        compiler_params=_cp(True))(*args)


NN = (((1,), (0,)), ((), ()))
NT = (((1,), (1,)), ((), ()))
TN = (((0,), (0,)), ((), ()))


W_CHUNK = 896


def _norm_project(name, h, g, winp, rider=None):
    Lp = h.shape[0]
    tm = _rt(Lp, 384)

    def core(ins, outs, scr):
        h_ref, g_ref, w_ref = ins
        o_ref, zt_ref = outs
        x = h_ref[...]
        y = x * lax.rsqrt(jnp.mean(x * x, axis=-1, keepdims=True) + EPS) * g_ref[...]
        zt_ref[...] = y.T.astype(BF16)
        z = y.astype(BF16)
        for n in range(HP // W_CHUNK):
            cs = slice(n * W_CHUNK, (n + 1) * W_CHUNK)
            o_ref[:, cs] = jnp.dot(z, w_ref[:, cs], preferred_element_type=F32)

    return _ridden_call(
        core, rider, name=name, grid=(Lp // tm,),
        in_specs=[pl.BlockSpec((tm, D), lambda i: (i, 0)), pl.BlockSpec((1, D), lambda i: (0, 0)),
                  pl.BlockSpec((D, HP), lambda i: (0, 0), pipeline_mode=pl.Buffered(1))],
        out_specs=[pl.BlockSpec((tm, HP), lambda i: (i, 0)), pl.BlockSpec((D, tm), lambda i: (0, i))],
        out_shape=[SDS((Lp, HP), F32), SDS((D, Lp), BF16)], scratch_shapes=[], args=(h, g, winp), mid_back=1)


def _norm_project_bwd(name, dH, winp, h, g, dh):
    Lp = h.shape[0]
    tm = _rt(Lp, 384)

    def body(dhh_ref, w_ref, h_ref, g_ref, dh_ref, o_ref, dg_ref):
        dn = lax.dot_general(dhh_ref[...], w_ref[...], NT, preferred_element_type=F32)
        x = h_ref[...]
        r = lax.rsqrt(jnp.mean(x * x, axis=-1, keepdims=True) + EPS)
        xh = x * r
        u = dn * g_ref[...]
        o_ref[...] = dh_ref[...] + r * (u - xh * jnp.mean(u * xh, axis=-1, keepdims=True))

        @pl.when(pl.program_id(0) == 0)
        def _():
            dg_ref[...] = jnp.zeros_like(dg_ref)

        dg_ref[...] += jnp.sum(dn * xh, axis=0, keepdims=True)

    row = pl.BlockSpec((tm, D), lambda i: (i, 0))
    vec = pl.BlockSpec((1, D), lambda i: (0, 0))
    return pl.pallas_call(
        body, name=name, grid=(Lp // tm,),
        in_specs=[pl.BlockSpec((tm, HP), lambda i: (i, 0)), pl.BlockSpec((D, HP), lambda i: (0, 0), pipeline_mode=pl.Buffered(1)),
                  row, vec, row],
        out_specs=[row, vec], out_shape=[SDS((Lp, D), F32), SDS((1, D), F32)], compiler_params=_cp(True))(dH, winp, h, g, dh)


def _loss_head(name, h, g, tgt):
    Lp = h.shape[0]
    tm = LANE

    def body(h_ref, g_ref, t_ref, dh_ref, dg_ref, loss_ref):
        i = pl.program_id(0)
        x = h_ref[...]
        r = lax.rsqrt(jnp.mean(x * x, axis=-1, keepdims=True) + EPS)
        xh = x * r
        gg = g_ref[...]
        err = jnp.where(i >= 1, xh * gg - t_ref[...], 0.0)
        dy = err * (1.0 / D)
        u = dy * gg
        dh_ref[...] = r * (u - xh * jnp.mean(u * xh, axis=-1, keepdims=True))

        @pl.when(i == 0)
        def _():
            dg_ref[...] = jnp.zeros_like(dg_ref)
            loss_ref[...] = jnp.zeros_like(loss_ref)

        dg_ref[...] += jnp.sum(dy * xh, axis=0, keepdims=True)
        loss_ref[...] += (0.5 / D) * jnp.sum(err * err)

    row = pl.BlockSpec((tm, D), lambda i: (i, 0))
    vec = pl.BlockSpec((1, D), lambda i: (0, 0))
    return pl.pallas_call(
        body, name=name, grid=(Lp // tm,),
        in_specs=[row, vec, pl.BlockSpec((tm, D), lambda i: (jnp.maximum(i - 1, 0), 0))],
        out_specs=[row, vec, pl.BlockSpec((8, LANE), lambda i: (0, 0))],
        out_shape=[SDS((Lp, D), F32), SDS((1, D), F32), SDS((8, LANE), F32)])(h, g, tgt)


def _merge_fwd(name, H, pa, pb, bm):
    Lp = H.shape[0]
    tm = _rt(Lp, 384)

    def body(g_ref, pa_ref, pb_ref, bm_ref, y_ref, yt_ref):
        gv = g_ref[...]
        y = (jax.nn.sigmoid(gv[:, :D] + bm_ref[0:1, :]) * pa_ref[...]
             + jax.nn.sigmoid(gv[:, D:] + bm_ref[1:2, :]) * pb_ref[...])
        y_ref[...] = y.astype(BF16)
        yt_ref[...] = y.T.astype(BF16)

    row = pl.BlockSpec((tm, D), lambda i: (i, 0))
    return pl.pallas_call(
        body, name=name, grid=(Lp // tm,),
        in_specs=[pl.BlockSpec((tm, 2 * D), lambda i: (i, 0)), row, row, pl.BlockSpec((2, D), lambda i: (0, 0))],
        out_specs=[row, pl.BlockSpec((D, tm), lambda i: (0, i))],
        out_shape=[SDS((Lp, D), BF16), SDS((D, Lp), BF16)])(H, pa, pb, bm)


def _merge_bwd(name, H, pa, pb, bm, dy):
    Lp = H.shape[0]
    tm = _rt(Lp, 384)

    def body(g_ref, pa_ref, pb_ref, bm_ref, dy_ref, dpa_ref, dpb_ref, dh_ref, dbm_ref):
        gv = g_ref[...]
        dyv = dy_ref[...]
        sa = jax.nn.sigmoid(gv[:, :D] + bm_ref[0:1, :])
        sb = jax.nn.sigmoid(gv[:, D:] + bm_ref[1:2, :])
        dpa_ref[...] = (dyv * sa).astype(BF16)
        dpb_ref[...] = (dyv * sb).astype(BF16)
        dga = dyv * pa_ref[...] * (sa * (1.0 - sa))
        dgb = dyv * pb_ref[...] * (sb * (1.0 - sb))
        dh_ref[:, :D] = dga.astype(BF16)
        dh_ref[:, D:] = dgb.astype(BF16)

        @pl.when(pl.program_id(0) == 0)
        def _():
            dbm_ref[...] = jnp.zeros_like(dbm_ref)

        dbm_ref[0:1, :] += jnp.sum(dga, axis=0, keepdims=True)
        dbm_ref[1:2, :] += jnp.sum(dgb, axis=0, keepdims=True)

    row = pl.BlockSpec((tm, D), lambda i: (i, 0))
    two = pl.BlockSpec((2, D), lambda i: (0, 0))
    gate = pl.BlockSpec((tm, 2 * D), lambda i: (i, 0))
    return pl.pallas_call(
        body, name=name, grid=(Lp // tm,), in_specs=[gate, row, row, two, row], out_specs=[row, row, gate, two],
        out_shape=[SDS((Lp, D), BF16), SDS((Lp, D), BF16), SDS((Lp, HP), BF16), SDS((2, D), F32)])(H, pa, pb, bm, dy)


def _gla_prep(name, H, w2p, b2p):
    Lp = H.shape[0]
    tm = _rt(Lp, 384)

    def body(qk_ref, v_ref, lr_ref, w_ref, b_ref, q_o, k_o, v_o, gf_o, gb_o):
        valid = _row_ids(pl.program_id(0), tm) >= PAD
        qk = qk_ref[...]
        vv = v_ref[...]
        pre = jnp.dot(lr_ref[...].astype(BF16), w_ref[...], preferred_element_type=F32) + b_ref[...]
        g = jnp.where(valid, jax.nn.log_sigmoid(pre) * (1.0 / GLA_TAU), 0.0)
        for hh in range(GLA_H):
            q_o[hh] = qk[:, 64 * hh:64 * hh + 64] * (GLA_DK ** -0.5)
            k_o[hh] = qk[:, 256 + 64 * hh:256 + 64 * hh + 64]
            v_o[hh] = vv[:, 128 * hh:128 * hh + 128].astype(BF16)
            gf_o[hh] = g[:, 64 * hh:64 * hh + 64]
            gb_o[hh] = g[:, 256 + 64 * hh:256 + 64 * hh + 64]

    h64 = pl.BlockSpec((GLA_H, tm, 64), lambda i: (0, i, 0))
    h128 = pl.BlockSpec((GLA_H, tm, 128), lambda i: (0, i, 0))
    return pl.pallas_call(
        body, name=name, grid=(Lp // tm,),
        in_specs=[pl.BlockSpec((tm, 512), lambda i: (i, C_QA // 512)), pl.BlockSpec((tm, 512), lambda i: (i, C_VA // 512)),
                  pl.BlockSpec((tm, LANE), lambda i: (i, C_LR // LANE)), pl.BlockSpec((LANE, 512), lambda i: (0, 0)),
                  pl.BlockSpec((1, 512), lambda i: (0, 0))],
        out_specs=[h64, h64, h128, h64, h64],
        out_shape=[SDS((GLA_H, Lp, 64), F32), SDS((GLA_H, Lp, 64), F32), SDS((GLA_H, Lp, 128), BF16),
                   SDS((GLA_H, Lp, 64), F32), SDS((GLA_H, Lp, 64), F32)])(H, H, H, w2p, b2p)


def _bdot(a, b, ca, cb, precision=None):
    return lax.dot_general(a, b, ((ca, cb), ((0,), (0,))), precision=precision, preferred_element_type=F32)


def _gla_chunk_terms(q_ref, k_ref, g_ref, v_ref, G, rev):
    B = GLA_H * G
    qv = q_ref[...].reshape(B, CHUNK, GLA_DK)
    kv = k_ref[...].reshape(B, CHUNK, GLA_DK)
    gv = g_ref[...].reshape(B, CHUNK, GLA_DK)
    vv = v_ref[...].reshape(B, CHUNK, GLA_DV)
    ii = lax.broadcasted_iota(jnp.int32, (CHUNK, CHUNK), 0)
    jj = lax.broadcasted_iota(jnp.int32, (CHUNK, CHUNK), 1)
    tri = (jj >= ii) if rev else (jj <= ii)
    tb = jnp.broadcast_to(tri.astype(F32)[None], (B, CHUNK, CHUNK))
    bc = _bdot(tb, gv, (2,), (1,), HIGHEST)
    bt = bc[:, 0:1, :] if rev else bc[:, CHUNK - 1:CHUNK, :]
    eq, eki, eke = jnp.exp(bc), jnp.exp(-bc), jnp.exp(bt - bc)
    qd, ki, ke = qv * eq, kv * eki, kv * eke
    att = jnp.where(tri[None], _bdot(qd.astype(BF16), ki.astype(BF16), (2,), (2,)), 0.0)
    dm = jnp.exp(_bdot(gv, jnp.ones((B, CHUNK, GLA_DV), F32), (1,), (1,), HIGHEST))
    return dict(B=B, vv=vv, tri=tri, tb=tb, bt=bt, eq=eq, eki=eki, eke=eke, qd=qd, ki=ki, ke=ke, att=att, dm=dm)


def _gla_fwd(name, q, k, g, v, rev, G):
    Lp = q.shape[1]
    tg = G * CHUNK
    ng = Lp // tg

    def body(q_ref, k_ref, g_ref, v_ref, o_ref, ss_ref, s_scr):
        @pl.when(pl.program_id(0) == 0)
        def _():
            s_scr[...] = jnp.zeros_like(s_scr)

        t = _gla_chunk_terms(q_ref, k_ref, g_ref, v_ref, G, rev)
        B, vv = t["B"], t["vv"]
        qd = t["qd"].astype(BF16)
        oi = _bdot(t["att"].astype(BF16), vv, (2,), (1,))
        kvc = _bdot(t["ke"].astype(BF16), vv, (1,), (1,)).reshape(GLA_H, G, GLA_DK, GLA_DV)
        dm = t["dm"].reshape(GLA_H, G, GLA_DK, GLA_DV)
        s = s_scr[...]
        sp = [None] * G
        for c in (range(G - 1, -1, -1) if rev else range(G)):
            sp[c] = s
            ss_ref[c] = s
            s = dm[:, c] * s + kvc[:, c]
        s_scr[...] = s
        spb = jnp.stack(sp, axis=1).reshape(B, GLA_DK, GLA_DV).astype(BF16)
        o_ref[...] = (oi + _bdot(qd, spb, (2,), (1,))).reshape(GLA_H, tg, GLA_DV)

    blk = (lambda i: (0, ng - 1 - i, 0)) if rev else (lambda i: (0, i, 0))
    sblk = (lambda i: (ng - 1 - i, 0, 0, 0)) if rev else (lambda i: (i, 0, 0, 0))
    h64 = pl.BlockSpec((GLA_H, tg, 64), blk)
    h128 = pl.BlockSpec((GLA_H, tg, 128), blk)
    return pl.pallas_call(
        body, name=name, grid=(ng,), in_specs=[h64, h64, h64, h128],
        out_specs=[h128, pl.BlockSpec((G, GLA_H, GLA_DK, GLA_DV), sblk)],
        out_shape=[SDS((GLA_H, Lp, GLA_DV), F32), SDS((Lp // CHUNK, GLA_H, GLA_DK, GLA_DV), F32)],
        scratch_shapes=[pltpu.VMEM((GLA_H, GLA_DK, GLA_DV), F32)], compiler_params=_cp(True))(q, k, g, v)


def _gla_bwd(name, q, k, g, v, ss, do, rev, G):
    Lp = q.shape[1]
    tg = G * CHUNK
    ng = Lp // tg

    def body(q_ref, k_ref, g_ref, v_ref, ss_ref, do_ref, dq_ref, dk_ref, dg_ref, dv_ref, ds_scr):
        @pl.when(pl.program_id(0) == 0)
        def _():
            ds_scr[...] = jnp.zeros_like(ds_scr)

        t = _gla_chunk_terms(q_ref, k_ref, g_ref, v_ref, G, rev)
        B, vv, tri = t["B"], t["vv"], t["tri"]
        qd, ki, ke = t["qd"], t["ki"], t["ke"]
        qdb, kib, keb = qd.astype(BF16), ki.astype(BF16), ke.astype(BF16)
        sp = jnp.stack([ss_ref[c] for c in range(G)], axis=1).reshape(B, GLA_DK, GLA_DV)
        dob = do_ref[...].reshape(B, CHUNK, GLA_DV).astype(BF16)
        da = jnp.where(tri[None], _bdot(dob, vv, (2,), (2,)), 0.0).astype(BF16)
        dqd = _bdot(da, kib, (2,), (1,)) + _bdot(dob, sp.astype(BF16), (2,), (2,))
        dki = _bdot(da, qdb, (1,), (1,))
        dv = _bdot(t["att"].astype(BF16), dob, (1,), (1,))
        cc = _bdot(qdb, dob, (1,), (1,)).reshape(GLA_H, G, GLA_DK, GLA_DV)
        dm = t["dm"].reshape(GLA_H, G, GLA_DK, GLA_DV)
        dsc = ds_scr[...]
        dsn = [None] * G
        for c in (range(G) if rev else range(G - 1, -1, -1)):
            dsn[c] = dsc
            dsc = dm[:, c] * dsc + cc[:, c]
        ds_scr[...] = dsc
        dsn = jnp.stack(dsn, axis=1).reshape(B, GLA_DK, GLA_DV)
        dsnb = dsn.astype(BF16)
        dv = dv + _bdot(keb, dsnb, (2,), (1,))
        dke = _bdot(vv, dsnb, (2,), (2,))
        ddrow = _bdot(jnp.ones((B, CHUNK, GLA_DV), F32), dsn * sp, (2,), (2,), HIGHEST)
        dbt = ddrow * jnp.exp(t["bt"]) + jnp.sum(dke * ke, axis=1, keepdims=True)
        db = dqd * qd - dki * ki - dke * ke
        dq_ref[...] = (dqd * t["eq"]).reshape(GLA_H, tg, GLA_DK)
        dk_ref[...] = (dki * t["eki"] + dke * t["eke"]).reshape(GLA_H, tg, GLA_DK)
        dg_ref[...] = (_bdot(t["tb"], db, (1,), (1,), HIGHEST) + dbt).reshape(GLA_H, tg, GLA_DK)
        dv_ref[...] = dv.reshape(GLA_H, tg, GLA_DV)

    blk = (lambda i: (0, i, 0)) if rev else (lambda i: (0, ng - 1 - i, 0))
    sblk = (lambda i: (i, 0, 0, 0)) if rev else (lambda i: (ng - 1 - i, 0, 0, 0))
    h64 = pl.BlockSpec((GLA_H, tg, 64), blk)
    h128 = pl.BlockSpec((GLA_H, tg, 128), blk)
    return pl.pallas_call(
        body, name=name, grid=(ng,),
        in_specs=[h64, h64, h64, h128, pl.BlockSpec((G, GLA_H, GLA_DK, GLA_DV), sblk), h128],
        out_specs=[h64, h64, h64, h128],
        out_shape=[SDS((GLA_H, Lp, 64), F32), SDS((GLA_H, Lp, 64), F32), SDS((GLA_H, Lp, 64), F32),
                   SDS((GLA_H, Lp, GLA_DV), F32)],
        scratch_shapes=[pltpu.VMEM((GLA_H, GLA_DK, GLA_DV), F32)], compiler_params=_cp(True))(q, k, g, v, ss, do)


def _gla_post(name, of, ob, H, gn):
    Lp = H.shape[0]
    tm = _rt(Lp, 384)

    def body(of_ref, ob_ref, r_ref, gn_ref, a_ref, at_ref):
        parts = []
        for hh in range(GLA_H):
            o = of_ref[hh] + ob_ref[hh]
            parts.append(o * lax.rsqrt(jnp.mean(o * o, axis=-1, keepdims=True) + EPS))
        rv = r_ref[...]
        a = (jnp.concatenate(parts, axis=1) * gn_ref[...]) * (rv * jax.nn.sigmoid(rv))
        a_ref[...] = a.astype(BF16)
        at_ref[...] = a.T.astype(BF16)

    h128 = pl.BlockSpec((GLA_H, tm, 128), lambda i: (0, i, 0))
    return pl.pallas_call(
        body, name=name, grid=(Lp // tm,),
        in_specs=[h128, h128, pl.BlockSpec((tm, 512), lambda i: (i, C_RA // 512)), pl.BlockSpec((1, 512), lambda i: (0, 0))],
        out_specs=[pl.BlockSpec((tm, 512), lambda i: (i, 0)), pl.BlockSpec((512, tm), lambda i: (0, i))],
        out_shape=[SDS((Lp, 512), BF16), SDS((512, Lp), BF16)])(of, ob, H, gn)


def _gla_post_bwd(name, of, ob, H, gn, da, dH):
    Lp = H.shape[0]
    tm = _rt(Lp, 384)

    def body(of_ref, ob_ref, r_ref, gn_ref, da_ref, dh_in, do_ref, dh_ref, dgn_ref):
        rv = r_ref[...]
        sg = jax.nn.sigmoid(rv)
        dav = da_ref[...]
        gnv = gn_ref[...]
        ons, rs = [], []
        for hh in range(GLA_H):
            o = of_ref[hh] + ob_ref[hh]
            r = lax.rsqrt(jnp.mean(o * o, axis=-1, keepdims=True) + EPS)
            rs.append(r)
            ons.append(o * r)
        on = jnp.concatenate(ons, axis=1)
        dw = dav * (rv * sg)
        dh_ref[...] = (dav * (on * gnv) * (sg * (1.0 + rv * (1.0 - sg)))).astype(BF16)

        @pl.when(pl.program_id(0) == 0)
        def _():
            dgn_ref[...] = jnp.zeros_like(dgn_ref)

        dgn_ref[...] += jnp.sum(dw * on, axis=0, keepdims=True)
        don = dw * gnv
        for hh in range(GLA_H):
            dd = don[:, 128 * hh:128 * hh + 128]
            do_ref[hh] = rs[hh] * (dd - ons[hh] * jnp.mean(dd * ons[hh], axis=-1, keepdims=True))

    h128 = pl.BlockSpec((GLA_H, tm, 128), lambda i: (0, i, 0))
    rblk = pl.BlockSpec((tm, 512), lambda i: (i, C_RA // 512))
    vec = pl.BlockSpec((1, 512), lambda i: (0, 0))
    return pl.pallas_call(
        body, name=name, grid=(Lp // tm,),
        in_specs=[h128, h128, rblk, vec, pl.BlockSpec((tm, 512), lambda i: (i, 0)), pl.BlockSpec(memory_space=pl.ANY)],
        out_specs=[h128, rblk, vec],
        out_shape=[SDS((GLA_H, Lp, 128), F32), SDS(dH.shape, BF16), SDS((1, 512), F32)],
        input_output_aliases={5: 1})(of, ob, H, gn, da, dH)


def _gla_qkv_bwd(name, dqf, dqb, dkf, dkb, dvf, dvb, dH):
    Lp = dqf.shape[1]
    tm = _rt(Lp, 384)

    def body(dqf_ref, dqb_ref, dkf_ref, dkb_ref, dvf_ref, dvb_ref, dh_in, dh_ref):
        valid = _row_ids(pl.program_id(0), tm) >= PAD
        for hh in range(GLA_H):
            dq = (dqf_ref[hh] + dqb_ref[hh]) * (GLA_DK ** -0.5)
            dh_ref[:, 64 * hh:64 * hh + 64] = jnp.where(valid, dq, 0.0).astype(BF16)
            dh_ref[:, 256 + 64 * hh:256 + 64 * hh + 64] = jnp.where(valid, dkf_ref[hh] + dkb_ref[hh], 0.0).astype(BF16)
            dh_ref[:, 512 + 128 * hh:512 + 128 * hh + 128] = jnp.where(valid, dvf_ref[hh] + dvb_ref[hh], 0.0).astype(BF16)

    h64 = pl.BlockSpec((GLA_H, tm, 64), lambda i: (0, i, 0))
    h128 = pl.BlockSpec((GLA_H, tm, 128), lambda i: (0, i, 0))
    return pl.pallas_call(
        body, name=name, grid=(Lp // tm,),
        in_specs=[h64, h64, h64, h64, h128, h128, pl.BlockSpec(memory_space=pl.ANY)],
        out_specs=pl.BlockSpec((tm, 1024), lambda i: (i, C_QA // 1024)), out_shape=SDS(dH.shape, BF16),
        input_output_aliases={6: 0})(dqf, dqb, dkf, dkb, dvf, dvb, dH)


def _gla_gate_bwd(name, H, w2p, b2p, dgf, dgb, dH):
    Lp = H.shape[0]
    tm = _rt(Lp, 384)

    def body(lr_ref, w_ref, b_ref, dgf_ref, dgb_ref, dh_in, dh_ref, dw_ref, db_ref, dg_scr):
        valid = _row_ids(pl.program_id(0), tm) >= PAD
        for hh in range(GLA_H):
            dg_scr[:, 64 * hh:64 * hh + 64] = dgf_ref[hh]
            dg_scr[:, 256 + 64 * hh:256 + 64 * hh + 64] = dgb_ref[hh]
        lrb = lr_ref[...].astype(BF16)
        wv = w_ref[...]
        pre = jnp.dot(lrb, wv, preferred_element_type=F32) + b_ref[...]
        dpre = jnp.where(valid, dg_scr[...] * (1.0 / GLA_TAU) * jax.nn.sigmoid(-pre), 0.0)
        dpb = dpre.astype(BF16)
        dh_ref[...] = lax.dot_general(dpb, wv, NT, preferred_element_type=F32).astype(BF16)

        @pl.when(pl.program_id(0) == 0)
        def _():
            dw_ref[...] = jnp.zeros_like(dw_ref)
            db_ref[...] = jnp.zeros_like(db_ref)

        dw_ref[...] += lax.dot_general(lrb, dpb, TN, preferred_element_type=F32)
        db_ref[...] += jnp.sum(dpre, axis=0, keepdims=True)

    h64 = pl.BlockSpec((GLA_H, tm, 64), lambda i: (0, i, 0))
    lrblk = pl.BlockSpec((tm, LANE), lambda i: (i, C_LR // LANE))
    wblk = pl.BlockSpec((LANE, 512), lambda i: (0, 0))
    vec = pl.BlockSpec((1, 512), lambda i: (0, 0))
    return pl.pallas_call(
        body, name=name, grid=(Lp // tm,),
        in_specs=[lrblk, wblk, vec, h64, h64, pl.BlockSpec(memory_space=pl.ANY)],
        out_specs=[lrblk, wblk, vec],
        out_shape=[SDS(dH.shape, BF16), SDS((LANE, 512), F32), SDS((1, 512), F32)],
        scratch_shapes=[pltpu.VMEM((tm, 512), F32)], input_output_aliases={5: 0})(H, w2p, b2p, dgf, dgb, dH)


def _swap16(x):
    n = x.shape[1]
    lane = lax.broadcasted_iota(jnp.int32, x.shape, 1)
    return jnp.where(lane % 32 < 16, pltpu.roll(x, n - 16, 1), pltpu.roll(x, 16, 1))


def _head_mean(v, bd):
    hi = v.astype(BF16)
    lo = (v - hi.astype(F32)).astype(BF16)
    bdb = bd.astype(BF16)
    return jnp.dot(hi, bdb, preferred_element_type=F32) + jnp.dot(lo, bdb, preferred_element_type=F32)


def _headnorm_rope(x, gain, cos, sin, bd):
    r = lax.rsqrt(_head_mean(x * x, bd) + EPS)
    xh = x * r
    xn = xh * gain
    return xn * cos + _swap16(xn) * sin, xh, r


def _headnorm_rope_bwd(dxr, xh, r, gain, cos, sin, bd):
    dxn = cos * dxr + _swap16(sin * dxr)
    u = dxn * gain
    dx = r * (u - xh * _head_mean(u * xh, bd))
    return dx, jnp.sum(dxn * xh, axis=0, keepdims=True)


def _att_prep(name, H, gq, gk, cos, sin, bd):
    Lp = H.shape[0]
    tm = _rt(Lp, 384)

    def body(q_ref, kv_ref, gq_ref, gk_ref, c_ref, s_ref, bd_ref, qp_ref, k_ref, v_ref):
        c1, s1 = c_ref[...], s_ref[...]
        c4, s4 = jnp.concatenate([c1] * 4, axis=1), jnp.concatenate([s1] * 4, axis=1)
        xr, _, _ = _headnorm_rope(q_ref[...], gq_ref[...], c4, s4, bd_ref[...])
        xr = xr * (HD ** -0.5)
        lane = lax.broadcasted_iota(jnp.int32, (tm, LANE), 1)
        for hh in range(ATT_H):
            grp = xr[:, LANE * (hh // 2):LANE * (hh // 2) + LANE]
            e, gi = hh % 2, hh // 4
            if e != gi:
                grp = pltpu.roll(grp, 64, 1)
            keep = (lane < 64) if gi == 0 else (lane >= 64)
            qp_ref[hh] = jnp.where(keep, grp, 0.0).astype(BF16)
        kv = kv_ref[...]
        kr, _, _ = _headnorm_rope(kv[:, :LANE], gk_ref[...], c1, s1, bd_ref[0:LANE, 0:LANE])
        k_ref[...] = kr.astype(BF16)
        v_ref[...] = kv[:, LANE:].astype(BF16)

    row128 = pl.BlockSpec((tm, LANE), lambda i: (i, 0))
    return pl.pallas_call(
        body, name=name, grid=(Lp // tm,),
        in_specs=[pl.BlockSpec((tm, 512), lambda i: (i, C_QB // 512)), pl.BlockSpec((tm, 256), lambda i: (i, C_KB // 256)),
                  pl.BlockSpec((1, 512), lambda i: (0, 0)), pl.BlockSpec((1, LANE), lambda i: (0, 0)), row128, row128,
                  pl.BlockSpec((512, 512), lambda i: (0, 0))],
        out_specs=[pl.BlockSpec((ATT_H, tm, LANE), lambda i: (0, i, 0)), row128, row128],
        out_shape=[SDS((ATT_H, Lp, LANE), BF16), SDS((Lp, LANE), BF16), SDS((Lp, LANE), BF16)])(H, H, gq, gk, cos, sin, bd)


def _att_q_bwd(name, H, gq, cos, sin, bd, dqp, dH):
    Lp = H.shape[0]
    tm = _rt(Lp, 384)

    def body(q_ref, gq_ref, c_ref, s_ref, bd_ref, dqp_ref, dh_in, dh_ref, dg_ref):
        c1, s1 = c_ref[...], s_ref[...]
        c4, s4 = jnp.concatenate([c1] * 4, axis=1), jnp.concatenate([s1] * 4, axis=1)
        gqv = gq_ref[...]
        _, xh, r = _headnorm_rope(q_ref[...], gqv, c4, s4, bd_ref[...])
        lane = lax.broadcasted_iota(jnp.int32, (tm, LANE), 1)
        groups = []
        for j in range(ATT_H // 2):
            pieces = []
            for e in range(2):
                hh = 2 * j + e
                piece = dqp_ref[hh]
                if e != hh // 4:
                    piece = pltpu.roll(piece, 64, 1)
                pieces.append(piece)
            groups.append(jnp.where(lane < 64, pieces[0], pieces[1]))
        dxr = jnp.concatenate(groups, axis=1) * (HD ** -0.5)
        dx, dg = _headnorm_rope_bwd(dxr, xh, r, gqv, c4, s4, bd_ref[...])
        dh_ref[...] = dx.astype(BF16)

        @pl.when(pl.program_id(0) == 0)
        def _():
            dg_ref[...] = jnp.zeros_like(dg_ref)

        dg_ref[...] += dg

    row128 = pl.BlockSpec((tm, LANE), lambda i: (i, 0))
    qblk = pl.BlockSpec((tm, 512), lambda i: (i, C_QB // 512))
    vec = pl.BlockSpec((1, 512), lambda i: (0, 0))
    return pl.pallas_call(
        body, name=name, grid=(Lp // tm,),
        in_specs=[qblk, vec, row128, row128, pl.BlockSpec((512, 512), lambda i: (0, 0)),
                  pl.BlockSpec((ATT_H, tm, LANE), lambda i: (0, i, 0)), pl.BlockSpec(memory_space=pl.ANY)],
        out_specs=[qblk, vec], out_shape=[SDS(dH.shape, BF16), SDS((1, 512), F32)],
        input_output_aliases={6: 0})(H, gq, cos, sin, bd, dqp, dH)


def _att_kv_bwd(name, H, gk, cos, sin, bd, dkr, dvb, dH):
    Lp = H.shape[0]
    tm = _rt(Lp, 384)

    def body(kv_ref, gk_ref, c_ref, s_ref, bd_ref, dk_ref, dv_ref, dh_in, dh_ref, dg_ref):
        c1, s1 = c_ref[...], s_ref[...]
        gkv = gk_ref[...]
        bdv = bd_ref[0:LANE, 0:LANE]
        _, xh, r = _headnorm_rope(kv_ref[:, :LANE], gkv, c1, s1, bdv)
        dx, dg = _headnorm_rope_bwd(dk_ref[...], xh, r, gkv, c1, s1, bdv)
        dh_ref[:, :LANE] = dx.astype(BF16)
        dh_ref[:, LANE:] = dv_ref[...].astype(BF16)

        @pl.when(pl.program_id(0) == 0)
        def _():
            dg_ref[...] = jnp.zeros_like(dg_ref)

        dg_ref[...] += dg

    row128 = pl.BlockSpec((tm, LANE), lambda i: (i, 0))
    kvblk = pl.BlockSpec((tm, 256), lambda i: (i, C_KB // 256))
    vec = pl.BlockSpec((1, LANE), lambda i: (0, 0))
    return pl.pallas_call(
        body, name=name, grid=(Lp // tm,),
        in_specs=[kvblk, vec, row128, row128, pl.BlockSpec((512, 512), lambda i: (0, 0)), row128, row128,
                  pl.BlockSpec(memory_space=pl.ANY)],
        out_specs=[kvblk, vec], out_shape=[SDS(dH.shape, BF16), SDS((1, LANE), F32)],
        input_output_aliases={7: 0})(H, gk, cos, sin, bd, dkr, dvb, dH)


def _ridden_call(core, rider, *, name, grid, in_specs, out_specs, out_shape, scratch_shapes, args, aliases=None,
                 mid_back=RIDER_MID_BACK):
    n_in, n_out, n_scr = len(in_specs), len(out_specs), len(scratch_shapes)
    r_in = len(rider.ins) if rider else 0
    r_out = len(rider.out_shape) if rider else 0
    total = int(np.prod(grid))
    mid_step = max(total - 1 - mid_back, 0)

    def body(*refs):
        ins, r_ins = refs[:n_in], refs[n_in:n_in + r_in]
        o0 = n_in + r_in
        outs, r_outs = refs[o0:o0 + n_out], refs[o0 + n_out:o0 + n_out + r_out]
        s0 = o0 + n_out + r_out
        scr, r_sems = refs[s0:s0 + n_scr], refs[s0 + n_scr:]
        if rider is None:
            core(ins, outs, scr)
            return
        step = pl.program_id(0)
        for ax in range(1, len(grid)):
            step = step * grid[ax] + pl.program_id(ax)
        start, mid, end = rider.hooks(r_ins, r_outs, r_sems)
        pl.when(step == 0)(start)
        core(ins, outs, scr)
        pl.when(step == mid_step)(mid)
        pl.when(step == total - 1)(end)

    any_spec = pl.BlockSpec(memory_space=pl.ANY)
    res = pl.pallas_call(
        body, name=name, grid=grid, in_specs=list(in_specs) + [any_spec] * r_in, out_specs=list(out_specs) + [any_spec] * r_out,
        out_shape=list(out_shape) + (list(rider.out_shape) if rider else []),
        scratch_shapes=list(scratch_shapes) + (list(rider.sems) if rider else []),
        input_output_aliases={**(aliases or {}), **({n_in + a: n_out + b for a, b in rider.aliases.items()} if rider else {})},
        compiler_params=_cp(True))(*args, *(rider.ins if rider else []))
    return res[:n_out], res[n_out:]


def _att_fwd(name, qp, kr, vb, bias, rider=None):
    Lp = kr.shape[0]
    tq = _rt(Lp, 384)
    sub = ATT_SUB_FWD if tq % ATT_SUB_FWD == 0 else tq

    def core(ins, outs, scr):
        q_ref, k_ref, v_ref, bias_ref = ins
        b_ref, bt_ref, lse_ref = outs
        j, i = pl.program_id(0), pl.program_id(1)
        valid = _row_ids(i, tq) >= PAD
        kk, vv, bb = k_ref[...], v_ref[...], bias_ref[...]
        chains = [(e, r) for e in range(2) for r in range(tq // sub)]
        ss = [lax.dot_general(q_ref[e, r * sub:(r + 1) * sub, :], kk, NT, preferred_element_type=F32) for e, r in chains]
        ps, ls = [], []
        for (e, r), s in zip(chains, ss):
            s = jnp.concatenate([s[:, :LANE] + bb[:, :LANE], s[:, LANE:]], axis=1)
            m = jnp.max(s, axis=-1, keepdims=True)
            p = jnp.exp(s - m)
            l = jnp.sum(p, axis=-1, keepdims=True)
            ps.append(p.astype(BF16))
            ls.append(l)
            lse_ref[e, r * sub:(r + 1) * sub, :] = m + jnp.log(l)
        os_ = [jnp.dot(p, vv, preferred_element_type=F32) * (1.0 / l) for p, l in zip(ps, ls)]
        n_sub = tq // sub
        outs = [jnp.where(valid, jnp.concatenate(os_[e * n_sub:(e + 1) * n_sub], axis=0), 0.0) for e in range(2)]
        lane = lax.broadcasted_iota(jnp.int32, (tq, LANE), 1)
        low = j < 2
        o0 = jnp.where(low, outs[0], pltpu.roll(outs[0], 64, 1))
        o1 = jnp.where(low, pltpu.roll(outs[1], 64, 1), outs[1])
        blk = jnp.where(lane < 64, o0, o1)
        b_ref[...] = blk.astype(BF16)
        bt_ref[...] = blk.T.astype(BF16)

    full = pl.BlockSpec((Lp, LANE), lambda j, i: (0, 0))
    return _ridden_call(
        core, rider, name=name, grid=(ATT_H // 2, Lp // tq),
        in_specs=[pl.BlockSpec((2, tq, LANE), lambda j, i: (j, i, 0)), full, full, pl.BlockSpec((1, Lp), lambda j, i: (0, 0))],
        out_specs=[pl.BlockSpec((tq, LANE), lambda j, i: (i, j)), pl.BlockSpec((LANE, tq), lambda j, i: (j, i)),
                   pl.BlockSpec((2, tq, 1), lambda j, i: (j, i, 0))],
        out_shape=[SDS((Lp, 512), BF16), SDS((512, Lp), BF16), SDS((ATT_H, Lp, 1), F32)],
        scratch_shapes=[], args=(qp, kr, vb, bias))


def _att_bwd(name, qp, kr, vb, bias, lse, db, b, rider=None):
    Lp = kr.shape[0]
    tq = _rt(Lp, 384)
    sub = ATT_SUB_BWD if tq % ATT_SUB_BWD == 0 else tq
    nq = Lp // tq

    def core(ins, outs, scr):
        q_ref, k_ref, v_ref, bias_ref, lse_ref, db_ref, b_ref = ins
        dq_ref, dk_ref, dv_ref = outs
        dkt_scr, dvt_scr = scr
        j, i = pl.program_id(0), pl.program_id(1)

        @pl.when((j == 0) & (i == 0))
        def _():
            dkt_scr[...] = jnp.zeros_like(dkt_scr)
            dvt_scr[...] = jnp.zeros_like(dvt_scr)

        kk, vv, bb = k_ref[...], v_ref[...], bias_ref[...]
        dbv = db_ref[...]
        rolled = pltpu.roll(dbv, 64, 1)
        lane = lax.broadcasted_iota(jnp.int32, (tq, LANE), 1)
        low = j < 2
        first = jnp.where(low, 0, 64)
        keep = (lane >= first) & (lane < first + 64)
        prod = dbv * b_ref[...].astype(F32)
        deltas = [jnp.sum(jnp.where(lane < 64, prod, 0.0), axis=-1, keepdims=True),
                  jnp.sum(jnp.where(lane >= 64, prod, 0.0), axis=-1, keepdims=True)]
        dkt, dvt = 0.0, 0.0
        for e in range(2):
            src = jnp.where(low, dbv, rolled) if e == 0 else jnp.where(low, rolled, dbv)
            dop = jnp.where(keep, src, 0.0).astype(BF16)
            pbs, dss = [], []
            for r in range(tq // sub):
                rs = slice(r * sub, (r + 1) * sub)
                s = lax.dot_general(q_ref[e, rs, :], kk, NT, preferred_element_type=F32)
                s = jnp.concatenate([s[:, :LANE] + bb[:, :LANE], s[:, LANE:]], axis=1)
                p = jnp.exp(s - lse_ref[e, rs, :])
                dp = lax.dot_general(dop[rs], vv, NT, preferred_element_type=F32)
                ds = (p * (dp - deltas[e][rs])).astype(BF16)
                dq_ref[e, rs, :] = jnp.where(keep[rs], jnp.dot(ds, kk, preferred_element_type=F32), 0.0)
                pbs.append(p.astype(BF16))
                dss.append(ds)
            dkt = dkt + lax.dot_general(q_ref[e], jnp.concatenate(dss, axis=0), TN, preferred_element_type=F32)
            dvt = dvt + lax.dot_general(dop, jnp.concatenate(pbs, axis=0), TN, preferred_element_type=F32)
        dkt_scr[...] += dkt
        dvt_scr[...] += dvt

        @pl.when((j == ATT_H // 2 - 1) & (i == nq - 1))
        def _():
            dk_ref[...] = dkt_scr[...].T
            dv_ref[...] = dvt_scr[...].T

    full = pl.BlockSpec((Lp, LANE), lambda j, i: (0, 0))
    pair = pl.BlockSpec((2, tq, LANE), lambda j, i: (j, i, 0))
    return _ridden_call(
        core, rider, name=name, grid=(ATT_H // 2, nq),
        in_specs=[pair, full, full, pl.BlockSpec((1, Lp), lambda j, i: (0, 0)), pl.BlockSpec((2, tq, 1), lambda j, i: (j, i, 0)),
                  pl.BlockSpec((tq, LANE), lambda j, i: (i, j)), pl.BlockSpec((tq, LANE), lambda j, i: (i, j))],
        out_specs=[pair, full, full],
        out_shape=[SDS((ATT_H, Lp, LANE), F32), SDS((Lp, LANE), F32), SDS((Lp, LANE), F32)],
        scratch_shapes=[pltpu.VMEM((LANE, Lp), F32), pltpu.VMEM((LANE, Lp), F32)], args=(qp, kr, vb, bias, lse, db, b))


def _rope_tables(n_tok):
    f32 = np.float32
    rows = n_tok // GRID_W
    row = np.repeat(np.arange(rows), GRID_W).astype(f32)
    col = np.tile(np.arange(GRID_W), rows).astype(f32)
    axis_dim = HD // 2
    inv = np.power(f32(ROPE_THETA), -np.arange(0, axis_dim, 2, dtype=f32) / f32(axis_dim)).astype(f32)
    ang = np.concatenate([row[:, None] * inv, col[:, None] * inv], axis=-1).astype(f32)
    ang = np.concatenate([np.zeros((LANE, axis_dim), f32), ang], axis=0)
    c, s = np.cos(ang).astype(f32), np.sin(ang).astype(f32)
    c64 = np.concatenate([c[:, :16], c[:, :16], c[:, 16:], c[:, 16:]], axis=1)
    s64 = np.concatenate([-s[:, :16], s[:, :16], -s[:, 16:], s[:, 16:]], axis=1)
    return jnp.asarray(np.concatenate([c64, c64], axis=1)), jnp.asarray(np.concatenate([s64, s64], axis=1))


def _ffn_fwd(tag, h, W, l, j, rider=None):
    Lp = h.shape[0]
    tm = _rt(Lp, 384)
    g = W["norm_gains"][l, 2 * j].reshape(1, D)

    def core(ins, outs, scr):
        h_ref, g_ref, wg_ref, wu_ref, wd_ref = ins
        h2_ref, a_ref, b_ref, st_ref, nt_ref = outs
        pad_scr, = scr
        x = h_ref[...]
        y = x * lax.rsqrt(jnp.mean(x * x, axis=-1, keepdims=True) + EPS) * g_ref[...]
        nt_ref[...] = y.T.astype(BF16)
        nv = y.astype(BF16)
        pad_scr[:, FF_S:] = jnp.zeros((tm, FF_P - FF_S), F32)
        acc = 0.0
        for s in range(NSH):
            a = jnp.dot(nv, wg_ref[s], preferred_element_type=F32)
            b = jnp.dot(nv, wu_ref[s], preferred_element_type=F32)
            sv = a * jax.nn.sigmoid(a) * b
            a_ref[s, :, :FF_S] = a.astype(BF16)
            a_ref[s, :, FF_S:] = jnp.zeros((tm, FF_P - FF_S), BF16)
            b_ref[s, :, :FF_S] = b.astype(BF16)
            b_ref[s, :, FF_S:] = jnp.zeros((tm, FF_P - FF_S), BF16)
            pad_scr[:, :FF_S] = sv
            st_ref[s] = pad_scr[...].T.astype(BF16)
            acc = acc + jnp.dot(sv.astype(BF16), wd_ref[s], preferred_element_type=F32)
        h2_ref[...] = x + 0.5 * acc

    once = dict(pipeline_mode=pl.Buffered(1))
    wup = pl.BlockSpec((NSH, None, D, FF_S), lambda i: (0, l, j, 0), **once)
    row = pl.BlockSpec((tm, D), lambda i: (i, 0))
    slab = pl.BlockSpec((NSH, tm, FF_P), lambda i: (0, i, 0))
    (h2, a, b, st, nt), carried = _ridden_call(
        core, rider, name=f"ffn_{tag}", grid=(Lp // tm,),
        in_specs=[row, pl.BlockSpec((1, D), lambda i: (0, 0)), wup, wup,
                  pl.BlockSpec((NSH, None, FF_S, D), lambda i: (0, l, j, 0), **once)],
        out_specs=[row, slab, slab, pl.BlockSpec((NSH, FF_P, tm), lambda i: (0, 0, i)), pl.BlockSpec((D, tm), lambda i: (0, i))],
        out_shape=[SDS((Lp, D), F32), SDS((NSH, Lp, FF_P), BF16), SDS((NSH, Lp, FF_P), BF16), SDS((NSH, FF_P, Lp), BF16),
                   SDS((D, Lp), BF16)],
        scratch_shapes=[pltpu.VMEM((tm, FF_P), F32)], args=(h, g, W["gate"], W["up"], W["down"]), mid_back=1)
    if rider is not None:
        W = rider.updated(W, carried)
    return h2, dict(h=h, g=g, nt=nt, a=a, b=b, st=st), W


def _ffn_bwd(tag, dh, sv, W, G, l, j, rider=None):
    Lp = dh.shape[0]
    tn = 512
    (da, db, dh_prev, dg), carried = _ffn_bwd_acts(tag, dh, sv, W, l, j, rider)
    G["down"] = _mm(f"bwd_{tag}", sv["st"], dh, grid=(NSH, D // tn),
                    a_spec=pl.BlockSpec((None, FF_P, Lp), lambda s, n: (s, 0, 0)), a_sl=(slice(0, FF_S), slice(None)),
                    b_spec=pl.BlockSpec((Lp, tn), lambda s, n: (0, n)),
                    o_spec=pl.BlockSpec((None, None, FF_S, tn), lambda s, n: (s, j, 0, n)), o_shape=(NSH, 2, FF_S, D), o_dtype=F32,
                    dims=NN, acc_shape=None, scale=0.5, into=G.get("down"))
    for key, dact in (("gate", da), ("up", db)):
        G[key] = _mm(f"bw{key[0]}_{tag}", sv["nt"], dact, grid=(NSH, D // tn),
                     a_spec=pl.BlockSpec((tn, Lp), lambda s, m: (m, 0)),
                     b_spec=pl.BlockSpec((None, Lp, FF_P), lambda s, m: (s, 0, 0)), b_sl=(slice(None), slice(0, FF_S)),
                     o_spec=pl.BlockSpec((None, None, tn, FF_S), lambda s, m: (s, j, m, 0)), o_shape=(NSH, 2, D, FF_S),
                     o_dtype=F32, dims=NN, acc_shape=None, into=G.get(key))
    return dh_prev, dg, carried


def _ffn_bwd_acts(tag, dh, sv, W, l, j, rider=None):
    Lp = dh.shape[0]
    tm = _rt(Lp, 384)

    def core(ins, outs, scr):
        dh_ref, h_ref, g_ref, wd_ref, wg_ref, wu_ref, a_ref, b_ref = ins
        da_ref, db_ref, dho_ref, dg_ref = outs
        dhv = dh_ref[...]
        dhb = dhv.astype(BF16)
        dss = [0.5 * lax.dot_general(dhb, wd_ref[s], NT, preferred_element_type=F32) for s in range(NSH)]
        das, dbs = [], []
        for s in range(NSH):
            av = a_ref[s, :, :FF_S].astype(F32)
            bv = b_ref[s, :, :FF_S].astype(F32)
            sg = jax.nn.sigmoid(av)
            da = (dss[s] * bv * (sg * (1.0 + av * (1.0 - sg)))).astype(BF16)
            db = (dss[s] * (av * sg)).astype(BF16)
            da_ref[s, :, :FF_S] = da
            da_ref[s, :, FF_S:] = jnp.zeros((tm, FF_P - FF_S), BF16)
            db_ref[s, :, :FF_S] = db
            db_ref[s, :, FF_S:] = jnp.zeros((tm, FF_P - FF_S), BF16)
            das.append(da)
            dbs.append(db)
        dn = 0.0
        for s in range(NSH):
            dn = dn + (lax.dot_general(das[s], wg_ref[s], NT, preferred_element_type=F32)
                       + lax.dot_general(dbs[s], wu_ref[s], NT, preferred_element_type=F32))
        x = h_ref[...]
        r = lax.rsqrt(jnp.mean(x * x, axis=-1, keepdims=True) + EPS)
        xh = x * r
        u = dn * g_ref[...]
        dho_ref[...] = dhv + r * (u - xh * jnp.mean(u * xh, axis=-1, keepdims=True))

        @pl.when(pl.program_id(0) == 0)
        def _():
            dg_ref[...] = jnp.zeros_like(dg_ref)

        dg_ref[...] += jnp.sum(dn * xh, axis=0, keepdims=True)

    once = dict(pipeline_mode=pl.Buffered(1))
    wup = pl.BlockSpec((NSH, None, D, FF_S), lambda i: (0, l, j, 0), **once)
    row = pl.BlockSpec((tm, D), lambda i: (i, 0))
    slab = pl.BlockSpec((NSH, tm, FF_P), lambda i: (0, i, 0))
    vec = pl.BlockSpec((1, D), lambda i: (0, 0))
    return _ridden_call(
        core, rider, name=f"bffn_{tag}", grid=(Lp // tm,),
        in_specs=[row, row, vec, pl.BlockSpec((NSH, None, FF_S, D), lambda i: (0, l, j, 0), **once), wup, wup, slab, slab],
        out_specs=[slab, slab, row, vec],
        out_shape=[SDS((NSH, Lp, FF_P), BF16), SDS((NSH, Lp, FF_P), BF16), SDS((Lp, D), F32), SDS((1, D), F32)],
        scratch_shapes=[], args=(dh, sv["h"], sv["g"], W["down"], W["gate"], W["up"], sv["a"], sv["b"]))


def _mixer_fwd(tag, h, W, winp, C, l, G_chunks, riders=(None, None)):
    Lp = h.shape[0]
    tm = _rt(Lp, 1408)
    g = W["norm_gains"][l, 1].reshape(1, D)
    (H, zt), carried = _norm_project(f"win_{tag}", h, g, winp, riders[0])
    if riders[0] is not None:
        W = riders[0].updated(W, carried)
    rider = riders[1]
    w2p, b2p = C["w2p"][l], C["b2p"][l]
    qh, kh, vh, gf, gb = _gla_prep(f"glap_{tag}", H, w2p, b2p)
    of, sf = _gla_fwd(f"glaf_{tag}", qh, kh, gf, vh, False, G_chunks)
    ob, sb = _gla_fwd(f"glar_{tag}", qh, kh, gb, vh, True, G_chunks)
    gn = W["gn"][l].reshape(1, 512)
    a, at = _gla_post(f"glao_{tag}", of, ob, H, gn)
    gq, gk = C["gq"][l], C["gk"][l]
    qp, kr, vb = _att_prep(f"attp_{tag}", H, gq, gk, C["cos"], C["sin"], C["bd"])
    (b, bt, lse), carried = _att_fwd(f"attf_{tag}", qp, kr, vb, C["bias"], rider)
    if rider is not None:
        W = rider.updated(W, carried)
    row = pl.BlockSpec((tm, D), lambda i: (i, 0))
    proj = functools.partial(
        _mm, grid=(Lp // tm,), a_spec=pl.BlockSpec((tm, 512), lambda i: (i, 0)),
        b_spec=pl.BlockSpec((NSH, None, 512, 256), lambda i: (0, l, 0, 0)), o_spec=row, o_shape=(Lp, D), o_dtype=F32, dims=NN,
        acc_shape=None, shards="cols")
    pa = proj(f"pa_{tag}", a, W["wpa"])
    pb = proj(f"pb_{tag}", b, W["wpb"])
    bm = W["bm"][l]
    y, yt = _merge_fwd(f"mrg_{tag}", H, pa, pb, bm)
    h2 = _mm(f"wout_{tag}", y, W["wout"], grid=(Lp // tm,), a_spec=row,
             b_spec=pl.BlockSpec((NSH, None, 256, D), lambda i: (0, l, 0, 0)), o_spec=row, o_shape=(Lp, D), o_dtype=F32,
             dims=NN, acc_shape=None, res=h, res_spec=row, shards="rows")
    sv = dict(h=h, g=g, zt=zt, H=H, w2p=w2p, b2p=b2p, qh=qh, kh=kh, vh=vh, gf=gf, gb=gb, of=of, ob=ob, sf=sf, sb=sb, gn=gn,
              at=at, gq=gq, gk=gk, qp=qp, kr=kr, vb=vb, b=b, bt=bt, lse=lse, pa=pa, pb=pb, bm=bm, yt=yt, winp=winp)
    return h2, sv, W


def _mixer_bwd(tag, dh, sv, W, C, G, S, l, G_chunks, rider=None):
    Lp = dh.shape[0]
    tm = _rt(Lp, 1408)
    H = sv["H"]
    row = pl.BlockSpec((tm, D), lambda i: (i, 0))
    dy = _mm(f"bdy_{tag}", dh, W["wout"], grid=(Lp // tm,), a_spec=row,
             b_spec=pl.BlockSpec((NSH, None, 256, D), lambda i: (0, l, 0, 0)), o_spec=row, o_shape=(Lp, D), o_dtype=F32, dims=NT,
             acc_shape=None, shards="rows")
    tn = 512
    G["wout"] = _mm(f"bwo_{tag}", sv["yt"], dh, grid=(NSH, D // tn), a_spec=pl.BlockSpec((256, Lp), lambda s, n: (s, 0)),
                    b_spec=pl.BlockSpec((Lp, tn), lambda s, n: (0, n)),
                    o_spec=pl.BlockSpec((None, 256, tn), lambda s, n: (s, 0, n)), o_shape=(NSH, 256, D), o_dtype=F32, dims=NN,
                    acc_shape=None)
    dpa, dpb, dH, S["bm"][l] = _merge_bwd(f"bmrg_{tag}", H, sv["pa"], sv["pb"], sv["bm"], dy)
    dbranch = {}
    for key, dp, xt in (("wpa", dpa, sv["at"]), ("wpb", dpb, sv["bt"])):
        dbranch[key] = _mm(f"bx{key[2]}_{tag}", dp, W[key], grid=(Lp // tm,), a_spec=row,
                           b_spec=pl.BlockSpec((NSH, None, 512, 256), lambda i: (0, l, 0, 0)),
                           o_spec=pl.BlockSpec((tm, 512), lambda i: (i, 0)), o_shape=(Lp, 512), o_dtype=F32, dims=NT,
                           acc_shape=None, shards="cols")
        G[key] = _mm(f"bw{key[2]}_{tag}", xt, dp, grid=(NSH,), a_spec=pl.BlockSpec((512, Lp), lambda s: (0, 0)),
                     b_spec=pl.BlockSpec((Lp, 256), lambda s: (0, s)),
                     o_spec=pl.BlockSpec((None, 512, 256), lambda s: (s, 0, 0)), o_shape=(NSH, 512, 256), o_dtype=F32, dims=NN,
                     acc_shape=None)
    (dqp, dkr, dvb), carried = _att_bwd(f"attb_{tag}", sv["qp"], sv["kr"], sv["vb"], C["bias"], sv["lse"], dbranch["wpb"],
                                        sv["b"], rider)
    dH, S["gq"][l] = _att_q_bwd(f"attq_{tag}", H, sv["gq"], C["cos"], C["sin"], C["bd"], dqp, dH)
    dH, S["gk"][l] = _att_kv_bwd(f"attk_{tag}", H, sv["gk"], C["cos"], C["sin"], C["bd"], dkr, dvb, dH)
    do, dH, S["gn"][l] = _gla_post_bwd(f"bglo_{tag}", sv["of"], sv["ob"], H, sv["gn"], dbranch["wpa"], dH)
    dqf, dkf, dgf, dvf = _gla_bwd(f"bglf_{tag}", sv["qh"], sv["kh"], sv["gf"], sv["vh"], sv["sf"], do, False, G_chunks)
    dqb, dkb, dgb, dvr = _gla_bwd(f"bglr_{tag}", sv["qh"], sv["kh"], sv["gb"], sv["vh"], sv["sb"], do, True, G_chunks)
    dH = _gla_qkv_bwd(f"bglq_{tag}", dqf, dqb, dkf, dkb, dvf, dvr, dH)
    dH, S["w2p"][l], S["b2p"][l] = _gla_gate_bwd(f"bglg_{tag}", H, sv["w2p"], sv["b2p"], dgf, dgb, dH)
    tmm = 256
    G["winp"] = _mm(f"bwi_{tag}", sv["zt"], dH, grid=(HP // W_CHUNK, D // tmm), a_spec=pl.BlockSpec((tmm, Lp), lambda n, m: (m, 0)),
                    b_spec=pl.BlockSpec((Lp, W_CHUNK), lambda n, m: (0, n)), o_spec=pl.BlockSpec((tmm, W_CHUNK), lambda n, m: (m, n)),
                    o_shape=(D, HP), o_dtype=F32, dims=NN, acc_shape=None)
    dh_prev, dg = _norm_project_bwd(f"bdz_{tag}", dH, sv["winp"], sv["h"], sv["g"], dh)
    return dh_prev, dg, carried


def _winp_layer(win_g, l):
    return _win_to_padded(jnp.transpose(win_g[:, l], (1, 0, 2)).reshape(D, D_IN))


def _local_step(x2, tgt2, W, comm=None):
    n_tok = x2.shape[0]
    Lp = n_tok + LANE
    nc = Lp // CHUNK
    g_chunks = max(g for g in (1, 2, 3, 6) if nc % g == 0)
    cos, sin = _rope_tables(n_tok)
    bd = jnp.asarray(np.kron(np.eye(ATT_H, dtype=np.float32), np.full((HD, HD), 1.0 / HD, np.float32)))
    bias = jnp.where(jnp.arange(Lp) >= PAD, 0.0, -1e30).astype(F32).reshape(1, Lp)
    w2, b2 = W["w2"], W["b2"]
    w2p = jnp.zeros((DEPTH, LANE, 512), F32)
    w2p = w2p.at[:, 0:GLA_RANK, 0:256].set(w2[:, 0]).at[:, GLA_RANK:2 * GLA_RANK, 256:512].set(w2[:, 1]).astype(BF16)
    C = dict(cos=cos, sin=sin, bd=bd, bias=bias, w2p=w2p, b2p=b2.reshape(DEPTH, 1, 512),
             gq=jnp.tile(W["qn"], (1, ATT_H)).reshape(DEPTH, 1, 512), gk=jnp.tile(W["kn"], (1, ATT_KV)).reshape(DEPTH, 1, LANE))
    h = jnp.concatenate([jnp.zeros((PAD, D), F32), W["meta"], x2], axis=0)
    saved = []
    for l in range(DEPTH):
        h, s0, W = _ffn_fwd(f"l{l}a", h, W, l, 0, comm.first_rider(W) if comm and l == 0 else None)
        riders = (comm.fwd_rider(W, l, ("down",)), comm.fwd_rider(W, l, ("gate", "up"))) if comm else (None, None)
        h, sm, W = _mixer_fwd(f"l{l}m", h, W, _winp_layer(W["win"], l), C, l, g_chunks, riders)
        h, s1, W = _ffn_fwd(f"l{l}b", h, W, l, 1, comm.fwd_rider(W, l, MIXER_W) if comm else None)
        saved.append((s0, sm, s1))
    dh, dfin, loss = _loss_head("loss_head", h, W["fin"].reshape(1, D), tgt2)
    S = dict(bm=[None] * DEPTH, gq=[None] * DEPTH, gk=[None] * DEPTH, gn=[None] * DEPTH, w2p=[None] * DEPTH,
             b2p=[None] * DEPTH, ng=[[None] * 3 for _ in range(DEPTH)])
    Gs = [None] * DEPTH
    for l in reversed(range(DEPTH)):
        s0, sm, s1 = saved[l]
        G = {}
        rider = comm.ffn_rider(l) if comm else None
        dh, S["ng"][l][2], carried = _ffn_bwd(f"l{l}b", dh, s1, W, G, l, 1, rider)
        if rider is not None:
            comm.ffn_carried(l, carried)
        rider = comm.bwd_rider(l) if comm else None
        dh, S["ng"][l][1], carried = _mixer_bwd(f"l{l}m", dh, sm, W, C, G, S, l, g_chunks, rider)
        if rider is not None:
            comm.bwd_carried(l, carried)
        rider = comm.last_rider(l) if comm else None
        dh, S["ng"][l][0], carried = _ffn_bwd(f"l{l}a", dh, s0, W, G, l, 0, rider)
        if rider is not None:
            comm.last_carried(carried)
        Gs[l] = G
        if comm:
            comm.layer_done(l, G)
    small = dict(
        meta=dh[PAD:LANE],
        norm_gains=jnp.stack([jnp.concatenate(S["ng"][l], axis=0) for l in range(DEPTH)]),
        w2=jnp.stack([jnp.stack([S["w2p"][l][0:GLA_RANK, 0:256], S["w2p"][l][GLA_RANK:2 * GLA_RANK, 256:512]]) for l in range(DEPTH)]),
        b2=jnp.stack([S["b2p"][l].reshape(2, 256) for l in range(DEPTH)]),
        gn=jnp.concatenate(S["gn"], axis=0),
        qn=jnp.stack([S["gq"][l].reshape(ATT_H, HD).sum(0) for l in range(DEPTH)]),
        kn=jnp.stack([S["gk"][l].reshape(ATT_KV, HD).sum(0) for l in range(DEPTH)]),
        bm=jnp.stack(S["bm"]),
        fin=dfin.reshape(D),
    )
    return loss[0, 0], dh, Gs, small


def _win_to_padded(w):
    pad = jnp.zeros(w.shape[:-1] + (HP - D_IN,), w.dtype)
    return jnp.concatenate([w[..., 2336:4384], w[..., 0:1536], w[..., 1568:2336], w[..., 1536:1568], pad], axis=-1)


def _win_from_padded(w):
    return jnp.concatenate([w[..., 2048:3584], w[..., 4352:4384], w[..., 3584:4352], w[..., 0:2048]], axis=-1)


def _assemble(g):
    W = dict(gate=g["gate"], up=g["up"], down=g["down"], win=g["win"], wpa=g["wpa"], wpb=g["wpb"], wout=g["wout"])
    sm = g["small"]
    parts = _unpack(sm, SHARDED_SMALL)
    W["meta"] = jnp.transpose(parts["meta"], (1, 0, 2)).reshape(N_META, D)
    W["norm_gains"] = jnp.transpose(parts["norm_gains"], (1, 2, 0, 3)).reshape(DEPTH, 3, D)
    W["w2"] = jnp.transpose(parts["w2"], (1, 2, 3, 0, 4)).reshape(DEPTH, 2, GLA_RANK, 256)
    W["b2"] = jnp.transpose(parts["b2"], (1, 2, 0, 3)).reshape(DEPTH, 2, 256)
    W["bm"] = jnp.transpose(parts["bm"], (1, 2, 0, 3)).reshape(DEPTH, 2, D)
    return W


SHARDED_SMALL = dict(meta=(N_META, 256), norm_gains=(DEPTH, 3, 256), w2=(DEPTH, 2, GLA_RANK, 64), b2=(DEPTH, 2, 64),
                     bm=(DEPTH, 2, 256))
FULL_SMALL = dict(meta=(N_META, D), norm_gains=(DEPTH, 3, D), w2=(DEPTH, 2, GLA_RANK, 256), b2=(DEPTH, 2, 256),
                  gn=(DEPTH, 512), qn=(DEPTH, HD), kn=(DEPTH, HD), bm=(DEPTH, 2, D), fin=(D,))


def _pack(parts, table, rows):
    flat = jnp.concatenate([parts[k].reshape(-1).astype(F32) for k in table])
    return jnp.pad(flat, (0, rows * LANE - flat.shape[0])).reshape(rows, LANE)


def _unpack(packed, table):
    lead = packed.shape[:-2]
    flat = packed.reshape(lead + (-1,))
    out, off = {}, 0
    for k, shp in table.items():
        n = int(np.prod(shp))
        out[k] = flat[..., off:off + n].reshape(lead + tuple(shp))
        off += n
    return out


def _rows_for(table, mult):
    n = sum(int(np.prod(s)) for s in table.values())
    return -(-n // (LANE * mult)) * mult


SMALL_ROWS = _rows_for(SHARDED_SMALL, 16)
GRAD_ROWS = _rows_for(dict(FULL_SMALL, loss=(1,)), 8)


def _place():
    x, y, c = lax.axis_index("x"), lax.axis_index("y"), lax.axis_index("c")
    return x, y, c


def _other_chips(x, y):
    return [(1 - x, y), (x, 1 - y), (1 - x, 1 - y)]


def _cast_place(name, w3, slot):
    A, R, Cc = w3.shape
    tr = _rt(R, 512, 16)

    def body(p_ref, w_ref, o_ref):
        o_ref[...] = w_ref[...].astype(BF16)

    return pl.pallas_call(
        body, name=name,
        grid_spec=pltpu.PrefetchScalarGridSpec(
            num_scalar_prefetch=1, grid=(A, R // tr),
            in_specs=[pl.BlockSpec((None, tr, Cc), lambda a, r, p_ref: (a, r, 0))],
            out_specs=pl.BlockSpec((None, None, tr, Cc), lambda a, r, p_ref: (p_ref[0], a, r, 0))),
        out_shape=SDS((NSH, A, R, Cc), BF16))(slot, w3)


class _GatherRider:
    def __init__(self, keys, bufs, layers):
        n = len(bufs)
        self.keys, self.ins, self.layers = keys, list(bufs), layers
        self.out_shape = [SDS(b.shape, b.dtype) for b in bufs]
        self.aliases = {a: a for a in range(n)}
        self.sems = [pltpu.SemaphoreType.DMA((n, 3)) for _ in range(4)]

    def updated(self, W, carried):
        return {**W, **dict(zip(self.keys, carried))}

    def hooks(self, ins, outs, sems):
        send, recv, fsend, frecv = sems
        n = len(outs)

        def rows(a, slot, core):
            ref, l = outs[a], self.layers[a]
            half = ref.shape[-2] // 2
            return ref.at[slot, pl.ds(core * half, half)] if l is None else ref.at[slot, l, pl.ds(core * half, half)]

        def ici(a, k, part, px, py, c):
            return pltpu.make_async_remote_copy(src_ref=part, dst_ref=part, send_sem=send.at[a, k], recv_sem=recv.at[a, k],
                                                device_id=(px, py, c), device_id_type=MESH)

        def d2d(a, k, part, x, y, c):
            return pltpu.make_async_remote_copy(src_ref=part, dst_ref=part, send_sem=fsend.at[a, k], recv_sem=frecv.at[a, k],
                                                device_id=(x, y, 1 - c), device_id_type=MESH)

        def start():
            x, y, c = _place()
            for a in range(n):
                for k, (px, py) in enumerate(_other_chips(x, y)):
                    ici(a, k, rows(a, 2 * x + y, c), px, py, c).start()

        def mid():
            x, y, c = _place()
            for a in range(n):
                for k, (px, py) in enumerate(_other_chips(x, y)):
                    landed = rows(a, 2 * px + py, c)
                    ici(a, k, landed, px, py, c).wait_recv()
                    d2d(a, k, landed, x, y, c).start()

        def end():
            x, y, c = _place()
            for a in range(n):
                for k, (px, py) in enumerate(_other_chips(x, y)):
                    d2d(a, k, rows(a, 2 * px + py, 1 - c), x, y, c).wait_recv()
                    ici(a, k, rows(a, 2 * x + y, c), px, py, c).wait_send()
                    d2d(a, k, rows(a, 2 * px + py, c), x, y, c).wait_send()

        return start, mid, end


class _ChipExchangeRider:
    def __init__(self, arrs):
        n = len(arrs)
        self.ins = list(arrs)
        self.out_shape = [SDS((3,) + a.shape[1:], a.dtype) for a in arrs]
        self.aliases = {}
        self.sems = [pltpu.SemaphoreType.DMA((n, 3)), pltpu.SemaphoreType.DMA((n, 3))]

    def hooks(self, ins, outs, sems):
        send, recv = sems

        def copy(a, k, px, py, c):
            return pltpu.make_async_remote_copy(src_ref=ins[a].at[2 * px + py], dst_ref=outs[a].at[k], send_sem=send.at[a, k],
                                                recv_sem=recv.at[a, k], device_id=(px, py, c), device_id_type=MESH)

        def start():
            x, y, c = _place()
            for a in range(len(ins)):
                for k, (px, py) in enumerate(_other_chips(x, y)):
                    copy(a, k, px, py, c).start()

        def mid():
            pass

        def end():
            x, y, c = _place()
            for a in range(len(ins)):
                for k, (px, py) in enumerate(_other_chips(x, y)):
                    copy(a, k, px, py, c).wait()

        return start, mid, end


def _run_rider(name, rider):
    r_in, r_out = len(rider.ins), len(rider.out_shape)

    def body(*refs):
        start, mid, end = rider.hooks(refs[:r_in], refs[r_in:r_in + r_out], refs[r_in + r_out:])
        start()
        mid()
        end()

    any_spec = pl.BlockSpec(memory_space=pl.ANY)
    return pl.pallas_call(
        body, name=name, in_specs=[any_spec] * r_in, out_specs=[any_spec] * r_out, out_shape=list(rider.out_shape),
        scratch_shapes=list(rider.sems), input_output_aliases=dict(rider.aliases))(*rider.ins)


class _PairExchangeRider:
    def __init__(self, arrs):
        n = len(arrs)
        self.ins = list(arrs)
        self.out_shape = [SDS((NSH, a.shape[1] // 2, a.shape[2]), a.dtype) for a in arrs]
        self.aliases = {}
        self.sems = [pltpu.SemaphoreType.DMA((n,)), pltpu.SemaphoreType.DMA((n,))]

    def hooks(self, ins, outs, sems):
        send, recv = sems

        def copy(a):
            x, y, c = _place()
            half = ins[a].shape[1] // 2
            return pltpu.make_async_remote_copy(
                src_ref=ins[a].at[:, pl.ds((1 - c) * half, half)], dst_ref=outs[a], send_sem=send.at[a], recv_sem=recv.at[a],
                device_id=(x, y, 1 - c), device_id_type=MESH)

        def start():
            for a in range(len(ins)):
                copy(a).start()

        def mid():
            pass

        def end():
            for a in range(len(ins)):
                copy(a).wait()

        return start, mid, end


def _pair_add(name, g, p, core):
    _, Rh, Cc = p.shape
    tr = _rt(Rh, 512, 16)
    nr = Rh // tr

    def body(c_ref, g_ref, p_ref, o_ref, ob_ref):
        v = g_ref[...] + p_ref[...]
        o_ref[...] = v
        ob_ref[...] = v.astype(BF16)

    blk = pl.BlockSpec((None, tr, Cc), lambda s, r, c_ref: (s, r, 0))
    return pl.pallas_call(
        body, name=name,
        grid_spec=pltpu.PrefetchScalarGridSpec(
            num_scalar_prefetch=1, grid=(NSH, nr),
            in_specs=[pl.BlockSpec((None, tr, Cc), lambda s, r, c_ref: (s, c_ref[0] * nr + r, 0)), blk],
            out_specs=[blk, blk]),
        out_shape=[SDS((NSH, Rh, Cc), F32), SDS((NSH, Rh, Cc), BF16)])(core, g, p)


def _chip_add(name, hsum, q, chip, core, l, into):
    _, Rh, Cc = hsum.shape
    tr = _rt(Rh, 512, 8)
    nr = Rh // tr

    def body(*refs):
        h_ref, q_ref, o_ref = refs[2], refs[3], refs[-1]
        o_ref[...] = ((h_ref[...] + q_ref[0].astype(F32)) + q_ref[1].astype(F32)) + q_ref[2].astype(F32)

    in_specs = [pl.BlockSpec((None, tr, Cc), lambda r, p_ref, c_ref: (p_ref[0], r, 0)),
                pl.BlockSpec((3, tr, Cc), lambda r, p_ref, c_ref: (0, r, 0))]
    args = [chip, core, hsum, q]
    aliases = {}
    if into is not None:
        in_specs.append(pl.BlockSpec(memory_space=pl.ANY))
        args.append(into)
        aliases = {4: 0}
    return pl.pallas_call(
        body, name=name,
        grid_spec=pltpu.PrefetchScalarGridSpec(
            num_scalar_prefetch=2, grid=(nr,), in_specs=in_specs,
            out_specs=pl.BlockSpec((None, tr, Cc), lambda r, p_ref, c_ref: (l, c_ref[0] * nr + r, 0))),
        out_shape=SDS((DEPTH, 2 * Rh, Cc), F32), input_output_aliases=aliases)(*args)


class _PairShareRider:
    def __init__(self, bufs, lo, hi):
        n = len(bufs)
        self.ins, self.lo, self.hi = list(bufs), lo, hi
        self.out_shape = [SDS(b.shape, b.dtype) for b in bufs]
        self.aliases = {a: a for a in range(n)}
        self.sems = [pltpu.SemaphoreType.DMA((n,)), pltpu.SemaphoreType.DMA((n,))]

    def hooks(self, ins, outs, sems):
        send, recv = sems
        lo, hi = self.lo, self.hi

        def copy(a, whose):
            x, y, c = _place()
            half = outs[a].shape[1] // 2
            part = outs[a].at[lo:hi, pl.ds((c if whose == 0 else 1 - c) * half, half)]
            return pltpu.make_async_remote_copy(src_ref=part, dst_ref=part, send_sem=send.at[a], recv_sem=recv.at[a],
                                                device_id=(x, y, 1 - c), device_id_type=MESH)

        def start():
            for a in range(len(outs)):
                copy(a, 0).start()

        def mid():
            pass

        def end():
            for a in range(len(outs)):
                copy(a, 0).wait_send()
                copy(a, 1).wait_recv()

        return start, mid, end


def _allreduce_small(v):
    rows = v.shape[0]

    def body(v_ref, o_ref, buf, send, recv):
        x, y, c = _place()
        me = 4 * x + 2 * y + c
        buf[me] = v_ref[...]
        cps = []
        k = 0
        for dx in range(2):
            for dy in range(2):
                for dc in range(2):
                    if dx + dy + dc == 0:
                        continue
                    cp = pltpu.make_async_remote_copy(
                        src_ref=v_ref, dst_ref=buf.at[me], send_sem=send.at[k], recv_sem=recv.at[k],
                        device_id=(jnp.bitwise_xor(x, dx), jnp.bitwise_xor(y, dy), jnp.bitwise_xor(c, dc)), device_id_type=MESH)
                    cp.start()
                    cps.append((cp, dx, dy, dc))
                    k += 1
        for k, (cp, dx, dy, dc) in enumerate(cps):
            cp.wait_send()
            src = 4 * jnp.bitwise_xor(x, dx) + 2 * jnp.bitwise_xor(y, dy) + jnp.bitwise_xor(c, dc)
            pltpu.make_async_remote_copy(
                src_ref=v_ref, dst_ref=buf.at[src], send_sem=send.at[k], recv_sem=recv.at[k],
                device_id=(x, y, c), device_id_type=MESH).wait_recv()
        acc = buf[0]
        for d in range(1, 8):
            acc = acc + buf[d]
        o_ref[...] = acc

    vm = pl.BlockSpec(memory_space=pltpu.VMEM)
    return pl.pallas_call(
        body, name="allreduce_small", in_specs=[vm], out_specs=vm, out_shape=SDS((rows, LANE), F32),
        scratch_shapes=[pltpu.VMEM((8, rows, LANE), F32), pltpu.SemaphoreType.DMA((7,)), pltpu.SemaphoreType.DMA((7,))])(v)


def _adamw(name, w, g, m, v, lo=0, hi=None, into=None, rider=None):
    A, R, Cc = w.shape
    hi = A if hi is None else hi
    tr = _rt(R, 512, 8)

    def core(ins, outs, scr):
        w_ref, g_ref, m_ref, v_ref = ins[:4]
        d_ref, mo_ref, vo_ref = outs
        gv = g_ref[...]
        mn = ADAM_B1 * m_ref[...] + (1.0 - ADAM_B1) * gv
        vn = ADAM_B2 * v_ref[...] + (1.0 - ADAM_B2) * (gv * gv)
        m_hat = mn / (1.0 - ADAM_B1 ** ADAM_STEP)
        v_hat = vn / (1.0 - ADAM_B2 ** ADAM_STEP)
        d_ref[...] = -ADAM_LR * (m_hat / (jnp.sqrt(v_hat) + ADAM_EPS) + ADAM_WD * w_ref[...])
        mo_ref[...] = mn
        vo_ref[...] = vn

    blk = pl.BlockSpec((None, tr, Cc), lambda a, r: (lo + a, r, 0))
    in_specs, args, aliases = [blk] * 4, [w, g, m, v], {}
    if into is not None:
        in_specs = in_specs + [pl.BlockSpec(memory_space=pl.ANY)] * 3
        args = args + list(into)
        aliases = {4: 0, 5: 1, 6: 2}
    return _ridden_call(core, rider, name=name, grid=(hi - lo, R // tr), in_specs=in_specs, out_specs=[blk] * 3,
                        out_shape=[SDS(w.shape, F32)] * 3, scratch_shapes=[], args=args, aliases=aliases)


BIG = ("gate", "up", "down", "win", "wpa", "wpb", "wout")
FFN_W, MIXER_W = BIG[:3], BIG[3:]
SMALL = ("meta", "norm_gains", "w2", "b2", "gn", "qn", "kn", "bm", "fin")


def _view3(a):
    return a.reshape(a.shape[0], -1, a.shape[-1])


class _StepComm:
    def __init__(self, chip, core):
        self.pvec, self.cvec = chip.reshape(1), core.reshape(1)
        self.loc, self.sums = {}, {}
        self.red = {k: None for k in BIG}

    def first_rider(self, W):
        return _GatherRider(MIXER_W, [W[k] for k in MIXER_W], [0] * len(MIXER_W))

    def fwd_rider(self, W, l, keys):
        if l + 1 >= DEPTH:
            return None
        return _GatherRider(keys, [W[k] for k in keys], [l + 1] * len(keys))

    def layer_done(self, l, G):
        dwin = jnp.transpose(_win_from_padded(G["winp"]).reshape(D, NSH, D_IN // NSH), (1, 0, 2))
        self.loc[l] = [dwin if k == "win" else _view3(G[k]) for k in BIG]

    def ffn_rider(self, l):
        return _PairExchangeRider(self.loc[l + 1]) if l + 1 < DEPTH else None

    def ffn_carried(self, l, got):
        self._pair_add(l + 1, got)

    def _pair_add(self, l, got):
        self.sums[l] = [_pair_add(f"pair_add_{k}_l{l}", a, p, self.cvec) for k, a, p in zip(BIG, self.loc.pop(l), got)]

    def bwd_rider(self, l):
        if l + 1 >= DEPTH:
            return None
        return _ChipExchangeRider([s[1] for s in self.sums[l + 1]])

    def bwd_carried(self, l, arrived):
        self._chip_add(l + 1, arrived)

    def _chip_add(self, l, arrived):
        for k, s, q in zip(BIG, self.sums.pop(l), arrived):
            self.red[k] = _chip_add(f"chip_add_{k}_l{l}", s[0], q, self.pvec, self.cvec, l, self.red[k])

    def last_rider(self, l):
        return _PairShareRider([self.red[k] for k in BIG], 1, DEPTH) if l == 0 else None

    def last_carried(self, shared):
        self.red = dict(zip(BIG, shared))

    def finish(self, w, m, v):
        self._pair_add(0, _run_rider("pair_exchange_l0", _PairExchangeRider(self.loc[0])))
        sums = dict(zip(BIG, self.sums.pop(0)))
        upd = {}
        for k in BIG:
            upd[k], arrived = _adamw(f"adamw_{k}_l123", w[k], self.red[k], m[k], v[k], 1, DEPTH,
                                     rider=_ChipExchangeRider([sums[k][1]]))
            self.red[k] = _chip_add(f"chip_add_{k}_l0", sums[k][0], arrived[0], self.pvec, self.cvec, 0, self.red[k])
        red = dict(zip(BIG, _run_rider("pair_share_l0", _PairShareRider([self.red[k] for k in BIG], 0, 1))))
        for k in BIG:
            upd[k], _ = _adamw(f"adamw_{k}_l0", w[k], red[k], m[k], v[k], 0, 1, into=upd[k])
        return red, upd


def kernel(x, meta_tokens, norm_gains, ffn_w_gate, ffn_w_up, ffn_w_down, w_in, gla_w2, gla_b2, gla_gn, q_norm, k_norm, w_pa, w_pb, b_merge, w_out, final_norm, loss_target, m_meta_tokens, m_norm_gains, m_ffn_w_gate, m_ffn_w_up, m_ffn_w_down, m_w_in, m_gla_w2, m_gla_b2, m_gla_gn, m_q_norm, m_k_norm, m_w_pa, m_w_pb, m_b_merge, m_w_out, m_final_norm, v_meta_tokens, v_norm_gains, v_ffn_w_gate, v_ffn_w_up, v_ffn_w_down, v_w_in, v_gla_w2, v_gla_b2, v_gla_gn, v_q_norm, v_k_norm, v_w_pa, v_w_pb, v_b_merge, v_w_out, v_final_norm):
    big_w = dict(gate=ffn_w_gate, up=ffn_w_up, down=ffn_w_down, win=w_in, wpa=w_pa, wpb=w_pb, wout=w_out)
    big_m = dict(gate=m_ffn_w_gate, up=m_ffn_w_up, down=m_ffn_w_down, win=m_w_in, wpa=m_w_pa, wpb=m_w_pb, wout=m_w_out)
    big_v = dict(gate=v_ffn_w_gate, up=v_ffn_w_up, down=v_ffn_w_down, win=v_w_in, wpa=v_w_pa, wpb=v_w_pb, wout=v_w_out)
    small_w = dict(meta=meta_tokens, norm_gains=norm_gains, w2=gla_w2, b2=gla_b2, gn=gla_gn, qn=q_norm, kn=k_norm,
                   bm=b_merge, fin=final_norm)
    small_m = dict(meta=m_meta_tokens, norm_gains=m_norm_gains, w2=m_gla_w2, b2=m_gla_b2, gn=m_gla_gn, qn=m_q_norm,
                   kn=m_k_norm, bm=m_b_merge, fin=m_final_norm)
    small_v = dict(meta=v_meta_tokens, norm_gains=v_norm_gains, w2=v_gla_w2, b2=v_gla_b2, gn=v_gla_gn, qn=v_q_norm,
                   kn=v_k_norm, bm=v_b_merge, fin=v_final_norm)
    xi, yi, ci = _place()
    chip = (2 * xi + yi).astype(jnp.int32)

    comm = _StepComm(chip, ci.astype(jnp.int32))
    shard_pack = _pack({k: small_w[k] for k in SHARDED_SMALL}, SHARDED_SMALL, SMALL_ROWS)
    placed = [_cast_place(f"cast_{k}", _view3(big_w[k]), comm.pvec) for k in BIG]
    placed.append(lax.dynamic_update_slice(jnp.zeros((NSH, SMALL_ROWS, LANE), F32), shard_pack[None], (chip, 0, 0)))
    first = FFN_W + ("small",)
    w0 = dict(zip(BIG + ("small",), placed))
    w0.update(zip(first, _run_rider("gather_l0", _GatherRider(first, [w0[k] for k in first], [0] * len(FFN_W) + [None]))))
    W = _assemble(w0)
    W.update(gn=gla_gn, qn=q_norm, kn=k_norm, fin=final_norm)

    loss, dh0, _, gs = _local_step(x[0], loss_target[0], W, comm)
    grad_x = dh0[LANE:][None]
    red, upd = comm.finish(*({k: _view3(t[k]) for k in BIG} for t in (big_w, big_m, big_v)))
    grads, deltas, new_m, new_v = {}, {}, {}, {}
    for k in BIG:
        shp = big_w[k].shape
        grads[k] = red[k].reshape(shp)
        deltas[k], new_m[k], new_v[k] = (t.reshape(shp) for t in upd[k])

    gs["loss"] = loss.reshape(1)
    table = dict(FULL_SMALL, loss=(1,))
    tot = _unpack(_allreduce_small(_pack(gs, table, GRAD_ROWS)), table)
    loss_out = tot["loss"][0]
    sl = dict(meta=(1, 256), norm_gains=(2, 256), w2=(3, 64), b2=(2, 64), bm=(2, 256))
    for k in SMALL:
        gk = tot[k]
        if k in sl:
            ax, width = sl[k]
            gk = lax.dynamic_slice_in_dim(gk, chip * width, width, axis=ax)
        grads[k] = gk
    tbl = {k: small_w[k].shape for k in SMALL}
    rows = _rows_for(tbl, 8)
    packs = [_pack(src, tbl, rows)[None] for src in (small_w, grads, small_m, small_v)]
    (d, mn, vn), _ = _adamw("adamw_small", *packs)
    for dst, packed in ((deltas, d), (new_m, mn), (new_v, vn)):
        dst.update(_unpack(packed[0], tbl))

    order = ("meta", "norm_gains", "gate", "up", "down", "win", "w2", "b2", "gn", "qn", "kn", "wpa", "wpb", "bm", "wout", "fin")
    return (loss_out, grad_x, *[grads[k] for k in order], *[deltas[k] for k in order], *[new_m[k] for k in order],
            *[new_v[k] for k in order])
```

```python
import functools

import numpy as np
import jax
import jax.numpy as jnp
from jax import lax
from jax.experimental import pallas as pl
from jax.experimental.pallas import tpu as pltpu

F32, BF16 = jnp.float32, jnp.bfloat16
SDS = jax.ShapeDtypeStruct
HIGHEST = lax.Precision.HIGHEST
MESH = pl.DeviceIdType.MESH

D = 1024
DEPTH = 4
N_META = 16
GRID_W = 64
GLA_H, GLA_DK, GLA_DV, GLA_RANK, GLA_TAU, CHUNK = 4, 64, 128, 16, 16.0, 64
ATT_H, ATT_KV, HD = 8, 2, 64
D_FF = 2816
EPS = 1e-6
ROPE_THETA = 10000.0
ADAM_LR, ADAM_B1, ADAM_B2, ADAM_EPS, ADAM_WD, ADAM_STEP = 0.001, 0.9, 0.999, 1e-08, 0.01, 10

NSH = 4
FF_S = D_FF // NSH
FF_P = 768
LANE = 128
PAD = LANE - N_META
D_IN = 4384
C_GA, C_GB, C_QA, C_KA, C_VA, C_RA, C_QB, C_KB, C_VB, C_LR, HP = 0, 1024, 2048, 2304, 2560, 3072, 3584, 4096, 4224, 4352, 4480
VMEM_BIG = 56 * 2 ** 20
ATT_SUB_FWD, ATT_SUB_BWD = 128, 192
RIDER_MID_BACK = 5


def _rt(n, cap, mult=LANE):
    best = None
    t = mult
    while t <= min(n, cap):
        if n % t == 0:
            best = t
        t += mult
    assert best is not None, (n, cap, mult)
    return best


def _cp(big=False):
    return pltpu.CompilerParams(vmem_limit_bytes=VMEM_BIG) if big else None


def _row_ids(i, tm):
    return i * tm + lax.broadcasted_iota(jnp.int32, (tm, 1), 0)


def _mm(name, a, b, *, grid, a_spec, b_spec, o_spec, o_shape, o_dtype, dims, acc_shape, nk=1, scale=None,
        res=None, res_spec=None, a_sl=None, b_sl=None, pad_w=None, into=None, shards=None):
    has_res, has_into = res is not None, into is not None

    def body(*refs):
        a_ref, b_ref = refs[0], refs[1]
        p = 2
        res_ref = None
        if has_res:
            res_ref = refs[p]
            p += 1
        if has_into:
            p += 1
        o_ref = refs[p]
        acc_ref = refs[p + 1] if nk > 1 else None
        av = (a_ref[a_sl] if a_sl is not None else a_ref[...]).astype(BF16)
        bv = (b_ref[b_sl] if b_sl is not None else b_ref[...]).astype(BF16)
        if shards == "rows":
            bv = bv.reshape(bv.shape[0] * bv.shape[1], bv.shape[2])
        if shards == "cols":
            w = bv.shape[2]
            if dims == NN:
                prod = jnp.concatenate([lax.dot_general(av, bv[s], NN, preferred_element_type=F32) for s in range(NSH)], axis=1)
            else:
                prod = sum(lax.dot_general(av[:, s * w:(s + 1) * w], bv[s], NT, preferred_element_type=F32) for s in range(NSH))
        else:
            prod = lax.dot_general(av, bv, dims, preferred_element_type=F32)

        def finish(v):
            if scale is not None:
                v = v * scale
            if has_res:
                v = v + res_ref[...]
            v = v.astype(o_dtype)
            if pad_w is None:
                o_ref[...] = v
            else:
                w = v.shape[-1]
                o_ref[:, :w] = v
                o_ref[:, w:] = jnp.zeros((v.shape[0], pad_w - w), o_dtype)

        if nk == 1:
            finish(prod)
        else:
            k = pl.program_id(len(grid) - 1)

            @pl.when(k == 0)
            def _():
                acc_ref[...] = prod

            @pl.when(k > 0)
            def _():
                acc_ref[...] += prod

            @pl.when(k == nk - 1)
            def _():
                finish(acc_ref[...])

    in_specs = [a_spec, b_spec]
    args = [a, b]
    if has_res:
        in_specs.append(res_spec)
        args.append(res)
    aliases = {}
    if has_into:
        aliases = {len(args): 0}
        in_specs.append(pl.BlockSpec(memory_space=pl.ANY))
        args.append(into)
        o_shape = into.shape
    return pl.pallas_call(
        body, name=name, grid=grid, in_specs=in_specs, out_specs=o_spec, out_shape=SDS(tuple(o_shape), o_dtype),
        scratch_shapes=[pltpu.VMEM(acc_shape, F32)] if nk > 1 else [], input_output_aliases=aliases,
        compiler_params=_cp(True))(*args)


NN = (((1,), (0,)), ((), ()))
NT = (((1,), (1,)), ((), ()))
TN = (((0,), (0,)), ((), ()))


W_CHUNK = 896


def _norm_project(name, h, g, winp, rider=None):
    Lp = h.shape[0]
    tm = _rt(Lp, 384)

    def core(ins, outs, scr):
        h_ref, g_ref, w_ref = ins
        o_ref, zt_ref = outs
        x = h_ref[...]
        y = x * lax.rsqrt(jnp.mean(x * x, axis=-1, keepdims=True) + EPS) * g_ref[...]
        zt_ref[...] = y.T.astype(BF16)
        z = y.astype(BF16)
        for n in range(HP // W_CHUNK):
            cs = slice(n * W_CHUNK, (n + 1) * W_CHUNK)
            o_ref[:, cs] = jnp.dot(z, w_ref[:, cs], preferred_element_type=F32)

    return _ridden_call(
        core, rider, name=name, grid=(Lp // tm,),
        in_specs=[pl.BlockSpec((tm, D), lambda i: (i, 0)), pl.BlockSpec((1, D), lambda i: (0, 0)),
                  pl.BlockSpec((D, HP), lambda i: (0, 0), pipeline_mode=pl.Buffered(1))],
        out_specs=[pl.BlockSpec((tm, HP), lambda i: (i, 0)), pl.BlockSpec((D, tm), lambda i: (0, i))],
        out_shape=[SDS((Lp, HP), F32), SDS((D, Lp), BF16)], scratch_shapes=[], args=(h, g, winp), mid_back=1)


def _norm_project_bwd(name, dH, winp, h, g, dh):
    Lp = h.shape[0]
    tm = _rt(Lp, 384)

    def body(dhh_ref, w_ref, h_ref, g_ref, dh_ref, o_ref, dg_ref):
        dn = lax.dot_general(dhh_ref[...], w_ref[...], NT, preferred_element_type=F32)
        x = h_ref[...]
        r = lax.rsqrt(jnp.mean(x * x, axis=-1, keepdims=True) + EPS)
        xh = x * r
        u = dn * g_ref[...]
        o_ref[...] = dh_ref[...] + r * (u - xh * jnp.mean(u * xh, axis=-1, keepdims=True))

        @pl.when(pl.program_id(0) == 0)
        def _():
            dg_ref[...] = jnp.zeros_like(dg_ref)

        dg_ref[...] += jnp.sum(dn * xh, axis=0, keepdims=True)

    row = pl.BlockSpec((tm, D), lambda i: (i, 0))
    vec = pl.BlockSpec((1, D), lambda i: (0, 0))
    return pl.pallas_call(
        body, name=name, grid=(Lp // tm,),
        in_specs=[pl.BlockSpec((tm, HP), lambda i: (i, 0)), pl.BlockSpec((D, HP), lambda i: (0, 0), pipeline_mode=pl.Buffered(1)),
                  row, vec, row],
        out_specs=[row, vec], out_shape=[SDS((Lp, D), F32), SDS((1, D), F32)], compiler_params=_cp(True))(dH, winp, h, g, dh)


def _loss_head(name, h, g, tgt):
    Lp = h.shape[0]
    tm = LANE

    def body(h_ref, g_ref, t_ref, dh_ref, dg_ref, loss_ref):
        i = pl.program_id(0)
        x = h_ref[...]
        r = lax.rsqrt(jnp.mean(x * x, axis=-1, keepdims=True) + EPS)
        xh = x * r
        gg = g_ref[...]
        err = jnp.where(i >= 1, xh * gg - t_ref[...], 0.0)
        dy = err * (1.0 / D)
        u = dy * gg
        dh_ref[...] = r * (u - xh * jnp.mean(u * xh, axis=-1, keepdims=True))

        @pl.when(i == 0)
        def _():
            dg_ref[...] = jnp.zeros_like(dg_ref)
            loss_ref[...] = jnp.zeros_like(loss_ref)

        dg_ref[...] += jnp.sum(dy * xh, axis=0, keepdims=True)
        loss_ref[...] += (0.5 / D) * jnp.sum(err * err)

    row = pl.BlockSpec((tm, D), lambda i: (i, 0))
    vec = pl.BlockSpec((1, D), lambda i: (0, 0))
    return pl.pallas_call(
        body, name=name, grid=(Lp // tm,),
        in_specs=[row, vec, pl.BlockSpec((tm, D), lambda i: (jnp.maximum(i - 1, 0), 0))],
        out_specs=[row, vec, pl.BlockSpec((8, LANE), lambda i: (0, 0))],
        out_shape=[SDS((Lp, D), F32), SDS((1, D), F32), SDS((8, LANE), F32)])(h, g, tgt)


def _merge_fwd(name, H, pa, pb, bm):
    Lp = H.shape[0]
    tm = _rt(Lp, 384)

    def body(g_ref, pa_ref, pb_ref, bm_ref, y_ref, yt_ref):
        gv = g_ref[...]
        y = (jax.nn.sigmoid(gv[:, :D] + bm_ref[0:1, :]) * pa_ref[...]
             + jax.nn.sigmoid(gv[:, D:] + bm_ref[1:2, :]) * pb_ref[...])
        y_ref[...] = y.astype(BF16)
        yt_ref[...] = y.T.astype(BF16)

    row = pl.BlockSpec((tm, D), lambda i: (i, 0))
    return pl.pallas_call(
        body, name=name, grid=(Lp // tm,),
        in_specs=[pl.BlockSpec((tm, 2 * D), lambda i: (i, 0)), row, row, pl.BlockSpec((2, D), lambda i: (0, 0))],
        out_specs=[row, pl.BlockSpec((D, tm), lambda i: (0, i))],
        out_shape=[SDS((Lp, D), BF16), SDS((D, Lp), BF16)])(H, pa, pb, bm)


def _merge_bwd(name, H, pa, pb, bm, dy):
    Lp = H.shape[0]
    tm = _rt(Lp, 384)

    def body(g_ref, pa_ref, pb_ref, bm_ref, dy_ref, dpa_ref, dpb_ref, dh_ref, dbm_ref):
        gv = g_ref[...]
        dyv = dy_ref[...]
        sa = jax.nn.sigmoid(gv[:, :D] + bm_ref[0:1, :])
        sb = jax.nn.sigmoid(gv[:, D:] + bm_ref[1:2, :])
        dpa_ref[...] = (dyv * sa).astype(BF16)
        dpb_ref[...] = (dyv * sb).astype(BF16)
        dga = dyv * pa_ref[...] * (sa * (1.0 - sa))
        dgb = dyv * pb_ref[...] * (sb * (1.0 - sb))
        dh_ref[:, :D] = dga.astype(BF16)
        dh_ref[:, D:] = dgb.astype(BF16)

        @pl.when(pl.program_id(0) == 0)
        def _():
            dbm_ref[...] = jnp.zeros_like(dbm_ref)

        dbm_ref[0:1, :] += jnp.sum(dga, axis=0, keepdims=True)
        dbm_ref[1:2, :] += jnp.sum(dgb, axis=0, keepdims=True)

    row = pl.BlockSpec((tm, D), lambda i: (i, 0))
    two = pl.BlockSpec((2, D), lambda i: (0, 0))
    gate = pl.BlockSpec((tm, 2 * D), lambda i: (i, 0))
    return pl.pallas_call(
        body, name=name, grid=(Lp // tm,), in_specs=[gate, row, row, two, row], out_specs=[row, row, gate, two],
        out_shape=[SDS((Lp, D), BF16), SDS((Lp, D), BF16), SDS((Lp, HP), BF16), SDS((2, D), F32)])(H, pa, pb, bm, dy)


def _gla_prep(name, H, w2p, b2p):
    Lp = H.shape[0]
    tm = _rt(Lp, 384)

    def body(qk_ref, v_ref, lr_ref, w_ref, b_ref, q_o, k_o, v_o, gf_o, gb_o):
        valid = _row_ids(pl.program_id(0), tm) >= PAD
        qk = qk_ref[...]
        vv = v_ref[...]
        pre = jnp.dot(lr_ref[...].astype(BF16), w_ref[...], preferred_element_type=F32) + b_ref[...]
        g = jnp.where(valid, jax.nn.log_sigmoid(pre) * (1.0 / GLA_TAU), 0.0)
        for hh in range(GLA_H):
            q_o[hh] = qk[:, 64 * hh:64 * hh + 64] * (GLA_DK ** -0.5)
            k_o[hh] = qk[:, 256 + 64 * hh:256 + 64 * hh + 64]
            v_o[hh] = vv[:, 128 * hh:128 * hh + 128].astype(BF16)
            gf_o[hh] = g[:, 64 * hh:64 * hh + 64]
            gb_o[hh] = g[:, 256 + 64 * hh:256 + 64 * hh + 64]

    h64 = pl.BlockSpec((GLA_H, tm, 64), lambda i: (0, i, 0))
    h128 = pl.BlockSpec((GLA_H, tm, 128), lambda i: (0, i, 0))
    return pl.pallas_call(
        body, name=name, grid=(Lp // tm,),
        in_specs=[pl.BlockSpec((tm, 512), lambda i: (i, C_QA // 512)), pl.BlockSpec((tm, 512), lambda i: (i, C_VA // 512)),
                  pl.BlockSpec((tm, LANE), lambda i: (i, C_LR // LANE)), pl.BlockSpec((LANE, 512), lambda i: (0, 0)),
                  pl.BlockSpec((1, 512), lambda i: (0, 0))],
        out_specs=[h64, h64, h128, h64, h64],
        out_shape=[SDS((GLA_H, Lp, 64), F32), SDS((GLA_H, Lp, 64), F32), SDS((GLA_H, Lp, 128), BF16),
                   SDS((GLA_H, Lp, 64), F32), SDS((GLA_H, Lp, 64), F32)])(H, H, H, w2p, b2p)


def _bdot(a, b, ca, cb, precision=None):
    return lax.dot_general(a, b, ((ca, cb), ((0,), (0,))), precision=precision, preferred_element_type=F32)


def _gla_chunk_terms(q_ref, k_ref, g_ref, v_ref, G, rev):
    B = GLA_H * G
    qv = q_ref[...].reshape(B, CHUNK, GLA_DK)
    kv = k_ref[...].reshape(B, CHUNK, GLA_DK)
    gv = g_ref[...].reshape(B, CHUNK, GLA_DK)
    vv = v_ref[...].reshape(B, CHUNK, GLA_DV)
    ii = lax.broadcasted_iota(jnp.int32, (CHUNK, CHUNK), 0)
    jj = lax.broadcasted_iota(jnp.int32, (CHUNK, CHUNK), 1)
    tri = (jj >= ii) if rev else (jj <= ii)
    tb = jnp.broadcast_to(tri.astype(F32)[None], (B, CHUNK, CHUNK))
    bc = _bdot(tb, gv, (2,), (1,), HIGHEST)
    bt = bc[:, 0:1, :] if rev else bc[:, CHUNK - 1:CHUNK, :]
    eq, eki, eke = jnp.exp(bc), jnp.exp(-bc), jnp.exp(bt - bc)
    qd, ki, ke = qv * eq, kv * eki, kv * eke
    att = jnp.where(tri[None], _bdot(qd.astype(BF16), ki.astype(BF16), (2,), (2,)), 0.0)
    dm = jnp.exp(_bdot(gv, jnp.ones((B, CHUNK, GLA_DV), F32), (1,), (1,), HIGHEST))
    return dict(B=B, vv=vv, tri=tri, tb=tb, bt=bt, eq=eq, eki=eki, eke=eke, qd=qd, ki=ki, ke=ke, att=att, dm=dm)


def _gla_fwd(name, q, k, g, v, rev, G):
    Lp = q.shape[1]
    tg = G * CHUNK
    ng = Lp // tg

    def body(q_ref, k_ref, g_ref, v_ref, o_ref, ss_ref, s_scr):
        @pl.when(pl.program_id(0) == 0)
        def _():
            s_scr[...] = jnp.zeros_like(s_scr)

        t = _gla_chunk_terms(q_ref, k_ref, g_ref, v_ref, G, rev)
        B, vv = t["B"], t["vv"]
        qd = t["qd"].astype(BF16)
        oi = _bdot(t["att"].astype(BF16), vv, (2,), (1,))
        kvc = _bdot(t["ke"].astype(BF16), vv, (1,), (1,)).reshape(GLA_H, G, GLA_DK, GLA_DV)
        dm = t["dm"].reshape(GLA_H, G, GLA_DK, GLA_DV)
        s = s_scr[...]
        sp = [None] * G
        for c in (range(G - 1, -1, -1) if rev else range(G)):
            sp[c] = s
            ss_ref[c] = s
            s = dm[:, c] * s + kvc[:, c]
        s_scr[...] = s
        spb = jnp.stack(sp, axis=1).reshape(B, GLA_DK, GLA_DV).astype(BF16)
        o_ref[...] = (oi + _bdot(qd, spb, (2,), (1,))).reshape(GLA_H, tg, GLA_DV)

    blk = (lambda i: (0, ng - 1 - i, 0)) if rev else (lambda i: (0, i, 0))
    sblk = (lambda i: (ng - 1 - i, 0, 0, 0)) if rev else (lambda i: (i, 0, 0, 0))
    h64 = pl.BlockSpec((GLA_H, tg, 64), blk)
    h128 = pl.BlockSpec((GLA_H, tg, 128), blk)
    return pl.pallas_call(
        body, name=name, grid=(ng,), in_specs=[h64, h64, h64, h128],
        out_specs=[h128, pl.BlockSpec((G, GLA_H, GLA_DK, GLA_DV), sblk)],
        out_shape=[SDS((GLA_H, Lp, GLA_DV), F32), SDS((Lp // CHUNK, GLA_H, GLA_DK, GLA_DV), F32)],
        scratch_shapes=[pltpu.VMEM((GLA_H, GLA_DK, GLA_DV), F32)], compiler_params=_cp(True))(q, k, g, v)


def _gla_bwd(name, q, k, g, v, ss, do, rev, G):
    Lp = q.shape[1]
    tg = G * CHUNK
    ng = Lp // tg

    def body(q_ref, k_ref, g_ref, v_ref, ss_ref, do_ref, dq_ref, dk_ref, dg_ref, dv_ref, ds_scr):
        @pl.when(pl.program_id(0) == 0)
        def _():
            ds_scr[...] = jnp.zeros_like(ds_scr)

        t = _gla_chunk_terms(q_ref, k_ref, g_ref, v_ref, G, rev)
        B, vv, tri = t["B"], t["vv"], t["tri"]
        qd, ki, ke = t["qd"], t["ki"], t["ke"]
        qdb, kib, keb = qd.astype(BF16), ki.astype(BF16), ke.astype(BF16)
        sp = jnp.stack([ss_ref[c] for c in range(G)], axis=1).reshape(B, GLA_DK, GLA_DV)
        dob = do_ref[...].reshape(B, CHUNK, GLA_DV).astype(BF16)
        da = jnp.where(tri[None], _bdot(dob, vv, (2,), (2,)), 0.0).astype(BF16)
        dqd = _bdot(da, kib, (2,), (1,)) + _bdot(dob, sp.astype(BF16), (2,), (2,))
        dki = _bdot(da, qdb, (1,), (1,))
        dv = _bdot(t["att"].astype(BF16), dob, (1,), (1,))
        cc = _bdot(qdb, dob, (1,), (1,)).reshape(GLA_H, G, GLA_DK, GLA_DV)
        dm = t["dm"].reshape(GLA_H, G, GLA_DK, GLA_DV)
        dsc = ds_scr[...]
        dsn = [None] * G
        for c in (range(G) if rev else range(G - 1, -1, -1)):
            dsn[c] = dsc
            dsc = dm[:, c] * dsc + cc[:, c]
        ds_scr[...] = dsc
        dsn = jnp.stack(dsn, axis=1).reshape(B, GLA_DK, GLA_DV)
        dsnb = dsn.astype(BF16)
        dv = dv + _bdot(keb, dsnb, (2,), (1,))
        dke = _bdot(vv, dsnb, (2,), (2,))
        ddrow = _bdot(jnp.ones((B, CHUNK, GLA_DV), F32), dsn * sp, (2,), (2,), HIGHEST)
        dbt = ddrow * jnp.exp(t["bt"]) + jnp.sum(dke * ke, axis=1, keepdims=True)
        db = dqd * qd - dki * ki - dke * ke
        dq_ref[...] = (dqd * t["eq"]).reshape(GLA_H, tg, GLA_DK)
        dk_ref[...] = (dki * t["eki"] + dke * t["eke"]).reshape(GLA_H, tg, GLA_DK)
        dg_ref[...] = (_bdot(t["tb"], db, (1,), (1,), HIGHEST) + dbt).reshape(GLA_H, tg, GLA_DK)
        dv_ref[...] = dv.reshape(GLA_H, tg, GLA_DV)

    blk = (lambda i: (0, i, 0)) if rev else (lambda i: (0, ng - 1 - i, 0))
    sblk = (lambda i: (i, 0, 0, 0)) if rev else (lambda i: (ng - 1 - i, 0, 0, 0))
    h64 = pl.BlockSpec((GLA_H, tg, 64), blk)
    h128 = pl.BlockSpec((GLA_H, tg, 128), blk)
    return pl.pallas_call(
        body, name=name, grid=(ng,),
        in_specs=[h64, h64, h64, h128, pl.BlockSpec((G, GLA_H, GLA_DK, GLA_DV), sblk), h128],
        out_specs=[h64, h64, h64, h128],
        out_shape=[SDS((GLA_H, Lp, 64), F32), SDS((GLA_H, Lp, 64), F32), SDS((GLA_H, Lp, 64), F32),
                   SDS((GLA_H, Lp, GLA_DV), F32)],
        scratch_shapes=[pltpu.VMEM((GLA_H, GLA_DK, GLA_DV), F32)], compiler_params=_cp(True))(q, k, g, v, ss, do)


def _gla_post(name, of, ob, H, gn):
    Lp = H.shape[0]
    tm = _rt(Lp, 384)

    def body(of_ref, ob_ref, r_ref, gn_ref, a_ref, at_ref):
        parts = []
        for hh in range(GLA_H):
            o = of_ref[hh] + ob_ref[hh]
            parts.append(o * lax.rsqrt(jnp.mean(o * o, axis=-1, keepdims=True) + EPS))
        rv = r_ref[...]
        a = (jnp.concatenate(parts, axis=1) * gn_ref[...]) * (rv * jax.nn.sigmoid(rv))
        a_ref[...] = a.astype(BF16)
        at_ref[...] = a.T.astype(BF16)

    h128 = pl.BlockSpec((GLA_H, tm, 128), lambda i: (0, i, 0))
    return pl.pallas_call(
        body, name=name, grid=(Lp // tm,),
        in_specs=[h128, h128, pl.BlockSpec((tm, 512), lambda i: (i, C_RA // 512)), pl.BlockSpec((1, 512), lambda i: (0, 0))],
        out_specs=[pl.BlockSpec((tm, 512), lambda i: (i, 0)), pl.BlockSpec((512, tm), lambda i: (0, i))],
        out_shape=[SDS((Lp, 512), BF16), SDS((512, Lp), BF16)])(of, ob, H, gn)


def _gla_post_bwd(name, of, ob, H, gn, da, dH):
    Lp = H.shape[0]
    tm = _rt(Lp, 384)

    def body(of_ref, ob_ref, r_ref, gn_ref, da_ref, dh_in, do_ref, dh_ref, dgn_ref):
        rv = r_ref[...]
        sg = jax.nn.sigmoid(rv)
        dav = da_ref[...]
        gnv = gn_ref[...]
        ons, rs = [], []
        for hh in range(GLA_H):
            o = of_ref[hh] + ob_ref[hh]
            r = lax.rsqrt(jnp.mean(o * o, axis=-1, keepdims=True) + EPS)
            rs.append(r)
            ons.append(o * r)
        on = jnp.concatenate(ons, axis=1)
        dw = dav * (rv * sg)
        dh_ref[...] = (dav * (on * gnv) * (sg * (1.0 + rv * (1.0 - sg)))).astype(BF16)

        @pl.when(pl.program_id(0) == 0)
        def _():
            dgn_ref[...] = jnp.zeros_like(dgn_ref)

        dgn_ref[...] += jnp.sum(dw * on, axis=0, keepdims=True)
        don = dw * gnv
        for hh in range(GLA_H):
            dd = don[:, 128 * hh:128 * hh + 128]
            do_ref[hh] = rs[hh] * (dd - ons[hh] * jnp.mean(dd * ons[hh], axis=-1, keepdims=True))

    h128 = pl.BlockSpec((GLA_H, tm, 128), lambda i: (0, i, 0))
    rblk = pl.BlockSpec((tm, 512), lambda i: (i, C_RA // 512))
    vec = pl.BlockSpec((1, 512), lambda i: (0, 0))
    return pl.pallas_call(
        body, name=name, grid=(Lp // tm,),
        in_specs=[h128, h128, rblk, vec, pl.BlockSpec((tm, 512), lambda i: (i, 0)), pl.BlockSpec(memory_space=pl.ANY)],
        out_specs=[h128, rblk, vec],
        out_shape=[SDS((GLA_H, Lp, 128), F32), SDS(dH.shape, BF16), SDS((1, 512), F32)],
        input_output_aliases={5: 1})(of, ob, H, gn, da, dH)


def _gla_qkv_bwd(name, dqf, dqb, dkf, dkb, dvf, dvb, dH):
    Lp = dqf.shape[1]
    tm = _rt(Lp, 384)

    def body(dqf_ref, dqb_ref, dkf_ref, dkb_ref, dvf_ref, dvb_ref, dh_in, dh_ref):
        valid = _row_ids(pl.program_id(0), tm) >= PAD
        for hh in range(GLA_H):
            dq = (dqf_ref[hh] + dqb_ref[hh]) * (GLA_DK ** -0.5)
            dh_ref[:, 64 * hh:64 * hh + 64] = jnp.where(valid, dq, 0.0).astype(BF16)
            dh_ref[:, 256 + 64 * hh:256 + 64 * hh + 64] = jnp.where(valid, dkf_ref[hh] + dkb_ref[hh], 0.0).astype(BF16)
            dh_ref[:, 512 + 128 * hh:512 + 128 * hh + 128] = jnp.where(valid, dvf_ref[hh] + dvb_ref[hh], 0.0).astype(BF16)

    h64 = pl.BlockSpec((GLA_H, tm, 64), lambda i: (0, i, 0))
    h128 = pl.BlockSpec((GLA_H, tm, 128), lambda i: (0, i, 0))
    return pl.pallas_call(
        body, name=name, grid=(Lp // tm,),
        in_specs=[h64, h64, h64, h64, h128, h128, pl.BlockSpec(memory_space=pl.ANY)],
        out_specs=pl.BlockSpec((tm, 1024), lambda i: (i, C_QA // 1024)), out_shape=SDS(dH.shape, BF16),
        input_output_aliases={6: 0})(dqf, dqb, dkf, dkb, dvf, dvb, dH)


def _gla_gate_bwd(name, H, w2p, b2p, dgf, dgb, dH):
    Lp = H.shape[0]
    tm = _rt(Lp, 384)

    def body(lr_ref, w_ref, b_ref, dgf_ref, dgb_ref, dh_in, dh_ref, dw_ref, db_ref, dg_scr):
        valid = _row_ids(pl.program_id(0), tm) >= PAD
        for hh in range(GLA_H):
            dg_scr[:, 64 * hh:64 * hh + 64] = dgf_ref[hh]
            dg_scr[:, 256 + 64 * hh:256 + 64 * hh + 64] = dgb_ref[hh]
        lrb = lr_ref[...].astype(BF16)
        wv = w_ref[...]
        pre = jnp.dot(lrb, wv, preferred_element_type=F32) + b_ref[...]
        dpre = jnp.where(valid, dg_scr[...] * (1.0 / GLA_TAU) * jax.nn.sigmoid(-pre), 0.0)
        dpb = dpre.astype(BF16)
        dh_ref[...] = lax.dot_general(dpb, wv, NT, preferred_element_type=F32).astype(BF16)

        @pl.when(pl.program_id(0) == 0)
        def _():
            dw_ref[...] = jnp.zeros_like(dw_ref)
            db_ref[...] = jnp.zeros_like(db_ref)

        dw_ref[...] += lax.dot_general(lrb, dpb, TN, preferred_element_type=F32)
        db_ref[...] += jnp.sum(dpre, axis=0, keepdims=True)

    h64 = pl.BlockSpec((GLA_H, tm, 64), lambda i: (0, i, 0))
    lrblk = pl.BlockSpec((tm, LANE), lambda i: (i, C_LR // LANE))
    wblk = pl.BlockSpec((LANE, 512), lambda i: (0, 0))
    vec = pl.BlockSpec((1, 512), lambda i: (0, 0))
    return pl.pallas_call(
        body, name=name, grid=(Lp // tm,),
        in_specs=[lrblk, wblk, vec, h64, h64, pl.BlockSpec(memory_space=pl.ANY)],
        out_specs=[lrblk, wblk, vec],
        out_shape=[SDS(dH.shape, BF16), SDS((LANE, 512), F32), SDS((1, 512), F32)],
        scratch_shapes=[pltpu.VMEM((tm, 512), F32)], input_output_aliases={5: 0})(H, w2p, b2p, dgf, dgb, dH)


def _swap16(x):
    n = x.shape[1]
    lane = lax.broadcasted_iota(jnp.int32, x.shape, 1)
    return jnp.where(lane % 32 < 16, pltpu.roll(x, n - 16, 1), pltpu.roll(x, 16, 1))


def _head_mean(v, bd):
    hi = v.astype(BF16)
    lo = (v - hi.astype(F32)).astype(BF16)
    bdb = bd.astype(BF16)
    return jnp.dot(hi, bdb, preferred_element_type=F32) + jnp.dot(lo, bdb, preferred_element_type=F32)


def _headnorm_rope(x, gain, cos, sin, bd):
    r = lax.rsqrt(_head_mean(x * x, bd) + EPS)
    xh = x * r
    xn = xh * gain
    return xn * cos + _swap16(xn) * sin, xh, r


def _headnorm_rope_bwd(dxr, xh, r, gain, cos, sin, bd):
    dxn = cos * dxr + _swap16(sin * dxr)
    u = dxn * gain
    dx = r * (u - xh * _head_mean(u * xh, bd))
    return dx, jnp.sum(dxn * xh, axis=0, keepdims=True)


def _att_prep(name, H, gq, gk, cos, sin, bd):
    Lp = H.shape[0]
    tm = _rt(Lp, 384)

    def body(q_ref, kv_ref, gq_ref, gk_ref, c_ref, s_ref, bd_ref, qp_ref, k_ref, v_ref):
        c1, s1 = c_ref[...], s_ref[...]
        c4, s4 = jnp.concatenate([c1] * 4, axis=1), jnp.concatenate([s1] * 4, axis=1)
        xr, _, _ = _headnorm_rope(q_ref[...], gq_ref[...], c4, s4, bd_ref[...])
        xr = xr * (HD ** -0.5)
        lane = lax.broadcasted_iota(jnp.int32, (tm, LANE), 1)
        for hh in range(ATT_H):
            grp = xr[:, LANE * (hh // 2):LANE * (hh // 2) + LANE]
            e, gi = hh % 2, hh // 4
            if e != gi:
                grp = pltpu.roll(grp, 64, 1)
            keep = (lane < 64) if gi == 0 else (lane >= 64)
            qp_ref[hh] = jnp.where(keep, grp, 0.0).astype(BF16)
        kv = kv_ref[...]
        kr, _, _ = _headnorm_rope(kv[:, :LANE], gk_ref[...], c1, s1, bd_ref[0:LANE, 0:LANE])
        k_ref[...] = kr.astype(BF16)
        v_ref[...] = kv[:, LANE:].astype(BF16)

    row128 = pl.BlockSpec((tm, LANE), lambda i: (i, 0))
    return pl.pallas_call(
        body, name=name, grid=(Lp // tm,),
        in_specs=[pl.BlockSpec((tm, 512), lambda i: (i, C_QB // 512)), pl.BlockSpec((tm, 256), lambda i: (i, C_KB // 256)),
                  pl.BlockSpec((1, 512), lambda i: (0, 0)), pl.BlockSpec((1, LANE), lambda i: (0, 0)), row128, row128,
                  pl.BlockSpec((512, 512), lambda i: (0, 0))],
        out_specs=[pl.BlockSpec((ATT_H, tm, LANE), lambda i: (0, i, 0)), row128, row128],
        out_shape=[SDS((ATT_H, Lp, LANE), BF16), SDS((Lp, LANE), BF16), SDS((Lp, LANE), BF16)])(H, H, gq, gk, cos, sin, bd)


def _att_q_bwd(name, H, gq, cos, sin, bd, dqp, dH):
    Lp = H.shape[0]
    tm = _rt(Lp, 384)

    def body(q_ref, gq_ref, c_ref, s_ref, bd_ref, dqp_ref, dh_in, dh_ref, dg_ref):
        c1, s1 = c_ref[...], s_ref[...]
        c4, s4 = jnp.concatenate([c1] * 4, axis=1), jnp.concatenate([s1] * 4, axis=1)
        gqv = gq_ref[...]
        _, xh, r = _headnorm_rope(q_ref[...], gqv, c4, s4, bd_ref[...])
        lane = lax.broadcasted_iota(jnp.int32, (tm, LANE), 1)
        groups = []
        for j in range(ATT_H // 2):
            pieces = []
            for e in range(2):
                hh = 2 * j + e
                piece = dqp_ref[hh]
                if e != hh // 4:
                    piece = pltpu.roll(piece, 64, 1)
                pieces.append(piece)
            groups.append(jnp.where(lane < 64, pieces[0], pieces[1]))
        dxr = jnp.concatenate(groups, axis=1) * (HD ** -0.5)
        dx, dg = _headnorm_rope_bwd(dxr, xh, r, gqv, c4, s4, bd_ref[...])
        dh_ref[...] = dx.astype(BF16)

        @pl.when(pl.program_id(0) == 0)
        def _():
            dg_ref[...] = jnp.zeros_like(dg_ref)

        dg_ref[...] += dg

    row128 = pl.BlockSpec((tm, LANE), lambda i: (i, 0))
    qblk = pl.BlockSpec((tm, 512), lambda i: (i, C_QB // 512))
    vec = pl.BlockSpec((1, 512), lambda i: (0, 0))
    return pl.pallas_call(
        body, name=name, grid=(Lp // tm,),
        in_specs=[qblk, vec, row128, row128, pl.BlockSpec((512, 512), lambda i: (0, 0)),
                  pl.BlockSpec((ATT_H, tm, LANE), lambda i: (0, i, 0)), pl.BlockSpec(memory_space=pl.ANY)],
        out_specs=[qblk, vec], out_shape=[SDS(dH.shape, BF16), SDS((1, 512), F32)],
        input_output_aliases={6: 0})(H, gq, cos, sin, bd, dqp, dH)


def _att_kv_bwd(name, H, gk, cos, sin, bd, dkr, dvb, dH):
    Lp = H.shape[0]
    tm = _rt(Lp, 384)

    def body(kv_ref, gk_ref, c_ref, s_ref, bd_ref, dk_ref, dv_ref, dh_in, dh_ref, dg_ref):
        c1, s1 = c_ref[...], s_ref[...]
        gkv = gk_ref[...]
        bdv = bd_ref[0:LANE, 0:LANE]
        _, xh, r = _headnorm_rope(kv_ref[:, :LANE], gkv, c1, s1, bdv)
        dx, dg = _headnorm_rope_bwd(dk_ref[...], xh, r, gkv, c1, s1, bdv)
        dh_ref[:, :LANE] = dx.astype(BF16)
        dh_ref[:, LANE:] = dv_ref[...].astype(BF16)

        @pl.when(pl.program_id(0) == 0)
        def _():
            dg_ref[...] = jnp.zeros_like(dg_ref)

        dg_ref[...] += dg

    row128 = pl.BlockSpec((tm, LANE), lambda i: (i, 0))
    kvblk = pl.BlockSpec((tm, 256), lambda i: (i, C_KB // 256))
    vec = pl.BlockSpec((1, LANE), lambda i: (0, 0))
    return pl.pallas_call(
        body, name=name, grid=(Lp // tm,),
        in_specs=[kvblk, vec, row128, row128, pl.BlockSpec((512, 512), lambda i: (0, 0)), row128, row128,
                  pl.BlockSpec(memory_space=pl.ANY)],
        out_specs=[kvblk, vec], out_shape=[SDS(dH.shape, BF16), SDS((1, LANE), F32)],
        input_output_aliases={7: 0})(H, gk, cos, sin, bd, dkr, dvb, dH)


def _ridden_call(core, rider, *, name, grid, in_specs, out_specs, out_shape, scratch_shapes, args, aliases=None,
                 mid_back=RIDER_MID_BACK):
    n_in, n_out, n_scr = len(in_specs), len(out_specs), len(scratch_shapes)
    r_in = len(rider.ins) if rider else 0
    r_out = len(rider.out_shape) if rider else 0
    total = int(np.prod(grid))
    mid_step = max(total - 1 - mid_back, 0)

    def body(*refs):
        ins, r_ins = refs[:n_in], refs[n_in:n_in + r_in]
        o0 = n_in + r_in
        outs, r_outs = refs[o0:o0 + n_out], refs[o0 + n_out:o0 + n_out + r_out]
        s0 = o0 + n_out + r_out
        scr, r_sems = refs[s0:s0 + n_scr], refs[s0 + n_scr:]
        if rider is None:
            core(ins, outs, scr)
            return
        step = pl.program_id(0)
        for ax in range(1, len(grid)):
            step = step * grid[ax] + pl.program_id(ax)
        start, mid, end = rider.hooks(r_ins, r_outs, r_sems)
        pl.when(step == 0)(start)
        core(ins, outs, scr)
        pl.when(step == mid_step)(mid)
        pl.when(step == total - 1)(end)

    any_spec = pl.BlockSpec(memory_space=pl.ANY)
    res = pl.pallas_call(
        body, name=name, grid=grid, in_specs=list(in_specs) + [any_spec] * r_in, out_specs=list(out_specs) + [any_spec] * r_out,
        out_shape=list(out_shape) + (list(rider.out_shape) if rider else []),
        scratch_shapes=list(scratch_shapes) + (list(rider.sems) if rider else []),
        input_output_aliases={**(aliases or {}), **({n_in + a: n_out + b for a, b in rider.aliases.items()} if rider else {})},
        compiler_params=_cp(True))(*args, *(rider.ins if rider else []))
    return res[:n_out], res[n_out:]


def _att_fwd(name, qp, kr, vb, bias, rider=None):
    Lp = kr.shape[0]
    tq = _rt(Lp, 384)
    sub = ATT_SUB_FWD if tq % ATT_SUB_FWD == 0 else tq

    def core(ins, outs, scr):
        q_ref, k_ref, v_ref, bias_ref = ins
        b_ref, bt_ref, lse_ref = outs
        j, i = pl.program_id(0), pl.program_id(1)
        valid = _row_ids(i, tq) >= PAD
        bb = bias_ref[...]
        chains = [(e, r) for e in range(2) for r in range(tq // sub)]
        ss = [lax.dot_general(q_ref[e, r * sub:(r + 1) * sub, :], k_ref[...], NT, preferred_element_type=F32) for e, r in chains]
        ps, ls = [], []
        for (e, r), s in zip(chains, ss):
            s = jnp.concatenate([s[:, :LANE] + bb[:, :LANE], s[:, LANE:]], axis=1)
            m = jnp.max(s, axis=-1, keepdims=True)
            p = jnp.exp(s - m)
            l = jnp.sum(p, axis=-1, keepdims=True)
            ps.append(p.astype(BF16))
            ls.append(l)
            lse_ref[e, r * sub:(r + 1) * sub, :] = m + jnp.log(l)
        os_ = [jnp.dot(p, v_ref[...], preferred_element_type=F32) * (1.0 / l) for p, l in zip(ps, ls)]
        n_sub = tq // sub
        outs = [jnp.where(valid, jnp.concatenate(os_[e * n_sub:(e + 1) * n_sub], axis=0), 0.0) for e in range(2)]
        lane = lax.broadcasted_iota(jnp.int32, (tq, LANE), 1)
        low = j < 2
        o0 = jnp.where(low, outs[0], pltpu.roll(outs[0], 64, 1))
        o1 = jnp.where(low, pltpu.roll(outs[1], 64, 1), outs[1])
        blk = jnp.where(lane < 64, o0, o1)
        b_ref[...] = blk.astype(BF16)
        bt_ref[...] = blk.T.astype(BF16)

    full = pl.BlockSpec((Lp, LANE), lambda j, i: (0, 0))
    return _ridden_call(
        core, rider, name=name, grid=(ATT_H // 2, Lp // tq),
        in_specs=[pl.BlockSpec((2, tq, LANE), lambda j, i: (j, i, 0)), full, full, pl.BlockSpec((1, Lp), lambda j, i: (0, 0))],
        out_specs=[pl.BlockSpec((tq, LANE), lambda j, i: (i, j)), pl.BlockSpec((LANE, tq), lambda j, i: (j, i)),
                   pl.BlockSpec((2, tq, 1), lambda j, i: (j, i, 0))],
        out_shape=[SDS((Lp, 512), BF16), SDS((512, Lp), BF16), SDS((ATT_H, Lp, 1), F32)],
        scratch_shapes=[], args=(qp, kr, vb, bias))


def _att_bwd(name, qp, kr, vb, bias, lse, db, b, rider=None):
    Lp = kr.shape[0]
    tq = _rt(Lp, 384)
    sub = ATT_SUB_BWD if tq % ATT_SUB_BWD == 0 else tq
    nq = Lp // tq

    def core(ins, outs, scr):
        q_ref, k_ref, v_ref, bias_ref, lse_ref, db_ref, b_ref = ins
        dq_ref, dk_ref, dv_ref = outs
        dkt_scr, dvt_scr = scr
        j, i = pl.program_id(0), pl.program_id(1)

        @pl.when((j == 0) & (i == 0))
        def _():
            dkt_scr[...] = jnp.zeros_like(dkt_scr)
            dvt_scr[...] = jnp.zeros_like(dvt_scr)

        bb = bias_ref[...]
        dbv = db_ref[...]
        rolled = pltpu.roll(dbv, 64, 1)
        lane = lax.broadcasted_iota(jnp.int32, (tq, LANE), 1)
        low = j < 2
        first = jnp.where(low, 0, 64)
        keep = (lane >= first) & (lane < first + 64)
        prod = dbv * b_ref[...].astype(F32)
        deltas = [jnp.sum(jnp.where(lane < 64, prod, 0.0), axis=-1, keepdims=True),
                  jnp.sum(jnp.where(lane >= 64, prod, 0.0), axis=-1, keepdims=True)]
        dkt, dvt = 0.0, 0.0
        for e in range(2):
            src = jnp.where(low, dbv, rolled) if e == 0 else jnp.where(low, rolled, dbv)
            dop = jnp.where(keep, src, 0.0).astype(BF16)
            pbs, dss = [], []
            for r in range(tq // sub):
                rs = slice(r * sub, (r + 1) * sub)
                s = lax.dot_general(q_ref[e, rs, :], k_ref[...], NT, preferred_element_type=F32)
                s = jnp.concatenate([s[:, :LANE] + bb[:, :LANE], s[:, LANE:]], axis=1)
                p = jnp.exp(s - lse_ref[e, rs, :])
                dp = lax.dot_general(dop[rs], v_ref[...], NT, preferred_element_type=F32)
                ds = (p * (dp - deltas[e][rs])).astype(BF16)
                dq_ref[e, rs, :] = jnp.where(keep[rs], jnp.dot(ds, k_ref[...], preferred_element_type=F32), 0.0)
                pbs.append(p.astype(BF16))
                dss.append(ds)
            dkt = dkt + lax.dot_general(q_ref[e], jnp.concatenate(dss, axis=0), TN, preferred_element_type=F32)
            dvt = dvt + lax.dot_general(dop, jnp.concatenate(pbs, axis=0), TN, preferred_element_type=F32)
        dkt_scr[...] += dkt
        dvt_scr[...] += dvt

        @pl.when((j == ATT_H // 2 - 1) & (i == nq - 1))
        def _():
            dk_ref[...] = dkt_scr[...].T
            dv_ref[...] = dvt_scr[...].T

    full = pl.BlockSpec((Lp, LANE), lambda j, i: (0, 0))
    pair = pl.BlockSpec((2, tq, LANE), lambda j, i: (j, i, 0))
    return _ridden_call(
        core, rider, name=name, grid=(ATT_H // 2, nq),
        in_specs=[pair, full, full, pl.BlockSpec((1, Lp), lambda j, i: (0, 0)), pl.BlockSpec((2, tq, 1), lambda j, i: (j, i, 0)),
                  pl.BlockSpec((tq, LANE), lambda j, i: (i, j)), pl.BlockSpec((tq, LANE), lambda j, i: (i, j))],
        out_specs=[pair, full, full],
        out_shape=[SDS((ATT_H, Lp, LANE), F32), SDS((Lp, LANE), F32), SDS((Lp, LANE), F32)],
        scratch_shapes=[pltpu.VMEM((LANE, Lp), F32), pltpu.VMEM((LANE, Lp), F32)], args=(qp, kr, vb, bias, lse, db, b))


def _rope_tables(n_tok):
    f32 = np.float32
    rows = n_tok // GRID_W
    row = np.repeat(np.arange(rows), GRID_W).astype(f32)
    col = np.tile(np.arange(GRID_W), rows).astype(f32)
    axis_dim = HD // 2
    inv = np.power(f32(ROPE_THETA), -np.arange(0, axis_dim, 2, dtype=f32) / f32(axis_dim)).astype(f32)
    ang = np.concatenate([row[:, None] * inv, col[:, None] * inv], axis=-1).astype(f32)
    ang = np.concatenate([np.zeros((LANE, axis_dim), f32), ang], axis=0)
    c, s = np.cos(ang).astype(f32), np.sin(ang).astype(f32)
    c64 = np.concatenate([c[:, :16], c[:, :16], c[:, 16:], c[:, 16:]], axis=1)
    s64 = np.concatenate([-s[:, :16], s[:, :16], -s[:, 16:], s[:, 16:]], axis=1)
    return jnp.asarray(np.concatenate([c64, c64], axis=1)), jnp.asarray(np.concatenate([s64, s64], axis=1))


def _ffn_fwd(tag, h, W, l, j, rider=None):
    Lp = h.shape[0]
    tm = _rt(Lp, 384)
    g = W["norm_gains"][l, 2 * j].reshape(1, D)

    def core(ins, outs, scr):
        h_ref, g_ref, wg_ref, wu_ref, wd_ref = ins
        h2_ref, a_ref, b_ref, st_ref, nt_ref = outs
        pad_scr, = scr
        x = h_ref[...]
        y = x * lax.rsqrt(jnp.mean(x * x, axis=-1, keepdims=True) + EPS) * g_ref[...]
        nt_ref[...] = y.T.astype(BF16)
        nv = y.astype(BF16)
        pad_scr[:, FF_S:] = jnp.zeros((tm, FF_P - FF_S), F32)
        acc = 0.0
        for s in range(NSH):
            a = jnp.dot(nv, wg_ref[s], preferred_element_type=F32)
            b = jnp.dot(nv, wu_ref[s], preferred_element_type=F32)
            sv = a * jax.nn.sigmoid(a) * b
            a_ref[s, :, :FF_S] = a.astype(BF16)
            a_ref[s, :, FF_S:] = jnp.zeros((tm, FF_P - FF_S), BF16)
            b_ref[s, :, :FF_S] = b.astype(BF16)
            b_ref[s, :, FF_S:] = jnp.zeros((tm, FF_P - FF_S), BF16)
            pad_scr[:, :FF_S] = sv
            st_ref[s] = pad_scr[...].T.astype(BF16)
            acc = acc + jnp.dot(sv.astype(BF16), wd_ref[s], preferred_element_type=F32)
        h2_ref[...] = x + 0.5 * acc

    once = dict(pipeline_mode=pl.Buffered(1))
    wup = pl.BlockSpec((NSH, None, D, FF_S), lambda i: (0, l, j, 0), **once)
    row = pl.BlockSpec((tm, D), lambda i: (i, 0))
    slab = pl.BlockSpec((NSH, tm, FF_P), lambda i: (0, i, 0))
    (h2, a, b, st, nt), carried = _ridden_call(
        core, rider, name=f"ffn_{tag}", grid=(Lp // tm,),
        in_specs=[row, pl.BlockSpec((1, D), lambda i: (0, 0)), wup, wup,
                  pl.BlockSpec((NSH, None, FF_S, D), lambda i: (0, l, j, 0), **once)],
        out_specs=[row, slab, slab, pl.BlockSpec((NSH, FF_P, tm), lambda i: (0, 0, i)), pl.BlockSpec((D, tm), lambda i: (0, i))],
        out_shape=[SDS((Lp, D), F32), SDS((NSH, Lp, FF_P), BF16), SDS((NSH, Lp, FF_P), BF16), SDS((NSH, FF_P, Lp), BF16),
                   SDS((D, Lp), BF16)],
        scratch_shapes=[pltpu.VMEM((tm, FF_P), F32)], args=(h, g, W["gate"], W["up"], W["down"]), mid_back=1)
    if rider is not None:
        W = rider.updated(W, carried)
    return h2, dict(h=h, g=g, nt=nt, a=a, b=b, st=st), W


def _ffn_bwd(tag, dh, sv, W, G, l, j, rider=None):
    Lp = dh.shape[0]
    tn = 512
    (da, db, dh_prev, dg), carried = _ffn_bwd_acts(tag, dh, sv, W, l, j, rider)
    G["down"] = _mm(f"bwd_{tag}", sv["st"], dh, grid=(NSH, D // tn),
                    a_spec=pl.BlockSpec((None, FF_P, Lp), lambda s, n: (s, 0, 0)), a_sl=(slice(0, FF_S), slice(None)),
                    b_spec=pl.BlockSpec((Lp, tn), lambda s, n: (0, n)),
                    o_spec=pl.BlockSpec((None, None, FF_S, tn), lambda s, n: (s, j, 0, n)), o_shape=(NSH, 2, FF_S, D), o_dtype=F32,
                    dims=NN, acc_shape=None, scale=0.5, into=G.get("down"))
    for key, dact in (("gate", da), ("up", db)):
        G[key] = _mm(f"bw{key[0]}_{tag}", sv["nt"], dact, grid=(NSH, D // tn),
                     a_spec=pl.BlockSpec((tn, Lp), lambda s, m: (m, 0)),
                     b_spec=pl.BlockSpec((None, Lp, FF_P), lambda s, m: (s, 0, 0)), b_sl=(slice(None), slice(0, FF_S)),
                     o_spec=pl.BlockSpec((None, None, tn, FF_S), lambda s, m: (s, j, m, 0)), o_shape=(NSH, 2, D, FF_S),
                     o_dtype=F32, dims=NN, acc_shape=None, into=G.get(key))
    return dh_prev, dg, carried


def _ffn_bwd_acts(tag, dh, sv, W, l, j, rider=None):
    Lp = dh.shape[0]
    tm = _rt(Lp, 384)

    def core(ins, outs, scr):
        dh_ref, h_ref, g_ref, wd_ref, wg_ref, wu_ref, a_ref, b_ref = ins
        da_ref, db_ref, dho_ref, dg_ref = outs
        dhv = dh_ref[...]
        dhb = dhv.astype(BF16)
        dss = [0.5 * lax.dot_general(dhb, wd_ref[s], NT, preferred_element_type=F32) for s in range(NSH)]
        das, dbs = [], []
        for s in range(NSH):
            av = a_ref[s, :, :FF_S].astype(F32)
            bv = b_ref[s, :, :FF_S].astype(F32)
            sg = jax.nn.sigmoid(av)
            da = (dss[s] * bv * (sg * (1.0 + av * (1.0 - sg)))).astype(BF16)
            db = (dss[s] * (av * sg)).astype(BF16)
            da_ref[s, :, :FF_S] = da
            da_ref[s, :, FF_S:] = jnp.zeros((tm, FF_P - FF_S), BF16)
            db_ref[s, :, :FF_S] = db
            db_ref[s, :, FF_S:] = jnp.zeros((tm, FF_P - FF_S), BF16)
            das.append(da)
            dbs.append(db)
        dn = 0.0
        for s in range(NSH):
            dn = dn + (lax.dot_general(das[s], wg_ref[s], NT, preferred_element_type=F32)
                       + lax.dot_general(dbs[s], wu_ref[s], NT, preferred_element_type=F32))
        x = h_ref[...]
        r = lax.rsqrt(jnp.mean(x * x, axis=-1, keepdims=True) + EPS)
        xh = x * r
        u = dn * g_ref[...]
        dho_ref[...] = dhv + r * (u - xh * jnp.mean(u * xh, axis=-1, keepdims=True))

        @pl.when(pl.program_id(0) == 0)
        def _():
            dg_ref[...] = jnp.zeros_like(dg_ref)

        dg_ref[...] += jnp.sum(dn * xh, axis=0, keepdims=True)

    once = dict(pipeline_mode=pl.Buffered(1))
    wup = pl.BlockSpec((NSH, None, D, FF_S), lambda i: (0, l, j, 0), **once)
    row = pl.BlockSpec((tm, D), lambda i: (i, 0))
    slab = pl.BlockSpec((NSH, tm, FF_P), lambda i: (0, i, 0))
    vec = pl.BlockSpec((1, D), lambda i: (0, 0))
    return _ridden_call(
        core, rider, name=f"bffn_{tag}", grid=(Lp // tm,),
        in_specs=[row, row, vec, pl.BlockSpec((NSH, None, FF_S, D), lambda i: (0, l, j, 0), **once), wup, wup, slab, slab],
        out_specs=[slab, slab, row, vec],
        out_shape=[SDS((NSH, Lp, FF_P), BF16), SDS((NSH, Lp, FF_P), BF16), SDS((Lp, D), F32), SDS((1, D), F32)],
        scratch_shapes=[], args=(dh, sv["h"], sv["g"], W["down"], W["gate"], W["up"], sv["a"], sv["b"]))


def _mixer_fwd(tag, h, W, winp, C, l, G_chunks, riders=(None, None)):
    Lp = h.shape[0]
    tm = _rt(Lp, 1408)
    g = W["norm_gains"][l, 1].reshape(1, D)
    (H, zt), carried = _norm_project(f"win_{tag}", h, g, winp, riders[0])
    if riders[0] is not None:
        W = riders[0].updated(W, carried)
    rider = riders[1]
    w2p, b2p = C["w2p"][l], C["b2p"][l]
    qh, kh, vh, gf, gb = _gla_prep(f"glap_{tag}", H, w2p, b2p)
    of, sf = _gla_fwd(f"glaf_{tag}", qh, kh, gf, vh, False, G_chunks)
    ob, sb = _gla_fwd(f"glar_{tag}", qh, kh, gb, vh, True, G_chunks)
    gn = W["gn"][l].reshape(1, 512)
    a, at = _gla_post(f"glao_{tag}", of, ob, H, gn)
    gq, gk = C["gq"][l], C["gk"][l]
    qp, kr, vb = _att_prep(f"attp_{tag}", H, gq, gk, C["cos"], C["sin"], C["bd"])
    (b, bt, lse), carried = _att_fwd(f"attf_{tag}", qp, kr, vb, C["bias"], rider)
    if rider is not None:
        W = rider.updated(W, carried)
    row = pl.BlockSpec((tm, D), lambda i: (i, 0))
    proj = functools.partial(
        _mm, grid=(Lp // tm,), a_spec=pl.BlockSpec((tm, 512), lambda i: (i, 0)),
        b_spec=pl.BlockSpec((NSH, None, 512, 256), lambda i: (0, l, 0, 0)), o_spec=row, o_shape=(Lp, D), o_dtype=F32, dims=NN,
        acc_shape=None, shards="cols")
    pa = proj(f"pa_{tag}", a, W["wpa"])
    pb = proj(f"pb_{tag}", b, W["wpb"])
    bm = W["bm"][l]
    y, yt = _merge_fwd(f"mrg_{tag}", H, pa, pb, bm)
    h2 = _mm(f"wout_{tag}", y, W["wout"], grid=(Lp // tm,), a_spec=row,
             b_spec=pl.BlockSpec((NSH, None, 256, D), lambda i: (0, l, 0, 0)), o_spec=row, o_shape=(Lp, D), o_dtype=F32,
             dims=NN, acc_shape=None, res=h, res_spec=row, shards="rows")
    sv = dict(h=h, g=g, zt=zt, H=H, w2p=w2p, b2p=b2p, qh=qh, kh=kh, vh=vh, gf=gf, gb=gb, of=of, ob=ob, sf=sf, sb=sb, gn=gn,
              at=at, gq=gq, gk=gk, qp=qp, kr=kr, vb=vb, b=b, bt=bt, lse=lse, pa=pa, pb=pb, bm=bm, yt=yt, winp=winp)
    return h2, sv, W


def _mixer_bwd(tag, dh, sv, W, C, G, S, l, G_chunks, rider=None):
    Lp = dh.shape[0]
    tm = _rt(Lp, 1408)
    H = sv["H"]
    row = pl.BlockSpec((tm, D), lambda i: (i, 0))
    dy = _mm(f"bdy_{tag}", dh, W["wout"], grid=(Lp // tm,), a_spec=row,
             b_spec=pl.BlockSpec((NSH, None, 256, D), lambda i: (0, l, 0, 0)), o_spec=row, o_shape=(Lp, D), o_dtype=F32, dims=NT,
             acc_shape=None, shards="rows")
    tn = 512
    G["wout"] = _mm(f"bwo_{tag}", sv["yt"], dh, grid=(NSH, D // tn), a_spec=pl.BlockSpec((256, Lp), lambda s, n: (s, 0)),
                    b_spec=pl.BlockSpec((Lp, tn), lambda s, n: (0, n)),
                    o_spec=pl.BlockSpec((None, 256, tn), lambda s, n: (s, 0, n)), o_shape=(NSH, 256, D), o_dtype=F32, dims=NN,
                    acc_shape=None)
    dpa, dpb, dH, S["bm"][l] = _merge_bwd(f"bmrg_{tag}", H, sv["pa"], sv["pb"], sv["bm"], dy)
    dbranch = {}
    for key, dp, xt in (("wpa", dpa, sv["at"]), ("wpb", dpb, sv["bt"])):
        dbranch[key] = _mm(f"bx{key[2]}_{tag}", dp, W[key], grid=(Lp // tm,), a_spec=row,
                           b_spec=pl.BlockSpec((NSH, None, 512, 256), lambda i: (0, l, 0, 0)),
                           o_spec=pl.BlockSpec((tm, 512), lambda i: (i, 0)), o_shape=(Lp, 512), o_dtype=F32, dims=NT,
                           acc_shape=None, shards="cols")
        G[key] = _mm(f"bw{key[2]}_{tag}", xt, dp, grid=(NSH,), a_spec=pl.BlockSpec((512, Lp), lambda s: (0, 0)),
                     b_spec=pl.BlockSpec((Lp, 256), lambda s: (0, s)),
                     o_spec=pl.BlockSpec((None, 512, 256), lambda s: (s, 0, 0)), o_shape=(NSH, 512, 256), o_dtype=F32, dims=NN,
                     acc_shape=None)
    (dqp, dkr, dvb), carried = _att_bwd(f"attb_{tag}", sv["qp"], sv["kr"], sv["vb"], C["bias"], sv["lse"], dbranch["wpb"],
                                        sv["b"], rider)
    dH, S["gq"][l] = _att_q_bwd(f"attq_{tag}", H, sv["gq"], C["cos"], C["sin"], C["bd"], dqp, dH)
    dH, S["gk"][l] = _att_kv_bwd(f"attk_{tag}", H, sv["gk"], C["cos"], C["sin"], C["bd"], dkr, dvb, dH)
    do, dH, S["gn"][l] = _gla_post_bwd(f"bglo_{tag}", sv["of"], sv["ob"], H, sv["gn"], dbranch["wpa"], dH)
    dqf, dkf, dgf, dvf = _gla_bwd(f"bglf_{tag}", sv["qh"], sv["kh"], sv["gf"], sv["vh"], sv["sf"], do, False, G_chunks)
    dqb, dkb, dgb, dvr = _gla_bwd(f"bglr_{tag}", sv["qh"], sv["kh"], sv["gb"], sv["vh"], sv["sb"], do, True, G_chunks)
    dH = _gla_qkv_bwd(f"bglq_{tag}", dqf, dqb, dkf, dkb, dvf, dvr, dH)
    dH, S["w2p"][l], S["b2p"][l] = _gla_gate_bwd(f"bglg_{tag}", H, sv["w2p"], sv["b2p"], dgf, dgb, dH)
    tmm = 256
    G["winp"] = _mm(f"bwi_{tag}", sv["zt"], dH, grid=(HP // W_CHUNK, D // tmm), a_spec=pl.BlockSpec((tmm, Lp), lambda n, m: (m, 0)),
                    b_spec=pl.BlockSpec((Lp, W_CHUNK), lambda n, m: (0, n)), o_spec=pl.BlockSpec((tmm, W_CHUNK), lambda n, m: (m, n)),
                    o_shape=(D, HP), o_dtype=F32, dims=NN, acc_shape=None)
    dh_prev, dg = _norm_project_bwd(f"bdz_{tag}", dH, sv["winp"], sv["h"], sv["g"], dh)
    return dh_prev, dg, carried


def _winp_layer(win_g, l):
    return _win_to_padded(jnp.transpose(win_g[:, l], (1, 0, 2)).reshape(D, D_IN))


def _local_step(x2, tgt2, W, comm=None):
    n_tok = x2.shape[0]
    Lp = n_tok + LANE
    nc = Lp // CHUNK
    g_chunks = max(g for g in (1, 2, 3, 6) if nc % g == 0)
    cos, sin = _rope_tables(n_tok)
    bd = jnp.asarray(np.kron(np.eye(ATT_H, dtype=np.float32), np.full((HD, HD), 1.0 / HD, np.float32)))
    bias = jnp.where(jnp.arange(Lp) >= PAD, 0.0, -1e30).astype(F32).reshape(1, Lp)
    w2, b2 = W["w2"], W["b2"]
    w2p = jnp.zeros((DEPTH, LANE, 512), F32)
    w2p = w2p.at[:, 0:GLA_RANK, 0:256].set(w2[:, 0]).at[:, GLA_RANK:2 * GLA_RANK, 256:512].set(w2[:, 1]).astype(BF16)
    C = dict(cos=cos, sin=sin, bd=bd, bias=bias, w2p=w2p, b2p=b2.reshape(DEPTH, 1, 512),
             gq=jnp.tile(W["qn"], (1, ATT_H)).reshape(DEPTH, 1, 512), gk=jnp.tile(W["kn"], (1, ATT_KV)).reshape(DEPTH, 1, LANE))
    h = jnp.concatenate([jnp.zeros((PAD, D), F32), W["meta"], x2], axis=0)
    saved = []
    for l in range(DEPTH):
        h, s0, W = _ffn_fwd(f"l{l}a", h, W, l, 0, comm.first_rider(W) if comm and l == 0 else None)
        riders = (comm.fwd_rider(W, l, ("down",)), comm.fwd_rider(W, l, ("gate", "up"))) if comm else (None, None)
        h, sm, W = _mixer_fwd(f"l{l}m", h, W, _winp_layer(W["win"], l), C, l, g_chunks, riders)
        h, s1, W = _ffn_fwd(f"l{l}b", h, W, l, 1, comm.fwd_rider(W, l, MIXER_W) if comm else None)
        saved.append((s0, sm, s1))
    dh, dfin, loss = _loss_head("loss_head", h, W["fin"].reshape(1, D), tgt2)
    S = dict(bm=[None] * DEPTH, gq=[None] * DEPTH, gk=[None] * DEPTH, gn=[None] * DEPTH, w2p=[None] * DEPTH,
             b2p=[None] * DEPTH, ng=[[None] * 3 for _ in range(DEPTH)])
    Gs = [None] * DEPTH
    for l in reversed(range(DEPTH)):
        s0, sm, s1 = saved[l]
        G = {}
        rider = comm.ffn_rider(l) if comm else None
        dh, S["ng"][l][2], carried = _ffn_bwd(f"l{l}b", dh, s1, W, G, l, 1, rider)
        if rider is not None:
            comm.ffn_carried(l, carried)
        rider = comm.bwd_rider(l) if comm else None
        dh, S["ng"][l][1], carried = _mixer_bwd(f"l{l}m", dh, sm, W, C, G, S, l, g_chunks, rider)
        if rider is not None:
            comm.bwd_carried(l, carried)
        rider = comm.last_rider(l) if comm else None
        dh, S["ng"][l][0], carried = _ffn_bwd(f"l{l}a", dh, s0, W, G, l, 0, rider)
        if rider is not None:
            comm.last_carried(carried)
        Gs[l] = G
        if comm:
            comm.layer_done(l, G)
    small = dict(
        meta=dh[PAD:LANE],
        norm_gains=jnp.stack([jnp.concatenate(S["ng"][l], axis=0) for l in range(DEPTH)]),
        w2=jnp.stack([jnp.stack([S["w2p"][l][0:GLA_RANK, 0:256], S["w2p"][l][GLA_RANK:2 * GLA_RANK, 256:512]]) for l in range(DEPTH)]),
        b2=jnp.stack([S["b2p"][l].reshape(2, 256) for l in range(DEPTH)]),
        gn=jnp.concatenate(S["gn"], axis=0),
        qn=jnp.stack([S["gq"][l].reshape(ATT_H, HD).sum(0) for l in range(DEPTH)]),
        kn=jnp.stack([S["gk"][l].reshape(ATT_KV, HD).sum(0) for l in range(DEPTH)]),
        bm=jnp.stack(S["bm"]),
        fin=dfin.reshape(D),
    )
    return loss[0, 0], dh, Gs, small


def _win_to_padded(w):
    pad = jnp.zeros(w.shape[:-1] + (HP - D_IN,), w.dtype)
    return jnp.concatenate([w[..., 2336:4384], w[..., 0:1536], w[..., 1568:2336], w[..., 1536:1568], pad], axis=-1)


def _win_from_padded(w):
    return jnp.concatenate([w[..., 2048:3584], w[..., 4352:4384], w[..., 3584:4352], w[..., 0:2048]], axis=-1)


def _assemble(g):
    W = dict(gate=g["gate"], up=g["up"], down=g["down"], win=g["win"], wpa=g["wpa"], wpb=g["wpb"], wout=g["wout"])
    sm = g["small"]
    parts = _unpack(sm, SHARDED_SMALL)
    W["meta"] = jnp.transpose(parts["meta"], (1, 0, 2)).reshape(N_META, D)
    W["norm_gains"] = jnp.transpose(parts["norm_gains"], (1, 2, 0, 3)).reshape(DEPTH, 3, D)
    W["w2"] = jnp.transpose(parts["w2"], (1, 2, 3, 0, 4)).reshape(DEPTH, 2, GLA_RANK, 256)
    W["b2"] = jnp.transpose(parts["b2"], (1, 2, 0, 3)).reshape(DEPTH, 2, 256)
    W["bm"] = jnp.transpose(parts["bm"], (1, 2, 0, 3)).reshape(DEPTH, 2, D)
    return W


SHARDED_SMALL = dict(meta=(N_META, 256), norm_gains=(DEPTH, 3, 256), w2=(DEPTH, 2, GLA_RANK, 64), b2=(DEPTH, 2, 64),
                     bm=(DEPTH, 2, 256))
FULL_SMALL = dict(meta=(N_META, D), norm_gains=(DEPTH, 3, D), w2=(DEPTH, 2, GLA_RANK, 256), b2=(DEPTH, 2, 256),
                  gn=(DEPTH, 512), qn=(DEPTH, HD), kn=(DEPTH, HD), bm=(DEPTH, 2, D), fin=(D,))


def _pack(parts, table, rows):
    flat = jnp.concatenate([parts[k].reshape(-1).astype(F32) for k in table])
    return jnp.pad(flat, (0, rows * LANE - flat.shape[0])).reshape(rows, LANE)


def _unpack(packed, table):
    lead = packed.shape[:-2]
    flat = packed.reshape(lead + (-1,))
    out, off = {}, 0
    for k, shp in table.items():
        n = int(np.prod(shp))
        out[k] = flat[..., off:off + n].reshape(lead + tuple(shp))
        off += n
    return out


def _rows_for(table, mult):
    n = sum(int(np.prod(s)) for s in table.values())
    return -(-n // (LANE * mult)) * mult


SMALL_ROWS = _rows_for(SHARDED_SMALL, 16)
GRAD_ROWS = _rows_for(dict(FULL_SMALL, loss=(1,)), 8)


def _place():
    x, y, c = lax.axis_index("x"), lax.axis_index("y"), lax.axis_index("c")
    return x, y, c


def _other_chips(x, y):
    return [(1 - x, y), (x, 1 - y), (1 - x, 1 - y)]


def _cast_place(name, w3, slot):
    A, R, Cc = w3.shape
    tr = _rt(R, 512, 16)

    def body(p_ref, w_ref, o_ref):
        o_ref[...] = w_ref[...].astype(BF16)

    return pl.pallas_call(
        body, name=name,
        grid_spec=pltpu.PrefetchScalarGridSpec(
            num_scalar_prefetch=1, grid=(A, R // tr),
            in_specs=[pl.BlockSpec((None, tr, Cc), lambda a, r, p_ref: (a, r, 0))],
            out_specs=pl.BlockSpec((None, None, tr, Cc), lambda a, r, p_ref: (p_ref[0], a, r, 0))),
        out_shape=SDS((NSH, A, R, Cc), BF16))(slot, w3)


class _GatherRider:
    def __init__(self, keys, bufs, layers):
        n = len(bufs)
        self.keys, self.ins, self.layers = keys, list(bufs), layers
        self.out_shape = [SDS(b.shape, b.dtype) for b in bufs]
        self.aliases = {a: a for a in range(n)}
        self.sems = [pltpu.SemaphoreType.DMA((n, 3)) for _ in range(4)]

    def updated(self, W, carried):
        return {**W, **dict(zip(self.keys, carried))}

    def hooks(self, ins, outs, sems):
        send, recv, fsend, frecv = sems
        n = len(outs)

        def rows(a, slot, core):
            ref, l = outs[a], self.layers[a]
            half = ref.shape[-2] // 2
            return ref.at[slot, pl.ds(core * half, half)] if l is None else ref.at[slot, l, pl.ds(core * half, half)]

        def ici(a, k, part, px, py, c):
            return pltpu.make_async_remote_copy(src_ref=part, dst_ref=part, send_sem=send.at[a, k], recv_sem=recv.at[a, k],
                                                device_id=(px, py, c), device_id_type=MESH)

        def d2d(a, k, part, x, y, c):
            return pltpu.make_async_remote_copy(src_ref=part, dst_ref=part, send_sem=fsend.at[a, k], recv_sem=frecv.at[a, k],
                                                device_id=(x, y, 1 - c), device_id_type=MESH)

        def start():
            x, y, c = _place()
            for a in range(n):
                for k, (px, py) in enumerate(_other_chips(x, y)):
                    ici(a, k, rows(a, 2 * x + y, c), px, py, c).start()

        def mid():
            x, y, c = _place()
            for a in range(n):
                for k, (px, py) in enumerate(_other_chips(x, y)):
                    landed = rows(a, 2 * px + py, c)
                    ici(a, k, landed, px, py, c).wait_recv()
                    d2d(a, k, landed, x, y, c).start()

        def end():
            x, y, c = _place()
            for a in range(n):
                for k, (px, py) in enumerate(_other_chips(x, y)):
                    d2d(a, k, rows(a, 2 * px + py, 1 - c), x, y, c).wait_recv()
                    ici(a, k, rows(a, 2 * x + y, c), px, py, c).wait_send()
                    d2d(a, k, rows(a, 2 * px + py, c), x, y, c).wait_send()

        return start, mid, end


class _ChipExchangeRider:
    def __init__(self, arrs):
        n = len(arrs)
        self.ins = list(arrs)
        self.out_shape = [SDS((3,) + a.shape[1:], a.dtype) for a in arrs]
        self.aliases = {}
        self.sems = [pltpu.SemaphoreType.DMA((n, 3)), pltpu.SemaphoreType.DMA((n, 3))]

    def hooks(self, ins, outs, sems):
        send, recv = sems

        def copy(a, k, px, py, c):
            return pltpu.make_async_remote_copy(src_ref=ins[a].at[2 * px + py], dst_ref=outs[a].at[k], send_sem=send.at[a, k],
                                                recv_sem=recv.at[a, k], device_id=(px, py, c), device_id_type=MESH)

        def start():
            x, y, c = _place()
            for a in range(len(ins)):
                for k, (px, py) in enumerate(_other_chips(x, y)):
                    copy(a, k, px, py, c).start()

        def mid():
            pass

        def end():
            x, y, c = _place()
            for a in range(len(ins)):
                for k, (px, py) in enumerate(_other_chips(x, y)):
                    copy(a, k, px, py, c).wait()

        return start, mid, end


def _run_rider(name, rider):
    r_in, r_out = len(rider.ins), len(rider.out_shape)

    def body(*refs):
        start, mid, end = rider.hooks(refs[:r_in], refs[r_in:r_in + r_out], refs[r_in + r_out:])
        start()
        mid()
        end()

    any_spec = pl.BlockSpec(memory_space=pl.ANY)
    return pl.pallas_call(
        body, name=name, in_specs=[any_spec] * r_in, out_specs=[any_spec] * r_out, out_shape=list(rider.out_shape),
        scratch_shapes=list(rider.sems), input_output_aliases=dict(rider.aliases))(*rider.ins)


class _PairExchangeRider:
    def __init__(self, arrs):
        n = len(arrs)
        self.ins = list(arrs)
        self.out_shape = [SDS((NSH, a.shape[1] // 2, a.shape[2]), a.dtype) for a in arrs]
        self.aliases = {}
        self.sems = [pltpu.SemaphoreType.DMA((n,)), pltpu.SemaphoreType.DMA((n,))]

    def hooks(self, ins, outs, sems):
        send, recv = sems

        def copy(a):
            x, y, c = _place()
            half = ins[a].shape[1] // 2
            return pltpu.make_async_remote_copy(
                src_ref=ins[a].at[:, pl.ds((1 - c) * half, half)], dst_ref=outs[a], send_sem=send.at[a], recv_sem=recv.at[a],
                device_id=(x, y, 1 - c), device_id_type=MESH)

        def start():
            for a in range(len(ins)):
                copy(a).start()

        def mid():
            pass

        def end():
            for a in range(len(ins)):
                copy(a).wait()

        return start, mid, end


def _pair_add(name, g, p, core):
    _, Rh, Cc = p.shape
    tr = _rt(Rh, 512, 16)
    nr = Rh // tr

    def body(c_ref, g_ref, p_ref, o_ref, ob_ref):
        v = g_ref[...] + p_ref[...]
        o_ref[...] = v
        ob_ref[...] = v.astype(BF16)

    blk = pl.BlockSpec((None, tr, Cc), lambda s, r, c_ref: (s, r, 0))
    return pl.pallas_call(
        body, name=name,
        grid_spec=pltpu.PrefetchScalarGridSpec(
            num_scalar_prefetch=1, grid=(NSH, nr),
            in_specs=[pl.BlockSpec((None, tr, Cc), lambda s, r, c_ref: (s, c_ref[0] * nr + r, 0)), blk],
            out_specs=[blk, blk]),
        out_shape=[SDS((NSH, Rh, Cc), F32), SDS((NSH, Rh, Cc), BF16)])(core, g, p)


def _chip_add(name, hsum, q, chip, core, l, into):
    _, Rh, Cc = hsum.shape
    tr = _rt(Rh, 512, 8)
    nr = Rh // tr

    def body(*refs):
        h_ref, q_ref, o_ref = refs[2], refs[3], refs[-1]
        o_ref[...] = ((h_ref[...] + q_ref[0].astype(F32)) + q_ref[1].astype(F32)) + q_ref[2].astype(F32)

    in_specs = [pl.BlockSpec((None, tr, Cc), lambda r, p_ref, c_ref: (p_ref[0], r, 0)),
                pl.BlockSpec((3, tr, Cc), lambda r, p_ref, c_ref: (0, r, 0))]
    args = [chip, core, hsum, q]
    aliases = {}
    if into is not None:
        in_specs.append(pl.BlockSpec(memory_space=pl.ANY))
        args.append(into)
        aliases = {4: 0}
    return pl.pallas_call(
        body, name=name,
        grid_spec=pltpu.PrefetchScalarGridSpec(
            num_scalar_prefetch=2, grid=(nr,), in_specs=in_specs,
            out_specs=pl.BlockSpec((None, tr, Cc), lambda r, p_ref, c_ref: (l, c_ref[0] * nr + r, 0))),
        out_shape=SDS((DEPTH, 2 * Rh, Cc), F32), input_output_aliases=aliases)(*args)


class _PairShareRider:
    def __init__(self, bufs, lo, hi):
        n = len(bufs)
        self.ins, self.lo, self.hi = list(bufs), lo, hi
        self.out_shape = [SDS(b.shape, b.dtype) for b in bufs]
        self.aliases = {a: a for a in range(n)}
        self.sems = [pltpu.SemaphoreType.DMA((n,)), pltpu.SemaphoreType.DMA((n,))]

    def hooks(self, ins, outs, sems):
        send, recv = sems
        lo, hi = self.lo, self.hi

        def copy(a, whose):
            x, y, c = _place()
            half = outs[a].shape[1] // 2
            part = outs[a].at[lo:hi, pl.ds((c if whose == 0 else 1 - c) * half, half)]
            return pltpu.make_async_remote_copy(src_ref=part, dst_ref=part, send_sem=send.at[a], recv_sem=recv.at[a],
                                                device_id=(x, y, 1 - c), device_id_type=MESH)

        def start():
            for a in range(len(outs)):
                copy(a, 0).start()

        def mid():
            pass

        def end():
            for a in range(len(outs)):
                copy(a, 0).wait_send()
                copy(a, 1).wait_recv()

        return start, mid, end


def _allreduce_small(v):
    rows = v.shape[0]

    def body(v_ref, o_ref, buf, send, recv):
        x, y, c = _place()
        me = 4 * x + 2 * y + c
        buf[me] = v_ref[...]
        cps = []
        k = 0
        for dx in range(2):
            for dy in range(2):
                for dc in range(2):
                    if dx + dy + dc == 0:
                        continue
                    cp = pltpu.make_async_remote_copy(
                        src_ref=v_ref, dst_ref=buf.at[me], send_sem=send.at[k], recv_sem=recv.at[k],
                        device_id=(jnp.bitwise_xor(x, dx), jnp.bitwise_xor(y, dy), jnp.bitwise_xor(c, dc)), device_id_type=MESH)
                    cp.start()
                    cps.append((cp, dx, dy, dc))
                    k += 1
        for k, (cp, dx, dy, dc) in enumerate(cps):
            cp.wait_send()
            src = 4 * jnp.bitwise_xor(x, dx) + 2 * jnp.bitwise_xor(y, dy) + jnp.bitwise_xor(c, dc)
            pltpu.make_async_remote_copy(
                src_ref=v_ref, dst_ref=buf.at[src], send_sem=send.at[k], recv_sem=recv.at[k],
                device_id=(x, y, c), device_id_type=MESH).wait_recv()
        acc = buf[0]
        for d in range(1, 8):
            acc = acc + buf[d]
        o_ref[...] = acc

    vm = pl.BlockSpec(memory_space=pltpu.VMEM)
    return pl.pallas_call(
        body, name="allreduce_small", in_specs=[vm], out_specs=vm, out_shape=SDS((rows, LANE), F32),
        scratch_shapes=[pltpu.VMEM((8, rows, LANE), F32), pltpu.SemaphoreType.DMA((7,)), pltpu.SemaphoreType.DMA((7,))])(v)


def _adamw(name, w, g, m, v, lo=0, hi=None, into=None, rider=None):
    A, R, Cc = w.shape
    hi = A if hi is None else hi
    tr = _rt(R, 512, 8)

    def core(ins, outs, scr):
        w_ref, g_ref, m_ref, v_ref = ins[:4]
        d_ref, mo_ref, vo_ref = outs
        gv = g_ref[...]
        mn = ADAM_B1 * m_ref[...] + (1.0 - ADAM_B1) * gv
        vn = ADAM_B2 * v_ref[...] + (1.0 - ADAM_B2) * (gv * gv)
        m_hat = mn / (1.0 - ADAM_B1 ** ADAM_STEP)
        v_hat = vn / (1.0 - ADAM_B2 ** ADAM_STEP)
        d_ref[...] = -ADAM_LR * (m_hat / (jnp.sqrt(v_hat) + ADAM_EPS) + ADAM_WD * w_ref[...])
        mo_ref[...] = mn
        vo_ref[...] = vn

    blk = pl.BlockSpec((None, tr, Cc), lambda a, r: (lo + a, r, 0))
    in_specs, args, aliases = [blk] * 4, [w, g, m, v], {}
    if into is not None:
        in_specs = in_specs + [pl.BlockSpec(memory_space=pl.ANY)] * 3
        args = args + list(into)
        aliases = {4: 0, 5: 1, 6: 2}
    return _ridden_call(core, rider, name=name, grid=(hi - lo, R // tr), in_specs=in_specs, out_specs=[blk] * 3,
                        out_shape=[SDS(w.shape, F32)] * 3, scratch_shapes=[], args=args, aliases=aliases)


BIG = ("gate", "up", "down", "win", "wpa", "wpb", "wout")
FFN_W, MIXER_W = BIG[:3], BIG[3:]
SMALL = ("meta", "norm_gains", "w2", "b2", "gn", "qn", "kn", "bm", "fin")


def _view3(a):
    return a.reshape(a.shape[0], -1, a.shape[-1])


class _StepComm:
    def __init__(self, chip, core):
        self.pvec, self.cvec = chip.reshape(1), core.reshape(1)
        self.loc, self.sums = {}, {}
        self.red = {k: None for k in BIG}

    def first_rider(self, W):
        return _GatherRider(MIXER_W, [W[k] for k in MIXER_W], [0] * len(MIXER_W))

    def fwd_rider(self, W, l, keys):
        if l + 1 >= DEPTH:
            return None
        return _GatherRider(keys, [W[k] for k in keys], [l + 1] * len(keys))

    def layer_done(self, l, G):
        dwin = jnp.transpose(_win_from_padded(G["winp"]).reshape(D, NSH, D_IN // NSH), (1, 0, 2))
        self.loc[l] = [dwin if k == "win" else _view3(G[k]) for k in BIG]

    def ffn_rider(self, l):
        return _PairExchangeRider(self.loc[l + 1]) if l + 1 < DEPTH else None

    def ffn_carried(self, l, got):
        self._pair_add(l + 1, got)

    def _pair_add(self, l, got):
        self.sums[l] = [_pair_add(f"pair_add_{k}_l{l}", a, p, self.cvec) for k, a, p in zip(BIG, self.loc.pop(l), got)]

    def bwd_rider(self, l):
        if l + 1 >= DEPTH:
            return None
        return _ChipExchangeRider([s[1] for s in self.sums[l + 1]])

    def bwd_carried(self, l, arrived):
        self._chip_add(l + 1, arrived)

    def _chip_add(self, l, arrived):
        for k, s, q in zip(BIG, self.sums.pop(l), arrived):
            self.red[k] = _chip_add(f"chip_add_{k}_l{l}", s[0], q, self.pvec, self.cvec, l, self.red[k])

    def last_rider(self, l):
        return _PairShareRider([self.red[k] for k in BIG], 1, DEPTH) if l == 0 else None

    def last_carried(self, shared):
        self.red = dict(zip(BIG, shared))

    def finish(self, w, m, v):
        self._pair_add(0, _run_rider("pair_exchange_l0", _PairExchangeRider(self.loc[0])))
        sums = dict(zip(BIG, self.sums.pop(0)))
        upd = {}
        for k in BIG:
            upd[k], arrived = _adamw(f"adamw_{k}_l123", w[k], self.red[k], m[k], v[k], 1, DEPTH,
                                     rider=_ChipExchangeRider([sums[k][1]]))
            self.red[k] = _chip_add(f"chip_add_{k}_l0", sums[k][0], arrived[0], self.pvec, self.cvec, 0, self.red[k])
        red = dict(zip(BIG, _run_rider("pair_share_l0", _PairShareRider([self.red[k] for k in BIG], 0, 1))))
        for k in BIG:
            upd[k], _ = _adamw(f"adamw_{k}_l0", w[k], red[k], m[k], v[k], 0, 1, into=upd[k])
        return red, upd


def kernel(x, meta_tokens, norm_gains, ffn_w_gate, ffn_w_up, ffn_w_down, w_in, gla_w2, gla_b2, gla_gn, q_norm, k_norm, w_pa, w_pb, b_merge, w_out, final_norm, loss_target, m_meta_tokens, m_norm_gains, m_ffn_w_gate, m_ffn_w_up, m_ffn_w_down, m_w_in, m_gla_w2, m_gla_b2, m_gla_gn, m_q_norm, m_k_norm, m_w_pa, m_w_pb, m_b_merge, m_w_out, m_final_norm, v_meta_tokens, v_norm_gains, v_ffn_w_gate, v_ffn_w_up, v_ffn_w_down, v_w_in, v_gla_w2, v_gla_b2, v_gla_gn, v_q_norm, v_k_norm, v_w_pa, v_w_pb, v_b_merge, v_w_out, v_final_norm):
    big_w = dict(gate=ffn_w_gate, up=ffn_w_up, down=ffn_w_down, win=w_in, wpa=w_pa, wpb=w_pb, wout=w_out)
    big_m = dict(gate=m_ffn_w_gate, up=m_ffn_w_up, down=m_ffn_w_down, win=m_w_in, wpa=m_w_pa, wpb=m_w_pb, wout=m_w_out)
    big_v = dict(gate=v_ffn_w_gate, up=v_ffn_w_up, down=v_ffn_w_down, win=v_w_in, wpa=v_w_pa, wpb=v_w_pb, wout=v_w_out)
    small_w = dict(meta=meta_tokens, norm_gains=norm_gains, w2=gla_w2, b2=gla_b2, gn=gla_gn, qn=q_norm, kn=k_norm,
                   bm=b_merge, fin=final_norm)
    small_m = dict(meta=m_meta_tokens, norm_gains=m_norm_gains, w2=m_gla_w2, b2=m_gla_b2, gn=m_gla_gn, qn=m_q_norm,
                   kn=m_k_norm, bm=m_b_merge, fin=m_final_norm)
    small_v = dict(meta=v_meta_tokens, norm_gains=v_norm_gains, w2=v_gla_w2, b2=v_gla_b2, gn=v_gla_gn, qn=v_q_norm,
                   kn=v_k_norm, bm=v_b_merge, fin=v_final_norm)
    xi, yi, ci = _place()
    chip = (2 * xi + yi).astype(jnp.int32)

    comm = _StepComm(chip, ci.astype(jnp.int32))
    shard_pack = _pack({k: small_w[k] for k in SHARDED_SMALL}, SHARDED_SMALL, SMALL_ROWS)
    placed = [_cast_place(f"cast_{k}", _view3(big_w[k]), comm.pvec) for k in BIG]
    placed.append(lax.dynamic_update_slice(jnp.zeros((NSH, SMALL_ROWS, LANE), F32), shard_pack[None], (chip, 0, 0)))
    first = FFN_W + ("small",)
    w0 = dict(zip(BIG + ("small",), placed))
    w0.update(zip(first, _run_rider("gather_l0", _GatherRider(first, [w0[k] for k in first], [0] * len(FFN_W) + [None]))))
    W = _assemble(w0)
    W.update(gn=gla_gn, qn=q_norm, kn=k_norm, fin=final_norm)

    loss, dh0, _, gs = _local_step(x[0], loss_target[0], W, comm)
    grad_x = dh0[LANE:][None]
    red, upd = comm.finish(*({k: _view3(t[k]) for k in BIG} for t in (big_w, big_m, big_v)))
    grads, deltas, new_m, new_v = {}, {}, {}, {}
    for k in BIG:
        shp = big_w[k].shape
        grads[k] = red[k].reshape(shp)
        deltas[k], new_m[k], new_v[k] = (t.reshape(shp) for t in upd[k])

    gs["loss"] = loss.reshape(1)
    table = dict(FULL_SMALL, loss=(1,))
    tot = _unpack(_allreduce_small(_pack(gs, table, GRAD_ROWS)), table)
    loss_out = tot["loss"][0]
    sl = dict(meta=(1, 256), norm_gains=(2, 256), w2=(3, 64), b2=(2, 64), bm=(2, 256))
    for k in SMALL:
        gk = tot[k]
        if k in sl:
            ax, width = sl[k]
            gk = lax.dynamic_slice_in_dim(gk, chip * width, width, axis=ax)
        grads[k] = gk
    tbl = {k: small_w[k].shape for k in SMALL}
    rows = _rows_for(tbl, 8)
    packs = [_pack(src, tbl, rows)[None] for src in (small_w, grads, small_m, small_v)]
    (d, mn, vn), _ = _adamw("adamw_small", *packs)
    for dst, packed in ((deltas, d), (new_m, mn), (new_v, vn)):
        dst.update(_unpack(packed[0], tbl))

    order = ("meta", "norm_gains", "gate", "up", "down", "win", "w2", "b2", "gn", "qn", "kn", "wpa", "wpb", "bm", "wout", "fin")
    return (loss_out, grad_x, *[grads[k] for k in order], *[deltas[k] for k in order], *[new_m[k] for k in order],
            *[new_v[k] for k in order])
```
